```python
import math
import jax, jax.numpy as jnp
from jax import lax
import numpy as np

D_MODEL = 1024
BATCH = 16
SEQ = 2048
DEPTH = 4

D_MIX = 2 * D_MODEL
HEAD_DIM = 64
A_WIDTH = D_MIX // 4
A_HEADS = A_WIDTH // HEAD_DIM
B_WIDTH = D_MIX // 4
B_HEADS = B_WIDTH // HEAD_DIM
C_WIDTH = D_MIX // 2
C_HEADS = C_WIDTH // HEAD_DIM
CONV_A_WIDTH = 31
GMLP_CHUNK = 128
SSM_STATE = 128
SSM_GROUPS = 2
SSM_CONV = 4
SSD_CHUNK = 128
D_CONV_C = C_WIDTH + 2 * SSM_GROUPS * SSM_STATE
D_IN_PROJ = 2 * A_WIDTH + 2 * B_WIDTH + C_WIDTH + D_CONV_C + C_HEADS
D_FF = 4 * D_MODEL
EPS = 1e-5

kernel_name = "hybrid_conv_gmlp_ssd_trunk"


def rmsnorm(x, g):
    xf = x.astype(jnp.float32)
    y = xf * lax.rsqrt(jnp.mean(xf * xf, axis=-1, keepdims=True) + EPS)
    return (y * g.astype(jnp.float32)).astype(x.dtype)


def head_layernorm(x, g, b, n_heads):
    lead = x.shape[:-1]
    xf = x.astype(jnp.float32).reshape(*lead, n_heads, -1)
    mu = jnp.mean(xf, axis=-1, keepdims=True)
    xc = xf - mu
    var = jnp.mean(xc * xc, axis=-1, keepdims=True)
    y = (xc * lax.rsqrt(var + EPS)).reshape(*lead, -1)
    return (y * g.astype(jnp.float32) + b.astype(jnp.float32)).astype(x.dtype)


def causal_dwconv(x, w, b):
    k, c = w.shape
    y = lax.conv_general_dilated(
        x, w[:, None, :].astype(x.dtype), window_strides=(1,), padding=[(k - 1, 0)],
        dimension_numbers=("NWC", "WIO", "NWC"), feature_group_count=c)
    return y + b.astype(x.dtype)


def conformer_mixer(a_val, a_gate, conv_w, conv_b, ln_g, ln_b):
    h = a_val * jax.nn.sigmoid(a_gate)
    h = causal_dwconv(h, conv_w, conv_b)
    h = head_layernorm(h, ln_g, ln_b, A_HEADS)
    return jax.nn.silu(h)


def gmlp_mixer(u, v, ln_g, ln_b, w_s, b_s):
    u = jax.nn.gelu(u, approximate=False)
    v = jax.nn.gelu(v, approximate=False)
    v = head_layernorm(v, ln_g, ln_b, B_HEADS)
    bsz, s, _ = v.shape
    nc = s // GMLP_CHUNK
    v = v.reshape(bsz, nc, GMLP_CHUNK, B_HEADS, HEAD_DIM)
    mask = jnp.tril(jnp.ones((GMLP_CHUNK, GMLP_CHUNK), dtype=bool))
    w = jnp.where(mask, w_s, jnp.zeros_like(w_s))
    mix = jnp.einsum("hts,bcshp->bcthp", w, v) + b_s.T[:, :, None]
    return (u.reshape(v.shape) * mix).reshape(bsz, s, B_WIDTH)


def ssd_chunked(x, dt, A, B, C):
    bsz, s, h, p = x.shape
    g, n = B.shape[2], B.shape[3]
    hg = h // g
    nc = s // SSD_CHUNK
    l = SSD_CHUNK
    xc = (x * dt[..., None]).reshape(bsz, nc, l, g, hg, p)
    a_cs = jnp.cumsum((dt * A).reshape(bsz, nc, l, g, hg), axis=2)
    Bc = B.reshape(bsz, nc, l, g, n)
    Cc = C.reshape(bsz, nc, l, g, n)
    a_t = jnp.moveaxis(a_cs, 2, -1)
    seg = a_t[..., :, None] - a_t[..., None, :]
    mask = jnp.tril(jnp.ones((l, l), dtype=bool))
    L = jnp.exp(jnp.where(mask, seg, -jnp.inf))
    CB = jnp.einsum("bclgn,bcsgn->bcgls", Cc, Bc)
    y_diag = jnp.einsum("bcghls,bcsghp->bclghp", CB[:, :, :, None] * L, xc)
    decay_states = jnp.exp(a_cs[:, :, -1:] - a_cs)
    states = jnp.einsum("bclgn,bclghp->bcghpn", Bc, xc * decay_states[..., None])
    chunk_decay = jnp.exp(a_cs[:, :, -1])

    def step(carry, inp):
        st, dec = inp
        return carry * dec[..., None, None] + st, carry

    init = jnp.zeros((bsz, g, hg, p, n), x.dtype)
    _, prev = lax.scan(step, init, (jnp.moveaxis(states, 1, 0), jnp.moveaxis(chunk_decay, 1, 0)))
    prev = jnp.moveaxis(prev, 0, 1)
    y_off = jnp.einsum("bclgn,bcghpn->bclghp", Cc, prev) * jnp.exp(a_cs)[..., None]
    return (y_diag + y_off).reshape(bsz, s, h, p)


def mamba2_mixer(z, xbc, dt_raw, conv_w, conv_b, dt_bias, a_log, d_skip, norm_g):
    xbc = jax.nn.silu(causal_dwconv(xbc, conv_w, conv_b))
    xs, Bm, Cm = jnp.split(xbc, [C_WIDTH, C_WIDTH + SSM_GROUPS * SSM_STATE], axis=-1)
    bsz, s, _ = xs.shape
    xs = xs.astype(jnp.float32).reshape(bsz, s, C_HEADS, HEAD_DIM)
    Bm = Bm.astype(jnp.float32).reshape(bsz, s, SSM_GROUPS, SSM_STATE)
    Cm = Cm.astype(jnp.float32).reshape(bsz, s, SSM_GROUPS, SSM_STATE)
    dt = jax.nn.softplus(dt_raw.astype(jnp.float32) + dt_bias.astype(jnp.float32))
    A = -jnp.exp(a_log.astype(jnp.float32))
    y = ssd_chunked(xs, dt, A, Bm, Cm) + d_skip.astype(jnp.float32)[:, None] * xs
    y = y.reshape(bsz, s, C_WIDTH) * jax.nn.silu(z.astype(jnp.float32))
    yg = y.reshape(bsz, s, SSM_GROUPS, -1)
    yg = yg * lax.rsqrt(jnp.mean(yg * yg, axis=-1, keepdims=True) + EPS)
    y = yg.reshape(bsz, s, C_WIDTH) * norm_g.astype(jnp.float32)
    return y.astype(z.dtype)


def _fwd_setup_inputs(seed: int = 0) -> dict:
    key = jax.random.key(seed)
    ks = jax.random.split(key, 24)
    f32 = jnp.float32

    def nrm(k, shape, scale):
        return jax.random.normal(k, shape, f32) * scale

    dt0 = jnp.exp(jax.random.uniform(ks[13], (DEPTH, C_HEADS), f32) * (math.log(0.1) - math.log(0.001)) + math.log(0.001))
    return {
        "x": nrm(ks[0], (BATCH, SEQ, D_MODEL), 1.0),
        "norm1_g": 1.0 + nrm(ks[1], (DEPTH, D_MODEL), 0.02),
        "w_in": nrm(ks[2], (DEPTH, D_MODEL, D_IN_PROJ), D_MODEL ** -0.5),
        "conv_a_w": nrm(ks[3], (DEPTH, CONV_A_WIDTH, A_WIDTH), CONV_A_WIDTH ** -0.5),
        "conv_a_b": nrm(ks[4], (DEPTH, A_WIDTH), 0.02),
        "ln_a_g": 1.0 + nrm(ks[5], (DEPTH, A_WIDTH), 0.02),
        "ln_a_b": nrm(ks[6], (DEPTH, A_WIDTH), 0.02),
        "ln_b_g": 1.0 + nrm(ks[7], (DEPTH, B_WIDTH), 0.02),
        "ln_b_b": nrm(ks[8], (DEPTH, B_WIDTH), 0.02),
        "w_spatial": nrm(ks[9], (DEPTH, B_HEADS, GMLP_CHUNK, GMLP_CHUNK), GMLP_CHUNK ** -0.5),
        "b_spatial": 1.0 + nrm(ks[10], (DEPTH, B_HEADS, GMLP_CHUNK), 0.1),
        "conv_c_w": nrm(ks[11], (DEPTH, SSM_CONV, D_CONV_C), SSM_CONV ** -0.5),
        "conv_c_b": nrm(ks[12], (DEPTH, D_CONV_C), 0.02),
        "dt_bias": dt0 + jnp.log(-jnp.expm1(-dt0)),
        "a_log": jnp.log(jax.random.uniform(ks[14], (DEPTH, C_HEADS), f32, 1.0, 16.0)),
        "d_skip": 1.0 + nrm(ks[15], (DEPTH, C_HEADS), 0.1),
        "norm_c_g": 1.0 + nrm(ks[16], (DEPTH, C_WIDTH), 0.02),
        "w_out": nrm(ks[17], (DEPTH, D_MIX, D_MODEL), D_MIX ** -0.5),
        "norm2_g": 1.0 + nrm(ks[18], (DEPTH, D_MODEL), 0.02),
        "w_ff1": nrm(ks[19], (DEPTH, D_MODEL, D_FF), D_MODEL ** -0.5),
        "w_ff2": nrm(ks[20], (DEPTH, D_FF, D_MODEL), D_FF ** -0.5),
        "final_g": 1.0 + nrm(ks[21], (D_MODEL,), 0.02),
    }


def _fwd_reference(x, norm1_g, w_in, conv_a_w, conv_a_b, ln_a_g, ln_a_b, ln_b_g, ln_b_b,
              w_spatial, b_spatial, conv_c_w, conv_c_b, dt_bias, a_log, d_skip, norm_c_g,
              w_out, norm2_g, w_ff1, w_ff2, final_g):
    split_idx = [A_WIDTH, 2 * A_WIDTH, 2 * A_WIDTH + B_WIDTH, 2 * A_WIDTH + 2 * B_WIDTH,
                 2 * A_WIDTH + 2 * B_WIDTH + C_WIDTH, 2 * A_WIDTH + 2 * B_WIDTH + C_WIDTH + D_CONV_C]
    for i in range(DEPTH):
        h = rmsnorm(x, norm1_g[i])
        proj = h @ w_in[i]
        a_val, a_gate, b_u, b_v, z, xbc, dt_raw = jnp.split(proj, split_idx, axis=-1)
        ya = conformer_mixer(a_val, a_gate, conv_a_w[i], conv_a_b[i], ln_a_g[i], ln_a_b[i])
        yb = gmlp_mixer(b_u, b_v, ln_b_g[i], ln_b_b[i], w_spatial[i], b_spatial[i])
        yc = mamba2_mixer(z, xbc, dt_raw, conv_c_w[i], conv_c_b[i], dt_bias[i], a_log[i],
                          d_skip[i], norm_c_g[i])
        x = x + jnp.concatenate([ya, yb, yc], axis=-1) @ w_out[i]
        h = rmsnorm(x, norm2_g[i])
        x = x + jnp.square(jax.nn.relu(h @ w_ff1[i])) @ w_ff2[i]
    return rmsnorm(x, final_g)


import jax as _jax
import jax.numpy as _jnp

TWIN_FORMAT = 'train_step'
FWD_PARAMS = ['x', 'norm1_g', 'w_in', 'conv_a_w', 'conv_a_b', 'ln_a_g', 'ln_a_b', 'ln_b_g', 'ln_b_b', 'w_spatial', 'b_spatial', 'conv_c_w', 'conv_c_b', 'dt_bias', 'a_log', 'd_skip', 'norm_c_g', 'w_out', 'norm2_g', 'w_ff1', 'w_ff2', 'final_g']
TWIN_WEIGHTS = ['norm1_g', 'w_in', 'conv_a_w', 'conv_a_b', 'ln_a_g', 'ln_a_b', 'ln_b_g', 'ln_b_b', 'w_spatial', 'b_spatial', 'conv_c_w', 'conv_c_b', 'dt_bias', 'a_log', 'd_skip', 'norm_c_g', 'w_out', 'norm2_g', 'w_ff1', 'w_ff2', 'final_g']
TWIN_DIFF_INPUT = 'x'
TWIN_INPUTS = ['x', 'norm1_g', 'w_in', 'conv_a_w', 'conv_a_b', 'ln_a_g', 'ln_a_b', 'ln_b_g', 'ln_b_b', 'w_spatial', 'b_spatial', 'conv_c_w', 'conv_c_b', 'dt_bias', 'a_log', 'd_skip', 'norm_c_g', 'w_out', 'norm2_g', 'w_ff1', 'w_ff2', 'final_g', 'loss_target', 'm_norm1_g', 'm_w_in', 'm_conv_a_w', 'm_conv_a_b', 'm_ln_a_g', 'm_ln_a_b', 'm_ln_b_g', 'm_ln_b_b', 'm_w_spatial', 'm_b_spatial', 'm_conv_c_w', 'm_conv_c_b', 'm_dt_bias', 'm_a_log', 'm_d_skip', 'm_norm_c_g', 'm_w_out', 'm_norm2_g', 'm_w_ff1', 'm_w_ff2', 'm_final_g', 'v_norm1_g', 'v_w_in', 'v_conv_a_w', 'v_conv_a_b', 'v_ln_a_g', 'v_ln_a_b', 'v_ln_b_g', 'v_ln_b_b', 'v_w_spatial', 'v_b_spatial', 'v_conv_c_w', 'v_conv_c_b', 'v_dt_bias', 'v_a_log', 'v_d_skip', 'v_norm_c_g', 'v_w_out', 'v_norm2_g', 'v_w_ff1', 'v_w_ff2', 'v_final_g']
TWIN_OUTPUTS = ['loss', 'grad_x', 'grad_norm1_g', 'grad_w_in', 'grad_conv_a_w', 'grad_conv_a_b', 'grad_ln_a_g', 'grad_ln_a_b', 'grad_ln_b_g', 'grad_ln_b_b', 'grad_w_spatial', 'grad_b_spatial', 'grad_conv_c_w', 'grad_conv_c_b', 'grad_dt_bias', 'grad_a_log', 'grad_d_skip', 'grad_norm_c_g', 'grad_w_out', 'grad_norm2_g', 'grad_w_ff1', 'grad_w_ff2', 'grad_final_g', 'delta_norm1_g', 'delta_w_in', 'delta_conv_a_w', 'delta_conv_a_b', 'delta_ln_a_g', 'delta_ln_a_b', 'delta_ln_b_g', 'delta_ln_b_b', 'delta_w_spatial', 'delta_b_spatial', 'delta_conv_c_w', 'delta_conv_c_b', 'delta_dt_bias', 'delta_a_log', 'delta_d_skip', 'delta_norm_c_g', 'delta_w_out', 'delta_norm2_g', 'delta_w_ff1', 'delta_w_ff2', 'delta_final_g', 'new_m_norm1_g', 'new_m_w_in', 'new_m_conv_a_w', 'new_m_conv_a_b', 'new_m_ln_a_g', 'new_m_ln_a_b', 'new_m_ln_b_g', 'new_m_ln_b_b', 'new_m_w_spatial', 'new_m_b_spatial', 'new_m_conv_c_w', 'new_m_conv_c_b', 'new_m_dt_bias', 'new_m_a_log', 'new_m_d_skip', 'new_m_norm_c_g', 'new_m_w_out', 'new_m_norm2_g', 'new_m_w_ff1', 'new_m_w_ff2', 'new_m_final_g', 'new_v_norm1_g', 'new_v_w_in', 'new_v_conv_a_w', 'new_v_conv_a_b', 'new_v_ln_a_g', 'new_v_ln_a_b', 'new_v_ln_b_g', 'new_v_ln_b_b', 'new_v_w_spatial', 'new_v_b_spatial', 'new_v_conv_c_w', 'new_v_conv_c_b', 'new_v_dt_bias', 'new_v_a_log', 'new_v_d_skip', 'new_v_norm_c_g', 'new_v_w_out', 'new_v_norm2_g', 'new_v_w_ff1', 'new_v_w_ff2', 'new_v_final_g']
TWIN_LEAF_KINDS = {'loss': 'loss', 'grad_x': 'grad_x', 'grad_norm1_g': 'grad_w', 'grad_w_in': 'grad_w', 'grad_conv_a_w': 'grad_w', 'grad_conv_a_b': 'grad_w', 'grad_ln_a_g': 'grad_w', 'grad_ln_a_b': 'grad_w', 'grad_ln_b_g': 'grad_w', 'grad_ln_b_b': 'grad_w', 'grad_w_spatial': 'grad_w', 'grad_b_spatial': 'grad_w', 'grad_conv_c_w': 'grad_w', 'grad_conv_c_b': 'grad_w', 'grad_dt_bias': 'grad_w', 'grad_a_log': 'grad_w', 'grad_d_skip': 'grad_w', 'grad_norm_c_g': 'grad_w', 'grad_w_out': 'grad_w', 'grad_norm2_g': 'grad_w', 'grad_w_ff1': 'grad_w', 'grad_w_ff2': 'grad_w', 'grad_final_g': 'grad_w', 'delta_norm1_g': 'delta_w', 'delta_w_in': 'delta_w', 'delta_conv_a_w': 'delta_w', 'delta_conv_a_b': 'delta_w', 'delta_ln_a_g': 'delta_w', 'delta_ln_a_b': 'delta_w', 'delta_ln_b_g': 'delta_w', 'delta_ln_b_b': 'delta_w', 'delta_w_spatial': 'delta_w', 'delta_b_spatial': 'delta_w', 'delta_conv_c_w': 'delta_w', 'delta_conv_c_b': 'delta_w', 'delta_dt_bias': 'delta_w', 'delta_a_log': 'delta_w', 'delta_d_skip': 'delta_w', 'delta_norm_c_g': 'delta_w', 'delta_w_out': 'delta_w', 'delta_norm2_g': 'delta_w', 'delta_w_ff1': 'delta_w', 'delta_w_ff2': 'delta_w', 'delta_final_g': 'delta_w', 'new_m_norm1_g': 'new_m', 'new_m_w_in': 'new_m', 'new_m_conv_a_w': 'new_m', 'new_m_conv_a_b': 'new_m', 'new_m_ln_a_g': 'new_m', 'new_m_ln_a_b': 'new_m', 'new_m_ln_b_g': 'new_m', 'new_m_ln_b_b': 'new_m', 'new_m_w_spatial': 'new_m', 'new_m_b_spatial': 'new_m', 'new_m_conv_c_w': 'new_m', 'new_m_conv_c_b': 'new_m', 'new_m_dt_bias': 'new_m', 'new_m_a_log': 'new_m', 'new_m_d_skip': 'new_m', 'new_m_norm_c_g': 'new_m', 'new_m_w_out': 'new_m', 'new_m_norm2_g': 'new_m', 'new_m_w_ff1': 'new_m', 'new_m_w_ff2': 'new_m', 'new_m_final_g': 'new_m', 'new_v_norm1_g': 'new_v', 'new_v_w_in': 'new_v', 'new_v_conv_a_w': 'new_v', 'new_v_conv_a_b': 'new_v', 'new_v_ln_a_g': 'new_v', 'new_v_ln_a_b': 'new_v', 'new_v_ln_b_g': 'new_v', 'new_v_ln_b_b': 'new_v', 'new_v_w_spatial': 'new_v', 'new_v_b_spatial': 'new_v', 'new_v_conv_c_w': 'new_v', 'new_v_conv_c_b': 'new_v', 'new_v_dt_bias': 'new_v', 'new_v_a_log': 'new_v', 'new_v_d_skip': 'new_v', 'new_v_norm_c_g': 'new_v', 'new_v_w_out': 'new_v', 'new_v_norm2_g': 'new_v', 'new_v_w_ff1': 'new_v', 'new_v_w_ff2': 'new_v', 'new_v_final_g': 'new_v'}


def _forward(args):
    return _fwd_reference(*[args[k] for k in FWD_PARAMS])


def _output_shape():
    out = _jax.eval_shape(lambda: _forward(_fwd_setup_inputs(0)))
    return out.shape, out.dtype

N_MICROBATCH = 1
ADAM_LR = 0.001
ADAM_B1 = 0.9
ADAM_B2 = 0.999
ADAM_EPS = 1e-08
ADAM_WD = 0.01
ADAM_STEP = 10
PER_EXAMPLE_BATCH_AXIS = {'x': 0, 'loss_target': 0}
SHARED_INPUTS = []
_WEIGHT_DTYPES = {'norm1_g': _jnp.float32, 'w_in': _jnp.float32, 'conv_a_w': _jnp.float32, 'conv_a_b': _jnp.float32, 'ln_a_g': _jnp.float32, 'ln_a_b': _jnp.float32, 'ln_b_g': _jnp.float32, 'ln_b_b': _jnp.float32, 'w_spatial': _jnp.float32, 'b_spatial': _jnp.float32, 'conv_c_w': _jnp.float32, 'conv_c_b': _jnp.float32, 'dt_bias': _jnp.float32, 'a_log': _jnp.float32, 'd_skip': _jnp.float32, 'norm_c_g': _jnp.float32, 'w_out': _jnp.float32, 'norm2_g': _jnp.float32, 'w_ff1': _jnp.float32, 'w_ff2': _jnp.float32, 'final_g': _jnp.float32}
MOMENT_SCALE = {'norm1_g': 1.513734e-01, 'w_in': 7.079054e-02, 'conv_a_w': 5.507963e-02, 'conv_a_b': 1.198301e-01, 'ln_a_g': 7.607050e-02, 'ln_a_b': 7.399613e-02, 'ln_b_g': 4.257727e-02, 'ln_b_b': 4.091968e-02, 'w_spatial': 2.761352e-02, 'b_spatial': 4.006922e-02, 'conv_c_w': 7.641881e-02, 'conv_c_b': 1.046025e-01, 'dt_bias': 3.974114e-01, 'a_log': 2.990760e-01, 'd_skip': 4.755316e-01, 'norm_c_g': 8.634654e-02, 'w_out': 1.075831e-01, 'norm2_g': 1.362086e-01, 'w_ff1': 6.507535e-02, 'w_ff2': 1.241596e-01, 'final_g': 3.283940e+01}


def _to_microbatches(a, axis):
    t = _jnp.moveaxis(a, axis, 0)
    t = t.reshape((N_MICROBATCH, t.shape[0] // N_MICROBATCH) + t.shape[1:])
    return _jnp.moveaxis(t, 1, axis + 1)


def setup_inputs(seed: int = 0) -> dict:
    inp = _fwd_setup_inputs(seed)
    key = _jax.random.fold_in(_jax.random.key(seed), 7919)
    shape, _ = _output_shape()
    out = dict(inp)
    out["loss_target"] = _jax.random.normal(_jax.random.fold_in(key, 0), shape, _jnp.float32)
    for i, name in enumerate(TWIN_WEIGHTS):
        w = inp[name].astype(_jnp.float32)
        if MOMENT_SCALE is None:
            s = _jnp.sqrt(_jnp.mean(_jnp.square(w)) + 1e-30)
        else:
            s = MOMENT_SCALE[name]
        km, kv = _jax.random.split(_jax.random.fold_in(key, i + 1))
        out[name] = w
        out["m_" + name] = s * _jax.random.normal(km, w.shape, _jnp.float32)
        out["v_" + name] = (s * s) * _jax.random.uniform(kv, w.shape, _jnp.float32, 0.5, 1.5)
    if N_MICROBATCH > 1:
        for name, axis in PER_EXAMPLE_BATCH_AXIS.items():
            out[name] = _to_microbatches(out[name], axis)
    return {'x': out['x'], 'norm1_g': out['norm1_g'], 'w_in': out['w_in'], 'conv_a_w': out['conv_a_w'], 'conv_a_b': out['conv_a_b'], 'ln_a_g': out['ln_a_g'], 'ln_a_b': out['ln_a_b'], 'ln_b_g': out['ln_b_g'], 'ln_b_b': out['ln_b_b'], 'w_spatial': out['w_spatial'], 'b_spatial': out['b_spatial'], 'conv_c_w': out['conv_c_w'], 'conv_c_b': out['conv_c_b'], 'dt_bias': out['dt_bias'], 'a_log': out['a_log'], 'd_skip': out['d_skip'], 'norm_c_g': out['norm_c_g'], 'w_out': out['w_out'], 'norm2_g': out['norm2_g'], 'w_ff1': out['w_ff1'], 'w_ff2': out['w_ff2'], 'final_g': out['final_g'], 'loss_target': out['loss_target'], 'm_norm1_g': out['m_norm1_g'], 'm_w_in': out['m_w_in'], 'm_conv_a_w': out['m_conv_a_w'], 'm_conv_a_b': out['m_conv_a_b'], 'm_ln_a_g': out['m_ln_a_g'], 'm_ln_a_b': out['m_ln_a_b'], 'm_ln_b_g': out['m_ln_b_g'], 'm_ln_b_b': out['m_ln_b_b'], 'm_w_spatial': out['m_w_spatial'], 'm_b_spatial': out['m_b_spatial'], 'm_conv_c_w': out['m_conv_c_w'], 'm_conv_c_b': out['m_conv_c_b'], 'm_dt_bias': out['m_dt_bias'], 'm_a_log': out['m_a_log'], 'm_d_skip': out['m_d_skip'], 'm_norm_c_g': out['m_norm_c_g'], 'm_w_out': out['m_w_out'], 'm_norm2_g': out['m_norm2_g'], 'm_w_ff1': out['m_w_ff1'], 'm_w_ff2': out['m_w_ff2'], 'm_final_g': out['m_final_g'], 'v_norm1_g': out['v_norm1_g'], 'v_w_in': out['v_w_in'], 'v_conv_a_w': out['v_conv_a_w'], 'v_conv_a_b': out['v_conv_a_b'], 'v_ln_a_g': out['v_ln_a_g'], 'v_ln_a_b': out['v_ln_a_b'], 'v_ln_b_g': out['v_ln_b_g'], 'v_ln_b_b': out['v_ln_b_b'], 'v_w_spatial': out['v_w_spatial'], 'v_b_spatial': out['v_b_spatial'], 'v_conv_c_w': out['v_conv_c_w'], 'v_conv_c_b': out['v_conv_c_b'], 'v_dt_bias': out['v_dt_bias'], 'v_a_log': out['v_a_log'], 'v_d_skip': out['v_d_skip'], 'v_norm_c_g': out['v_norm_c_g'], 'v_w_out': out['v_w_out'], 'v_norm2_g': out['v_norm2_g'], 'v_w_ff1': out['v_w_ff1'], 'v_w_ff2': out['v_w_ff2'], 'v_final_g': out['v_final_g']}


def _loss(weights, diff, rest, loss_target):
    with _jax.named_scope("forward"):
        args = {**rest, TWIN_DIFF_INPUT: diff, **{k: w.astype(_WEIGHT_DTYPES[k]) for k, w in weights.items()}}
        y = _forward(args)
    with _jax.named_scope("loss_head"):
        err = _jnp.square(y.astype(_jnp.float32) - loss_target)
        return 0.5 * _jnp.sum(_jnp.mean(err, axis=-1)) if err.ndim else 0.5 * err


def _adamw(w, g, m, v):
    m = ADAM_B1 * m + (1.0 - ADAM_B1) * g
    v = ADAM_B2 * v + (1.0 - ADAM_B2) * _jnp.square(g)
    m_hat = m / (1.0 - ADAM_B1 ** ADAM_STEP)
    v_hat = v / (1.0 - ADAM_B2 ** ADAM_STEP)
    delta = -ADAM_LR * (m_hat / (_jnp.sqrt(v_hat) + ADAM_EPS) + ADAM_WD * w)
    return delta, m, v


def reference(x, norm1_g, w_in, conv_a_w, conv_a_b, ln_a_g, ln_a_b, ln_b_g, ln_b_b, w_spatial, b_spatial, conv_c_w, conv_c_b, dt_bias, a_log, d_skip, norm_c_g, w_out, norm2_g, w_ff1, w_ff2, final_g, loss_target, m_norm1_g, m_w_in, m_conv_a_w, m_conv_a_b, m_ln_a_g, m_ln_a_b, m_ln_b_g, m_ln_b_b, m_w_spatial, m_b_spatial, m_conv_c_w, m_conv_c_b, m_dt_bias, m_a_log, m_d_skip, m_norm_c_g, m_w_out, m_norm2_g, m_w_ff1, m_w_ff2, m_final_g, v_norm1_g, v_w_in, v_conv_a_w, v_conv_a_b, v_ln_a_g, v_ln_a_b, v_ln_b_g, v_ln_b_b, v_w_spatial, v_b_spatial, v_conv_c_w, v_conv_c_b, v_dt_bias, v_a_log, v_d_skip, v_norm_c_g, v_w_out, v_norm2_g, v_w_ff1, v_w_ff2, v_final_g):
    given = dict(x=x, norm1_g=norm1_g, w_in=w_in, conv_a_w=conv_a_w, conv_a_b=conv_a_b, ln_a_g=ln_a_g, ln_a_b=ln_a_b, ln_b_g=ln_b_g, ln_b_b=ln_b_b, w_spatial=w_spatial, b_spatial=b_spatial, conv_c_w=conv_c_w, conv_c_b=conv_c_b, dt_bias=dt_bias, a_log=a_log, d_skip=d_skip, norm_c_g=norm_c_g, w_out=w_out, norm2_g=norm2_g, w_ff1=w_ff1, w_ff2=w_ff2, final_g=final_g, loss_target=loss_target, m_norm1_g=m_norm1_g, m_w_in=m_w_in, m_conv_a_w=m_conv_a_w, m_conv_a_b=m_conv_a_b, m_ln_a_g=m_ln_a_g, m_ln_a_b=m_ln_a_b, m_ln_b_g=m_ln_b_g, m_ln_b_b=m_ln_b_b, m_w_spatial=m_w_spatial, m_b_spatial=m_b_spatial, m_conv_c_w=m_conv_c_w, m_conv_c_b=m_conv_c_b, m_dt_bias=m_dt_bias, m_a_log=m_a_log, m_d_skip=m_d_skip, m_norm_c_g=m_norm_c_g, m_w_out=m_w_out, m_norm2_g=m_norm2_g, m_w_ff1=m_w_ff1, m_w_ff2=m_w_ff2, m_final_g=m_final_g, v_norm1_g=v_norm1_g, v_w_in=v_w_in, v_conv_a_w=v_conv_a_w, v_conv_a_b=v_conv_a_b, v_ln_a_g=v_ln_a_g, v_ln_a_b=v_ln_a_b, v_ln_b_g=v_ln_b_g, v_ln_b_b=v_ln_b_b, v_w_spatial=v_w_spatial, v_b_spatial=v_b_spatial, v_conv_c_w=v_conv_c_w, v_conv_c_b=v_conv_c_b, v_dt_bias=v_dt_bias, v_a_log=v_a_log, v_d_skip=v_d_skip, v_norm_c_g=v_norm_c_g, v_w_out=v_w_out, v_norm2_g=v_norm2_g, v_w_ff1=v_w_ff1, v_w_ff2=v_w_ff2, v_final_g=v_final_g)
    weights = {n: given[n] for n in TWIN_WEIGHTS}
    shared = {n: given[n] for n in SHARED_INPUTS}
    per_example = {n: given[n] for n in ['x']}
    grad_fn = _jax.value_and_grad(_loss, argnums=(0, 1))

    def one_microbatch(ex, loss_target):
        ex = dict(ex)
        diff = ex.pop(TWIN_DIFF_INPUT)
        return grad_fn(weights, diff, {**shared, **ex}, loss_target)

    if N_MICROBATCH == 1:
        loss, (grad_w, grad_x) = one_microbatch(per_example, given["loss_target"])
    else:
        def body(carry, xs):
            loss_sum, grad_sum = carry
            l_k, (gw_k, gx_k) = one_microbatch(xs[0], xs[1])
            with _jax.named_scope("update"):
                return (loss_sum + l_k, _jax.tree.map(_jnp.add, grad_sum, gw_k)), gx_k

        init = (_jnp.zeros((), _jnp.float32), _jax.tree.map(_jnp.zeros_like, weights))
        (loss, grad_w), grad_x = _jax.lax.scan(body, init, (per_example, given["loss_target"]))
    with _jax.named_scope("update"):
        delta_w, new_m, new_v = {}, {}, {}
        for n in TWIN_WEIGHTS:
            delta_w[n], new_m[n], new_v[n] = _adamw(weights[n], grad_w[n], given["m_" + n], given["v_" + n])
    return (loss, grad_x, *[grad_w[n] for n in TWIN_WEIGHTS], *[delta_w[n] for n in TWIN_WEIGHTS],
            *[new_m[n] for n in TWIN_WEIGHTS], *[new_v[n] for n in TWIN_WEIGHTS])
```

```python
import functools
import math

import jax
import jax.numpy as jnp
from jax import lax
from jax.experimental import pallas as pl
from jax.experimental.pallas import tpu as pltpu

F32 = jnp.float32
BF16 = jnp.bfloat16
MESH = pl.DeviceIdType.MESH

D_MODEL = 1024
DEPTH = 4
HEAD_DIM = 64
A_WIDTH = 512
B_WIDTH = 512
C_WIDTH = 1024
C_HEADS = 16
CONV_A_K = 31
CONV_C_K = 4
CHUNK = 128
SSM_STATE = 128
D_CONV_C = 1536
D_MAIN = 4608
D_IN_PROJ = 4624
D_MIX = 2048
D_FF = 4096
EPS = 1e-5
NEG = -1e30
LANES = 128
CONV_PAD = 32
N_CHIPS = 4

ADAM_LR = 0.001
ADAM_B1 = 0.9
ADAM_B2 = 0.999
ADAM_EPS = 1e-08
ADAM_WD = 0.01
ADAM_STEP = 10

VMEM_LIMIT = 48 * 1024 * 1024

COL_AVAL, COL_AGATE, COL_BU, COL_BV, COL_Z, COL_XBC = 0, 4, 8, 12, 16, 24


def _cparams(*sem):
    return pltpu.CompilerParams(dimension_semantics=sem, vmem_limit_bytes=VMEM_LIMIT)


_DN = {"nn": (((1,), (0,)), ((), ())), "nt": (((1,), (1,)), ((), ())), "tn": (((0,), (0,)), ((), ()))}


def _dot_raw(a, b, mode):
    return lax.dot_general(a.astype(BF16), b.astype(BF16), _DN[mode], preferred_element_type=F32)


def _make_dot(mode):
    @jax.custom_vjp
    def f(a, b):
        return _dot_raw(a, b, mode)

    def fwd(a, b):
        return _dot_raw(a, b, mode), (a, b)

    def bwd(res, g):
        a, b = res
        if mode == "nn":
            return _dot_raw(g, b, "nt"), _dot_raw(a, g, "tn")
        if mode == "nt":
            return _dot_raw(g, b, "nn"), _dot_raw(g, a, "tn")
        return _dot_raw(b, g, "nt"), _dot_raw(a, g, "nn")

    f.defvjp(fwd, bwd)
    return f


_nn = _make_dot("nn")
_nt = _make_dot("nt")
_tn = _make_dot("tn")


def _xdot(a, e):
    return jnp.dot(a, e, precision=lax.Precision.HIGHEST, preferred_element_type=F32)


def _iota2(shape, dim):
    return lax.broadcasted_iota(jnp.int32, shape, dim)


def _gmean_impl(x):
    n = x.shape[-1]
    same = (_iota2((n, n), 0) < HEAD_DIM) == (_iota2((n, n), 1) < HEAD_DIM)
    p = jnp.where(same, 1.0 / HEAD_DIM, 0.0).astype(BF16)
    hi = x.astype(BF16)
    lo = (x - hi.astype(F32)).astype(BF16)
    dn = _DN["nn"]
    return (lax.dot_general(hi, p, dn, preferred_element_type=F32)
            + lax.dot_general(lo, p, dn, preferred_element_type=F32))


@jax.custom_vjp
def _gmean(x):
    return _gmean_impl(x)


_gmean.defvjp(lambda x: (_gmean_impl(x), None), lambda _, g: (_gmean_impl(g),))


def _sigmoid(x):
    return 1.0 / (1.0 + jnp.exp(-x))


def _silu(x):
    return x * _sigmoid(x)


def _gelu(x):
    return 0.5 * x * (1.0 + lax.erf(x * 0.7071067811865476))


def _softplus(x):
    return jnp.maximum(x, 0.0) + jnp.log(1.0 + jnp.exp(-jnp.abs(x)))


def _rms(x, g):
    return x * lax.rsqrt(jnp.mean(x * x, axis=-1, keepdims=True) + EPS) * g


def _ln64(x, g, b):
    mu = _gmean(x)
    xc = x - mu
    var = _gmean(xc * xc)
    return xc * lax.rsqrt(var + EPS) * g + b


def _lane_lt64(shape):
    return _iota2(shape, 1) < HEAD_DIM


def _pick(n, pref):
    for t in pref:
        if n % t == 0:
            return t
    return n


def _matmul(a, b, *, mode, name, add=None, epilogue=None, extra=None, out_dtypes=(F32,)):
    if mode == "nn":
        (m, k), (_, n) = a.shape, b.shape
    elif mode == "nt":
        (m, k), (n, _) = a.shape, b.shape
    else:
        (k, m), (_, n) = a.shape, b.shape
    tm = _pick(m, (512, 256, 128))
    tn = _pick(n, (512, 256, 128))
    tk = _pick(k, (1024, 512, 256, 128))
    nk = k // tk
    a_spec = {"nn": pl.BlockSpec((tm, tk), lambda i, j, kk: (i, kk)),
              "nt": pl.BlockSpec((tm, tk), lambda i, j, kk: (i, kk)),
              "tn": pl.BlockSpec((tk, tm), lambda i, j, kk: (kk, i))}[mode]
    b_spec = {"nn": pl.BlockSpec((tk, tn), lambda i, j, kk: (kk, j)),
              "nt": pl.BlockSpec((tn, tk), lambda i, j, kk: (j, kk)),
              "tn": pl.BlockSpec((tk, tn), lambda i, j, kk: (kk, j))}[mode]
    o_spec = pl.BlockSpec((tm, tn), lambda i, j, kk: (i, j))
    ins = [a, b]
    in_specs = [a_spec, b_spec]
    if add is not None:
        ins.append(add)
        in_specs.append(o_spec)
    if extra is not None:
        ins.append(extra)
        in_specs.append(o_spec)
    n_out = len(out_dtypes)

    def body(*refs):
        a_ref, b_ref = refs[0], refs[1]
        pos = 2
        add_ref = ex_ref = None
        if add is not None:
            add_ref = refs[pos]
            pos += 1
        if extra is not None:
            ex_ref = refs[pos]
            pos += 1
        o_refs = refs[pos:pos + n_out]
        acc_ref = refs[pos + n_out]
        kk = pl.program_id(2)

        @pl.when(kk == 0)
        def _():
            acc_ref[...] = jnp.zeros_like(acc_ref)

        acc_ref[...] += _dot_raw(a_ref[...], b_ref[...], mode)

        @pl.when(kk == nk - 1)
        def _():
            acc = acc_ref[...]
            if add_ref is not None:
                acc = acc + add_ref[...].astype(F32)
            outs = (acc,) if epilogue is None else epilogue(acc, None if ex_ref is None else ex_ref[...])
            for o_ref, o in zip(o_refs, outs):
                o_ref[...] = o.astype(o_ref.dtype)

    res = pl.pallas_call(
        body, name=name, grid=(m // tm, n // tn, nk),
        in_specs=in_specs, out_specs=[o_spec] * n_out,
        out_shape=[jax.ShapeDtypeStruct((m, n), dt) for dt in out_dtypes],
        scratch_shapes=[pltpu.VMEM((tm, tn), F32)],
        compiler_params=_cparams("parallel", "parallel", "arbitrary"),
    )(*ins)
    return res[0] if n_out == 1 else res


def _relu2_epilogue(acc, _):
    r = jnp.maximum(acc, 0.0)
    return acc, r * r


def _relu2_bwd_epilogue(acc, u):
    return (acc * (2.0 * jnp.maximum(u, 0.0)),)


def _row_tile(t):
    return _pick(t, (512, 256, 128))


def _rms_fwd(x, g, name):
    t, d = x.shape
    tm = _row_tile(t)

    def body(x_ref, g_ref, o_ref):
        o_ref[...] = _rms(x_ref[...], g_ref[...]).astype(BF16)

    return pl.pallas_call(
        body, name=name, grid=(t // tm,),
        in_specs=[pl.BlockSpec((tm, d), lambda i: (i, 0)), pl.BlockSpec((1, d), lambda i: (0, 0))],
        out_specs=pl.BlockSpec((tm, d), lambda i: (i, 0)),
        out_shape=jax.ShapeDtypeStruct((t, d), BF16),
        compiler_params=_cparams("parallel"),
    )(x, g.reshape(1, d))


def _rms_bwd(x, g, dh, dres, name):
    t, d = x.shape
    tm = _row_tile(t)

    def body(x_ref, g_ref, dh_ref, dres_ref, dx_ref, dg_ref):
        @pl.when(pl.program_id(0) == 0)
        def _():
            dg_ref[...] = jnp.zeros_like(dg_ref)

        _, vjp = jax.vjp(_rms, x_ref[...], g_ref[...])
        dx, dg = vjp(dh_ref[...].astype(F32))
        dx_ref[...] = dx + dres_ref[...]
        dg_ref[...] += dg

    row = pl.BlockSpec((tm, d), lambda i: (i, 0))
    vec = pl.BlockSpec((1, d), lambda i: (0, 0))
    dx, dg = pl.pallas_call(
        body, name=name, grid=(t // tm,),
        in_specs=[row, vec, row, row], out_specs=[row, vec],
        out_shape=[jax.ShapeDtypeStruct((t, d), F32), jax.ShapeDtypeStruct((1, d), F32)],
        compiler_params=_cparams("arbitrary"),
    )(x, g.reshape(1, d), dh, dres)
    return dx, dg.reshape(d)


def _loss_head(x, g, target):
    t, d = x.shape
    tm = _row_tile(t)

    def loss_fn(xv, gv, tv):
        err = _rms(xv, gv) - tv
        return 0.5 * jnp.sum(jnp.mean(err * err, axis=-1, keepdims=True))

    def body(x_ref, g_ref, t_ref, loss_ref, dx_ref, dg_ref):
        @pl.when(pl.program_id(0) == 0)
        def _():
            dg_ref[...] = jnp.zeros_like(dg_ref)
            loss_ref[...] = jnp.zeros_like(loss_ref)

        tv = t_ref[...]
        val, vjp = jax.vjp(lambda xv, gv: loss_fn(xv, gv, tv), x_ref[...], g_ref[...])
        dx, dg = vjp(jnp.ones((), F32))
        dx_ref[...] = dx
        dg_ref[...] += dg
        loss_ref[...] += jnp.full(loss_ref.shape, val, F32)

    row = pl.BlockSpec((tm, d), lambda i: (i, 0))
    vec = pl.BlockSpec((1, d), lambda i: (0, 0))
    loss, dx, dg = pl.pallas_call(
        body, name="loss_head", grid=(t // tm,),
        in_specs=[row, vec, row], out_specs=[pl.BlockSpec((1, LANES), lambda i: (0, 0)), row, vec],
        out_shape=[jax.ShapeDtypeStruct((1, LANES), F32), jax.ShapeDtypeStruct((t, d), F32),
                   jax.ShapeDtypeStruct((1, d), F32)],
        compiler_params=_cparams("arbitrary"),
    )(x, g.reshape(1, d), target)
    return loss[0, 0], dx, dg.reshape(d)


def _pre_glu(val, gate):
    return val * _sigmoid(gate)


def _pre_id(x):
    return x


def _post_lnsilu(c, g, b):
    return _silu(_ln64(c, g, b))


def _post_silu(c):
    return _silu(c)


def _conv_cfg(kind):
    if kind == "a":
        return dict(k=CONV_A_K, pre=_pre_glu, post=_post_lnsilu, n_in=2, n_par=2, nblk=A_WIDTH // LANES,
                    cols=(COL_AVAL, COL_AGATE))
    return dict(k=CONV_C_K, pre=_pre_id, post=_post_silu, n_in=1, n_par=0, nblk=D_CONV_C // LANES,
                cols=(COL_XBC,))


def _conv_fwd(kind, proj, w, bias, params, seq, name):
    cfg = _conv_cfg(kind)
    kt, pre, post, n_in = cfg["k"], cfg["pre"], cfg["post"], cfg["n_in"]
    t = proj.shape[0]
    nseq = t // seq
    c = cfg["nblk"] * LANES
    rt = min(256, seq)
    nrt = seq // rt
    off0 = CONV_PAD - (kt - 1)

    def body(*refs):
        in_refs = refs[:n_in]
        w_ref, b_ref = refs[n_in], refs[n_in + 1]
        par_refs = refs[n_in + 2:n_in + 2 + cfg["n_par"]]
        o_ref, hpad = refs[n_in + 2 + cfg["n_par"]:]
        hpad[pl.ds(0, CONV_PAD), :] = jnp.zeros((CONV_PAD, LANES), F32)
        for r in range(nrt):
            hpad[pl.ds(CONV_PAD + r * rt, rt), :] = pre(*[x[pl.ds(r * rt, rt), :] for x in in_refs])
        pars = [p[...] for p in par_refs]
        for r in range(nrt):
            acc = jnp.broadcast_to(b_ref[...], (rt, LANES))
            for k in range(kt):
                acc = acc + w_ref[pl.ds(k, 1), :] * hpad[pl.ds(off0 + k + r * rt, rt), :]
            o_ref[pl.ds(r * rt, rt), :] = post(acc, *pars)

    in_specs = [pl.BlockSpec((seq, LANES), functools.partial(lambda s, j, col: (s, col + j), col=col))
                for col in cfg["cols"]]
    vec = pl.BlockSpec((1, LANES), lambda s, j: (0, j))
    in_specs += [pl.BlockSpec((CONV_PAD, LANES), lambda s, j: (0, j)), vec] + [vec] * cfg["n_par"]
    return pl.pallas_call(
        body, name=name, grid=(nseq, cfg["nblk"]),
        in_specs=in_specs, out_specs=pl.BlockSpec((seq, LANES), lambda s, j: (s, j)),
        out_shape=jax.ShapeDtypeStruct((t, c), F32),
        scratch_shapes=[pltpu.VMEM((seq + CONV_PAD, LANES), F32)],
        compiler_params=_cparams("parallel", "parallel"),
    )(*([proj] * n_in), w, bias, *params)


def _conv_bwd(kind, proj, w, bias, params, dy, seq, name):
    cfg = _conv_cfg(kind)
    kt, pre, post, n_in, n_par = cfg["k"], cfg["pre"], cfg["post"], cfg["n_in"], cfg["n_par"]
    t = proj.shape[0]
    nseq = t // seq
    c = cfg["nblk"] * LANES
    rt = min(256, seq)
    nrt = seq // rt
    off0 = CONV_PAD - (kt - 1)

    def body(*refs):
        in_refs = refs[:n_in]
        w_ref, b_ref = refs[n_in], refs[n_in + 1]
        par_refs = refs[n_in + 2:n_in + 2 + n_par]
        pos = n_in + 2 + n_par
        dy_ref = refs[pos]
        din_refs = refs[pos + 1:pos + 1 + n_in]
        dw_ref, db_ref = refs[pos + 1 + n_in], refs[pos + 2 + n_in]
        dpar_refs = refs[pos + 3 + n_in:pos + 3 + n_in + n_par]
        hpad, dcpad = refs[pos + 3 + n_in + n_par:]

        @pl.when(pl.program_id(1) == 0)
        def _():
            dw_ref[...] = jnp.zeros_like(dw_ref)
            db_ref[...] = jnp.zeros_like(db_ref)
            for r in dpar_refs:
                r[...] = jnp.zeros_like(r)

        hpad[pl.ds(0, CONV_PAD), :] = jnp.zeros((CONV_PAD, LANES), F32)
        dcpad[pl.ds(seq, CONV_PAD), :] = jnp.zeros((CONV_PAD, LANES), F32)
        for r in range(nrt):
            hpad[pl.ds(CONV_PAD + r * rt, rt), :] = pre(*[x[pl.ds(r * rt, rt), :] for x in in_refs])
        pars = [p[...] for p in par_refs]
        for r in range(nrt):
            acc = jnp.broadcast_to(b_ref[...], (rt, LANES))
            for k in range(kt):
                acc = acc + w_ref[pl.ds(k, 1), :] * hpad[pl.ds(off0 + k + r * rt, rt), :]
            _, vjp = jax.vjp(post, acc, *pars)
            grads = vjp(dy_ref[pl.ds(r * rt, rt), :])
            dcpad[pl.ds(r * rt, rt), :] = grads[0]
            db_ref[...] += jnp.sum(grads[0], axis=0, keepdims=True)
            for ref, gpar in zip(dpar_refs, grads[1:]):
                ref[...] += gpar
        for r in range(nrt):
            dh = jnp.zeros((rt, LANES), F32)
            for k in range(kt):
                dh = dh + w_ref[pl.ds(k, 1), :] * dcpad[pl.ds(r * rt + kt - 1 - k, rt), :]
            _, vjp = jax.vjp(pre, *[x[pl.ds(r * rt, rt), :] for x in in_refs])
            for ref, gin in zip(din_refs, vjp(dh)):
                ref[pl.ds(r * rt, rt), :] = gin.astype(ref.dtype)
        for k in range(kt):
            s = jnp.zeros((1, LANES), F32)
            for r in range(nrt):
                s = s + jnp.sum(dcpad[pl.ds(r * rt, rt), :] * hpad[pl.ds(off0 + k + r * rt, rt), :],
                                axis=0, keepdims=True)
            dw_ref[pl.ds(k, 1), :] += s

    in_specs = [pl.BlockSpec((seq, LANES), functools.partial(lambda j, s, col: (s, col + j), col=col))
                for col in cfg["cols"]]
    vec = pl.BlockSpec((1, LANES), lambda j, s: (0, j))
    wspec = pl.BlockSpec((CONV_PAD, LANES), lambda j, s: (0, j))
    blk = pl.BlockSpec((seq, LANES), lambda j, s: (s, j))
    in_specs += [wspec, vec] + [vec] * n_par + [blk]
    out_specs = [blk] * n_in + [wspec, vec] + [vec] * n_par
    out_shape = ([jax.ShapeDtypeStruct((t, c), BF16)] * n_in
                 + [jax.ShapeDtypeStruct((CONV_PAD, c), F32), jax.ShapeDtypeStruct((1, c), F32)]
                 + [jax.ShapeDtypeStruct((1, c), F32)] * n_par)
    res = pl.pallas_call(
        body, name=name, grid=(cfg["nblk"], nseq),
        in_specs=in_specs, out_specs=out_specs, out_shape=out_shape,
        scratch_shapes=[pltpu.VMEM((seq + CONV_PAD, LANES), F32), pltpu.VMEM((seq + CONV_PAD, LANES), F32)],
        compiler_params=_cparams("parallel", "arbitrary"),
    )(*([proj] * n_in), w, bias, *params, dy)
    return res[:n_in], res[n_in], res[n_in + 1], res[n_in + 2:]


def _gmlp_chunk(bu, bv, g, b, w0, w1, b0row, b1row):
    u = _gelu(bu)
    vn = _ln64(_gelu(bv), g, b)
    tri = _iota2((CHUNK, CHUNK), 0) >= _iota2((CHUNK, CHUNK), 1)
    m0 = _nn(jnp.where(tri, w0, 0.0), vn) + jnp.broadcast_to(b0row, (CHUNK, CHUNK)).T
    m1 = _nn(jnp.where(tri, w1, 0.0), vn) + jnp.broadcast_to(b1row, (CHUNK, CHUNK)).T
    return u * jnp.where(_lane_lt64((CHUNK, LANES)), m0, m1)


def _gmlp_specs(tm, order):
    def im(f):
        return lambda *ids: f(*order(*ids))
    return dict(
        bu=pl.BlockSpec((tm, LANES), im(lambda j, r: (r, COL_BU + j))),
        bv=pl.BlockSpec((tm, LANES), im(lambda j, r: (r, COL_BV + j))),
        vec=pl.BlockSpec((1, LANES), im(lambda j, r: (0, j))),
        ws=pl.BlockSpec((2, CHUNK, CHUNK), im(lambda j, r: (j, 0, 0))),
        bs=pl.BlockSpec((None, 2, CHUNK), im(lambda j, r: (j, 0, 0))),
        blk=pl.BlockSpec((tm, LANES), im(lambda j, r: (r, j))),
    )


def _gmlp_fwd(proj, ln_g, ln_b, w_s, b_s, name):
    t = proj.shape[0]
    tm = _row_tile(t)
    nch = tm // CHUNK
    sp = _gmlp_specs(tm, lambda r, j: (j, r))

    def body(bu_ref, bv_ref, g_ref, b_ref, ws_ref, bs_ref, o_ref):
        for ci in range(nch):
            rows = pl.ds(ci * CHUNK, CHUNK)
            o_ref[rows, :] = _gmlp_chunk(bu_ref[rows, :], bv_ref[rows, :], g_ref[...], b_ref[...],
                                         ws_ref[0], ws_ref[1], bs_ref[pl.ds(0, 1), :], bs_ref[pl.ds(1, 1), :])

    return pl.pallas_call(
        body, name=name, grid=(t // tm, B_WIDTH // LANES),
        in_specs=[sp["bu"], sp["bv"], sp["vec"], sp["vec"], sp["ws"], sp["bs"]],
        out_specs=sp["blk"], out_shape=jax.ShapeDtypeStruct((t, B_WIDTH), F32),
        compiler_params=_cparams("parallel", "parallel"),
    )(proj, proj, ln_g, ln_b, w_s, b_s.reshape(B_WIDTH // LANES, 2, CHUNK))


def _gmlp_bwd(proj, ln_g, ln_b, w_s, b_s, dy, name):
    t = proj.shape[0]
    tm = _row_tile(t)
    nch = tm // CHUNK
    sp = _gmlp_specs(tm, lambda j, r: (j, r))

    def body(bu_ref, bv_ref, g_ref, b_ref, ws_ref, bs_ref, dy_ref, dbu_ref, dbv_ref, dg_ref, db_ref, dws_ref, dbs_ref):
        @pl.when(pl.program_id(1) == 0)
        def _():
            for r in (dg_ref, db_ref, dws_ref, dbs_ref):
                r[...] = jnp.zeros_like(r)

        for ci in range(nch):
            rows = pl.ds(ci * CHUNK, CHUNK)
            _, vjp = jax.vjp(_gmlp_chunk, bu_ref[rows, :], bv_ref[rows, :], g_ref[...], b_ref[...],
                             ws_ref[0], ws_ref[1], bs_ref[pl.ds(0, 1), :], bs_ref[pl.ds(1, 1), :])
            dbu, dbv, dg, db, dw0, dw1, db0, db1 = vjp(dy_ref[rows, :])
            dbu_ref[rows, :] = dbu.astype(BF16)
            dbv_ref[rows, :] = dbv.astype(BF16)
            dg_ref[...] += dg
            db_ref[...] += db
            dws_ref[0] += dw0
            dws_ref[1] += dw1
            dbs_ref[pl.ds(0, 1), :] += db0
            dbs_ref[pl.ds(1, 1), :] += db1

    nh = B_WIDTH // LANES
    res = pl.pallas_call(
        body, name=name, grid=(nh, t // tm),
        in_specs=[sp["bu"], sp["bv"], sp["vec"], sp["vec"], sp["ws"], sp["bs"], sp["blk"]],
        out_specs=[sp["blk"], sp["blk"], sp["vec"], sp["vec"], sp["ws"], sp["bs"]],
        out_shape=[jax.ShapeDtypeStruct((t, B_WIDTH), BF16), jax.ShapeDtypeStruct((t, B_WIDTH), BF16),
                   jax.ShapeDtypeStruct((1, B_WIDTH), F32), jax.ShapeDtypeStruct((1, B_WIDTH), F32),
                   jax.ShapeDtypeStruct(w_s.shape, F32), jax.ShapeDtypeStruct((nh, 2, CHUNK), F32)],
        compiler_params=_cparams("parallel", "arbitrary"),
    )(proj, proj, ln_g, ln_b, w_s, b_s.reshape(nh, 2, CHUNK), dy)
    dbu, dbv, dg, db, dws, dbs = res
    return dbu, dbv, dg, db, dws, dbs.reshape(b_s.shape)


def _expand_mats():
    head = jnp.arange(LANES)[:, None]
    e64 = (head == (jnp.arange(C_WIDTH)[None, :] // HEAD_DIM)).astype(F32)
    e128 = (head == (jnp.arange(C_HEADS * LANES)[None, :] // LANES)).astype(F32)
    return e64, e128


def _ssd_prep_fn(dt_raw, dt_bias, a_log, e64, e128):
    dt = _softplus(dt_raw + dt_bias)
    a = dt * (-jnp.exp(a_log))
    incl = (_iota2((CHUNK, CHUNK), 0) >= _iota2((CHUNK, CHUNK), 1)).astype(F32)
    acs = _xdot(incl, a)
    alast = _xdot(jnp.ones((CHUNK, CHUNK), F32), a)
    return _xdot(dt, e64), _xdot(acs, e64), _xdot(alast, e64), _xdot(acs, e128)


def _ssd_prep_specs():
    blk = lambda w: pl.BlockSpec((CHUNK, w), lambda i: (i, 0))
    const = lambda r, w: pl.BlockSpec((r, w), lambda i: (0, 0))
    ins = [blk(LANES), const(1, LANES), const(1, LANES), const(LANES, C_WIDTH), const(LANES, C_HEADS * LANES)]
    outs = [blk(C_WIDTH), blk(C_WIDTH), blk(C_WIDTH), blk(C_HEADS * LANES)]
    return ins, outs


def _ssd_prep_fwd(dt_raw, dt_bias, a_log, name):
    t = dt_raw.shape[0]
    e64, e128 = _expand_mats()
    ins, outs = _ssd_prep_specs()

    def body(raw_ref, bias_ref, alog_ref, e64_ref, e128_ref, dt_ref, acs_ref, alast_ref, acs128_ref):
        res = _ssd_prep_fn(raw_ref[...], bias_ref[...], alog_ref[...], e64_ref[...], e128_ref[...])
        for ref, v in zip((dt_ref, acs_ref, alast_ref, acs128_ref), res):
            ref[...] = v

    return pl.pallas_call(
        body, name=name, grid=(t // CHUNK,), in_specs=ins, out_specs=outs,
        out_shape=[jax.ShapeDtypeStruct((t, C_WIDTH), F32)] * 3 + [jax.ShapeDtypeStruct((t, C_HEADS * LANES), F32)],
        compiler_params=_cparams("parallel"),
    )(dt_raw, dt_bias, a_log, e64, e128)


def _ssd_prep_bwd(dt_raw, dt_bias, a_log, d_dt, d_acs, d_alast, d_acs128, name):
    t = dt_raw.shape[0]
    e64, e128 = _expand_mats()
    ins, outs = _ssd_prep_specs()
    vec = pl.BlockSpec((1, LANES), lambda i: (0, 0))

    def body(raw_ref, bias_ref, alog_ref, e64_ref, e128_ref, g0, g1, g2, g3, draw_ref, dbias_ref, dalog_ref):
        @pl.when(pl.program_id(0) == 0)
        def _():
            dbias_ref[...] = jnp.zeros_like(dbias_ref)
            dalog_ref[...] = jnp.zeros_like(dalog_ref)

        e64v, e128v = e64_ref[...], e128_ref[...]
        _, vjp = jax.vjp(lambda r, b, al: _ssd_prep_fn(r, b, al, e64v, e128v),
                         raw_ref[...], bias_ref[...], alog_ref[...])
        draw, dbias, dalog = vjp((g0[...], g1[...], g2[...], g3[...]))
        draw_ref[...] = draw.astype(BF16)
        dbias_ref[...] += dbias
        dalog_ref[...] += dalog

    return pl.pallas_call(
        body, name=name, grid=(t // CHUNK,), in_specs=ins + outs,
        out_specs=[pl.BlockSpec((CHUNK, LANES), lambda i: (i, 0)), vec, vec],
        out_shape=[jax.ShapeDtypeStruct((t, LANES), BF16), jax.ShapeDtypeStruct((1, LANES), F32),
                   jax.ShapeDtypeStruct((1, LANES), F32)],
        compiler_params=_cparams("arbitrary"),
    )(dt_raw, dt_bias, a_log, e64, e128, d_dt, d_acs, d_alast, d_acs128)


def _ssd_chunk(x, dt, acs, alast, col0, col1, bm, cm, prev):
    xdt = x * dt
    cb = _nt(cm, bm)
    tri = _iota2((CHUNK, CHUNK), 0) >= _iota2((CHUNK, CHUNK), 1)
    l0 = jnp.exp(jnp.where(tri, col0 - col0.T, NEG))
    l1 = jnp.exp(jnp.where(tri, col1 - col1.T, NEG))
    ydiag = jnp.where(_lane_lt64((CHUNK, LANES)), _nn(cb * l0, xdt), _nn(cb * l1, xdt))
    states = _tn(xdt * jnp.exp(alast - acs), bm)
    yoff = _nt(cm, prev) * jnp.exp(acs)
    new = prev * jnp.exp(alast).T + states
    return ydiag + yoff, new


def _ssd_specs(seq, rev):
    ncs = seq // CHUNK
    npair = C_WIDTH // LANES

    def row(s, c):
        return s * ncs + (ncs - 1 - c if rev else c)

    return dict(
        x=pl.BlockSpec((CHUNK, LANES), lambda s, j, c: (row(s, c), j)),
        bm=pl.BlockSpec((CHUNK, SSM_STATE), lambda s, j, c: (row(s, c), C_WIDTH // LANES + j // 4)),
        cm=pl.BlockSpec((CHUNK, SSM_STATE), lambda s, j, c: (row(s, c), C_WIDTH // LANES + 2 + j // 4)),
        col=pl.BlockSpec((CHUNK, 2 * LANES), lambda s, j, c: (row(s, c), j)),
        st=pl.BlockSpec((None, None, LANES, SSM_STATE), lambda s, j, c: (row(s, c), j, 0, 0)),
        npair=npair, ncs=ncs,
    )


def _ssd_fwd(xbc_act, dt64, acs64, alast64, acs128, seq, name):
    t = xbc_act.shape[0]
    sp = _ssd_specs(seq, False)

    def body(x_ref, dt_ref, acs_ref, alast_ref, col_ref, bm_ref, cm_ref, y_ref, prev_ref, state):
        @pl.when(pl.program_id(2) == 0)
        def _():
            state[...] = jnp.zeros_like(state)

        prev = state[...]
        prev_ref[...] = prev
        y, new = _ssd_chunk(x_ref[...], dt_ref[...], acs_ref[...], alast_ref[...],
                            col_ref[:, pl.ds(0, LANES)], col_ref[:, pl.ds(LANES, LANES)],
                            bm_ref[...], cm_ref[...], prev)
        y_ref[...] = y
        state[...] = new

    return pl.pallas_call(
        body, name=name, grid=(t // seq, sp["npair"], sp["ncs"]),
        in_specs=[sp["x"], sp["x"], sp["x"], sp["x"], sp["col"], sp["bm"], sp["cm"]],
        out_specs=[sp["x"], sp["st"]],
        out_shape=[jax.ShapeDtypeStruct((t, C_WIDTH), F32),
                   jax.ShapeDtypeStruct((t // CHUNK, sp["npair"], LANES, SSM_STATE), F32)],
        scratch_shapes=[pltpu.VMEM((LANES, SSM_STATE), F32)],
        compiler_params=_cparams("parallel", "parallel", "arbitrary"),
    )(xbc_act, dt64, acs64, alast64, acs128, xbc_act, xbc_act)


def _ssd_bwd(xbc_act, dt64, acs64, alast64, acs128, prev_saved, dy, seq, name):
    t = xbc_act.shape[0]
    sp = _ssd_specs(seq, True)

    def body(x_ref, dt_ref, acs_ref, alast_ref, col_ref, bm_ref, cm_ref, prev_ref, dy_ref,
             dx_ref, ddt_ref, dacs_ref, dalast_ref, dcol_ref, dbc_ref, dstate):
        @pl.when(pl.program_id(2) == 0)
        def _():
            dstate[...] = jnp.zeros_like(dstate)

        _, vjp = jax.vjp(_ssd_chunk, x_ref[...], dt_ref[...], acs_ref[...], alast_ref[...],
                         col_ref[:, pl.ds(0, LANES)], col_ref[:, pl.ds(LANES, LANES)],
                         bm_ref[...], cm_ref[...], prev_ref[...])
        dx, ddt, dacs, dalast, dc0, dc1, dbm, dcm, dprev = vjp((dy_ref[...], dstate[...]))
        dx_ref[...] = dx
        ddt_ref[...] = ddt
        dacs_ref[...] = dacs
        dalast_ref[...] = dalast
        dcol_ref[:, pl.ds(0, LANES)] = dc0
        dcol_ref[:, pl.ds(LANES, LANES)] = dc1
        dbc_ref[:, pl.ds(0, SSM_STATE)] = dbm
        dbc_ref[:, pl.ds(SSM_STATE, SSM_STATE)] = dcm
        dstate[...] = dprev

    wide = jax.ShapeDtypeStruct((t, C_WIDTH), F32)
    return pl.pallas_call(
        body, name=name, grid=(t // seq, sp["npair"], sp["ncs"]),
        in_specs=[sp["x"], sp["x"], sp["x"], sp["x"], sp["col"], sp["bm"], sp["cm"], sp["st"], sp["x"]],
        out_specs=[sp["x"], sp["x"], sp["x"], sp["x"], sp["col"], sp["col"]],
        out_shape=[wide, wide, wide, wide, jax.ShapeDtypeStruct((t, 2 * C_WIDTH), F32),
                   jax.ShapeDtypeStruct((t, 2 * C_WIDTH), F32)],
        scratch_shapes=[pltpu.VMEM((LANES, SSM_STATE), F32)],
        compiler_params=_cparams("parallel", "parallel", "arbitrary"),
    )(xbc_act, dt64, acs64, alast64, acs128, xbc_act, xbc_act, prev_saved, dy)


def _ssd_post_fn(y, xs, z, dskip, g):
    v = (y + dskip * xs) * _silu(z)
    return v * lax.rsqrt(jnp.mean(v * v, axis=-1, keepdims=True) + EPS) * g


def _ssd_post_specs(tm, order):
    gw = C_WIDTH // 2

    def im(f):
        return lambda *ids: f(*order(*ids))
    return dict(
        blk=pl.BlockSpec((tm, gw), im(lambda g, r: (r, g))),
        z=pl.BlockSpec((tm, gw), im(lambda g, r: (r, COL_Z * LANES // gw + g))),
        vec=pl.BlockSpec((1, gw), im(lambda g, r: (0, g))),
    )


def _ssd_post_fwd(y_ssd, xbc_act, proj, dskip64, norm_g, name):
    t = y_ssd.shape[0]
    tm = _row_tile(t)
    sp = _ssd_post_specs(tm, lambda r, g: (g, r))

    def body(y_ref, xs_ref, z_ref, ds_ref, g_ref, o_ref):
        o_ref[...] = _ssd_post_fn(y_ref[...], xs_ref[...], z_ref[...], ds_ref[...], g_ref[...])

    return pl.pallas_call(
        body, name=name, grid=(t // tm, 2),
        in_specs=[sp["blk"], sp["blk"], sp["z"], sp["vec"], sp["vec"]], out_specs=sp["blk"],
        out_shape=jax.ShapeDtypeStruct((t, C_WIDTH), F32),
        compiler_params=_cparams("parallel", "parallel"),
    )(y_ssd, xbc_act, proj, dskip64, norm_g)


def _ssd_post_bwd(y_ssd, xbc_act, proj, dskip64, norm_g, dyc, name):
    t = y_ssd.shape[0]
    tm = _row_tile(t)
    sp = _ssd_post_specs(tm, lambda g, r: (g, r))

    def body(y_ref, xs_ref, z_ref, ds_ref, g_ref, dyc_ref, dy_ref, dxs_ref, dz_ref, dds_ref, dg_ref):
        @pl.when(pl.program_id(1) == 0)
        def _():
            dds_ref[...] = jnp.zeros_like(dds_ref)
            dg_ref[...] = jnp.zeros_like(dg_ref)

        _, vjp = jax.vjp(_ssd_post_fn, y_ref[...], xs_ref[...], z_ref[...], ds_ref[...], g_ref[...])
        dy, dxs, dz, dds, dg = vjp(dyc_ref[...])
        dy_ref[...] = dy
        dxs_ref[...] = dxs
        dz_ref[...] = dz.astype(BF16)
        dds_ref[...] += dds
        dg_ref[...] += dg

    wide = jax.ShapeDtypeStruct((t, C_WIDTH), F32)
    vec = jax.ShapeDtypeStruct((1, C_WIDTH), F32)
    return pl.pallas_call(
        body, name=name, grid=(2, t // tm),
        in_specs=[sp["blk"], sp["blk"], sp["z"], sp["vec"], sp["vec"], sp["blk"]],
        out_specs=[sp["blk"], sp["blk"], sp["blk"], sp["vec"], sp["vec"]],
        out_shape=[wide, wide, jax.ShapeDtypeStruct((t, C_WIDTH), BF16), vec, vec],
        compiler_params=_cparams("parallel", "arbitrary"),
    )(y_ssd, xbc_act, proj, dskip64, norm_g, dyc)


def _ssd_assemble(dxs_ssd, dxs_skip, dbc, name):
    t = dxs_ssd.shape[0]
    tm = _row_tile(t)
    npair = C_WIDTH // LANES

    def body(a_ref, b_ref, dbc_ref, o_ref):
        o_ref[:, pl.ds(0, C_WIDTH)] = a_ref[...] + b_ref[...]
        for grp in range(2):
            for which in range(2):
                acc = jnp.zeros((tm, SSM_STATE), F32)
                for j in range(grp * npair // 2, (grp + 1) * npair // 2):
                    acc = acc + dbc_ref[:, pl.ds((2 * j + which) * SSM_STATE, SSM_STATE)]
                o_ref[:, pl.ds(C_WIDTH + (2 * which + grp) * SSM_STATE, SSM_STATE)] = acc

    return pl.pallas_call(
        body, name=name, grid=(t // tm,),
        in_specs=[pl.BlockSpec((tm, C_WIDTH), lambda i: (i, 0))] * 2 + [pl.BlockSpec((tm, 2 * C_WIDTH), lambda i: (i, 0))],
        out_specs=pl.BlockSpec((tm, D_CONV_C), lambda i: (i, 0)),
        out_shape=jax.ShapeDtypeStruct((t, D_CONV_C), F32),
        compiler_params=_cparams("parallel"),
    )(dxs_ssd, dxs_skip, dbc)


def _pad_taps(w):
    return jnp.pad(w, ((0, CONV_PAD - w.shape[0]), (0, 0)))


def _pad_heads(v):
    return jnp.pad(v, (0, LANES - v.shape[0])).reshape(1, LANES)


def _layer_fwd(x, p, seq, li):
    n = f"l{li}_"
    h1 = _rms_fwd(x, p["norm1_g"], n + "rms1")
    proj = _matmul(h1, p["w_main"], mode="nn", name=n + "inproj")
    dt_raw = _matmul(h1, p["w_dt"], mode="nn", name=n + "inproj_dt")
    row = lambda v: v.reshape(1, -1)
    ya = _conv_fwd("a", proj, _pad_taps(p["conv_a_w"]), row(p["conv_a_b"]), (row(p["ln_a_g"]), row(p["ln_a_b"])),
                   seq, n + "conva")
    yb = _gmlp_fwd(proj, row(p["ln_b_g"]), row(p["ln_b_b"]), p["w_spatial"], p["b_spatial"], n + "gmlp")
    xbc_act = _conv_fwd("c", proj, _pad_taps(p["conv_c_w"]), row(p["conv_c_b"]), (), seq, n + "convc")
    dt64, acs64, alast64, acs128 = _ssd_prep_fwd(dt_raw, _pad_heads(p["dt_bias"]), _pad_heads(p["a_log"]), n + "ssdprep")
    y_ssd, prev = _ssd_fwd(xbc_act, dt64, acs64, alast64, acs128, seq, n + "ssd")
    dskip64 = jnp.repeat(p["d_skip"], HEAD_DIM).reshape(1, C_WIDTH)
    yc = _ssd_post_fwd(y_ssd, xbc_act, proj, dskip64, row(p["norm_c_g"]), n + "ssdpost")
    ycat = jnp.concatenate([ya, yb, yc], axis=1).astype(BF16)
    x1 = _matmul(ycat, p["w_out"], mode="nn", name=n + "outproj", add=x)
    h2 = _rms_fwd(x1, p["norm2_g"], n + "rms2")
    u, act = _matmul(h2, p["w_ff1"], mode="nn", name=n + "ff1", epilogue=_relu2_epilogue, out_dtypes=(F32, BF16))
    x2 = _matmul(act, p["w_ff2"], mode="nn", name=n + "ff2", add=x1)
    saved = dict(x=x, h1=h1, proj=proj, dt_raw=dt_raw, xbc_act=xbc_act, dt64=dt64, acs64=acs64, alast64=alast64,
                 acs128=acs128, prev=prev, y_ssd=y_ssd, dskip64=dskip64, ycat=ycat, x1=x1, h2=h2, u=u, act=act)
    return x2, saved


def _layer_bwd(dx2, p, s, seq, li):
    n = f"l{li}_b_"
    row = lambda v: v.reshape(1, -1)
    g = {}
    du = _matmul(dx2, p["w_ff2"], mode="nt", name=n + "ff2_dx", epilogue=_relu2_bwd_epilogue, extra=s["u"],
                 out_dtypes=(BF16,))
    g["w_ff2"] = _matmul(s["act"], dx2, mode="tn", name=n + "ff2_dw")
    g["w_ff1"] = _matmul(s["h2"], du, mode="tn", name=n + "ff1_dw")
    dh2 = _matmul(du, p["w_ff1"], mode="nt", name=n + "ff1_dx")
    dx1, g["norm2_g"] = _rms_bwd(s["x1"], p["norm2_g"], dh2, dx2, n + "rms2")
    g["w_out"] = _matmul(s["ycat"], dx1, mode="tn", name=n + "out_dw")
    dycat = _matmul(dx1, p["w_out"], mode="nt", name=n + "out_dx")
    dya, dyb, dyc = dycat[:, :A_WIDTH], dycat[:, A_WIDTH:A_WIDTH + B_WIDTH], dycat[:, A_WIDTH + B_WIDTH:]
    proj = s["proj"]
    (dval, dgate), dwa, dba, (dlag, dlab) = _conv_bwd(
        "a", proj, _pad_taps(p["conv_a_w"]), row(p["conv_a_b"]), (row(p["ln_a_g"]), row(p["ln_a_b"])), dya, seq, n + "conva")
    g["conv_a_w"], g["conv_a_b"], g["ln_a_g"], g["ln_a_b"] = dwa[:CONV_A_K], dba[0], dlag[0], dlab[0]
    dbu, dbv, dlbg, dlbb, g["w_spatial"], g["b_spatial"] = _gmlp_bwd(
        proj, row(p["ln_b_g"]), row(p["ln_b_b"]), p["w_spatial"], p["b_spatial"], dyb, n + "gmlp")
    g["ln_b_g"], g["ln_b_b"] = dlbg[0], dlbb[0]
    dy_ssd, dxs_skip, dz, dds, dncg = _ssd_post_bwd(s["y_ssd"], s["xbc_act"], proj, s["dskip64"], row(p["norm_c_g"]),
                                                    dyc, n + "ssdpost")
    g["norm_c_g"] = dncg[0]
    g["d_skip"] = dds.reshape(C_HEADS, HEAD_DIM).sum(axis=1)
    dxs, ddt64, dacs64, dalast64, dacs128, dbc = _ssd_bwd(
        s["xbc_act"], s["dt64"], s["acs64"], s["alast64"], s["acs128"], s["prev"], dy_ssd, seq, n + "ssd")
    ddt_raw, ddtb, dalog = _ssd_prep_bwd(s["dt_raw"], _pad_heads(p["dt_bias"]), _pad_heads(p["a_log"]),
                                         ddt64, dacs64, dalast64, dacs128, n + "ssdprep")
    g["dt_bias"], g["a_log"] = ddtb[0, :C_HEADS], dalog[0, :C_HEADS]
    dconv = _ssd_assemble(dxs, dxs_skip, dbc, n + "ssdasm")
    (dxbc,), dwc, dbcv, _ = _conv_bwd("c", proj, _pad_taps(p["conv_c_w"]), row(p["conv_c_b"]), (), dconv, seq, n + "convc")
    g["conv_c_w"], g["conv_c_b"] = dwc[:CONV_C_K], dbcv[0]
    dproj = jnp.concatenate([dval, dgate, dbu, dbv, dz, dxbc], axis=1)
    g["w_main"] = _matmul(s["h1"], dproj, mode="tn", name=n + "in_dw")
    g["w_dt"] = _matmul(s["h1"], ddt_raw, mode="tn", name=n + "indt_dw")
    dh1 = _matmul(dproj, p["w_main"], mode="nt", name=n + "in_dx")
    dh1 = _matmul(ddt_raw, p["w_dt"], mode="nt", name=n + "indt_dx", add=dh1)
    dx, g["norm1_g"] = _rms_bwd(s["x"], p["norm1_g"], dh1, dx1, n + "rms1")
    return dx, g


def _step_local(x, target, layer_params, final_g, seq):
    saved = []
    h = x
    for li, p in enumerate(layer_params):
        h, s = _layer_fwd(h, p, seq, li)
        saved.append(s)
    loss, dx, dgf = _loss_head(h, final_g, target)
    grads = [None] * len(layer_params)
    for li in reversed(range(len(layer_params))):
        dx, grads[li] = _layer_bwd(dx, layer_params[li], saved[li], seq, li)
    return loss, dx, grads, dgf


def _ew(fn, ins, out_dtypes, name, leads=None):
    leads = leads or [None] * len(ins)
    rows, c = ins[0].shape[-2:]
    tr = _pick(rows, (256, 128, 64, 32, 16, 8))

    def spec(lead):
        if lead is None:
            return pl.BlockSpec((tr, c), lambda i: (i, 0))
        return pl.BlockSpec((None, tr, c), functools.partial(lambda i, k: (k, i, 0), k=lead))

    def body(*refs):
        outs = fn(*[r[...].astype(F32) for r in refs[:len(ins)]])
        for o_ref, o in zip(refs[len(ins):], outs):
            o_ref[...] = o.astype(o_ref.dtype)

    res = pl.pallas_call(
        body, name=name, grid=(rows // tr,),
        in_specs=[spec(l) for l in leads], out_specs=[spec(None)] * len(out_dtypes),
        out_shape=[jax.ShapeDtypeStruct((rows, c), dt) for dt in out_dtypes],
        compiler_params=_cparams("parallel"),
    )(*ins)
    return res


def _adam_fn(w, g, m, v):
    m2 = ADAM_B1 * m + (1.0 - ADAM_B1) * g
    v2 = ADAM_B2 * v + (1.0 - ADAM_B2) * (g * g)
    m_hat = m2 / (1.0 - ADAM_B1 ** ADAM_STEP)
    v_hat = v2 / (1.0 - ADAM_B2 ** ADAM_STEP)
    delta = -ADAM_LR * (m_hat / (jnp.sqrt(v_hat) + ADAM_EPS) + ADAM_WD * w)
    return delta, m2, v2


def _adam(w, g, m, v, name):
    shape = w.shape
    two_d = lambda a: a.reshape(-1, shape[-1])
    outs = _ew(_adam_fn, [two_d(w), two_d(g), two_d(m), two_d(v)], (F32, F32, F32), name)
    return [o.reshape(shape) for o in outs]


_ANY = pl.BlockSpec(memory_space=pl.ANY)


def _mesh_pos():
    return lax.axis_index("x"), lax.axis_index("y"), lax.axis_index("c")


def _peer_chips(x, y):
    return [(1 - x, y), (x, 1 - y), (1 - x, 1 - y)]


def _remote(src, dst, send_sems, recv_sems, sem, to):
    return pltpu.make_async_remote_copy(src_ref=src, dst_ref=dst, send_sem=send_sems.at[sem],
                                        recv_sem=recv_sems.at[sem], device_id=to, device_id_type=MESH)


def _half_rows(n_rows, which):
    half = n_rows // 2
    return pl.ds(pl.multiple_of(which * half, 8), half)


def _comm_call(body, ins, out_shapes, n_sems, name, n_local=0):
    scratch = [pltpu.SemaphoreType.DMA((n_sems,)), pltpu.SemaphoreType.DMA((n_sems,))]
    if n_local:
        scratch.append(pltpu.SemaphoreType.DMA((n_local,)))
    return pl.pallas_call(
        body, name=name, in_specs=[_ANY] * len(ins), out_specs=[_ANY] * len(out_shapes),
        out_shape=out_shapes, scratch_shapes=scratch,
        compiler_params=pltpu.CompilerParams(has_side_effects=True),
    )(*ins)


def _gather_weights(big, small, name):
    nb, ns = len(big), len(small)
    n = nb + ns

    def body(*refs):
        ins, outs = refs[:n], refs[n:2 * n]
        send_sems, recv_sems, local_sems = refs[2 * n:]
        x, y, c = _mesh_pos()
        q = 2 * x + y
        me, sib = (x, y, c), (x, y, 1 - c)
        chips = _peer_chips(x, y)
        rem = functools.partial(_remote, send_sems=send_sems, recv_sems=recv_sems)
        local = [pltpu.make_async_copy(ins[i], outs[i].at[q], local_sems.at[i]) for i in range(n)]
        for cp in local:
            cp.start()
        first = []
        for i in range(nb):
            mine = _half_rows(big[i].shape[0], c)
            for k, (px, py) in enumerate(chips):
                first.append(rem(ins[i].at[mine], outs[i].at[q, mine], sem=6 * i + k, to=(px, py, c)))
        for j in range(ns):
            for k, (px, py) in enumerate(chips):
                first.append(rem(ins[nb + j], outs[nb + j].at[q], sem=6 * nb + 3 * j + k, to=(px, py, c)))
        for cp in first:
            cp.start()
        passed = []
        for i in range(nb):
            mine = _half_rows(big[i].shape[0], c)
            for k, (px, py) in enumerate(chips):
                landed = outs[i].at[2 * px + py, mine]
                rem(landed, landed, sem=6 * i + k, to=me).wait_recv()
                fwd = rem(landed, landed, sem=6 * i + 3 + k, to=sib)
                fwd.start()
                passed.append(fwd)
        for i in range(nb):
            other = _half_rows(big[i].shape[0], 1 - c)
            for k, (px, py) in enumerate(chips):
                theirs = outs[i].at[2 * px + py, other]
                rem(theirs, theirs, sem=6 * i + 3 + k, to=me).wait_recv()
        for j in range(ns):
            for k, (px, py) in enumerate(chips):
                dst = outs[nb + j].at[2 * px + py]
                rem(dst, dst, sem=6 * nb + 3 * j + k, to=me).wait_recv()
        for cp in first + passed:
            cp.wait_send()
        for cp in local:
            cp.wait()

    out_shapes = [jax.ShapeDtypeStruct((N_CHIPS,) + a.shape, a.dtype) for a in list(big) + list(small)]
    return _comm_call(body, list(big) + list(small), out_shapes, 6 * nb + 3 * ns, name, n_local=n)


def _sibling_other_halves(gs, name):
    n = len(gs)

    def body(*refs):
        ins, outs = refs[:n], refs[n:2 * n]
        send_sems, recv_sems = refs[2 * n:]
        x, y, c = _mesh_pos()
        copies = [_remote(ins[i].at[:, _half_rows(gs[i].shape[1], 1 - c)], outs[i], send_sems, recv_sems, i, (x, y, 1 - c))
                  for i in range(n)]
        for cp in copies:
            cp.start()
        for cp in copies:
            cp.wait()

    out_shapes = [jax.ShapeDtypeStruct((N_CHIPS, g.shape[1] // 2, g.shape[2]), g.dtype) for g in gs]
    return _comm_call(body, list(gs), out_shapes, n, name)


def _chip_scatter(cs, name):
    n = len(cs)

    def body(*refs):
        ins, outs = refs[:n], refs[n:2 * n]
        send_sems, recv_sems = refs[2 * n:]
        x, y, c = _mesh_pos()
        copies = []
        for i in range(n):
            for k, (px, py) in enumerate(_peer_chips(x, y)):
                copies.append(_remote(ins[i].at[2 * px + py], outs[i].at[k], send_sems, recv_sems, 3 * i + k, (px, py, c)))
        for cp in copies:
            cp.start()
        for cp in copies:
            cp.wait()

    out_shapes = [jax.ShapeDtypeStruct((3,) + a.shape[1:], a.dtype) for a in cs]
    return _comm_call(body, list(cs), out_shapes, 3 * n, name)


def _sibling_share(fs, name):
    n = len(fs)

    def body(*refs):
        ins, outs = refs[:n], refs[n:2 * n]
        send_sems, recv_sems, local_sems = refs[2 * n:]
        x, y, c = _mesh_pos()
        local = [pltpu.make_async_copy(ins[i], outs[i].at[c], local_sems.at[i]) for i in range(n)]
        copies = [_remote(ins[i], outs[i].at[c], send_sems, recv_sems, i, (x, y, 1 - c)) for i in range(n)]
        for cp in local + copies:
            cp.start()
        for i in range(n):
            theirs = outs[i].at[1 - c]
            _remote(theirs, theirs, send_sems, recv_sems, i, (x, y, c)).wait_recv()
        for cp in copies:
            cp.wait_send()
        for cp in local:
            cp.wait()

    out_shapes = [jax.ShapeDtypeStruct((2,) + a.shape, a.dtype) for a in fs]
    return _comm_call(body, list(fs), out_shapes, n, name, n_local=n)


def _allgather8(v, name):
    m = v.shape[0]

    def body(v_ref, out_ref, send_sems, recv_sems, local_sems):
        x, y, c = _mesh_pos()
        me, sib = (x, y, c), (x, y, 1 - c)
        chips = _peer_chips(x, y)
        rem = functools.partial(_remote, send_sems=send_sems, recv_sems=recv_sems)

        def blk(px, py, pc):
            return out_ref.at[4 * px + 2 * py + pc]

        mine = pltpu.make_async_copy(v_ref, blk(*me), local_sems.at[0])
        mine.start()
        first = [rem(v_ref, blk(*me), sem=0, to=sib)]
        first += [rem(v_ref, blk(*me), sem=1 + k, to=(px, py, c)) for k, (px, py) in enumerate(chips)]
        for cp in first:
            cp.start()
        passed = []
        for k, (px, py) in enumerate(chips):
            landed = blk(px, py, c)
            rem(landed, landed, sem=1 + k, to=me).wait_recv()
            fwd = rem(landed, landed, sem=4 + k, to=sib)
            fwd.start()
            passed.append(fwd)
        rem(blk(*sib), blk(*sib), sem=0, to=me).wait_recv()
        for k, (px, py) in enumerate(chips):
            theirs = blk(px, py, 1 - c)
            rem(theirs, theirs, sem=4 + k, to=me).wait_recv()
        for cp in first + passed:
            cp.wait_send()
        mine.wait()

    return _comm_call(body, [v], [jax.ShapeDtypeStruct((8, m, LANES), v.dtype)], 7, name, n_local=1)[0]


_WEIGHTS = ["norm1_g", "w_in", "conv_a_w", "conv_a_b", "ln_a_g", "ln_a_b", "ln_b_g", "ln_b_b", "w_spatial", "b_spatial",
            "conv_c_w", "conv_c_b", "dt_bias", "a_log", "d_skip", "norm_c_g", "w_out", "norm2_g", "w_ff1", "w_ff2", "final_g"]
_BIG = ["w_in", "w_out", "w_ff1", "w_ff2"]
_CONV_SHARDED = ["conv_a_w", "conv_c_w"]
_SMALL = [w for w in _WEIGHTS if w not in _BIG and w != "final_g"]
_PACK_ROWS = 512


def _pack(arrs):
    flat = jnp.concatenate([a.reshape(-1) for a in arrs])
    blk = _PACK_ROWS * LANES
    n = flat.shape[0]
    return jnp.pad(flat, (0, -(-n // blk) * blk - n)).reshape(-1, LANES)


def _unpack(packed, shapes):
    flat = packed.reshape(-1)
    out, off = [], 0
    for s in shapes:
        n = math.prod(s)
        out.append(flat[off:off + n].reshape(s))
        off += n
    return out


def _cols_to_chips(a):
    k = a.shape[0]
    return a.reshape(k, N_CHIPS, -1).transpose(1, 0, 2)


def _chips_to_cols(a):
    return a.transpose(1, 0, 2).reshape(a.shape[1], -1)


def _gathered_layer_params(w, li):
    big = [w[k][li].astype(BF16) for k in _BIG]
    small = [w[k][li] for k in _CONV_SHARDED]
    g_in, g_out, g_ff1, g_ff2, g_ca, g_cc = _gather_weights(big, small, f"l{li}_gather")
    p = {k: w[k][li] for k in _SMALL if k not in _CONV_SHARDED}
    w_in = _chips_to_cols(g_in)
    p["w_main"] = w_in[:, :D_MAIN]
    p["w_dt"] = jnp.pad(w_in[:, D_MAIN:], ((0, 0), (0, LANES - C_HEADS)))
    p["w_out"] = g_out.reshape(D_MIX, D_MODEL)
    p["w_ff1"] = _chips_to_cols(g_ff1)
    p["w_ff2"] = g_ff2.reshape(D_FF, D_MODEL)
    p["conv_a_w"] = _chips_to_cols(g_ca)
    p["conv_c_w"] = _chips_to_cols(g_cc)
    return p


def _reduce_scatter_layer(g, li, c, q):
    n = f"l{li}_rs_"
    g_in = jnp.concatenate([g["w_main"], g["w_dt"][:, :C_HEADS]], axis=1)
    full = [_cols_to_chips(g_in), g["w_out"].reshape(N_CHIPS, -1, D_MODEL), _cols_to_chips(g["w_ff1"]),
            g["w_ff2"].reshape(N_CHIPS, -1, D_MODEL)]
    from_sib = _sibling_other_halves(full, n + "sib")
    chip_f32, chip_bf16 = [], []
    for i, (a, b) in enumerate(zip(full, from_sib)):
        r2, cols = b.shape[1:]
        mine = lax.dynamic_slice_in_dim(a, c * r2, r2, axis=1)
        s32, s16 = _ew(lambda u, v: (u + v, u + v), [mine.reshape(-1, cols), b.reshape(-1, cols)], (F32, BF16),
                       n + f"chipsum{i}")
        chip_f32.append(lax.dynamic_index_in_dim(s32.reshape(b.shape), q, axis=0, keepdims=False))
        chip_bf16.append(s16.reshape(b.shape))
    from_chips = _chip_scatter(chip_bf16, n + "scatter")
    halves = [_ew(lambda o, r0, r1, r2_: (((o + r0) + r1) + r2_,), [own, rb, rb, rb], (F32,), n + f"final{i}",
                  leads=[None, 0, 1, 2])[0] for i, (own, rb) in enumerate(zip(chip_f32, from_chips))]
    shared = _sibling_share(halves, n + "share")
    return [s.reshape(-1, s.shape[-1]) for s in shared]


def kernel(x, norm1_g, w_in, conv_a_w, conv_a_b, ln_a_g, ln_a_b, ln_b_g, ln_b_b, w_spatial, b_spatial, conv_c_w, conv_c_b, dt_bias, a_log, d_skip, norm_c_g, w_out, norm2_g, w_ff1, w_ff2, final_g, loss_target, m_norm1_g, m_w_in, m_conv_a_w, m_conv_a_b, m_ln_a_g, m_ln_a_b, m_ln_b_g, m_ln_b_b, m_w_spatial, m_b_spatial, m_conv_c_w, m_conv_c_b, m_dt_bias, m_a_log, m_d_skip, m_norm_c_g, m_w_out, m_norm2_g, m_w_ff1, m_w_ff2, m_final_g, v_norm1_g, v_w_in, v_conv_a_w, v_conv_a_b, v_ln_a_g, v_ln_a_b, v_ln_b_g, v_ln_b_b, v_w_spatial, v_b_spatial, v_conv_c_w, v_conv_c_b, v_dt_bias, v_a_log, v_d_skip, v_norm_c_g, v_w_out, v_norm2_g, v_w_ff1, v_w_ff2, v_final_g):
    given = dict(locals())
    w = {k: given[k] for k in _WEIGHTS}
    m = {k: given["m_" + k] for k in _WEIGHTS}
    v = {k: given["v_" + k] for k in _WEIGHTS}
    depth = w_in.shape[0]
    nseq, seq, d = x.shape
    xi, yi, ci = _mesh_pos()
    q = 2 * xi + yi

    layer_params = [_gathered_layer_params(w, li) for li in range(depth)]
    loss, dx, grads, d_final = _step_local(x.reshape(nseq * seq, d), loss_target.reshape(nseq * seq, d), layer_params,
                                           final_g, seq)

    big_grads = {k: [] for k in _BIG}
    for li in range(depth):
        for k, g in zip(_BIG, _reduce_scatter_layer(grads[li], li, ci, q)):
            big_grads[k].append(g)
    grad_out, delta_out, m_out, v_out = {}, {}, {}, {}
    for k in _BIG:
        grad_out[k] = jnp.stack(big_grads[k])
        delta_out[k], m_out[k], v_out[k] = _adam(w[k], grad_out[k], m[k], v[k], "adam_" + k)

    small_shapes = [grads[0][k].shape for k in _SMALL]
    parts = [grads[li][k] for li in range(depth) for k in _SMALL] + [d_final, loss.reshape(1)]
    gathered = _allgather8(_pack(parts), "small_allgather")

    def sum8(*blocks):
        acc = blocks[0]
        for b in blocks[1:]:
            acc = acc + b
        return (acc,)

    total = _ew(sum8, [gathered] * 8, (F32,), "small_sum", leads=list(range(8)))[0]
    summed = _unpack(total, small_shapes * depth + [d_final.shape, (1,)])
    loss_total = summed[-1][0]
    small_grads = {k: jnp.stack([summed[li * len(_SMALL) + i] for li in range(depth)]) for i, k in enumerate(_SMALL)}
    small_grads["final_g"] = summed[-2]
    for k in _CONV_SHARDED:
        n_shard = w[k].shape[-1]
        small_grads[k] = lax.dynamic_slice_in_dim(small_grads[k], q * n_shard, n_shard, axis=2)
    names = _SMALL + ["final_g"]
    shapes = [w[k].shape for k in names]
    packed = [_pack([src[k] for k in names]) for src in (w, small_grads, m, v)]
    outs = _ew(_adam_fn, packed, (F32, F32, F32), "adam_small")
    for dst, o in zip((delta_out, m_out, v_out), outs):
        for k, a in zip(names, _unpack(o, shapes)):
            dst[k] = a
    for k in names:
        grad_out[k] = small_grads[k]

    return (loss_total, dx.reshape(nseq, seq, d), *[grad_out[k] for k in _WEIGHTS], *[delta_out[k] for k in _WEIGHTS],
            *[m_out[k] for k in _WEIGHTS], *[v_out[k] for k in _WEIGHTS])
```

```python
import functools
import math

import jax
import jax.numpy as jnp
from jax import lax
from jax.experimental import pallas as pl
from jax.experimental.pallas import tpu as pltpu

F32 = jnp.float32
BF16 = jnp.bfloat16
MESH = pl.DeviceIdType.MESH

D_MODEL = 1024
DEPTH = 4
HEAD_DIM = 64
A_WIDTH = 512
B_WIDTH = 512
C_WIDTH = 1024
C_HEADS = 16
CONV_A_K = 31
CONV_C_K = 4
CHUNK = 128
SSM_STATE = 128
D_CONV_C = 1536
D_MAIN = 4608
D_IN_PROJ = 4624
D_MIX = 2048
D_FF = 4096
EPS = 1e-5
NEG = -1e30
LANES = 128
CONV_PAD = 32
N_CHIPS = 4

ADAM_LR = 0.001
ADAM_B1 = 0.9
ADAM_B2 = 0.999
ADAM_EPS = 1e-08
ADAM_WD = 0.01
ADAM_STEP = 10

VMEM_LIMIT = 56 * 1024 * 1024

COL_AVAL, COL_AGATE, COL_BU, COL_BV, COL_Z, COL_XBC = 0, 4, 8, 12, 16, 24


def _cparams(*sem):
    return pltpu.CompilerParams(dimension_semantics=sem, vmem_limit_bytes=VMEM_LIMIT)


_DN = {"nn": (((1,), (0,)), ((), ())), "nt": (((1,), (1,)), ((), ())), "tn": (((0,), (0,)), ((), ()))}


def _dot_raw(a, b, mode):
    return lax.dot_general(a.astype(BF16), b.astype(BF16), _DN[mode], preferred_element_type=F32)


def _make_dot(mode):
    @jax.custom_vjp
    def f(a, b):
        return _dot_raw(a, b, mode)

    def fwd(a, b):
        return _dot_raw(a, b, mode), (a, b)

    def bwd(res, g):
        a, b = res
        if mode == "nn":
            return _dot_raw(g, b, "nt"), _dot_raw(a, g, "tn")
        if mode == "nt":
            return _dot_raw(g, b, "nn"), _dot_raw(g, a, "tn")
        return _dot_raw(b, g, "nt"), _dot_raw(a, g, "nn")

    f.defvjp(fwd, bwd)
    return f


_nn = _make_dot("nn")
_nt = _make_dot("nt")
_tn = _make_dot("tn")


def _xdot(a, e):
    return jnp.dot(a, e, precision=lax.Precision.HIGHEST, preferred_element_type=F32)


def _iota2(shape, dim):
    return lax.broadcasted_iota(jnp.int32, shape, dim)


def _gmean_impl(x):
    n = x.shape[-1]
    same = (_iota2((n, n), 0) < HEAD_DIM) == (_iota2((n, n), 1) < HEAD_DIM)
    p = jnp.where(same, 1.0 / HEAD_DIM, 0.0).astype(BF16)
    hi = x.astype(BF16)
    lo = (x - hi.astype(F32)).astype(BF16)
    dn = _DN["nn"]
    return (lax.dot_general(hi, p, dn, preferred_element_type=F32)
            + lax.dot_general(lo, p, dn, preferred_element_type=F32))


@jax.custom_vjp
def _gmean(x):
    return _gmean_impl(x)


_gmean.defvjp(lambda x: (_gmean_impl(x), None), lambda _, g: (_gmean_impl(g),))


def _sigmoid(x):
    return 1.0 / (1.0 + jnp.exp(-x))


def _silu(x):
    return x * _sigmoid(x)


def _gelu(x):
    return 0.5 * x * (1.0 + lax.erf(x * 0.7071067811865476))


def _softplus(x):
    return jnp.maximum(x, 0.0) + jnp.log(1.0 + jnp.exp(-jnp.abs(x)))


def _rms(x, g):
    return x * lax.rsqrt(jnp.mean(x * x, axis=-1, keepdims=True) + EPS) * g


def _ln64(x, g, b):
    mu = _gmean(x)
    xc = x - mu
    var = _gmean(xc * xc)
    return xc * lax.rsqrt(var + EPS) * g + b


def _lane_lt64(shape):
    return _iota2(shape, 1) < HEAD_DIM


def _pick(n, pref):
    for t in pref:
        if n % t == 0:
            return t
    return n


def _matmul(a, b, *, mode, name, add=None, epilogue=None, extra=None, out_dtypes=(F32,)):
    if mode == "nn":
        (m, k), (_, n) = a.shape, b.shape
    elif mode == "nt":
        (m, k), (n, _) = a.shape, b.shape
    else:
        (k, m), (_, n) = a.shape, b.shape
    tm = _pick(m, (1024, 512, 256, 128))
    tn = _pick(n, (1536, 1024, 512, 256, 128))
    tk = _pick(k, (1536, 1024, 512, 256, 128))
    nk = k // tk
    a_spec = {"nn": pl.BlockSpec((tm, tk), lambda i, j, kk: (i, kk)),
              "nt": pl.BlockSpec((tm, tk), lambda i, j, kk: (i, kk)),
              "tn": pl.BlockSpec((tk, tm), lambda i, j, kk: (kk, i))}[mode]
    b_spec = {"nn": pl.BlockSpec((tk, tn), lambda i, j, kk: (kk, j)),
              "nt": pl.BlockSpec((tn, tk), lambda i, j, kk: (j, kk)),
              "tn": pl.BlockSpec((tk, tn), lambda i, j, kk: (kk, j))}[mode]
    o_spec = pl.BlockSpec((tm, tn), lambda i, j, kk: (i, j))
    ins = [a, b]
    in_specs = [a_spec, b_spec]
    if add is not None:
        ins.append(add)
        in_specs.append(o_spec)
    if extra is not None:
        ins.append(extra)
        in_specs.append(o_spec)
    n_out = len(out_dtypes)

    def body(*refs):
        a_ref, b_ref = refs[0], refs[1]
        pos = 2
        add_ref = ex_ref = None
        if add is not None:
            add_ref = refs[pos]
            pos += 1
        if extra is not None:
            ex_ref = refs[pos]
            pos += 1
        o_refs = refs[pos:pos + n_out]

        def finish(acc):
            if add_ref is not None:
                acc = acc + add_ref[...].astype(F32)
            outs = (acc,) if epilogue is None else epilogue(acc, None if ex_ref is None else ex_ref[...])
            for o_ref, o in zip(o_refs, outs):
                o_ref[...] = o.astype(o_ref.dtype)

        part = _dot_raw(a_ref[...], b_ref[...], mode)
        if nk == 1:
            finish(part)
            return
        acc_ref = refs[pos + n_out]
        kk = pl.program_id(2)

        @pl.when(kk == 0)
        def _():
            acc_ref[...] = part

        @pl.when(jnp.logical_and(kk > 0, kk < nk - 1))
        def _():
            acc_ref[...] += part

        @pl.when(kk == nk - 1)
        def _():
            finish(acc_ref[...] + part)

    res = pl.pallas_call(
        body, name=name, grid=(m // tm, n // tn, nk),
        in_specs=in_specs, out_specs=[o_spec] * n_out,
        out_shape=[jax.ShapeDtypeStruct((m, n), dt) for dt in out_dtypes],
        scratch_shapes=[pltpu.VMEM((tm, tn), F32)] if nk > 1 else [],
        compiler_params=_cparams("parallel", "parallel", "arbitrary"),
    )(*ins)
    return res[0] if n_out == 1 else res


def _relu2_epilogue(acc, _):
    r = jnp.maximum(acc, 0.0)
    return acc, r * r


def _relu2_bwd_epilogue(acc, u):
    return (acc * (2.0 * jnp.maximum(u, 0.0)),)


def _row_tile(t):
    return _pick(t, (512, 256, 128))


def _rms_fwd(x, g, name):
    t, d = x.shape
    tm = _row_tile(t)

    def body(x_ref, g_ref, o_ref):
        o_ref[...] = _rms(x_ref[...], g_ref[...]).astype(BF16)

    return pl.pallas_call(
        body, name=name, grid=(t // tm,),
        in_specs=[pl.BlockSpec((tm, d), lambda i: (i, 0)), pl.BlockSpec((1, d), lambda i: (0, 0))],
        out_specs=pl.BlockSpec((tm, d), lambda i: (i, 0)),
        out_shape=jax.ShapeDtypeStruct((t, d), BF16),
        compiler_params=_cparams("parallel"),
    )(x, g.reshape(1, d))


def _rms_bwd(x, g, dh, dres, name):
    t, d = x.shape
    tm = _row_tile(t)

    def body(x_ref, g_ref, dh_ref, dres_ref, dx_ref, dg_ref):
        @pl.when(pl.program_id(0) == 0)
        def _():
            dg_ref[...] = jnp.zeros_like(dg_ref)

        _, vjp = jax.vjp(_rms, x_ref[...], g_ref[...])
        dx, dg = vjp(dh_ref[...].astype(F32))
        dx_ref[...] = dx + dres_ref[...]
        dg_ref[...] += dg

    row = pl.BlockSpec((tm, d), lambda i: (i, 0))
    vec = pl.BlockSpec((1, d), lambda i: (0, 0))
    dx, dg = pl.pallas_call(
        body, name=name, grid=(t // tm,),
        in_specs=[row, vec, row, row], out_specs=[row, vec],
        out_shape=[jax.ShapeDtypeStruct((t, d), F32), jax.ShapeDtypeStruct((1, d), F32)],
        compiler_params=_cparams("arbitrary"),
    )(x, g.reshape(1, d), dh, dres)
    return dx, dg.reshape(d)


def _loss_head(x, g, target):
    t, d = x.shape
    tm = _row_tile(t)

    def loss_fn(xv, gv, tv):
        err = _rms(xv, gv) - tv
        return 0.5 * jnp.sum(jnp.mean(err * err, axis=-1, keepdims=True))

    def body(x_ref, g_ref, t_ref, loss_ref, dx_ref, dg_ref):
        @pl.when(pl.program_id(0) == 0)
        def _():
            dg_ref[...] = jnp.zeros_like(dg_ref)
            loss_ref[...] = jnp.zeros_like(loss_ref)

        tv = t_ref[...]
        val, vjp = jax.vjp(lambda xv, gv: loss_fn(xv, gv, tv), x_ref[...], g_ref[...])
        dx, dg = vjp(jnp.ones((), F32))
        dx_ref[...] = dx
        dg_ref[...] += dg
        loss_ref[...] += jnp.full(loss_ref.shape, val, F32)

    row = pl.BlockSpec((tm, d), lambda i: (i, 0))
    vec = pl.BlockSpec((1, d), lambda i: (0, 0))
    loss, dx, dg = pl.pallas_call(
        body, name="loss_head", grid=(t // tm,),
        in_specs=[row, vec, row], out_specs=[pl.BlockSpec((1, LANES), lambda i: (0, 0)), row, vec],
        out_shape=[jax.ShapeDtypeStruct((1, LANES), F32), jax.ShapeDtypeStruct((t, d), F32),
                   jax.ShapeDtypeStruct((1, d), F32)],
        compiler_params=_cparams("arbitrary"),
    )(x, g.reshape(1, d), target)
    return loss[0, 0], dx, dg.reshape(d)


def _pre_glu(val, gate):
    return val * _sigmoid(gate)


def _pre_id(x):
    return x


def _post_lnsilu(c, g, b):
    return _silu(_ln64(c, g, b))


def _post_silu(c):
    return _silu(c)


def _conv_cfg(kind):
    if kind == "a":
        return dict(k=CONV_A_K, pre=_pre_glu, post=_post_lnsilu, n_in=2, n_par=2, nblk=A_WIDTH // LANES,
                    cols=(COL_AVAL, COL_AGATE))
    return dict(k=CONV_C_K, pre=_pre_id, post=_post_silu, n_in=1, n_par=0, nblk=D_CONV_C // LANES,
                cols=(COL_XBC,))


def _conv_fwd(kind, proj, w, bias, params, seq, name):
    cfg = _conv_cfg(kind)
    kt, pre, post, n_in = cfg["k"], cfg["pre"], cfg["post"], cfg["n_in"]
    t = proj.shape[0]
    nseq = t // seq
    c = cfg["nblk"] * LANES
    rt = min(256, seq)
    nrt = seq // rt
    off0 = CONV_PAD - (kt - 1)

    def body(*refs):
        in_refs = refs[:n_in]
        w_ref, b_ref = refs[n_in], refs[n_in + 1]
        par_refs = refs[n_in + 2:n_in + 2 + cfg["n_par"]]
        o_ref, hpad = refs[n_in + 2 + cfg["n_par"]:]
        hpad[pl.ds(0, CONV_PAD), :] = jnp.zeros((CONV_PAD, LANES), F32)
        for r in range(nrt):
            hpad[pl.ds(CONV_PAD + r * rt, rt), :] = pre(*[x[pl.ds(r * rt, rt), :] for x in in_refs])
        pars = [p[...] for p in par_refs]
        for r in range(nrt):
            acc = jnp.broadcast_to(b_ref[...], (rt, LANES))
            for k in range(kt):
                acc = acc + w_ref[pl.ds(k, 1), :] * hpad[pl.ds(off0 + k + r * rt, rt), :]
            o_ref[pl.ds(r * rt, rt), :] = post(acc, *pars)

    in_specs = [pl.BlockSpec((seq, LANES), functools.partial(lambda s, j, col: (s, col + j), col=col))
                for col in cfg["cols"]]
    vec = pl.BlockSpec((1, LANES), lambda s, j: (0, j))
    in_specs += [pl.BlockSpec((CONV_PAD, LANES), lambda s, j: (0, j)), vec] + [vec] * cfg["n_par"]
    return pl.pallas_call(
        body, name=name, grid=(nseq, cfg["nblk"]),
        in_specs=in_specs, out_specs=pl.BlockSpec((seq, LANES), lambda s, j: (s, j)),
        out_shape=jax.ShapeDtypeStruct((t, c), F32),
        scratch_shapes=[pltpu.VMEM((seq + CONV_PAD, LANES), F32)],
        compiler_params=_cparams("parallel", "parallel"),
    )(*([proj] * n_in), w, bias, *params)


def _conv_bwd(kind, proj, w, bias, params, dy, seq, name):
    cfg = _conv_cfg(kind)
    kt, pre, post, n_in, n_par = cfg["k"], cfg["pre"], cfg["post"], cfg["n_in"], cfg["n_par"]
    t = proj.shape[0]
    nseq = t // seq
    c = cfg["nblk"] * LANES
    rt = min(256, seq)
    nrt = seq // rt
    off0 = CONV_PAD - (kt - 1)

    def body(*refs):
        in_refs = refs[:n_in]
        w_ref, b_ref = refs[n_in], refs[n_in + 1]
        par_refs = refs[n_in + 2:n_in + 2 + n_par]
        pos = n_in + 2 + n_par
        dy_ref = refs[pos]
        din_refs = refs[pos + 1:pos + 1 + n_in]
        dw_ref, db_ref = refs[pos + 1 + n_in], refs[pos + 2 + n_in]
        dpar_refs = refs[pos + 3 + n_in:pos + 3 + n_in + n_par]
        hpad, dcpad = refs[pos + 3 + n_in + n_par:]

        @pl.when(pl.program_id(1) == 0)
        def _():
            dw_ref[...] = jnp.zeros_like(dw_ref)
            db_ref[...] = jnp.zeros_like(db_ref)
            for r in dpar_refs:
                r[...] = jnp.zeros_like(r)

        hpad[pl.ds(0, CONV_PAD), :] = jnp.zeros((CONV_PAD, LANES), F32)
        dcpad[pl.ds(seq, CONV_PAD), :] = jnp.zeros((CONV_PAD, LANES), F32)
        for r in range(nrt):
            hpad[pl.ds(CONV_PAD + r * rt, rt), :] = pre(*[x[pl.ds(r * rt, rt), :] for x in in_refs])
        pars = [p[...] for p in par_refs]
        for r in range(nrt):
            acc = jnp.broadcast_to(b_ref[...], (rt, LANES))
            for k in range(kt):
                acc = acc + w_ref[pl.ds(k, 1), :] * hpad[pl.ds(off0 + k + r * rt, rt), :]
            _, vjp = jax.vjp(post, acc, *pars)
            grads = vjp(dy_ref[pl.ds(r * rt, rt), :])
            dcpad[pl.ds(r * rt, rt), :] = grads[0]
            db_ref[...] += jnp.sum(grads[0], axis=0, keepdims=True)
            for ref, gpar in zip(dpar_refs, grads[1:]):
                ref[...] += gpar
        for r in range(nrt):
            dh = jnp.zeros((rt, LANES), F32)
            for k in range(kt):
                dh = dh + w_ref[pl.ds(k, 1), :] * dcpad[pl.ds(r * rt + kt - 1 - k, rt), :]
            _, vjp = jax.vjp(pre, *[x[pl.ds(r * rt, rt), :] for x in in_refs])
            for ref, gin in zip(din_refs, vjp(dh)):
                ref[pl.ds(r * rt, rt), :] = gin.astype(ref.dtype)
        for k in range(kt):
            s = jnp.zeros((1, LANES), F32)
            for r in range(nrt):
                s = s + jnp.sum(dcpad[pl.ds(r * rt, rt), :] * hpad[pl.ds(off0 + k + r * rt, rt), :],
                                axis=0, keepdims=True)
            dw_ref[pl.ds(k, 1), :] += s

    in_specs = [pl.BlockSpec((seq, LANES), functools.partial(lambda j, s, col: (s, col + j), col=col))
                for col in cfg["cols"]]
    vec = pl.BlockSpec((1, LANES), lambda j, s: (0, j))
    wspec = pl.BlockSpec((CONV_PAD, LANES), lambda j, s: (0, j))
    blk = pl.BlockSpec((seq, LANES), lambda j, s: (s, j))
    in_specs += [wspec, vec] + [vec] * n_par + [blk]
    out_specs = [blk] * n_in + [wspec, vec] + [vec] * n_par
    out_shape = ([jax.ShapeDtypeStruct((t, c), BF16)] * n_in
                 + [jax.ShapeDtypeStruct((CONV_PAD, c), F32), jax.ShapeDtypeStruct((1, c), F32)]
                 + [jax.ShapeDtypeStruct((1, c), F32)] * n_par)
    res = pl.pallas_call(
        body, name=name, grid=(cfg["nblk"], nseq),
        in_specs=in_specs, out_specs=out_specs, out_shape=out_shape,
        scratch_shapes=[pltpu.VMEM((seq + CONV_PAD, LANES), F32), pltpu.VMEM((seq + CONV_PAD, LANES), F32)],
        compiler_params=_cparams("parallel", "arbitrary"),
    )(*([proj] * n_in), w, bias, *params, dy)
    return res[:n_in], res[n_in], res[n_in + 1], res[n_in + 2:]


def _gmlp_chunk(bu, bv, g, b, w0, w1, b0row, b1row):
    u = _gelu(bu)
    vn = _ln64(_gelu(bv), g, b)
    tri = _iota2((CHUNK, CHUNK), 0) >= _iota2((CHUNK, CHUNK), 1)
    m0 = _nn(jnp.where(tri, w0, 0.0), vn) + jnp.broadcast_to(b0row, (CHUNK, CHUNK)).T
    m1 = _nn(jnp.where(tri, w1, 0.0), vn) + jnp.broadcast_to(b1row, (CHUNK, CHUNK)).T
    return u * jnp.where(_lane_lt64((CHUNK, LANES)), m0, m1)


def _gmlp_specs(tm, order):
    def im(f):
        return lambda *ids: f(*order(*ids))
    return dict(
        bu=pl.BlockSpec((tm, LANES), im(lambda j, r: (r, COL_BU + j))),
        bv=pl.BlockSpec((tm, LANES), im(lambda j, r: (r, COL_BV + j))),
        vec=pl.BlockSpec((1, LANES), im(lambda j, r: (0, j))),
        ws=pl.BlockSpec((2, CHUNK, CHUNK), im(lambda j, r: (j, 0, 0))),
        bs=pl.BlockSpec((None, 2, CHUNK), im(lambda j, r: (j, 0, 0))),
        blk=pl.BlockSpec((tm, LANES), im(lambda j, r: (r, j))),
    )


def _gmlp_fwd(proj, ln_g, ln_b, w_s, b_s, name):
    t = proj.shape[0]
    tm = _row_tile(t)
    nch = tm // CHUNK
    sp = _gmlp_specs(tm, lambda r, j: (j, r))

    def body(bu_ref, bv_ref, g_ref, b_ref, ws_ref, bs_ref, o_ref):
        for ci in range(nch):
            rows = pl.ds(ci * CHUNK, CHUNK)
            o_ref[rows, :] = _gmlp_chunk(bu_ref[rows, :], bv_ref[rows, :], g_ref[...], b_ref[...],
                                         ws_ref[0], ws_ref[1], bs_ref[pl.ds(0, 1), :], bs_ref[pl.ds(1, 1), :])

    return pl.pallas_call(
        body, name=name, grid=(t // tm, B_WIDTH // LANES),
        in_specs=[sp["bu"], sp["bv"], sp["vec"], sp["vec"], sp["ws"], sp["bs"]],
        out_specs=sp["blk"], out_shape=jax.ShapeDtypeStruct((t, B_WIDTH), F32),
        compiler_params=_cparams("parallel", "parallel"),
    )(proj, proj, ln_g, ln_b, w_s, b_s.reshape(B_WIDTH // LANES, 2, CHUNK))


def _gmlp_bwd(proj, ln_g, ln_b, w_s, b_s, dy, name):
    t = proj.shape[0]
    tm = _row_tile(t)
    nch = tm // CHUNK
    sp = _gmlp_specs(tm, lambda j, r: (j, r))

    def body(bu_ref, bv_ref, g_ref, b_ref, ws_ref, bs_ref, dy_ref, dbu_ref, dbv_ref, dg_ref, db_ref, dws_ref, dbs_ref):
        @pl.when(pl.program_id(1) == 0)
        def _():
            for r in (dg_ref, db_ref, dws_ref, dbs_ref):
                r[...] = jnp.zeros_like(r)

        for ci in range(nch):
            rows = pl.ds(ci * CHUNK, CHUNK)
            _, vjp = jax.vjp(_gmlp_chunk, bu_ref[rows, :], bv_ref[rows, :], g_ref[...], b_ref[...],
                             ws_ref[0], ws_ref[1], bs_ref[pl.ds(0, 1), :], bs_ref[pl.ds(1, 1), :])
            dbu, dbv, dg, db, dw0, dw1, db0, db1 = vjp(dy_ref[rows, :])
            dbu_ref[rows, :] = dbu.astype(BF16)
            dbv_ref[rows, :] = dbv.astype(BF16)
            dg_ref[...] += dg
            db_ref[...] += db
            dws_ref[0] += dw0
            dws_ref[1] += dw1
            dbs_ref[pl.ds(0, 1), :] += db0
            dbs_ref[pl.ds(1, 1), :] += db1

    nh = B_WIDTH // LANES
    res = pl.pallas_call(
        body, name=name, grid=(nh, t // tm),
        in_specs=[sp["bu"], sp["bv"], sp["vec"], sp["vec"], sp["ws"], sp["bs"], sp["blk"]],
        out_specs=[sp["blk"], sp["blk"], sp["vec"], sp["vec"], sp["ws"], sp["bs"]],
        out_shape=[jax.ShapeDtypeStruct((t, B_WIDTH), BF16), jax.ShapeDtypeStruct((t, B_WIDTH), BF16),
                   jax.ShapeDtypeStruct((1, B_WIDTH), F32), jax.ShapeDtypeStruct((1, B_WIDTH), F32),
                   jax.ShapeDtypeStruct(w_s.shape, F32), jax.ShapeDtypeStruct((nh, 2, CHUNK), F32)],
        compiler_params=_cparams("parallel", "arbitrary"),
    )(proj, proj, ln_g, ln_b, w_s, b_s.reshape(nh, 2, CHUNK), dy)
    dbu, dbv, dg, db, dws, dbs = res
    return dbu, dbv, dg, db, dws, dbs.reshape(b_s.shape)


def _tri_apply(a, lower):
    l = a.shape[0]
    r, c = _iota2((l, l), 0), _iota2((l, l), 1)
    t = jnp.where((r >= c) if lower else (r <= c), 1.0, 0.0).astype(BF16)
    hi = a.astype(BF16)
    r1 = a - hi.astype(F32)
    mid = r1.astype(BF16)
    lo = (r1 - mid.astype(F32)).astype(BF16)
    dn = _DN["nn"]
    return (lax.dot_general(t, hi, dn, preferred_element_type=F32) + lax.dot_general(t, mid, dn, preferred_element_type=F32)
            + lax.dot_general(t, lo, dn, preferred_element_type=F32))


@jax.custom_vjp
def _cumsum_rows(a):
    return _tri_apply(a, True)


_cumsum_rows.defvjp(lambda a: (_tri_apply(a, True), None), lambda _, g: (_tri_apply(g, False),))

SSD_GROUP_HEADS = 8
SSD_GROUP_PAIRS = 4


def _ssd_group(x0, x1, x2, x3, dt_raw, bias, alog, bm, cm, p0, p1, p2, p3):
    xs, prevs = (x0, x1, x2, x3), (p0, p1, p2, p3)
    dt = _softplus(dt_raw + bias)
    a = dt * (-jnp.exp(alog))
    acs = _cumsum_rows(a)
    alast = jnp.sum(a, axis=0, keepdims=True)
    dt_t, acs_t = dt.T, acs.T
    cb = _nt(cm, bm)
    tri = _iota2((CHUNK, CHUNK), 0) >= _iota2((CHUNK, CHUNK), 1)
    lane = _iota2((CHUNK, LANES), 1)
    sub = _iota2((LANES, CHUNK), 0)
    lane1 = _iota2((1, LANES), 1)

    def column(v, i):
        return jnp.broadcast_to(jnp.sum(jnp.where(lane == i, v, 0.0), axis=1, keepdims=True), (CHUNK, LANES))

    def row(vt, i):
        return jnp.broadcast_to(jnp.sum(jnp.where(sub == i, vt, 0.0), axis=0, keepdims=True), (CHUNK, CHUNK))

    heads = []
    for i in range(SSD_GROUP_HEADS):
        col_a = column(acs, i)
        al = jnp.sum(jnp.where(lane1 == i, alast, 0.0), axis=1, keepdims=True)
        m = cb * jnp.exp(jnp.where(tri, col_a - row(acs_t, i), NEG)) * row(dt_t, i)
        heads.append((m, jnp.exp(col_a), column(dt, i) * jnp.exp(al - col_a), jnp.exp(al)))
    lo_lanes = _lane_lt64((CHUNK, LANES))
    lo_rows = _iota2((LANES, SSM_STATE), 0) < HEAD_DIM
    ys, news = [], []
    for j in range(SSD_GROUP_PAIRS):
        (m0, ea0, w0, cd0), (m1, ea1, w1, cd1) = heads[2 * j], heads[2 * j + 1]
        x, prev = xs[j], prevs[j]
        ydiag = jnp.where(lo_lanes, _nn(m0, x), _nn(m1, x))
        yoff = jnp.where(lo_lanes, _nt(cm * ea0, prev), _nt(cm * ea1, prev))
        states = jnp.where(lo_rows, _tn(x, bm * w0), _tn(x, bm * w1))
        ys.append(ydiag + yoff)
        news.append(prev * jnp.where(lo_rows, cd0, cd1) + states)
    return tuple(ys) + tuple(news)


def _ssd2_specs(seq, rev):
    ncs = seq // CHUNK
    gw = SSD_GROUP_PAIRS * LANES
    nblk_x = C_WIDTH // LANES

    def row(s, c):
        return s * ncs + (ncs - 1 - c if rev else c)

    return dict(
        x=pl.BlockSpec((CHUNK, gw), lambda g, s, c: (row(s, c), g)),
        dt=pl.BlockSpec((CHUNK, LANES), lambda g, s, c: (row(s, c), g)),
        vec=pl.BlockSpec((1, LANES), lambda g, s, c: (0, g)),
        bm=pl.BlockSpec((CHUNK, SSM_STATE), lambda g, s, c: (row(s, c), nblk_x + g)),
        cm=pl.BlockSpec((CHUNK, SSM_STATE), lambda g, s, c: (row(s, c), nblk_x + 2 + g)),
        st=pl.BlockSpec((None, SSD_GROUP_PAIRS, LANES, SSM_STATE), lambda g, s, c: (row(s, c), g, 0, 0)),
        ncs=ncs,
    )


def _lane_blocks(ref):
    return [ref[:, pl.ds(j * LANES, LANES)] for j in range(SSD_GROUP_PAIRS)]


def _ssd2_fwd(xbc_act, dt_raw, dt_bias, a_log, seq, name):
    t = xbc_act.shape[0]
    sp = _ssd2_specs(seq, False)

    def body(x_ref, dt_ref, bias_ref, alog_ref, bm_ref, cm_ref, y_ref, prev_ref, state):
        @pl.when(pl.program_id(2) == 0)
        def _():
            state[...] = jnp.zeros_like(state)

        prevs = [state[j] for j in range(SSD_GROUP_PAIRS)]
        for j in range(SSD_GROUP_PAIRS):
            prev_ref[j] = prevs[j]
        res = _ssd_group(*_lane_blocks(x_ref), dt_ref[...], bias_ref[...], alog_ref[...], bm_ref[...], cm_ref[...], *prevs)
        for j in range(SSD_GROUP_PAIRS):
            y_ref[:, pl.ds(j * LANES, LANES)] = res[j]
            state[j] = res[SSD_GROUP_PAIRS + j]

    return pl.pallas_call(
        body, name=name, grid=(2, t // seq, sp["ncs"]),
        in_specs=[sp["x"], sp["dt"], sp["vec"], sp["vec"], sp["bm"], sp["cm"]],
        out_specs=[sp["x"], sp["st"]],
        out_shape=[jax.ShapeDtypeStruct((t, C_WIDTH), F32),
                   jax.ShapeDtypeStruct((t // CHUNK, C_WIDTH // LANES, LANES, SSM_STATE), F32)],
        scratch_shapes=[pltpu.VMEM((SSD_GROUP_PAIRS, LANES, SSM_STATE), F32)],
        compiler_params=_cparams("parallel", "parallel", "arbitrary"),
    )(xbc_act, dt_raw, dt_bias, a_log, xbc_act, xbc_act)


def _ssd2_bwd(xbc_act, dt_raw, dt_bias, a_log, prev_saved, dy, seq, name):
    t = xbc_act.shape[0]
    sp = _ssd2_specs(seq, True)
    npair = SSD_GROUP_PAIRS

    def body(x_ref, dt_ref, bias_ref, alog_ref, bm_ref, cm_ref, prev_ref, dy_ref,
             dx_ref, ddt_ref, dbias_ref, dalog_ref, dbm_ref, dcm_ref, dstate):
        @pl.when(pl.program_id(2) == 0)
        def _():
            dstate[...] = jnp.zeros_like(dstate)

        @pl.when(jnp.logical_and(pl.program_id(1) == 0, pl.program_id(2) == 0))
        def _():
            dbias_ref[...] = jnp.zeros_like(dbias_ref)
            dalog_ref[...] = jnp.zeros_like(dalog_ref)

        _, vjp = jax.vjp(_ssd_group, *_lane_blocks(x_ref), dt_ref[...], bias_ref[...], alog_ref[...], bm_ref[...],
                         cm_ref[...], *[prev_ref[j] for j in range(npair)])
        grads = vjp(tuple(_lane_blocks(dy_ref)) + tuple(dstate[j] for j in range(npair)))
        for j in range(npair):
            dx_ref[:, pl.ds(j * LANES, LANES)] = grads[j]
            dstate[j] = grads[npair + 5 + j]
        ddt_ref[...] = grads[npair].astype(BF16)
        dbias_ref[...] += grads[npair + 1]
        dalog_ref[...] += grads[npair + 2]
        dbm_ref[...] = grads[npair + 3]
        dcm_ref[...] = grads[npair + 4]

    return pl.pallas_call(
        body, name=name, grid=(2, t // seq, sp["ncs"]),
        in_specs=[sp["x"], sp["dt"], sp["vec"], sp["vec"], sp["bm"], sp["cm"], sp["st"], sp["x"]],
        out_specs=[sp["x"], sp["dt"], sp["vec"], sp["vec"], sp["dt"], sp["dt"]],
        out_shape=[jax.ShapeDtypeStruct((t, C_WIDTH), F32), jax.ShapeDtypeStruct((t, 2 * LANES), BF16),
                   jax.ShapeDtypeStruct((1, 2 * LANES), F32), jax.ShapeDtypeStruct((1, 2 * LANES), F32),
                   jax.ShapeDtypeStruct((t, 2 * SSM_STATE), F32), jax.ShapeDtypeStruct((t, 2 * SSM_STATE), F32)],
        scratch_shapes=[pltpu.VMEM((npair, LANES, SSM_STATE), F32)],
        compiler_params=_cparams("parallel", "arbitrary", "arbitrary"),
    )(xbc_act, dt_raw, dt_bias, a_log, xbc_act, xbc_act, prev_saved, dy)


def _ssd2_assemble(dxs_ssd, dxs_skip, dbm, dcm, name):
    t = dxs_ssd.shape[0]
    tm = _row_tile(t)

    def body(a_ref, b_ref, dbm_ref, dcm_ref, o_ref):
        o_ref[:, pl.ds(0, C_WIDTH)] = a_ref[...] + b_ref[...]
        o_ref[:, pl.ds(C_WIDTH, 2 * SSM_STATE)] = dbm_ref[...]
        o_ref[:, pl.ds(C_WIDTH + 2 * SSM_STATE, 2 * SSM_STATE)] = dcm_ref[...]

    wide = pl.BlockSpec((tm, C_WIDTH), lambda i: (i, 0))
    narrow = pl.BlockSpec((tm, 2 * SSM_STATE), lambda i: (i, 0))
    return pl.pallas_call(
        body, name=name, grid=(t // tm,), in_specs=[wide, wide, narrow, narrow],
        out_specs=pl.BlockSpec((tm, D_CONV_C), lambda i: (i, 0)),
        out_shape=jax.ShapeDtypeStruct((t, D_CONV_C), F32),
        compiler_params=_cparams("parallel"),
    )(dxs_ssd, dxs_skip, dbm, dcm)


def _expand_mats():
    head = jnp.arange(LANES)[:, None]
    e64 = (head == (jnp.arange(C_WIDTH)[None, :] // HEAD_DIM)).astype(F32)
    e128 = (head == (jnp.arange(C_HEADS * LANES)[None, :] // LANES)).astype(F32)
    return e64, e128


def _ssd_prep_fn(dt_raw, dt_bias, a_log, e64, e128):
    dt = _softplus(dt_raw + dt_bias)
    a = dt * (-jnp.exp(a_log))
    incl = (_iota2((CHUNK, CHUNK), 0) >= _iota2((CHUNK, CHUNK), 1)).astype(F32)
    acs = _xdot(incl, a)
    alast = _xdot(jnp.ones((CHUNK, CHUNK), F32), a)
    return _xdot(dt, e64), _xdot(acs, e64), _xdot(alast, e64), _xdot(acs, e128)


def _ssd_prep_specs():
    blk = lambda w: pl.BlockSpec((CHUNK, w), lambda i: (i, 0))
    const = lambda r, w: pl.BlockSpec((r, w), lambda i: (0, 0))
    ins = [blk(LANES), const(1, LANES), const(1, LANES), const(LANES, C_WIDTH), const(LANES, C_HEADS * LANES)]
    outs = [blk(C_WIDTH), blk(C_WIDTH), blk(C_WIDTH), blk(C_HEADS * LANES)]
    return ins, outs


def _ssd_prep_fwd(dt_raw, dt_bias, a_log, name):
    t = dt_raw.shape[0]
    e64, e128 = _expand_mats()
    ins, outs = _ssd_prep_specs()

    def body(raw_ref, bias_ref, alog_ref, e64_ref, e128_ref, dt_ref, acs_ref, alast_ref, acs128_ref):
        res = _ssd_prep_fn(raw_ref[...], bias_ref[...], alog_ref[...], e64_ref[...], e128_ref[...])
        for ref, v in zip((dt_ref, acs_ref, alast_ref, acs128_ref), res):
            ref[...] = v

    return pl.pallas_call(
        body, name=name, grid=(t // CHUNK,), in_specs=ins, out_specs=outs,
        out_shape=[jax.ShapeDtypeStruct((t, C_WIDTH), F32)] * 3 + [jax.ShapeDtypeStruct((t, C_HEADS * LANES), F32)],
        compiler_params=_cparams("parallel"),
    )(dt_raw, dt_bias, a_log, e64, e128)


def _ssd_prep_bwd(dt_raw, dt_bias, a_log, d_dt, d_acs, d_alast, d_acs128, name):
    t = dt_raw.shape[0]
    e64, e128 = _expand_mats()
    ins, outs = _ssd_prep_specs()
    vec = pl.BlockSpec((1, LANES), lambda i: (0, 0))

    def body(raw_ref, bias_ref, alog_ref, e64_ref, e128_ref, g0, g1, g2, g3, draw_ref, dbias_ref, dalog_ref):
        @pl.when(pl.program_id(0) == 0)
        def _():
            dbias_ref[...] = jnp.zeros_like(dbias_ref)
            dalog_ref[...] = jnp.zeros_like(dalog_ref)

        e64v, e128v = e64_ref[...], e128_ref[...]
        _, vjp = jax.vjp(lambda r, b, al: _ssd_prep_fn(r, b, al, e64v, e128v),
                         raw_ref[...], bias_ref[...], alog_ref[...])
        draw, dbias, dalog = vjp((g0[...], g1[...], g2[...], g3[...]))
        draw_ref[...] = draw.astype(BF16)
        dbias_ref[...] += dbias
        dalog_ref[...] += dalog

    return pl.pallas_call(
        body, name=name, grid=(t // CHUNK,), in_specs=ins + outs,
        out_specs=[pl.BlockSpec((CHUNK, LANES), lambda i: (i, 0)), vec, vec],
        out_shape=[jax.ShapeDtypeStruct((t, LANES), BF16), jax.ShapeDtypeStruct((1, LANES), F32),
                   jax.ShapeDtypeStruct((1, LANES), F32)],
        compiler_params=_cparams("arbitrary"),
    )(dt_raw, dt_bias, a_log, e64, e128, d_dt, d_acs, d_alast, d_acs128)


def _ssd_chunk(x, dt, acs, alast, col0, col1, bm, cm, prev):
    xdt = x * dt
    cb = _nt(cm, bm)
    tri = _iota2((CHUNK, CHUNK), 0) >= _iota2((CHUNK, CHUNK), 1)
    l0 = jnp.exp(jnp.where(tri, col0 - col0.T, NEG))
    l1 = jnp.exp(jnp.where(tri, col1 - col1.T, NEG))
    ydiag = jnp.where(_lane_lt64((CHUNK, LANES)), _nn(cb * l0, xdt), _nn(cb * l1, xdt))
    states = _tn(xdt * jnp.exp(alast - acs), bm)
    yoff = _nt(cm, prev) * jnp.exp(acs)
    new = prev * jnp.exp(alast).T + states
    return ydiag + yoff, new


def _ssd_specs(seq, rev):
    ncs = seq // CHUNK
    npair = C_WIDTH // LANES

    def row(s, c):
        return s * ncs + (ncs - 1 - c if rev else c)

    return dict(
        x=pl.BlockSpec((CHUNK, LANES), lambda s, j, c: (row(s, c), j)),
        bm=pl.BlockSpec((CHUNK, SSM_STATE), lambda s, j, c: (row(s, c), C_WIDTH // LANES + j // 4)),
        cm=pl.BlockSpec((CHUNK, SSM_STATE), lambda s, j, c: (row(s, c), C_WIDTH // LANES + 2 + j // 4)),
        col=pl.BlockSpec((CHUNK, 2 * LANES), lambda s, j, c: (row(s, c), j)),
        st=pl.BlockSpec((None, None, LANES, SSM_STATE), lambda s, j, c: (row(s, c), j, 0, 0)),
        npair=npair, ncs=ncs,
    )


def _ssd_fwd(xbc_act, dt64, acs64, alast64, acs128, seq, name):
    t = xbc_act.shape[0]
    sp = _ssd_specs(seq, False)

    def body(x_ref, dt_ref, acs_ref, alast_ref, col_ref, bm_ref, cm_ref, y_ref, prev_ref, state):
        @pl.when(pl.program_id(2) == 0)
        def _():
            state[...] = jnp.zeros_like(state)

        prev = state[...]
        prev_ref[...] = prev
        y, new = _ssd_chunk(x_ref[...], dt_ref[...], acs_ref[...], alast_ref[...],
                            col_ref[:, pl.ds(0, LANES)], col_ref[:, pl.ds(LANES, LANES)],
                            bm_ref[...], cm_ref[...], prev)
        y_ref[...] = y
        state[...] = new

    return pl.pallas_call(
        body, name=name, grid=(t // seq, sp["npair"], sp["ncs"]),
        in_specs=[sp["x"], sp["x"], sp["x"], sp["x"], sp["col"], sp["bm"], sp["cm"]],
        out_specs=[sp["x"], sp["st"]],
        out_shape=[jax.ShapeDtypeStruct((t, C_WIDTH), F32),
                   jax.ShapeDtypeStruct((t // CHUNK, sp["npair"], LANES, SSM_STATE), F32)],
        scratch_shapes=[pltpu.VMEM((LANES, SSM_STATE), F32)],
        compiler_params=_cparams("parallel", "parallel", "arbitrary"),
    )(xbc_act, dt64, acs64, alast64, acs128, xbc_act, xbc_act)


def _ssd_bwd(xbc_act, dt64, acs64, alast64, acs128, prev_saved, dy, seq, name):
    t = xbc_act.shape[0]
    sp = _ssd_specs(seq, True)

    def body(x_ref, dt_ref, acs_ref, alast_ref, col_ref, bm_ref, cm_ref, prev_ref, dy_ref,
             dx_ref, ddt_ref, dacs_ref, dalast_ref, dcol_ref, dbc_ref, dstate):
        @pl.when(pl.program_id(2) == 0)
        def _():
            dstate[...] = jnp.zeros_like(dstate)

        _, vjp = jax.vjp(_ssd_chunk, x_ref[...], dt_ref[...], acs_ref[...], alast_ref[...],
                         col_ref[:, pl.ds(0, LANES)], col_ref[:, pl.ds(LANES, LANES)],
                         bm_ref[...], cm_ref[...], prev_ref[...])
        dx, ddt, dacs, dalast, dc0, dc1, dbm, dcm, dprev = vjp((dy_ref[...], dstate[...]))
        dx_ref[...] = dx
        ddt_ref[...] = ddt
        dacs_ref[...] = dacs
        dalast_ref[...] = dalast
        dcol_ref[:, pl.ds(0, LANES)] = dc0
        dcol_ref[:, pl.ds(LANES, LANES)] = dc1
        dbc_ref[:, pl.ds(0, SSM_STATE)] = dbm
        dbc_ref[:, pl.ds(SSM_STATE, SSM_STATE)] = dcm
        dstate[...] = dprev

    wide = jax.ShapeDtypeStruct((t, C_WIDTH), F32)
    return pl.pallas_call(
        body, name=name, grid=(t // seq, sp["npair"], sp["ncs"]),
        in_specs=[sp["x"], sp["x"], sp["x"], sp["x"], sp["col"], sp["bm"], sp["cm"], sp["st"], sp["x"]],
        out_specs=[sp["x"], sp["x"], sp["x"], sp["x"], sp["col"], sp["col"]],
        out_shape=[wide, wide, wide, wide, jax.ShapeDtypeStruct((t, 2 * C_WIDTH), F32),
                   jax.ShapeDtypeStruct((t, 2 * C_WIDTH), F32)],
        scratch_shapes=[pltpu.VMEM((LANES, SSM_STATE), F32)],
        compiler_params=_cparams("parallel", "parallel", "arbitrary"),
    )(xbc_act, dt64, acs64, alast64, acs128, xbc_act, xbc_act, prev_saved, dy)


def _ssd_post_fn(y, xs, z, dskip, g):
    v = (y + dskip * xs) * _silu(z)
    return v * lax.rsqrt(jnp.mean(v * v, axis=-1, keepdims=True) + EPS) * g


def _ssd_post_specs(tm, order):
    gw = C_WIDTH // 2

    def im(f):
        return lambda *ids: f(*order(*ids))
    return dict(
        blk=pl.BlockSpec((tm, gw), im(lambda g, r: (r, g))),
        z=pl.BlockSpec((tm, gw), im(lambda g, r: (r, COL_Z * LANES // gw + g))),
        vec=pl.BlockSpec((1, gw), im(lambda g, r: (0, g))),
    )


def _ssd_post_fwd(y_ssd, xbc_act, proj, dskip64, norm_g, name):
    t = y_ssd.shape[0]
    tm = _row_tile(t)
    sp = _ssd_post_specs(tm, lambda r, g: (g, r))

    def body(y_ref, xs_ref, z_ref, ds_ref, g_ref, o_ref):
        o_ref[...] = _ssd_post_fn(y_ref[...], xs_ref[...], z_ref[...], ds_ref[...], g_ref[...])

    return pl.pallas_call(
        body, name=name, grid=(t // tm, 2),
        in_specs=[sp["blk"], sp["blk"], sp["z"], sp["vec"], sp["vec"]], out_specs=sp["blk"],
        out_shape=jax.ShapeDtypeStruct((t, C_WIDTH), F32),
        compiler_params=_cparams("parallel", "parallel"),
    )(y_ssd, xbc_act, proj, dskip64, norm_g)


def _ssd_post_bwd(y_ssd, xbc_act, proj, dskip64, norm_g, dyc, name):
    t = y_ssd.shape[0]
    tm = _row_tile(t)
    sp = _ssd_post_specs(tm, lambda g, r: (g, r))

    def body(y_ref, xs_ref, z_ref, ds_ref, g_ref, dyc_ref, dy_ref, dxs_ref, dz_ref, dds_ref, dg_ref):
        @pl.when(pl.program_id(1) == 0)
        def _():
            dds_ref[...] = jnp.zeros_like(dds_ref)
            dg_ref[...] = jnp.zeros_like(dg_ref)

        _, vjp = jax.vjp(_ssd_post_fn, y_ref[...], xs_ref[...], z_ref[...], ds_ref[...], g_ref[...])
        dy, dxs, dz, dds, dg = vjp(dyc_ref[...])
        dy_ref[...] = dy
        dxs_ref[...] = dxs
        dz_ref[...] = dz.astype(BF16)
        dds_ref[...] += dds
        dg_ref[...] += dg

    wide = jax.ShapeDtypeStruct((t, C_WIDTH), F32)
    vec = jax.ShapeDtypeStruct((1, C_WIDTH), F32)
    return pl.pallas_call(
        body, name=name, grid=(2, t // tm),
        in_specs=[sp["blk"], sp["blk"], sp["z"], sp["vec"], sp["vec"], sp["blk"]],
        out_specs=[sp["blk"], sp["blk"], sp["blk"], sp["vec"], sp["vec"]],
        out_shape=[wide, wide, jax.ShapeDtypeStruct((t, C_WIDTH), BF16), vec, vec],
        compiler_params=_cparams("parallel", "arbitrary"),
    )(y_ssd, xbc_act, proj, dskip64, norm_g, dyc)


def _ssd_assemble(dxs_ssd, dxs_skip, dbc, name):
    t = dxs_ssd.shape[0]
    tm = _row_tile(t)
    npair = C_WIDTH // LANES

    def body(a_ref, b_ref, dbc_ref, o_ref):
        o_ref[:, pl.ds(0, C_WIDTH)] = a_ref[...] + b_ref[...]
        for grp in range(2):
            for which in range(2):
                acc = jnp.zeros((tm, SSM_STATE), F32)
                for j in range(grp * npair // 2, (grp + 1) * npair // 2):
                    acc = acc + dbc_ref[:, pl.ds((2 * j + which) * SSM_STATE, SSM_STATE)]
                o_ref[:, pl.ds(C_WIDTH + (2 * which + grp) * SSM_STATE, SSM_STATE)] = acc

    return pl.pallas_call(
        body, name=name, grid=(t // tm,),
        in_specs=[pl.BlockSpec((tm, C_WIDTH), lambda i: (i, 0))] * 2 + [pl.BlockSpec((tm, 2 * C_WIDTH), lambda i: (i, 0))],
        out_specs=pl.BlockSpec((tm, D_CONV_C), lambda i: (i, 0)),
        out_shape=jax.ShapeDtypeStruct((t, D_CONV_C), F32),
        compiler_params=_cparams("parallel"),
    )(dxs_ssd, dxs_skip, dbc)


def _pad_taps(w):
    return jnp.pad(w, ((0, CONV_PAD - w.shape[0]), (0, 0)))


def _pad_heads(v):
    return jnp.pad(v, (0, LANES - v.shape[0])).reshape(1, LANES)


def _group_heads(a):
    pad = [(0, 0)] * (a.ndim - 1) + [(0, LANES - SSD_GROUP_HEADS)]
    return jnp.concatenate([jnp.pad(a[..., :SSD_GROUP_HEADS], pad), jnp.pad(a[..., SSD_GROUP_HEADS:], pad)], axis=-1)


def _ungroup_heads(a):
    return jnp.concatenate([a[..., :SSD_GROUP_HEADS], a[..., LANES:LANES + SSD_GROUP_HEADS]], axis=-1)


def _layer_fwd(x, p, seq, li):
    n = f"l{li}_"
    h1 = _rms_fwd(x, p["norm1_g"], n + "rms1")
    proj = _matmul(h1, p["w_main"], mode="nn", name=n + "inproj")
    dt_raw = _matmul(h1, p["w_dt"], mode="nn", name=n + "inproj_dt")
    row = lambda v: v.reshape(1, -1)
    ya = _conv_fwd("a", proj, _pad_taps(p["conv_a_w"]), row(p["conv_a_b"]), (row(p["ln_a_g"]), row(p["ln_a_b"])),
                   seq, n + "conva")
    yb = _gmlp_fwd(proj, row(p["ln_b_g"]), row(p["ln_b_b"]), p["w_spatial"], p["b_spatial"], n + "gmlp")
    xbc_act = _conv_fwd("c", proj, _pad_taps(p["conv_c_w"]), row(p["conv_c_b"]), (), seq, n + "convc")
    y_ssd, prev = _ssd2_fwd(xbc_act, dt_raw, _group_heads(row(p["dt_bias"])), _group_heads(row(p["a_log"])), seq, n + "ssd")
    dskip64 = jnp.repeat(p["d_skip"], HEAD_DIM).reshape(1, C_WIDTH)
    yc = _ssd_post_fwd(y_ssd, xbc_act, proj, dskip64, row(p["norm_c_g"]), n + "ssdpost")
    ycat = jnp.concatenate([ya, yb, yc], axis=1).astype(BF16)
    x1 = _matmul(ycat, p["w_out"], mode="nn", name=n + "outproj", add=x)
    h2 = _rms_fwd(x1, p["norm2_g"], n + "rms2")
    u, act = _matmul(h2, p["w_ff1"], mode="nn", name=n + "ff1", epilogue=_relu2_epilogue, out_dtypes=(F32, BF16))
    x2 = _matmul(act, p["w_ff2"], mode="nn", name=n + "ff2", add=x1)
    saved = dict(x=x, h1=h1, proj=proj, dt_raw=dt_raw, xbc_act=xbc_act, prev=prev, y_ssd=y_ssd, dskip64=dskip64,
                 ycat=ycat, x1=x1, h2=h2, u=u, act=act)
    return x2, saved


def _layer_bwd(dx2, p, s, seq, li):
    n = f"l{li}_b_"
    row = lambda v: v.reshape(1, -1)
    g = {}
    du = _matmul(dx2, p["w_ff2"], mode="nt", name=n + "ff2_dx", epilogue=_relu2_bwd_epilogue, extra=s["u"],
                 out_dtypes=(BF16,))
    g["w_ff2"] = _matmul(s["act"], dx2, mode="tn", name=n + "ff2_dw")
    g["w_ff1"] = _matmul(s["h2"], du, mode="tn", name=n + "ff1_dw")
    dh2 = _matmul(du, p["w_ff1"], mode="nt", name=n + "ff1_dx")
    dx1, g["norm2_g"] = _rms_bwd(s["x1"], p["norm2_g"], dh2, dx2, n + "rms2")
    g["w_out"] = _matmul(s["ycat"], dx1, mode="tn", name=n + "out_dw")
    dycat = _matmul(dx1, p["w_out"], mode="nt", name=n + "out_dx")
    dya, dyb, dyc = dycat[:, :A_WIDTH], dycat[:, A_WIDTH:A_WIDTH + B_WIDTH], dycat[:, A_WIDTH + B_WIDTH:]
    proj = s["proj"]
    (dval, dgate), dwa, dba, (dlag, dlab) = _conv_bwd(
        "a", proj, _pad_taps(p["conv_a_w"]), row(p["conv_a_b"]), (row(p["ln_a_g"]), row(p["ln_a_b"])), dya, seq, n + "conva")
    g["conv_a_w"], g["conv_a_b"], g["ln_a_g"], g["ln_a_b"] = dwa[:CONV_A_K], dba[0], dlag[0], dlab[0]
    dbu, dbv, dlbg, dlbb, g["w_spatial"], g["b_spatial"] = _gmlp_bwd(
        proj, row(p["ln_b_g"]), row(p["ln_b_b"]), p["w_spatial"], p["b_spatial"], dyb, n + "gmlp")
    g["ln_b_g"], g["ln_b_b"] = dlbg[0], dlbb[0]
    dy_ssd, dxs_skip, dz, dds, dncg = _ssd_post_bwd(s["y_ssd"], s["xbc_act"], proj, s["dskip64"], row(p["norm_c_g"]),
                                                    dyc, n + "ssdpost")
    g["norm_c_g"] = dncg[0]
    g["d_skip"] = dds.reshape(C_HEADS, HEAD_DIM).sum(axis=1)
    dxs, ddt_raw, ddtb, dalog, dbm, dcm = _ssd2_bwd(
        s["xbc_act"], s["dt_raw"], _group_heads(row(p["dt_bias"])), _group_heads(row(p["a_log"])), s["prev"], dy_ssd, seq,
        n + "ssd")
    g["dt_bias"], g["a_log"] = _ungroup_heads(ddtb)[0], _ungroup_heads(dalog)[0]
    dconv = _ssd2_assemble(dxs, dxs_skip, dbm, dcm, n + "ssdasm")
    (dxbc,), dwc, dbcv, _ = _conv_bwd("c", proj, _pad_taps(p["conv_c_w"]), row(p["conv_c_b"]), (), dconv, seq, n + "convc")
    g["conv_c_w"], g["conv_c_b"] = dwc[:CONV_C_K], dbcv[0]
    dproj = jnp.concatenate([dval, dgate, dbu, dbv, dz, dxbc], axis=1)
    g["w_main"] = _matmul(s["h1"], dproj, mode="tn", name=n + "in_dw")
    g["w_dt"] = _matmul(s["h1"], ddt_raw, mode="tn", name=n + "indt_dw")
    dh1 = _matmul(dproj, p["w_main"], mode="nt", name=n + "in_dx")
    dh1 = _matmul(ddt_raw, p["w_dt"], mode="nt", name=n + "indt_dx", add=dh1)
    dx, g["norm1_g"] = _rms_bwd(s["x"], p["norm1_g"], dh1, dx1, n + "rms1")
    return dx, g


def _step_local(x, target, layer_params, final_g, seq):
    saved = []
    h = x
    for li, p in enumerate(layer_params):
        h, s = _layer_fwd(h, p, seq, li)
        saved.append(s)
    loss, dx, dgf = _loss_head(h, final_g, target)
    grads = [None] * len(layer_params)
    for li in reversed(range(len(layer_params))):
        dx, grads[li] = _layer_bwd(dx, layer_params[li], saved[li], seq, li)
    return loss, dx, grads, dgf


EW_BLOCK_BYTES = 1 << 20


def _ew(fn, ins, out_dtypes, name, leads=None):
    leads = leads or [None] * len(ins)
    rows, c = ins[0].shape[-2:]
    tr = _pick(rows, [t for t in (2048, 1024, 512, 256, 128, 64, 32, 16, 8) if t * c * 4 <= EW_BLOCK_BYTES])
    n_in = len(ins)

    def spec(lead):
        if lead is None:
            return pl.BlockSpec((tr, c), lambda i: (i, 0))
        return pl.BlockSpec((None, tr, c), functools.partial(lambda i, k: (k, i, 0), k=lead))

    def body(*refs):
        outs = fn(*[r[...].astype(F32) for r in refs[:n_in]])
        for o_ref, o in zip(refs[n_in:], outs):
            o_ref[...] = o.astype(o_ref.dtype)

    return pl.pallas_call(
        body, name=name, grid=(rows // tr,),
        in_specs=[spec(l) for l in leads], out_specs=[spec(None)] * len(out_dtypes),
        out_shape=[jax.ShapeDtypeStruct((rows, c), dt) for dt in out_dtypes],
        compiler_params=_cparams("parallel"),
    )(*ins)


def _adam_fn(w, g, m, v):
    m2 = ADAM_B1 * m + (1.0 - ADAM_B1) * g
    v2 = ADAM_B2 * v + (1.0 - ADAM_B2) * (g * g)
    m_hat = m2 / (1.0 - ADAM_B1 ** ADAM_STEP)
    v_hat = v2 / (1.0 - ADAM_B2 ** ADAM_STEP)
    delta = -ADAM_LR * (m_hat / (jnp.sqrt(v_hat) + ADAM_EPS) + ADAM_WD * w)
    return delta, m2, v2


def _adam(w, g, m, v, name):
    shape = w.shape
    two_d = lambda a: a.reshape(-1, shape[-1])
    outs = _ew(_adam_fn, [two_d(w), two_d(g), two_d(m), two_d(v)], (F32, F32, F32), name)
    return [o.reshape(shape) for o in outs]


_ANY = pl.BlockSpec(memory_space=pl.ANY)


def _mesh_pos():
    return lax.axis_index("x"), lax.axis_index("y"), lax.axis_index("c")


def _peer_chips(x, y):
    return [(1 - x, y), (x, 1 - y), (1 - x, 1 - y)]


def _remote(src, dst, send_sems, recv_sems, sem, to):
    return pltpu.make_async_remote_copy(src_ref=src, dst_ref=dst, send_sem=send_sems.at[sem],
                                        recv_sem=recv_sems.at[sem], device_id=to, device_id_type=MESH)


def _half_rows(n_rows, which):
    half = n_rows // 2
    return pl.ds(pl.multiple_of(which * half, 8), half)


def _comm_call(body, ins, out_shapes, n_sems, name, n_local=0):
    scratch = [pltpu.SemaphoreType.DMA((n_sems,)), pltpu.SemaphoreType.DMA((n_sems,))]
    if n_local:
        scratch.append(pltpu.SemaphoreType.DMA((n_local,)))
    return pl.pallas_call(
        body, name=name, in_specs=[_ANY] * len(ins), out_specs=[_ANY] * len(out_shapes),
        out_shape=out_shapes, scratch_shapes=scratch,
    )(*ins)


def _gather_weights(big, small, name):
    nb, ns = len(big), len(small)
    n = nb + ns

    def body(*refs):
        ins, outs = refs[:n], refs[n:2 * n]
        send_sems, recv_sems, local_sems = refs[2 * n:]
        x, y, c = _mesh_pos()
        q = 2 * x + y
        me, sib = (x, y, c), (x, y, 1 - c)
        chips = _peer_chips(x, y)
        rem = functools.partial(_remote, send_sems=send_sems, recv_sems=recv_sems)
        local = [pltpu.make_async_copy(ins[i], outs[i].at[q], local_sems.at[i]) for i in range(n)]
        for cp in local:
            cp.start()
        first = []
        for i in range(nb):
            mine = _half_rows(big[i].shape[0], c)
            for k, (px, py) in enumerate(chips):
                first.append(rem(ins[i].at[mine], outs[i].at[q, mine], sem=6 * i + k, to=(px, py, c)))
        for j in range(ns):
            for k, (px, py) in enumerate(chips):
                first.append(rem(ins[nb + j], outs[nb + j].at[q], sem=6 * nb + 3 * j + k, to=(px, py, c)))
        for cp in first:
            cp.start()
        passed = []
        for i in range(nb):
            mine = _half_rows(big[i].shape[0], c)
            for k, (px, py) in enumerate(chips):
                landed = outs[i].at[2 * px + py, mine]
                rem(landed, landed, sem=6 * i + k, to=me).wait_recv()
                fwd = rem(landed, landed, sem=6 * i + 3 + k, to=sib)
                fwd.start()
                passed.append(fwd)
        for i in range(nb):
            other = _half_rows(big[i].shape[0], 1 - c)
            for k, (px, py) in enumerate(chips):
                theirs = outs[i].at[2 * px + py, other]
                rem(theirs, theirs, sem=6 * i + 3 + k, to=me).wait_recv()
        for j in range(ns):
            for k, (px, py) in enumerate(chips):
                dst = outs[nb + j].at[2 * px + py]
                rem(dst, dst, sem=6 * nb + 3 * j + k, to=me).wait_recv()
        for cp in first + passed:
            cp.wait_send()
        for cp in local:
            cp.wait()

    out_shapes = [jax.ShapeDtypeStruct((N_CHIPS,) + a.shape, a.dtype) for a in list(big) + list(small)]
    return _comm_call(body, list(big) + list(small), out_shapes, 6 * nb + 3 * ns, name, n_local=n)


def _sibling_swap(p, name):
    def body(p_ref, mine_ref, theirs_ref, send_sems, recv_sems, local_sems):
        x, y, c = _mesh_pos()
        keep = pltpu.make_async_copy(p_ref.at[c], mine_ref, local_sems.at[0])
        keep.start()
        cp = _remote(p_ref.at[1 - c], theirs_ref, send_sems, recv_sems, 0, (x, y, 1 - c))
        cp.start()
        cp.wait()
        keep.wait()

    slab = jax.ShapeDtypeStruct(p.shape[1:], p.dtype)
    return _comm_call(body, [p], [slab, slab], 1, name, n_local=1)


def _chip_scatter(send, keep, name):
    def body(send_ref, keep_ref, got_ref, own_ref, send_sems, recv_sems, local_sems):
        x, y, c = _mesh_pos()
        own = pltpu.make_async_copy(keep_ref.at[2 * x + y], own_ref, local_sems.at[0])
        own.start()
        copies = [_remote(send_ref.at[2 * px + py], got_ref.at[k], send_sems, recv_sems, k, (px, py, c))
                  for k, (px, py) in enumerate(_peer_chips(x, y))]
        for cp in copies:
            cp.start()
        for cp in copies:
            cp.wait()
        own.wait()

    out_shapes = [jax.ShapeDtypeStruct((3,) + send.shape[1:], send.dtype), jax.ShapeDtypeStruct(keep.shape[1:], keep.dtype)]
    return _comm_call(body, [send, keep], out_shapes, 3, name, n_local=1)


def _sibling_share(fs, name):
    n = len(fs)

    def body(*refs):
        ins, outs = refs[:n], refs[n:2 * n]
        send_sems, recv_sems, local_sems = refs[2 * n:]
        x, y, c = _mesh_pos()
        local = [pltpu.make_async_copy(ins[i], outs[i].at[c], local_sems.at[i]) for i in range(n)]
        copies = [_remote(ins[i], outs[i].at[c], send_sems, recv_sems, i, (x, y, 1 - c)) for i in range(n)]
        for cp in local + copies:
            cp.start()
        for i in range(n):
            theirs = outs[i].at[1 - c]
            _remote(theirs, theirs, send_sems, recv_sems, i, (x, y, c)).wait_recv()
        for cp in copies:
            cp.wait_send()
        for cp in local:
            cp.wait()

    out_shapes = [jax.ShapeDtypeStruct((2,) + a.shape, a.dtype) for a in fs]
    return _comm_call(body, list(fs), out_shapes, n, name, n_local=n)


def _allgather8(v, name):
    m = v.shape[0]

    def body(v_ref, out_ref, send_sems, recv_sems, local_sems):
        x, y, c = _mesh_pos()
        me, sib = (x, y, c), (x, y, 1 - c)
        chips = _peer_chips(x, y)
        rem = functools.partial(_remote, send_sems=send_sems, recv_sems=recv_sems)

        def blk(px, py, pc):
            return out_ref.at[4 * px + 2 * py + pc]

        mine = pltpu.make_async_copy(v_ref, blk(*me), local_sems.at[0])
        mine.start()
        first = [rem(v_ref, blk(*me), sem=0, to=sib)]
        first += [rem(v_ref, blk(*me), sem=1 + k, to=(px, py, c)) for k, (px, py) in enumerate(chips)]
        for cp in first:
            cp.start()
        passed = []
        for k, (px, py) in enumerate(chips):
            landed = blk(px, py, c)
            rem(landed, landed, sem=1 + k, to=me).wait_recv()
            fwd = rem(landed, landed, sem=4 + k, to=sib)
            fwd.start()
            passed.append(fwd)
        rem(blk(*sib), blk(*sib), sem=0, to=me).wait_recv()
        for k, (px, py) in enumerate(chips):
            theirs = blk(px, py, 1 - c)
            rem(theirs, theirs, sem=4 + k, to=me).wait_recv()
        for cp in first + passed:
            cp.wait_send()
        mine.wait()

    return _comm_call(body, [v], [jax.ShapeDtypeStruct((8, m, LANES), v.dtype)], 7, name, n_local=1)[0]


_WEIGHTS = ["norm1_g", "w_in", "conv_a_w", "conv_a_b", "ln_a_g", "ln_a_b", "ln_b_g", "ln_b_b", "w_spatial", "b_spatial",
            "conv_c_w", "conv_c_b", "dt_bias", "a_log", "d_skip", "norm_c_g", "w_out", "norm2_g", "w_ff1", "w_ff2", "final_g"]
_BIG = ["w_in", "w_out", "w_ff1", "w_ff2"]
_CONV_SHARDED = ["conv_a_w", "conv_c_w"]
_SMALL = [w for w in _WEIGHTS if w not in _BIG and w != "final_g"]
_PACK_ROWS = 512


def _pack(arrs):
    flat = jnp.concatenate([a.reshape(-1) for a in arrs])
    blk = _PACK_ROWS * LANES
    n = flat.shape[0]
    return jnp.pad(flat, (0, -(-n // blk) * blk - n)).reshape(-1, LANES)


def _unpack(packed, shapes):
    flat = packed.reshape(-1)
    out, off = [], 0
    for s in shapes:
        n = math.prod(s)
        out.append(flat[off:off + n].reshape(s))
        off += n
    return out


def _cols_to_chips(a):
    k = a.shape[0]
    return a.reshape(k, N_CHIPS, -1).transpose(1, 0, 2)


def _chips_to_cols(a):
    return a.transpose(1, 0, 2).reshape(a.shape[1], -1)


PIECE_ROWS = 15 * 1024


def _to_pieces(arrs):
    lead = arrs[0].shape[:-2]
    flat = jnp.concatenate([a.reshape(lead + (2, -1)) for a in arrs], axis=-1)
    pad = PIECE_ROWS * LANES - flat.shape[-1]
    flat = jnp.pad(flat, [(0, 0)] * (len(lead) + 1) + [(0, pad)])
    return flat.reshape(lead + (2, PIECE_ROWS, LANES))


def _from_pieces(pieces, shapes):
    lead = pieces.shape[:-3]
    flat = pieces.reshape(lead + (2, -1))
    out, off = [], 0
    for r, c in shapes:
        n = r * c // 2
        out.append(flat[..., off:off + n].reshape(lead + (r, c)))
        off += n
    return out


def _gathered_layer_params(w, li):
    shards = [w[k][li].astype(BF16) for k in _BIG]
    small = [w[k][li] for k in _CONV_SHARDED]
    pieces = _to_pieces(shards).reshape(2 * PIECE_ROWS, LANES)
    g_big, g_ca, g_cc = _gather_weights([pieces], small, f"l{li}_gather")
    g_in, g_out, g_ff1, g_ff2 = _from_pieces(g_big.reshape(N_CHIPS, 2, PIECE_ROWS, LANES), [s.shape for s in shards])
    p = {k: w[k][li] for k in _SMALL if k not in _CONV_SHARDED}
    w_in = _chips_to_cols(g_in)
    p["w_main"] = w_in[:, :D_MAIN]
    p["w_dt"] = _group_heads(w_in[:, D_MAIN:])
    p["w_out"] = g_out.reshape(D_MIX, D_MODEL)
    p["w_ff1"] = _chips_to_cols(g_ff1)
    p["w_ff2"] = g_ff2.reshape(D_FF, D_MODEL)
    p["conv_a_w"] = _chips_to_cols(g_ca)
    p["conv_c_w"] = _chips_to_cols(g_cc)
    return p


def _reduce_scatter_layer(g, li):
    n = f"l{li}_rs_"
    g_in = jnp.concatenate([g["w_main"], _ungroup_heads(g["w_dt"])], axis=1)
    full = [_cols_to_chips(g_in), g["w_out"].reshape(N_CHIPS, -1, D_MODEL), _cols_to_chips(g["w_ff1"]),
            g["w_ff2"].reshape(N_CHIPS, -1, D_MODEL)]
    shapes = [a.shape[1:] for a in full]
    by_core = _to_pieces(full).transpose(1, 0, 2, 3).reshape(2, N_CHIPS * PIECE_ROWS, LANES)
    mine, from_sib = _sibling_swap(by_core, n + "sib")
    chip32, chip16 = _ew(lambda u, v: (u + v, u + v), [mine, from_sib], (F32, BF16), n + "chipsum")
    by_chip = (N_CHIPS, PIECE_ROWS, LANES)
    from_chips, own = _chip_scatter(chip16.reshape(by_chip), chip32.reshape(by_chip), n + "scatter")
    half = _ew(lambda o, r0, r1, r2: (((o + r0) + r1) + r2,), [own, from_chips, from_chips, from_chips], (F32,),
               n + "final", leads=[None, 0, 1, 2])[0]
    both = _sibling_share([half], n + "share")[0]
    return _from_pieces(both, shapes)


def kernel(x, norm1_g, w_in, conv_a_w, conv_a_b, ln_a_g, ln_a_b, ln_b_g, ln_b_b, w_spatial, b_spatial, conv_c_w, conv_c_b, dt_bias, a_log, d_skip, norm_c_g, w_out, norm2_g, w_ff1, w_ff2, final_g, loss_target, m_norm1_g, m_w_in, m_conv_a_w, m_conv_a_b, m_ln_a_g, m_ln_a_b, m_ln_b_g, m_ln_b_b, m_w_spatial, m_b_spatial, m_conv_c_w, m_conv_c_b, m_dt_bias, m_a_log, m_d_skip, m_norm_c_g, m_w_out, m_norm2_g, m_w_ff1, m_w_ff2, m_final_g, v_norm1_g, v_w_in, v_conv_a_w, v_conv_a_b, v_ln_a_g, v_ln_a_b, v_ln_b_g, v_ln_b_b, v_w_spatial, v_b_spatial, v_conv_c_w, v_conv_c_b, v_dt_bias, v_a_log, v_d_skip, v_norm_c_g, v_w_out, v_norm2_g, v_w_ff1, v_w_ff2, v_final_g):
    given = dict(locals())
    w = {k: given[k] for k in _WEIGHTS}
    m = {k: given["m_" + k] for k in _WEIGHTS}
    v = {k: given["v_" + k] for k in _WEIGHTS}
    depth = w_in.shape[0]
    nseq, seq, d = x.shape
    xi, yi, ci = _mesh_pos()
    q = 2 * xi + yi

    layer_params = [_gathered_layer_params(w, li) for li in range(depth)]
    loss, dx, grads, d_final = _step_local(x.reshape(nseq * seq, d), loss_target.reshape(nseq * seq, d), layer_params,
                                           final_g, seq)

    big_grads = {k: [] for k in _BIG}
    for li in range(depth):
        for k, g in zip(_BIG, _reduce_scatter_layer(grads[li], li)):
            big_grads[k].append(g)
    grad_out, delta_out, m_out, v_out = {}, {}, {}, {}
    for k in _BIG:
        grad_out[k] = jnp.stack(big_grads[k])
        delta_out[k], m_out[k], v_out[k] = _adam(w[k], grad_out[k], m[k], v[k], "adam_" + k)

    small_shapes = [grads[0][k].shape for k in _SMALL]
    parts = [grads[li][k] for li in range(depth) for k in _SMALL] + [d_final, loss.reshape(1)]
    gathered = _allgather8(_pack(parts), "small_allgather")

    def sum8(*blocks):
        acc = blocks[0]
        for b in blocks[1:]:
            acc = acc + b
        return (acc,)

    total = _ew(sum8, [gathered] * 8, (F32,), "small_sum", leads=list(range(8)))[0]
    summed = _unpack(total, small_shapes * depth + [d_final.shape, (1,)])
    loss_total = summed[-1][0]
    small_grads = {k: jnp.stack([summed[li * len(_SMALL) + i] for li in range(depth)]) for i, k in enumerate(_SMALL)}
    small_grads["final_g"] = summed[-2]
    for k in _CONV_SHARDED:
        n_shard = w[k].shape[-1]
        small_grads[k] = lax.dynamic_slice_in_dim(small_grads[k], q * n_shard, n_shard, axis=2)
    names = _SMALL + ["final_g"]
    shapes = [w[k].shape for k in names]
    packed = [_pack([src[k] for k in names]) for src in (w, small_grads, m, v)]
    outs = _ew(_adam_fn, packed, (F32, F32, F32), "adam_small")
    for dst, o in zip((delta_out, m_out, v_out), outs):
        for k, a in zip(names, _unpack(o, shapes)):
            dst[k] = a
    for k in names:
        grad_out[k] = small_grads[k]

    return (loss_total, dx.reshape(nseq, seq, d), *[grad_out[k] for k in _WEIGHTS], *[delta_out[k] for k in _WEIGHTS],
            *[m_out[k] for k in _WEIGHTS], *[v_out[k] for k in _WEIGHTS])
```

```python
import functools
import math

import jax
import jax.numpy as jnp
from jax import lax
from jax.experimental import pallas as pl
from jax.experimental.pallas import tpu as pltpu

F32 = jnp.float32
BF16 = jnp.bfloat16
MESH = pl.DeviceIdType.MESH

D_MODEL = 1024
DEPTH = 4
HEAD_DIM = 64
A_WIDTH = 512
B_WIDTH = 512
C_WIDTH = 1024
C_HEADS = 16
CONV_A_K = 31
CONV_C_K = 4
CHUNK = 128
SSM_STATE = 128
D_CONV_C = 1536
D_MAIN = 4608
D_IN_PROJ = 4624
D_MIX = 2048
D_FF = 4096
EPS = 1e-5
NEG = -1e30
LANES = 128
CONV_PAD = 32
N_CHIPS = 4

ADAM_LR = 0.001
ADAM_B1 = 0.9
ADAM_B2 = 0.999
ADAM_EPS = 1e-08
ADAM_WD = 0.01
ADAM_STEP = 10

VMEM_LIMIT = 56 * 1024 * 1024

COL_AVAL, COL_AGATE, COL_BU, COL_BV, COL_Z, COL_XBC = 0, 4, 8, 12, 16, 24


def _cparams(*sem):
    return pltpu.CompilerParams(dimension_semantics=sem, vmem_limit_bytes=VMEM_LIMIT)


_DN = {"nn": (((1,), (0,)), ((), ())), "nt": (((1,), (1,)), ((), ())), "tn": (((0,), (0,)), ((), ()))}


def _dot_raw(a, b, mode):
    return lax.dot_general(a.astype(BF16), b.astype(BF16), _DN[mode], preferred_element_type=F32)


def _make_dot(mode):
    @jax.custom_vjp
    def f(a, b):
        return _dot_raw(a, b, mode)

    def fwd(a, b):
        return _dot_raw(a, b, mode), (a, b)

    def bwd(res, g):
        a, b = res
        if mode == "nn":
            return _dot_raw(g, b, "nt"), _dot_raw(a, g, "tn")
        if mode == "nt":
            return _dot_raw(g, b, "nn"), _dot_raw(g, a, "tn")
        return _dot_raw(b, g, "nt"), _dot_raw(a, g, "nn")

    f.defvjp(fwd, bwd)
    return f


_nn = _make_dot("nn")
_nt = _make_dot("nt")
_tn = _make_dot("tn")


def _xdot(a, e):
    return jnp.dot(a, e, precision=lax.Precision.HIGHEST, preferred_element_type=F32)


def _iota2(shape, dim):
    return lax.broadcasted_iota(jnp.int32, shape, dim)


def _gmean_impl(x):
    n = x.shape[-1]
    same = (_iota2((n, n), 0) < HEAD_DIM) == (_iota2((n, n), 1) < HEAD_DIM)
    p = jnp.where(same, 1.0 / HEAD_DIM, 0.0).astype(BF16)
    hi = x.astype(BF16)
    lo = (x - hi.astype(F32)).astype(BF16)
    dn = _DN["nn"]
    return (lax.dot_general(hi, p, dn, preferred_element_type=F32)
            + lax.dot_general(lo, p, dn, preferred_element_type=F32))


@jax.custom_vjp
def _gmean(x):
    return _gmean_impl(x)


_gmean.defvjp(lambda x: (_gmean_impl(x), None), lambda _, g: (_gmean_impl(g),))


def _sigmoid(x):
    return 1.0 / (1.0 + jnp.exp(-x))


def _silu(x):
    return x * _sigmoid(x)


def _gelu(x):
    return 0.5 * x * (1.0 + lax.erf(x * 0.7071067811865476))


def _softplus(x):
    return jnp.maximum(x, 0.0) + jnp.log(1.0 + jnp.exp(-jnp.abs(x)))


def _rms(x, g):
    return x * lax.rsqrt(jnp.mean(x * x, axis=-1, keepdims=True) + EPS) * g


def _ln64(x, g, b):
    mu = _gmean(x)
    xc = x - mu
    var = _gmean(xc * xc)
    return xc * lax.rsqrt(var + EPS) * g + b


def _lane_lt64(shape):
    return _iota2(shape, 1) < HEAD_DIM


def _pick(n, pref):
    for t in pref:
        if n % t == 0:
            return t
    return n


def _matmul(a, b, *, mode, name, add=None, epilogue=None, extra=None, out_dtypes=(F32,)):
    if mode == "nn":
        (m, k), (_, n) = a.shape, b.shape
    elif mode == "nt":
        (m, k), (n, _) = a.shape, b.shape
    else:
        (k, m), (_, n) = a.shape, b.shape
    tm = _pick(m, (1024, 512, 256, 128))
    tn = _pick(n, (1536, 1024, 512, 256, 128))
    tk = _pick(k, (1536, 1024, 512, 256, 128))
    nk = k // tk
    a_spec = {"nn": pl.BlockSpec((tm, tk), lambda i, j, kk: (i, kk)),
              "nt": pl.BlockSpec((tm, tk), lambda i, j, kk: (i, kk)),
              "tn": pl.BlockSpec((tk, tm), lambda i, j, kk: (kk, i))}[mode]
    b_spec = {"nn": pl.BlockSpec((tk, tn), lambda i, j, kk: (kk, j)),
              "nt": pl.BlockSpec((tn, tk), lambda i, j, kk: (j, kk)),
              "tn": pl.BlockSpec((tk, tn), lambda i, j, kk: (kk, j))}[mode]
    o_spec = pl.BlockSpec((tm, tn), lambda i, j, kk: (i, j))
    ins = [a, b]
    in_specs = [a_spec, b_spec]
    if add is not None:
        ins.append(add)
        in_specs.append(o_spec)
    if extra is not None:
        ins.append(extra)
        in_specs.append(o_spec)
    n_out = len(out_dtypes)

    def body(*refs):
        a_ref, b_ref = refs[0], refs[1]
        pos = 2
        add_ref = ex_ref = None
        if add is not None:
            add_ref = refs[pos]
            pos += 1
        if extra is not None:
            ex_ref = refs[pos]
            pos += 1
        o_refs = refs[pos:pos + n_out]

        def finish(acc):
            if add_ref is not None:
                acc = acc + add_ref[...].astype(F32)
            outs = (acc,) if epilogue is None else epilogue(acc, None if ex_ref is None else ex_ref[...])
            for o_ref, o in zip(o_refs, outs):
                o_ref[...] = o.astype(o_ref.dtype)

        part = _dot_raw(a_ref[...], b_ref[...], mode)
        if nk == 1:
            finish(part)
            return
        acc_ref = refs[pos + n_out]
        kk = pl.program_id(2)

        @pl.when(kk == 0)
        def _():
            acc_ref[...] = part

        @pl.when(jnp.logical_and(kk > 0, kk < nk - 1))
        def _():
            acc_ref[...] += part

        @pl.when(kk == nk - 1)
        def _():
            finish(acc_ref[...] + part)

    res = pl.pallas_call(
        body, name=name, grid=(m // tm, n // tn, nk),
        in_specs=in_specs, out_specs=[o_spec] * n_out,
        out_shape=[jax.ShapeDtypeStruct((m, n), dt) for dt in out_dtypes],
        scratch_shapes=[pltpu.VMEM((tm, tn), F32)] if nk > 1 else [],
        compiler_params=_cparams("parallel", "parallel", "arbitrary"),
    )(*ins)
    return res[0] if n_out == 1 else res


def _relu2_epilogue(acc, _):
    r = jnp.maximum(acc, 0.0)
    return acc, r * r


def _relu2_bwd_epilogue(acc, u):
    return (acc * (2.0 * jnp.maximum(u, 0.0)),)


def _row_tile(t):
    return _pick(t, (512, 256, 128))


def _rms_fwd(x, g, name):
    t, d = x.shape
    tm = _row_tile(t)

    def body(x_ref, g_ref, o_ref):
        o_ref[...] = _rms(x_ref[...], g_ref[...]).astype(BF16)

    return pl.pallas_call(
        body, name=name, grid=(t // tm,),
        in_specs=[pl.BlockSpec((tm, d), lambda i: (i, 0)), pl.BlockSpec((1, d), lambda i: (0, 0))],
        out_specs=pl.BlockSpec((tm, d), lambda i: (i, 0)),
        out_shape=jax.ShapeDtypeStruct((t, d), BF16),
        compiler_params=_cparams("parallel"),
    )(x, g.reshape(1, d))


def _rms_bwd(x, g, dh, dres, name):
    t, d = x.shape
    tm = _row_tile(t)

    def body(x_ref, g_ref, dh_ref, dres_ref, dx_ref, dg_ref):
        @pl.when(pl.program_id(0) == 0)
        def _():
            dg_ref[...] = jnp.zeros_like(dg_ref)

        _, vjp = jax.vjp(_rms, x_ref[...], g_ref[...])
        dx, dg = vjp(dh_ref[...].astype(F32))
        dx_ref[...] = dx + dres_ref[...]
        dg_ref[...] += dg

    row = pl.BlockSpec((tm, d), lambda i: (i, 0))
    vec = pl.BlockSpec((1, d), lambda i: (0, 0))
    dx, dg = pl.pallas_call(
        body, name=name, grid=(t // tm,),
        in_specs=[row, vec, row, row], out_specs=[row, vec],
        out_shape=[jax.ShapeDtypeStruct((t, d), F32), jax.ShapeDtypeStruct((1, d), F32)],
        compiler_params=_cparams("arbitrary"),
    )(x, g.reshape(1, d), dh, dres)
    return dx, dg.reshape(d)


def _loss_head(x, g, target):
    t, d = x.shape
    tm = _row_tile(t)

    def loss_fn(xv, gv, tv):
        err = _rms(xv, gv) - tv
        return 0.5 * jnp.sum(jnp.mean(err * err, axis=-1, keepdims=True))

    def body(x_ref, g_ref, t_ref, loss_ref, dx_ref, dg_ref):
        @pl.when(pl.program_id(0) == 0)
        def _():
            dg_ref[...] = jnp.zeros_like(dg_ref)
            loss_ref[...] = jnp.zeros_like(loss_ref)

        tv = t_ref[...]
        val, vjp = jax.vjp(lambda xv, gv: loss_fn(xv, gv, tv), x_ref[...], g_ref[...])
        dx, dg = vjp(jnp.ones((), F32))
        dx_ref[...] = dx
        dg_ref[...] += dg
        loss_ref[...] += jnp.full(loss_ref.shape, val, F32)

    row = pl.BlockSpec((tm, d), lambda i: (i, 0))
    vec = pl.BlockSpec((1, d), lambda i: (0, 0))
    loss, dx, dg = pl.pallas_call(
        body, name="loss_head", grid=(t // tm,),
        in_specs=[row, vec, row], out_specs=[pl.BlockSpec((1, LANES), lambda i: (0, 0)), row, vec],
        out_shape=[jax.ShapeDtypeStruct((1, LANES), F32), jax.ShapeDtypeStruct((t, d), F32),
                   jax.ShapeDtypeStruct((1, d), F32)],
        compiler_params=_cparams("arbitrary"),
    )(x, g.reshape(1, d), target)
    return loss[0, 0], dx, dg.reshape(d)


def _pre_glu(val, gate):
    return val * _sigmoid(gate)


def _pre_id(x):
    return x


def _post_lnsilu(c, g, b):
    return _silu(_ln64(c, g, b))


def _post_silu(c):
    return _silu(c)


def _conv_cfg(kind):
    if kind == "a":
        return dict(k=CONV_A_K, pre=_pre_glu, post=_post_lnsilu, n_in=2, n_par=2, nblk=A_WIDTH // LANES,
                    cols=(COL_AVAL, COL_AGATE))
    return dict(k=CONV_C_K, pre=_pre_id, post=_post_silu, n_in=1, n_par=0, nblk=D_CONV_C // LANES,
                cols=(COL_XBC,))


def _conv_fwd(kind, proj, w, bias, params, seq, name):
    cfg = _conv_cfg(kind)
    kt, pre, post, n_in = cfg["k"], cfg["pre"], cfg["post"], cfg["n_in"]
    t = proj.shape[0]
    nseq = t // seq
    c = cfg["nblk"] * LANES
    rt = min(256, seq)
    nrt = seq // rt
    off0 = CONV_PAD - (kt - 1)

    def body(*refs):
        in_refs = refs[:n_in]
        w_ref, b_ref = refs[n_in], refs[n_in + 1]
        par_refs = refs[n_in + 2:n_in + 2 + cfg["n_par"]]
        o_ref, hpad = refs[n_in + 2 + cfg["n_par"]:]
        hpad[pl.ds(0, CONV_PAD), :] = jnp.zeros((CONV_PAD, LANES), F32)
        for r in range(nrt):
            hpad[pl.ds(CONV_PAD + r * rt, rt), :] = pre(*[x[pl.ds(r * rt, rt), :] for x in in_refs])
        pars = [p[...] for p in par_refs]
        for r in range(nrt):
            acc = jnp.broadcast_to(b_ref[...], (rt, LANES))
            for k in range(kt):
                acc = acc + w_ref[pl.ds(k, 1), :] * hpad[pl.ds(off0 + k + r * rt, rt), :]
            o_ref[pl.ds(r * rt, rt), :] = post(acc, *pars)

    in_specs = [pl.BlockSpec((seq, LANES), functools.partial(lambda s, j, col: (s, col + j), col=col))
                for col in cfg["cols"]]
    vec = pl.BlockSpec((1, LANES), lambda s, j: (0, j))
    in_specs += [pl.BlockSpec((CONV_PAD, LANES), lambda s, j: (0, j)), vec] + [vec] * cfg["n_par"]
    return pl.pallas_call(
        body, name=name, grid=(nseq, cfg["nblk"]),
        in_specs=in_specs, out_specs=pl.BlockSpec((seq, LANES), lambda s, j: (s, j)),
        out_shape=jax.ShapeDtypeStruct((t, c), F32),
        scratch_shapes=[pltpu.VMEM((seq + CONV_PAD, LANES), F32)],
        compiler_params=_cparams("parallel", "parallel"),
    )(*([proj] * n_in), w, bias, *params)


def _conv_bwd(kind, proj, w, bias, params, dy, seq, name):
    cfg = _conv_cfg(kind)
    kt, pre, post, n_in, n_par = cfg["k"], cfg["pre"], cfg["post"], cfg["n_in"], cfg["n_par"]
    t = proj.shape[0]
    nseq = t // seq
    c = cfg["nblk"] * LANES
    rt = min(256, seq)
    nrt = seq // rt
    off0 = CONV_PAD - (kt - 1)

    def body(*refs):
        in_refs = refs[:n_in]
        w_ref, b_ref = refs[n_in], refs[n_in + 1]
        par_refs = refs[n_in + 2:n_in + 2 + n_par]
        pos = n_in + 2 + n_par
        dy_ref = refs[pos]
        din_refs = refs[pos + 1:pos + 1 + n_in]
        dw_ref, db_ref = refs[pos + 1 + n_in], refs[pos + 2 + n_in]
        dpar_refs = refs[pos + 3 + n_in:pos + 3 + n_in + n_par]
        hpad, dcpad = refs[pos + 3 + n_in + n_par:]

        @pl.when(pl.program_id(1) == 0)
        def _():
            dw_ref[...] = jnp.zeros_like(dw_ref)
            db_ref[...] = jnp.zeros_like(db_ref)
            for r in dpar_refs:
                r[...] = jnp.zeros_like(r)

        hpad[pl.ds(0, CONV_PAD), :] = jnp.zeros((CONV_PAD, LANES), F32)
        dcpad[pl.ds(seq, CONV_PAD), :] = jnp.zeros((CONV_PAD, LANES), F32)
        for r in range(nrt):
            hpad[pl.ds(CONV_PAD + r * rt, rt), :] = pre(*[x[pl.ds(r * rt, rt), :] for x in in_refs])
        pars = [p[...] for p in par_refs]
        for r in range(nrt):
            acc = jnp.broadcast_to(b_ref[...], (rt, LANES))
            for k in range(kt):
                acc = acc + w_ref[pl.ds(k, 1), :] * hpad[pl.ds(off0 + k + r * rt, rt), :]
            _, vjp = jax.vjp(post, acc, *pars)
            grads = vjp(dy_ref[pl.ds(r * rt, rt), :])
            dcpad[pl.ds(r * rt, rt), :] = grads[0]
            db_ref[...] += jnp.sum(grads[0], axis=0, keepdims=True)
            for ref, gpar in zip(dpar_refs, grads[1:]):
                ref[...] += gpar
        for r in range(nrt):
            dh = jnp.zeros((rt, LANES), F32)
            for k in range(kt):
                dh = dh + w_ref[pl.ds(k, 1), :] * dcpad[pl.ds(r * rt + kt - 1 - k, rt), :]
            _, vjp = jax.vjp(pre, *[x[pl.ds(r * rt, rt), :] for x in in_refs])
            for ref, gin in zip(din_refs, vjp(dh)):
                ref[pl.ds(r * rt, rt), :] = gin.astype(ref.dtype)
        for k in range(kt):
            s = jnp.zeros((1, LANES), F32)
            for r in range(nrt):
                s = s + jnp.sum(dcpad[pl.ds(r * rt, rt), :] * hpad[pl.ds(off0 + k + r * rt, rt), :],
                                axis=0, keepdims=True)
            dw_ref[pl.ds(k, 1), :] += s

    in_specs = [pl.BlockSpec((seq, LANES), functools.partial(lambda j, s, col: (s, col + j), col=col))
                for col in cfg["cols"]]
    vec = pl.BlockSpec((1, LANES), lambda j, s: (0, j))
    wspec = pl.BlockSpec((CONV_PAD, LANES), lambda j, s: (0, j))
    blk = pl.BlockSpec((seq, LANES), lambda j, s: (s, j))
    in_specs += [wspec, vec] + [vec] * n_par + [blk]
    out_specs = [blk] * n_in + [wspec, vec] + [vec] * n_par
    out_shape = ([jax.ShapeDtypeStruct((t, c), BF16)] * n_in
                 + [jax.ShapeDtypeStruct((CONV_PAD, c), F32), jax.ShapeDtypeStruct((1, c), F32)]
                 + [jax.ShapeDtypeStruct((1, c), F32)] * n_par)
    res = pl.pallas_call(
        body, name=name, grid=(cfg["nblk"], nseq),
        in_specs=in_specs, out_specs=out_specs, out_shape=out_shape,
        scratch_shapes=[pltpu.VMEM((seq + CONV_PAD, LANES), F32), pltpu.VMEM((seq + CONV_PAD, LANES), F32)],
        compiler_params=_cparams("parallel", "arbitrary"),
    )(*([proj] * n_in), w, bias, *params, dy)
    return res[:n_in], res[n_in], res[n_in + 1], res[n_in + 2:]


def _gmlp_chunk(bu, bv, g, b, w0, w1, b0row, b1row):
    u = _gelu(bu)
    vn = _ln64(_gelu(bv), g, b)
    tri = _iota2((CHUNK, CHUNK), 0) >= _iota2((CHUNK, CHUNK), 1)
    m0 = _nn(jnp.where(tri, w0, 0.0), vn) + jnp.broadcast_to(b0row, (CHUNK, CHUNK)).T
    m1 = _nn(jnp.where(tri, w1, 0.0), vn) + jnp.broadcast_to(b1row, (CHUNK, CHUNK)).T
    return u * jnp.where(_lane_lt64((CHUNK, LANES)), m0, m1)


def _gmlp_specs(tm, order):
    def im(f):
        return lambda *ids: f(*order(*ids))
    return dict(
        bu=pl.BlockSpec((tm, LANES), im(lambda j, r: (r, COL_BU + j))),
        bv=pl.BlockSpec((tm, LANES), im(lambda j, r: (r, COL_BV + j))),
        vec=pl.BlockSpec((1, LANES), im(lambda j, r: (0, j))),
        ws=pl.BlockSpec((2, CHUNK, CHUNK), im(lambda j, r: (j, 0, 0))),
        bs=pl.BlockSpec((None, 2, CHUNK), im(lambda j, r: (j, 0, 0))),
        blk=pl.BlockSpec((tm, LANES), im(lambda j, r: (r, j))),
    )


def _gmlp_fwd(proj, ln_g, ln_b, w_s, b_s, name):
    t = proj.shape[0]
    tm = _row_tile(t)
    nch = tm // CHUNK
    sp = _gmlp_specs(tm, lambda r, j: (j, r))

    def body(bu_ref, bv_ref, g_ref, b_ref, ws_ref, bs_ref, o_ref):
        for ci in range(nch):
            rows = pl.ds(ci * CHUNK, CHUNK)
            o_ref[rows, :] = _gmlp_chunk(bu_ref[rows, :], bv_ref[rows, :], g_ref[...], b_ref[...],
                                         ws_ref[0], ws_ref[1], bs_ref[pl.ds(0, 1), :], bs_ref[pl.ds(1, 1), :])

    return pl.pallas_call(
        body, name=name, grid=(t // tm, B_WIDTH // LANES),
        in_specs=[sp["bu"], sp["bv"], sp["vec"], sp["vec"], sp["ws"], sp["bs"]],
        out_specs=sp["blk"], out_shape=jax.ShapeDtypeStruct((t, B_WIDTH), F32),
        compiler_params=_cparams("parallel", "parallel"),
    )(proj, proj, ln_g, ln_b, w_s, b_s.reshape(B_WIDTH // LANES, 2, CHUNK))


def _gmlp_bwd(proj, ln_g, ln_b, w_s, b_s, dy, name):
    t = proj.shape[0]
    tm = _row_tile(t)
    nch = tm // CHUNK
    sp = _gmlp_specs(tm, lambda j, r: (j, r))

    def body(bu_ref, bv_ref, g_ref, b_ref, ws_ref, bs_ref, dy_ref, dbu_ref, dbv_ref, dg_ref, db_ref, dws_ref, dbs_ref):
        @pl.when(pl.program_id(1) == 0)
        def _():
            for r in (dg_ref, db_ref, dws_ref, dbs_ref):
                r[...] = jnp.zeros_like(r)

        for ci in range(nch):
            rows = pl.ds(ci * CHUNK, CHUNK)
            _, vjp = jax.vjp(_gmlp_chunk, bu_ref[rows, :], bv_ref[rows, :], g_ref[...], b_ref[...],
                             ws_ref[0], ws_ref[1], bs_ref[pl.ds(0, 1), :], bs_ref[pl.ds(1, 1), :])
            dbu, dbv, dg, db, dw0, dw1, db0, db1 = vjp(dy_ref[rows, :])
            dbu_ref[rows, :] = dbu.astype(BF16)
            dbv_ref[rows, :] = dbv.astype(BF16)
            dg_ref[...] += dg
            db_ref[...] += db
            dws_ref[0] += dw0
            dws_ref[1] += dw1
            dbs_ref[pl.ds(0, 1), :] += db0
            dbs_ref[pl.ds(1, 1), :] += db1

    nh = B_WIDTH // LANES
    res = pl.pallas_call(
        body, name=name, grid=(nh, t // tm),
        in_specs=[sp["bu"], sp["bv"], sp["vec"], sp["vec"], sp["ws"], sp["bs"], sp["blk"]],
        out_specs=[sp["blk"], sp["blk"], sp["vec"], sp["vec"], sp["ws"], sp["bs"]],
        out_shape=[jax.ShapeDtypeStruct((t, B_WIDTH), BF16), jax.ShapeDtypeStruct((t, B_WIDTH), BF16),
                   jax.ShapeDtypeStruct((1, B_WIDTH), F32), jax.ShapeDtypeStruct((1, B_WIDTH), F32),
                   jax.ShapeDtypeStruct(w_s.shape, F32), jax.ShapeDtypeStruct((nh, 2, CHUNK), F32)],
        compiler_params=_cparams("parallel", "arbitrary"),
    )(proj, proj, ln_g, ln_b, w_s, b_s.reshape(nh, 2, CHUNK), dy)
    dbu, dbv, dg, db, dws, dbs = res
    return dbu, dbv, dg, db, dws, dbs.reshape(b_s.shape)


def _tri_apply(a, lower):
    l = a.shape[0]
    r, c = _iota2((l, l), 0), _iota2((l, l), 1)
    t = jnp.where((r >= c) if lower else (r <= c), 1.0, 0.0).astype(BF16)
    hi = a.astype(BF16)
    r1 = a - hi.astype(F32)
    mid = r1.astype(BF16)
    lo = (r1 - mid.astype(F32)).astype(BF16)
    dn = _DN["nn"]
    return (lax.dot_general(t, hi, dn, preferred_element_type=F32) + lax.dot_general(t, mid, dn, preferred_element_type=F32)
            + lax.dot_general(t, lo, dn, preferred_element_type=F32))


@jax.custom_vjp
def _cumsum_rows(a):
    return _tri_apply(a, True)


_cumsum_rows.defvjp(lambda a: (_tri_apply(a, True), None), lambda _, g: (_tri_apply(g, False),))

SSD_GROUP_HEADS = 8
SSD_GROUP_PAIRS = 4


def _ssd_group(x0, x1, x2, x3, dt_raw, bias, alog, bm, cm, p0, p1, p2, p3):
    xs, prevs = (x0, x1, x2, x3), (p0, p1, p2, p3)
    dt = _softplus(dt_raw + bias)
    a = dt * (-jnp.exp(alog))
    acs = _cumsum_rows(a)
    alast = jnp.sum(a, axis=0, keepdims=True)
    dt_t, acs_t = dt.T, acs.T
    cb = _nt(cm, bm)
    tri = _iota2((CHUNK, CHUNK), 0) >= _iota2((CHUNK, CHUNK), 1)
    lane = _iota2((CHUNK, LANES), 1)
    sub = _iota2((LANES, CHUNK), 0)
    lane1 = _iota2((1, LANES), 1)

    def column(v, i):
        return jnp.broadcast_to(jnp.sum(jnp.where(lane == i, v, 0.0), axis=1, keepdims=True), (CHUNK, LANES))

    def row(vt, i):
        return jnp.broadcast_to(jnp.sum(jnp.where(sub == i, vt, 0.0), axis=0, keepdims=True), (CHUNK, CHUNK))

    heads = []
    for i in range(SSD_GROUP_HEADS):
        col_a = column(acs, i)
        al = jnp.sum(jnp.where(lane1 == i, alast, 0.0), axis=1, keepdims=True)
        m = cb * jnp.exp(jnp.where(tri, col_a - row(acs_t, i), NEG)) * row(dt_t, i)
        heads.append((m, jnp.exp(col_a), column(dt, i) * jnp.exp(al - col_a), jnp.exp(al)))
    lo_lanes = _lane_lt64((CHUNK, LANES))
    lo_rows = _iota2((LANES, SSM_STATE), 0) < HEAD_DIM
    ys, news = [], []
    for j in range(SSD_GROUP_PAIRS):
        (m0, ea0, w0, cd0), (m1, ea1, w1, cd1) = heads[2 * j], heads[2 * j + 1]
        x, prev = xs[j], prevs[j]
        ydiag = jnp.where(lo_lanes, _nn(m0, x), _nn(m1, x))
        yoff = jnp.where(lo_lanes, _nt(cm * ea0, prev), _nt(cm * ea1, prev))
        states = jnp.where(lo_rows, _tn(x, bm * w0), _tn(x, bm * w1))
        ys.append(ydiag + yoff)
        news.append(prev * jnp.where(lo_rows, cd0, cd1) + states)
    return tuple(ys) + tuple(news)


def _ssd2_specs(seq, rev):
    ncs = seq // CHUNK
    gw = SSD_GROUP_PAIRS * LANES
    nblk_x = C_WIDTH // LANES

    def row(s, c):
        return s * ncs + (ncs - 1 - c if rev else c)

    return dict(
        x=pl.BlockSpec((CHUNK, gw), lambda g, s, c: (row(s, c), g)),
        dt=pl.BlockSpec((CHUNK, LANES), lambda g, s, c: (row(s, c), g)),
        vec=pl.BlockSpec((1, LANES), lambda g, s, c: (0, g)),
        bm=pl.BlockSpec((CHUNK, SSM_STATE), lambda g, s, c: (row(s, c), nblk_x + g)),
        cm=pl.BlockSpec((CHUNK, SSM_STATE), lambda g, s, c: (row(s, c), nblk_x + 2 + g)),
        st=pl.BlockSpec((None, SSD_GROUP_PAIRS, LANES, SSM_STATE), lambda g, s, c: (row(s, c), g, 0, 0)),
        ncs=ncs,
    )


def _lane_blocks(ref):
    return [ref[:, pl.ds(j * LANES, LANES)] for j in range(SSD_GROUP_PAIRS)]


def _ssd2_fwd(xbc_act, dt_raw, dt_bias, a_log, seq, name):
    t = xbc_act.shape[0]
    sp = _ssd2_specs(seq, False)

    def body(x_ref, dt_ref, bias_ref, alog_ref, bm_ref, cm_ref, y_ref, prev_ref, state):
        @pl.when(pl.program_id(2) == 0)
        def _():
            state[...] = jnp.zeros_like(state)

        prevs = [state[j] for j in range(SSD_GROUP_PAIRS)]
        for j in range(SSD_GROUP_PAIRS):
            prev_ref[j] = prevs[j]
        res = _ssd_group(*_lane_blocks(x_ref), dt_ref[...], bias_ref[...], alog_ref[...], bm_ref[...], cm_ref[...], *prevs)
        for j in range(SSD_GROUP_PAIRS):
            y_ref[:, pl.ds(j * LANES, LANES)] = res[j]
            state[j] = res[SSD_GROUP_PAIRS + j]

    return pl.pallas_call(
        body, name=name, grid=(2, t // seq, sp["ncs"]),
        in_specs=[sp["x"], sp["dt"], sp["vec"], sp["vec"], sp["bm"], sp["cm"]],
        out_specs=[sp["x"], sp["st"]],
        out_shape=[jax.ShapeDtypeStruct((t, C_WIDTH), F32),
                   jax.ShapeDtypeStruct((t // CHUNK, C_WIDTH // LANES, LANES, SSM_STATE), F32)],
        scratch_shapes=[pltpu.VMEM((SSD_GROUP_PAIRS, LANES, SSM_STATE), F32)],
        compiler_params=_cparams("parallel", "parallel", "arbitrary"),
    )(xbc_act, dt_raw, dt_bias, a_log, xbc_act, xbc_act)


def _ssd2_bwd(xbc_act, dt_raw, dt_bias, a_log, prev_saved, dy, seq, name):
    t = xbc_act.shape[0]
    sp = _ssd2_specs(seq, True)
    npair = SSD_GROUP_PAIRS

    def body(x_ref, dt_ref, bias_ref, alog_ref, bm_ref, cm_ref, prev_ref, dy_ref,
             dx_ref, ddt_ref, dbias_ref, dalog_ref, dbm_ref, dcm_ref, dstate):
        @pl.when(pl.program_id(2) == 0)
        def _():
            dstate[...] = jnp.zeros_like(dstate)

        @pl.when(jnp.logical_and(pl.program_id(1) == 0, pl.program_id(2) == 0))
        def _():
            dbias_ref[...] = jnp.zeros_like(dbias_ref)
            dalog_ref[...] = jnp.zeros_like(dalog_ref)

        _, vjp = jax.vjp(_ssd_group, *_lane_blocks(x_ref), dt_ref[...], bias_ref[...], alog_ref[...], bm_ref[...],
                         cm_ref[...], *[prev_ref[j] for j in range(npair)])
        grads = vjp(tuple(_lane_blocks(dy_ref)) + tuple(dstate[j] for j in range(npair)))
        for j in range(npair):
            dx_ref[:, pl.ds(j * LANES, LANES)] = grads[j]
            dstate[j] = grads[npair + 5 + j]
        ddt_ref[...] = grads[npair].astype(BF16)
        dbias_ref[...] += grads[npair + 1]
        dalog_ref[...] += grads[npair + 2]
        dbm_ref[...] = grads[npair + 3]
        dcm_ref[...] = grads[npair + 4]

    return pl.pallas_call(
        body, name=name, grid=(2, t // seq, sp["ncs"]),
        in_specs=[sp["x"], sp["dt"], sp["vec"], sp["vec"], sp["bm"], sp["cm"], sp["st"], sp["x"]],
        out_specs=[sp["x"], sp["dt"], sp["vec"], sp["vec"], sp["dt"], sp["dt"]],
        out_shape=[jax.ShapeDtypeStruct((t, C_WIDTH), F32), jax.ShapeDtypeStruct((t, 2 * LANES), BF16),
                   jax.ShapeDtypeStruct((1, 2 * LANES), F32), jax.ShapeDtypeStruct((1, 2 * LANES), F32),
                   jax.ShapeDtypeStruct((t, 2 * SSM_STATE), F32), jax.ShapeDtypeStruct((t, 2 * SSM_STATE), F32)],
        scratch_shapes=[pltpu.VMEM((npair, LANES, SSM_STATE), F32)],
        compiler_params=_cparams("parallel", "arbitrary", "arbitrary"),
    )(xbc_act, dt_raw, dt_bias, a_log, xbc_act, xbc_act, prev_saved, dy)


def _ssd2_assemble(dxs_ssd, dxs_skip, dbm, dcm, name):
    t = dxs_ssd.shape[0]
    tm = _row_tile(t)

    def body(a_ref, b_ref, dbm_ref, dcm_ref, o_ref):
        o_ref[:, pl.ds(0, C_WIDTH)] = a_ref[...] + b_ref[...]
        o_ref[:, pl.ds(C_WIDTH, 2 * SSM_STATE)] = dbm_ref[...]
        o_ref[:, pl.ds(C_WIDTH + 2 * SSM_STATE, 2 * SSM_STATE)] = dcm_ref[...]

    wide = pl.BlockSpec((tm, C_WIDTH), lambda i: (i, 0))
    narrow = pl.BlockSpec((tm, 2 * SSM_STATE), lambda i: (i, 0))
    return pl.pallas_call(
        body, name=name, grid=(t // tm,), in_specs=[wide, wide, narrow, narrow],
        out_specs=pl.BlockSpec((tm, D_CONV_C), lambda i: (i, 0)),
        out_shape=jax.ShapeDtypeStruct((t, D_CONV_C), F32),
        compiler_params=_cparams("parallel"),
    )(dxs_ssd, dxs_skip, dbm, dcm)


def _expand_mats():
    head = jnp.arange(LANES)[:, None]
    e64 = (head == (jnp.arange(C_WIDTH)[None, :] // HEAD_DIM)).astype(F32)
    e128 = (head == (jnp.arange(C_HEADS * LANES)[None, :] // LANES)).astype(F32)
    return e64, e128


def _ssd_prep_fn(dt_raw, dt_bias, a_log, e64, e128):
    dt = _softplus(dt_raw + dt_bias)
    a = dt * (-jnp.exp(a_log))
    incl = (_iota2((CHUNK, CHUNK), 0) >= _iota2((CHUNK, CHUNK), 1)).astype(F32)
    acs = _xdot(incl, a)
    alast = _xdot(jnp.ones((CHUNK, CHUNK), F32), a)
    return _xdot(dt, e64), _xdot(acs, e64), _xdot(alast, e64), _xdot(acs, e128)


def _ssd_prep_specs():
    blk = lambda w: pl.BlockSpec((CHUNK, w), lambda i: (i, 0))
    const = lambda r, w: pl.BlockSpec((r, w), lambda i: (0, 0))
    ins = [blk(LANES), const(1, LANES), const(1, LANES), const(LANES, C_WIDTH), const(LANES, C_HEADS * LANES)]
    outs = [blk(C_WIDTH), blk(C_WIDTH), blk(C_WIDTH), blk(C_HEADS * LANES)]
    return ins, outs


def _ssd_prep_fwd(dt_raw, dt_bias, a_log, name):
    t = dt_raw.shape[0]
    e64, e128 = _expand_mats()
    ins, outs = _ssd_prep_specs()

    def body(raw_ref, bias_ref, alog_ref, e64_ref, e128_ref, dt_ref, acs_ref, alast_ref, acs128_ref):
        res = _ssd_prep_fn(raw_ref[...], bias_ref[...], alog_ref[...], e64_ref[...], e128_ref[...])
        for ref, v in zip((dt_ref, acs_ref, alast_ref, acs128_ref), res):
            ref[...] = v

    return pl.pallas_call(
        body, name=name, grid=(t // CHUNK,), in_specs=ins, out_specs=outs,
        out_shape=[jax.ShapeDtypeStruct((t, C_WIDTH), F32)] * 3 + [jax.ShapeDtypeStruct((t, C_HEADS * LANES), F32)],
        compiler_params=_cparams("parallel"),
    )(dt_raw, dt_bias, a_log, e64, e128)


def _ssd_prep_bwd(dt_raw, dt_bias, a_log, d_dt, d_acs, d_alast, d_acs128, name):
    t = dt_raw.shape[0]
    e64, e128 = _expand_mats()
    ins, outs = _ssd_prep_specs()
    vec = pl.BlockSpec((1, LANES), lambda i: (0, 0))

    def body(raw_ref, bias_ref, alog_ref, e64_ref, e128_ref, g0, g1, g2, g3, draw_ref, dbias_ref, dalog_ref):
        @pl.when(pl.program_id(0) == 0)
        def _():
            dbias_ref[...] = jnp.zeros_like(dbias_ref)
            dalog_ref[...] = jnp.zeros_like(dalog_ref)

        e64v, e128v = e64_ref[...], e128_ref[...]
        _, vjp = jax.vjp(lambda r, b, al: _ssd_prep_fn(r, b, al, e64v, e128v),
                         raw_ref[...], bias_ref[...], alog_ref[...])
        draw, dbias, dalog = vjp((g0[...], g1[...], g2[...], g3[...]))
        draw_ref[...] = draw.astype(BF16)
        dbias_ref[...] += dbias
        dalog_ref[...] += dalog

    return pl.pallas_call(
        body, name=name, grid=(t // CHUNK,), in_specs=ins + outs,
        out_specs=[pl.BlockSpec((CHUNK, LANES), lambda i: (i, 0)), vec, vec],
        out_shape=[jax.ShapeDtypeStruct((t, LANES), BF16), jax.ShapeDtypeStruct((1, LANES), F32),
                   jax.ShapeDtypeStruct((1, LANES), F32)],
        compiler_params=_cparams("arbitrary"),
    )(dt_raw, dt_bias, a_log, e64, e128, d_dt, d_acs, d_alast, d_acs128)


def _ssd_chunk(x, dt, acs, alast, col0, col1, bm, cm, prev):
    xdt = x * dt
    cb = _nt(cm, bm)
    tri = _iota2((CHUNK, CHUNK), 0) >= _iota2((CHUNK, CHUNK), 1)
    l0 = jnp.exp(jnp.where(tri, col0 - col0.T, NEG))
    l1 = jnp.exp(jnp.where(tri, col1 - col1.T, NEG))
    ydiag = jnp.where(_lane_lt64((CHUNK, LANES)), _nn(cb * l0, xdt), _nn(cb * l1, xdt))
    states = _tn(xdt * jnp.exp(alast - acs), bm)
    yoff = _nt(cm, prev) * jnp.exp(acs)
    new = prev * jnp.exp(alast).T + states
    return ydiag + yoff, new


def _ssd_specs(seq, rev):
    ncs = seq // CHUNK
    npair = C_WIDTH // LANES

    def row(s, c):
        return s * ncs + (ncs - 1 - c if rev else c)

    return dict(
        x=pl.BlockSpec((CHUNK, LANES), lambda s, j, c: (row(s, c), j)),
        bm=pl.BlockSpec((CHUNK, SSM_STATE), lambda s, j, c: (row(s, c), C_WIDTH // LANES + j // 4)),
        cm=pl.BlockSpec((CHUNK, SSM_STATE), lambda s, j, c: (row(s, c), C_WIDTH // LANES + 2 + j // 4)),
        col=pl.BlockSpec((CHUNK, 2 * LANES), lambda s, j, c: (row(s, c), j)),
        st=pl.BlockSpec((None, None, LANES, SSM_STATE), lambda s, j, c: (row(s, c), j, 0, 0)),
        npair=npair, ncs=ncs,
    )


def _ssd_fwd(xbc_act, dt64, acs64, alast64, acs128, seq, name):
    t = xbc_act.shape[0]
    sp = _ssd_specs(seq, False)

    def body(x_ref, dt_ref, acs_ref, alast_ref, col_ref, bm_ref, cm_ref, y_ref, prev_ref, state):
        @pl.when(pl.program_id(2) == 0)
        def _():
            state[...] = jnp.zeros_like(state)

        prev = state[...]
        prev_ref[...] = prev
        y, new = _ssd_chunk(x_ref[...], dt_ref[...], acs_ref[...], alast_ref[...],
                            col_ref[:, pl.ds(0, LANES)], col_ref[:, pl.ds(LANES, LANES)],
                            bm_ref[...], cm_ref[...], prev)
        y_ref[...] = y
        state[...] = new

    return pl.pallas_call(
        body, name=name, grid=(t // seq, sp["npair"], sp["ncs"]),
        in_specs=[sp["x"], sp["x"], sp["x"], sp["x"], sp["col"], sp["bm"], sp["cm"]],
        out_specs=[sp["x"], sp["st"]],
        out_shape=[jax.ShapeDtypeStruct((t, C_WIDTH), F32),
                   jax.ShapeDtypeStruct((t // CHUNK, sp["npair"], LANES, SSM_STATE), F32)],
        scratch_shapes=[pltpu.VMEM((LANES, SSM_STATE), F32)],
        compiler_params=_cparams("parallel", "parallel", "arbitrary"),
    )(xbc_act, dt64, acs64, alast64, acs128, xbc_act, xbc_act)


def _ssd_bwd(xbc_act, dt64, acs64, alast64, acs128, prev_saved, dy, seq, name):
    t = xbc_act.shape[0]
    sp = _ssd_specs(seq, True)

    def body(x_ref, dt_ref, acs_ref, alast_ref, col_ref, bm_ref, cm_ref, prev_ref, dy_ref,
             dx_ref, ddt_ref, dacs_ref, dalast_ref, dcol_ref, dbc_ref, dstate):
        @pl.when(pl.program_id(2) == 0)
        def _():
            dstate[...] = jnp.zeros_like(dstate)

        _, vjp = jax.vjp(_ssd_chunk, x_ref[...], dt_ref[...], acs_ref[...], alast_ref[...],
                         col_ref[:, pl.ds(0, LANES)], col_ref[:, pl.ds(LANES, LANES)],
                         bm_ref[...], cm_ref[...], prev_ref[...])
        dx, ddt, dacs, dalast, dc0, dc1, dbm, dcm, dprev = vjp((dy_ref[...], dstate[...]))
        dx_ref[...] = dx
        ddt_ref[...] = ddt
        dacs_ref[...] = dacs
        dalast_ref[...] = dalast
        dcol_ref[:, pl.ds(0, LANES)] = dc0
        dcol_ref[:, pl.ds(LANES, LANES)] = dc1
        dbc_ref[:, pl.ds(0, SSM_STATE)] = dbm
        dbc_ref[:, pl.ds(SSM_STATE, SSM_STATE)] = dcm
        dstate[...] = dprev

    wide = jax.ShapeDtypeStruct((t, C_WIDTH), F32)
    return pl.pallas_call(
        body, name=name, grid=(t // seq, sp["npair"], sp["ncs"]),
        in_specs=[sp["x"], sp["x"], sp["x"], sp["x"], sp["col"], sp["bm"], sp["cm"], sp["st"], sp["x"]],
        out_specs=[sp["x"], sp["x"], sp["x"], sp["x"], sp["col"], sp["col"]],
        out_shape=[wide, wide, wide, wide, jax.ShapeDtypeStruct((t, 2 * C_WIDTH), F32),
                   jax.ShapeDtypeStruct((t, 2 * C_WIDTH), F32)],
        scratch_shapes=[pltpu.VMEM((LANES, SSM_STATE), F32)],
        compiler_params=_cparams("parallel", "parallel", "arbitrary"),
    )(xbc_act, dt64, acs64, alast64, acs128, xbc_act, xbc_act, prev_saved, dy)


def _ssd_post_fn(y, xs, z, dskip, g):
    v = (y + dskip * xs) * _silu(z)
    return v * lax.rsqrt(jnp.mean(v * v, axis=-1, keepdims=True) + EPS) * g


def _ssd_post_specs(tm, order):
    gw = C_WIDTH // 2

    def im(f):
        return lambda *ids: f(*order(*ids))
    return dict(
        blk=pl.BlockSpec((tm, gw), im(lambda g, r: (r, g))),
        z=pl.BlockSpec((tm, gw), im(lambda g, r: (r, COL_Z * LANES // gw + g))),
        vec=pl.BlockSpec((1, gw), im(lambda g, r: (0, g))),
    )


def _ssd_post_fwd(y_ssd, xbc_act, proj, dskip64, norm_g, name):
    t = y_ssd.shape[0]
    tm = _row_tile(t)
    sp = _ssd_post_specs(tm, lambda r, g: (g, r))

    def body(y_ref, xs_ref, z_ref, ds_ref, g_ref, o_ref):
        o_ref[...] = _ssd_post_fn(y_ref[...], xs_ref[...], z_ref[...], ds_ref[...], g_ref[...])

    return pl.pallas_call(
        body, name=name, grid=(t // tm, 2),
        in_specs=[sp["blk"], sp["blk"], sp["z"], sp["vec"], sp["vec"]], out_specs=sp["blk"],
        out_shape=jax.ShapeDtypeStruct((t, C_WIDTH), F32),
        compiler_params=_cparams("parallel", "parallel"),
    )(y_ssd, xbc_act, proj, dskip64, norm_g)


def _ssd_post_bwd(y_ssd, xbc_act, proj, dskip64, norm_g, dyc, name):
    t = y_ssd.shape[0]
    tm = _row_tile(t)
    sp = _ssd_post_specs(tm, lambda g, r: (g, r))

    def body(y_ref, xs_ref, z_ref, ds_ref, g_ref, dyc_ref, dy_ref, dxs_ref, dz_ref, dds_ref, dg_ref):
        @pl.when(pl.program_id(1) == 0)
        def _():
            dds_ref[...] = jnp.zeros_like(dds_ref)
            dg_ref[...] = jnp.zeros_like(dg_ref)

        _, vjp = jax.vjp(_ssd_post_fn, y_ref[...], xs_ref[...], z_ref[...], ds_ref[...], g_ref[...])
        dy, dxs, dz, dds, dg = vjp(dyc_ref[...])
        dy_ref[...] = dy
        dxs_ref[...] = dxs
        dz_ref[...] = dz.astype(BF16)
        dds_ref[...] += dds
        dg_ref[...] += dg

    wide = jax.ShapeDtypeStruct((t, C_WIDTH), F32)
    vec = jax.ShapeDtypeStruct((1, C_WIDTH), F32)
    return pl.pallas_call(
        body, name=name, grid=(2, t // tm),
        in_specs=[sp["blk"], sp["blk"], sp["z"], sp["vec"], sp["vec"], sp["blk"]],
        out_specs=[sp["blk"], sp["blk"], sp["blk"], sp["vec"], sp["vec"]],
        out_shape=[wide, wide, jax.ShapeDtypeStruct((t, C_WIDTH), BF16), vec, vec],
        compiler_params=_cparams("parallel", "arbitrary"),
    )(y_ssd, xbc_act, proj, dskip64, norm_g, dyc)


def _ssd_assemble(dxs_ssd, dxs_skip, dbc, name):
    t = dxs_ssd.shape[0]
    tm = _row_tile(t)
    npair = C_WIDTH // LANES

    def body(a_ref, b_ref, dbc_ref, o_ref):
        o_ref[:, pl.ds(0, C_WIDTH)] = a_ref[...] + b_ref[...]
        for grp in range(2):
            for which in range(2):
                acc = jnp.zeros((tm, SSM_STATE), F32)
                for j in range(grp * npair // 2, (grp + 1) * npair // 2):
                    acc = acc + dbc_ref[:, pl.ds((2 * j + which) * SSM_STATE, SSM_STATE)]
                o_ref[:, pl.ds(C_WIDTH + (2 * which + grp) * SSM_STATE, SSM_STATE)] = acc

    return pl.pallas_call(
        body, name=name, grid=(t // tm,),
        in_specs=[pl.BlockSpec((tm, C_WIDTH), lambda i: (i, 0))] * 2 + [pl.BlockSpec((tm, 2 * C_WIDTH), lambda i: (i, 0))],
        out_specs=pl.BlockSpec((tm, D_CONV_C), lambda i: (i, 0)),
        out_shape=jax.ShapeDtypeStruct((t, D_CONV_C), F32),
        compiler_params=_cparams("parallel"),
    )(dxs_ssd, dxs_skip, dbc)


def _pad_taps(w):
    return jnp.pad(w, ((0, CONV_PAD - w.shape[0]), (0, 0)))


def _pad_heads(v):
    return jnp.pad(v, (0, LANES - v.shape[0])).reshape(1, LANES)


def _group_heads(a):
    pad = [(0, 0)] * (a.ndim - 1) + [(0, LANES - SSD_GROUP_HEADS)]
    return jnp.concatenate([jnp.pad(a[..., :SSD_GROUP_HEADS], pad), jnp.pad(a[..., SSD_GROUP_HEADS:], pad)], axis=-1)


def _ungroup_heads(a):
    return jnp.concatenate([a[..., :SSD_GROUP_HEADS], a[..., LANES:LANES + SSD_GROUP_HEADS]], axis=-1)


def _layer_fwd(x, p, seq, li):
    n = f"l{li}_"
    h1 = _rms_fwd(x, p["norm1_g"], n + "rms1")
    proj = _matmul(h1, p["w_main"], mode="nn", name=n + "inproj")
    dt_raw = _matmul(h1, p["w_dt"], mode="nn", name=n + "inproj_dt")
    row = lambda v: v.reshape(1, -1)
    ya = _conv_fwd("a", proj, _pad_taps(p["conv_a_w"]), row(p["conv_a_b"]), (row(p["ln_a_g"]), row(p["ln_a_b"])),
                   seq, n + "conva")
    yb = _gmlp_fwd(proj, row(p["ln_b_g"]), row(p["ln_b_b"]), p["w_spatial"], p["b_spatial"], n + "gmlp")
    xbc_act = _conv_fwd("c", proj, _pad_taps(p["conv_c_w"]), row(p["conv_c_b"]), (), seq, n + "convc")
    y_ssd, prev = _ssd2_fwd(xbc_act, dt_raw, _group_heads(row(p["dt_bias"])), _group_heads(row(p["a_log"])), seq, n + "ssd")
    dskip64 = jnp.repeat(p["d_skip"], HEAD_DIM).reshape(1, C_WIDTH)
    yc = _ssd_post_fwd(y_ssd, xbc_act, proj, dskip64, row(p["norm_c_g"]), n + "ssdpost")
    ycat = jnp.concatenate([ya, yb, yc], axis=1).astype(BF16)
    x1 = _matmul(ycat, p["w_out"], mode="nn", name=n + "outproj", add=x)
    h2 = _rms_fwd(x1, p["norm2_g"], n + "rms2")
    u, act = _matmul(h2, p["w_ff1"], mode="nn", name=n + "ff1", epilogue=_relu2_epilogue, out_dtypes=(F32, BF16))
    x2 = _matmul(act, p["w_ff2"], mode="nn", name=n + "ff2", add=x1)
    saved = dict(x=x, h1=h1, proj=proj, dt_raw=dt_raw, xbc_act=xbc_act, prev=prev, y_ssd=y_ssd, dskip64=dskip64,
                 ycat=ycat, x1=x1, h2=h2, u=u, act=act)
    return x2, saved


def _layer_bwd(dx2, p, s, seq, li):
    n = f"l{li}_b_"
    row = lambda v: v.reshape(1, -1)
    g = {}
    du = _matmul(dx2, p["w_ff2"], mode="nt", name=n + "ff2_dx", epilogue=_relu2_bwd_epilogue, extra=s["u"],
                 out_dtypes=(BF16,))
    g["w_ff2"] = _matmul(s["act"], dx2, mode="tn", name=n + "ff2_dw")
    g["w_ff1"] = _matmul(s["h2"], du, mode="tn", name=n + "ff1_dw")
    dh2 = _matmul(du, p["w_ff1"], mode="nt", name=n + "ff1_dx")
    dx1, g["norm2_g"] = _rms_bwd(s["x1"], p["norm2_g"], dh2, dx2, n + "rms2")
    g["w_out"] = _matmul(s["ycat"], dx1, mode="tn", name=n + "out_dw")
    dycat = _matmul(dx1, p["w_out"], mode="nt", name=n + "out_dx")
    dya, dyb, dyc = dycat[:, :A_WIDTH], dycat[:, A_WIDTH:A_WIDTH + B_WIDTH], dycat[:, A_WIDTH + B_WIDTH:]
    proj = s["proj"]
    (dval, dgate), dwa, dba, (dlag, dlab) = _conv_bwd(
        "a", proj, _pad_taps(p["conv_a_w"]), row(p["conv_a_b"]), (row(p["ln_a_g"]), row(p["ln_a_b"])), dya, seq, n + "conva")
    g["conv_a_w"], g["conv_a_b"], g["ln_a_g"], g["ln_a_b"] = dwa[:CONV_A_K], dba[0], dlag[0], dlab[0]
    dbu, dbv, dlbg, dlbb, g["w_spatial"], g["b_spatial"] = _gmlp_bwd(
        proj, row(p["ln_b_g"]), row(p["ln_b_b"]), p["w_spatial"], p["b_spatial"], dyb, n + "gmlp")
    g["ln_b_g"], g["ln_b_b"] = dlbg[0], dlbb[0]
    dy_ssd, dxs_skip, dz, dds, dncg = _ssd_post_bwd(s["y_ssd"], s["xbc_act"], proj, s["dskip64"], row(p["norm_c_g"]),
                                                    dyc, n + "ssdpost")
    g["norm_c_g"] = dncg[0]
    g["d_skip"] = dds.reshape(C_HEADS, HEAD_DIM).sum(axis=1)
    dxs, ddt_raw, ddtb, dalog, dbm, dcm = _ssd2_bwd(
        s["xbc_act"], s["dt_raw"], _group_heads(row(p["dt_bias"])), _group_heads(row(p["a_log"])), s["prev"], dy_ssd, seq,
        n + "ssd")
    g["dt_bias"], g["a_log"] = _ungroup_heads(ddtb)[0], _ungroup_heads(dalog)[0]
    dconv = _ssd2_assemble(dxs, dxs_skip, dbm, dcm, n + "ssdasm")
    (dxbc,), dwc, dbcv, _ = _conv_bwd("c", proj, _pad_taps(p["conv_c_w"]), row(p["conv_c_b"]), (), dconv, seq, n + "convc")
    g["conv_c_w"], g["conv_c_b"] = dwc[:CONV_C_K], dbcv[0]
    dproj = jnp.concatenate([dval, dgate, dbu, dbv, dz, dxbc], axis=1)
    g["w_main"] = _matmul(s["h1"], dproj, mode="tn", name=n + "in_dw")
    g["w_dt"] = _matmul(s["h1"], ddt_raw, mode="tn", name=n + "indt_dw")
    dh1 = _matmul(dproj, p["w_main"], mode="nt", name=n + "in_dx")
    dh1 = _matmul(ddt_raw, p["w_dt"], mode="nt", name=n + "indt_dx", add=dh1)
    dx, g["norm1_g"] = _rms_bwd(s["x"], p["norm1_g"], dh1, dx1, n + "rms1")
    return dx, g


def _step_local(x, target, layer_params, final_g, seq):
    saved = []
    h = x
    for li, p in enumerate(layer_params):
        h, s = _layer_fwd(h, p, seq, li)
        saved.append(s)
    loss, dx, dgf = _loss_head(h, final_g, target)
    grads = [None] * len(layer_params)
    for li in reversed(range(len(layer_params))):
        dx, grads[li] = _layer_bwd(dx, layer_params[li], saved[li], seq, li)
    return loss, dx, grads, dgf


EW_BLOCK_BYTES = 1 << 20


def _ew(fn, ins, out_dtypes, name, leads=None):
    leads = leads or [None] * len(ins)
    rows, c = ins[0].shape[-2:]
    tr = _pick(rows, [t for t in (2048, 1024, 512, 256, 128, 64, 32, 16, 8) if t * c * 4 <= EW_BLOCK_BYTES])
    n_in = len(ins)

    def spec(lead):
        if lead is None:
            return pl.BlockSpec((tr, c), lambda i: (i, 0))
        return pl.BlockSpec((None, tr, c), functools.partial(lambda i, k: (k, i, 0), k=lead))

    def body(*refs):
        outs = fn(*[r[...].astype(F32) for r in refs[:n_in]])
        for o_ref, o in zip(refs[n_in:], outs):
            o_ref[...] = o.astype(o_ref.dtype)

    return pl.pallas_call(
        body, name=name, grid=(rows // tr,),
        in_specs=[spec(l) for l in leads], out_specs=[spec(None)] * len(out_dtypes),
        out_shape=[jax.ShapeDtypeStruct((rows, c), dt) for dt in out_dtypes],
        compiler_params=_cparams("parallel"),
    )(*ins)


def _adam_fn(w, g, m, v):
    m2 = ADAM_B1 * m + (1.0 - ADAM_B1) * g
    v2 = ADAM_B2 * v + (1.0 - ADAM_B2) * (g * g)
    m_hat = m2 / (1.0 - ADAM_B1 ** ADAM_STEP)
    v_hat = v2 / (1.0 - ADAM_B2 ** ADAM_STEP)
    delta = -ADAM_LR * (m_hat / (jnp.sqrt(v_hat) + ADAM_EPS) + ADAM_WD * w)
    return delta, m2, v2


def _adam(w, g, m, v, name):
    shape = w.shape
    two_d = lambda a: a.reshape(-1, shape[-1])
    outs = _ew(_adam_fn, [two_d(w), two_d(g), two_d(m), two_d(v)], (F32, F32, F32), name)
    return [o.reshape(shape) for o in outs]


_ANY = pl.BlockSpec(memory_space=pl.ANY)


def _mesh_pos():
    return lax.axis_index("x"), lax.axis_index("y"), lax.axis_index("c")


def _peer_chips(x, y):
    return [(1 - x, y), (x, 1 - y), (1 - x, 1 - y)]


def _remote(src, dst, send_sems, recv_sems, sem, to):
    return pltpu.make_async_remote_copy(src_ref=src, dst_ref=dst, send_sem=send_sems.at[sem],
                                        recv_sem=recv_sems.at[sem], device_id=to, device_id_type=MESH)


def _half_rows(n_rows, which):
    half = n_rows // 2
    return pl.ds(pl.multiple_of(which * half, 8), half)


def _comm_call(body, ins, out_shapes, n_sems, name):
    scratch = [pltpu.SemaphoreType.DMA((n_sems,)), pltpu.SemaphoreType.DMA((n_sems,))]
    return pl.pallas_call(
        body, name=name, in_specs=[_ANY] * len(ins), out_specs=[_ANY] * len(out_shapes),
        out_shape=out_shapes, scratch_shapes=scratch,
    )(*ins)


def _gather_weights(big, small, name):
    nb, ns = len(big), len(small)
    n = nb + ns

    def body(*refs):
        ins, outs = refs[:n], refs[n:2 * n]
        send_sems, recv_sems = refs[2 * n:]
        x, y, c = _mesh_pos()
        q = 2 * x + y
        me, sib = (x, y, c), (x, y, 1 - c)
        chips = _peer_chips(x, y)
        rem = functools.partial(_remote, send_sems=send_sems, recv_sems=recv_sems)
        first = []
        for i in range(nb):
            mine = _half_rows(big[i].shape[0], c)
            for k, (px, py) in enumerate(chips):
                first.append(rem(ins[i].at[mine], outs[i].at[q, mine], sem=6 * i + k, to=(px, py, c)))
        for j in range(ns):
            for k, (px, py) in enumerate(chips):
                first.append(rem(ins[nb + j], outs[nb + j].at[q], sem=6 * nb + 3 * j + k, to=(px, py, c)))
        for cp in first:
            cp.start()
        passed = []
        for i in range(nb):
            mine = _half_rows(big[i].shape[0], c)
            for k, (px, py) in enumerate(chips):
                landed = outs[i].at[2 * px + py, mine]
                rem(landed, landed, sem=6 * i + k, to=me).wait_recv()
                fwd = rem(landed, landed, sem=6 * i + 3 + k, to=sib)
                fwd.start()
                passed.append(fwd)
        for i in range(nb):
            other = _half_rows(big[i].shape[0], 1 - c)
            for k, (px, py) in enumerate(chips):
                theirs = outs[i].at[2 * px + py, other]
                rem(theirs, theirs, sem=6 * i + 3 + k, to=me).wait_recv()
        for j in range(ns):
            for k, (px, py) in enumerate(chips):
                dst = outs[nb + j].at[2 * px + py]
                rem(dst, dst, sem=6 * nb + 3 * j + k, to=me).wait_recv()
        for cp in first + passed:
            cp.wait_send()

    out_shapes = [jax.ShapeDtypeStruct((N_CHIPS,) + a.shape, a.dtype) for a in list(big) + list(small)]
    return _comm_call(body, list(big) + list(small), out_shapes, 6 * nb + 3 * ns, name)


def _sibling_other_halves(gs, name):
    n = len(gs)

    def body(*refs):
        ins, outs = refs[:n], refs[n:2 * n]
        send_sems, recv_sems = refs[2 * n:]
        x, y, c = _mesh_pos()
        copies = [_remote(ins[i].at[:, _half_rows(gs[i].shape[1], 1 - c)], outs[i], send_sems, recv_sems, i, (x, y, 1 - c))
                  for i in range(n)]
        for cp in copies:
            cp.start()
        for cp in copies:
            cp.wait()

    out_shapes = [jax.ShapeDtypeStruct((N_CHIPS, g.shape[1] // 2, g.shape[2]), g.dtype) for g in gs]
    return _comm_call(body, list(gs), out_shapes, n, name)


def _chip_scatter(cs, name):
    n = len(cs)

    def body(*refs):
        ins, outs = refs[:n], refs[n:2 * n]
        send_sems, recv_sems = refs[2 * n:]
        x, y, c = _mesh_pos()
        copies = []
        for i in range(n):
            for k, (px, py) in enumerate(_peer_chips(x, y)):
                copies.append(_remote(ins[i].at[2 * px + py], outs[i].at[k], send_sems, recv_sems, 3 * i + k, (px, py, c)))
        for cp in copies:
            cp.start()
        for cp in copies:
            cp.wait()

    out_shapes = [jax.ShapeDtypeStruct((3,) + a.shape[1:], a.dtype) for a in cs]
    return _comm_call(body, list(cs), out_shapes, 3 * n, name)


def _sibling_share(fs, name):
    n = len(fs)

    def body(*refs):
        ins, outs = refs[:n], refs[n:2 * n]
        send_sems, recv_sems = refs[2 * n:]
        x, y, c = _mesh_pos()
        copies = [_remote(ins[i], outs[i], send_sems, recv_sems, i, (x, y, 1 - c)) for i in range(n)]
        for cp in copies:
            cp.start()
        for cp in copies:
            cp.wait()

    out_shapes = [jax.ShapeDtypeStruct(a.shape, a.dtype) for a in fs]
    return _comm_call(body, list(fs), out_shapes, n, name)


def _allgather8(v, name):
    m = v.shape[0]

    def body(v_ref, out_ref, send_sems, recv_sems):
        x, y, c = _mesh_pos()
        me, sib = (x, y, c), (x, y, 1 - c)
        chips = _peer_chips(x, y)
        rem = functools.partial(_remote, send_sems=send_sems, recv_sems=recv_sems)

        def blk(px, py, pc):
            return out_ref.at[4 * px + 2 * py + pc]

        first = [rem(v_ref, blk(*me), sem=0, to=sib)]
        first += [rem(v_ref, blk(*me), sem=1 + k, to=(px, py, c)) for k, (px, py) in enumerate(chips)]
        for cp in first:
            cp.start()
        passed = []
        for k, (px, py) in enumerate(chips):
            landed = blk(px, py, c)
            rem(landed, landed, sem=1 + k, to=me).wait_recv()
            fwd = rem(landed, landed, sem=4 + k, to=sib)
            fwd.start()
            passed.append(fwd)
        rem(blk(*sib), blk(*sib), sem=0, to=me).wait_recv()
        for k, (px, py) in enumerate(chips):
            theirs = blk(px, py, 1 - c)
            rem(theirs, theirs, sem=4 + k, to=me).wait_recv()
        for cp in first + passed:
            cp.wait_send()

    return _comm_call(body, [v], [jax.ShapeDtypeStruct((8, m, LANES), v.dtype)], 7, name)[0]


_WEIGHTS = ["norm1_g", "w_in", "conv_a_w", "conv_a_b", "ln_a_g", "ln_a_b", "ln_b_g", "ln_b_b", "w_spatial", "b_spatial",
            "conv_c_w", "conv_c_b", "dt_bias", "a_log", "d_skip", "norm_c_g", "w_out", "norm2_g", "w_ff1", "w_ff2", "final_g"]
_BIG = ["w_in", "w_out", "w_ff1", "w_ff2"]
_CONV_SHARDED = ["conv_a_w", "conv_c_w"]
_SMALL = [w for w in _WEIGHTS if w not in _BIG and w != "final_g"]
_PACK_ROWS = 512


def _pack(arrs):
    flat = jnp.concatenate([a.reshape(-1) for a in arrs])
    blk = _PACK_ROWS * LANES
    n = flat.shape[0]
    return jnp.pad(flat, (0, -(-n // blk) * blk - n)).reshape(-1, LANES)


def _unpack(packed, shapes):
    flat = packed.reshape(-1)
    out, off = [], 0
    for s in shapes:
        n = math.prod(s)
        out.append(flat[off:off + n].reshape(s))
        off += n
    return out


def _cols_to_chips(a):
    k = a.shape[0]
    return a.reshape(k, N_CHIPS, -1).transpose(1, 0, 2)


def _chips_to_cols(a):
    return a.transpose(1, 0, 2).reshape(a.shape[1], -1)


def _gathered_layer_params(w, li, q):
    own = [w[k][li].astype(BF16) for k in _BIG] + [w[k][li] for k in _CONV_SHARDED]
    gathered = _gather_weights(own[:len(_BIG)], own[len(_BIG):], f"l{li}_gather")
    g_in, g_out, g_ff1, g_ff2, g_ca, g_cc = [lax.dynamic_update_index_in_dim(g, o, q, axis=0)
                                             for g, o in zip(gathered, own)]
    p = {k: w[k][li] for k in _SMALL if k not in _CONV_SHARDED}
    w_in = _chips_to_cols(g_in)
    p["w_main"] = w_in[:, :D_MAIN]
    p["w_dt"] = _group_heads(w_in[:, D_MAIN:])
    p["w_out"] = g_out.reshape(D_MIX, D_MODEL)
    p["w_ff1"] = _chips_to_cols(g_ff1)
    p["w_ff2"] = g_ff2.reshape(D_FF, D_MODEL)
    p["conv_a_w"] = _chips_to_cols(g_ca)
    p["conv_c_w"] = _chips_to_cols(g_cc)
    return p


def _reduce_scatter_layer(g, li, c, q):
    n = f"l{li}_rs_"
    g_in = jnp.concatenate([g["w_main"], _ungroup_heads(g["w_dt"])], axis=1)
    full = [_cols_to_chips(g_in), g["w_out"].reshape(N_CHIPS, -1, D_MODEL), _cols_to_chips(g["w_ff1"]),
            g["w_ff2"].reshape(N_CHIPS, -1, D_MODEL)]
    from_sib = _sibling_other_halves(full, n + "sib")
    chip_f32, chip_bf16 = [], []
    for i, (a, b) in enumerate(zip(full, from_sib)):
        r2, cols = b.shape[1:]
        mine = lax.dynamic_slice_in_dim(a, c * r2, r2, axis=1)
        s32, s16 = _ew(lambda u, v: (u + v, u + v), [mine.reshape(-1, cols), b.reshape(-1, cols)], (F32, BF16),
                       n + f"chipsum{i}")
        chip_f32.append(lax.dynamic_index_in_dim(s32.reshape(b.shape), q, axis=0, keepdims=False))
        chip_bf16.append(s16.reshape(b.shape))
    from_chips = _chip_scatter(chip_bf16, n + "scatter")
    halves = [_ew(lambda o, r0, r1, r2_: (((o + r0) + r1) + r2_,), [own, rb, rb, rb], (F32,), n + f"final{i}",
                  leads=[None, 0, 1, 2])[0] for i, (own, rb) in enumerate(zip(chip_f32, from_chips))]
    from_sib = _sibling_share(halves, n + "share")
    return [jnp.where(c == 0, jnp.concatenate([h, s], axis=0), jnp.concatenate([s, h], axis=0))
            for h, s in zip(halves, from_sib)]


def kernel(x, norm1_g, w_in, conv_a_w, conv_a_b, ln_a_g, ln_a_b, ln_b_g, ln_b_b, w_spatial, b_spatial, conv_c_w, conv_c_b, dt_bias, a_log, d_skip, norm_c_g, w_out, norm2_g, w_ff1, w_ff2, final_g, loss_target, m_norm1_g, m_w_in, m_conv_a_w, m_conv_a_b, m_ln_a_g, m_ln_a_b, m_ln_b_g, m_ln_b_b, m_w_spatial, m_b_spatial, m_conv_c_w, m_conv_c_b, m_dt_bias, m_a_log, m_d_skip, m_norm_c_g, m_w_out, m_norm2_g, m_w_ff1, m_w_ff2, m_final_g, v_norm1_g, v_w_in, v_conv_a_w, v_conv_a_b, v_ln_a_g, v_ln_a_b, v_ln_b_g, v_ln_b_b, v_w_spatial, v_b_spatial, v_conv_c_w, v_conv_c_b, v_dt_bias, v_a_log, v_d_skip, v_norm_c_g, v_w_out, v_norm2_g, v_w_ff1, v_w_ff2, v_final_g):
    given = dict(locals())
    w = {k: given[k] for k in _WEIGHTS}
    m = {k: given["m_" + k] for k in _WEIGHTS}
    v = {k: given["v_" + k] for k in _WEIGHTS}
    depth = w_in.shape[0]
    nseq, seq, d = x.shape
    xi, yi, ci = _mesh_pos()
    q = 2 * xi + yi

    layer_params = [_gathered_layer_params(w, li, q) for li in range(depth)]
    loss, dx, grads, d_final = _step_local(x.reshape(nseq * seq, d), loss_target.reshape(nseq * seq, d), layer_params,
                                           final_g, seq)

    big_grads = {k: [] for k in _BIG}
    for li in range(depth):
        for k, g in zip(_BIG, _reduce_scatter_layer(grads[li], li, ci, q)):
            big_grads[k].append(g)
    grad_out, delta_out, m_out, v_out = {}, {}, {}, {}
    for k in _BIG:
        grad_out[k] = jnp.stack(big_grads[k])
        delta_out[k], m_out[k], v_out[k] = _adam(w[k], grad_out[k], m[k], v[k], "adam_" + k)

    small_shapes = [grads[0][k].shape for k in _SMALL]
    parts = [grads[li][k] for li in range(depth) for k in _SMALL] + [d_final, loss.reshape(1)]
    packed_parts = _pack(parts)
    gathered = lax.dynamic_update_index_in_dim(_allgather8(packed_parts, "small_allgather"), packed_parts,
                                               2 * q + ci, axis=0)

    def sum8(*blocks):
        acc = blocks[0]
        for b in blocks[1:]:
            acc = acc + b
        return (acc,)

    total = _ew(sum8, [gathered] * 8, (F32,), "small_sum", leads=list(range(8)))[0]
    summed = _unpack(total, small_shapes * depth + [d_final.shape, (1,)])
    loss_total = summed[-1][0]
    small_grads = {k: jnp.stack([summed[li * len(_SMALL) + i] for li in range(depth)]) for i, k in enumerate(_SMALL)}
    small_grads["final_g"] = summed[-2]
    for k in _CONV_SHARDED:
        n_shard = w[k].shape[-1]
        small_grads[k] = lax.dynamic_slice_in_dim(small_grads[k], q * n_shard, n_shard, axis=2)
    names = _SMALL + ["final_g"]
    shapes = [w[k].shape for k in names]
    packed = [_pack([src[k] for k in names]) for src in (w, small_grads, m, v)]
    outs = _ew(_adam_fn, packed, (F32, F32, F32), "adam_small")
    for dst, o in zip((delta_out, m_out, v_out), outs):
        for k, a in zip(names, _unpack(o, shapes)):
            dst[k] = a
    for k in names:
        grad_out[k] = small_grads[k]

    return (loss_total, dx.reshape(nseq, seq, d), *[grad_out[k] for k in _WEIGHTS], *[delta_out[k] for k in _WEIGHTS],
            *[m_out[k] for k in _WEIGHTS], *[v_out[k] for k in _WEIGHTS])
```

```python
import functools
import math

import jax
import jax.numpy as jnp
from jax import lax
from jax.experimental import pallas as pl
from jax.experimental.pallas import tpu as pltpu

F32 = jnp.float32
BF16 = jnp.bfloat16
MESH = pl.DeviceIdType.MESH

D_MODEL = 1024
DEPTH = 4
HEAD_DIM = 64
A_WIDTH = 512
B_WIDTH = 512
C_WIDTH = 1024
C_HEADS = 16
CONV_A_K = 31
CONV_C_K = 4
CHUNK = 128
SSM_STATE = 128
D_CONV_C = 1536
D_MAIN = 4608
D_IN_PROJ = 4624
D_MIX = 2048
D_FF = 4096
EPS = 1e-5
NEG = -1e30
LANES = 128
CONV_PAD = 32
N_CHIPS = 4

ADAM_LR = 0.001
ADAM_B1 = 0.9
ADAM_B2 = 0.999
ADAM_EPS = 1e-08
ADAM_WD = 0.01
ADAM_STEP = 10

VMEM_LIMIT = 56 * 1024 * 1024

COL_AVAL, COL_AGATE, COL_BU, COL_BV, COL_Z, COL_XBC = 0, 4, 8, 12, 16, 24


def _cparams(*sem):
    return pltpu.CompilerParams(dimension_semantics=sem, vmem_limit_bytes=VMEM_LIMIT)


_DN = {"nn": (((1,), (0,)), ((), ())), "nt": (((1,), (1,)), ((), ())), "tn": (((0,), (0,)), ((), ()))}


def _dot_raw(a, b, mode):
    return lax.dot_general(a.astype(BF16), b.astype(BF16), _DN[mode], preferred_element_type=F32)


def _make_dot(mode):
    @jax.custom_vjp
    def f(a, b):
        return _dot_raw(a, b, mode)

    def fwd(a, b):
        return _dot_raw(a, b, mode), (a, b)

    def bwd(res, g):
        a, b = res
        if mode == "nn":
            return _dot_raw(g, b, "nt"), _dot_raw(a, g, "tn")
        if mode == "nt":
            return _dot_raw(g, b, "nn"), _dot_raw(g, a, "tn")
        return _dot_raw(b, g, "nt"), _dot_raw(a, g, "nn")

    f.defvjp(fwd, bwd)
    return f


_nn = _make_dot("nn")
_nt = _make_dot("nt")
_tn = _make_dot("tn")


def _xdot(a, e):
    return jnp.dot(a, e, precision=lax.Precision.HIGHEST, preferred_element_type=F32)


def _iota2(shape, dim):
    return lax.broadcasted_iota(jnp.int32, shape, dim)


def _gmean_impl(x):
    n = x.shape[-1]
    same = (_iota2((n, n), 0) < HEAD_DIM) == (_iota2((n, n), 1) < HEAD_DIM)
    p = jnp.where(same, 1.0 / HEAD_DIM, 0.0).astype(BF16)
    hi = x.astype(BF16)
    lo = (x - hi.astype(F32)).astype(BF16)
    dn = _DN["nn"]
    return (lax.dot_general(hi, p, dn, preferred_element_type=F32)
            + lax.dot_general(lo, p, dn, preferred_element_type=F32))


@jax.custom_vjp
def _gmean(x):
    return _gmean_impl(x)


_gmean.defvjp(lambda x: (_gmean_impl(x), None), lambda _, g: (_gmean_impl(g),))


def _sigmoid(x):
    return 1.0 / (1.0 + jnp.exp(-x))


def _silu(x):
    return x * _sigmoid(x)


def _gelu(x):
    return 0.5 * x * (1.0 + lax.erf(x * 0.7071067811865476))


def _softplus(x):
    return jnp.maximum(x, 0.0) + jnp.log(1.0 + jnp.exp(-jnp.abs(x)))


def _rms(x, g):
    return x * lax.rsqrt(jnp.mean(x * x, axis=-1, keepdims=True) + EPS) * g


def _ln64(x, g, b):
    mu = _gmean(x)
    xc = x - mu
    var = _gmean(xc * xc)
    return xc * lax.rsqrt(var + EPS) * g + b


def _lane_lt64(shape):
    return _iota2(shape, 1) < HEAD_DIM


def _pick(n, pref):
    for t in pref:
        if n % t == 0:
            return t
    return n


_UNREAD = pl.BlockSpec(memory_space=pl.ANY)


def _matmul(a, b, *, mode, name, add=None, epilogue=None, extra=None, out_dtypes=(F32,), after=()):
    if mode == "nn":
        (m, k), (_, n) = a.shape, b.shape
    elif mode == "nt":
        (m, k), (n, _) = a.shape, b.shape
    else:
        (k, m), (_, n) = a.shape, b.shape
    tm = _pick(m, (1024, 512, 256, 128))
    tn = _pick(n, (1536, 1024, 512, 256, 128))
    tk = _pick(k, (1536, 1024, 512, 256, 128))
    nk = k // tk
    a_spec = {"nn": pl.BlockSpec((tm, tk), lambda i, j, kk: (i, kk)),
              "nt": pl.BlockSpec((tm, tk), lambda i, j, kk: (i, kk)),
              "tn": pl.BlockSpec((tk, tm), lambda i, j, kk: (kk, i))}[mode]
    b_spec = {"nn": pl.BlockSpec((tk, tn), lambda i, j, kk: (kk, j)),
              "nt": pl.BlockSpec((tn, tk), lambda i, j, kk: (j, kk)),
              "tn": pl.BlockSpec((tk, tn), lambda i, j, kk: (kk, j))}[mode]
    o_spec = pl.BlockSpec((tm, tn), lambda i, j, kk: (i, j))
    ins = [a, b]
    in_specs = [a_spec, b_spec]
    if add is not None:
        ins.append(add)
        in_specs.append(o_spec)
    if extra is not None:
        ins.append(extra)
        in_specs.append(o_spec)
    ins += list(after)
    in_specs += [_UNREAD] * len(after)
    n_out = len(out_dtypes)

    def body(*refs):
        a_ref, b_ref = refs[0], refs[1]
        pos = 2
        add_ref = ex_ref = None
        if add is not None:
            add_ref = refs[pos]
            pos += 1
        if extra is not None:
            ex_ref = refs[pos]
            pos += 1
        pos += len(after)
        o_refs = refs[pos:pos + n_out]

        def finish(acc):
            if add_ref is not None:
                acc = acc + add_ref[...].astype(F32)
            outs = (acc,) if epilogue is None else epilogue(acc, None if ex_ref is None else ex_ref[...])
            for o_ref, o in zip(o_refs, outs):
                o_ref[...] = o.astype(o_ref.dtype)

        part = _dot_raw(a_ref[...], b_ref[...], mode)
        if nk == 1:
            finish(part)
            return
        acc_ref = refs[pos + n_out]
        kk = pl.program_id(2)

        @pl.when(kk == 0)
        def _():
            acc_ref[...] = part

        @pl.when(jnp.logical_and(kk > 0, kk < nk - 1))
        def _():
            acc_ref[...] += part

        @pl.when(kk == nk - 1)
        def _():
            finish(acc_ref[...] + part)

    res = pl.pallas_call(
        body, name=name, grid=(m // tm, n // tn, nk),
        in_specs=in_specs, out_specs=[o_spec] * n_out,
        out_shape=[jax.ShapeDtypeStruct((m, n), dt) for dt in out_dtypes],
        scratch_shapes=[pltpu.VMEM((tm, tn), F32)] if nk > 1 else [],
        compiler_params=_cparams("parallel", "parallel", "arbitrary"),
    )(*ins)
    return res[0] if n_out == 1 else res


def _relu2_epilogue(acc, _):
    r = jnp.maximum(acc, 0.0)
    return acc, r * r


def _relu2_bwd_epilogue(acc, u):
    return (acc * (2.0 * jnp.maximum(u, 0.0)),)


def _row_tile(t):
    return _pick(t, (512, 256, 128))


def _rms_fwd(x, g, name, after=()):
    t, d = x.shape
    tm = _row_tile(t)

    def body(x_ref, g_ref, *rest):
        o_ref = rest[-1]
        o_ref[...] = _rms(x_ref[...], g_ref[...]).astype(BF16)

    return pl.pallas_call(
        body, name=name, grid=(t // tm,),
        in_specs=[pl.BlockSpec((tm, d), lambda i: (i, 0)), pl.BlockSpec((1, d), lambda i: (0, 0))] + [_UNREAD] * len(after),
        out_specs=pl.BlockSpec((tm, d), lambda i: (i, 0)),
        out_shape=jax.ShapeDtypeStruct((t, d), BF16),
        compiler_params=_cparams("parallel"),
    )(x, g.reshape(1, d), *after)


def _rms_bwd(x, g, dh, dres, name):
    t, d = x.shape
    tm = _row_tile(t)

    def body(x_ref, g_ref, dh_ref, dres_ref, dx_ref, dg_ref):
        @pl.when(pl.program_id(0) == 0)
        def _():
            dg_ref[...] = jnp.zeros_like(dg_ref)

        _, vjp = jax.vjp(_rms, x_ref[...], g_ref[...])
        dx, dg = vjp(dh_ref[...].astype(F32))
        dx_ref[...] = dx + dres_ref[...]
        dg_ref[...] += dg

    row = pl.BlockSpec((tm, d), lambda i: (i, 0))
    vec = pl.BlockSpec((1, d), lambda i: (0, 0))
    dx, dg = pl.pallas_call(
        body, name=name, grid=(t // tm,),
        in_specs=[row, vec, row, row], out_specs=[row, vec],
        out_shape=[jax.ShapeDtypeStruct((t, d), F32), jax.ShapeDtypeStruct((1, d), F32)],
        compiler_params=_cparams("arbitrary"),
    )(x, g.reshape(1, d), dh, dres)
    return dx, dg.reshape(d)


def _loss_head(x, g, target):
    t, d = x.shape
    tm = _row_tile(t)

    def loss_fn(xv, gv, tv):
        err = _rms(xv, gv) - tv
        return 0.5 * jnp.sum(jnp.mean(err * err, axis=-1, keepdims=True))

    def body(x_ref, g_ref, t_ref, loss_ref, dx_ref, dg_ref):
        @pl.when(pl.program_id(0) == 0)
        def _():
            dg_ref[...] = jnp.zeros_like(dg_ref)
            loss_ref[...] = jnp.zeros_like(loss_ref)

        tv = t_ref[...]
        val, vjp = jax.vjp(lambda xv, gv: loss_fn(xv, gv, tv), x_ref[...], g_ref[...])
        dx, dg = vjp(jnp.ones((), F32))
        dx_ref[...] = dx
        dg_ref[...] += dg
        loss_ref[...] += jnp.full(loss_ref.shape, val, F32)

    row = pl.BlockSpec((tm, d), lambda i: (i, 0))
    vec = pl.BlockSpec((1, d), lambda i: (0, 0))
    loss, dx, dg = pl.pallas_call(
        body, name="loss_head", grid=(t // tm,),
        in_specs=[row, vec, row], out_specs=[pl.BlockSpec((1, LANES), lambda i: (0, 0)), row, vec],
        out_shape=[jax.ShapeDtypeStruct((1, LANES), F32), jax.ShapeDtypeStruct((t, d), F32),
                   jax.ShapeDtypeStruct((1, d), F32)],
        compiler_params=_cparams("arbitrary"),
    )(x, g.reshape(1, d), target)
    return loss[0, 0], dx, dg.reshape(d)


def _pre_glu(val, gate):
    return val * _sigmoid(gate)


def _pre_id(x):
    return x


def _post_lnsilu(c, g, b):
    return _silu(_ln64(c, g, b))


def _post_silu(c):
    return _silu(c)


def _conv_cfg(kind):
    if kind == "a":
        return dict(k=CONV_A_K, pre=_pre_glu, post=_post_lnsilu, n_in=2, n_par=2, nblk=A_WIDTH // LANES,
                    cols=(COL_AVAL, COL_AGATE))
    return dict(k=CONV_C_K, pre=_pre_id, post=_post_silu, n_in=1, n_par=0, nblk=D_CONV_C // LANES,
                cols=(COL_XBC,))


def _conv_fwd(kind, proj, w, bias, params, seq, name):
    cfg = _conv_cfg(kind)
    kt, pre, post, n_in = cfg["k"], cfg["pre"], cfg["post"], cfg["n_in"]
    t = proj.shape[0]
    nseq = t // seq
    c = cfg["nblk"] * LANES
    rt = min(256, seq)
    nrt = seq // rt
    off0 = CONV_PAD - (kt - 1)

    def body(*refs):
        in_refs = refs[:n_in]
        w_ref, b_ref = refs[n_in], refs[n_in + 1]
        par_refs = refs[n_in + 2:n_in + 2 + cfg["n_par"]]
        o_ref, hpad = refs[n_in + 2 + cfg["n_par"]:]
        hpad[pl.ds(0, CONV_PAD), :] = jnp.zeros((CONV_PAD, LANES), F32)
        for r in range(nrt):
            hpad[pl.ds(CONV_PAD + r * rt, rt), :] = pre(*[x[pl.ds(r * rt, rt), :] for x in in_refs])
        pars = [p[...] for p in par_refs]
        for r in range(nrt):
            acc = jnp.broadcast_to(b_ref[...], (rt, LANES))
            for k in range(kt):
                acc = acc + w_ref[pl.ds(k, 1), :] * hpad[pl.ds(off0 + k + r * rt, rt), :]
            o_ref[pl.ds(r * rt, rt), :] = post(acc, *pars)

    in_specs = [pl.BlockSpec((seq, LANES), functools.partial(lambda s, j, col: (s, col + j), col=col))
                for col in cfg["cols"]]
    vec = pl.BlockSpec((1, LANES), lambda s, j: (0, j))
    in_specs += [pl.BlockSpec((CONV_PAD, LANES), lambda s, j: (0, j)), vec] + [vec] * cfg["n_par"]
    return pl.pallas_call(
        body, name=name, grid=(nseq, cfg["nblk"]),
        in_specs=in_specs, out_specs=pl.BlockSpec((seq, LANES), lambda s, j: (s, j)),
        out_shape=jax.ShapeDtypeStruct((t, c), F32),
        scratch_shapes=[pltpu.VMEM((seq + CONV_PAD, LANES), F32)],
        compiler_params=_cparams("parallel", "parallel"),
    )(*([proj] * n_in), w, bias, *params)


def _conv_bwd(kind, proj, w, bias, params, dy, seq, name):
    cfg = _conv_cfg(kind)
    kt, pre, post, n_in, n_par = cfg["k"], cfg["pre"], cfg["post"], cfg["n_in"], cfg["n_par"]
    t = proj.shape[0]
    nseq = t // seq
    c = cfg["nblk"] * LANES
    rt = min(256, seq)
    nrt = seq // rt
    off0 = CONV_PAD - (kt - 1)

    def body(*refs):
        in_refs = refs[:n_in]
        w_ref, b_ref = refs[n_in], refs[n_in + 1]
        par_refs = refs[n_in + 2:n_in + 2 + n_par]
        pos = n_in + 2 + n_par
        dy_ref = refs[pos]
        din_refs = refs[pos + 1:pos + 1 + n_in]
        dw_ref, db_ref = refs[pos + 1 + n_in], refs[pos + 2 + n_in]
        dpar_refs = refs[pos + 3 + n_in:pos + 3 + n_in + n_par]
        hpad, dcpad = refs[pos + 3 + n_in + n_par:]

        @pl.when(pl.program_id(1) == 0)
        def _():
            dw_ref[...] = jnp.zeros_like(dw_ref)
            db_ref[...] = jnp.zeros_like(db_ref)
            for r in dpar_refs:
                r[...] = jnp.zeros_like(r)

        hpad[pl.ds(0, CONV_PAD), :] = jnp.zeros((CONV_PAD, LANES), F32)
        dcpad[pl.ds(seq, CONV_PAD), :] = jnp.zeros((CONV_PAD, LANES), F32)
        for r in range(nrt):
            hpad[pl.ds(CONV_PAD + r * rt, rt), :] = pre(*[x[pl.ds(r * rt, rt), :] for x in in_refs])
        pars = [p[...] for p in par_refs]
        for r in range(nrt):
            acc = jnp.broadcast_to(b_ref[...], (rt, LANES))
            for k in range(kt):
                acc = acc + w_ref[pl.ds(k, 1), :] * hpad[pl.ds(off0 + k + r * rt, rt), :]
            _, vjp = jax.vjp(post, acc, *pars)
            grads = vjp(dy_ref[pl.ds(r * rt, rt), :])
            dcpad[pl.ds(r * rt, rt), :] = grads[0]
            db_ref[...] += jnp.sum(grads[0], axis=0, keepdims=True)
            for ref, gpar in zip(dpar_refs, grads[1:]):
                ref[...] += gpar
        for r in range(nrt):
            dh = jnp.zeros((rt, LANES), F32)
            for k in range(kt):
                dh = dh + w_ref[pl.ds(k, 1), :] * dcpad[pl.ds(r * rt + kt - 1 - k, rt), :]
            _, vjp = jax.vjp(pre, *[x[pl.ds(r * rt, rt), :] for x in in_refs])
            for ref, gin in zip(din_refs, vjp(dh)):
                ref[pl.ds(r * rt, rt), :] = gin.astype(ref.dtype)
        for k in range(kt):
            s = jnp.zeros((1, LANES), F32)
            for r in range(nrt):
                s = s + jnp.sum(dcpad[pl.ds(r * rt, rt), :] * hpad[pl.ds(off0 + k + r * rt, rt), :],
                                axis=0, keepdims=True)
            dw_ref[pl.ds(k, 1), :] += s

    in_specs = [pl.BlockSpec((seq, LANES), functools.partial(lambda j, s, col: (s, col + j), col=col))
                for col in cfg["cols"]]
    vec = pl.BlockSpec((1, LANES), lambda j, s: (0, j))
    wspec = pl.BlockSpec((CONV_PAD, LANES), lambda j, s: (0, j))
    blk = pl.BlockSpec((seq, LANES), lambda j, s: (s, j))
    in_specs += [wspec, vec] + [vec] * n_par + [blk]
    out_specs = [blk] * n_in + [wspec, vec] + [vec] * n_par
    out_shape = ([jax.ShapeDtypeStruct((t, c), BF16)] * n_in
                 + [jax.ShapeDtypeStruct((CONV_PAD, c), F32), jax.ShapeDtypeStruct((1, c), F32)]
                 + [jax.ShapeDtypeStruct((1, c), F32)] * n_par)
    res = pl.pallas_call(
        body, name=name, grid=(cfg["nblk"], nseq),
        in_specs=in_specs, out_specs=out_specs, out_shape=out_shape,
        scratch_shapes=[pltpu.VMEM((seq + CONV_PAD, LANES), F32), pltpu.VMEM((seq + CONV_PAD, LANES), F32)],
        compiler_params=_cparams("parallel", "arbitrary"),
    )(*([proj] * n_in), w, bias, *params, dy)
    return res[:n_in], res[n_in], res[n_in + 1], res[n_in + 2:]


def _gmlp_chunk(bu, bv, g, b, w0, w1, b0row, b1row):
    u = _gelu(bu)
    vn = _ln64(_gelu(bv), g, b)
    tri = _iota2((CHUNK, CHUNK), 0) >= _iota2((CHUNK, CHUNK), 1)
    m0 = _nn(jnp.where(tri, w0, 0.0), vn) + jnp.broadcast_to(b0row, (CHUNK, CHUNK)).T
    m1 = _nn(jnp.where(tri, w1, 0.0), vn) + jnp.broadcast_to(b1row, (CHUNK, CHUNK)).T
    return u * jnp.where(_lane_lt64((CHUNK, LANES)), m0, m1)


def _gmlp_specs(tm, order):
    def im(f):
        return lambda *ids: f(*order(*ids))
    return dict(
        bu=pl.BlockSpec((tm, LANES), im(lambda j, r: (r, COL_BU + j))),
        bv=pl.BlockSpec((tm, LANES), im(lambda j, r: (r, COL_BV + j))),
        vec=pl.BlockSpec((1, LANES), im(lambda j, r: (0, j))),
        ws=pl.BlockSpec((2, CHUNK, CHUNK), im(lambda j, r: (j, 0, 0))),
        bs=pl.BlockSpec((None, 2, CHUNK), im(lambda j, r: (j, 0, 0))),
        blk=pl.BlockSpec((tm, LANES), im(lambda j, r: (r, j))),
    )


def _gmlp_fwd(proj, ln_g, ln_b, w_s, b_s, name):
    t = proj.shape[0]
    tm = _row_tile(t)
    nch = tm // CHUNK
    sp = _gmlp_specs(tm, lambda r, j: (j, r))

    def body(bu_ref, bv_ref, g_ref, b_ref, ws_ref, bs_ref, o_ref):
        for ci in range(nch):
            rows = pl.ds(ci * CHUNK, CHUNK)
            o_ref[rows, :] = _gmlp_chunk(bu_ref[rows, :], bv_ref[rows, :], g_ref[...], b_ref[...],
                                         ws_ref[0], ws_ref[1], bs_ref[pl.ds(0, 1), :], bs_ref[pl.ds(1, 1), :])

    return pl.pallas_call(
        body, name=name, grid=(t // tm, B_WIDTH // LANES),
        in_specs=[sp["bu"], sp["bv"], sp["vec"], sp["vec"], sp["ws"], sp["bs"]],
        out_specs=sp["blk"], out_shape=jax.ShapeDtypeStruct((t, B_WIDTH), F32),
        compiler_params=_cparams("parallel", "parallel"),
    )(proj, proj, ln_g, ln_b, w_s, b_s.reshape(B_WIDTH // LANES, 2, CHUNK))


def _gmlp_bwd(proj, ln_g, ln_b, w_s, b_s, dy, name):
    t = proj.shape[0]
    tm = _row_tile(t)
    nch = tm // CHUNK
    sp = _gmlp_specs(tm, lambda j, r: (j, r))

    def body(bu_ref, bv_ref, g_ref, b_ref, ws_ref, bs_ref, dy_ref, dbu_ref, dbv_ref, dg_ref, db_ref, dws_ref, dbs_ref):
        @pl.when(pl.program_id(1) == 0)
        def _():
            for r in (dg_ref, db_ref, dws_ref, dbs_ref):
                r[...] = jnp.zeros_like(r)

        for ci in range(nch):
            rows = pl.ds(ci * CHUNK, CHUNK)
            _, vjp = jax.vjp(_gmlp_chunk, bu_ref[rows, :], bv_ref[rows, :], g_ref[...], b_ref[...],
                             ws_ref[0], ws_ref[1], bs_ref[pl.ds(0, 1), :], bs_ref[pl.ds(1, 1), :])
            dbu, dbv, dg, db, dw0, dw1, db0, db1 = vjp(dy_ref[rows, :])
            dbu_ref[rows, :] = dbu.astype(BF16)
            dbv_ref[rows, :] = dbv.astype(BF16)
            dg_ref[...] += dg
            db_ref[...] += db
            dws_ref[0] += dw0
            dws_ref[1] += dw1
            dbs_ref[pl.ds(0, 1), :] += db0
            dbs_ref[pl.ds(1, 1), :] += db1

    nh = B_WIDTH // LANES
    res = pl.pallas_call(
        body, name=name, grid=(nh, t // tm),
        in_specs=[sp["bu"], sp["bv"], sp["vec"], sp["vec"], sp["ws"], sp["bs"], sp["blk"]],
        out_specs=[sp["blk"], sp["blk"], sp["vec"], sp["vec"], sp["ws"], sp["bs"]],
        out_shape=[jax.ShapeDtypeStruct((t, B_WIDTH), BF16), jax.ShapeDtypeStruct((t, B_WIDTH), BF16),
                   jax.ShapeDtypeStruct((1, B_WIDTH), F32), jax.ShapeDtypeStruct((1, B_WIDTH), F32),
                   jax.ShapeDtypeStruct(w_s.shape, F32), jax.ShapeDtypeStruct((nh, 2, CHUNK), F32)],
        compiler_params=_cparams("parallel", "arbitrary"),
    )(proj, proj, ln_g, ln_b, w_s, b_s.reshape(nh, 2, CHUNK), dy)
    dbu, dbv, dg, db, dws, dbs = res
    return dbu, dbv, dg, db, dws, dbs.reshape(b_s.shape)


def _tri_apply(a, lower):
    l = a.shape[0]
    r, c = _iota2((l, l), 0), _iota2((l, l), 1)
    t = jnp.where((r >= c) if lower else (r <= c), 1.0, 0.0).astype(BF16)
    hi = a.astype(BF16)
    r1 = a - hi.astype(F32)
    mid = r1.astype(BF16)
    lo = (r1 - mid.astype(F32)).astype(BF16)
    dn = _DN["nn"]
    return (lax.dot_general(t, hi, dn, preferred_element_type=F32) + lax.dot_general(t, mid, dn, preferred_element_type=F32)
            + lax.dot_general(t, lo, dn, preferred_element_type=F32))


@jax.custom_vjp
def _cumsum_rows(a):
    return _tri_apply(a, True)


_cumsum_rows.defvjp(lambda a: (_tri_apply(a, True), None), lambda _, g: (_tri_apply(g, False),))

SSD_GROUP_HEADS = 8
SSD_GROUP_PAIRS = 4


def _ssd_group(x0, x1, x2, x3, dt_raw, bias, alog, bm, cm, p0, p1, p2, p3):
    xs, prevs = (x0, x1, x2, x3), (p0, p1, p2, p3)
    dt = _softplus(dt_raw + bias)
    a = dt * (-jnp.exp(alog))
    acs = _cumsum_rows(a)
    alast = jnp.sum(a, axis=0, keepdims=True)
    dt_t, acs_t = dt.T, acs.T
    cb = _nt(cm, bm)
    tri = _iota2((CHUNK, CHUNK), 0) >= _iota2((CHUNK, CHUNK), 1)
    lane = _iota2((CHUNK, LANES), 1)
    sub = _iota2((LANES, CHUNK), 0)
    lane1 = _iota2((1, LANES), 1)

    def column(v, i):
        return jnp.broadcast_to(jnp.sum(jnp.where(lane == i, v, 0.0), axis=1, keepdims=True), (CHUNK, LANES))

    def row(vt, i):
        return jnp.broadcast_to(jnp.sum(jnp.where(sub == i, vt, 0.0), axis=0, keepdims=True), (CHUNK, CHUNK))

    heads = []
    for i in range(SSD_GROUP_HEADS):
        col_a = column(acs, i)
        al = jnp.sum(jnp.where(lane1 == i, alast, 0.0), axis=1, keepdims=True)
        m = cb * jnp.exp(jnp.where(tri, col_a - row(acs_t, i), NEG)) * row(dt_t, i)
        heads.append((m, jnp.exp(col_a), column(dt, i) * jnp.exp(al - col_a), jnp.exp(al)))
    lo_lanes = _lane_lt64((CHUNK, LANES))
    lo_rows = _iota2((LANES, SSM_STATE), 0) < HEAD_DIM
    ys, news = [], []
    for j in range(SSD_GROUP_PAIRS):
        (m0, ea0, w0, cd0), (m1, ea1, w1, cd1) = heads[2 * j], heads[2 * j + 1]
        x, prev = xs[j], prevs[j]
        ydiag = jnp.where(lo_lanes, _nn(m0, x), _nn(m1, x))
        yoff = jnp.where(lo_lanes, _nt(cm * ea0, prev), _nt(cm * ea1, prev))
        states = jnp.where(lo_rows, _tn(x, bm * w0), _tn(x, bm * w1))
        ys.append(ydiag + yoff)
        news.append(prev * jnp.where(lo_rows, cd0, cd1) + states)
    return tuple(ys) + tuple(news)


def _ssd2_specs(seq, rev):
    ncs = seq // CHUNK
    gw = SSD_GROUP_PAIRS * LANES
    nblk_x = C_WIDTH // LANES

    def row(s, c):
        return s * ncs + (ncs - 1 - c if rev else c)

    return dict(
        x=pl.BlockSpec((CHUNK, gw), lambda g, s, c: (row(s, c), g)),
        dt=pl.BlockSpec((CHUNK, LANES), lambda g, s, c: (row(s, c), g)),
        vec=pl.BlockSpec((1, LANES), lambda g, s, c: (0, g)),
        bm=pl.BlockSpec((CHUNK, SSM_STATE), lambda g, s, c: (row(s, c), nblk_x + g)),
        cm=pl.BlockSpec((CHUNK, SSM_STATE), lambda g, s, c: (row(s, c), nblk_x + 2 + g)),
        st=pl.BlockSpec((None, SSD_GROUP_PAIRS, LANES, SSM_STATE), lambda g, s, c: (row(s, c), g, 0, 0)),
        ncs=ncs,
    )


def _lane_blocks(ref):
    return [ref[:, pl.ds(j * LANES, LANES)] for j in range(SSD_GROUP_PAIRS)]


def _ssd2_fwd(xbc_act, dt_raw, dt_bias, a_log, seq, name):
    t = xbc_act.shape[0]
    sp = _ssd2_specs(seq, False)

    def body(x_ref, dt_ref, bias_ref, alog_ref, bm_ref, cm_ref, y_ref, prev_ref, state):
        @pl.when(pl.program_id(2) == 0)
        def _():
            state[...] = jnp.zeros_like(state)

        prevs = [state[j] for j in range(SSD_GROUP_PAIRS)]
        for j in range(SSD_GROUP_PAIRS):
            prev_ref[j] = prevs[j]
        res = _ssd_group(*_lane_blocks(x_ref), dt_ref[...], bias_ref[...], alog_ref[...], bm_ref[...], cm_ref[...], *prevs)
        for j in range(SSD_GROUP_PAIRS):
            y_ref[:, pl.ds(j * LANES, LANES)] = res[j]
            state[j] = res[SSD_GROUP_PAIRS + j]

    return pl.pallas_call(
        body, name=name, grid=(2, t // seq, sp["ncs"]),
        in_specs=[sp["x"], sp["dt"], sp["vec"], sp["vec"], sp["bm"], sp["cm"]],
        out_specs=[sp["x"], sp["st"]],
        out_shape=[jax.ShapeDtypeStruct((t, C_WIDTH), F32),
                   jax.ShapeDtypeStruct((t // CHUNK, C_WIDTH // LANES, LANES, SSM_STATE), F32)],
        scratch_shapes=[pltpu.VMEM((SSD_GROUP_PAIRS, LANES, SSM_STATE), F32)],
        compiler_params=_cparams("parallel", "parallel", "arbitrary"),
    )(xbc_act, dt_raw, dt_bias, a_log, xbc_act, xbc_act)


def _ssd2_bwd(xbc_act, dt_raw, dt_bias, a_log, prev_saved, dy, seq, name):
    t = xbc_act.shape[0]
    sp = _ssd2_specs(seq, True)
    npair = SSD_GROUP_PAIRS

    def body(x_ref, dt_ref, bias_ref, alog_ref, bm_ref, cm_ref, prev_ref, dy_ref,
             dx_ref, ddt_ref, dbias_ref, dalog_ref, dbm_ref, dcm_ref, dstate):
        @pl.when(pl.program_id(2) == 0)
        def _():
            dstate[...] = jnp.zeros_like(dstate)

        @pl.when(jnp.logical_and(pl.program_id(1) == 0, pl.program_id(2) == 0))
        def _():
            dbias_ref[...] = jnp.zeros_like(dbias_ref)
            dalog_ref[...] = jnp.zeros_like(dalog_ref)

        _, vjp = jax.vjp(_ssd_group, *_lane_blocks(x_ref), dt_ref[...], bias_ref[...], alog_ref[...], bm_ref[...],
                         cm_ref[...], *[prev_ref[j] for j in range(npair)])
        grads = vjp(tuple(_lane_blocks(dy_ref)) + tuple(dstate[j] for j in range(npair)))
        for j in range(npair):
            dx_ref[:, pl.ds(j * LANES, LANES)] = grads[j]
            dstate[j] = grads[npair + 5 + j]
        ddt_ref[...] = grads[npair].astype(BF16)
        dbias_ref[...] += grads[npair + 1]
        dalog_ref[...] += grads[npair + 2]
        dbm_ref[...] = grads[npair + 3]
        dcm_ref[...] = grads[npair + 4]

    return pl.pallas_call(
        body, name=name, grid=(2, t // seq, sp["ncs"]),
        in_specs=[sp["x"], sp["dt"], sp["vec"], sp["vec"], sp["bm"], sp["cm"], sp["st"], sp["x"]],
        out_specs=[sp["x"], sp["dt"], sp["vec"], sp["vec"], sp["dt"], sp["dt"]],
        out_shape=[jax.ShapeDtypeStruct((t, C_WIDTH), F32), jax.ShapeDtypeStruct((t, 2 * LANES), BF16),
                   jax.ShapeDtypeStruct((1, 2 * LANES), F32), jax.ShapeDtypeStruct((1, 2 * LANES), F32),
                   jax.ShapeDtypeStruct((t, 2 * SSM_STATE), F32), jax.ShapeDtypeStruct((t, 2 * SSM_STATE), F32)],
        scratch_shapes=[pltpu.VMEM((npair, LANES, SSM_STATE), F32)],
        compiler_params=_cparams("parallel", "arbitrary", "arbitrary"),
    )(xbc_act, dt_raw, dt_bias, a_log, xbc_act, xbc_act, prev_saved, dy)


def _ssd2_assemble(dxs_ssd, dxs_skip, dbm, dcm, name):
    t = dxs_ssd.shape[0]
    tm = _row_tile(t)

    def body(a_ref, b_ref, dbm_ref, dcm_ref, o_ref):
        o_ref[:, pl.ds(0, C_WIDTH)] = a_ref[...] + b_ref[...]
        o_ref[:, pl.ds(C_WIDTH, 2 * SSM_STATE)] = dbm_ref[...]
        o_ref[:, pl.ds(C_WIDTH + 2 * SSM_STATE, 2 * SSM_STATE)] = dcm_ref[...]

    wide = pl.BlockSpec((tm, C_WIDTH), lambda i: (i, 0))
    narrow = pl.BlockSpec((tm, 2 * SSM_STATE), lambda i: (i, 0))
    return pl.pallas_call(
        body, name=name, grid=(t // tm,), in_specs=[wide, wide, narrow, narrow],
        out_specs=pl.BlockSpec((tm, D_CONV_C), lambda i: (i, 0)),
        out_shape=jax.ShapeDtypeStruct((t, D_CONV_C), F32),
        compiler_params=_cparams("parallel"),
    )(dxs_ssd, dxs_skip, dbm, dcm)


def _expand_mats():
    head = jnp.arange(LANES)[:, None]
    e64 = (head == (jnp.arange(C_WIDTH)[None, :] // HEAD_DIM)).astype(F32)
    e128 = (head == (jnp.arange(C_HEADS * LANES)[None, :] // LANES)).astype(F32)
    return e64, e128


def _ssd_prep_fn(dt_raw, dt_bias, a_log, e64, e128):
    dt = _softplus(dt_raw + dt_bias)
    a = dt * (-jnp.exp(a_log))
    incl = (_iota2((CHUNK, CHUNK), 0) >= _iota2((CHUNK, CHUNK), 1)).astype(F32)
    acs = _xdot(incl, a)
    alast = _xdot(jnp.ones((CHUNK, CHUNK), F32), a)
    return _xdot(dt, e64), _xdot(acs, e64), _xdot(alast, e64), _xdot(acs, e128)


def _ssd_prep_specs():
    blk = lambda w: pl.BlockSpec((CHUNK, w), lambda i: (i, 0))
    const = lambda r, w: pl.BlockSpec((r, w), lambda i: (0, 0))
    ins = [blk(LANES), const(1, LANES), const(1, LANES), const(LANES, C_WIDTH), const(LANES, C_HEADS * LANES)]
    outs = [blk(C_WIDTH), blk(C_WIDTH), blk(C_WIDTH), blk(C_HEADS * LANES)]
    return ins, outs


def _ssd_prep_fwd(dt_raw, dt_bias, a_log, name):
    t = dt_raw.shape[0]
    e64, e128 = _expand_mats()
    ins, outs = _ssd_prep_specs()

    def body(raw_ref, bias_ref, alog_ref, e64_ref, e128_ref, dt_ref, acs_ref, alast_ref, acs128_ref):
        res = _ssd_prep_fn(raw_ref[...], bias_ref[...], alog_ref[...], e64_ref[...], e128_ref[...])
        for ref, v in zip((dt_ref, acs_ref, alast_ref, acs128_ref), res):
            ref[...] = v

    return pl.pallas_call(
        body, name=name, grid=(t // CHUNK,), in_specs=ins, out_specs=outs,
        out_shape=[jax.ShapeDtypeStruct((t, C_WIDTH), F32)] * 3 + [jax.ShapeDtypeStruct((t, C_HEADS * LANES), F32)],
        compiler_params=_cparams("parallel"),
    )(dt_raw, dt_bias, a_log, e64, e128)


def _ssd_prep_bwd(dt_raw, dt_bias, a_log, d_dt, d_acs, d_alast, d_acs128, name):
    t = dt_raw.shape[0]
    e64, e128 = _expand_mats()
    ins, outs = _ssd_prep_specs()
    vec = pl.BlockSpec((1, LANES), lambda i: (0, 0))

    def body(raw_ref, bias_ref, alog_ref, e64_ref, e128_ref, g0, g1, g2, g3, draw_ref, dbias_ref, dalog_ref):
        @pl.when(pl.program_id(0) == 0)
        def _():
            dbias_ref[...] = jnp.zeros_like(dbias_ref)
            dalog_ref[...] = jnp.zeros_like(dalog_ref)

        e64v, e128v = e64_ref[...], e128_ref[...]
        _, vjp = jax.vjp(lambda r, b, al: _ssd_prep_fn(r, b, al, e64v, e128v),
                         raw_ref[...], bias_ref[...], alog_ref[...])
        draw, dbias, dalog = vjp((g0[...], g1[...], g2[...], g3[...]))
        draw_ref[...] = draw.astype(BF16)
        dbias_ref[...] += dbias
        dalog_ref[...] += dalog

    return pl.pallas_call(
        body, name=name, grid=(t // CHUNK,), in_specs=ins + outs,
        out_specs=[pl.BlockSpec((CHUNK, LANES), lambda i: (i, 0)), vec, vec],
        out_shape=[jax.ShapeDtypeStruct((t, LANES), BF16), jax.ShapeDtypeStruct((1, LANES), F32),
                   jax.ShapeDtypeStruct((1, LANES), F32)],
        compiler_params=_cparams("arbitrary"),
    )(dt_raw, dt_bias, a_log, e64, e128, d_dt, d_acs, d_alast, d_acs128)


def _ssd_chunk(x, dt, acs, alast, col0, col1, bm, cm, prev):
    xdt = x * dt
    cb = _nt(cm, bm)
    tri = _iota2((CHUNK, CHUNK), 0) >= _iota2((CHUNK, CHUNK), 1)
    l0 = jnp.exp(jnp.where(tri, col0 - col0.T, NEG))
    l1 = jnp.exp(jnp.where(tri, col1 - col1.T, NEG))
    ydiag = jnp.where(_lane_lt64((CHUNK, LANES)), _nn(cb * l0, xdt), _nn(cb * l1, xdt))
    states = _tn(xdt * jnp.exp(alast - acs), bm)
    yoff = _nt(cm, prev) * jnp.exp(acs)
    new = prev * jnp.exp(alast).T + states
    return ydiag + yoff, new


def _ssd_specs(seq, rev):
    ncs = seq // CHUNK
    npair = C_WIDTH // LANES

    def row(s, c):
        return s * ncs + (ncs - 1 - c if rev else c)

    return dict(
        x=pl.BlockSpec((CHUNK, LANES), lambda s, j, c: (row(s, c), j)),
        bm=pl.BlockSpec((CHUNK, SSM_STATE), lambda s, j, c: (row(s, c), C_WIDTH // LANES + j // 4)),
        cm=pl.BlockSpec((CHUNK, SSM_STATE), lambda s, j, c: (row(s, c), C_WIDTH // LANES + 2 + j // 4)),
        col=pl.BlockSpec((CHUNK, 2 * LANES), lambda s, j, c: (row(s, c), j)),
        st=pl.BlockSpec((None, None, LANES, SSM_STATE), lambda s, j, c: (row(s, c), j, 0, 0)),
        npair=npair, ncs=ncs,
    )


def _ssd_fwd(xbc_act, dt64, acs64, alast64, acs128, seq, name):
    t = xbc_act.shape[0]
    sp = _ssd_specs(seq, False)

    def body(x_ref, dt_ref, acs_ref, alast_ref, col_ref, bm_ref, cm_ref, y_ref, prev_ref, state):
        @pl.when(pl.program_id(2) == 0)
        def _():
            state[...] = jnp.zeros_like(state)

        prev = state[...]
        prev_ref[...] = prev
        y, new = _ssd_chunk(x_ref[...], dt_ref[...], acs_ref[...], alast_ref[...],
                            col_ref[:, pl.ds(0, LANES)], col_ref[:, pl.ds(LANES, LANES)],
                            bm_ref[...], cm_ref[...], prev)
        y_ref[...] = y
        state[...] = new

    return pl.pallas_call(
        body, name=name, grid=(t // seq, sp["npair"], sp["ncs"]),
        in_specs=[sp["x"], sp["x"], sp["x"], sp["x"], sp["col"], sp["bm"], sp["cm"]],
        out_specs=[sp["x"], sp["st"]],
        out_shape=[jax.ShapeDtypeStruct((t, C_WIDTH), F32),
                   jax.ShapeDtypeStruct((t // CHUNK, sp["npair"], LANES, SSM_STATE), F32)],
        scratch_shapes=[pltpu.VMEM((LANES, SSM_STATE), F32)],
        compiler_params=_cparams("parallel", "parallel", "arbitrary"),
    )(xbc_act, dt64, acs64, alast64, acs128, xbc_act, xbc_act)


def _ssd_bwd(xbc_act, dt64, acs64, alast64, acs128, prev_saved, dy, seq, name):
    t = xbc_act.shape[0]
    sp = _ssd_specs(seq, True)

    def body(x_ref, dt_ref, acs_ref, alast_ref, col_ref, bm_ref, cm_ref, prev_ref, dy_ref,
             dx_ref, ddt_ref, dacs_ref, dalast_ref, dcol_ref, dbc_ref, dstate):
        @pl.when(pl.program_id(2) == 0)
        def _():
            dstate[...] = jnp.zeros_like(dstate)

        _, vjp = jax.vjp(_ssd_chunk, x_ref[...], dt_ref[...], acs_ref[...], alast_ref[...],
                         col_ref[:, pl.ds(0, LANES)], col_ref[:, pl.ds(LANES, LANES)],
                         bm_ref[...], cm_ref[...], prev_ref[...])
        dx, ddt, dacs, dalast, dc0, dc1, dbm, dcm, dprev = vjp((dy_ref[...], dstate[...]))
        dx_ref[...] = dx
        ddt_ref[...] = ddt
        dacs_ref[...] = dacs
        dalast_ref[...] = dalast
        dcol_ref[:, pl.ds(0, LANES)] = dc0
        dcol_ref[:, pl.ds(LANES, LANES)] = dc1
        dbc_ref[:, pl.ds(0, SSM_STATE)] = dbm
        dbc_ref[:, pl.ds(SSM_STATE, SSM_STATE)] = dcm
        dstate[...] = dprev

    wide = jax.ShapeDtypeStruct((t, C_WIDTH), F32)
    return pl.pallas_call(
        body, name=name, grid=(t // seq, sp["npair"], sp["ncs"]),
        in_specs=[sp["x"], sp["x"], sp["x"], sp["x"], sp["col"], sp["bm"], sp["cm"], sp["st"], sp["x"]],
        out_specs=[sp["x"], sp["x"], sp["x"], sp["x"], sp["col"], sp["col"]],
        out_shape=[wide, wide, wide, wide, jax.ShapeDtypeStruct((t, 2 * C_WIDTH), F32),
                   jax.ShapeDtypeStruct((t, 2 * C_WIDTH), F32)],
        scratch_shapes=[pltpu.VMEM((LANES, SSM_STATE), F32)],
        compiler_params=_cparams("parallel", "parallel", "arbitrary"),
    )(xbc_act, dt64, acs64, alast64, acs128, xbc_act, xbc_act, prev_saved, dy)


def _ssd_post_fn(y, xs, z, dskip, g):
    v = (y + dskip * xs) * _silu(z)
    return v * lax.rsqrt(jnp.mean(v * v, axis=-1, keepdims=True) + EPS) * g


def _ssd_post_specs(tm, order):
    gw = C_WIDTH // 2

    def im(f):
        return lambda *ids: f(*order(*ids))
    return dict(
        blk=pl.BlockSpec((tm, gw), im(lambda g, r: (r, g))),
        z=pl.BlockSpec((tm, gw), im(lambda g, r: (r, COL_Z * LANES // gw + g))),
        vec=pl.BlockSpec((1, gw), im(lambda g, r: (0, g))),
    )


def _ssd_post_fwd(y_ssd, xbc_act, proj, dskip64, norm_g, name):
    t = y_ssd.shape[0]
    tm = _row_tile(t)
    sp = _ssd_post_specs(tm, lambda r, g: (g, r))

    def body(y_ref, xs_ref, z_ref, ds_ref, g_ref, o_ref):
        o_ref[...] = _ssd_post_fn(y_ref[...], xs_ref[...], z_ref[...], ds_ref[...], g_ref[...])

    return pl.pallas_call(
        body, name=name, grid=(t // tm, 2),
        in_specs=[sp["blk"], sp["blk"], sp["z"], sp["vec"], sp["vec"]], out_specs=sp["blk"],
        out_shape=jax.ShapeDtypeStruct((t, C_WIDTH), F32),
        compiler_params=_cparams("parallel", "parallel"),
    )(y_ssd, xbc_act, proj, dskip64, norm_g)


def _ssd_post_bwd(y_ssd, xbc_act, proj, dskip64, norm_g, dyc, name):
    t = y_ssd.shape[0]
    tm = _row_tile(t)
    sp = _ssd_post_specs(tm, lambda g, r: (g, r))

    def body(y_ref, xs_ref, z_ref, ds_ref, g_ref, dyc_ref, dy_ref, dxs_ref, dz_ref, dds_ref, dg_ref):
        @pl.when(pl.program_id(1) == 0)
        def _():
            dds_ref[...] = jnp.zeros_like(dds_ref)
            dg_ref[...] = jnp.zeros_like(dg_ref)

        _, vjp = jax.vjp(_ssd_post_fn, y_ref[...], xs_ref[...], z_ref[...], ds_ref[...], g_ref[...])
        dy, dxs, dz, dds, dg = vjp(dyc_ref[...])
        dy_ref[...] = dy
        dxs_ref[...] = dxs
        dz_ref[...] = dz.astype(BF16)
        dds_ref[...] += dds
        dg_ref[...] += dg

    wide = jax.ShapeDtypeStruct((t, C_WIDTH), F32)
    vec = jax.ShapeDtypeStruct((1, C_WIDTH), F32)
    return pl.pallas_call(
        body, name=name, grid=(2, t // tm),
        in_specs=[sp["blk"], sp["blk"], sp["z"], sp["vec"], sp["vec"], sp["blk"]],
        out_specs=[sp["blk"], sp["blk"], sp["blk"], sp["vec"], sp["vec"]],
        out_shape=[wide, wide, jax.ShapeDtypeStruct((t, C_WIDTH), BF16), vec, vec],
        compiler_params=_cparams("parallel", "arbitrary"),
    )(y_ssd, xbc_act, proj, dskip64, norm_g, dyc)


def _ssd_assemble(dxs_ssd, dxs_skip, dbc, name):
    t = dxs_ssd.shape[0]
    tm = _row_tile(t)
    npair = C_WIDTH // LANES

    def body(a_ref, b_ref, dbc_ref, o_ref):
        o_ref[:, pl.ds(0, C_WIDTH)] = a_ref[...] + b_ref[...]
        for grp in range(2):
            for which in range(2):
                acc = jnp.zeros((tm, SSM_STATE), F32)
                for j in range(grp * npair // 2, (grp + 1) * npair // 2):
                    acc = acc + dbc_ref[:, pl.ds((2 * j + which) * SSM_STATE, SSM_STATE)]
                o_ref[:, pl.ds(C_WIDTH + (2 * which + grp) * SSM_STATE, SSM_STATE)] = acc

    return pl.pallas_call(
        body, name=name, grid=(t // tm,),
        in_specs=[pl.BlockSpec((tm, C_WIDTH), lambda i: (i, 0))] * 2 + [pl.BlockSpec((tm, 2 * C_WIDTH), lambda i: (i, 0))],
        out_specs=pl.BlockSpec((tm, D_CONV_C), lambda i: (i, 0)),
        out_shape=jax.ShapeDtypeStruct((t, D_CONV_C), F32),
        compiler_params=_cparams("parallel"),
    )(dxs_ssd, dxs_skip, dbc)


def _pad_taps(w):
    return jnp.pad(w, ((0, CONV_PAD - w.shape[0]), (0, 0)))


def _pad_heads(v):
    return jnp.pad(v, (0, LANES - v.shape[0])).reshape(1, LANES)


def _group_heads(a):
    pad = [(0, 0)] * (a.ndim - 1) + [(0, LANES - SSD_GROUP_HEADS)]
    return jnp.concatenate([jnp.pad(a[..., :SSD_GROUP_HEADS], pad), jnp.pad(a[..., SSD_GROUP_HEADS:], pad)], axis=-1)


def _ungroup_heads(a):
    return jnp.concatenate([a[..., :SSD_GROUP_HEADS], a[..., LANES:LANES + SSD_GROUP_HEADS]], axis=-1)


def _layer_fwd(x, p, seq, li, after=()):
    n = f"l{li}_"
    h1 = _rms_fwd(x, p["norm1_g"], n + "rms1", after=after)
    proj = _matmul(h1, p["w_main"], mode="nn", name=n + "inproj")
    dt_raw = _matmul(h1, p["w_dt"], mode="nn", name=n + "inproj_dt")
    row = lambda v: v.reshape(1, -1)
    ya = _conv_fwd("a", proj, _pad_taps(p["conv_a_w"]), row(p["conv_a_b"]), (row(p["ln_a_g"]), row(p["ln_a_b"])),
                   seq, n + "conva")
    yb = _gmlp_fwd(proj, row(p["ln_b_g"]), row(p["ln_b_b"]), p["w_spatial"], p["b_spatial"], n + "gmlp")
    xbc_act = _conv_fwd("c", proj, _pad_taps(p["conv_c_w"]), row(p["conv_c_b"]), (), seq, n + "convc")
    y_ssd, prev = _ssd2_fwd(xbc_act, dt_raw, _group_heads(row(p["dt_bias"])), _group_heads(row(p["a_log"])), seq, n + "ssd")
    dskip64 = jnp.repeat(p["d_skip"], HEAD_DIM).reshape(1, C_WIDTH)
    yc = _ssd_post_fwd(y_ssd, xbc_act, proj, dskip64, row(p["norm_c_g"]), n + "ssdpost")
    ycat = jnp.concatenate([ya, yb, yc], axis=1).astype(BF16)
    x1 = _matmul(ycat, p["w_out"], mode="nn", name=n + "outproj", add=x)
    h2 = _rms_fwd(x1, p["norm2_g"], n + "rms2")
    u, act = _matmul(h2, p["w_ff1"], mode="nn", name=n + "ff1", epilogue=_relu2_epilogue, out_dtypes=(F32, BF16))
    x2 = _matmul(act, p["w_ff2"], mode="nn", name=n + "ff2", add=x1)
    saved = dict(x=x, h1=h1, proj=proj, dt_raw=dt_raw, xbc_act=xbc_act, prev=prev, y_ssd=y_ssd, dskip64=dskip64,
                 ycat=ycat, x1=x1, h2=h2, u=u, act=act)
    return x2, saved


def _layer_bwd(dx2, p, s, seq, li, after=()):
    n = f"l{li}_b_"
    row = lambda v: v.reshape(1, -1)
    g = {}
    du = _matmul(dx2, p["w_ff2"], mode="nt", name=n + "ff2_dx", epilogue=_relu2_bwd_epilogue, extra=s["u"],
                 out_dtypes=(BF16,), after=after)
    g["w_ff2"] = _matmul(s["act"], dx2, mode="tn", name=n + "ff2_dw")
    g["w_ff1"] = _matmul(s["h2"], du, mode="tn", name=n + "ff1_dw")
    dh2 = _matmul(du, p["w_ff1"], mode="nt", name=n + "ff1_dx")
    dx1, g["norm2_g"] = _rms_bwd(s["x1"], p["norm2_g"], dh2, dx2, n + "rms2")
    g["w_out"] = _matmul(s["ycat"], dx1, mode="tn", name=n + "out_dw")
    dycat = _matmul(dx1, p["w_out"], mode="nt", name=n + "out_dx")
    dya, dyb, dyc = dycat[:, :A_WIDTH], dycat[:, A_WIDTH:A_WIDTH + B_WIDTH], dycat[:, A_WIDTH + B_WIDTH:]
    proj = s["proj"]
    (dval, dgate), dwa, dba, (dlag, dlab) = _conv_bwd(
        "a", proj, _pad_taps(p["conv_a_w"]), row(p["conv_a_b"]), (row(p["ln_a_g"]), row(p["ln_a_b"])), dya, seq, n + "conva")
    g["conv_a_w"], g["conv_a_b"], g["ln_a_g"], g["ln_a_b"] = dwa[:CONV_A_K], dba[0], dlag[0], dlab[0]
    dbu, dbv, dlbg, dlbb, g["w_spatial"], g["b_spatial"] = _gmlp_bwd(
        proj, row(p["ln_b_g"]), row(p["ln_b_b"]), p["w_spatial"], p["b_spatial"], dyb, n + "gmlp")
    g["ln_b_g"], g["ln_b_b"] = dlbg[0], dlbb[0]
    dy_ssd, dxs_skip, dz, dds, dncg = _ssd_post_bwd(s["y_ssd"], s["xbc_act"], proj, s["dskip64"], row(p["norm_c_g"]),
                                                    dyc, n + "ssdpost")
    g["norm_c_g"] = dncg[0]
    g["d_skip"] = dds.reshape(C_HEADS, HEAD_DIM).sum(axis=1)
    dxs, ddt_raw, ddtb, dalog, dbm, dcm = _ssd2_bwd(
        s["xbc_act"], s["dt_raw"], _group_heads(row(p["dt_bias"])), _group_heads(row(p["a_log"])), s["prev"], dy_ssd, seq,
        n + "ssd")
    g["dt_bias"], g["a_log"] = _ungroup_heads(ddtb)[0], _ungroup_heads(dalog)[0]
    dconv = _ssd2_assemble(dxs, dxs_skip, dbm, dcm, n + "ssdasm")
    (dxbc,), dwc, dbcv, _ = _conv_bwd("c", proj, _pad_taps(p["conv_c_w"]), row(p["conv_c_b"]), (), dconv, seq, n + "convc")
    g["conv_c_w"], g["conv_c_b"] = dwc[:CONV_C_K], dbcv[0]
    dproj = jnp.concatenate([dval, dgate, dbu, dbv, dz, dxbc], axis=1)
    g["w_main"] = _matmul(s["h1"], dproj, mode="tn", name=n + "in_dw")
    g["w_dt"] = _matmul(s["h1"], ddt_raw, mode="tn", name=n + "indt_dw")
    dh1 = _matmul(dproj, p["w_main"], mode="nt", name=n + "in_dx")
    dh1 = _matmul(ddt_raw, p["w_dt"], mode="nt", name=n + "indt_dx", add=dh1)
    dx, g["norm1_g"] = _rms_bwd(s["x"], p["norm1_g"], dh1, dx1, n + "rms1")
    return dx, g


EW_BLOCK_BYTES = 1 << 20


def _ew(fn, ins, out_dtypes, name, leads=None):
    leads = leads or [None] * len(ins)
    rows, c = ins[0].shape[-2:]
    tr = _pick(rows, [t for t in (2048, 1024, 512, 256, 128, 64, 32, 16, 8) if t * c * 4 <= EW_BLOCK_BYTES])
    n_in = len(ins)

    def spec(lead):
        if lead is None:
            return pl.BlockSpec((tr, c), lambda i: (i, 0))
        return pl.BlockSpec((None, tr, c), functools.partial(lambda i, k: (k, i, 0), k=lead))

    def body(*refs):
        outs = fn(*[r[...].astype(F32) for r in refs[:n_in]])
        for o_ref, o in zip(refs[n_in:], outs):
            o_ref[...] = o.astype(o_ref.dtype)

    return pl.pallas_call(
        body, name=name, grid=(rows // tr,),
        in_specs=[spec(l) for l in leads], out_specs=[spec(None)] * len(out_dtypes),
        out_shape=[jax.ShapeDtypeStruct((rows, c), dt) for dt in out_dtypes],
        compiler_params=_cparams("parallel"),
    )(*ins)


def _adam_fn(w, g, m, v):
    m2 = ADAM_B1 * m + (1.0 - ADAM_B1) * g
    v2 = ADAM_B2 * v + (1.0 - ADAM_B2) * (g * g)
    m_hat = m2 / (1.0 - ADAM_B1 ** ADAM_STEP)
    v_hat = v2 / (1.0 - ADAM_B2 ** ADAM_STEP)
    delta = -ADAM_LR * (m_hat / (jnp.sqrt(v_hat) + ADAM_EPS) + ADAM_WD * w)
    return delta, m2, v2


def _adam(w, g, m, v, name):
    shape = w.shape
    two_d = lambda a: a.reshape(-1, shape[-1])
    outs = _ew(_adam_fn, [two_d(w), two_d(g), two_d(m), two_d(v)], (F32, F32, F32), name)
    return [o.reshape(shape) for o in outs]


_ANY = pl.BlockSpec(memory_space=pl.ANY)


def _mesh_pos():
    return lax.axis_index("x"), lax.axis_index("y"), lax.axis_index("c")


def _peer_chips(x, y):
    return [(1 - x, y), (x, 1 - y), (1 - x, 1 - y)]


def _remote(src, dst, send_sems, recv_sems, sem, to):
    return pltpu.make_async_remote_copy(src_ref=src, dst_ref=dst, send_sem=send_sems.at[sem],
                                        recv_sem=recv_sems.at[sem], device_id=to, device_id_type=MESH)


def _half_rows(n_rows, which):
    half = n_rows // 2
    return pl.ds(pl.multiple_of(which * half, 8), half)


def _comm_call(body, ins, out_shapes, n_sems, name):
    scratch = [pltpu.SemaphoreType.DMA((n_sems,)), pltpu.SemaphoreType.DMA((n_sems,))]
    return pl.pallas_call(
        body, name=name, in_specs=[_ANY] * len(ins), out_specs=[_ANY] * len(out_shapes),
        out_shape=out_shapes, scratch_shapes=scratch,
    )(*ins)


def _gather_weights(big, small, name):
    nb, ns = len(big), len(small)
    n = nb + ns

    def body(*refs):
        ins, outs = refs[:n], refs[n:2 * n]
        send_sems, recv_sems = refs[2 * n:]
        x, y, c = _mesh_pos()
        q = 2 * x + y
        me, sib = (x, y, c), (x, y, 1 - c)
        chips = _peer_chips(x, y)
        rem = functools.partial(_remote, send_sems=send_sems, recv_sems=recv_sems)
        first = []
        for i in range(nb):
            mine = _half_rows(big[i].shape[0], c)
            for k, (px, py) in enumerate(chips):
                first.append(rem(ins[i].at[mine], outs[i].at[q, mine], sem=6 * i + k, to=(px, py, c)))
        for j in range(ns):
            for k, (px, py) in enumerate(chips):
                first.append(rem(ins[nb + j], outs[nb + j].at[q], sem=6 * nb + 3 * j + k, to=(px, py, c)))
        for cp in first:
            cp.start()
        passed = []
        for i in range(nb):
            mine = _half_rows(big[i].shape[0], c)
            for k, (px, py) in enumerate(chips):
                landed = outs[i].at[2 * px + py, mine]
                rem(landed, landed, sem=6 * i + k, to=me).wait_recv()
                fwd = rem(landed, landed, sem=6 * i + 3 + k, to=sib)
                fwd.start()
                passed.append(fwd)
        for i in range(nb):
            other = _half_rows(big[i].shape[0], 1 - c)
            for k, (px, py) in enumerate(chips):
                theirs = outs[i].at[2 * px + py, other]
                rem(theirs, theirs, sem=6 * i + 3 + k, to=me).wait_recv()
        for j in range(ns):
            for k, (px, py) in enumerate(chips):
                dst = outs[nb + j].at[2 * px + py]
                rem(dst, dst, sem=6 * nb + 3 * j + k, to=me).wait_recv()
        for cp in first + passed:
            cp.wait_send()

    out_shapes = [jax.ShapeDtypeStruct((N_CHIPS,) + a.shape, a.dtype) for a in list(big) + list(small)]
    return _comm_call(body, list(big) + list(small), out_shapes, 6 * nb + 3 * ns, name)


def _sibling_other_halves(gs, name):
    n = len(gs)

    def body(*refs):
        ins, outs = refs[:n], refs[n:2 * n]
        send_sems, recv_sems = refs[2 * n:]
        x, y, c = _mesh_pos()
        copies = [_remote(ins[i].at[:, _half_rows(gs[i].shape[1], 1 - c)], outs[i], send_sems, recv_sems, i, (x, y, 1 - c))
                  for i in range(n)]
        for cp in copies:
            cp.start()
        for cp in copies:
            cp.wait()

    out_shapes = [jax.ShapeDtypeStruct((N_CHIPS, g.shape[1] // 2, g.shape[2]), g.dtype) for g in gs]
    return _comm_call(body, list(gs), out_shapes, n, name)


def _chip_scatter(cs, name):
    n = len(cs)

    def body(*refs):
        ins, outs = refs[:n], refs[n:2 * n]
        send_sems, recv_sems = refs[2 * n:]
        x, y, c = _mesh_pos()
        copies = []
        for i in range(n):
            for k, (px, py) in enumerate(_peer_chips(x, y)):
                copies.append(_remote(ins[i].at[2 * px + py], outs[i].at[k], send_sems, recv_sems, 3 * i + k, (px, py, c)))
        for cp in copies:
            cp.start()
        for cp in copies:
            cp.wait()

    out_shapes = [jax.ShapeDtypeStruct((3,) + a.shape[1:], a.dtype) for a in cs]
    return _comm_call(body, list(cs), out_shapes, 3 * n, name)


_HBM = pl.BlockSpec(memory_space=pltpu.HBM)
_SEM = pl.BlockSpec(memory_space=pltpu.SEMAPHORE)


def _in_hbm(a):
    return pltpu.with_memory_space_constraint(a, pltpu.HBM)


def _split_plan(kind, srcs, lands, x, y, c):
    plan = []
    for src, land in zip(srcs, lands):
        for k, (px, py) in enumerate(_peer_chips(x, y)):
            if kind == "scatter":
                plan.append((src.at[2 * px + py], land.at[k], (px, py, c)))
            else:
                plan.append((src, land.at[2 * x + y], (px, py, c)))
    return plan


def _split_start(kind, srcs, land_shapes, name):
    n = len(srcs)

    def body(*refs):
        ins, lands = refs[:n], refs[n:2 * n]
        send_sems, recv_sems = refs[2 * n], refs[2 * n + 1]
        token = refs[-1]
        x, y, c = _mesh_pos()
        for i, (src, dst, to) in enumerate(_split_plan(kind, ins, lands, x, y, c)):
            pltpu.make_async_remote_copy(src_ref=src, dst_ref=dst, send_sem=send_sems.at[i], recv_sem=recv_sems.at[i],
                                         device_id=to, device_id_type=MESH).start()
        token[...] = jnp.zeros_like(token)

    zones = [lax.empty(s.shape, s.dtype) for s in land_shapes]
    n_sems = 3 * n
    res = pl.pallas_call(
        body, name=name,
        out_shape=(pltpu.SemaphoreType.DMA((n_sems,)), pltpu.SemaphoreType.DMA((n_sems,)),
                   *[pltpu.HBM(a.shape, a.dtype) for a in srcs], *[pltpu.HBM(s.shape, s.dtype) for s in land_shapes],
                   jax.ShapeDtypeStruct((8, LANES), F32)),
        in_specs=[_HBM] * (2 * n), out_specs=(_SEM, _SEM, *[_HBM] * (2 * n), pl.BlockSpec(memory_space=pltpu.VMEM)),
        input_output_aliases={i: 2 + i for i in range(2 * n)},
        compiler_params=pltpu.CompilerParams(has_side_effects=pltpu.SideEffectType.DATAFLOW_SIDE_EFFECTING),
    )(*[_in_hbm(a) for a in srcs], *[_in_hbm(z) for z in zones])
    return dict(send=res[0], recv=res[1], srcs=list(res[2:2 + n]), lands=list(res[2 + n:2 + 2 * n]), token=res[-1], kind=kind)


def _split_wait(started, after, name):
    n = len(started["srcs"])
    kind = started["kind"]

    def body(*refs):
        ins, lands = refs[:n], refs[n:2 * n]
        send_sems, recv_sems = refs[2 * n], refs[2 * n + 1]
        x, y, c = _mesh_pos()
        for i, (src, dst, _) in enumerate(_split_plan(kind, ins, lands, x, y, c)):
            cp = pltpu.make_async_remote_copy(src_ref=src, dst_ref=dst, send_sem=send_sems.at[i], recv_sem=recv_sems.at[i],
                                              device_id=(x, y, c), device_id_type=MESH)
            cp.wait_send()
            cp.wait_recv()

    arrs = started["srcs"] + started["lands"]
    res = pl.pallas_call(
        body, name=name, out_shape=tuple(pltpu.HBM(a.shape, a.dtype) for a in arrs),
        in_specs=[_HBM] * (2 * n) + [_SEM, _SEM, pl.BlockSpec(memory_space=pl.ANY)], out_specs=tuple([_HBM] * (2 * n)),
        input_output_aliases={i: i for i in range(2 * n)},
        compiler_params=pltpu.CompilerParams(has_side_effects=pltpu.SideEffectType.DATAFLOW_SIDE_EFFECTING),
    )(*arrs, started["send"], started["recv"], after)
    return list(res[n:])


def _sibling_share(fs, name):
    n = len(fs)

    def body(*refs):
        ins, outs = refs[:n], refs[n:2 * n]
        send_sems, recv_sems = refs[2 * n:]
        x, y, c = _mesh_pos()
        copies = [_remote(ins[i], outs[i], send_sems, recv_sems, i, (x, y, 1 - c)) for i in range(n)]
        for cp in copies:
            cp.start()
        for cp in copies:
            cp.wait()

    out_shapes = [jax.ShapeDtypeStruct(a.shape, a.dtype) for a in fs]
    return _comm_call(body, list(fs), out_shapes, n, name)


def _allgather8(v, name):
    m = v.shape[0]

    def body(v_ref, out_ref, send_sems, recv_sems):
        x, y, c = _mesh_pos()
        me, sib = (x, y, c), (x, y, 1 - c)
        chips = _peer_chips(x, y)
        rem = functools.partial(_remote, send_sems=send_sems, recv_sems=recv_sems)

        def blk(px, py, pc):
            return out_ref.at[4 * px + 2 * py + pc]

        first = [rem(v_ref, blk(*me), sem=0, to=sib)]
        first += [rem(v_ref, blk(*me), sem=1 + k, to=(px, py, c)) for k, (px, py) in enumerate(chips)]
        for cp in first:
            cp.start()
        passed = []
        for k, (px, py) in enumerate(chips):
            landed = blk(px, py, c)
            rem(landed, landed, sem=1 + k, to=me).wait_recv()
            fwd = rem(landed, landed, sem=4 + k, to=sib)
            fwd.start()
            passed.append(fwd)
        rem(blk(*sib), blk(*sib), sem=0, to=me).wait_recv()
        for k, (px, py) in enumerate(chips):
            theirs = blk(px, py, 1 - c)
            rem(theirs, theirs, sem=4 + k, to=me).wait_recv()
        for cp in first + passed:
            cp.wait_send()

    return _comm_call(body, [v], [jax.ShapeDtypeStruct((8, m, LANES), v.dtype)], 7, name)[0]


_WEIGHTS = ["norm1_g", "w_in", "conv_a_w", "conv_a_b", "ln_a_g", "ln_a_b", "ln_b_g", "ln_b_b", "w_spatial", "b_spatial",
            "conv_c_w", "conv_c_b", "dt_bias", "a_log", "d_skip", "norm_c_g", "w_out", "norm2_g", "w_ff1", "w_ff2", "final_g"]
_BIG = ["w_in", "w_out", "w_ff1", "w_ff2"]
_CONV_SHARDED = ["conv_a_w", "conv_c_w"]
_SMALL = [w for w in _WEIGHTS if w not in _BIG and w != "final_g"]
_PACK_ROWS = 512


def _pack(arrs):
    flat = jnp.concatenate([a.reshape(-1) for a in arrs])
    blk = _PACK_ROWS * LANES
    n = flat.shape[0]
    return jnp.pad(flat, (0, -(-n // blk) * blk - n)).reshape(-1, LANES)


def _unpack(packed, shapes):
    flat = packed.reshape(-1)
    out, off = [], 0
    for s in shapes:
        n = math.prod(s)
        out.append(flat[off:off + n].reshape(s))
        off += n
    return out


def _cols_to_chips(a):
    k = a.shape[0]
    return a.reshape(k, N_CHIPS, -1).transpose(1, 0, 2)


def _chips_to_cols(a):
    return a.transpose(1, 0, 2).reshape(a.shape[1], -1)


def _own_shards(w, li):
    return [w[k][li].astype(BF16) for k in _BIG] + [w[k][li] for k in _CONV_SHARDED]


def _layer_params(w, li, own, gathered, q):
    g_in, g_out, g_ff1, g_ff2, g_ca, g_cc = [lax.dynamic_update_index_in_dim(g, o, q, axis=0)
                                             for g, o in zip(gathered, own)]
    p = {k: w[k][li] for k in _SMALL if k not in _CONV_SHARDED}
    w_in = _chips_to_cols(g_in)
    p["w_main"] = w_in[:, :D_MAIN]
    p["w_dt"] = _group_heads(w_in[:, D_MAIN:])
    p["w_out"] = g_out.reshape(D_MIX, D_MODEL)
    p["w_ff1"] = _chips_to_cols(g_ff1)
    p["w_ff2"] = g_ff2.reshape(D_FF, D_MODEL)
    p["conv_a_w"] = _chips_to_cols(g_ca)
    p["conv_c_w"] = _chips_to_cols(g_cc)
    return p


def _chip_sums(g, li, c, q):
    n = f"l{li}_rs_"
    g_in = jnp.concatenate([g["w_main"], _ungroup_heads(g["w_dt"])], axis=1)
    full = [_cols_to_chips(g_in), g["w_out"].reshape(N_CHIPS, -1, D_MODEL), _cols_to_chips(g["w_ff1"]),
            g["w_ff2"].reshape(N_CHIPS, -1, D_MODEL)]
    from_sib = _sibling_other_halves(full, n + "sib")
    chip_f32, chip_bf16 = [], []
    for i, (a, b) in enumerate(zip(full, from_sib)):
        r2, cols = b.shape[1:]
        mine = lax.dynamic_slice_in_dim(a, c * r2, r2, axis=1)
        s32, s16 = _ew(lambda u, v: (u + v, u + v), [mine.reshape(-1, cols), b.reshape(-1, cols)], (F32, BF16),
                       n + f"chipsum{i}")
        chip_f32.append(lax.dynamic_index_in_dim(s32.reshape(b.shape), q, axis=0, keepdims=False))
        chip_bf16.append(s16.reshape(b.shape))
    return chip_f32, chip_bf16


def _finish_reduce(chip_f32, from_chips, li, c):
    n = f"l{li}_rs_"
    halves = [_ew(lambda o, r0, r1, r2_: (((o + r0) + r1) + r2_,), [own, rb, rb, rb], (F32,), n + f"final{i}",
                  leads=[None, 0, 1, 2])[0] for i, (own, rb) in enumerate(zip(chip_f32, from_chips))]
    from_sib = _sibling_share(halves, n + "share")
    return [jnp.where(c == 0, jnp.concatenate([h, s], axis=0), jnp.concatenate([s, h], axis=0))
            for h, s in zip(halves, from_sib)]


def kernel(x, norm1_g, w_in, conv_a_w, conv_a_b, ln_a_g, ln_a_b, ln_b_g, ln_b_b, w_spatial, b_spatial, conv_c_w, conv_c_b, dt_bias, a_log, d_skip, norm_c_g, w_out, norm2_g, w_ff1, w_ff2, final_g, loss_target, m_norm1_g, m_w_in, m_conv_a_w, m_conv_a_b, m_ln_a_g, m_ln_a_b, m_ln_b_g, m_ln_b_b, m_w_spatial, m_b_spatial, m_conv_c_w, m_conv_c_b, m_dt_bias, m_a_log, m_d_skip, m_norm_c_g, m_w_out, m_norm2_g, m_w_ff1, m_w_ff2, m_final_g, v_norm1_g, v_w_in, v_conv_a_w, v_conv_a_b, v_ln_a_g, v_ln_a_b, v_ln_b_g, v_ln_b_b, v_w_spatial, v_b_spatial, v_conv_c_w, v_conv_c_b, v_dt_bias, v_a_log, v_d_skip, v_norm_c_g, v_w_out, v_norm2_g, v_w_ff1, v_w_ff2, v_final_g):
    given = dict(locals())
    w = {k: given[k] for k in _WEIGHTS}
    m = {k: given["m_" + k] for k in _WEIGHTS}
    v = {k: given["v_" + k] for k in _WEIGHTS}
    depth = w_in.shape[0]
    nseq, seq, d = x.shape
    xi, yi, ci = _mesh_pos()
    q = 2 * xi + yi

    own = [_own_shards(w, li) for li in range(depth)]
    nb = len(_BIG)
    first = _gather_weights(own[0][:nb], own[0][nb:], "l0_gather")
    started = {}
    for li in range(1, depth):
        srcs, _ = lax.optimization_barrier((own[li], first))
        started[li] = _split_start("gather", srcs, [jax.ShapeDtypeStruct((N_CHIPS,) + a.shape, a.dtype) for a in srcs],
                                   f"l{li}_gather_start")
    h = x.reshape(nseq * seq, d)
    layer_params, saved = [], []
    for li in range(depth):
        if li == 0:
            gathered, after = first, [started[lj]["token"] for lj in range(1, depth)]
        else:
            gathered, after = _split_wait(started[li], h, f"l{li}_gather_wait"), ()
        layer_params.append(_layer_params(w, li, own[li], gathered, q))
        h, s = _layer_fwd(h, layer_params[li], seq, li, after=after)
        saved.append(s)
    loss, dx, d_final = _loss_head(h, final_g, loss_target.reshape(nseq * seq, d))

    grads = [None] * depth
    big_grads = [None] * depth
    pending = None
    for li in reversed(range(depth)):
        dx, grads[li] = _layer_bwd(dx, layer_params[li], saved[li], seq, li,
                                   after=() if pending is None else (pending[1]["token"],))
        if pending is not None:
            lj, scatter, chip_f32 = pending
            big_grads[lj] = _finish_reduce(chip_f32, _split_wait(scatter, dx, f"l{lj}_rs_scatter_wait"), lj, ci)
        chip_f32, chip_bf16 = _chip_sums(grads[li], li, ci, q)
        if li > 0:
            lands = [jax.ShapeDtypeStruct((3,) + a.shape[1:], a.dtype) for a in chip_bf16]
            pending = (li, _split_start("scatter", chip_bf16, lands, f"l{li}_rs_scatter_start"), chip_f32)
        else:
            big_grads[li] = _finish_reduce(chip_f32, _chip_scatter(chip_bf16, "l0_rs_scatter"), li, ci)
    grad_out, delta_out, m_out, v_out = {}, {}, {}, {}
    for i, k in enumerate(_BIG):
        grad_out[k] = jnp.stack([big_grads[li][i] for li in range(depth)])
        delta_out[k], m_out[k], v_out[k] = _adam(w[k], grad_out[k], m[k], v[k], "adam_" + k)

    small_shapes = [grads[0][k].shape for k in _SMALL]
    parts = [grads[li][k] for li in range(depth) for k in _SMALL] + [d_final, loss.reshape(1)]
    packed_parts = _pack(parts)
    gathered = lax.dynamic_update_index_in_dim(_allgather8(packed_parts, "small_allgather"), packed_parts,
                                               2 * q + ci, axis=0)

    def sum8(*blocks):
        acc = blocks[0]
        for b in blocks[1:]:
            acc = acc + b
        return (acc,)

    total = _ew(sum8, [gathered] * 8, (F32,), "small_sum", leads=list(range(8)))[0]
    summed = _unpack(total, small_shapes * depth + [d_final.shape, (1,)])
    loss_total = summed[-1][0]
    small_grads = {k: jnp.stack([summed[li * len(_SMALL) + i] for li in range(depth)]) for i, k in enumerate(_SMALL)}
    small_grads["final_g"] = summed[-2]
    for k in _CONV_SHARDED:
        n_shard = w[k].shape[-1]
        small_grads[k] = lax.dynamic_slice_in_dim(small_grads[k], q * n_shard, n_shard, axis=2)
    names = _SMALL + ["final_g"]
    shapes = [w[k].shape for k in names]
    packed = [_pack([src[k] for k in names]) for src in (w, small_grads, m, v)]
    outs = _ew(_adam_fn, packed, (F32, F32, F32), "adam_small")
    for dst, o in zip((delta_out, m_out, v_out), outs):
        for k, a in zip(names, _unpack(o, shapes)):
            dst[k] = a
    for k in names:
        grad_out[k] = small_grads[k]

    return (loss_total, dx.reshape(nseq, seq, d), *[grad_out[k] for k in _WEIGHTS], *[delta_out[k] for k in _WEIGHTS],
            *[m_out[k] for k in _WEIGHTS], *[v_out[k] for k in _WEIGHTS])
```

```python
import functools
import math

import jax
import jax.numpy as jnp
from jax import lax
from jax.experimental import pallas as pl
from jax.experimental.pallas import tpu as pltpu

F32 = jnp.float32
BF16 = jnp.bfloat16
MESH = pl.DeviceIdType.MESH

D_MODEL = 1024
DEPTH = 4
HEAD_DIM = 64
A_WIDTH = 512
B_WIDTH = 512
C_WIDTH = 1024
C_HEADS = 16
CONV_A_K = 31
CONV_C_K = 4
CHUNK = 128
SSM_STATE = 128
D_CONV_C = 1536
D_MAIN = 4608
D_IN_PROJ = 4624
D_MIX = 2048
D_FF = 4096
EPS = 1e-5
NEG = -1e30
LANES = 128
CONV_PAD = 32
N_CHIPS = 4

ADAM_LR = 0.001
ADAM_B1 = 0.9
ADAM_B2 = 0.999
ADAM_EPS = 1e-08
ADAM_WD = 0.01
ADAM_STEP = 10

VMEM_LIMIT = 56 * 1024 * 1024

COL_AVAL, COL_AGATE, COL_BU, COL_BV, COL_Z, COL_XBC = 0, 4, 8, 12, 16, 24


def _cparams(*sem):
    return pltpu.CompilerParams(dimension_semantics=sem, vmem_limit_bytes=VMEM_LIMIT)


_DN = {"nn": (((1,), (0,)), ((), ())), "nt": (((1,), (1,)), ((), ())), "tn": (((0,), (0,)), ((), ()))}


def _dot_raw(a, b, mode):
    return lax.dot_general(a.astype(BF16), b.astype(BF16), _DN[mode], preferred_element_type=F32)


def _make_dot(mode):
    @jax.custom_vjp
    def f(a, b):
        return _dot_raw(a, b, mode)

    def fwd(a, b):
        return _dot_raw(a, b, mode), (a, b)

    def bwd(res, g):
        a, b = res
        if mode == "nn":
            return _dot_raw(g, b, "nt"), _dot_raw(a, g, "tn")
        if mode == "nt":
            return _dot_raw(g, b, "nn"), _dot_raw(g, a, "tn")
        return _dot_raw(b, g, "nt"), _dot_raw(a, g, "nn")

    f.defvjp(fwd, bwd)
    return f


_nn = _make_dot("nn")
_nt = _make_dot("nt")
_tn = _make_dot("tn")


def _xdot(a, e):
    return jnp.dot(a, e, precision=lax.Precision.HIGHEST, preferred_element_type=F32)


def _iota2(shape, dim):
    return lax.broadcasted_iota(jnp.int32, shape, dim)


def _gmean_impl(x):
    n = x.shape[-1]
    same = (_iota2((n, n), 0) < HEAD_DIM) == (_iota2((n, n), 1) < HEAD_DIM)
    p = jnp.where(same, 1.0 / HEAD_DIM, 0.0).astype(BF16)
    hi = x.astype(BF16)
    lo = (x - hi.astype(F32)).astype(BF16)
    dn = _DN["nn"]
    return (lax.dot_general(hi, p, dn, preferred_element_type=F32)
            + lax.dot_general(lo, p, dn, preferred_element_type=F32))


@jax.custom_vjp
def _gmean(x):
    return _gmean_impl(x)


_gmean.defvjp(lambda x: (_gmean_impl(x), None), lambda _, g: (_gmean_impl(g),))


def _sigmoid(x):
    return 1.0 / (1.0 + jnp.exp(-x))


def _silu(x):
    return x * _sigmoid(x)


def _gelu(x):
    return 0.5 * x * (1.0 + lax.erf(x * 0.7071067811865476))


def _softplus(x):
    return jnp.maximum(x, 0.0) + jnp.log(1.0 + jnp.exp(-jnp.abs(x)))


def _rms(x, g):
    return x * lax.rsqrt(jnp.mean(x * x, axis=-1, keepdims=True) + EPS) * g


def _ln64(x, g, b):
    mu = _gmean(x)
    xc = x - mu
    var = _gmean(xc * xc)
    return xc * lax.rsqrt(var + EPS) * g + b


def _lane_lt64(shape):
    return _iota2(shape, 1) < HEAD_DIM


def _pick(n, pref):
    for t in pref:
        if n % t == 0:
            return t
    return n


_UNREAD = pl.BlockSpec(memory_space=pl.ANY)


def _matmul(a, b, *, mode, name, add=None, epilogue=None, extra=None, out_dtypes=(F32,), after=()):
    if mode == "nn":
        (m, k), (_, n) = a.shape, b.shape
    elif mode == "nt":
        (m, k), (n, _) = a.shape, b.shape
    else:
        (k, m), (_, n) = a.shape, b.shape
    tm = _pick(m, (1024, 512, 256, 128))
    tn = _pick(n, (1536, 1024, 512, 256, 128))
    tk = _pick(k, (1536, 1024, 512, 256, 128))
    nk = k // tk
    a_spec = {"nn": pl.BlockSpec((tm, tk), lambda i, j, kk: (i, kk)),
              "nt": pl.BlockSpec((tm, tk), lambda i, j, kk: (i, kk)),
              "tn": pl.BlockSpec((tk, tm), lambda i, j, kk: (kk, i))}[mode]
    b_spec = {"nn": pl.BlockSpec((tk, tn), lambda i, j, kk: (kk, j)),
              "nt": pl.BlockSpec((tn, tk), lambda i, j, kk: (j, kk)),
              "tn": pl.BlockSpec((tk, tn), lambda i, j, kk: (kk, j))}[mode]
    o_spec = pl.BlockSpec((tm, tn), lambda i, j, kk: (i, j))
    ins = [a, b]
    in_specs = [a_spec, b_spec]
    if add is not None:
        ins.append(add)
        in_specs.append(o_spec)
    if extra is not None:
        ins.append(extra)
        in_specs.append(o_spec)
    ins += list(after)
    in_specs += [_UNREAD] * len(after)
    n_out = len(out_dtypes)

    def body(*refs):
        a_ref, b_ref = refs[0], refs[1]
        pos = 2
        add_ref = ex_ref = None
        if add is not None:
            add_ref = refs[pos]
            pos += 1
        if extra is not None:
            ex_ref = refs[pos]
            pos += 1
        pos += len(after)
        o_refs = refs[pos:pos + n_out]

        def finish(acc):
            if add_ref is not None:
                acc = acc + add_ref[...].astype(F32)
            outs = (acc,) if epilogue is None else epilogue(acc, None if ex_ref is None else ex_ref[...])
            for o_ref, o in zip(o_refs, outs):
                o_ref[...] = o.astype(o_ref.dtype)

        part = _dot_raw(a_ref[...], b_ref[...], mode)
        if nk == 1:
            finish(part)
            return
        acc_ref = refs[pos + n_out]
        kk = pl.program_id(2)

        @pl.when(kk == 0)
        def _():
            acc_ref[...] = part

        @pl.when(jnp.logical_and(kk > 0, kk < nk - 1))
        def _():
            acc_ref[...] += part

        @pl.when(kk == nk - 1)
        def _():
            finish(acc_ref[...] + part)

    res = pl.pallas_call(
        body, name=name, grid=(m // tm, n // tn, nk),
        in_specs=in_specs, out_specs=[o_spec] * n_out,
        out_shape=[jax.ShapeDtypeStruct((m, n), dt) for dt in out_dtypes],
        scratch_shapes=[pltpu.VMEM((tm, tn), F32)] if nk > 1 else [],
        compiler_params=_cparams("parallel", "parallel", "arbitrary"),
    )(*ins)
    return res[0] if n_out == 1 else res


def _relu2_epilogue(acc, _):
    r = jnp.maximum(acc, 0.0)
    return acc, r * r


def _relu2_bwd_epilogue(acc, u):
    return (acc * (2.0 * jnp.maximum(u, 0.0)),)


def _row_tile(t):
    return _pick(t, (512, 256, 128))


def _rms_fwd(x, g, name, after=()):
    t, d = x.shape
    tm = _row_tile(t)

    def body(x_ref, g_ref, *rest):
        o_ref = rest[-1]
        o_ref[...] = _rms(x_ref[...], g_ref[...]).astype(BF16)

    return pl.pallas_call(
        body, name=name, grid=(t // tm,),
        in_specs=[pl.BlockSpec((tm, d), lambda i: (i, 0)), pl.BlockSpec((1, d), lambda i: (0, 0))] + [_UNREAD] * len(after),
        out_specs=pl.BlockSpec((tm, d), lambda i: (i, 0)),
        out_shape=jax.ShapeDtypeStruct((t, d), BF16),
        compiler_params=_cparams("parallel"),
    )(x, g.reshape(1, d), *after)


def _rms_bwd(x, g, dh, dres, name):
    t, d = x.shape
    tm = _row_tile(t)

    def body(x_ref, g_ref, dh_ref, dres_ref, dx_ref, dg_ref):
        @pl.when(pl.program_id(0) == 0)
        def _():
            dg_ref[...] = jnp.zeros_like(dg_ref)

        _, vjp = jax.vjp(_rms, x_ref[...], g_ref[...])
        dx, dg = vjp(dh_ref[...].astype(F32))
        dx_ref[...] = dx + dres_ref[...]
        dg_ref[...] += dg

    row = pl.BlockSpec((tm, d), lambda i: (i, 0))
    vec = pl.BlockSpec((1, d), lambda i: (0, 0))
    dx, dg = pl.pallas_call(
        body, name=name, grid=(t // tm,),
        in_specs=[row, vec, row, row], out_specs=[row, vec],
        out_shape=[jax.ShapeDtypeStruct((t, d), F32), jax.ShapeDtypeStruct((1, d), F32)],
        compiler_params=_cparams("arbitrary"),
    )(x, g.reshape(1, d), dh, dres)
    return dx, dg.reshape(d)


def _loss_head(x, g, target):
    t, d = x.shape
    tm = _row_tile(t)

    def loss_fn(xv, gv, tv):
        err = _rms(xv, gv) - tv
        return 0.5 * jnp.sum(jnp.mean(err * err, axis=-1, keepdims=True))

    def body(x_ref, g_ref, t_ref, loss_ref, dx_ref, dg_ref):
        @pl.when(pl.program_id(0) == 0)
        def _():
            dg_ref[...] = jnp.zeros_like(dg_ref)
            loss_ref[...] = jnp.zeros_like(loss_ref)

        tv = t_ref[...]
        val, vjp = jax.vjp(lambda xv, gv: loss_fn(xv, gv, tv), x_ref[...], g_ref[...])
        dx, dg = vjp(jnp.ones((), F32))
        dx_ref[...] = dx
        dg_ref[...] += dg
        loss_ref[...] += jnp.full(loss_ref.shape, val, F32)

    row = pl.BlockSpec((tm, d), lambda i: (i, 0))
    vec = pl.BlockSpec((1, d), lambda i: (0, 0))
    loss, dx, dg = pl.pallas_call(
        body, name="loss_head", grid=(t // tm,),
        in_specs=[row, vec, row], out_specs=[pl.BlockSpec((1, LANES), lambda i: (0, 0)), row, vec],
        out_shape=[jax.ShapeDtypeStruct((1, LANES), F32), jax.ShapeDtypeStruct((t, d), F32),
                   jax.ShapeDtypeStruct((1, d), F32)],
        compiler_params=_cparams("arbitrary"),
    )(x, g.reshape(1, d), target)
    return loss[0, 0], dx, dg.reshape(d)


def _pre_glu(val, gate):
    return val * _sigmoid(gate)


def _pre_id(x):
    return x


def _post_lnsilu(c, g, b):
    return _silu(_ln64(c, g, b))


def _post_silu(c):
    return _silu(c)


def _conv_cfg(kind):
    if kind == "a":
        return dict(k=CONV_A_K, pre=_pre_glu, post=_post_lnsilu, n_in=2, n_par=2, nblk=A_WIDTH // LANES,
                    cols=(COL_AVAL, COL_AGATE))
    return dict(k=CONV_C_K, pre=_pre_id, post=_post_silu, n_in=1, n_par=0, nblk=D_CONV_C // LANES,
                cols=(COL_XBC,))


def _conv_fwd(kind, proj, w, bias, params, seq, name, out_dtype=F32):
    cfg = _conv_cfg(kind)
    kt, pre, post, n_in = cfg["k"], cfg["pre"], cfg["post"], cfg["n_in"]
    t = proj.shape[0]
    nseq = t // seq
    c = cfg["nblk"] * LANES
    rt = min(256, seq)
    nrt = seq // rt
    off0 = CONV_PAD - (kt - 1)

    def body(*refs):
        in_refs = refs[:n_in]
        w_ref, b_ref = refs[n_in], refs[n_in + 1]
        par_refs = refs[n_in + 2:n_in + 2 + cfg["n_par"]]
        o_ref, hpad = refs[n_in + 2 + cfg["n_par"]:]
        hpad[pl.ds(0, CONV_PAD), :] = jnp.zeros((CONV_PAD, LANES), F32)
        for r in range(nrt):
            hpad[pl.ds(CONV_PAD + r * rt, rt), :] = pre(*[x[pl.ds(r * rt, rt), :] for x in in_refs])
        pars = [p[...] for p in par_refs]
        for r in range(nrt):
            acc = jnp.broadcast_to(b_ref[...], (rt, LANES))
            for k in range(kt):
                acc = acc + w_ref[pl.ds(k, 1), :] * hpad[pl.ds(off0 + k + r * rt, rt), :]
            o_ref[pl.ds(r * rt, rt), :] = post(acc, *pars).astype(out_dtype)

    in_specs = [pl.BlockSpec((seq, LANES), functools.partial(lambda s, j, col: (s, col + j), col=col))
                for col in cfg["cols"]]
    vec = pl.BlockSpec((1, LANES), lambda s, j: (0, j))
    in_specs += [pl.BlockSpec((CONV_PAD, LANES), lambda s, j: (0, j)), vec] + [vec] * cfg["n_par"]
    return pl.pallas_call(
        body, name=name, grid=(nseq, cfg["nblk"]),
        in_specs=in_specs, out_specs=pl.BlockSpec((seq, LANES), lambda s, j: (s, j)),
        out_shape=jax.ShapeDtypeStruct((t, c), out_dtype),
        scratch_shapes=[pltpu.VMEM((seq + CONV_PAD, LANES), F32)],
        compiler_params=_cparams("parallel", "parallel"),
    )(*([proj] * n_in), w, bias, *params)


def _conv_bwd(kind, proj, w, bias, params, dy, seq, name, dy_col=0):
    cfg = _conv_cfg(kind)
    kt, pre, post, n_in, n_par = cfg["k"], cfg["pre"], cfg["post"], cfg["n_in"], cfg["n_par"]
    t = proj.shape[0]
    nseq = t // seq
    c = cfg["nblk"] * LANES
    rt = min(256, seq)
    nrt = seq // rt
    off0 = CONV_PAD - (kt - 1)

    def body(*refs):
        in_refs = refs[:n_in]
        w_ref, b_ref = refs[n_in], refs[n_in + 1]
        par_refs = refs[n_in + 2:n_in + 2 + n_par]
        pos = n_in + 2 + n_par
        dy_ref = refs[pos]
        din_refs = refs[pos + 1:pos + 1 + n_in]
        dw_ref, db_ref = refs[pos + 1 + n_in], refs[pos + 2 + n_in]
        dpar_refs = refs[pos + 3 + n_in:pos + 3 + n_in + n_par]
        hpad, dcpad = refs[pos + 3 + n_in + n_par:]

        @pl.when(pl.program_id(1) == 0)
        def _():
            dw_ref[...] = jnp.zeros_like(dw_ref)
            db_ref[...] = jnp.zeros_like(db_ref)
            for r in dpar_refs:
                r[...] = jnp.zeros_like(r)

        hpad[pl.ds(0, CONV_PAD), :] = jnp.zeros((CONV_PAD, LANES), F32)
        dcpad[pl.ds(seq, CONV_PAD), :] = jnp.zeros((CONV_PAD, LANES), F32)
        for r in range(nrt):
            hpad[pl.ds(CONV_PAD + r * rt, rt), :] = pre(*[x[pl.ds(r * rt, rt), :] for x in in_refs])
        pars = [p[...] for p in par_refs]
        for r in range(nrt):
            acc = jnp.broadcast_to(b_ref[...], (rt, LANES))
            for k in range(kt):
                acc = acc + w_ref[pl.ds(k, 1), :] * hpad[pl.ds(off0 + k + r * rt, rt), :]
            _, vjp = jax.vjp(post, acc, *pars)
            grads = vjp(dy_ref[pl.ds(r * rt, rt), :])
            dcpad[pl.ds(r * rt, rt), :] = grads[0]
            db_ref[...] += jnp.sum(grads[0], axis=0, keepdims=True)
            for ref, gpar in zip(dpar_refs, grads[1:]):
                ref[...] += gpar
        for r in range(nrt):
            dh = jnp.zeros((rt, LANES), F32)
            for k in range(kt):
                dh = dh + w_ref[pl.ds(k, 1), :] * dcpad[pl.ds(r * rt + kt - 1 - k, rt), :]
            _, vjp = jax.vjp(pre, *[x[pl.ds(r * rt, rt), :] for x in in_refs])
            for ref, gin in zip(din_refs, vjp(dh)):
                ref[pl.ds(r * rt, rt), :] = gin.astype(ref.dtype)
        for k in range(kt):
            s = jnp.zeros((1, LANES), F32)
            for r in range(nrt):
                s = s + jnp.sum(dcpad[pl.ds(r * rt, rt), :] * hpad[pl.ds(off0 + k + r * rt, rt), :],
                                axis=0, keepdims=True)
            dw_ref[pl.ds(k, 1), :] += s

    in_specs = [pl.BlockSpec((seq, LANES), functools.partial(lambda j, s, col: (s, col + j), col=col))
                for col in cfg["cols"]]
    vec = pl.BlockSpec((1, LANES), lambda j, s: (0, j))
    wspec = pl.BlockSpec((CONV_PAD, LANES), lambda j, s: (0, j))
    blk = pl.BlockSpec((seq, LANES), lambda j, s: (s, j))
    in_specs += [wspec, vec] + [vec] * n_par + [pl.BlockSpec((seq, LANES), lambda j, s: (s, dy_col + j))]
    out_specs = [blk] * n_in + [wspec, vec] + [vec] * n_par
    out_shape = ([jax.ShapeDtypeStruct((t, c), BF16)] * n_in
                 + [jax.ShapeDtypeStruct((CONV_PAD, c), F32), jax.ShapeDtypeStruct((1, c), F32)]
                 + [jax.ShapeDtypeStruct((1, c), F32)] * n_par)
    res = pl.pallas_call(
        body, name=name, grid=(cfg["nblk"], nseq),
        in_specs=in_specs, out_specs=out_specs, out_shape=out_shape,
        scratch_shapes=[pltpu.VMEM((seq + CONV_PAD, LANES), F32), pltpu.VMEM((seq + CONV_PAD, LANES), F32)],
        compiler_params=_cparams("parallel", "arbitrary"),
    )(*([proj] * n_in), w, bias, *params, dy)
    return res[:n_in], res[n_in], res[n_in + 1], res[n_in + 2:]


def _gmlp_chunk(bu, bv, g, b, w0, w1, b0row, b1row):
    u = _gelu(bu)
    vn = _ln64(_gelu(bv), g, b)
    tri = _iota2((CHUNK, CHUNK), 0) >= _iota2((CHUNK, CHUNK), 1)
    m0 = _nn(jnp.where(tri, w0, 0.0), vn) + jnp.broadcast_to(b0row, (CHUNK, CHUNK)).T
    m1 = _nn(jnp.where(tri, w1, 0.0), vn) + jnp.broadcast_to(b1row, (CHUNK, CHUNK)).T
    return u * jnp.where(_lane_lt64((CHUNK, LANES)), m0, m1)


def _gmlp_specs(tm, order):
    def im(f):
        return lambda *ids: f(*order(*ids))
    return dict(
        bu=pl.BlockSpec((tm, LANES), im(lambda j, r: (r, COL_BU + j))),
        bv=pl.BlockSpec((tm, LANES), im(lambda j, r: (r, COL_BV + j))),
        vec=pl.BlockSpec((1, LANES), im(lambda j, r: (0, j))),
        ws=pl.BlockSpec((2, CHUNK, CHUNK), im(lambda j, r: (j, 0, 0))),
        bs=pl.BlockSpec((None, 2, CHUNK), im(lambda j, r: (j, 0, 0))),
        blk=pl.BlockSpec((tm, LANES), im(lambda j, r: (r, j))),
    )


def _gmlp_fwd(proj, ln_g, ln_b, w_s, b_s, name):
    t = proj.shape[0]
    tm = _row_tile(t)
    nch = tm // CHUNK
    sp = _gmlp_specs(tm, lambda r, j: (j, r))

    def body(bu_ref, bv_ref, g_ref, b_ref, ws_ref, bs_ref, o_ref):
        for ci in range(nch):
            rows = pl.ds(ci * CHUNK, CHUNK)
            o_ref[rows, :] = _gmlp_chunk(bu_ref[rows, :], bv_ref[rows, :], g_ref[...], b_ref[...], ws_ref[0], ws_ref[1],
                                         bs_ref[pl.ds(0, 1), :], bs_ref[pl.ds(1, 1), :]).astype(BF16)

    return pl.pallas_call(
        body, name=name, grid=(t // tm, B_WIDTH // LANES),
        in_specs=[sp["bu"], sp["bv"], sp["vec"], sp["vec"], sp["ws"], sp["bs"]],
        out_specs=sp["blk"], out_shape=jax.ShapeDtypeStruct((t, B_WIDTH), BF16),
        compiler_params=_cparams("parallel", "parallel"),
    )(proj, proj, ln_g, ln_b, w_s, b_s.reshape(B_WIDTH // LANES, 2, CHUNK))


def _gmlp_bwd(proj, ln_g, ln_b, w_s, b_s, dy, name, dy_col=0):
    t = proj.shape[0]
    tm = _row_tile(t)
    nch = tm // CHUNK
    sp = _gmlp_specs(tm, lambda j, r: (j, r))
    dy_spec = pl.BlockSpec((tm, LANES), lambda j, r: (r, dy_col + j))

    def body(bu_ref, bv_ref, g_ref, b_ref, ws_ref, bs_ref, dy_ref, dbu_ref, dbv_ref, dg_ref, db_ref, dws_ref, dbs_ref):
        @pl.when(pl.program_id(1) == 0)
        def _():
            for r in (dg_ref, db_ref, dws_ref, dbs_ref):
                r[...] = jnp.zeros_like(r)

        for ci in range(nch):
            rows = pl.ds(ci * CHUNK, CHUNK)
            _, vjp = jax.vjp(_gmlp_chunk, bu_ref[rows, :], bv_ref[rows, :], g_ref[...], b_ref[...],
                             ws_ref[0], ws_ref[1], bs_ref[pl.ds(0, 1), :], bs_ref[pl.ds(1, 1), :])
            dbu, dbv, dg, db, dw0, dw1, db0, db1 = vjp(dy_ref[rows, :])
            dbu_ref[rows, :] = dbu.astype(BF16)
            dbv_ref[rows, :] = dbv.astype(BF16)
            dg_ref[...] += dg
            db_ref[...] += db
            dws_ref[0] += dw0
            dws_ref[1] += dw1
            dbs_ref[pl.ds(0, 1), :] += db0
            dbs_ref[pl.ds(1, 1), :] += db1

    nh = B_WIDTH // LANES
    res = pl.pallas_call(
        body, name=name, grid=(nh, t // tm),
        in_specs=[sp["bu"], sp["bv"], sp["vec"], sp["vec"], sp["ws"], sp["bs"], dy_spec],
        out_specs=[sp["blk"], sp["blk"], sp["vec"], sp["vec"], sp["ws"], sp["bs"]],
        out_shape=[jax.ShapeDtypeStruct((t, B_WIDTH), BF16), jax.ShapeDtypeStruct((t, B_WIDTH), BF16),
                   jax.ShapeDtypeStruct((1, B_WIDTH), F32), jax.ShapeDtypeStruct((1, B_WIDTH), F32),
                   jax.ShapeDtypeStruct(w_s.shape, F32), jax.ShapeDtypeStruct((nh, 2, CHUNK), F32)],
        compiler_params=_cparams("parallel", "arbitrary"),
    )(proj, proj, ln_g, ln_b, w_s, b_s.reshape(nh, 2, CHUNK), dy)
    dbu, dbv, dg, db, dws, dbs = res
    return dbu, dbv, dg, db, dws, dbs.reshape(b_s.shape)


def _tri_apply(a, lower):
    l = a.shape[0]
    r, c = _iota2((l, l), 0), _iota2((l, l), 1)
    t = jnp.where((r >= c) if lower else (r <= c), 1.0, 0.0).astype(BF16)
    hi = a.astype(BF16)
    r1 = a - hi.astype(F32)
    mid = r1.astype(BF16)
    lo = (r1 - mid.astype(F32)).astype(BF16)
    dn = _DN["nn"]
    return (lax.dot_general(t, hi, dn, preferred_element_type=F32) + lax.dot_general(t, mid, dn, preferred_element_type=F32)
            + lax.dot_general(t, lo, dn, preferred_element_type=F32))


@jax.custom_vjp
def _cumsum_rows(a):
    return _tri_apply(a, True)


_cumsum_rows.defvjp(lambda a: (_tri_apply(a, True), None), lambda _, g: (_tri_apply(g, False),))

SSD_GROUP_HEADS = 8
SSD_GROUP_PAIRS = 4


def _ssd_group(x0, x1, x2, x3, dt_raw, bias, alog, bm, cm, p0, p1, p2, p3):
    xs, prevs = (x0, x1, x2, x3), (p0, p1, p2, p3)
    dt = _softplus(dt_raw + bias)
    a = dt * (-jnp.exp(alog))
    acs = _cumsum_rows(a)
    alast = jnp.sum(a, axis=0, keepdims=True)
    dt_t, acs_t = dt.T, acs.T
    cb = _nt(cm, bm)
    tri = _iota2((CHUNK, CHUNK), 0) >= _iota2((CHUNK, CHUNK), 1)
    lane = _iota2((CHUNK, LANES), 1)
    sub = _iota2((LANES, CHUNK), 0)
    lane1 = _iota2((1, LANES), 1)

    def column(v, i):
        return jnp.broadcast_to(jnp.sum(jnp.where(lane == i, v, 0.0), axis=1, keepdims=True), (CHUNK, LANES))

    def row(vt, i):
        return jnp.broadcast_to(jnp.sum(jnp.where(sub == i, vt, 0.0), axis=0, keepdims=True), (CHUNK, CHUNK))

    heads = []
    for i in range(SSD_GROUP_HEADS):
        col_a = column(acs, i)
        al = jnp.sum(jnp.where(lane1 == i, alast, 0.0), axis=1, keepdims=True)
        m = cb * jnp.exp(jnp.where(tri, col_a - row(acs_t, i), NEG)) * row(dt_t, i)
        heads.append((m, jnp.exp(col_a), column(dt, i) * jnp.exp(al - col_a), jnp.exp(al)))
    lo_lanes = _lane_lt64((CHUNK, LANES))
    lo_rows = _iota2((LANES, SSM_STATE), 0) < HEAD_DIM
    ys, news = [], []
    for j in range(SSD_GROUP_PAIRS):
        (m0, ea0, w0, cd0), (m1, ea1, w1, cd1) = heads[2 * j], heads[2 * j + 1]
        x, prev = xs[j], prevs[j]
        ydiag = jnp.where(lo_lanes, _nn(m0, x), _nn(m1, x))
        yoff = jnp.where(lo_lanes, _nt(cm * ea0, prev), _nt(cm * ea1, prev))
        states = jnp.where(lo_rows, _tn(x, bm * w0), _tn(x, bm * w1))
        ys.append(ydiag + yoff)
        news.append(prev * jnp.where(lo_rows, cd0, cd1) + states)
    return tuple(ys) + tuple(news)


def _ssd2_specs(seq, rev):
    ncs = seq // CHUNK
    gw = SSD_GROUP_PAIRS * LANES
    nblk_x = C_WIDTH // LANES

    def row(s, c):
        return s * ncs + (ncs - 1 - c if rev else c)

    return dict(
        x=pl.BlockSpec((CHUNK, gw), lambda g, s, c: (row(s, c), g)),
        dt=pl.BlockSpec((CHUNK, LANES), lambda g, s, c: (row(s, c), g)),
        vec=pl.BlockSpec((1, LANES), lambda g, s, c: (0, g)),
        bm=pl.BlockSpec((CHUNK, SSM_STATE), lambda g, s, c: (row(s, c), nblk_x + g)),
        cm=pl.BlockSpec((CHUNK, SSM_STATE), lambda g, s, c: (row(s, c), nblk_x + 2 + g)),
        st=pl.BlockSpec((None, SSD_GROUP_PAIRS, LANES, SSM_STATE), lambda g, s, c: (row(s, c), g, 0, 0)),
        ncs=ncs,
    )


def _lane_blocks(ref):
    return [ref[:, pl.ds(j * LANES, LANES)] for j in range(SSD_GROUP_PAIRS)]


def _ssd2_fwd(xbc_act, dt_raw, dt_bias, a_log, seq, name):
    t = xbc_act.shape[0]
    sp = _ssd2_specs(seq, False)

    def body(x_ref, dt_ref, bias_ref, alog_ref, bm_ref, cm_ref, y_ref, prev_ref, state):
        @pl.when(pl.program_id(2) == 0)
        def _():
            state[...] = jnp.zeros_like(state)

        prevs = [state[j] for j in range(SSD_GROUP_PAIRS)]
        for j in range(SSD_GROUP_PAIRS):
            prev_ref[j] = prevs[j]
        res = _ssd_group(*_lane_blocks(x_ref), dt_ref[...], bias_ref[...], alog_ref[...], bm_ref[...], cm_ref[...], *prevs)
        for j in range(SSD_GROUP_PAIRS):
            y_ref[:, pl.ds(j * LANES, LANES)] = res[j]
            state[j] = res[SSD_GROUP_PAIRS + j]

    return pl.pallas_call(
        body, name=name, grid=(2, t // seq, sp["ncs"]),
        in_specs=[sp["x"], sp["dt"], sp["vec"], sp["vec"], sp["bm"], sp["cm"]],
        out_specs=[sp["x"], sp["st"]],
        out_shape=[jax.ShapeDtypeStruct((t, C_WIDTH), F32),
                   jax.ShapeDtypeStruct((t // CHUNK, C_WIDTH // LANES, LANES, SSM_STATE), F32)],
        scratch_shapes=[pltpu.VMEM((SSD_GROUP_PAIRS, LANES, SSM_STATE), F32)],
        compiler_params=_cparams("parallel", "parallel", "arbitrary"),
    )(xbc_act, dt_raw, dt_bias, a_log, xbc_act, xbc_act)


def _ssd2_bwd(xbc_act, dt_raw, dt_bias, a_log, prev_saved, dy, seq, name):
    t = xbc_act.shape[0]
    sp = _ssd2_specs(seq, True)
    npair = SSD_GROUP_PAIRS

    def body(x_ref, dt_ref, bias_ref, alog_ref, bm_ref, cm_ref, prev_ref, dy_ref,
             dx_ref, ddt_ref, dbias_ref, dalog_ref, dbm_ref, dcm_ref, dstate):
        @pl.when(pl.program_id(2) == 0)
        def _():
            dstate[...] = jnp.zeros_like(dstate)

        @pl.when(jnp.logical_and(pl.program_id(1) == 0, pl.program_id(2) == 0))
        def _():
            dbias_ref[...] = jnp.zeros_like(dbias_ref)
            dalog_ref[...] = jnp.zeros_like(dalog_ref)

        _, vjp = jax.vjp(_ssd_group, *_lane_blocks(x_ref), dt_ref[...], bias_ref[...], alog_ref[...], bm_ref[...],
                         cm_ref[...], *[prev_ref[j] for j in range(npair)])
        grads = vjp(tuple(_lane_blocks(dy_ref)) + tuple(dstate[j] for j in range(npair)))
        for j in range(npair):
            dx_ref[:, pl.ds(j * LANES, LANES)] = grads[j]
            dstate[j] = grads[npair + 5 + j]
        ddt_ref[...] = grads[npair].astype(BF16)
        dbias_ref[...] += grads[npair + 1]
        dalog_ref[...] += grads[npair + 2]
        dbm_ref[...] = grads[npair + 3]
        dcm_ref[...] = grads[npair + 4]

    return pl.pallas_call(
        body, name=name, grid=(2, t // seq, sp["ncs"]),
        in_specs=[sp["x"], sp["dt"], sp["vec"], sp["vec"], sp["bm"], sp["cm"], sp["st"], sp["x"]],
        out_specs=[sp["x"], sp["dt"], sp["vec"], sp["vec"], sp["dt"], sp["dt"]],
        out_shape=[jax.ShapeDtypeStruct((t, C_WIDTH), F32), jax.ShapeDtypeStruct((t, 2 * LANES), BF16),
                   jax.ShapeDtypeStruct((1, 2 * LANES), F32), jax.ShapeDtypeStruct((1, 2 * LANES), F32),
                   jax.ShapeDtypeStruct((t, 2 * SSM_STATE), F32), jax.ShapeDtypeStruct((t, 2 * SSM_STATE), F32)],
        scratch_shapes=[pltpu.VMEM((npair, LANES, SSM_STATE), F32)],
        compiler_params=_cparams("parallel", "arbitrary", "arbitrary"),
    )(xbc_act, dt_raw, dt_bias, a_log, xbc_act, xbc_act, prev_saved, dy)


def _ssd2_assemble(dxs_ssd, dxs_skip, dbm, dcm, name):
    t = dxs_ssd.shape[0]
    tm = _row_tile(t)

    def body(a_ref, b_ref, dbm_ref, dcm_ref, o_ref):
        o_ref[:, pl.ds(0, C_WIDTH)] = a_ref[...] + b_ref[...]
        o_ref[:, pl.ds(C_WIDTH, 2 * SSM_STATE)] = dbm_ref[...]
        o_ref[:, pl.ds(C_WIDTH + 2 * SSM_STATE, 2 * SSM_STATE)] = dcm_ref[...]

    wide = pl.BlockSpec((tm, C_WIDTH), lambda i: (i, 0))
    narrow = pl.BlockSpec((tm, 2 * SSM_STATE), lambda i: (i, 0))
    return pl.pallas_call(
        body, name=name, grid=(t // tm,), in_specs=[wide, wide, narrow, narrow],
        out_specs=pl.BlockSpec((tm, D_CONV_C), lambda i: (i, 0)),
        out_shape=jax.ShapeDtypeStruct((t, D_CONV_C), F32),
        compiler_params=_cparams("parallel"),
    )(dxs_ssd, dxs_skip, dbm, dcm)


def _expand_mats():
    head = jnp.arange(LANES)[:, None]
    e64 = (head == (jnp.arange(C_WIDTH)[None, :] // HEAD_DIM)).astype(F32)
    e128 = (head == (jnp.arange(C_HEADS * LANES)[None, :] // LANES)).astype(F32)
    return e64, e128


def _ssd_prep_fn(dt_raw, dt_bias, a_log, e64, e128):
    dt = _softplus(dt_raw + dt_bias)
    a = dt * (-jnp.exp(a_log))
    incl = (_iota2((CHUNK, CHUNK), 0) >= _iota2((CHUNK, CHUNK), 1)).astype(F32)
    acs = _xdot(incl, a)
    alast = _xdot(jnp.ones((CHUNK, CHUNK), F32), a)
    return _xdot(dt, e64), _xdot(acs, e64), _xdot(alast, e64), _xdot(acs, e128)


def _ssd_prep_specs():
    blk = lambda w: pl.BlockSpec((CHUNK, w), lambda i: (i, 0))
    const = lambda r, w: pl.BlockSpec((r, w), lambda i: (0, 0))
    ins = [blk(LANES), const(1, LANES), const(1, LANES), const(LANES, C_WIDTH), const(LANES, C_HEADS * LANES)]
    outs = [blk(C_WIDTH), blk(C_WIDTH), blk(C_WIDTH), blk(C_HEADS * LANES)]
    return ins, outs


def _ssd_prep_fwd(dt_raw, dt_bias, a_log, name):
    t = dt_raw.shape[0]
    e64, e128 = _expand_mats()
    ins, outs = _ssd_prep_specs()

    def body(raw_ref, bias_ref, alog_ref, e64_ref, e128_ref, dt_ref, acs_ref, alast_ref, acs128_ref):
        res = _ssd_prep_fn(raw_ref[...], bias_ref[...], alog_ref[...], e64_ref[...], e128_ref[...])
        for ref, v in zip((dt_ref, acs_ref, alast_ref, acs128_ref), res):
            ref[...] = v

    return pl.pallas_call(
        body, name=name, grid=(t // CHUNK,), in_specs=ins, out_specs=outs,
        out_shape=[jax.ShapeDtypeStruct((t, C_WIDTH), F32)] * 3 + [jax.ShapeDtypeStruct((t, C_HEADS * LANES), F32)],
        compiler_params=_cparams("parallel"),
    )(dt_raw, dt_bias, a_log, e64, e128)


def _ssd_prep_bwd(dt_raw, dt_bias, a_log, d_dt, d_acs, d_alast, d_acs128, name):
    t = dt_raw.shape[0]
    e64, e128 = _expand_mats()
    ins, outs = _ssd_prep_specs()
    vec = pl.BlockSpec((1, LANES), lambda i: (0, 0))

    def body(raw_ref, bias_ref, alog_ref, e64_ref, e128_ref, g0, g1, g2, g3, draw_ref, dbias_ref, dalog_ref):
        @pl.when(pl.program_id(0) == 0)
        def _():
            dbias_ref[...] = jnp.zeros_like(dbias_ref)
            dalog_ref[...] = jnp.zeros_like(dalog_ref)

        e64v, e128v = e64_ref[...], e128_ref[...]
        _, vjp = jax.vjp(lambda r, b, al: _ssd_prep_fn(r, b, al, e64v, e128v),
                         raw_ref[...], bias_ref[...], alog_ref[...])
        draw, dbias, dalog = vjp((g0[...], g1[...], g2[...], g3[...]))
        draw_ref[...] = draw.astype(BF16)
        dbias_ref[...] += dbias
        dalog_ref[...] += dalog

    return pl.pallas_call(
        body, name=name, grid=(t // CHUNK,), in_specs=ins + outs,
        out_specs=[pl.BlockSpec((CHUNK, LANES), lambda i: (i, 0)), vec, vec],
        out_shape=[jax.ShapeDtypeStruct((t, LANES), BF16), jax.ShapeDtypeStruct((1, LANES), F32),
                   jax.ShapeDtypeStruct((1, LANES), F32)],
        compiler_params=_cparams("arbitrary"),
    )(dt_raw, dt_bias, a_log, e64, e128, d_dt, d_acs, d_alast, d_acs128)


def _ssd_chunk(x, dt, acs, alast, col0, col1, bm, cm, prev):
    xdt = x * dt
    cb = _nt(cm, bm)
    tri = _iota2((CHUNK, CHUNK), 0) >= _iota2((CHUNK, CHUNK), 1)
    l0 = jnp.exp(jnp.where(tri, col0 - col0.T, NEG))
    l1 = jnp.exp(jnp.where(tri, col1 - col1.T, NEG))
    ydiag = jnp.where(_lane_lt64((CHUNK, LANES)), _nn(cb * l0, xdt), _nn(cb * l1, xdt))
    states = _tn(xdt * jnp.exp(alast - acs), bm)
    yoff = _nt(cm, prev) * jnp.exp(acs)
    new = prev * jnp.exp(alast).T + states
    return ydiag + yoff, new


def _ssd_specs(seq, rev):
    ncs = seq // CHUNK
    npair = C_WIDTH // LANES

    def row(s, c):
        return s * ncs + (ncs - 1 - c if rev else c)

    return dict(
        x=pl.BlockSpec((CHUNK, LANES), lambda s, j, c: (row(s, c), j)),
        bm=pl.BlockSpec((CHUNK, SSM_STATE), lambda s, j, c: (row(s, c), C_WIDTH // LANES + j // 4)),
        cm=pl.BlockSpec((CHUNK, SSM_STATE), lambda s, j, c: (row(s, c), C_WIDTH // LANES + 2 + j // 4)),
        col=pl.BlockSpec((CHUNK, 2 * LANES), lambda s, j, c: (row(s, c), j)),
        st=pl.BlockSpec((None, None, LANES, SSM_STATE), lambda s, j, c: (row(s, c), j, 0, 0)),
        npair=npair, ncs=ncs,
    )


def _ssd_fwd(xbc_act, dt64, acs64, alast64, acs128, seq, name):
    t = xbc_act.shape[0]
    sp = _ssd_specs(seq, False)

    def body(x_ref, dt_ref, acs_ref, alast_ref, col_ref, bm_ref, cm_ref, y_ref, prev_ref, state):
        @pl.when(pl.program_id(2) == 0)
        def _():
            state[...] = jnp.zeros_like(state)

        prev = state[...]
        prev_ref[...] = prev
        y, new = _ssd_chunk(x_ref[...], dt_ref[...], acs_ref[...], alast_ref[...],
                            col_ref[:, pl.ds(0, LANES)], col_ref[:, pl.ds(LANES, LANES)],
                            bm_ref[...], cm_ref[...], prev)
        y_ref[...] = y
        state[...] = new

    return pl.pallas_call(
        body, name=name, grid=(t // seq, sp["npair"], sp["ncs"]),
        in_specs=[sp["x"], sp["x"], sp["x"], sp["x"], sp["col"], sp["bm"], sp["cm"]],
        out_specs=[sp["x"], sp["st"]],
        out_shape=[jax.ShapeDtypeStruct((t, C_WIDTH), F32),
                   jax.ShapeDtypeStruct((t // CHUNK, sp["npair"], LANES, SSM_STATE), F32)],
        scratch_shapes=[pltpu.VMEM((LANES, SSM_STATE), F32)],
        compiler_params=_cparams("parallel", "parallel", "arbitrary"),
    )(xbc_act, dt64, acs64, alast64, acs128, xbc_act, xbc_act)


def _ssd_bwd(xbc_act, dt64, acs64, alast64, acs128, prev_saved, dy, seq, name):
    t = xbc_act.shape[0]
    sp = _ssd_specs(seq, True)

    def body(x_ref, dt_ref, acs_ref, alast_ref, col_ref, bm_ref, cm_ref, prev_ref, dy_ref,
             dx_ref, ddt_ref, dacs_ref, dalast_ref, dcol_ref, dbc_ref, dstate):
        @pl.when(pl.program_id(2) == 0)
        def _():
            dstate[...] = jnp.zeros_like(dstate)

        _, vjp = jax.vjp(_ssd_chunk, x_ref[...], dt_ref[...], acs_ref[...], alast_ref[...],
                         col_ref[:, pl.ds(0, LANES)], col_ref[:, pl.ds(LANES, LANES)],
                         bm_ref[...], cm_ref[...], prev_ref[...])
        dx, ddt, dacs, dalast, dc0, dc1, dbm, dcm, dprev = vjp((dy_ref[...], dstate[...]))
        dx_ref[...] = dx
        ddt_ref[...] = ddt
        dacs_ref[...] = dacs
        dalast_ref[...] = dalast
        dcol_ref[:, pl.ds(0, LANES)] = dc0
        dcol_ref[:, pl.ds(LANES, LANES)] = dc1
        dbc_ref[:, pl.ds(0, SSM_STATE)] = dbm
        dbc_ref[:, pl.ds(SSM_STATE, SSM_STATE)] = dcm
        dstate[...] = dprev

    wide = jax.ShapeDtypeStruct((t, C_WIDTH), F32)
    return pl.pallas_call(
        body, name=name, grid=(t // seq, sp["npair"], sp["ncs"]),
        in_specs=[sp["x"], sp["x"], sp["x"], sp["x"], sp["col"], sp["bm"], sp["cm"], sp["st"], sp["x"]],
        out_specs=[sp["x"], sp["x"], sp["x"], sp["x"], sp["col"], sp["col"]],
        out_shape=[wide, wide, wide, wide, jax.ShapeDtypeStruct((t, 2 * C_WIDTH), F32),
                   jax.ShapeDtypeStruct((t, 2 * C_WIDTH), F32)],
        scratch_shapes=[pltpu.VMEM((LANES, SSM_STATE), F32)],
        compiler_params=_cparams("parallel", "parallel", "arbitrary"),
    )(xbc_act, dt64, acs64, alast64, acs128, xbc_act, xbc_act, prev_saved, dy)


def _ssd_post_fn(y, xs, z, dskip, g):
    v = (y + dskip * xs) * _silu(z)
    return v * lax.rsqrt(jnp.mean(v * v, axis=-1, keepdims=True) + EPS) * g


def _ssd_post_specs(tm, order):
    gw = C_WIDTH // 2

    def im(f):
        return lambda *ids: f(*order(*ids))
    return dict(
        blk=pl.BlockSpec((tm, gw), im(lambda g, r: (r, g))),
        z=pl.BlockSpec((tm, gw), im(lambda g, r: (r, COL_Z * LANES // gw + g))),
        vec=pl.BlockSpec((1, gw), im(lambda g, r: (0, g))),
    )


def _ssd_post_fwd(y_ssd, xbc_act, proj, dskip64, norm_g, name):
    t = y_ssd.shape[0]
    tm = _row_tile(t)
    sp = _ssd_post_specs(tm, lambda r, g: (g, r))

    def body(y_ref, xs_ref, z_ref, ds_ref, g_ref, o_ref):
        o_ref[...] = _ssd_post_fn(y_ref[...], xs_ref[...], z_ref[...], ds_ref[...], g_ref[...]).astype(BF16)

    return pl.pallas_call(
        body, name=name, grid=(t // tm, 2),
        in_specs=[sp["blk"], sp["blk"], sp["z"], sp["vec"], sp["vec"]], out_specs=sp["blk"],
        out_shape=jax.ShapeDtypeStruct((t, C_WIDTH), BF16),
        compiler_params=_cparams("parallel", "parallel"),
    )(y_ssd, xbc_act, proj, dskip64, norm_g)


def _ssd_post_bwd(y_ssd, xbc_act, proj, dskip64, norm_g, dyc, name, dy_col=0):
    t = y_ssd.shape[0]
    tm = _row_tile(t)
    sp = _ssd_post_specs(tm, lambda g, r: (g, r))
    dy_spec = pl.BlockSpec((tm, C_WIDTH // 2), lambda g, r: (r, dy_col + g))

    def body(y_ref, xs_ref, z_ref, ds_ref, g_ref, dyc_ref, dy_ref, dxs_ref, dz_ref, dds_ref, dg_ref):
        @pl.when(pl.program_id(1) == 0)
        def _():
            dds_ref[...] = jnp.zeros_like(dds_ref)
            dg_ref[...] = jnp.zeros_like(dg_ref)

        _, vjp = jax.vjp(_ssd_post_fn, y_ref[...], xs_ref[...], z_ref[...], ds_ref[...], g_ref[...])
        dy, dxs, dz, dds, dg = vjp(dyc_ref[...])
        dy_ref[...] = dy
        dxs_ref[...] = dxs
        dz_ref[...] = dz.astype(BF16)
        dds_ref[...] += dds
        dg_ref[...] += dg

    wide = jax.ShapeDtypeStruct((t, C_WIDTH), F32)
    vec = jax.ShapeDtypeStruct((1, C_WIDTH), F32)
    return pl.pallas_call(
        body, name=name, grid=(2, t // tm),
        in_specs=[sp["blk"], sp["blk"], sp["z"], sp["vec"], sp["vec"], dy_spec],
        out_specs=[sp["blk"], sp["blk"], sp["blk"], sp["vec"], sp["vec"]],
        out_shape=[wide, wide, jax.ShapeDtypeStruct((t, C_WIDTH), BF16), vec, vec],
        compiler_params=_cparams("parallel", "arbitrary"),
    )(y_ssd, xbc_act, proj, dskip64, norm_g, dyc)


def _ssd_assemble(dxs_ssd, dxs_skip, dbc, name):
    t = dxs_ssd.shape[0]
    tm = _row_tile(t)
    npair = C_WIDTH // LANES

    def body(a_ref, b_ref, dbc_ref, o_ref):
        o_ref[:, pl.ds(0, C_WIDTH)] = a_ref[...] + b_ref[...]
        for grp in range(2):
            for which in range(2):
                acc = jnp.zeros((tm, SSM_STATE), F32)
                for j in range(grp * npair // 2, (grp + 1) * npair // 2):
                    acc = acc + dbc_ref[:, pl.ds((2 * j + which) * SSM_STATE, SSM_STATE)]
                o_ref[:, pl.ds(C_WIDTH + (2 * which + grp) * SSM_STATE, SSM_STATE)] = acc

    return pl.pallas_call(
        body, name=name, grid=(t // tm,),
        in_specs=[pl.BlockSpec((tm, C_WIDTH), lambda i: (i, 0))] * 2 + [pl.BlockSpec((tm, 2 * C_WIDTH), lambda i: (i, 0))],
        out_specs=pl.BlockSpec((tm, D_CONV_C), lambda i: (i, 0)),
        out_shape=jax.ShapeDtypeStruct((t, D_CONV_C), F32),
        compiler_params=_cparams("parallel"),
    )(dxs_ssd, dxs_skip, dbc)


def _pad_taps(w):
    return jnp.pad(w, ((0, CONV_PAD - w.shape[0]), (0, 0)))


def _pad_heads(v):
    return jnp.pad(v, (0, LANES - v.shape[0])).reshape(1, LANES)


def _group_heads(a):
    pad = [(0, 0)] * (a.ndim - 1) + [(0, LANES - SSD_GROUP_HEADS)]
    return jnp.concatenate([jnp.pad(a[..., :SSD_GROUP_HEADS], pad), jnp.pad(a[..., SSD_GROUP_HEADS:], pad)], axis=-1)


def _ungroup_heads(a):
    return jnp.concatenate([a[..., :SSD_GROUP_HEADS], a[..., LANES:LANES + SSD_GROUP_HEADS]], axis=-1)


def _layer_fwd(x, p, seq, li, after=()):
    n = f"l{li}_"
    h1 = _rms_fwd(x, p["norm1_g"], n + "rms1", after=after)
    proj = _matmul(h1, p["w_main"], mode="nn", name=n + "inproj")
    dt_raw = _matmul(h1, p["w_dt"], mode="nn", name=n + "inproj_dt")
    row = lambda v: v.reshape(1, -1)
    ya = _conv_fwd("a", proj, _pad_taps(p["conv_a_w"]), row(p["conv_a_b"]), (row(p["ln_a_g"]), row(p["ln_a_b"])),
                   seq, n + "conva", out_dtype=BF16)
    yb = _gmlp_fwd(proj, row(p["ln_b_g"]), row(p["ln_b_b"]), p["w_spatial"], p["b_spatial"], n + "gmlp")
    xbc_act = _conv_fwd("c", proj, _pad_taps(p["conv_c_w"]), row(p["conv_c_b"]), (), seq, n + "convc")
    y_ssd, prev = _ssd2_fwd(xbc_act, dt_raw, _group_heads(row(p["dt_bias"])), _group_heads(row(p["a_log"])), seq, n + "ssd")
    dskip64 = jnp.repeat(p["d_skip"], HEAD_DIM).reshape(1, C_WIDTH)
    yc = _ssd_post_fwd(y_ssd, xbc_act, proj, dskip64, row(p["norm_c_g"]), n + "ssdpost")
    ycat = jnp.concatenate([ya, yb, yc], axis=1)
    x1 = _matmul(ycat, p["w_out"], mode="nn", name=n + "outproj", add=x)
    h2 = _rms_fwd(x1, p["norm2_g"], n + "rms2")
    u, act = _matmul(h2, p["w_ff1"], mode="nn", name=n + "ff1", epilogue=_relu2_epilogue, out_dtypes=(F32, BF16))
    x2 = _matmul(act, p["w_ff2"], mode="nn", name=n + "ff2", add=x1)
    saved = dict(x=x, h1=h1, proj=proj, dt_raw=dt_raw, xbc_act=xbc_act, prev=prev, y_ssd=y_ssd, dskip64=dskip64,
                 ycat=ycat, x1=x1, h2=h2, u=u, act=act)
    return x2, saved


def _layer_bwd(dx2, p, s, seq, li, after=()):
    n = f"l{li}_b_"
    row = lambda v: v.reshape(1, -1)
    g = {}
    du = _matmul(dx2, p["w_ff2"], mode="nt", name=n + "ff2_dx", epilogue=_relu2_bwd_epilogue, extra=s["u"],
                 out_dtypes=(BF16,), after=after)
    g["w_ff2"] = _matmul(s["act"], dx2, mode="tn", name=n + "ff2_dw")
    g["w_ff1"] = _matmul(s["h2"], du, mode="tn", name=n + "ff1_dw")
    dh2 = _matmul(du, p["w_ff1"], mode="nt", name=n + "ff1_dx")
    dx1, g["norm2_g"] = _rms_bwd(s["x1"], p["norm2_g"], dh2, dx2, n + "rms2")
    g["w_out"] = _matmul(s["ycat"], dx1, mode="tn", name=n + "out_dw")
    dycat = _matmul(dx1, p["w_out"], mode="nt", name=n + "out_dx")
    proj = s["proj"]
    (dval, dgate), dwa, dba, (dlag, dlab) = _conv_bwd(
        "a", proj, _pad_taps(p["conv_a_w"]), row(p["conv_a_b"]), (row(p["ln_a_g"]), row(p["ln_a_b"])), dycat, seq,
        n + "conva", dy_col=0)
    g["conv_a_w"], g["conv_a_b"], g["ln_a_g"], g["ln_a_b"] = dwa[:CONV_A_K], dba[0], dlag[0], dlab[0]
    dbu, dbv, dlbg, dlbb, g["w_spatial"], g["b_spatial"] = _gmlp_bwd(
        proj, row(p["ln_b_g"]), row(p["ln_b_b"]), p["w_spatial"], p["b_spatial"], dycat, n + "gmlp",
        dy_col=A_WIDTH // LANES)
    g["ln_b_g"], g["ln_b_b"] = dlbg[0], dlbb[0]
    dy_ssd, dxs_skip, dz, dds, dncg = _ssd_post_bwd(s["y_ssd"], s["xbc_act"], proj, s["dskip64"], row(p["norm_c_g"]),
                                                    dycat, n + "ssdpost", dy_col=(A_WIDTH + B_WIDTH) * 2 // C_WIDTH)
    g["norm_c_g"] = dncg[0]
    g["d_skip"] = dds.reshape(C_HEADS, HEAD_DIM).sum(axis=1)
    dxs, ddt_raw, ddtb, dalog, dbm, dcm = _ssd2_bwd(
        s["xbc_act"], s["dt_raw"], _group_heads(row(p["dt_bias"])), _group_heads(row(p["a_log"])), s["prev"], dy_ssd, seq,
        n + "ssd")
    g["dt_bias"], g["a_log"] = _ungroup_heads(ddtb)[0], _ungroup_heads(dalog)[0]
    dconv = _ssd2_assemble(dxs, dxs_skip, dbm, dcm, n + "ssdasm")
    (dxbc,), dwc, dbcv, _ = _conv_bwd("c", proj, _pad_taps(p["conv_c_w"]), row(p["conv_c_b"]), (), dconv, seq, n + "convc")
    g["conv_c_w"], g["conv_c_b"] = dwc[:CONV_C_K], dbcv[0]
    dproj = jnp.concatenate([dval, dgate, dbu, dbv, dz, dxbc], axis=1)
    g["w_main"] = _matmul(s["h1"], dproj, mode="tn", name=n + "in_dw")
    g["w_dt"] = _matmul(s["h1"], ddt_raw, mode="tn", name=n + "indt_dw")
    dh1 = _matmul(dproj, p["w_main"], mode="nt", name=n + "in_dx")
    dh1 = _matmul(ddt_raw, p["w_dt"], mode="nt", name=n + "indt_dx", add=dh1)
    dx, g["norm1_g"] = _rms_bwd(s["x"], p["norm1_g"], dh1, dx1, n + "rms1")
    return dx, g


EW_BLOCK_BYTES = 1 << 20


def _ew(fn, ins, out_dtypes, name, leads=None):
    leads = leads or [None] * len(ins)
    rows, c = ins[0].shape[-2:]
    tr = _pick(rows, [t for t in (2048, 1024, 512, 256, 128, 64, 32, 16, 8) if t * c * 4 <= EW_BLOCK_BYTES])
    n_in = len(ins)

    def spec(lead):
        if lead is None:
            return pl.BlockSpec((tr, c), lambda i: (i, 0))
        return pl.BlockSpec((None, tr, c), functools.partial(lambda i, k: (k, i, 0), k=lead))

    def body(*refs):
        outs = fn(*[r[...].astype(F32) for r in refs[:n_in]])
        for o_ref, o in zip(refs[n_in:], outs):
            o_ref[...] = o.astype(o_ref.dtype)

    return pl.pallas_call(
        body, name=name, grid=(rows // tr,),
        in_specs=[spec(l) for l in leads], out_specs=[spec(None)] * len(out_dtypes),
        out_shape=[jax.ShapeDtypeStruct((rows, c), dt) for dt in out_dtypes],
        compiler_params=_cparams("parallel"),
    )(*ins)


def _adam_fn(w, g, m, v):
    m2 = ADAM_B1 * m + (1.0 - ADAM_B1) * g
    v2 = ADAM_B2 * v + (1.0 - ADAM_B2) * (g * g)
    m_hat = m2 / (1.0 - ADAM_B1 ** ADAM_STEP)
    v_hat = v2 / (1.0 - ADAM_B2 ** ADAM_STEP)
    delta = -ADAM_LR * (m_hat / (jnp.sqrt(v_hat) + ADAM_EPS) + ADAM_WD * w)
    return delta, m2, v2


def _adam(w, g, m, v, name):
    shape = w.shape
    two_d = lambda a: a.reshape(-1, shape[-1])
    outs = _ew(_adam_fn, [two_d(w), two_d(g), two_d(m), two_d(v)], (F32, F32, F32), name)
    return [o.reshape(shape) for o in outs]


_ANY = pl.BlockSpec(memory_space=pl.ANY)


def _mesh_pos():
    return lax.axis_index("x"), lax.axis_index("y"), lax.axis_index("c")


def _peer_chips(x, y):
    return [(1 - x, y), (x, 1 - y), (1 - x, 1 - y)]


def _remote(src, dst, send_sems, recv_sems, sem, to):
    return pltpu.make_async_remote_copy(src_ref=src, dst_ref=dst, send_sem=send_sems.at[sem],
                                        recv_sem=recv_sems.at[sem], device_id=to, device_id_type=MESH)


def _half_rows(n_rows, which):
    half = n_rows // 2
    return pl.ds(pl.multiple_of(which * half, 8), half)


def _comm_call(body, ins, out_shapes, n_sems, name):
    scratch = [pltpu.SemaphoreType.DMA((n_sems,)), pltpu.SemaphoreType.DMA((n_sems,))]
    return pl.pallas_call(
        body, name=name, in_specs=[_ANY] * len(ins), out_specs=[_ANY] * len(out_shapes),
        out_shape=out_shapes, scratch_shapes=scratch,
    )(*ins)


def _gather_weights(big, small, name):
    nb, ns = len(big), len(small)
    n = nb + ns

    def body(*refs):
        ins, outs = refs[:n], refs[n:2 * n]
        send_sems, recv_sems = refs[2 * n:]
        x, y, c = _mesh_pos()
        q = 2 * x + y
        me, sib = (x, y, c), (x, y, 1 - c)
        chips = _peer_chips(x, y)
        rem = functools.partial(_remote, send_sems=send_sems, recv_sems=recv_sems)
        first = []
        for i in range(nb):
            mine = _half_rows(big[i].shape[0], c)
            for k, (px, py) in enumerate(chips):
                first.append(rem(ins[i].at[mine], outs[i].at[q, mine], sem=6 * i + k, to=(px, py, c)))
        for j in range(ns):
            for k, (px, py) in enumerate(chips):
                first.append(rem(ins[nb + j], outs[nb + j].at[q], sem=6 * nb + 3 * j + k, to=(px, py, c)))
        for cp in first:
            cp.start()
        passed = []
        for i in range(nb):
            mine = _half_rows(big[i].shape[0], c)
            for k, (px, py) in enumerate(chips):
                landed = outs[i].at[2 * px + py, mine]
                rem(landed, landed, sem=6 * i + k, to=me).wait_recv()
                fwd = rem(landed, landed, sem=6 * i + 3 + k, to=sib)
                fwd.start()
                passed.append(fwd)
        for i in range(nb):
            other = _half_rows(big[i].shape[0], 1 - c)
            for k, (px, py) in enumerate(chips):
                theirs = outs[i].at[2 * px + py, other]
                rem(theirs, theirs, sem=6 * i + 3 + k, to=me).wait_recv()
        for j in range(ns):
            for k, (px, py) in enumerate(chips):
                dst = outs[nb + j].at[2 * px + py]
                rem(dst, dst, sem=6 * nb + 3 * j + k, to=me).wait_recv()
        for cp in first + passed:
            cp.wait_send()

    out_shapes = [jax.ShapeDtypeStruct((N_CHIPS,) + a.shape, a.dtype) for a in list(big) + list(small)]
    return _comm_call(body, list(big) + list(small), out_shapes, 6 * nb + 3 * ns, name)


def _sibling_other_halves(gs, name):
    n = len(gs)

    def body(*refs):
        ins, outs = refs[:n], refs[n:2 * n]
        send_sems, recv_sems = refs[2 * n:]
        x, y, c = _mesh_pos()
        copies = [_remote(ins[i].at[:, _half_rows(gs[i].shape[1], 1 - c)], outs[i], send_sems, recv_sems, i, (x, y, 1 - c))
                  for i in range(n)]
        for cp in copies:
            cp.start()
        for cp in copies:
            cp.wait()

    out_shapes = [jax.ShapeDtypeStruct((N_CHIPS, g.shape[1] // 2, g.shape[2]), g.dtype) for g in gs]
    return _comm_call(body, list(gs), out_shapes, n, name)


def _chip_scatter(cs, name):
    n = len(cs)

    def body(*refs):
        ins, outs = refs[:n], refs[n:2 * n]
        send_sems, recv_sems = refs[2 * n:]
        x, y, c = _mesh_pos()
        copies = []
        for i in range(n):
            for k, (px, py) in enumerate(_peer_chips(x, y)):
                copies.append(_remote(ins[i].at[2 * px + py], outs[i].at[k], send_sems, recv_sems, 3 * i + k, (px, py, c)))
        for cp in copies:
            cp.start()
        for cp in copies:
            cp.wait()

    out_shapes = [jax.ShapeDtypeStruct((3,) + a.shape[1:], a.dtype) for a in cs]
    return _comm_call(body, list(cs), out_shapes, 3 * n, name)


_HBM = pl.BlockSpec(memory_space=pltpu.HBM)
_SEM = pl.BlockSpec(memory_space=pltpu.SEMAPHORE)


def _in_hbm(a):
    return pltpu.with_memory_space_constraint(a, pltpu.HBM)


def _split_plan(kind, srcs, lands, x, y, c):
    plan = []
    for src, land in zip(srcs, lands):
        for k, (px, py) in enumerate(_peer_chips(x, y)):
            if kind == "scatter":
                plan.append((src.at[2 * px + py], land.at[k], (px, py, c)))
            else:
                plan.append((src, land.at[2 * x + y], (px, py, c)))
    return plan


def _split_start(kind, srcs, land_shapes, name):
    n = len(srcs)

    def body(*refs):
        ins, lands = refs[:n], refs[n:2 * n]
        send_sems, recv_sems = refs[2 * n], refs[2 * n + 1]
        token = refs[-1]
        x, y, c = _mesh_pos()
        for i, (src, dst, to) in enumerate(_split_plan(kind, ins, lands, x, y, c)):
            pltpu.make_async_remote_copy(src_ref=src, dst_ref=dst, send_sem=send_sems.at[i], recv_sem=recv_sems.at[i],
                                         device_id=to, device_id_type=MESH).start()
        token[...] = jnp.zeros_like(token)

    zones = [lax.empty(s.shape, s.dtype) for s in land_shapes]
    n_sems = 3 * n
    res = pl.pallas_call(
        body, name=name,
        out_shape=(pltpu.SemaphoreType.DMA((n_sems,)), pltpu.SemaphoreType.DMA((n_sems,)),
                   *[pltpu.HBM(a.shape, a.dtype) for a in srcs], *[pltpu.HBM(s.shape, s.dtype) for s in land_shapes],
                   jax.ShapeDtypeStruct((8, LANES), F32)),
        in_specs=[_HBM] * (2 * n), out_specs=(_SEM, _SEM, *[_HBM] * (2 * n), pl.BlockSpec(memory_space=pltpu.VMEM)),
        input_output_aliases={i: 2 + i for i in range(2 * n)},
        compiler_params=pltpu.CompilerParams(has_side_effects=pltpu.SideEffectType.DATAFLOW_SIDE_EFFECTING),
    )(*[_in_hbm(a) for a in srcs], *[_in_hbm(z) for z in zones])
    return dict(send=res[0], recv=res[1], srcs=list(res[2:2 + n]), lands=list(res[2 + n:2 + 2 * n]), token=res[-1], kind=kind)


def _split_wait(started, after, name):
    n = len(started["srcs"])
    kind = started["kind"]

    def body(*refs):
        ins, lands = refs[:n], refs[n:2 * n]
        send_sems, recv_sems = refs[2 * n], refs[2 * n + 1]
        x, y, c = _mesh_pos()
        for i, (src, dst, _) in enumerate(_split_plan(kind, ins, lands, x, y, c)):
            cp = pltpu.make_async_remote_copy(src_ref=src, dst_ref=dst, send_sem=send_sems.at[i], recv_sem=recv_sems.at[i],
                                              device_id=(x, y, c), device_id_type=MESH)
            cp.wait_send()
            cp.wait_recv()

    arrs = started["srcs"] + started["lands"]
    res = pl.pallas_call(
        body, name=name, out_shape=tuple(pltpu.HBM(a.shape, a.dtype) for a in arrs),
        in_specs=[_HBM] * (2 * n) + [_SEM, _SEM, pl.BlockSpec(memory_space=pl.ANY)], out_specs=tuple([_HBM] * (2 * n)),
        input_output_aliases={i: i for i in range(2 * n)},
        compiler_params=pltpu.CompilerParams(has_side_effects=pltpu.SideEffectType.DATAFLOW_SIDE_EFFECTING),
    )(*arrs, started["send"], started["recv"], after)
    return list(res[n:])


def _sibling_share(fs, name):
    n = len(fs)

    def body(*refs):
        ins, outs = refs[:n], refs[n:2 * n]
        send_sems, recv_sems = refs[2 * n:]
        x, y, c = _mesh_pos()
        copies = [_remote(ins[i], outs[i], send_sems, recv_sems, i, (x, y, 1 - c)) for i in range(n)]
        for cp in copies:
            cp.start()
        for cp in copies:
            cp.wait()

    out_shapes = [jax.ShapeDtypeStruct(a.shape, a.dtype) for a in fs]
    return _comm_call(body, list(fs), out_shapes, n, name)


def _allgather8(v, name):
    m = v.shape[0]

    def body(v_ref, out_ref, send_sems, recv_sems):
        x, y, c = _mesh_pos()
        me, sib = (x, y, c), (x, y, 1 - c)
        chips = _peer_chips(x, y)
        rem = functools.partial(_remote, send_sems=send_sems, recv_sems=recv_sems)

        def blk(px, py, pc):
            return out_ref.at[4 * px + 2 * py + pc]

        first = [rem(v_ref, blk(*me), sem=0, to=sib)]
        first += [rem(v_ref, blk(*me), sem=1 + k, to=(px, py, c)) for k, (px, py) in enumerate(chips)]
        for cp in first:
            cp.start()
        passed = []
        for k, (px, py) in enumerate(chips):
            landed = blk(px, py, c)
            rem(landed, landed, sem=1 + k, to=me).wait_recv()
            fwd = rem(landed, landed, sem=4 + k, to=sib)
            fwd.start()
            passed.append(fwd)
        rem(blk(*sib), blk(*sib), sem=0, to=me).wait_recv()
        for k, (px, py) in enumerate(chips):
            theirs = blk(px, py, 1 - c)
            rem(theirs, theirs, sem=4 + k, to=me).wait_recv()
        for cp in first + passed:
            cp.wait_send()

    return _comm_call(body, [v], [jax.ShapeDtypeStruct((8, m, LANES), v.dtype)], 7, name)[0]


_WEIGHTS = ["norm1_g", "w_in", "conv_a_w", "conv_a_b", "ln_a_g", "ln_a_b", "ln_b_g", "ln_b_b", "w_spatial", "b_spatial",
            "conv_c_w", "conv_c_b", "dt_bias", "a_log", "d_skip", "norm_c_g", "w_out", "norm2_g", "w_ff1", "w_ff2", "final_g"]
_BIG = ["w_in", "w_out", "w_ff1", "w_ff2"]
_CONV_SHARDED = ["conv_a_w", "conv_c_w"]
_SMALL = [w for w in _WEIGHTS if w not in _BIG and w != "final_g"]
_PACK_ROWS = 512


def _pack(arrs):
    flat = jnp.concatenate([a.reshape(-1) for a in arrs])
    blk = _PACK_ROWS * LANES
    n = flat.shape[0]
    return jnp.pad(flat, (0, -(-n // blk) * blk - n)).reshape(-1, LANES)


def _unpack(packed, shapes):
    flat = packed.reshape(-1)
    out, off = [], 0
    for s in shapes:
        n = math.prod(s)
        out.append(flat[off:off + n].reshape(s))
        off += n
    return out


def _cols_to_chips(a):
    k = a.shape[0]
    return a.reshape(k, N_CHIPS, -1).transpose(1, 0, 2)


def _chips_to_cols(a):
    return a.transpose(1, 0, 2).reshape(a.shape[1], -1)


def _own_shards(w, li):
    return [w[k][li].astype(BF16) for k in _BIG] + [w[k][li] for k in _CONV_SHARDED]


def _layer_params(w, li, own, gathered, q):
    g_in, g_out, g_ff1, g_ff2, g_ca, g_cc = [lax.dynamic_update_index_in_dim(g, o, q, axis=0)
                                             for g, o in zip(gathered, own)]
    p = {k: w[k][li] for k in _SMALL if k not in _CONV_SHARDED}
    w_in = _chips_to_cols(g_in)
    p["w_main"] = w_in[:, :D_MAIN]
    p["w_dt"] = _group_heads(w_in[:, D_MAIN:])
    p["w_out"] = g_out.reshape(D_MIX, D_MODEL)
    p["w_ff1"] = _chips_to_cols(g_ff1)
    p["w_ff2"] = g_ff2.reshape(D_FF, D_MODEL)
    p["conv_a_w"] = _chips_to_cols(g_ca)
    p["conv_c_w"] = _chips_to_cols(g_cc)
    return p


def _chip_sums(g, li, c, q):
    n = f"l{li}_rs_"
    g_in = jnp.concatenate([g["w_main"], _ungroup_heads(g["w_dt"])], axis=1)
    full = [_cols_to_chips(g_in), g["w_out"].reshape(N_CHIPS, -1, D_MODEL), _cols_to_chips(g["w_ff1"]),
            g["w_ff2"].reshape(N_CHIPS, -1, D_MODEL)]
    from_sib = _sibling_other_halves(full, n + "sib")
    chip_f32, chip_bf16 = [], []
    for i, (a, b) in enumerate(zip(full, from_sib)):
        r2, cols = b.shape[1:]
        mine = lax.dynamic_slice_in_dim(a, c * r2, r2, axis=1)
        s32, s16 = _ew(lambda u, v: (u + v, u + v), [mine.reshape(-1, cols), b.reshape(-1, cols)], (F32, BF16),
                       n + f"chipsum{i}")
        chip_f32.append(lax.dynamic_index_in_dim(s32.reshape(b.shape), q, axis=0, keepdims=False))
        chip_bf16.append(s16.reshape(b.shape))
    return chip_f32, chip_bf16


def _finish_reduce(chip_f32, from_chips, li, c):
    n = f"l{li}_rs_"
    halves = [_ew(lambda o, r0, r1, r2_: (((o + r0) + r1) + r2_,), [own, rb, rb, rb], (F32,), n + f"final{i}",
                  leads=[None, 0, 1, 2])[0] for i, (own, rb) in enumerate(zip(chip_f32, from_chips))]
    from_sib = _sibling_share(halves, n + "share")
    return [jnp.where(c == 0, jnp.concatenate([h, s], axis=0), jnp.concatenate([s, h], axis=0))
            for h, s in zip(halves, from_sib)]


def kernel(x, norm1_g, w_in, conv_a_w, conv_a_b, ln_a_g, ln_a_b, ln_b_g, ln_b_b, w_spatial, b_spatial, conv_c_w, conv_c_b, dt_bias, a_log, d_skip, norm_c_g, w_out, norm2_g, w_ff1, w_ff2, final_g, loss_target, m_norm1_g, m_w_in, m_conv_a_w, m_conv_a_b, m_ln_a_g, m_ln_a_b, m_ln_b_g, m_ln_b_b, m_w_spatial, m_b_spatial, m_conv_c_w, m_conv_c_b, m_dt_bias, m_a_log, m_d_skip, m_norm_c_g, m_w_out, m_norm2_g, m_w_ff1, m_w_ff2, m_final_g, v_norm1_g, v_w_in, v_conv_a_w, v_conv_a_b, v_ln_a_g, v_ln_a_b, v_ln_b_g, v_ln_b_b, v_w_spatial, v_b_spatial, v_conv_c_w, v_conv_c_b, v_dt_bias, v_a_log, v_d_skip, v_norm_c_g, v_w_out, v_norm2_g, v_w_ff1, v_w_ff2, v_final_g):
    given = dict(locals())
    w = {k: given[k] for k in _WEIGHTS}
    m = {k: given["m_" + k] for k in _WEIGHTS}
    v = {k: given["v_" + k] for k in _WEIGHTS}
    depth = w_in.shape[0]
    nseq, seq, d = x.shape
    xi, yi, ci = _mesh_pos()
    q = 2 * xi + yi

    own = [_own_shards(w, li) for li in range(depth)]
    nb = len(_BIG)
    gathered = _gather_weights(own[0][:nb], own[0][nb:], "l0_gather")
    h = x.reshape(nseq * seq, d)
    layer_params, saved = [], []
    for li in range(depth):
        nxt = None
        if li + 1 < depth:
            srcs, _ = lax.optimization_barrier((own[li + 1], gathered))
            zones = [jax.ShapeDtypeStruct((N_CHIPS,) + a.shape, a.dtype) for a in srcs]
            nxt = _split_start("gather", srcs, zones, f"l{li + 1}_gather_start")
        layer_params.append(_layer_params(w, li, own[li], gathered, q))
        h, s = _layer_fwd(h, layer_params[li], seq, li, after=() if nxt is None else (nxt["token"],))
        saved.append(s)
        if nxt is not None:
            gathered = _split_wait(nxt, h, f"l{li + 1}_gather_wait")
    loss, dx, d_final = _loss_head(h, final_g, loss_target.reshape(nseq * seq, d))

    grads = [None] * depth
    big_grads = [None] * depth
    pending = None
    for li in reversed(range(depth)):
        dx, grads[li] = _layer_bwd(dx, layer_params[li], saved[li], seq, li,
                                   after=() if pending is None else (pending[1]["token"],))
        if pending is not None:
            lj, scatter, chip_f32 = pending
            big_grads[lj] = _finish_reduce(chip_f32, _split_wait(scatter, dx, f"l{lj}_rs_scatter_wait"), lj, ci)
        chip_f32, chip_bf16 = _chip_sums(grads[li], li, ci, q)
        if li > 0:
            lands = [jax.ShapeDtypeStruct((3,) + a.shape[1:], a.dtype) for a in chip_bf16]
            pending = (li, _split_start("scatter", chip_bf16, lands, f"l{li}_rs_scatter_start"), chip_f32)
        else:
            big_grads[li] = _finish_reduce(chip_f32, _chip_scatter(chip_bf16, "l0_rs_scatter"), li, ci)
    grad_out, delta_out, m_out, v_out = {}, {}, {}, {}
    for i, k in enumerate(_BIG):
        grad_out[k] = jnp.stack([big_grads[li][i] for li in range(depth)])
        delta_out[k], m_out[k], v_out[k] = _adam(w[k], grad_out[k], m[k], v[k], "adam_" + k)

    small_shapes = [grads[0][k].shape for k in _SMALL]
    parts = [grads[li][k] for li in range(depth) for k in _SMALL] + [d_final, loss.reshape(1)]
    packed_parts = _pack(parts)
    gathered = lax.dynamic_update_index_in_dim(_allgather8(packed_parts, "small_allgather"), packed_parts,
                                               2 * q + ci, axis=0)

    def sum8(*blocks):
        acc = blocks[0]
        for b in blocks[1:]:
            acc = acc + b
        return (acc,)

    total = _ew(sum8, [gathered] * 8, (F32,), "small_sum", leads=list(range(8)))[0]
    summed = _unpack(total, small_shapes * depth + [d_final.shape, (1,)])
    loss_total = summed[-1][0]
    small_grads = {k: jnp.stack([summed[li * len(_SMALL) + i] for li in range(depth)]) for i, k in enumerate(_SMALL)}
    small_grads["final_g"] = summed[-2]
    for k in _CONV_SHARDED:
        n_shard = w[k].shape[-1]
        small_grads[k] = lax.dynamic_slice_in_dim(small_grads[k], q * n_shard, n_shard, axis=2)
    names = _SMALL + ["final_g"]
    shapes = [w[k].shape for k in names]
    packed = [_pack([src[k] for k in names]) for src in (w, small_grads, m, v)]
    outs = _ew(_adam_fn, packed, (F32, F32, F32), "adam_small")
    for dst, o in zip((delta_out, m_out, v_out), outs):
        for k, a in zip(names, _unpack(o, shapes)):
            dst[k] = a
    for k in names:
        grad_out[k] = small_grads[k]

    return (loss_total, dx.reshape(nseq, seq, d), *[grad_out[k] for k in _WEIGHTS], *[delta_out[k] for k in _WEIGHTS],
            *[m_out[k] for k in _WEIGHTS], *[v_out[k] for k in _WEIGHTS])
```

```python
import functools
import math

import jax
import jax.numpy as jnp
from jax import lax
from jax.experimental import pallas as pl
from jax.experimental.pallas import tpu as pltpu

F32 = jnp.float32
BF16 = jnp.bfloat16
MESH = pl.DeviceIdType.MESH

D_MODEL = 1024
DEPTH = 4
HEAD_DIM = 64
A_WIDTH = 512
B_WIDTH = 512
C_WIDTH = 1024
C_HEADS = 16
CONV_A_K = 31
CONV_C_K = 4
CHUNK = 128
SSM_STATE = 128
D_CONV_C = 1536
D_MAIN = 4608
D_IN_PROJ = 4624
D_MIX = 2048
D_FF = 4096
EPS = 1e-5
NEG = -1e30
LANES = 128
CONV_PAD = 32
N_CHIPS = 4

ADAM_LR = 0.001
ADAM_B1 = 0.9
ADAM_B2 = 0.999
ADAM_EPS = 1e-08
ADAM_WD = 0.01
ADAM_STEP = 10

VMEM_LIMIT = 56 * 1024 * 1024

COL_AVAL, COL_AGATE, COL_BU, COL_BV, COL_Z, COL_XBC = 0, 4, 8, 12, 16, 24


def _cparams(*sem):
    return pltpu.CompilerParams(dimension_semantics=sem, vmem_limit_bytes=VMEM_LIMIT)


_DN = {"nn": (((1,), (0,)), ((), ())), "nt": (((1,), (1,)), ((), ())), "tn": (((0,), (0,)), ((), ()))}


def _dot_raw(a, b, mode):
    return lax.dot_general(a.astype(BF16), b.astype(BF16), _DN[mode], preferred_element_type=F32)


def _make_dot(mode):
    @jax.custom_vjp
    def f(a, b):
        return _dot_raw(a, b, mode)

    def fwd(a, b):
        return _dot_raw(a, b, mode), (a, b)

    def bwd(res, g):
        a, b = res
        if mode == "nn":
            return _dot_raw(g, b, "nt"), _dot_raw(a, g, "tn")
        if mode == "nt":
            return _dot_raw(g, b, "nn"), _dot_raw(g, a, "tn")
        return _dot_raw(b, g, "nt"), _dot_raw(a, g, "nn")

    f.defvjp(fwd, bwd)
    return f


_nn = _make_dot("nn")
_nt = _make_dot("nt")
_tn = _make_dot("tn")


def _xdot(a, e):
    return jnp.dot(a, e, precision=lax.Precision.HIGHEST, preferred_element_type=F32)


def _iota2(shape, dim):
    return lax.broadcasted_iota(jnp.int32, shape, dim)


def _gmean_impl(x):
    n = x.shape[-1]
    same = (_iota2((n, n), 0) < HEAD_DIM) == (_iota2((n, n), 1) < HEAD_DIM)
    p = jnp.where(same, 1.0 / HEAD_DIM, 0.0).astype(BF16)
    hi = x.astype(BF16)
    lo = (x - hi.astype(F32)).astype(BF16)
    dn = _DN["nn"]
    return (lax.dot_general(hi, p, dn, preferred_element_type=F32)
            + lax.dot_general(lo, p, dn, preferred_element_type=F32))


@jax.custom_vjp
def _gmean(x):
    return _gmean_impl(x)


_gmean.defvjp(lambda x: (_gmean_impl(x), None), lambda _, g: (_gmean_impl(g),))


def _sigmoid(x):
    return 1.0 / (1.0 + jnp.exp(-x))


def _silu(x):
    return x * _sigmoid(x)


def _gelu(x):
    return 0.5 * x * (1.0 + lax.erf(x * 0.7071067811865476))


def _softplus(x):
    return jnp.maximum(x, 0.0) + jnp.log(1.0 + jnp.exp(-jnp.abs(x)))


def _rms(x, g):
    return x * lax.rsqrt(jnp.mean(x * x, axis=-1, keepdims=True) + EPS) * g


def _ln64(x, g, b):
    mu = _gmean(x)
    xc = x - mu
    var = _gmean(xc * xc)
    return xc * lax.rsqrt(var + EPS) * g + b


def _lane_lt64(shape):
    return _iota2(shape, 1) < HEAD_DIM


def _pick(n, pref):
    for t in pref:
        if n % t == 0:
            return t
    return n


_UNREAD = pl.BlockSpec(memory_space=pl.ANY)


def _matmul(a, b, *, mode, name, add=None, epilogue=None, extra=None, out_dtypes=(F32,), after=(), b_chips=False,
            out_chips=False):
    sh = b.shape[-1] if b_chips else None
    if mode == "nn":
        (m, k), n = a.shape, (N_CHIPS * sh if b_chips else b.shape[1])
    elif mode == "nt":
        (m, k), n = a.shape, b.shape[-2]
    else:
        (k, m), n = a.shape, b.shape[1]
    osh = n // N_CHIPS if out_chips else None
    tm = _pick(m, (1024, 512, 256, 128))
    tn = _pick(sh if (b_chips and mode == "nn") else (osh or n), (1536, 1024, 512, 256, 128))
    tk = _pick(sh if (b_chips and mode == "nt") else k, (1536, 1024, 512, 256, 128))
    nk = k // tk
    a_spec = {"nn": pl.BlockSpec((tm, tk), lambda i, j, kk: (i, kk)),
              "nt": pl.BlockSpec((tm, tk), lambda i, j, kk: (i, kk)),
              "tn": pl.BlockSpec((tk, tm), lambda i, j, kk: (kk, i))}[mode]
    if b_chips:
        per = sh // (tn if mode == "nn" else tk)
        b_spec = {"nn": pl.BlockSpec((None, tk, tn), lambda i, j, kk: (j // per, kk, j % per)),
                  "nt": pl.BlockSpec((None, tn, tk), lambda i, j, kk: (kk // per, j, kk % per))}[mode]
    else:
        b_spec = {"nn": pl.BlockSpec((tk, tn), lambda i, j, kk: (kk, j)),
                  "nt": pl.BlockSpec((tn, tk), lambda i, j, kk: (j, kk)),
                  "tn": pl.BlockSpec((tk, tn), lambda i, j, kk: (kk, j))}[mode]
    if out_chips:
        o_per = osh // tn
        o_spec = pl.BlockSpec((None, tm, tn), lambda i, j, kk: (j // o_per, i, j % o_per))
        out_shape = [jax.ShapeDtypeStruct((N_CHIPS, m, osh), dt) for dt in out_dtypes]
    else:
        o_spec = pl.BlockSpec((tm, tn), lambda i, j, kk: (i, j))
        out_shape = [jax.ShapeDtypeStruct((m, n), dt) for dt in out_dtypes]
    ins = [a, b]
    in_specs = [a_spec, b_spec]
    if add is not None:
        ins.append(add)
        in_specs.append(o_spec)
    if extra is not None:
        ins.append(extra)
        in_specs.append(o_spec)
    ins += list(after)
    in_specs += [_UNREAD] * len(after)
    n_out = len(out_dtypes)

    def body(*refs):
        a_ref, b_ref = refs[0], refs[1]
        pos = 2
        add_ref = ex_ref = None
        if add is not None:
            add_ref = refs[pos]
            pos += 1
        if extra is not None:
            ex_ref = refs[pos]
            pos += 1
        pos += len(after)
        o_refs = refs[pos:pos + n_out]

        def finish(acc):
            if add_ref is not None:
                acc = acc + add_ref[...].astype(F32)
            outs = (acc,) if epilogue is None else epilogue(acc, None if ex_ref is None else ex_ref[...])
            for o_ref, o in zip(o_refs, outs):
                o_ref[...] = o.astype(o_ref.dtype)

        part = _dot_raw(a_ref[...], b_ref[...], mode)
        if nk == 1:
            finish(part)
            return
        acc_ref = refs[pos + n_out]
        kk = pl.program_id(2)

        @pl.when(kk == 0)
        def _():
            acc_ref[...] = part

        @pl.when(jnp.logical_and(kk > 0, kk < nk - 1))
        def _():
            acc_ref[...] += part

        @pl.when(kk == nk - 1)
        def _():
            finish(acc_ref[...] + part)

    res = pl.pallas_call(
        body, name=name, grid=(m // tm, n // tn, nk),
        in_specs=in_specs, out_specs=[o_spec] * n_out, out_shape=out_shape,
        scratch_shapes=[pltpu.VMEM((tm, tn), F32)] if nk > 1 else [],
        compiler_params=_cparams("parallel", "parallel", "arbitrary"),
    )(*ins)
    return res[0] if n_out == 1 else res


def _relu2_epilogue(acc, _):
    r = jnp.maximum(acc, 0.0)
    return acc, r * r


def _relu2_bwd_epilogue(acc, u):
    return (acc * (2.0 * jnp.maximum(u, 0.0)),)


def _row_tile(t):
    return _pick(t, (512, 256, 128))


def _rms_fwd(x, g, name, after=()):
    t, d = x.shape
    tm = _row_tile(t)

    def body(x_ref, g_ref, *rest):
        o_ref = rest[-1]
        o_ref[...] = _rms(x_ref[...], g_ref[...]).astype(BF16)

    return pl.pallas_call(
        body, name=name, grid=(t // tm,),
        in_specs=[pl.BlockSpec((tm, d), lambda i: (i, 0)), pl.BlockSpec((1, d), lambda i: (0, 0))] + [_UNREAD] * len(after),
        out_specs=pl.BlockSpec((tm, d), lambda i: (i, 0)),
        out_shape=jax.ShapeDtypeStruct((t, d), BF16),
        compiler_params=_cparams("parallel"),
    )(x, g.reshape(1, d), *after)


def _rms_bwd(x, g, dh, dres, name):
    t, d = x.shape
    tm = _row_tile(t)

    def body(x_ref, g_ref, dh_ref, dres_ref, dx_ref, dg_ref):
        @pl.when(pl.program_id(0) == 0)
        def _():
            dg_ref[...] = jnp.zeros_like(dg_ref)

        _, vjp = jax.vjp(_rms, x_ref[...], g_ref[...])
        dx, dg = vjp(dh_ref[...].astype(F32))
        dx_ref[...] = dx + dres_ref[...]
        dg_ref[...] += dg

    row = pl.BlockSpec((tm, d), lambda i: (i, 0))
    vec = pl.BlockSpec((1, d), lambda i: (0, 0))
    dx, dg = pl.pallas_call(
        body, name=name, grid=(t // tm,),
        in_specs=[row, vec, row, row], out_specs=[row, vec],
        out_shape=[jax.ShapeDtypeStruct((t, d), F32), jax.ShapeDtypeStruct((1, d), F32)],
        compiler_params=_cparams("arbitrary"),
    )(x, g.reshape(1, d), dh, dres)
    return dx, dg.reshape(d)


def _loss_head(x, g, target):
    t, d = x.shape
    tm = _row_tile(t)

    def loss_fn(xv, gv, tv):
        err = _rms(xv, gv) - tv
        return 0.5 * jnp.sum(jnp.mean(err * err, axis=-1, keepdims=True))

    def body(x_ref, g_ref, t_ref, loss_ref, dx_ref, dg_ref):
        @pl.when(pl.program_id(0) == 0)
        def _():
            dg_ref[...] = jnp.zeros_like(dg_ref)
            loss_ref[...] = jnp.zeros_like(loss_ref)

        tv = t_ref[...]
        val, vjp = jax.vjp(lambda xv, gv: loss_fn(xv, gv, tv), x_ref[...], g_ref[...])
        dx, dg = vjp(jnp.ones((), F32))
        dx_ref[...] = dx
        dg_ref[...] += dg
        loss_ref[...] += jnp.full(loss_ref.shape, val, F32)

    row = pl.BlockSpec((tm, d), lambda i: (i, 0))
    vec = pl.BlockSpec((1, d), lambda i: (0, 0))
    loss, dx, dg = pl.pallas_call(
        body, name="loss_head", grid=(t // tm,),
        in_specs=[row, vec, row], out_specs=[pl.BlockSpec((1, LANES), lambda i: (0, 0)), row, vec],
        out_shape=[jax.ShapeDtypeStruct((1, LANES), F32), jax.ShapeDtypeStruct((t, d), F32),
                   jax.ShapeDtypeStruct((1, d), F32)],
        compiler_params=_cparams("arbitrary"),
    )(x, g.reshape(1, d), target)
    return loss[0, 0], dx, dg.reshape(d)


def _pre_glu(val, gate):
    return val * _sigmoid(gate)


def _pre_id(x):
    return x


def _post_lnsilu(c, g, b):
    return _silu(_ln64(c, g, b))


def _post_silu(c):
    return _silu(c)


def _conv_cfg(kind):
    if kind == "a":
        return dict(k=CONV_A_K, pre=_pre_glu, post=_post_lnsilu, n_in=2, n_par=2, nblk=A_WIDTH // LANES,
                    cols=(COL_AVAL, COL_AGATE))
    return dict(k=CONV_C_K, pre=_pre_id, post=_post_silu, n_in=1, n_par=0, nblk=D_CONV_C // LANES,
                cols=(COL_XBC,))


def _conv_fwd(kind, proj, w, bias, params, seq, name, out_dtype=F32, keep_conv=False):
    cfg = _conv_cfg(kind)
    kt, pre, post, n_in = cfg["k"], cfg["pre"], cfg["post"], cfg["n_in"]
    t = proj.shape[0]
    nseq = t // seq
    c = cfg["nblk"] * LANES
    rt = min(256, seq)
    nrt = seq // rt
    off0 = CONV_PAD - (kt - 1)

    def body(*refs):
        in_refs = refs[:n_in]
        w_ref, b_ref = refs[n_in], refs[n_in + 1]
        par_refs = refs[n_in + 2:n_in + 2 + cfg["n_par"]]
        out_refs = refs[n_in + 2 + cfg["n_par"]:-1]
        hpad = refs[-1]
        hpad[pl.ds(0, CONV_PAD), :] = jnp.zeros((CONV_PAD, LANES), F32)
        for r in range(nrt):
            hpad[pl.ds(CONV_PAD + r * rt, rt), :] = pre(*[x[pl.ds(r * rt, rt), :] for x in in_refs])
        pars = [p[...] for p in par_refs]
        for r in range(nrt):
            acc = jnp.broadcast_to(b_ref[...], (rt, LANES))
            for k in range(kt):
                acc = acc + w_ref[pl.ds(k, 1), :] * hpad[pl.ds(off0 + k + r * rt, rt), :]
            out_refs[0][pl.ds(r * rt, rt), :] = post(acc, *pars).astype(out_dtype)
            if keep_conv:
                out_refs[1][pl.ds(r * rt, rt), :] = acc

    in_specs = [pl.BlockSpec((seq, LANES), functools.partial(lambda s, j, col: (s, col + j), col=col))
                for col in cfg["cols"]]
    vec = pl.BlockSpec((1, LANES), lambda s, j: (0, j))
    in_specs += [pl.BlockSpec((CONV_PAD, LANES), lambda s, j: (0, j)), vec] + [vec] * cfg["n_par"]
    blk = pl.BlockSpec((seq, LANES), lambda s, j: (s, j))
    res = pl.pallas_call(
        body, name=name, grid=(nseq, cfg["nblk"]),
        in_specs=in_specs, out_specs=[blk, blk] if keep_conv else [blk],
        out_shape=[jax.ShapeDtypeStruct((t, c), out_dtype)] + ([jax.ShapeDtypeStruct((t, c), F32)] if keep_conv else []),
        scratch_shapes=[pltpu.VMEM((seq + CONV_PAD, LANES), F32)],
        compiler_params=_cparams("parallel", "parallel"),
    )(*([proj] * n_in), w, bias, *params)
    return tuple(res) if keep_conv else res[0]


def _conv_bwd(kind, proj, w, bias, params, dy, seq, name, dy_col=0, conv_out=None):
    kept = conv_out is not None
    cfg = _conv_cfg(kind)
    kt, pre, post, n_in, n_par = cfg["k"], cfg["pre"], cfg["post"], cfg["n_in"], cfg["n_par"]
    t = proj.shape[0]
    nseq = t // seq
    c = cfg["nblk"] * LANES
    rt = min(256, seq)
    nrt = seq // rt
    off0 = CONV_PAD - (kt - 1)

    def body(*refs):
        in_refs = refs[:n_in]
        w_ref, b_ref = refs[n_in], refs[n_in + 1]
        par_refs = refs[n_in + 2:n_in + 2 + n_par]
        pos = n_in + 2 + n_par
        dy_ref = refs[pos]
        if kept:
            pos += 1
            conv_ref = refs[pos]
        din_refs = refs[pos + 1:pos + 1 + n_in]
        dw_ref, db_ref = refs[pos + 1 + n_in], refs[pos + 2 + n_in]
        dpar_refs = refs[pos + 3 + n_in:pos + 3 + n_in + n_par]
        hpad, dcpad = refs[pos + 3 + n_in + n_par:]

        @pl.when(pl.program_id(1) == 0)
        def _():
            dw_ref[...] = jnp.zeros_like(dw_ref)
            db_ref[...] = jnp.zeros_like(db_ref)
            for r in dpar_refs:
                r[...] = jnp.zeros_like(r)

        hpad[pl.ds(0, CONV_PAD), :] = jnp.zeros((CONV_PAD, LANES), F32)
        dcpad[pl.ds(seq, CONV_PAD), :] = jnp.zeros((CONV_PAD, LANES), F32)
        for r in range(nrt):
            hpad[pl.ds(CONV_PAD + r * rt, rt), :] = pre(*[x[pl.ds(r * rt, rt), :] for x in in_refs])
        pars = [p[...] for p in par_refs]
        for r in range(nrt):
            if kept:
                acc = conv_ref[pl.ds(r * rt, rt), :]
            else:
                acc = jnp.broadcast_to(b_ref[...], (rt, LANES))
                for k in range(kt):
                    acc = acc + w_ref[pl.ds(k, 1), :] * hpad[pl.ds(off0 + k + r * rt, rt), :]
            _, vjp = jax.vjp(post, acc, *pars)
            grads = vjp(dy_ref[pl.ds(r * rt, rt), :])
            dcpad[pl.ds(r * rt, rt), :] = grads[0]
            db_ref[...] += jnp.sum(grads[0], axis=0, keepdims=True)
            for ref, gpar in zip(dpar_refs, grads[1:]):
                ref[...] += gpar
        for r in range(nrt):
            dh = jnp.zeros((rt, LANES), F32)
            for k in range(kt):
                dh = dh + w_ref[pl.ds(k, 1), :] * dcpad[pl.ds(r * rt + kt - 1 - k, rt), :]
            _, vjp = jax.vjp(pre, *[x[pl.ds(r * rt, rt), :] for x in in_refs])
            for ref, gin in zip(din_refs, vjp(dh)):
                ref[pl.ds(r * rt, rt), :] = gin.astype(ref.dtype)
        for k in range(kt):
            s = jnp.zeros((1, LANES), F32)
            for r in range(nrt):
                s = s + jnp.sum(dcpad[pl.ds(r * rt, rt), :] * hpad[pl.ds(off0 + k + r * rt, rt), :],
                                axis=0, keepdims=True)
            dw_ref[pl.ds(k, 1), :] += s

    in_specs = [pl.BlockSpec((seq, LANES), functools.partial(lambda j, s, col: (s, col + j), col=col))
                for col in cfg["cols"]]
    vec = pl.BlockSpec((1, LANES), lambda j, s: (0, j))
    wspec = pl.BlockSpec((CONV_PAD, LANES), lambda j, s: (0, j))
    blk = pl.BlockSpec((seq, LANES), lambda j, s: (s, j))
    in_specs += [wspec, vec] + [vec] * n_par + [pl.BlockSpec((seq, LANES), lambda j, s: (s, dy_col + j))]
    in_specs += [blk] if kept else []
    out_specs = [blk] * n_in + [wspec, vec] + [vec] * n_par
    out_shape = ([jax.ShapeDtypeStruct((t, c), BF16)] * n_in
                 + [jax.ShapeDtypeStruct((CONV_PAD, c), F32), jax.ShapeDtypeStruct((1, c), F32)]
                 + [jax.ShapeDtypeStruct((1, c), F32)] * n_par)
    res = pl.pallas_call(
        body, name=name, grid=(cfg["nblk"], nseq),
        in_specs=in_specs, out_specs=out_specs, out_shape=out_shape,
        scratch_shapes=[pltpu.VMEM((seq + CONV_PAD, LANES), F32), pltpu.VMEM((seq + CONV_PAD, LANES), F32)],
        compiler_params=_cparams("parallel", "arbitrary"),
    )(*([proj] * n_in), w, bias, *params, dy, *([conv_out] if kept else []))
    return res[:n_in], res[n_in], res[n_in + 1], res[n_in + 2:]


def _gmlp_chunk(bu, bv, g, b, w0, w1, b0row, b1row):
    u = _gelu(bu)
    vn = _ln64(_gelu(bv), g, b)
    tri = _iota2((CHUNK, CHUNK), 0) >= _iota2((CHUNK, CHUNK), 1)
    m0 = _nn(jnp.where(tri, w0, 0.0), vn) + jnp.broadcast_to(b0row, (CHUNK, CHUNK)).T
    m1 = _nn(jnp.where(tri, w1, 0.0), vn) + jnp.broadcast_to(b1row, (CHUNK, CHUNK)).T
    return u * jnp.where(_lane_lt64((CHUNK, LANES)), m0, m1)


def _gmlp_specs(tm, order):
    def im(f):
        return lambda *ids: f(*order(*ids))
    return dict(
        bu=pl.BlockSpec((tm, LANES), im(lambda j, r: (r, COL_BU + j))),
        bv=pl.BlockSpec((tm, LANES), im(lambda j, r: (r, COL_BV + j))),
        vec=pl.BlockSpec((1, LANES), im(lambda j, r: (0, j))),
        ws=pl.BlockSpec((2, CHUNK, CHUNK), im(lambda j, r: (j, 0, 0))),
        bs=pl.BlockSpec((None, 2, CHUNK), im(lambda j, r: (j, 0, 0))),
        blk=pl.BlockSpec((tm, LANES), im(lambda j, r: (r, j))),
    )


def _gmlp_fwd(proj, ln_g, ln_b, w_s, b_s, name):
    t = proj.shape[0]
    tm = _row_tile(t)
    nch = tm // CHUNK
    sp = _gmlp_specs(tm, lambda r, j: (j, r))

    def body(bu_ref, bv_ref, g_ref, b_ref, ws_ref, bs_ref, o_ref):
        for ci in range(nch):
            rows = pl.ds(ci * CHUNK, CHUNK)
            o_ref[rows, :] = _gmlp_chunk(bu_ref[rows, :], bv_ref[rows, :], g_ref[...], b_ref[...], ws_ref[0], ws_ref[1],
                                         bs_ref[pl.ds(0, 1), :], bs_ref[pl.ds(1, 1), :]).astype(BF16)

    return pl.pallas_call(
        body, name=name, grid=(t // tm, B_WIDTH // LANES),
        in_specs=[sp["bu"], sp["bv"], sp["vec"], sp["vec"], sp["ws"], sp["bs"]],
        out_specs=sp["blk"], out_shape=jax.ShapeDtypeStruct((t, B_WIDTH), BF16),
        compiler_params=_cparams("parallel", "parallel"),
    )(proj, proj, ln_g, ln_b, w_s, b_s.reshape(B_WIDTH // LANES, 2, CHUNK))


def _gmlp_bwd(proj, ln_g, ln_b, w_s, b_s, dy, name, dy_col=0):
    t = proj.shape[0]
    tm = _row_tile(t)
    nch = tm // CHUNK
    sp = _gmlp_specs(tm, lambda j, r: (j, r))
    dy_spec = pl.BlockSpec((tm, LANES), lambda j, r: (r, dy_col + j))

    def body(bu_ref, bv_ref, g_ref, b_ref, ws_ref, bs_ref, dy_ref, dbu_ref, dbv_ref, dg_ref, db_ref, dws_ref, dbs_ref):
        @pl.when(pl.program_id(1) == 0)
        def _():
            for r in (dg_ref, db_ref, dws_ref, dbs_ref):
                r[...] = jnp.zeros_like(r)

        for ci in range(nch):
            rows = pl.ds(ci * CHUNK, CHUNK)
            _, vjp = jax.vjp(_gmlp_chunk, bu_ref[rows, :], bv_ref[rows, :], g_ref[...], b_ref[...],
                             ws_ref[0], ws_ref[1], bs_ref[pl.ds(0, 1), :], bs_ref[pl.ds(1, 1), :])
            dbu, dbv, dg, db, dw0, dw1, db0, db1 = vjp(dy_ref[rows, :])
            dbu_ref[rows, :] = dbu.astype(BF16)
            dbv_ref[rows, :] = dbv.astype(BF16)
            dg_ref[...] += dg
            db_ref[...] += db
            dws_ref[0] += dw0
            dws_ref[1] += dw1
            dbs_ref[pl.ds(0, 1), :] += db0
            dbs_ref[pl.ds(1, 1), :] += db1

    nh = B_WIDTH // LANES
    res = pl.pallas_call(
        body, name=name, grid=(nh, t // tm),
        in_specs=[sp["bu"], sp["bv"], sp["vec"], sp["vec"], sp["ws"], sp["bs"], dy_spec],
        out_specs=[sp["blk"], sp["blk"], sp["vec"], sp["vec"], sp["ws"], sp["bs"]],
        out_shape=[jax.ShapeDtypeStruct((t, B_WIDTH), BF16), jax.ShapeDtypeStruct((t, B_WIDTH), BF16),
                   jax.ShapeDtypeStruct((1, B_WIDTH), F32), jax.ShapeDtypeStruct((1, B_WIDTH), F32),
                   jax.ShapeDtypeStruct(w_s.shape, F32), jax.ShapeDtypeStruct((nh, 2, CHUNK), F32)],
        compiler_params=_cparams("parallel", "arbitrary"),
    )(proj, proj, ln_g, ln_b, w_s, b_s.reshape(nh, 2, CHUNK), dy)
    dbu, dbv, dg, db, dws, dbs = res
    return dbu, dbv, dg, db, dws, dbs.reshape(b_s.shape)


def _tri_apply(a, lower):
    l = a.shape[0]
    r, c = _iota2((l, l), 0), _iota2((l, l), 1)
    t = jnp.where((r >= c) if lower else (r <= c), 1.0, 0.0).astype(BF16)
    hi = a.astype(BF16)
    r1 = a - hi.astype(F32)
    mid = r1.astype(BF16)
    lo = (r1 - mid.astype(F32)).astype(BF16)
    dn = _DN["nn"]
    return (lax.dot_general(t, hi, dn, preferred_element_type=F32) + lax.dot_general(t, mid, dn, preferred_element_type=F32)
            + lax.dot_general(t, lo, dn, preferred_element_type=F32))


@jax.custom_vjp
def _cumsum_rows(a):
    return _tri_apply(a, True)


_cumsum_rows.defvjp(lambda a: (_tri_apply(a, True), None), lambda _, g: (_tri_apply(g, False),))

SSD_GROUP_HEADS = 8
SSD_GROUP_PAIRS = 4


def _ssd_group(x0, x1, x2, x3, dt_raw, bias, alog, bm, cm, p0, p1, p2, p3):
    xs, prevs = (x0, x1, x2, x3), (p0, p1, p2, p3)
    dt = _softplus(dt_raw + bias)
    a = dt * (-jnp.exp(alog))
    acs = _cumsum_rows(a)
    alast = jnp.sum(a, axis=0, keepdims=True)
    dt_t, acs_t = dt.T, acs.T
    cb = _nt(cm, bm)
    tri = _iota2((CHUNK, CHUNK), 0) >= _iota2((CHUNK, CHUNK), 1)
    lane = _iota2((CHUNK, LANES), 1)
    sub = _iota2((LANES, CHUNK), 0)
    lane1 = _iota2((1, LANES), 1)

    def column(v, i):
        return jnp.broadcast_to(jnp.sum(jnp.where(lane == i, v, 0.0), axis=1, keepdims=True), (CHUNK, LANES))

    def row(vt, i):
        return jnp.broadcast_to(jnp.sum(jnp.where(sub == i, vt, 0.0), axis=0, keepdims=True), (CHUNK, CHUNK))

    heads = []
    for i in range(SSD_GROUP_HEADS):
        col_a = column(acs, i)
        al = jnp.sum(jnp.where(lane1 == i, alast, 0.0), axis=1, keepdims=True)
        m = cb * jnp.exp(jnp.where(tri, col_a - row(acs_t, i), NEG)) * row(dt_t, i)
        heads.append((m, jnp.exp(col_a), column(dt, i) * jnp.exp(al - col_a), jnp.exp(al)))
    lo_lanes = _lane_lt64((CHUNK, LANES))
    lo_rows = _iota2((LANES, SSM_STATE), 0) < HEAD_DIM
    ys, news = [], []
    for j in range(SSD_GROUP_PAIRS):
        (m0, ea0, w0, cd0), (m1, ea1, w1, cd1) = heads[2 * j], heads[2 * j + 1]
        x, prev = xs[j], prevs[j]
        ydiag = jnp.where(lo_lanes, _nn(m0, x), _nn(m1, x))
        yoff = jnp.where(lo_lanes, _nt(cm * ea0, prev), _nt(cm * ea1, prev))
        states = jnp.where(lo_rows, _tn(x, bm * w0), _tn(x, bm * w1))
        ys.append(ydiag + yoff)
        news.append(prev * jnp.where(lo_rows, cd0, cd1) + states)
    return tuple(ys) + tuple(news)


def _ssd2_specs(seq, rev):
    ncs = seq // CHUNK
    gw = SSD_GROUP_PAIRS * LANES
    nblk_x = C_WIDTH // LANES

    def row(s, c):
        return s * ncs + (ncs - 1 - c if rev else c)

    return dict(
        x=pl.BlockSpec((CHUNK, gw), lambda g, s, c: (row(s, c), g)),
        dt=pl.BlockSpec((CHUNK, LANES), lambda g, s, c: (row(s, c), g)),
        vec=pl.BlockSpec((1, LANES), lambda g, s, c: (0, g)),
        bm=pl.BlockSpec((CHUNK, SSM_STATE), lambda g, s, c: (row(s, c), nblk_x + g)),
        cm=pl.BlockSpec((CHUNK, SSM_STATE), lambda g, s, c: (row(s, c), nblk_x + 2 + g)),
        st=pl.BlockSpec((None, SSD_GROUP_PAIRS, LANES, SSM_STATE), lambda g, s, c: (row(s, c), g, 0, 0)),
        ncs=ncs,
    )


def _lane_blocks(ref):
    return [ref[:, pl.ds(j * LANES, LANES)] for j in range(SSD_GROUP_PAIRS)]


def _ssd2_fwd(xbc_act, dt_raw, dt_bias, a_log, seq, name):
    t = xbc_act.shape[0]
    sp = _ssd2_specs(seq, False)

    def body(x_ref, dt_ref, bias_ref, alog_ref, bm_ref, cm_ref, y_ref, prev_ref, state):
        @pl.when(pl.program_id(2) == 0)
        def _():
            state[...] = jnp.zeros_like(state)

        prevs = [state[j] for j in range(SSD_GROUP_PAIRS)]
        for j in range(SSD_GROUP_PAIRS):
            prev_ref[j] = prevs[j]
        res = _ssd_group(*_lane_blocks(x_ref), dt_ref[...], bias_ref[...], alog_ref[...], bm_ref[...], cm_ref[...], *prevs)
        for j in range(SSD_GROUP_PAIRS):
            y_ref[:, pl.ds(j * LANES, LANES)] = res[j]
            state[j] = res[SSD_GROUP_PAIRS + j]

    return pl.pallas_call(
        body, name=name, grid=(2, t // seq, sp["ncs"]),
        in_specs=[sp["x"], sp["dt"], sp["vec"], sp["vec"], sp["bm"], sp["cm"]],
        out_specs=[sp["x"], sp["st"]],
        out_shape=[jax.ShapeDtypeStruct((t, C_WIDTH), F32),
                   jax.ShapeDtypeStruct((t // CHUNK, C_WIDTH // LANES, LANES, SSM_STATE), F32)],
        scratch_shapes=[pltpu.VMEM((SSD_GROUP_PAIRS, LANES, SSM_STATE), F32)],
        compiler_params=_cparams("parallel", "parallel", "arbitrary"),
    )(xbc_act, dt_raw, dt_bias, a_log, xbc_act, xbc_act)


def _ssd2_bwd(xbc_act, dt_raw, dt_bias, a_log, prev_saved, dy, seq, name):
    t = xbc_act.shape[0]
    sp = _ssd2_specs(seq, True)
    npair = SSD_GROUP_PAIRS

    def body(x_ref, dt_ref, bias_ref, alog_ref, bm_ref, cm_ref, prev_ref, dy_ref,
             dx_ref, ddt_ref, dbias_ref, dalog_ref, dbm_ref, dcm_ref, dstate):
        @pl.when(pl.program_id(2) == 0)
        def _():
            dstate[...] = jnp.zeros_like(dstate)

        @pl.when(jnp.logical_and(pl.program_id(1) == 0, pl.program_id(2) == 0))
        def _():
            dbias_ref[...] = jnp.zeros_like(dbias_ref)
            dalog_ref[...] = jnp.zeros_like(dalog_ref)

        _, vjp = jax.vjp(_ssd_group, *_lane_blocks(x_ref), dt_ref[...], bias_ref[...], alog_ref[...], bm_ref[...],
                         cm_ref[...], *[prev_ref[j] for j in range(npair)])
        grads = vjp(tuple(_lane_blocks(dy_ref)) + tuple(dstate[j] for j in range(npair)))
        for j in range(npair):
            dx_ref[:, pl.ds(j * LANES, LANES)] = grads[j]
            dstate[j] = grads[npair + 5 + j]
        ddt_ref[...] = grads[npair].astype(BF16)
        dbias_ref[...] += grads[npair + 1]
        dalog_ref[...] += grads[npair + 2]
        dbm_ref[...] = grads[npair + 3]
        dcm_ref[...] = grads[npair + 4]

    return pl.pallas_call(
        body, name=name, grid=(2, t // seq, sp["ncs"]),
        in_specs=[sp["x"], sp["dt"], sp["vec"], sp["vec"], sp["bm"], sp["cm"], sp["st"], sp["x"]],
        out_specs=[sp["x"], sp["dt"], sp["vec"], sp["vec"], sp["dt"], sp["dt"]],
        out_shape=[jax.ShapeDtypeStruct((t, C_WIDTH), F32), jax.ShapeDtypeStruct((t, 2 * LANES), BF16),
                   jax.ShapeDtypeStruct((1, 2 * LANES), F32), jax.ShapeDtypeStruct((1, 2 * LANES), F32),
                   jax.ShapeDtypeStruct((t, 2 * SSM_STATE), F32), jax.ShapeDtypeStruct((t, 2 * SSM_STATE), F32)],
        scratch_shapes=[pltpu.VMEM((npair, LANES, SSM_STATE), F32)],
        compiler_params=_cparams("parallel", "arbitrary", "arbitrary"),
    )(xbc_act, dt_raw, dt_bias, a_log, xbc_act, xbc_act, prev_saved, dy)


def _ssd2_assemble(dxs_ssd, dxs_skip, dbm, dcm, name):
    t = dxs_ssd.shape[0]
    tm = _row_tile(t)

    def body(a_ref, b_ref, dbm_ref, dcm_ref, o_ref):
        o_ref[:, pl.ds(0, C_WIDTH)] = a_ref[...] + b_ref[...]
        o_ref[:, pl.ds(C_WIDTH, 2 * SSM_STATE)] = dbm_ref[...]
        o_ref[:, pl.ds(C_WIDTH + 2 * SSM_STATE, 2 * SSM_STATE)] = dcm_ref[...]

    wide = pl.BlockSpec((tm, C_WIDTH), lambda i: (i, 0))
    narrow = pl.BlockSpec((tm, 2 * SSM_STATE), lambda i: (i, 0))
    return pl.pallas_call(
        body, name=name, grid=(t // tm,), in_specs=[wide, wide, narrow, narrow],
        out_specs=pl.BlockSpec((tm, D_CONV_C), lambda i: (i, 0)),
        out_shape=jax.ShapeDtypeStruct((t, D_CONV_C), F32),
        compiler_params=_cparams("parallel"),
    )(dxs_ssd, dxs_skip, dbm, dcm)


def _expand_mats():
    head = jnp.arange(LANES)[:, None]
    e64 = (head == (jnp.arange(C_WIDTH)[None, :] // HEAD_DIM)).astype(F32)
    e128 = (head == (jnp.arange(C_HEADS * LANES)[None, :] // LANES)).astype(F32)
    return e64, e128


def _ssd_prep_fn(dt_raw, dt_bias, a_log, e64, e128):
    dt = _softplus(dt_raw + dt_bias)
    a = dt * (-jnp.exp(a_log))
    incl = (_iota2((CHUNK, CHUNK), 0) >= _iota2((CHUNK, CHUNK), 1)).astype(F32)
    acs = _xdot(incl, a)
    alast = _xdot(jnp.ones((CHUNK, CHUNK), F32), a)
    return _xdot(dt, e64), _xdot(acs, e64), _xdot(alast, e64), _xdot(acs, e128)


def _ssd_prep_specs():
    blk = lambda w: pl.BlockSpec((CHUNK, w), lambda i: (i, 0))
    const = lambda r, w: pl.BlockSpec((r, w), lambda i: (0, 0))
    ins = [blk(LANES), const(1, LANES), const(1, LANES), const(LANES, C_WIDTH), const(LANES, C_HEADS * LANES)]
    outs = [blk(C_WIDTH), blk(C_WIDTH), blk(C_WIDTH), blk(C_HEADS * LANES)]
    return ins, outs


def _ssd_prep_fwd(dt_raw, dt_bias, a_log, name):
    t = dt_raw.shape[0]
    e64, e128 = _expand_mats()
    ins, outs = _ssd_prep_specs()

    def body(raw_ref, bias_ref, alog_ref, e64_ref, e128_ref, dt_ref, acs_ref, alast_ref, acs128_ref):
        res = _ssd_prep_fn(raw_ref[...], bias_ref[...], alog_ref[...], e64_ref[...], e128_ref[...])
        for ref, v in zip((dt_ref, acs_ref, alast_ref, acs128_ref), res):
            ref[...] = v

    return pl.pallas_call(
        body, name=name, grid=(t // CHUNK,), in_specs=ins, out_specs=outs,
        out_shape=[jax.ShapeDtypeStruct((t, C_WIDTH), F32)] * 3 + [jax.ShapeDtypeStruct((t, C_HEADS * LANES), F32)],
        compiler_params=_cparams("parallel"),
    )(dt_raw, dt_bias, a_log, e64, e128)


def _ssd_prep_bwd(dt_raw, dt_bias, a_log, d_dt, d_acs, d_alast, d_acs128, name):
    t = dt_raw.shape[0]
    e64, e128 = _expand_mats()
    ins, outs = _ssd_prep_specs()
    vec = pl.BlockSpec((1, LANES), lambda i: (0, 0))

    def body(raw_ref, bias_ref, alog_ref, e64_ref, e128_ref, g0, g1, g2, g3, draw_ref, dbias_ref, dalog_ref):
        @pl.when(pl.program_id(0) == 0)
        def _():
            dbias_ref[...] = jnp.zeros_like(dbias_ref)
            dalog_ref[...] = jnp.zeros_like(dalog_ref)

        e64v, e128v = e64_ref[...], e128_ref[...]
        _, vjp = jax.vjp(lambda r, b, al: _ssd_prep_fn(r, b, al, e64v, e128v),
                         raw_ref[...], bias_ref[...], alog_ref[...])
        draw, dbias, dalog = vjp((g0[...], g1[...], g2[...], g3[...]))
        draw_ref[...] = draw.astype(BF16)
        dbias_ref[...] += dbias
        dalog_ref[...] += dalog

    return pl.pallas_call(
        body, name=name, grid=(t // CHUNK,), in_specs=ins + outs,
        out_specs=[pl.BlockSpec((CHUNK, LANES), lambda i: (i, 0)), vec, vec],
        out_shape=[jax.ShapeDtypeStruct((t, LANES), BF16), jax.ShapeDtypeStruct((1, LANES), F32),
                   jax.ShapeDtypeStruct((1, LANES), F32)],
        compiler_params=_cparams("arbitrary"),
    )(dt_raw, dt_bias, a_log, e64, e128, d_dt, d_acs, d_alast, d_acs128)


def _ssd_chunk(x, dt, acs, alast, col0, col1, bm, cm, prev):
    xdt = x * dt
    cb = _nt(cm, bm)
    tri = _iota2((CHUNK, CHUNK), 0) >= _iota2((CHUNK, CHUNK), 1)
    l0 = jnp.exp(jnp.where(tri, col0 - col0.T, NEG))
    l1 = jnp.exp(jnp.where(tri, col1 - col1.T, NEG))
    ydiag = jnp.where(_lane_lt64((CHUNK, LANES)), _nn(cb * l0, xdt), _nn(cb * l1, xdt))
    states = _tn(xdt * jnp.exp(alast - acs), bm)
    yoff = _nt(cm, prev) * jnp.exp(acs)
    new = prev * jnp.exp(alast).T + states
    return ydiag + yoff, new


def _ssd_specs(seq, rev):
    ncs = seq // CHUNK
    npair = C_WIDTH // LANES

    def row(s, c):
        return s * ncs + (ncs - 1 - c if rev else c)

    return dict(
        x=pl.BlockSpec((CHUNK, LANES), lambda s, j, c: (row(s, c), j)),
        bm=pl.BlockSpec((CHUNK, SSM_STATE), lambda s, j, c: (row(s, c), C_WIDTH // LANES + j // 4)),
        cm=pl.BlockSpec((CHUNK, SSM_STATE), lambda s, j, c: (row(s, c), C_WIDTH // LANES + 2 + j // 4)),
        col=pl.BlockSpec((CHUNK, 2 * LANES), lambda s, j, c: (row(s, c), j)),
        st=pl.BlockSpec((None, None, LANES, SSM_STATE), lambda s, j, c: (row(s, c), j, 0, 0)),
        npair=npair, ncs=ncs,
    )


def _ssd_fwd(xbc_act, dt64, acs64, alast64, acs128, seq, name):
    t = xbc_act.shape[0]
    sp = _ssd_specs(seq, False)

    def body(x_ref, dt_ref, acs_ref, alast_ref, col_ref, bm_ref, cm_ref, y_ref, prev_ref, state):
        @pl.when(pl.program_id(2) == 0)
        def _():
            state[...] = jnp.zeros_like(state)

        prev = state[...]
        prev_ref[...] = prev
        y, new = _ssd_chunk(x_ref[...], dt_ref[...], acs_ref[...], alast_ref[...],
                            col_ref[:, pl.ds(0, LANES)], col_ref[:, pl.ds(LANES, LANES)],
                            bm_ref[...], cm_ref[...], prev)
        y_ref[...] = y
        state[...] = new

    return pl.pallas_call(
        body, name=name, grid=(t // seq, sp["npair"], sp["ncs"]),
        in_specs=[sp["x"], sp["x"], sp["x"], sp["x"], sp["col"], sp["bm"], sp["cm"]],
        out_specs=[sp["x"], sp["st"]],
        out_shape=[jax.ShapeDtypeStruct((t, C_WIDTH), F32),
                   jax.ShapeDtypeStruct((t // CHUNK, sp["npair"], LANES, SSM_STATE), F32)],
        scratch_shapes=[pltpu.VMEM((LANES, SSM_STATE), F32)],
        compiler_params=_cparams("parallel", "parallel", "arbitrary"),
    )(xbc_act, dt64, acs64, alast64, acs128, xbc_act, xbc_act)


def _ssd_bwd(xbc_act, dt64, acs64, alast64, acs128, prev_saved, dy, seq, name):
    t = xbc_act.shape[0]
    sp = _ssd_specs(seq, True)

    def body(x_ref, dt_ref, acs_ref, alast_ref, col_ref, bm_ref, cm_ref, prev_ref, dy_ref,
             dx_ref, ddt_ref, dacs_ref, dalast_ref, dcol_ref, dbc_ref, dstate):
        @pl.when(pl.program_id(2) == 0)
        def _():
            dstate[...] = jnp.zeros_like(dstate)

        _, vjp = jax.vjp(_ssd_chunk, x_ref[...], dt_ref[...], acs_ref[...], alast_ref[...],
                         col_ref[:, pl.ds(0, LANES)], col_ref[:, pl.ds(LANES, LANES)],
                         bm_ref[...], cm_ref[...], prev_ref[...])
        dx, ddt, dacs, dalast, dc0, dc1, dbm, dcm, dprev = vjp((dy_ref[...], dstate[...]))
        dx_ref[...] = dx
        ddt_ref[...] = ddt
        dacs_ref[...] = dacs
        dalast_ref[...] = dalast
        dcol_ref[:, pl.ds(0, LANES)] = dc0
        dcol_ref[:, pl.ds(LANES, LANES)] = dc1
        dbc_ref[:, pl.ds(0, SSM_STATE)] = dbm
        dbc_ref[:, pl.ds(SSM_STATE, SSM_STATE)] = dcm
        dstate[...] = dprev

    wide = jax.ShapeDtypeStruct((t, C_WIDTH), F32)
    return pl.pallas_call(
        body, name=name, grid=(t // seq, sp["npair"], sp["ncs"]),
        in_specs=[sp["x"], sp["x"], sp["x"], sp["x"], sp["col"], sp["bm"], sp["cm"], sp["st"], sp["x"]],
        out_specs=[sp["x"], sp["x"], sp["x"], sp["x"], sp["col"], sp["col"]],
        out_shape=[wide, wide, wide, wide, jax.ShapeDtypeStruct((t, 2 * C_WIDTH), F32),
                   jax.ShapeDtypeStruct((t, 2 * C_WIDTH), F32)],
        scratch_shapes=[pltpu.VMEM((LANES, SSM_STATE), F32)],
        compiler_params=_cparams("parallel", "parallel", "arbitrary"),
    )(xbc_act, dt64, acs64, alast64, acs128, xbc_act, xbc_act, prev_saved, dy)


def _ssd_post_fn(y, xs, z, dskip, g):
    v = (y + dskip * xs) * _silu(z)
    return v * lax.rsqrt(jnp.mean(v * v, axis=-1, keepdims=True) + EPS) * g


def _ssd_post_specs(tm, order):
    gw = C_WIDTH // 2

    def im(f):
        return lambda *ids: f(*order(*ids))
    return dict(
        blk=pl.BlockSpec((tm, gw), im(lambda g, r: (r, g))),
        z=pl.BlockSpec((tm, gw), im(lambda g, r: (r, COL_Z * LANES // gw + g))),
        vec=pl.BlockSpec((1, gw), im(lambda g, r: (0, g))),
    )


def _ssd_post_fwd(y_ssd, xbc_act, proj, dskip64, norm_g, name):
    t = y_ssd.shape[0]
    tm = _row_tile(t)
    sp = _ssd_post_specs(tm, lambda r, g: (g, r))

    def body(y_ref, xs_ref, z_ref, ds_ref, g_ref, o_ref):
        o_ref[...] = _ssd_post_fn(y_ref[...], xs_ref[...], z_ref[...], ds_ref[...], g_ref[...]).astype(BF16)

    return pl.pallas_call(
        body, name=name, grid=(t // tm, 2),
        in_specs=[sp["blk"], sp["blk"], sp["z"], sp["vec"], sp["vec"]], out_specs=sp["blk"],
        out_shape=jax.ShapeDtypeStruct((t, C_WIDTH), BF16),
        compiler_params=_cparams("parallel", "parallel"),
    )(y_ssd, xbc_act, proj, dskip64, norm_g)


def _ssd_post_bwd(y_ssd, xbc_act, proj, dskip64, norm_g, dyc, name, dy_col=0):
    t = y_ssd.shape[0]
    tm = _row_tile(t)
    sp = _ssd_post_specs(tm, lambda g, r: (g, r))
    dy_spec = pl.BlockSpec((tm, C_WIDTH // 2), lambda g, r: (r, dy_col + g))

    def body(y_ref, xs_ref, z_ref, ds_ref, g_ref, dyc_ref, dy_ref, dxs_ref, dz_ref, dds_ref, dg_ref):
        @pl.when(pl.program_id(1) == 0)
        def _():
            dds_ref[...] = jnp.zeros_like(dds_ref)
            dg_ref[...] = jnp.zeros_like(dg_ref)

        _, vjp = jax.vjp(_ssd_post_fn, y_ref[...], xs_ref[...], z_ref[...], ds_ref[...], g_ref[...])
        dy, dxs, dz, dds, dg = vjp(dyc_ref[...])
        dy_ref[...] = dy
        dxs_ref[...] = dxs
        dz_ref[...] = dz.astype(BF16)
        dds_ref[...] += dds
        dg_ref[...] += dg

    wide = jax.ShapeDtypeStruct((t, C_WIDTH), F32)
    vec = jax.ShapeDtypeStruct((1, C_WIDTH), F32)
    return pl.pallas_call(
        body, name=name, grid=(2, t // tm),
        in_specs=[sp["blk"], sp["blk"], sp["z"], sp["vec"], sp["vec"], dy_spec],
        out_specs=[sp["blk"], sp["blk"], sp["blk"], sp["vec"], sp["vec"]],
        out_shape=[wide, wide, jax.ShapeDtypeStruct((t, C_WIDTH), BF16), vec, vec],
        compiler_params=_cparams("parallel", "arbitrary"),
    )(y_ssd, xbc_act, proj, dskip64, norm_g, dyc)


def _ssd_assemble(dxs_ssd, dxs_skip, dbc, name):
    t = dxs_ssd.shape[0]
    tm = _row_tile(t)
    npair = C_WIDTH // LANES

    def body(a_ref, b_ref, dbc_ref, o_ref):
        o_ref[:, pl.ds(0, C_WIDTH)] = a_ref[...] + b_ref[...]
        for grp in range(2):
            for which in range(2):
                acc = jnp.zeros((tm, SSM_STATE), F32)
                for j in range(grp * npair // 2, (grp + 1) * npair // 2):
                    acc = acc + dbc_ref[:, pl.ds((2 * j + which) * SSM_STATE, SSM_STATE)]
                o_ref[:, pl.ds(C_WIDTH + (2 * which + grp) * SSM_STATE, SSM_STATE)] = acc

    return pl.pallas_call(
        body, name=name, grid=(t // tm,),
        in_specs=[pl.BlockSpec((tm, C_WIDTH), lambda i: (i, 0))] * 2 + [pl.BlockSpec((tm, 2 * C_WIDTH), lambda i: (i, 0))],
        out_specs=pl.BlockSpec((tm, D_CONV_C), lambda i: (i, 0)),
        out_shape=jax.ShapeDtypeStruct((t, D_CONV_C), F32),
        compiler_params=_cparams("parallel"),
    )(dxs_ssd, dxs_skip, dbc)


def _pad_taps(w):
    return jnp.pad(w, ((0, CONV_PAD - w.shape[0]), (0, 0)))


def _pad_heads(v):
    return jnp.pad(v, (0, LANES - v.shape[0])).reshape(1, LANES)


def _group_heads(a):
    pad = [(0, 0)] * (a.ndim - 1) + [(0, LANES - SSD_GROUP_HEADS)]
    return jnp.concatenate([jnp.pad(a[..., :SSD_GROUP_HEADS], pad), jnp.pad(a[..., SSD_GROUP_HEADS:], pad)], axis=-1)


def _ungroup_heads(a):
    return jnp.concatenate([a[..., :SSD_GROUP_HEADS], a[..., LANES:LANES + SSD_GROUP_HEADS]], axis=-1)


def _layer_fwd(x, p, seq, li, after=()):
    n = f"l{li}_"
    h1 = _rms_fwd(x, p["norm1_g"], n + "rms1", after=after)
    proj = _matmul(h1, p["w_main"], mode="nn", name=n + "inproj")
    dt_raw = _matmul(h1, p["w_dt"], mode="nn", name=n + "inproj_dt")
    row = lambda v: v.reshape(1, -1)
    ya, conv_a = _conv_fwd("a", proj, _pad_taps(p["conv_a_w"]), row(p["conv_a_b"]), (row(p["ln_a_g"]), row(p["ln_a_b"])),
                           seq, n + "conva", out_dtype=BF16, keep_conv=True)
    yb = _gmlp_fwd(proj, row(p["ln_b_g"]), row(p["ln_b_b"]), p["w_spatial"], p["b_spatial"], n + "gmlp")
    xbc_act = _conv_fwd("c", proj, _pad_taps(p["conv_c_w"]), row(p["conv_c_b"]), (), seq, n + "convc")
    y_ssd, prev = _ssd2_fwd(xbc_act, dt_raw, _group_heads(row(p["dt_bias"])), _group_heads(row(p["a_log"])), seq, n + "ssd")
    dskip64 = jnp.repeat(p["d_skip"], HEAD_DIM).reshape(1, C_WIDTH)
    yc = _ssd_post_fwd(y_ssd, xbc_act, proj, dskip64, row(p["norm_c_g"]), n + "ssdpost")
    ycat = jnp.concatenate([ya, yb, yc], axis=1)
    x1 = _matmul(ycat, p["w_out"], mode="nn", name=n + "outproj", add=x)
    h2 = _rms_fwd(x1, p["norm2_g"], n + "rms2")
    u, act = _matmul(h2, p["w_ff1"], mode="nn", name=n + "ff1", epilogue=_relu2_epilogue, out_dtypes=(F32, BF16),
                     b_chips=True)
    x2 = _matmul(act, p["w_ff2"], mode="nn", name=n + "ff2", add=x1)
    saved = dict(x=x, h1=h1, proj=proj, conv_a=conv_a, dt_raw=dt_raw, xbc_act=xbc_act, prev=prev, y_ssd=y_ssd,
                 dskip64=dskip64, ycat=ycat, x1=x1, h2=h2, u=u, act=act)
    return x2, saved


def _layer_bwd(dx2, p, s, seq, li, after=()):
    n = f"l{li}_b_"
    row = lambda v: v.reshape(1, -1)
    g = {}
    du = _matmul(dx2, p["w_ff2"], mode="nt", name=n + "ff2_dx", epilogue=_relu2_bwd_epilogue, extra=s["u"],
                 out_dtypes=(BF16,), after=after)
    g["w_ff2"] = _matmul(s["act"], dx2, mode="tn", name=n + "ff2_dw")
    g["w_ff1"] = _matmul(s["h2"], du, mode="tn", name=n + "ff1_dw", out_chips=True)
    dh2 = _matmul(du, p["w_ff1"], mode="nt", name=n + "ff1_dx", b_chips=True)
    dx1, g["norm2_g"] = _rms_bwd(s["x1"], p["norm2_g"], dh2, dx2, n + "rms2")
    g["w_out"] = _matmul(s["ycat"], dx1, mode="tn", name=n + "out_dw")
    dycat = _matmul(dx1, p["w_out"], mode="nt", name=n + "out_dx")
    proj = s["proj"]
    (dval, dgate), dwa, dba, (dlag, dlab) = _conv_bwd(
        "a", proj, _pad_taps(p["conv_a_w"]), row(p["conv_a_b"]), (row(p["ln_a_g"]), row(p["ln_a_b"])), dycat, seq,
        n + "conva", dy_col=0, conv_out=s["conv_a"])
    g["conv_a_w"], g["conv_a_b"], g["ln_a_g"], g["ln_a_b"] = dwa[:CONV_A_K], dba[0], dlag[0], dlab[0]
    dbu, dbv, dlbg, dlbb, g["w_spatial"], g["b_spatial"] = _gmlp_bwd(
        proj, row(p["ln_b_g"]), row(p["ln_b_b"]), p["w_spatial"], p["b_spatial"], dycat, n + "gmlp",
        dy_col=A_WIDTH // LANES)
    g["ln_b_g"], g["ln_b_b"] = dlbg[0], dlbb[0]
    dy_ssd, dxs_skip, dz, dds, dncg = _ssd_post_bwd(s["y_ssd"], s["xbc_act"], proj, s["dskip64"], row(p["norm_c_g"]),
                                                    dycat, n + "ssdpost", dy_col=(A_WIDTH + B_WIDTH) * 2 // C_WIDTH)
    g["norm_c_g"] = dncg[0]
    g["d_skip"] = dds.reshape(C_HEADS, HEAD_DIM).sum(axis=1)
    dxs, ddt_raw, ddtb, dalog, dbm, dcm = _ssd2_bwd(
        s["xbc_act"], s["dt_raw"], _group_heads(row(p["dt_bias"])), _group_heads(row(p["a_log"])), s["prev"], dy_ssd, seq,
        n + "ssd")
    g["dt_bias"], g["a_log"] = _ungroup_heads(ddtb)[0], _ungroup_heads(dalog)[0]
    dconv = _ssd2_assemble(dxs, dxs_skip, dbm, dcm, n + "ssdasm")
    (dxbc,), dwc, dbcv, _ = _conv_bwd("c", proj, _pad_taps(p["conv_c_w"]), row(p["conv_c_b"]), (), dconv, seq, n + "convc")
    g["conv_c_w"], g["conv_c_b"] = dwc[:CONV_C_K], dbcv[0]
    dproj = jnp.concatenate([dval, dgate, dbu, dbv, dz, dxbc], axis=1)
    g["w_main"] = _matmul(s["h1"], dproj, mode="tn", name=n + "in_dw")
    g["w_dt"] = _matmul(s["h1"], ddt_raw, mode="tn", name=n + "indt_dw")
    dh1 = _matmul(dproj, p["w_main"], mode="nt", name=n + "in_dx")
    dh1 = _matmul(ddt_raw, p["w_dt"], mode="nt", name=n + "indt_dx", add=dh1)
    dx, g["norm1_g"] = _rms_bwd(s["x"], p["norm1_g"], dh1, dx1, n + "rms1")
    return dx, g


EW_BLOCK_BYTES = 1 << 20


def _ew(fn, ins, out_dtypes, name, leads=None):
    leads = leads or [None] * len(ins)
    rows, c = ins[0].shape[-2:]
    tr = _pick(rows, [t for t in (2048, 1024, 512, 256, 128, 64, 32, 16, 8) if t * c * 4 <= EW_BLOCK_BYTES])
    n_in = len(ins)

    def spec(lead):
        if lead is None:
            return pl.BlockSpec((tr, c), lambda i: (i, 0))
        return pl.BlockSpec((None, tr, c), functools.partial(lambda i, k: (k, i, 0), k=lead))

    def body(*refs):
        outs = fn(*[r[...].astype(F32) for r in refs[:n_in]])
        for o_ref, o in zip(refs[n_in:], outs):
            o_ref[...] = o.astype(o_ref.dtype)

    return pl.pallas_call(
        body, name=name, grid=(rows // tr,),
        in_specs=[spec(l) for l in leads], out_specs=[spec(None)] * len(out_dtypes),
        out_shape=[jax.ShapeDtypeStruct((rows, c), dt) for dt in out_dtypes],
        compiler_params=_cparams("parallel"),
    )(*ins)


def _adam_fn(w, g, m, v):
    m2 = ADAM_B1 * m + (1.0 - ADAM_B1) * g
    v2 = ADAM_B2 * v + (1.0 - ADAM_B2) * (g * g)
    m_hat = m2 / (1.0 - ADAM_B1 ** ADAM_STEP)
    v_hat = v2 / (1.0 - ADAM_B2 ** ADAM_STEP)
    delta = -ADAM_LR * (m_hat / (jnp.sqrt(v_hat) + ADAM_EPS) + ADAM_WD * w)
    return delta, m2, v2


def _adam(w, g, m, v, name):
    shape = w.shape
    two_d = lambda a: a.reshape(-1, shape[-1])
    outs = _ew(_adam_fn, [two_d(w), two_d(g), two_d(m), two_d(v)], (F32, F32, F32), name)
    return [o.reshape(shape) for o in outs]


_ANY = pl.BlockSpec(memory_space=pl.ANY)


def _mesh_pos():
    return lax.axis_index("x"), lax.axis_index("y"), lax.axis_index("c")


def _peer_chips(x, y):
    return [(1 - x, y), (x, 1 - y), (1 - x, 1 - y)]


def _remote(src, dst, send_sems, recv_sems, sem, to):
    return pltpu.make_async_remote_copy(src_ref=src, dst_ref=dst, send_sem=send_sems.at[sem],
                                        recv_sem=recv_sems.at[sem], device_id=to, device_id_type=MESH)


def _half_rows(n_rows, which):
    half = n_rows // 2
    return pl.ds(pl.multiple_of(which * half, 8), half)


def _comm_call(body, ins, out_shapes, n_sems, name):
    scratch = [pltpu.SemaphoreType.DMA((n_sems,)), pltpu.SemaphoreType.DMA((n_sems,))]
    return pl.pallas_call(
        body, name=name, in_specs=[_ANY] * len(ins), out_specs=[_ANY] * len(out_shapes),
        out_shape=out_shapes, scratch_shapes=scratch,
    )(*ins)


def _gather_weights(big, small, name):
    nb, ns = len(big), len(small)
    n = nb + ns

    def body(*refs):
        ins, outs = refs[:n], refs[n:2 * n]
        send_sems, recv_sems = refs[2 * n:]
        x, y, c = _mesh_pos()
        q = 2 * x + y
        me, sib = (x, y, c), (x, y, 1 - c)
        chips = _peer_chips(x, y)
        rem = functools.partial(_remote, send_sems=send_sems, recv_sems=recv_sems)
        first = []
        for i in range(nb):
            mine = _half_rows(big[i].shape[0], c)
            for k, (px, py) in enumerate(chips):
                first.append(rem(ins[i].at[mine], outs[i].at[q, mine], sem=6 * i + k, to=(px, py, c)))
        for j in range(ns):
            for k, (px, py) in enumerate(chips):
                first.append(rem(ins[nb + j], outs[nb + j].at[q], sem=6 * nb + 3 * j + k, to=(px, py, c)))
        for cp in first:
            cp.start()
        passed = []
        for i in range(nb):
            mine = _half_rows(big[i].shape[0], c)
            for k, (px, py) in enumerate(chips):
                landed = outs[i].at[2 * px + py, mine]
                rem(landed, landed, sem=6 * i + k, to=me).wait_recv()
                fwd = rem(landed, landed, sem=6 * i + 3 + k, to=sib)
                fwd.start()
                passed.append(fwd)
        for i in range(nb):
            other = _half_rows(big[i].shape[0], 1 - c)
            for k, (px, py) in enumerate(chips):
                theirs = outs[i].at[2 * px + py, other]
                rem(theirs, theirs, sem=6 * i + 3 + k, to=me).wait_recv()
        for j in range(ns):
            for k, (px, py) in enumerate(chips):
                dst = outs[nb + j].at[2 * px + py]
                rem(dst, dst, sem=6 * nb + 3 * j + k, to=me).wait_recv()
        for cp in first + passed:
            cp.wait_send()

    out_shapes = [jax.ShapeDtypeStruct((N_CHIPS,) + a.shape, a.dtype) for a in list(big) + list(small)]
    return _comm_call(body, list(big) + list(small), out_shapes, 6 * nb + 3 * ns, name)


def _sibling_other_halves(gs, name):
    n = len(gs)

    def other_half(ref, shape, c):
        rows = _half_rows(shape[-2], 1 - c)
        return ref.at[rows] if len(shape) == 2 else ref.at[:, rows]

    def body(*refs):
        ins, outs = refs[:n], refs[n:2 * n]
        send_sems, recv_sems = refs[2 * n:]
        x, y, c = _mesh_pos()
        copies = [_remote(other_half(ins[i], gs[i].shape, c), outs[i], send_sems, recv_sems, i, (x, y, 1 - c))
                  for i in range(n)]
        for cp in copies:
            cp.start()
        for cp in copies:
            cp.wait()

    out_shapes = [jax.ShapeDtypeStruct(g.shape[:-2] + (g.shape[-2] // 2, g.shape[-1]), g.dtype) for g in gs]
    return _comm_call(body, list(gs), out_shapes, n, name)


IN_SHARD = D_IN_PROJ // N_CHIPS


def _chipsum_in(mine, mine_dt, theirs, theirs_dt, name):
    r = mine.shape[0]
    tr = _pick(r, (128, 64, 32, 16, 8))
    last = D_MAIN - (N_CHIPS - 1) * IN_SHARD

    def body(a_ref, adt_ref, b_ref, bdt_ref, o32_ref, o16_ref):
        for p in range(N_CHIPS):
            wid = IN_SHARD if p < N_CHIPS - 1 else last
            s = a_ref[:, pl.ds(IN_SHARD * p, wid)] + b_ref[:, pl.ds(IN_SHARD * p, wid)]
            o32_ref[p, :, pl.ds(0, wid)] = s
            o16_ref[p, :, pl.ds(0, wid)] = s.astype(BF16)
        for grp in range(2):
            src = pl.ds(grp * LANES, SSD_GROUP_HEADS)
            s = adt_ref[:, src] + bdt_ref[:, src]
            dst = pl.ds(last + grp * SSD_GROUP_HEADS, SSD_GROUP_HEADS)
            o32_ref[N_CHIPS - 1, :, dst] = s
            o16_ref[N_CHIPS - 1, :, dst] = s.astype(BF16)

    wide = pl.BlockSpec((tr, D_MAIN), lambda i: (i, 0))
    narrow = pl.BlockSpec((tr, 2 * LANES), lambda i: (i, 0))
    out = pl.BlockSpec((N_CHIPS, tr, IN_SHARD), lambda i: (0, i, 0))
    return pl.pallas_call(
        body, name=name, grid=(r // tr,), in_specs=[wide, narrow, wide, narrow], out_specs=[out, out],
        out_shape=[jax.ShapeDtypeStruct((N_CHIPS, r, IN_SHARD), F32), jax.ShapeDtypeStruct((N_CHIPS, r, IN_SHARD), BF16)],
        compiler_params=_cparams("parallel"),
    )(mine, mine_dt, theirs, theirs_dt)


def _chip_scatter(cs, name):
    n = len(cs)

    def body(*refs):
        ins, outs = refs[:n], refs[n:2 * n]
        send_sems, recv_sems = refs[2 * n:]
        x, y, c = _mesh_pos()
        copies = []
        for i in range(n):
            for k, (px, py) in enumerate(_peer_chips(x, y)):
                copies.append(_remote(ins[i].at[2 * px + py], outs[i].at[k], send_sems, recv_sems, 3 * i + k, (px, py, c)))
        for cp in copies:
            cp.start()
        for cp in copies:
            cp.wait()

    out_shapes = [jax.ShapeDtypeStruct((3,) + a.shape[1:], a.dtype) for a in cs]
    return _comm_call(body, list(cs), out_shapes, 3 * n, name)


_HBM = pl.BlockSpec(memory_space=pltpu.HBM)
_SEM = pl.BlockSpec(memory_space=pltpu.SEMAPHORE)


def _in_hbm(a):
    return pltpu.with_memory_space_constraint(a, pltpu.HBM)


def _split_plan(kind, srcs, lands, x, y, c):
    plan = []
    for src, land in zip(srcs, lands):
        for k, (px, py) in enumerate(_peer_chips(x, y)):
            if kind == "scatter":
                plan.append((src.at[2 * px + py], land.at[k], (px, py, c)))
            else:
                plan.append((src, land.at[2 * x + y], (px, py, c)))
    return plan


def _split_start(kind, srcs, land_shapes, name):
    n = len(srcs)

    def body(*refs):
        ins, lands = refs[:n], refs[n:2 * n]
        send_sems, recv_sems = refs[2 * n], refs[2 * n + 1]
        token = refs[-1]
        x, y, c = _mesh_pos()
        for i, (src, dst, to) in enumerate(_split_plan(kind, ins, lands, x, y, c)):
            pltpu.make_async_remote_copy(src_ref=src, dst_ref=dst, send_sem=send_sems.at[i], recv_sem=recv_sems.at[i],
                                         device_id=to, device_id_type=MESH).start()
        token[...] = jnp.zeros_like(token)

    zones = [lax.empty(s.shape, s.dtype) for s in land_shapes]
    n_sems = 3 * n
    res = pl.pallas_call(
        body, name=name,
        out_shape=(pltpu.SemaphoreType.DMA((n_sems,)), pltpu.SemaphoreType.DMA((n_sems,)),
                   *[pltpu.HBM(a.shape, a.dtype) for a in srcs], *[pltpu.HBM(s.shape, s.dtype) for s in land_shapes],
                   jax.ShapeDtypeStruct((8, LANES), F32)),
        in_specs=[_HBM] * (2 * n), out_specs=(_SEM, _SEM, *[_HBM] * (2 * n), pl.BlockSpec(memory_space=pltpu.VMEM)),
        input_output_aliases={i: 2 + i for i in range(2 * n)},
        compiler_params=pltpu.CompilerParams(has_side_effects=pltpu.SideEffectType.DATAFLOW_SIDE_EFFECTING),
    )(*[_in_hbm(a) for a in srcs], *[_in_hbm(z) for z in zones])
    return dict(send=res[0], recv=res[1], srcs=list(res[2:2 + n]), lands=list(res[2 + n:2 + 2 * n]), token=res[-1], kind=kind)


def _split_wait(started, after, name):
    n = len(started["srcs"])
    kind = started["kind"]

    def body(*refs):
        ins, lands = refs[:n], refs[n:2 * n]
        send_sems, recv_sems = refs[2 * n], refs[2 * n + 1]
        x, y, c = _mesh_pos()
        for i, (src, dst, _) in enumerate(_split_plan(kind, ins, lands, x, y, c)):
            cp = pltpu.make_async_remote_copy(src_ref=src, dst_ref=dst, send_sem=send_sems.at[i], recv_sem=recv_sems.at[i],
                                              device_id=(x, y, c), device_id_type=MESH)
            cp.wait_send()
            cp.wait_recv()

    arrs = started["srcs"] + started["lands"]
    res = pl.pallas_call(
        body, name=name, out_shape=tuple(pltpu.HBM(a.shape, a.dtype) for a in arrs),
        in_specs=[_HBM] * (2 * n) + [_SEM, _SEM, pl.BlockSpec(memory_space=pl.ANY)], out_specs=tuple([_HBM] * (2 * n)),
        input_output_aliases={i: i for i in range(2 * n)},
        compiler_params=pltpu.CompilerParams(has_side_effects=pltpu.SideEffectType.DATAFLOW_SIDE_EFFECTING),
    )(*arrs, started["send"], started["recv"], after)
    return list(res[n:])


def _sibling_share(fs, name):
    n = len(fs)

    def body(*refs):
        ins, outs = refs[:n], refs[n:2 * n]
        send_sems, recv_sems = refs[2 * n:]
        x, y, c = _mesh_pos()
        copies = [_remote(ins[i], outs[i], send_sems, recv_sems, i, (x, y, 1 - c)) for i in range(n)]
        for cp in copies:
            cp.start()
        for cp in copies:
            cp.wait()

    out_shapes = [jax.ShapeDtypeStruct(a.shape, a.dtype) for a in fs]
    return _comm_call(body, list(fs), out_shapes, n, name)


def _allgather8(v, name):
    m = v.shape[0]

    def body(v_ref, out_ref, send_sems, recv_sems):
        x, y, c = _mesh_pos()
        me, sib = (x, y, c), (x, y, 1 - c)
        chips = _peer_chips(x, y)
        rem = functools.partial(_remote, send_sems=send_sems, recv_sems=recv_sems)

        def blk(px, py, pc):
            return out_ref.at[4 * px + 2 * py + pc]

        first = [rem(v_ref, blk(*me), sem=0, to=sib)]
        first += [rem(v_ref, blk(*me), sem=1 + k, to=(px, py, c)) for k, (px, py) in enumerate(chips)]
        for cp in first:
            cp.start()
        passed = []
        for k, (px, py) in enumerate(chips):
            landed = blk(px, py, c)
            rem(landed, landed, sem=1 + k, to=me).wait_recv()
            fwd = rem(landed, landed, sem=4 + k, to=sib)
            fwd.start()
            passed.append(fwd)
        rem(blk(*sib), blk(*sib), sem=0, to=me).wait_recv()
        for k, (px, py) in enumerate(chips):
            theirs = blk(px, py, 1 - c)
            rem(theirs, theirs, sem=4 + k, to=me).wait_recv()
        for cp in first + passed:
            cp.wait_send()

    return _comm_call(body, [v], [jax.ShapeDtypeStruct((8, m, LANES), v.dtype)], 7, name)[0]


_WEIGHTS = ["norm1_g", "w_in", "conv_a_w", "conv_a_b", "ln_a_g", "ln_a_b", "ln_b_g", "ln_b_b", "w_spatial", "b_spatial",
            "conv_c_w", "conv_c_b", "dt_bias", "a_log", "d_skip", "norm_c_g", "w_out", "norm2_g", "w_ff1", "w_ff2", "final_g"]
_BIG = ["w_in", "w_out", "w_ff1", "w_ff2"]
_CONV_SHARDED = ["conv_a_w", "conv_c_w"]
_SMALL = [w for w in _WEIGHTS if w not in _BIG and w != "final_g"]
_PACK_ROWS = 512


def _pack(arrs):
    flat = jnp.concatenate([a.reshape(-1) for a in arrs])
    blk = _PACK_ROWS * LANES
    n = flat.shape[0]
    return jnp.pad(flat, (0, -(-n // blk) * blk - n)).reshape(-1, LANES)


def _unpack(packed, shapes):
    flat = packed.reshape(-1)
    out, off = [], 0
    for s in shapes:
        n = math.prod(s)
        out.append(flat[off:off + n].reshape(s))
        off += n
    return out


def _cols_to_chips(a):
    k = a.shape[0]
    return a.reshape(k, N_CHIPS, -1).transpose(1, 0, 2)


def _chips_to_cols(a):
    return a.transpose(1, 0, 2).reshape(a.shape[1], -1)


def _own_shards(w, li):
    return [w[k][li].astype(BF16) for k in _BIG] + [w[k][li] for k in _CONV_SHARDED]


def _layer_params(w, li, own, gathered, q):
    g_in, g_out, g_ff1, g_ff2, g_ca, g_cc = [lax.dynamic_update_index_in_dim(g, o, q, axis=0)
                                             for g, o in zip(gathered, own)]
    p = {k: w[k][li] for k in _SMALL if k not in _CONV_SHARDED}
    w_in = _chips_to_cols(g_in)
    p["w_main"] = w_in[:, :D_MAIN]
    p["w_dt"] = _group_heads(w_in[:, D_MAIN:])
    p["w_out"] = g_out.reshape(D_MIX, D_MODEL)
    p["w_ff1"] = g_ff1
    p["w_ff2"] = g_ff2.reshape(D_FF, D_MODEL)
    p["conv_a_w"] = _chips_to_cols(g_ca)
    p["conv_c_w"] = _chips_to_cols(g_cc)
    return p


def _chip_sums(g, li, c, q):
    n = f"l{li}_rs_"
    full = [g["w_main"], g["w_dt"], g["w_out"].reshape(N_CHIPS, -1, D_MODEL), g["w_ff1"],
            g["w_ff2"].reshape(N_CHIPS, -1, D_MODEL)]
    from_sib = _sibling_other_halves(full, n + "sib")
    mine = [lax.dynamic_slice_in_dim(a, c * b.shape[-2], b.shape[-2], axis=a.ndim - 2) for a, b in zip(full, from_sib)]
    sums = [_chipsum_in(mine[0], mine[1], from_sib[0], from_sib[1], n + "chipsum0")]
    for i in range(2, len(full)):
        shape = from_sib[i].shape
        s32, s16 = _ew(lambda u, v: (u + v, u + v), [mine[i].reshape(-1, shape[-1]), from_sib[i].reshape(-1, shape[-1])],
                       (F32, BF16), n + f"chipsum{i - 1}")
        sums.append((s32.reshape(shape), s16.reshape(shape)))
    chip_f32 = [lax.dynamic_index_in_dim(s32, q, axis=0, keepdims=False) for s32, _ in sums]
    chip_bf16 = [s16 for _, s16 in sums]
    return chip_f32, chip_bf16


def _finish_reduce(chip_f32, from_chips, li, c):
    n = f"l{li}_rs_"
    halves = [_ew(lambda o, r0, r1, r2_: (((o + r0) + r1) + r2_,), [own, rb, rb, rb], (F32,), n + f"final{i}",
                  leads=[None, 0, 1, 2])[0] for i, (own, rb) in enumerate(zip(chip_f32, from_chips))]
    from_sib = _sibling_share(halves, n + "share")
    return [jnp.where(c == 0, jnp.concatenate([h, s], axis=0), jnp.concatenate([s, h], axis=0))
            for h, s in zip(halves, from_sib)]


def kernel(x, norm1_g, w_in, conv_a_w, conv_a_b, ln_a_g, ln_a_b, ln_b_g, ln_b_b, w_spatial, b_spatial, conv_c_w, conv_c_b, dt_bias, a_log, d_skip, norm_c_g, w_out, norm2_g, w_ff1, w_ff2, final_g, loss_target, m_norm1_g, m_w_in, m_conv_a_w, m_conv_a_b, m_ln_a_g, m_ln_a_b, m_ln_b_g, m_ln_b_b, m_w_spatial, m_b_spatial, m_conv_c_w, m_conv_c_b, m_dt_bias, m_a_log, m_d_skip, m_norm_c_g, m_w_out, m_norm2_g, m_w_ff1, m_w_ff2, m_final_g, v_norm1_g, v_w_in, v_conv_a_w, v_conv_a_b, v_ln_a_g, v_ln_a_b, v_ln_b_g, v_ln_b_b, v_w_spatial, v_b_spatial, v_conv_c_w, v_conv_c_b, v_dt_bias, v_a_log, v_d_skip, v_norm_c_g, v_w_out, v_norm2_g, v_w_ff1, v_w_ff2, v_final_g):
    given = dict(locals())
    w = {k: given[k] for k in _WEIGHTS}
    m = {k: given["m_" + k] for k in _WEIGHTS}
    v = {k: given["v_" + k] for k in _WEIGHTS}
    depth = w_in.shape[0]
    nseq, seq, d = x.shape
    xi, yi, ci = _mesh_pos()
    q = 2 * xi + yi

    own = [_own_shards(w, li) for li in range(depth)]
    nb = len(_BIG)
    gathered = _gather_weights(own[0][:nb], own[0][nb:], "l0_gather")
    h = x.reshape(nseq * seq, d)
    layer_params, saved = [], []
    for li in range(depth):
        nxt = None
        if li + 1 < depth:
            srcs, _ = lax.optimization_barrier((own[li + 1], gathered))
            zones = [jax.ShapeDtypeStruct((N_CHIPS,) + a.shape, a.dtype) for a in srcs]
            nxt = _split_start("gather", srcs, zones, f"l{li + 1}_gather_start")
        layer_params.append(_layer_params(w, li, own[li], gathered, q))
        h, s = _layer_fwd(h, layer_params[li], seq, li, after=() if nxt is None else (nxt["token"],))
        saved.append(s)
        if nxt is not None:
            gathered = _split_wait(nxt, h, f"l{li + 1}_gather_wait")
    loss, dx, d_final = _loss_head(h, final_g, loss_target.reshape(nseq * seq, d))

    grads = [None] * depth
    big_grads = [None] * depth
    pending = None
    for li in reversed(range(depth)):
        dx, grads[li] = _layer_bwd(dx, layer_params[li], saved[li], seq, li,
                                   after=() if pending is None else (pending[1]["token"],))
        if pending is not None:
            lj, scatter, chip_f32 = pending
            big_grads[lj] = _finish_reduce(chip_f32, _split_wait(scatter, dx, f"l{lj}_rs_scatter_wait"), lj, ci)
        chip_f32, chip_bf16 = _chip_sums(grads[li], li, ci, q)
        if li > 0:
            lands = [jax.ShapeDtypeStruct((3,) + a.shape[1:], a.dtype) for a in chip_bf16]
            pending = (li, _split_start("scatter", chip_bf16, lands, f"l{li}_rs_scatter_start"), chip_f32)
        else:
            big_grads[li] = _finish_reduce(chip_f32, _chip_scatter(chip_bf16, "l0_rs_scatter"), li, ci)
    grad_out, delta_out, m_out, v_out = {}, {}, {}, {}
    for i, k in enumerate(_BIG):
        grad_out[k] = jnp.stack([big_grads[li][i] for li in range(depth)])
        delta_out[k], m_out[k], v_out[k] = _adam(w[k], grad_out[k], m[k], v[k], "adam_" + k)

    small_shapes = [grads[0][k].shape for k in _SMALL]
    parts = [grads[li][k] for li in range(depth) for k in _SMALL] + [d_final, loss.reshape(1)]
    packed_parts = _pack(parts)
    gathered = lax.dynamic_update_index_in_dim(_allgather8(packed_parts, "small_allgather"), packed_parts,
                                               2 * q + ci, axis=0)

    def sum8(*blocks):
        acc = blocks[0]
        for b in blocks[1:]:
            acc = acc + b
        return (acc,)

    total = _ew(sum8, [gathered] * 8, (F32,), "small_sum", leads=list(range(8)))[0]
    summed = _unpack(total, small_shapes * depth + [d_final.shape, (1,)])
    loss_total = summed[-1][0]
    small_grads = {k: jnp.stack([summed[li * len(_SMALL) + i] for li in range(depth)]) for i, k in enumerate(_SMALL)}
    small_grads["final_g"] = summed[-2]
    for k in _CONV_SHARDED:
        n_shard = w[k].shape[-1]
        small_grads[k] = lax.dynamic_slice_in_dim(small_grads[k], q * n_shard, n_shard, axis=2)
    names = _SMALL + ["final_g"]
    shapes = [w[k].shape for k in names]
    packed = [_pack([src[k] for k in names]) for src in (w, small_grads, m, v)]
    outs = _ew(_adam_fn, packed, (F32, F32, F32), "adam_small")
    for dst, o in zip((delta_out, m_out, v_out), outs):
        for k, a in zip(names, _unpack(o, shapes)):
            dst[k] = a
    for k in names:
        grad_out[k] = small_grads[k]

    return (loss_total, dx.reshape(nseq, seq, d), *[grad_out[k] for k in _WEIGHTS], *[delta_out[k] for k in _WEIGHTS],
            *[m_out[k] for k in _WEIGHTS], *[v_out[k] for k in _WEIGHTS])
```

```python
import functools
import math

import jax
import jax.numpy as jnp
from jax import lax
from jax.experimental import pallas as pl
from jax.experimental.pallas import tpu as pltpu

F32 = jnp.float32
BF16 = jnp.bfloat16
MESH = pl.DeviceIdType.MESH

D_MODEL = 1024
DEPTH = 4
HEAD_DIM = 64
A_WIDTH = 512
B_WIDTH = 512
C_WIDTH = 1024
C_HEADS = 16
CONV_A_K = 31
CONV_C_K = 4
CHUNK = 128
SSM_STATE = 128
D_CONV_C = 1536
D_MAIN = 4608
D_IN_PROJ = 4624
D_MIX = 2048
D_FF = 4096
EPS = 1e-5
NEG = -1e30
LANES = 128
CONV_PAD = 32
N_CHIPS = 4

ADAM_LR = 0.001
ADAM_B1 = 0.9
ADAM_B2 = 0.999
ADAM_EPS = 1e-08
ADAM_WD = 0.01
ADAM_STEP = 10

VMEM_LIMIT = 56 * 1024 * 1024

COL_AVAL, COL_AGATE, COL_BU, COL_BV, COL_Z, COL_XBC = 0, 4, 8, 12, 16, 24


def _cparams(*sem):
    return pltpu.CompilerParams(dimension_semantics=sem, vmem_limit_bytes=VMEM_LIMIT)


_DN = {"nn": (((1,), (0,)), ((), ())), "nt": (((1,), (1,)), ((), ())), "tn": (((0,), (0,)), ((), ()))}


def _dot_raw(a, b, mode):
    return lax.dot_general(a.astype(BF16), b.astype(BF16), _DN[mode], preferred_element_type=F32)


def _make_dot(mode):
    @jax.custom_vjp
    def f(a, b):
        return _dot_raw(a, b, mode)

    def fwd(a, b):
        return _dot_raw(a, b, mode), (a, b)

    def bwd(res, g):
        a, b = res
        if mode == "nn":
            return _dot_raw(g, b, "nt"), _dot_raw(a, g, "tn")
        if mode == "nt":
            return _dot_raw(g, b, "nn"), _dot_raw(g, a, "tn")
        return _dot_raw(b, g, "nt"), _dot_raw(a, g, "nn")

    f.defvjp(fwd, bwd)
    return f


_nn = _make_dot("nn")
_nt = _make_dot("nt")
_tn = _make_dot("tn")


def _xdot(a, e):
    return jnp.dot(a, e, precision=lax.Precision.HIGHEST, preferred_element_type=F32)


def _iota2(shape, dim):
    return lax.broadcasted_iota(jnp.int32, shape, dim)


def _gmean_impl(x):
    n = x.shape[-1]
    same = (_iota2((n, n), 0) < HEAD_DIM) == (_iota2((n, n), 1) < HEAD_DIM)
    p = jnp.where(same, 1.0 / HEAD_DIM, 0.0).astype(BF16)
    hi = x.astype(BF16)
    lo = (x - hi.astype(F32)).astype(BF16)
    dn = _DN["nn"]
    return (lax.dot_general(hi, p, dn, preferred_element_type=F32)
            + lax.dot_general(lo, p, dn, preferred_element_type=F32))


@jax.custom_vjp
def _gmean(x):
    return _gmean_impl(x)


_gmean.defvjp(lambda x: (_gmean_impl(x), None), lambda _, g: (_gmean_impl(g),))


def _sigmoid(x):
    return 1.0 / (1.0 + jnp.exp(-x))


def _silu(x):
    return x * _sigmoid(x)


def _gelu(x):
    return 0.5 * x * (1.0 + lax.erf(x * 0.7071067811865476))


def _softplus(x):
    return jnp.maximum(x, 0.0) + jnp.log(1.0 + jnp.exp(-jnp.abs(x)))


def _rms(x, g):
    return x * lax.rsqrt(jnp.mean(x * x, axis=-1, keepdims=True) + EPS) * g


def _ln64(x, g, b):
    mu = _gmean(x)
    xc = x - mu
    var = _gmean(xc * xc)
    return xc * lax.rsqrt(var + EPS) * g + b


def _lane_lt64(shape):
    return _iota2(shape, 1) < HEAD_DIM


def _pick(n, pref):
    for t in pref:
        if n % t == 0:
            return t
    return n


_UNREAD = pl.BlockSpec(memory_space=pl.ANY)


def _matmul(a, b, *, mode, name, add=None, epilogue=None, extra=None, out_dtypes=(F32,), after=(), b_chips=False,
            out_chips=False):
    sh = b.shape[-1] if b_chips else None
    if mode == "nn":
        (m, k), n = a.shape, (N_CHIPS * sh if b_chips else b.shape[1])
    elif mode == "nt":
        (m, k), n = a.shape, b.shape[-2]
    else:
        (k, m), n = a.shape, b.shape[1]
    osh = n // N_CHIPS if out_chips else None
    tm = _pick(m, (1024, 512, 256, 128))
    tn = _pick(sh if (b_chips and mode == "nn") else (osh or n), (1536, 1024, 512, 256, 128))
    tk = _pick(sh if (b_chips and mode == "nt") else k, (1536, 1024, 512, 256, 128))
    nk = k // tk
    a_spec = {"nn": pl.BlockSpec((tm, tk), lambda i, j, kk: (i, kk)),
              "nt": pl.BlockSpec((tm, tk), lambda i, j, kk: (i, kk)),
              "tn": pl.BlockSpec((tk, tm), lambda i, j, kk: (kk, i))}[mode]
    if b_chips:
        per = sh // (tn if mode == "nn" else tk)
        b_spec = {"nn": pl.BlockSpec((None, tk, tn), lambda i, j, kk: (j // per, kk, j % per)),
                  "nt": pl.BlockSpec((None, tn, tk), lambda i, j, kk: (kk // per, j, kk % per))}[mode]
    else:
        b_spec = {"nn": pl.BlockSpec((tk, tn), lambda i, j, kk: (kk, j)),
                  "nt": pl.BlockSpec((tn, tk), lambda i, j, kk: (j, kk)),
                  "tn": pl.BlockSpec((tk, tn), lambda i, j, kk: (kk, j))}[mode]
    if out_chips:
        o_per = osh // tn
        o_spec = pl.BlockSpec((None, tm, tn), lambda i, j, kk: (j // o_per, i, j % o_per))
        out_shape = [jax.ShapeDtypeStruct((N_CHIPS, m, osh), dt) for dt in out_dtypes]
    else:
        o_spec = pl.BlockSpec((tm, tn), lambda i, j, kk: (i, j))
        out_shape = [jax.ShapeDtypeStruct((m, n), dt) for dt in out_dtypes]
    ins = [a, b]
    in_specs = [a_spec, b_spec]
    if add is not None:
        ins.append(add)
        in_specs.append(o_spec)
    if extra is not None:
        ins.append(extra)
        in_specs.append(o_spec)
    ins += list(after)
    in_specs += [_UNREAD] * len(after)
    n_out = len(out_dtypes)

    def body(*refs):
        a_ref, b_ref = refs[0], refs[1]
        pos = 2
        add_ref = ex_ref = None
        if add is not None:
            add_ref = refs[pos]
            pos += 1
        if extra is not None:
            ex_ref = refs[pos]
            pos += 1
        pos += len(after)
        o_refs = refs[pos:pos + n_out]

        def finish(acc):
            if add_ref is not None:
                acc = acc + add_ref[...].astype(F32)
            outs = (acc,) if epilogue is None else epilogue(acc, None if ex_ref is None else ex_ref[...])
            for o_ref, o in zip(o_refs, outs):
                o_ref[...] = o.astype(o_ref.dtype)

        part = _dot_raw(a_ref[...], b_ref[...], mode)
        if nk == 1:
            finish(part)
            return
        acc_ref = refs[pos + n_out]
        kk = pl.program_id(2)

        @pl.when(kk == 0)
        def _():
            acc_ref[...] = part

        @pl.when(jnp.logical_and(kk > 0, kk < nk - 1))
        def _():
            acc_ref[...] += part

        @pl.when(kk == nk - 1)
        def _():
            finish(acc_ref[...] + part)

    res = pl.pallas_call(
        body, name=name, grid=(m // tm, n // tn, nk),
        in_specs=in_specs, out_specs=[o_spec] * n_out, out_shape=out_shape,
        scratch_shapes=[pltpu.VMEM((tm, tn), F32)] if nk > 1 else [],
        compiler_params=_cparams("parallel", "parallel", "arbitrary"),
    )(*ins)
    return res[0] if n_out == 1 else res


def _relu2_epilogue(acc, _):
    r = jnp.maximum(acc, 0.0)
    return acc, r * r


def _relu2_bwd_epilogue(acc, u):
    return (acc * (2.0 * jnp.maximum(u, 0.0)),)


def _row_tile(t):
    return _pick(t, (512, 256, 128))


def _rms_fwd(x, g, name, after=()):
    t, d = x.shape
    tm = _row_tile(t)

    def body(x_ref, g_ref, *rest):
        o_ref = rest[-1]
        o_ref[...] = _rms(x_ref[...], g_ref[...]).astype(BF16)

    return pl.pallas_call(
        body, name=name, grid=(t // tm,),
        in_specs=[pl.BlockSpec((tm, d), lambda i: (i, 0)), pl.BlockSpec((1, d), lambda i: (0, 0))] + [_UNREAD] * len(after),
        out_specs=pl.BlockSpec((tm, d), lambda i: (i, 0)),
        out_shape=jax.ShapeDtypeStruct((t, d), BF16),
        compiler_params=_cparams("parallel"),
    )(x, g.reshape(1, d), *after)


def _rms_bwd(x, g, dh, dres, name):
    t, d = x.shape
    tm = _row_tile(t)

    def body(x_ref, g_ref, dh_ref, dres_ref, dx_ref, dg_ref):
        @pl.when(pl.program_id(0) == 0)
        def _():
            dg_ref[...] = jnp.zeros_like(dg_ref)

        _, vjp = jax.vjp(_rms, x_ref[...], g_ref[...])
        dx, dg = vjp(dh_ref[...].astype(F32))
        dx_ref[...] = dx + dres_ref[...]
        dg_ref[...] += dg

    row = pl.BlockSpec((tm, d), lambda i: (i, 0))
    vec = pl.BlockSpec((1, d), lambda i: (0, 0))
    dx, dg = pl.pallas_call(
        body, name=name, grid=(t // tm,),
        in_specs=[row, vec, row, row], out_specs=[row, vec],
        out_shape=[jax.ShapeDtypeStruct((t, d), F32), jax.ShapeDtypeStruct((1, d), F32)],
        compiler_params=_cparams("arbitrary"),
    )(x, g.reshape(1, d), dh, dres)
    return dx, dg.reshape(d)


def _loss_head(x, g, target):
    t, d = x.shape
    tm = _row_tile(t)

    def loss_fn(xv, gv, tv):
        err = _rms(xv, gv) - tv
        return 0.5 * jnp.sum(jnp.mean(err * err, axis=-1, keepdims=True))

    def body(x_ref, g_ref, t_ref, loss_ref, dx_ref, dg_ref):
        @pl.when(pl.program_id(0) == 0)
        def _():
            dg_ref[...] = jnp.zeros_like(dg_ref)
            loss_ref[...] = jnp.zeros_like(loss_ref)

        tv = t_ref[...]
        val, vjp = jax.vjp(lambda xv, gv: loss_fn(xv, gv, tv), x_ref[...], g_ref[...])
        dx, dg = vjp(jnp.ones((), F32))
        dx_ref[...] = dx
        dg_ref[...] += dg
        loss_ref[...] += jnp.full(loss_ref.shape, val, F32)

    row = pl.BlockSpec((tm, d), lambda i: (i, 0))
    vec = pl.BlockSpec((1, d), lambda i: (0, 0))
    loss, dx, dg = pl.pallas_call(
        body, name="loss_head", grid=(t // tm,),
        in_specs=[row, vec, row], out_specs=[pl.BlockSpec((1, LANES), lambda i: (0, 0)), row, vec],
        out_shape=[jax.ShapeDtypeStruct((1, LANES), F32), jax.ShapeDtypeStruct((t, d), F32),
                   jax.ShapeDtypeStruct((1, d), F32)],
        compiler_params=_cparams("arbitrary"),
    )(x, g.reshape(1, d), target)
    return loss[0, 0], dx, dg.reshape(d)


def _pre_glu(val, gate):
    return val * _sigmoid(gate)


def _pre_id(x):
    return x


def _post_lnsilu(c, g, b):
    return _silu(_ln64(c, g, b))


def _post_silu(c):
    return _silu(c)


def _conv_cfg(kind):
    if kind == "a":
        return dict(k=CONV_A_K, pre=_pre_glu, post=_post_lnsilu, n_in=2, n_par=2, nblk=A_WIDTH // LANES,
                    cols=(COL_AVAL, COL_AGATE))
    return dict(k=CONV_C_K, pre=_pre_id, post=_post_silu, n_in=1, n_par=0, nblk=D_CONV_C // LANES,
                cols=(COL_XBC,))


def _conv_fwd(kind, proj, w, bias, params, seq, name, out_dtype=F32, keep_conv=False):
    cfg = _conv_cfg(kind)
    kt, pre, post, n_in = cfg["k"], cfg["pre"], cfg["post"], cfg["n_in"]
    t = proj.shape[0]
    nseq = t // seq
    c = cfg["nblk"] * LANES
    rt = min(256, seq)
    nrt = seq // rt
    off0 = CONV_PAD - (kt - 1)

    def body(*refs):
        in_refs = refs[:n_in]
        w_ref, b_ref = refs[n_in], refs[n_in + 1]
        par_refs = refs[n_in + 2:n_in + 2 + cfg["n_par"]]
        out_refs = refs[n_in + 2 + cfg["n_par"]:-1]
        hpad = refs[-1]
        hpad[pl.ds(0, CONV_PAD), :] = jnp.zeros((CONV_PAD, LANES), F32)
        for r in range(nrt):
            hpad[pl.ds(CONV_PAD + r * rt, rt), :] = pre(*[x[pl.ds(r * rt, rt), :] for x in in_refs])
        pars = [p[...] for p in par_refs]
        for r in range(nrt):
            acc = jnp.broadcast_to(b_ref[...], (rt, LANES))
            for k in range(kt):
                acc = acc + w_ref[pl.ds(k, 1), :] * hpad[pl.ds(off0 + k + r * rt, rt), :]
            out_refs[0][pl.ds(r * rt, rt), :] = post(acc, *pars).astype(out_dtype)
            if keep_conv:
                out_refs[1][pl.ds(r * rt, rt), :] = acc

    in_specs = [pl.BlockSpec((seq, LANES), functools.partial(lambda s, j, col: (s, col + j), col=col))
                for col in cfg["cols"]]
    vec = pl.BlockSpec((1, LANES), lambda s, j: (0, j))
    in_specs += [pl.BlockSpec((CONV_PAD, LANES), lambda s, j: (0, j)), vec] + [vec] * cfg["n_par"]
    blk = pl.BlockSpec((seq, LANES), lambda s, j: (s, j))
    res = pl.pallas_call(
        body, name=name, grid=(nseq, cfg["nblk"]),
        in_specs=in_specs, out_specs=[blk, blk] if keep_conv else [blk],
        out_shape=[jax.ShapeDtypeStruct((t, c), out_dtype)] + ([jax.ShapeDtypeStruct((t, c), F32)] if keep_conv else []),
        scratch_shapes=[pltpu.VMEM((seq + CONV_PAD, LANES), F32)],
        compiler_params=_cparams("parallel", "parallel"),
    )(*([proj] * n_in), w, bias, *params)
    return tuple(res) if keep_conv else res[0]


def _conv_bwd(kind, proj, w, bias, params, dy, seq, name, dy_col=0, conv_out=None):
    kept = conv_out is not None
    cfg = _conv_cfg(kind)
    kt, pre, post, n_in, n_par = cfg["k"], cfg["pre"], cfg["post"], cfg["n_in"], cfg["n_par"]
    t = proj.shape[0]
    nseq = t // seq
    c = cfg["nblk"] * LANES
    rt = min(256, seq)
    nrt = seq // rt
    off0 = CONV_PAD - (kt - 1)

    def body(*refs):
        in_refs = refs[:n_in]
        w_ref, b_ref = refs[n_in], refs[n_in + 1]
        par_refs = refs[n_in + 2:n_in + 2 + n_par]
        pos = n_in + 2 + n_par
        dy_ref = refs[pos]
        if kept:
            pos += 1
            conv_ref = refs[pos]
        din_refs = refs[pos + 1:pos + 1 + n_in]
        dw_ref, db_ref = refs[pos + 1 + n_in], refs[pos + 2 + n_in]
        dpar_refs = refs[pos + 3 + n_in:pos + 3 + n_in + n_par]
        hpad, dcpad = refs[pos + 3 + n_in + n_par:]

        @pl.when(pl.program_id(1) == 0)
        def _():
            dw_ref[...] = jnp.zeros_like(dw_ref)
            db_ref[...] = jnp.zeros_like(db_ref)
            for r in dpar_refs:
                r[...] = jnp.zeros_like(r)

        hpad[pl.ds(0, CONV_PAD), :] = jnp.zeros((CONV_PAD, LANES), F32)
        dcpad[pl.ds(seq, CONV_PAD), :] = jnp.zeros((CONV_PAD, LANES), F32)
        for r in range(nrt):
            hpad[pl.ds(CONV_PAD + r * rt, rt), :] = pre(*[x[pl.ds(r * rt, rt), :] for x in in_refs])
        pars = [p[...] for p in par_refs]
        for r in range(nrt):
            if kept:
                acc = conv_ref[pl.ds(r * rt, rt), :]
            else:
                acc = jnp.broadcast_to(b_ref[...], (rt, LANES))
                for k in range(kt):
                    acc = acc + w_ref[pl.ds(k, 1), :] * hpad[pl.ds(off0 + k + r * rt, rt), :]
            _, vjp = jax.vjp(post, acc, *pars)
            grads = vjp(dy_ref[pl.ds(r * rt, rt), :])
            dcpad[pl.ds(r * rt, rt), :] = grads[0]
            db_ref[...] += jnp.sum(grads[0], axis=0, keepdims=True)
            for ref, gpar in zip(dpar_refs, grads[1:]):
                ref[...] += gpar
        for r in range(nrt):
            dh = jnp.zeros((rt, LANES), F32)
            for k in range(kt):
                dh = dh + w_ref[pl.ds(k, 1), :] * dcpad[pl.ds(r * rt + kt - 1 - k, rt), :]
            _, vjp = jax.vjp(pre, *[x[pl.ds(r * rt, rt), :] for x in in_refs])
            for ref, gin in zip(din_refs, vjp(dh)):
                ref[pl.ds(r * rt, rt), :] = gin.astype(ref.dtype)
        for k in range(kt):
            s = jnp.zeros((1, LANES), F32)
            for r in range(nrt):
                s = s + jnp.sum(dcpad[pl.ds(r * rt, rt), :] * hpad[pl.ds(off0 + k + r * rt, rt), :],
                                axis=0, keepdims=True)
            dw_ref[pl.ds(k, 1), :] += s

    in_specs = [pl.BlockSpec((seq, LANES), functools.partial(lambda j, s, col: (s, col + j), col=col))
                for col in cfg["cols"]]
    vec = pl.BlockSpec((1, LANES), lambda j, s: (0, j))
    wspec = pl.BlockSpec((CONV_PAD, LANES), lambda j, s: (0, j))
    blk = pl.BlockSpec((seq, LANES), lambda j, s: (s, j))
    in_specs += [wspec, vec] + [vec] * n_par + [pl.BlockSpec((seq, LANES), lambda j, s: (s, dy_col + j))]
    in_specs += [blk] if kept else []
    out_specs = [blk] * n_in + [wspec, vec] + [vec] * n_par
    out_shape = ([jax.ShapeDtypeStruct((t, c), BF16)] * n_in
                 + [jax.ShapeDtypeStruct((CONV_PAD, c), F32), jax.ShapeDtypeStruct((1, c), F32)]
                 + [jax.ShapeDtypeStruct((1, c), F32)] * n_par)
    res = pl.pallas_call(
        body, name=name, grid=(cfg["nblk"], nseq),
        in_specs=in_specs, out_specs=out_specs, out_shape=out_shape,
        scratch_shapes=[pltpu.VMEM((seq + CONV_PAD, LANES), F32), pltpu.VMEM((seq + CONV_PAD, LANES), F32)],
        compiler_params=_cparams("parallel", "arbitrary"),
    )(*([proj] * n_in), w, bias, *params, dy, *([conv_out] if kept else []))
    return res[:n_in], res[n_in], res[n_in + 1], res[n_in + 2:]


def _gmlp_chunk(bu, bv, g, b, w0, w1, b0row, b1row):
    u = _gelu(bu)
    vn = _ln64(_gelu(bv), g, b)
    tri = _iota2((CHUNK, CHUNK), 0) >= _iota2((CHUNK, CHUNK), 1)
    m0 = _nn(jnp.where(tri, w0, 0.0), vn) + jnp.broadcast_to(b0row, (CHUNK, CHUNK)).T
    m1 = _nn(jnp.where(tri, w1, 0.0), vn) + jnp.broadcast_to(b1row, (CHUNK, CHUNK)).T
    return u * jnp.where(_lane_lt64((CHUNK, LANES)), m0, m1)


def _gmlp_specs(tm, order):
    def im(f):
        return lambda *ids: f(*order(*ids))
    return dict(
        bu=pl.BlockSpec((tm, LANES), im(lambda j, r: (r, COL_BU + j))),
        bv=pl.BlockSpec((tm, LANES), im(lambda j, r: (r, COL_BV + j))),
        vec=pl.BlockSpec((1, LANES), im(lambda j, r: (0, j))),
        ws=pl.BlockSpec((2, CHUNK, CHUNK), im(lambda j, r: (j, 0, 0))),
        bs=pl.BlockSpec((None, 2, CHUNK), im(lambda j, r: (j, 0, 0))),
        blk=pl.BlockSpec((tm, LANES), im(lambda j, r: (r, j))),
    )


def _gmlp_fwd(proj, ln_g, ln_b, w_s, b_s, name):
    t = proj.shape[0]
    tm = _row_tile(t)
    nch = tm // CHUNK
    sp = _gmlp_specs(tm, lambda r, j: (j, r))

    def body(bu_ref, bv_ref, g_ref, b_ref, ws_ref, bs_ref, o_ref):
        for ci in range(nch):
            rows = pl.ds(ci * CHUNK, CHUNK)
            o_ref[rows, :] = _gmlp_chunk(bu_ref[rows, :], bv_ref[rows, :], g_ref[...], b_ref[...], ws_ref[0], ws_ref[1],
                                         bs_ref[pl.ds(0, 1), :], bs_ref[pl.ds(1, 1), :]).astype(BF16)

    return pl.pallas_call(
        body, name=name, grid=(t // tm, B_WIDTH // LANES),
        in_specs=[sp["bu"], sp["bv"], sp["vec"], sp["vec"], sp["ws"], sp["bs"]],
        out_specs=sp["blk"], out_shape=jax.ShapeDtypeStruct((t, B_WIDTH), BF16),
        compiler_params=_cparams("parallel", "parallel"),
    )(proj, proj, ln_g, ln_b, w_s, b_s.reshape(B_WIDTH // LANES, 2, CHUNK))


def _gmlp_bwd(proj, ln_g, ln_b, w_s, b_s, dy, name, dy_col=0):
    t = proj.shape[0]
    tm = _row_tile(t)
    nch = tm // CHUNK
    sp = _gmlp_specs(tm, lambda j, r: (j, r))
    dy_spec = pl.BlockSpec((tm, LANES), lambda j, r: (r, dy_col + j))

    def body(bu_ref, bv_ref, g_ref, b_ref, ws_ref, bs_ref, dy_ref, dbu_ref, dbv_ref, dg_ref, db_ref, dws_ref, dbs_ref):
        @pl.when(pl.program_id(1) == 0)
        def _():
            for r in (dg_ref, db_ref, dws_ref, dbs_ref):
                r[...] = jnp.zeros_like(r)

        for ci in range(nch):
            rows = pl.ds(ci * CHUNK, CHUNK)
            _, vjp = jax.vjp(_gmlp_chunk, bu_ref[rows, :], bv_ref[rows, :], g_ref[...], b_ref[...],
                             ws_ref[0], ws_ref[1], bs_ref[pl.ds(0, 1), :], bs_ref[pl.ds(1, 1), :])
            dbu, dbv, dg, db, dw0, dw1, db0, db1 = vjp(dy_ref[rows, :])
            dbu_ref[rows, :] = dbu.astype(BF16)
            dbv_ref[rows, :] = dbv.astype(BF16)
            dg_ref[...] += dg
            db_ref[...] += db
            dws_ref[0] += dw0
            dws_ref[1] += dw1
            dbs_ref[pl.ds(0, 1), :] += db0
            dbs_ref[pl.ds(1, 1), :] += db1

    nh = B_WIDTH // LANES
    res = pl.pallas_call(
        body, name=name, grid=(nh, t // tm),
        in_specs=[sp["bu"], sp["bv"], sp["vec"], sp["vec"], sp["ws"], sp["bs"], dy_spec],
        out_specs=[sp["blk"], sp["blk"], sp["vec"], sp["vec"], sp["ws"], sp["bs"]],
        out_shape=[jax.ShapeDtypeStruct((t, B_WIDTH), BF16), jax.ShapeDtypeStruct((t, B_WIDTH), BF16),
                   jax.ShapeDtypeStruct((1, B_WIDTH), F32), jax.ShapeDtypeStruct((1, B_WIDTH), F32),
                   jax.ShapeDtypeStruct(w_s.shape, F32), jax.ShapeDtypeStruct((nh, 2, CHUNK), F32)],
        compiler_params=_cparams("parallel", "arbitrary"),
    )(proj, proj, ln_g, ln_b, w_s, b_s.reshape(nh, 2, CHUNK), dy)
    dbu, dbv, dg, db, dws, dbs = res
    return dbu, dbv, dg, db, dws, dbs.reshape(b_s.shape)


def _tri_apply(a, lower):
    l = a.shape[0]
    r, c = _iota2((l, l), 0), _iota2((l, l), 1)
    t = jnp.where((r >= c) if lower else (r <= c), 1.0, 0.0).astype(BF16)
    hi = a.astype(BF16)
    r1 = a - hi.astype(F32)
    mid = r1.astype(BF16)
    lo = (r1 - mid.astype(F32)).astype(BF16)
    dn = _DN["nn"]
    return (lax.dot_general(t, hi, dn, preferred_element_type=F32) + lax.dot_general(t, mid, dn, preferred_element_type=F32)
            + lax.dot_general(t, lo, dn, preferred_element_type=F32))


@jax.custom_vjp
def _cumsum_rows(a):
    return _tri_apply(a, True)


_cumsum_rows.defvjp(lambda a: (_tri_apply(a, True), None), lambda _, g: (_tri_apply(g, False),))

SSD_GROUP_HEADS = 8
SSD_GROUP_PAIRS = 4


def _ssd_group(x0, x1, x2, x3, dt_raw, bias, alog, bm, cm, p0, p1, p2, p3):
    xs, prevs = (x0, x1, x2, x3), (p0, p1, p2, p3)
    dt = _softplus(dt_raw + bias)
    a = dt * (-jnp.exp(alog))
    acs = _cumsum_rows(a)
    alast = jnp.sum(a, axis=0, keepdims=True)
    dt_t, acs_t = dt.T, acs.T
    cb = _nt(cm, bm)
    tri = _iota2((CHUNK, CHUNK), 0) >= _iota2((CHUNK, CHUNK), 1)
    lane = _iota2((CHUNK, LANES), 1)
    sub = _iota2((LANES, CHUNK), 0)
    lane1 = _iota2((1, LANES), 1)

    def column(v, i):
        return jnp.broadcast_to(jnp.sum(jnp.where(lane == i, v, 0.0), axis=1, keepdims=True), (CHUNK, LANES))

    def row(vt, i):
        return jnp.broadcast_to(jnp.sum(jnp.where(sub == i, vt, 0.0), axis=0, keepdims=True), (CHUNK, CHUNK))

    heads = []
    for i in range(SSD_GROUP_HEADS):
        col_a = column(acs, i)
        al = jnp.sum(jnp.where(lane1 == i, alast, 0.0), axis=1, keepdims=True)
        m = cb * jnp.exp(jnp.where(tri, col_a - row(acs_t, i), NEG)) * row(dt_t, i)
        heads.append((m, jnp.exp(col_a), column(dt, i) * jnp.exp(al - col_a), jnp.exp(al)))
    lo_lanes = _lane_lt64((CHUNK, LANES))
    lo_rows = _iota2((LANES, SSM_STATE), 0) < HEAD_DIM
    ys, news = [], []
    for j in range(SSD_GROUP_PAIRS):
        (m0, ea0, w0, cd0), (m1, ea1, w1, cd1) = heads[2 * j], heads[2 * j + 1]
        x, prev = xs[j], prevs[j]
        ydiag = jnp.where(lo_lanes, _nn(m0, x), _nn(m1, x))
        yoff = jnp.where(lo_lanes, _nt(cm * ea0, prev), _nt(cm * ea1, prev))
        states = jnp.where(lo_rows, _tn(x, bm * w0), _tn(x, bm * w1))
        ys.append(ydiag + yoff)
        news.append(prev * jnp.where(lo_rows, cd0, cd1) + states)
    return tuple(ys) + tuple(news)


def _ssd2_specs(seq, rev):
    ncs = seq // CHUNK
    gw = SSD_GROUP_PAIRS * LANES
    nblk_x = C_WIDTH // LANES

    def row(s, c):
        return s * ncs + (ncs - 1 - c if rev else c)

    return dict(
        x=pl.BlockSpec((CHUNK, gw), lambda g, s, c: (row(s, c), g)),
        dt=pl.BlockSpec((CHUNK, LANES), lambda g, s, c: (row(s, c), g)),
        vec=pl.BlockSpec((1, LANES), lambda g, s, c: (0, g)),
        bm=pl.BlockSpec((CHUNK, SSM_STATE), lambda g, s, c: (row(s, c), nblk_x + g)),
        cm=pl.BlockSpec((CHUNK, SSM_STATE), lambda g, s, c: (row(s, c), nblk_x + 2 + g)),
        st=pl.BlockSpec((None, SSD_GROUP_PAIRS, LANES, SSM_STATE), lambda g, s, c: (row(s, c), g, 0, 0)),
        ncs=ncs,
    )


def _lane_blocks(ref):
    return [ref[:, pl.ds(j * LANES, LANES)] for j in range(SSD_GROUP_PAIRS)]


def _ssd2_fwd(xbc_act, dt_raw, dt_bias, a_log, seq, name):
    t = xbc_act.shape[0]
    sp = _ssd2_specs(seq, False)

    def body(x_ref, dt_ref, bias_ref, alog_ref, bm_ref, cm_ref, y_ref, prev_ref, state):
        @pl.when(pl.program_id(2) == 0)
        def _():
            state[...] = jnp.zeros_like(state)

        prevs = [state[j] for j in range(SSD_GROUP_PAIRS)]
        for j in range(SSD_GROUP_PAIRS):
            prev_ref[j] = prevs[j]
        res = _ssd_group(*_lane_blocks(x_ref), dt_ref[...], bias_ref[...], alog_ref[...], bm_ref[...], cm_ref[...], *prevs)
        for j in range(SSD_GROUP_PAIRS):
            y_ref[:, pl.ds(j * LANES, LANES)] = res[j]
            state[j] = res[SSD_GROUP_PAIRS + j]

    return pl.pallas_call(
        body, name=name, grid=(2, t // seq, sp["ncs"]),
        in_specs=[sp["x"], sp["dt"], sp["vec"], sp["vec"], sp["bm"], sp["cm"]],
        out_specs=[sp["x"], sp["st"]],
        out_shape=[jax.ShapeDtypeStruct((t, C_WIDTH), F32),
                   jax.ShapeDtypeStruct((t // CHUNK, C_WIDTH // LANES, LANES, SSM_STATE), F32)],
        scratch_shapes=[pltpu.VMEM((SSD_GROUP_PAIRS, LANES, SSM_STATE), F32)],
        compiler_params=_cparams("parallel", "parallel", "arbitrary"),
    )(xbc_act, dt_raw, dt_bias, a_log, xbc_act, xbc_act)


def _ssd2_bwd(xbc_act, dt_raw, dt_bias, a_log, prev_saved, dy, seq, name):
    t = xbc_act.shape[0]
    sp = _ssd2_specs(seq, True)
    npair = SSD_GROUP_PAIRS

    def body(x_ref, dt_ref, bias_ref, alog_ref, bm_ref, cm_ref, prev_ref, dy_ref,
             dx_ref, ddt_ref, dbias_ref, dalog_ref, dbm_ref, dcm_ref, dstate):
        @pl.when(pl.program_id(2) == 0)
        def _():
            dstate[...] = jnp.zeros_like(dstate)

        @pl.when(jnp.logical_and(pl.program_id(1) == 0, pl.program_id(2) == 0))
        def _():
            dbias_ref[...] = jnp.zeros_like(dbias_ref)
            dalog_ref[...] = jnp.zeros_like(dalog_ref)

        _, vjp = jax.vjp(_ssd_group, *_lane_blocks(x_ref), dt_ref[...], bias_ref[...], alog_ref[...], bm_ref[...],
                         cm_ref[...], *[prev_ref[j] for j in range(npair)])
        grads = vjp(tuple(_lane_blocks(dy_ref)) + tuple(dstate[j] for j in range(npair)))
        for j in range(npair):
            dx_ref[:, pl.ds(j * LANES, LANES)] = grads[j]
            dstate[j] = grads[npair + 5 + j]
        ddt_ref[...] = grads[npair].astype(BF16)
        dbias_ref[...] += grads[npair + 1]
        dalog_ref[...] += grads[npair + 2]
        dbm_ref[...] = grads[npair + 3]
        dcm_ref[...] = grads[npair + 4]

    return pl.pallas_call(
        body, name=name, grid=(2, t // seq, sp["ncs"]),
        in_specs=[sp["x"], sp["dt"], sp["vec"], sp["vec"], sp["bm"], sp["cm"], sp["st"], sp["x"]],
        out_specs=[sp["x"], sp["dt"], sp["vec"], sp["vec"], sp["dt"], sp["dt"]],
        out_shape=[jax.ShapeDtypeStruct((t, C_WIDTH), F32), jax.ShapeDtypeStruct((t, 2 * LANES), BF16),
                   jax.ShapeDtypeStruct((1, 2 * LANES), F32), jax.ShapeDtypeStruct((1, 2 * LANES), F32),
                   jax.ShapeDtypeStruct((t, 2 * SSM_STATE), F32), jax.ShapeDtypeStruct((t, 2 * SSM_STATE), F32)],
        scratch_shapes=[pltpu.VMEM((npair, LANES, SSM_STATE), F32)],
        compiler_params=_cparams("parallel", "arbitrary", "arbitrary"),
    )(xbc_act, dt_raw, dt_bias, a_log, xbc_act, xbc_act, prev_saved, dy)


def _ssd2_assemble(dxs_ssd, dxs_skip, dbm, dcm, name):
    t = dxs_ssd.shape[0]
    tm = _row_tile(t)

    def body(a_ref, b_ref, dbm_ref, dcm_ref, o_ref):
        o_ref[:, pl.ds(0, C_WIDTH)] = a_ref[...] + b_ref[...]
        o_ref[:, pl.ds(C_WIDTH, 2 * SSM_STATE)] = dbm_ref[...]
        o_ref[:, pl.ds(C_WIDTH + 2 * SSM_STATE, 2 * SSM_STATE)] = dcm_ref[...]

    wide = pl.BlockSpec((tm, C_WIDTH), lambda i: (i, 0))
    narrow = pl.BlockSpec((tm, 2 * SSM_STATE), lambda i: (i, 0))
    return pl.pallas_call(
        body, name=name, grid=(t // tm,), in_specs=[wide, wide, narrow, narrow],
        out_specs=pl.BlockSpec((tm, D_CONV_C), lambda i: (i, 0)),
        out_shape=jax.ShapeDtypeStruct((t, D_CONV_C), F32),
        compiler_params=_cparams("parallel"),
    )(dxs_ssd, dxs_skip, dbm, dcm)


def _expand_mats():
    head = jnp.arange(LANES)[:, None]
    e64 = (head == (jnp.arange(C_WIDTH)[None, :] // HEAD_DIM)).astype(F32)
    e128 = (head == (jnp.arange(C_HEADS * LANES)[None, :] // LANES)).astype(F32)
    return e64, e128


def _ssd_prep_fn(dt_raw, dt_bias, a_log, e64, e128):
    dt = _softplus(dt_raw + dt_bias)
    a = dt * (-jnp.exp(a_log))
    incl = (_iota2((CHUNK, CHUNK), 0) >= _iota2((CHUNK, CHUNK), 1)).astype(F32)
    acs = _xdot(incl, a)
    alast = _xdot(jnp.ones((CHUNK, CHUNK), F32), a)
    return _xdot(dt, e64), _xdot(acs, e64), _xdot(alast, e64), _xdot(acs, e128)


def _ssd_prep_specs():
    blk = lambda w: pl.BlockSpec((CHUNK, w), lambda i: (i, 0))
    const = lambda r, w: pl.BlockSpec((r, w), lambda i: (0, 0))
    ins = [blk(LANES), const(1, LANES), const(1, LANES), const(LANES, C_WIDTH), const(LANES, C_HEADS * LANES)]
    outs = [blk(C_WIDTH), blk(C_WIDTH), blk(C_WIDTH), blk(C_HEADS * LANES)]
    return ins, outs


def _ssd_prep_fwd(dt_raw, dt_bias, a_log, name):
    t = dt_raw.shape[0]
    e64, e128 = _expand_mats()
    ins, outs = _ssd_prep_specs()

    def body(raw_ref, bias_ref, alog_ref, e64_ref, e128_ref, dt_ref, acs_ref, alast_ref, acs128_ref):
        res = _ssd_prep_fn(raw_ref[...], bias_ref[...], alog_ref[...], e64_ref[...], e128_ref[...])
        for ref, v in zip((dt_ref, acs_ref, alast_ref, acs128_ref), res):
            ref[...] = v

    return pl.pallas_call(
        body, name=name, grid=(t // CHUNK,), in_specs=ins, out_specs=outs,
        out_shape=[jax.ShapeDtypeStruct((t, C_WIDTH), F32)] * 3 + [jax.ShapeDtypeStruct((t, C_HEADS * LANES), F32)],
        compiler_params=_cparams("parallel"),
    )(dt_raw, dt_bias, a_log, e64, e128)


def _ssd_prep_bwd(dt_raw, dt_bias, a_log, d_dt, d_acs, d_alast, d_acs128, name):
    t = dt_raw.shape[0]
    e64, e128 = _expand_mats()
    ins, outs = _ssd_prep_specs()
    vec = pl.BlockSpec((1, LANES), lambda i: (0, 0))

    def body(raw_ref, bias_ref, alog_ref, e64_ref, e128_ref, g0, g1, g2, g3, draw_ref, dbias_ref, dalog_ref):
        @pl.when(pl.program_id(0) == 0)
        def _():
            dbias_ref[...] = jnp.zeros_like(dbias_ref)
            dalog_ref[...] = jnp.zeros_like(dalog_ref)

        e64v, e128v = e64_ref[...], e128_ref[...]
        _, vjp = jax.vjp(lambda r, b, al: _ssd_prep_fn(r, b, al, e64v, e128v),
                         raw_ref[...], bias_ref[...], alog_ref[...])
        draw, dbias, dalog = vjp((g0[...], g1[...], g2[...], g3[...]))
        draw_ref[...] = draw.astype(BF16)
        dbias_ref[...] += dbias
        dalog_ref[...] += dalog

    return pl.pallas_call(
        body, name=name, grid=(t // CHUNK,), in_specs=ins + outs,
        out_specs=[pl.BlockSpec((CHUNK, LANES), lambda i: (i, 0)), vec, vec],
        out_shape=[jax.ShapeDtypeStruct((t, LANES), BF16), jax.ShapeDtypeStruct((1, LANES), F32),
                   jax.ShapeDtypeStruct((1, LANES), F32)],
        compiler_params=_cparams("arbitrary"),
    )(dt_raw, dt_bias, a_log, e64, e128, d_dt, d_acs, d_alast, d_acs128)


def _ssd_chunk(x, dt, acs, alast, col0, col1, bm, cm, prev):
    xdt = x * dt
    cb = _nt(cm, bm)
    tri = _iota2((CHUNK, CHUNK), 0) >= _iota2((CHUNK, CHUNK), 1)
    l0 = jnp.exp(jnp.where(tri, col0 - col0.T, NEG))
    l1 = jnp.exp(jnp.where(tri, col1 - col1.T, NEG))
    ydiag = jnp.where(_lane_lt64((CHUNK, LANES)), _nn(cb * l0, xdt), _nn(cb * l1, xdt))
    states = _tn(xdt * jnp.exp(alast - acs), bm)
    yoff = _nt(cm, prev) * jnp.exp(acs)
    new = prev * jnp.exp(alast).T + states
    return ydiag + yoff, new


def _ssd_specs(seq, rev):
    ncs = seq // CHUNK
    npair = C_WIDTH // LANES

    def row(s, c):
        return s * ncs + (ncs - 1 - c if rev else c)

    return dict(
        x=pl.BlockSpec((CHUNK, LANES), lambda s, j, c: (row(s, c), j)),
        bm=pl.BlockSpec((CHUNK, SSM_STATE), lambda s, j, c: (row(s, c), C_WIDTH // LANES + j // 4)),
        cm=pl.BlockSpec((CHUNK, SSM_STATE), lambda s, j, c: (row(s, c), C_WIDTH // LANES + 2 + j // 4)),
        col=pl.BlockSpec((CHUNK, 2 * LANES), lambda s, j, c: (row(s, c), j)),
        st=pl.BlockSpec((None, None, LANES, SSM_STATE), lambda s, j, c: (row(s, c), j, 0, 0)),
        npair=npair, ncs=ncs,
    )


def _ssd_fwd(xbc_act, dt64, acs64, alast64, acs128, seq, name):
    t = xbc_act.shape[0]
    sp = _ssd_specs(seq, False)

    def body(x_ref, dt_ref, acs_ref, alast_ref, col_ref, bm_ref, cm_ref, y_ref, prev_ref, state):
        @pl.when(pl.program_id(2) == 0)
        def _():
            state[...] = jnp.zeros_like(state)

        prev = state[...]
        prev_ref[...] = prev
        y, new = _ssd_chunk(x_ref[...], dt_ref[...], acs_ref[...], alast_ref[...],
                            col_ref[:, pl.ds(0, LANES)], col_ref[:, pl.ds(LANES, LANES)],
                            bm_ref[...], cm_ref[...], prev)
        y_ref[...] = y
        state[...] = new

    return pl.pallas_call(
        body, name=name, grid=(t // seq, sp["npair"], sp["ncs"]),
        in_specs=[sp["x"], sp["x"], sp["x"], sp["x"], sp["col"], sp["bm"], sp["cm"]],
        out_specs=[sp["x"], sp["st"]],
        out_shape=[jax.ShapeDtypeStruct((t, C_WIDTH), F32),
                   jax.ShapeDtypeStruct((t // CHUNK, sp["npair"], LANES, SSM_STATE), F32)],
        scratch_shapes=[pltpu.VMEM((LANES, SSM_STATE), F32)],
        compiler_params=_cparams("parallel", "parallel", "arbitrary"),
    )(xbc_act, dt64, acs64, alast64, acs128, xbc_act, xbc_act)


def _ssd_bwd(xbc_act, dt64, acs64, alast64, acs128, prev_saved, dy, seq, name):
    t = xbc_act.shape[0]
    sp = _ssd_specs(seq, True)

    def body(x_ref, dt_ref, acs_ref, alast_ref, col_ref, bm_ref, cm_ref, prev_ref, dy_ref,
             dx_ref, ddt_ref, dacs_ref, dalast_ref, dcol_ref, dbc_ref, dstate):
        @pl.when(pl.program_id(2) == 0)
        def _():
            dstate[...] = jnp.zeros_like(dstate)

        _, vjp = jax.vjp(_ssd_chunk, x_ref[...], dt_ref[...], acs_ref[...], alast_ref[...],
                         col_ref[:, pl.ds(0, LANES)], col_ref[:, pl.ds(LANES, LANES)],
                         bm_ref[...], cm_ref[...], prev_ref[...])
        dx, ddt, dacs, dalast, dc0, dc1, dbm, dcm, dprev = vjp((dy_ref[...], dstate[...]))
        dx_ref[...] = dx
        ddt_ref[...] = ddt
        dacs_ref[...] = dacs
        dalast_ref[...] = dalast
        dcol_ref[:, pl.ds(0, LANES)] = dc0
        dcol_ref[:, pl.ds(LANES, LANES)] = dc1
        dbc_ref[:, pl.ds(0, SSM_STATE)] = dbm
        dbc_ref[:, pl.ds(SSM_STATE, SSM_STATE)] = dcm
        dstate[...] = dprev

    wide = jax.ShapeDtypeStruct((t, C_WIDTH), F32)
    return pl.pallas_call(
        body, name=name, grid=(t // seq, sp["npair"], sp["ncs"]),
        in_specs=[sp["x"], sp["x"], sp["x"], sp["x"], sp["col"], sp["bm"], sp["cm"], sp["st"], sp["x"]],
        out_specs=[sp["x"], sp["x"], sp["x"], sp["x"], sp["col"], sp["col"]],
        out_shape=[wide, wide, wide, wide, jax.ShapeDtypeStruct((t, 2 * C_WIDTH), F32),
                   jax.ShapeDtypeStruct((t, 2 * C_WIDTH), F32)],
        scratch_shapes=[pltpu.VMEM((LANES, SSM_STATE), F32)],
        compiler_params=_cparams("parallel", "parallel", "arbitrary"),
    )(xbc_act, dt64, acs64, alast64, acs128, xbc_act, xbc_act, prev_saved, dy)


def _ssd_post_fn(y, xs, z, dskip, g):
    v = (y + dskip * xs) * _silu(z)
    return v * lax.rsqrt(jnp.mean(v * v, axis=-1, keepdims=True) + EPS) * g


def _ssd_post_specs(tm, order):
    gw = C_WIDTH // 2

    def im(f):
        return lambda *ids: f(*order(*ids))
    return dict(
        blk=pl.BlockSpec((tm, gw), im(lambda g, r: (r, g))),
        z=pl.BlockSpec((tm, gw), im(lambda g, r: (r, COL_Z * LANES // gw + g))),
        vec=pl.BlockSpec((1, gw), im(lambda g, r: (0, g))),
    )


def _ssd_post_fwd(y_ssd, xbc_act, proj, dskip64, norm_g, name):
    t = y_ssd.shape[0]
    tm = _row_tile(t)
    sp = _ssd_post_specs(tm, lambda r, g: (g, r))

    def body(y_ref, xs_ref, z_ref, ds_ref, g_ref, o_ref):
        o_ref[...] = _ssd_post_fn(y_ref[...], xs_ref[...], z_ref[...], ds_ref[...], g_ref[...]).astype(BF16)

    return pl.pallas_call(
        body, name=name, grid=(t // tm, 2),
        in_specs=[sp["blk"], sp["blk"], sp["z"], sp["vec"], sp["vec"]], out_specs=sp["blk"],
        out_shape=jax.ShapeDtypeStruct((t, C_WIDTH), BF16),
        compiler_params=_cparams("parallel", "parallel"),
    )(y_ssd, xbc_act, proj, dskip64, norm_g)


def _ssd_post_bwd(y_ssd, xbc_act, proj, dskip64, norm_g, dyc, name, dy_col=0):
    t = y_ssd.shape[0]
    tm = _row_tile(t)
    sp = _ssd_post_specs(tm, lambda g, r: (g, r))
    dy_spec = pl.BlockSpec((tm, C_WIDTH // 2), lambda g, r: (r, dy_col + g))

    def body(y_ref, xs_ref, z_ref, ds_ref, g_ref, dyc_ref, dy_ref, dxs_ref, dz_ref, dds_ref, dg_ref):
        @pl.when(pl.program_id(1) == 0)
        def _():
            dds_ref[...] = jnp.zeros_like(dds_ref)
            dg_ref[...] = jnp.zeros_like(dg_ref)

        _, vjp = jax.vjp(_ssd_post_fn, y_ref[...], xs_ref[...], z_ref[...], ds_ref[...], g_ref[...])
        dy, dxs, dz, dds, dg = vjp(dyc_ref[...])
        dy_ref[...] = dy
        dxs_ref[...] = dxs
        dz_ref[...] = dz.astype(BF16)
        dds_ref[...] += dds
        dg_ref[...] += dg

    wide = jax.ShapeDtypeStruct((t, C_WIDTH), F32)
    vec = jax.ShapeDtypeStruct((1, C_WIDTH), F32)
    return pl.pallas_call(
        body, name=name, grid=(2, t // tm),
        in_specs=[sp["blk"], sp["blk"], sp["z"], sp["vec"], sp["vec"], dy_spec],
        out_specs=[sp["blk"], sp["blk"], sp["blk"], sp["vec"], sp["vec"]],
        out_shape=[wide, wide, jax.ShapeDtypeStruct((t, C_WIDTH), BF16), vec, vec],
        compiler_params=_cparams("parallel", "arbitrary"),
    )(y_ssd, xbc_act, proj, dskip64, norm_g, dyc)


def _ssd_assemble(dxs_ssd, dxs_skip, dbc, name):
    t = dxs_ssd.shape[0]
    tm = _row_tile(t)
    npair = C_WIDTH // LANES

    def body(a_ref, b_ref, dbc_ref, o_ref):
        o_ref[:, pl.ds(0, C_WIDTH)] = a_ref[...] + b_ref[...]
        for grp in range(2):
            for which in range(2):
                acc = jnp.zeros((tm, SSM_STATE), F32)
                for j in range(grp * npair // 2, (grp + 1) * npair // 2):
                    acc = acc + dbc_ref[:, pl.ds((2 * j + which) * SSM_STATE, SSM_STATE)]
                o_ref[:, pl.ds(C_WIDTH + (2 * which + grp) * SSM_STATE, SSM_STATE)] = acc

    return pl.pallas_call(
        body, name=name, grid=(t // tm,),
        in_specs=[pl.BlockSpec((tm, C_WIDTH), lambda i: (i, 0))] * 2 + [pl.BlockSpec((tm, 2 * C_WIDTH), lambda i: (i, 0))],
        out_specs=pl.BlockSpec((tm, D_CONV_C), lambda i: (i, 0)),
        out_shape=jax.ShapeDtypeStruct((t, D_CONV_C), F32),
        compiler_params=_cparams("parallel"),
    )(dxs_ssd, dxs_skip, dbc)


def _pad_taps(w):
    return jnp.pad(w, ((0, CONV_PAD - w.shape[0]), (0, 0)))


def _pad_heads(v):
    return jnp.pad(v, (0, LANES - v.shape[0])).reshape(1, LANES)


def _group_heads(a):
    pad = [(0, 0)] * (a.ndim - 1) + [(0, LANES - SSD_GROUP_HEADS)]
    return jnp.concatenate([jnp.pad(a[..., :SSD_GROUP_HEADS], pad), jnp.pad(a[..., SSD_GROUP_HEADS:], pad)], axis=-1)


def _ungroup_heads(a):
    return jnp.concatenate([a[..., :SSD_GROUP_HEADS], a[..., LANES:LANES + SSD_GROUP_HEADS]], axis=-1)


def _layer_fwd(x, p, seq, li, after=()):
    n = f"l{li}_"
    h1 = _rms_fwd(x, p["norm1_g"], n + "rms1", after=after)
    proj = _matmul(h1, p["w_main"], mode="nn", name=n + "inproj")
    dt_raw = _matmul(h1, p["w_dt"], mode="nn", name=n + "inproj_dt")
    row = lambda v: v.reshape(1, -1)
    ya, conv_a = _conv_fwd("a", proj, _pad_taps(p["conv_a_w"]), row(p["conv_a_b"]), (row(p["ln_a_g"]), row(p["ln_a_b"])),
                           seq, n + "conva", out_dtype=BF16, keep_conv=True)
    yb = _gmlp_fwd(proj, row(p["ln_b_g"]), row(p["ln_b_b"]), p["w_spatial"], p["b_spatial"], n + "gmlp")
    xbc_act = _conv_fwd("c", proj, _pad_taps(p["conv_c_w"]), row(p["conv_c_b"]), (), seq, n + "convc")
    y_ssd, prev = _ssd2_fwd(xbc_act, dt_raw, _group_heads(row(p["dt_bias"])), _group_heads(row(p["a_log"])), seq, n + "ssd")
    dskip64 = jnp.repeat(p["d_skip"], HEAD_DIM).reshape(1, C_WIDTH)
    yc = _ssd_post_fwd(y_ssd, xbc_act, proj, dskip64, row(p["norm_c_g"]), n + "ssdpost")
    ycat = jnp.concatenate([ya, yb, yc], axis=1)
    x1 = _matmul(ycat, p["w_out"], mode="nn", name=n + "outproj", add=x)
    h2 = _rms_fwd(x1, p["norm2_g"], n + "rms2")
    u, act = _matmul(h2, p["w_ff1"], mode="nn", name=n + "ff1", epilogue=_relu2_epilogue, out_dtypes=(F32, BF16),
                     b_chips=True)
    x2 = _matmul(act, p["w_ff2"], mode="nn", name=n + "ff2", add=x1)
    saved = dict(x=x, h1=h1, proj=proj, conv_a=conv_a, dt_raw=dt_raw, xbc_act=xbc_act, prev=prev, y_ssd=y_ssd,
                 dskip64=dskip64, ycat=ycat, x1=x1, h2=h2, u=u, act=act)
    return x2, saved


def _layer_bwd(dx2, p, s, seq, li, after=(), on_ffn_grads=None):
    n = f"l{li}_b_"
    row = lambda v: v.reshape(1, -1)
    g = {}
    du = _matmul(dx2, p["w_ff2"], mode="nt", name=n + "ff2_dx", epilogue=_relu2_bwd_epilogue, extra=s["u"],
                 out_dtypes=(BF16,), after=after)
    g["w_ff2"] = _matmul(s["act"], dx2, mode="tn", name=n + "ff2_dw")
    g["w_ff1"] = _matmul(s["h2"], du, mode="tn", name=n + "ff1_dw", out_chips=True)
    dh2 = _matmul(du, p["w_ff1"], mode="nt", name=n + "ff1_dx", b_chips=True)
    dx1, g["norm2_g"] = _rms_bwd(s["x1"], p["norm2_g"], dh2, dx2, n + "rms2")
    g["w_out"] = _matmul(s["ycat"], dx1, mode="tn", name=n + "out_dw")
    dycat = _matmul(dx1, p["w_out"], mode="nt", name=n + "out_dx",
                    after=() if on_ffn_grads is None else on_ffn_grads(g))
    proj = s["proj"]
    (dval, dgate), dwa, dba, (dlag, dlab) = _conv_bwd(
        "a", proj, _pad_taps(p["conv_a_w"]), row(p["conv_a_b"]), (row(p["ln_a_g"]), row(p["ln_a_b"])), dycat, seq,
        n + "conva", dy_col=0, conv_out=s["conv_a"])
    g["conv_a_w"], g["conv_a_b"], g["ln_a_g"], g["ln_a_b"] = dwa[:CONV_A_K], dba[0], dlag[0], dlab[0]
    dbu, dbv, dlbg, dlbb, g["w_spatial"], g["b_spatial"] = _gmlp_bwd(
        proj, row(p["ln_b_g"]), row(p["ln_b_b"]), p["w_spatial"], p["b_spatial"], dycat, n + "gmlp",
        dy_col=A_WIDTH // LANES)
    g["ln_b_g"], g["ln_b_b"] = dlbg[0], dlbb[0]
    dy_ssd, dxs_skip, dz, dds, dncg = _ssd_post_bwd(s["y_ssd"], s["xbc_act"], proj, s["dskip64"], row(p["norm_c_g"]),
                                                    dycat, n + "ssdpost", dy_col=(A_WIDTH + B_WIDTH) * 2 // C_WIDTH)
    g["norm_c_g"] = dncg[0]
    g["d_skip"] = dds.reshape(C_HEADS, HEAD_DIM).sum(axis=1)
    dxs, ddt_raw, ddtb, dalog, dbm, dcm = _ssd2_bwd(
        s["xbc_act"], s["dt_raw"], _group_heads(row(p["dt_bias"])), _group_heads(row(p["a_log"])), s["prev"], dy_ssd, seq,
        n + "ssd")
    g["dt_bias"], g["a_log"] = _ungroup_heads(ddtb)[0], _ungroup_heads(dalog)[0]
    dconv = _ssd2_assemble(dxs, dxs_skip, dbm, dcm, n + "ssdasm")
    (dxbc,), dwc, dbcv, _ = _conv_bwd("c", proj, _pad_taps(p["conv_c_w"]), row(p["conv_c_b"]), (), dconv, seq, n + "convc")
    g["conv_c_w"], g["conv_c_b"] = dwc[:CONV_C_K], dbcv[0]
    dproj = jnp.concatenate([dval, dgate, dbu, dbv, dz, dxbc], axis=1)
    g["w_main"] = _matmul(s["h1"], dproj, mode="tn", name=n + "in_dw")
    g["w_dt"] = _matmul(s["h1"], ddt_raw, mode="tn", name=n + "indt_dw")
    dh1 = _matmul(dproj, p["w_main"], mode="nt", name=n + "in_dx")
    dh1 = _matmul(ddt_raw, p["w_dt"], mode="nt", name=n + "indt_dx", add=dh1)
    dx, g["norm1_g"] = _rms_bwd(s["x"], p["norm1_g"], dh1, dx1, n + "rms1")
    return dx, g


EW_BLOCK_BYTES = 1 << 20


def _ew(fn, ins, out_dtypes, name, leads=None):
    leads = leads or [None] * len(ins)
    rows, c = ins[0].shape[-2:]
    tr = _pick(rows, [t for t in (2048, 1024, 512, 256, 128, 64, 32, 16, 8) if t * c * 4 <= EW_BLOCK_BYTES])
    n_in = len(ins)

    def spec(lead):
        if lead is None:
            return pl.BlockSpec((tr, c), lambda i: (i, 0))
        return pl.BlockSpec((None, tr, c), functools.partial(lambda i, k: (k, i, 0), k=lead))

    def body(*refs):
        outs = fn(*[r[...].astype(F32) for r in refs[:n_in]])
        for o_ref, o in zip(refs[n_in:], outs):
            o_ref[...] = o.astype(o_ref.dtype)

    return pl.pallas_call(
        body, name=name, grid=(rows // tr,),
        in_specs=[spec(l) for l in leads], out_specs=[spec(None)] * len(out_dtypes),
        out_shape=[jax.ShapeDtypeStruct((rows, c), dt) for dt in out_dtypes],
        compiler_params=_cparams("parallel"),
    )(*ins)


def _adam_fn(w, g, m, v):
    m2 = ADAM_B1 * m + (1.0 - ADAM_B1) * g
    v2 = ADAM_B2 * v + (1.0 - ADAM_B2) * (g * g)
    m_hat = m2 / (1.0 - ADAM_B1 ** ADAM_STEP)
    v_hat = v2 / (1.0 - ADAM_B2 ** ADAM_STEP)
    delta = -ADAM_LR * (m_hat / (jnp.sqrt(v_hat) + ADAM_EPS) + ADAM_WD * w)
    return delta, m2, v2


def _adam(w, g, m, v, name):
    shape = w.shape
    two_d = lambda a: a.reshape(-1, shape[-1])
    outs = _ew(_adam_fn, [two_d(w), two_d(g), two_d(m), two_d(v)], (F32, F32, F32), name)
    return [o.reshape(shape) for o in outs]


_ANY = pl.BlockSpec(memory_space=pl.ANY)


def _mesh_pos():
    return lax.axis_index("x"), lax.axis_index("y"), lax.axis_index("c")


def _peer_chips(x, y):
    return [(1 - x, y), (x, 1 - y), (1 - x, 1 - y)]


def _remote(src, dst, send_sems, recv_sems, sem, to):
    return pltpu.make_async_remote_copy(src_ref=src, dst_ref=dst, send_sem=send_sems.at[sem],
                                        recv_sem=recv_sems.at[sem], device_id=to, device_id_type=MESH)


def _half_rows(n_rows, which):
    half = n_rows // 2
    return pl.ds(pl.multiple_of(which * half, 8), half)


def _comm_call(body, ins, out_shapes, n_sems, name):
    scratch = [pltpu.SemaphoreType.DMA((n_sems,)), pltpu.SemaphoreType.DMA((n_sems,))]
    return pl.pallas_call(
        body, name=name, in_specs=[_ANY] * len(ins), out_specs=[_ANY] * len(out_shapes),
        out_shape=out_shapes, scratch_shapes=scratch,
    )(*ins)


def _gather_weights(big, small, name):
    nb, ns = len(big), len(small)
    n = nb + ns

    def body(*refs):
        ins, outs = refs[:n], refs[n:2 * n]
        send_sems, recv_sems = refs[2 * n:]
        x, y, c = _mesh_pos()
        q = 2 * x + y
        me, sib = (x, y, c), (x, y, 1 - c)
        chips = _peer_chips(x, y)
        rem = functools.partial(_remote, send_sems=send_sems, recv_sems=recv_sems)
        first = []
        for i in range(nb):
            mine = _half_rows(big[i].shape[0], c)
            for k, (px, py) in enumerate(chips):
                first.append(rem(ins[i].at[mine], outs[i].at[q, mine], sem=6 * i + k, to=(px, py, c)))
        for j in range(ns):
            for k, (px, py) in enumerate(chips):
                first.append(rem(ins[nb + j], outs[nb + j].at[q], sem=6 * nb + 3 * j + k, to=(px, py, c)))
        for cp in first:
            cp.start()
        passed = []
        for i in range(nb):
            mine = _half_rows(big[i].shape[0], c)
            for k, (px, py) in enumerate(chips):
                landed = outs[i].at[2 * px + py, mine]
                rem(landed, landed, sem=6 * i + k, to=me).wait_recv()
                fwd = rem(landed, landed, sem=6 * i + 3 + k, to=sib)
                fwd.start()
                passed.append(fwd)
        for i in range(nb):
            other = _half_rows(big[i].shape[0], 1 - c)
            for k, (px, py) in enumerate(chips):
                theirs = outs[i].at[2 * px + py, other]
                rem(theirs, theirs, sem=6 * i + 3 + k, to=me).wait_recv()
        for j in range(ns):
            for k, (px, py) in enumerate(chips):
                dst = outs[nb + j].at[2 * px + py]
                rem(dst, dst, sem=6 * nb + 3 * j + k, to=me).wait_recv()
        for cp in first + passed:
            cp.wait_send()

    out_shapes = [jax.ShapeDtypeStruct((N_CHIPS,) + a.shape, a.dtype) for a in list(big) + list(small)]
    return _comm_call(body, list(big) + list(small), out_shapes, 6 * nb + 3 * ns, name)


def _sibling_other_halves(gs, name):
    n = len(gs)

    def other_half(ref, shape, c):
        rows = _half_rows(shape[-2], 1 - c)
        return ref.at[rows] if len(shape) == 2 else ref.at[:, rows]

    def body(*refs):
        ins, outs = refs[:n], refs[n:2 * n]
        send_sems, recv_sems = refs[2 * n:]
        x, y, c = _mesh_pos()
        copies = [_remote(other_half(ins[i], gs[i].shape, c), outs[i], send_sems, recv_sems, i, (x, y, 1 - c))
                  for i in range(n)]
        for cp in copies:
            cp.start()
        for cp in copies:
            cp.wait()

    out_shapes = [jax.ShapeDtypeStruct(g.shape[:-2] + (g.shape[-2] // 2, g.shape[-1]), g.dtype) for g in gs]
    return _comm_call(body, list(gs), out_shapes, n, name)


IN_SHARD = D_IN_PROJ // N_CHIPS


def _chipsum_in(mine, mine_dt, theirs, theirs_dt, name):
    r = mine.shape[0]
    tr = _pick(r, (128, 64, 32, 16, 8))
    last = D_MAIN - (N_CHIPS - 1) * IN_SHARD

    def body(a_ref, adt_ref, b_ref, bdt_ref, o32_ref, o16_ref):
        for p in range(N_CHIPS):
            wid = IN_SHARD if p < N_CHIPS - 1 else last
            s = a_ref[:, pl.ds(IN_SHARD * p, wid)] + b_ref[:, pl.ds(IN_SHARD * p, wid)]
            o32_ref[p, :, pl.ds(0, wid)] = s
            o16_ref[p, :, pl.ds(0, wid)] = s.astype(BF16)
        for grp in range(2):
            src = pl.ds(grp * LANES, SSD_GROUP_HEADS)
            s = adt_ref[:, src] + bdt_ref[:, src]
            dst = pl.ds(last + grp * SSD_GROUP_HEADS, SSD_GROUP_HEADS)
            o32_ref[N_CHIPS - 1, :, dst] = s
            o16_ref[N_CHIPS - 1, :, dst] = s.astype(BF16)

    wide = pl.BlockSpec((tr, D_MAIN), lambda i: (i, 0))
    narrow = pl.BlockSpec((tr, 2 * LANES), lambda i: (i, 0))
    out = pl.BlockSpec((N_CHIPS, tr, IN_SHARD), lambda i: (0, i, 0))
    return pl.pallas_call(
        body, name=name, grid=(r // tr,), in_specs=[wide, narrow, wide, narrow], out_specs=[out, out],
        out_shape=[jax.ShapeDtypeStruct((N_CHIPS, r, IN_SHARD), F32), jax.ShapeDtypeStruct((N_CHIPS, r, IN_SHARD), BF16)],
        compiler_params=_cparams("parallel"),
    )(mine, mine_dt, theirs, theirs_dt)


def _chip_scatter(cs, name):
    n = len(cs)

    def body(*refs):
        ins, outs = refs[:n], refs[n:2 * n]
        send_sems, recv_sems = refs[2 * n:]
        x, y, c = _mesh_pos()
        copies = []
        for i in range(n):
            for k, (px, py) in enumerate(_peer_chips(x, y)):
                copies.append(_remote(ins[i].at[2 * px + py], outs[i].at[k], send_sems, recv_sems, 3 * i + k, (px, py, c)))
        for cp in copies:
            cp.start()
        for cp in copies:
            cp.wait()

    out_shapes = [jax.ShapeDtypeStruct((3,) + a.shape[1:], a.dtype) for a in cs]
    return _comm_call(body, list(cs), out_shapes, 3 * n, name)


_HBM = pl.BlockSpec(memory_space=pltpu.HBM)
_SEM = pl.BlockSpec(memory_space=pltpu.SEMAPHORE)


def _in_hbm(a):
    return pltpu.with_memory_space_constraint(a, pltpu.HBM)


def _split_plan(kind, srcs, lands, x, y, c):
    plan = []
    for src, land in zip(srcs, lands):
        if kind == "sibling":
            rows = _half_rows(src.shape[-2], 1 - c)
            plan.append((src.at[rows] if len(src.shape) == 2 else src.at[:, rows], land, (x, y, 1 - c)))
            continue
        for k, (px, py) in enumerate(_peer_chips(x, y)):
            if kind == "scatter":
                plan.append((src.at[2 * px + py], land.at[k], (px, py, c)))
            else:
                plan.append((src, land.at[2 * x + y], (px, py, c)))
    return plan


def _split_start(kind, srcs, land_shapes, name):
    n = len(srcs)

    def body(*refs):
        ins, lands = refs[:n], refs[n:2 * n]
        send_sems, recv_sems = refs[2 * n], refs[2 * n + 1]
        token = refs[-1]
        x, y, c = _mesh_pos()
        for i, (src, dst, to) in enumerate(_split_plan(kind, ins, lands, x, y, c)):
            pltpu.make_async_remote_copy(src_ref=src, dst_ref=dst, send_sem=send_sems.at[i], recv_sem=recv_sems.at[i],
                                         device_id=to, device_id_type=MESH).start()
        token[...] = jnp.zeros_like(token)

    zones = [lax.empty(s.shape, s.dtype) for s in land_shapes]
    n_sems = n if kind == "sibling" else 3 * n
    res = pl.pallas_call(
        body, name=name,
        out_shape=(pltpu.SemaphoreType.DMA((n_sems,)), pltpu.SemaphoreType.DMA((n_sems,)),
                   *[pltpu.HBM(a.shape, a.dtype) for a in srcs], *[pltpu.HBM(s.shape, s.dtype) for s in land_shapes],
                   jax.ShapeDtypeStruct((8, LANES), F32)),
        in_specs=[_HBM] * (2 * n), out_specs=(_SEM, _SEM, *[_HBM] * (2 * n), pl.BlockSpec(memory_space=pltpu.VMEM)),
        input_output_aliases={i: 2 + i for i in range(2 * n)},
        compiler_params=pltpu.CompilerParams(has_side_effects=pltpu.SideEffectType.DATAFLOW_SIDE_EFFECTING),
    )(*[_in_hbm(a) for a in srcs], *[_in_hbm(z) for z in zones])
    return dict(send=res[0], recv=res[1], srcs=list(res[2:2 + n]), lands=list(res[2 + n:2 + 2 * n]), token=res[-1], kind=kind)


def _split_wait(started, after, name):
    n = len(started["srcs"])
    kind = started["kind"]

    def body(*refs):
        ins, lands = refs[:n], refs[n:2 * n]
        send_sems, recv_sems = refs[2 * n], refs[2 * n + 1]
        x, y, c = _mesh_pos()
        for i, (src, dst, _) in enumerate(_split_plan(kind, ins, lands, x, y, c)):
            cp = pltpu.make_async_remote_copy(src_ref=src, dst_ref=dst, send_sem=send_sems.at[i], recv_sem=recv_sems.at[i],
                                              device_id=(x, y, c), device_id_type=MESH)
            cp.wait_send()
            cp.wait_recv()

    arrs = started["srcs"] + started["lands"]
    res = pl.pallas_call(
        body, name=name, out_shape=tuple(pltpu.HBM(a.shape, a.dtype) for a in arrs),
        in_specs=[_HBM] * (2 * n) + [_SEM, _SEM, pl.BlockSpec(memory_space=pl.ANY)], out_specs=tuple([_HBM] * (2 * n)),
        input_output_aliases={i: i for i in range(2 * n)},
        compiler_params=pltpu.CompilerParams(has_side_effects=pltpu.SideEffectType.DATAFLOW_SIDE_EFFECTING),
    )(*arrs, started["send"], started["recv"], after)
    return list(res[:n]), list(res[n:])


def _sibling_share(fs, name):
    n = len(fs)

    def body(*refs):
        ins, outs = refs[:n], refs[n:2 * n]
        send_sems, recv_sems = refs[2 * n:]
        x, y, c = _mesh_pos()
        copies = [_remote(ins[i], outs[i], send_sems, recv_sems, i, (x, y, 1 - c)) for i in range(n)]
        for cp in copies:
            cp.start()
        for cp in copies:
            cp.wait()

    out_shapes = [jax.ShapeDtypeStruct(a.shape, a.dtype) for a in fs]
    return _comm_call(body, list(fs), out_shapes, n, name)


def _allgather8(v, name, after=()):
    m = v.shape[0]

    def body(v_ref, *rest):
        out_ref, send_sems, recv_sems = rest[len(after):]
        x, y, c = _mesh_pos()
        me, sib = (x, y, c), (x, y, 1 - c)
        chips = _peer_chips(x, y)
        rem = functools.partial(_remote, send_sems=send_sems, recv_sems=recv_sems)

        def blk(px, py, pc):
            return out_ref.at[4 * px + 2 * py + pc]

        first = [rem(v_ref, blk(*me), sem=0, to=sib)]
        first += [rem(v_ref, blk(*me), sem=1 + k, to=(px, py, c)) for k, (px, py) in enumerate(chips)]
        for cp in first:
            cp.start()
        passed = []
        for k, (px, py) in enumerate(chips):
            landed = blk(px, py, c)
            rem(landed, landed, sem=1 + k, to=me).wait_recv()
            fwd = rem(landed, landed, sem=4 + k, to=sib)
            fwd.start()
            passed.append(fwd)
        rem(blk(*sib), blk(*sib), sem=0, to=me).wait_recv()
        for k, (px, py) in enumerate(chips):
            theirs = blk(px, py, 1 - c)
            rem(theirs, theirs, sem=4 + k, to=me).wait_recv()
        for cp in first + passed:
            cp.wait_send()

    return _comm_call(body, [v, *after], [jax.ShapeDtypeStruct((8, m, LANES), v.dtype)], 7, name)[0]


_WEIGHTS = ["norm1_g", "w_in", "conv_a_w", "conv_a_b", "ln_a_g", "ln_a_b", "ln_b_g", "ln_b_b", "w_spatial", "b_spatial",
            "conv_c_w", "conv_c_b", "dt_bias", "a_log", "d_skip", "norm_c_g", "w_out", "norm2_g", "w_ff1", "w_ff2", "final_g"]
_BIG = ["w_in", "w_out", "w_ff1", "w_ff2"]
_CONV_SHARDED = ["conv_a_w", "conv_c_w"]
_SMALL = [w for w in _WEIGHTS if w not in _BIG and w != "final_g"]
_PACK_ROWS = 512


def _pack(arrs):
    flat = jnp.concatenate([a.reshape(-1) for a in arrs])
    blk = _PACK_ROWS * LANES
    n = flat.shape[0]
    return jnp.pad(flat, (0, -(-n // blk) * blk - n)).reshape(-1, LANES)


def _unpack(packed, shapes):
    flat = packed.reshape(-1)
    out, off = [], 0
    for s in shapes:
        n = math.prod(s)
        out.append(flat[off:off + n].reshape(s))
        off += n
    return out


def _cols_to_chips(a):
    k = a.shape[0]
    return a.reshape(k, N_CHIPS, -1).transpose(1, 0, 2)


def _chips_to_cols(a):
    return a.transpose(1, 0, 2).reshape(a.shape[1], -1)


def _own_shards(w, li):
    return [w[k][li].astype(BF16) for k in _BIG] + [w[k][li] for k in _CONV_SHARDED]


def _layer_params(w, li, own, gathered, q):
    g_in, g_out, g_ff1, g_ff2, g_ca, g_cc = [lax.dynamic_update_index_in_dim(g, o, q, axis=0)
                                             for g, o in zip(gathered, own)]
    p = {k: w[k][li] for k in _SMALL if k not in _CONV_SHARDED}
    w_in = _chips_to_cols(g_in)
    p["w_main"] = w_in[:, :D_MAIN]
    p["w_dt"] = _group_heads(w_in[:, D_MAIN:])
    p["w_out"] = g_out.reshape(D_MIX, D_MODEL)
    p["w_ff1"] = g_ff1
    p["w_ff2"] = g_ff2.reshape(D_FF, D_MODEL)
    p["conv_a_w"] = _chips_to_cols(g_ca)
    p["conv_c_w"] = _chips_to_cols(g_cc)
    return p


def _ffn_out_grads(g):
    return [g["w_out"].reshape(N_CHIPS, -1, D_MODEL), g["w_ff1"], g["w_ff2"].reshape(N_CHIPS, -1, D_MODEL)]


def _half_shape(a):
    return jax.ShapeDtypeStruct(a.shape[:-2] + (a.shape[-2] // 2, a.shape[-1]), a.dtype)


def _chip_sums(g, early, early_from_sib, li, c, q):
    n = f"l{li}_rs_"
    late = [g["w_main"], g["w_dt"]]
    full = late + list(early)
    from_sib = list(_sibling_other_halves(late, n + "sib")) + list(early_from_sib)
    mine = [lax.dynamic_slice_in_dim(a, c * b.shape[-2], b.shape[-2], axis=a.ndim - 2) for a, b in zip(full, from_sib)]
    sums = [_chipsum_in(mine[0], mine[1], from_sib[0], from_sib[1], n + "chipsum0")]
    for i in range(2, len(full)):
        shape = from_sib[i].shape
        s32, s16 = _ew(lambda u, v: (u + v, u + v), [mine[i].reshape(-1, shape[-1]), from_sib[i].reshape(-1, shape[-1])],
                       (F32, BF16), n + f"chipsum{i - 1}")
        sums.append((s32.reshape(shape), s16.reshape(shape)))
    chip_f32 = [lax.dynamic_index_in_dim(s32, q, axis=0, keepdims=False) for s32, _ in sums]
    chip_bf16 = [s16 for _, s16 in sums]
    return chip_f32, chip_bf16


def _finish_reduce(chip_f32, from_chips, li, c):
    n = f"l{li}_rs_"
    halves = [_ew(lambda o, r0, r1, r2_: (((o + r0) + r1) + r2_,), [own, rb, rb, rb], (F32,), n + f"final{i}",
                  leads=[None, 0, 1, 2])[0] for i, (own, rb) in enumerate(zip(chip_f32, from_chips))]
    from_sib = _sibling_share(halves, n + "share")
    return [jnp.where(c == 0, jnp.concatenate([h, s], axis=0), jnp.concatenate([s, h], axis=0))
            for h, s in zip(halves, from_sib)]


def kernel(x, norm1_g, w_in, conv_a_w, conv_a_b, ln_a_g, ln_a_b, ln_b_g, ln_b_b, w_spatial, b_spatial, conv_c_w, conv_c_b, dt_bias, a_log, d_skip, norm_c_g, w_out, norm2_g, w_ff1, w_ff2, final_g, loss_target, m_norm1_g, m_w_in, m_conv_a_w, m_conv_a_b, m_ln_a_g, m_ln_a_b, m_ln_b_g, m_ln_b_b, m_w_spatial, m_b_spatial, m_conv_c_w, m_conv_c_b, m_dt_bias, m_a_log, m_d_skip, m_norm_c_g, m_w_out, m_norm2_g, m_w_ff1, m_w_ff2, m_final_g, v_norm1_g, v_w_in, v_conv_a_w, v_conv_a_b, v_ln_a_g, v_ln_a_b, v_ln_b_g, v_ln_b_b, v_w_spatial, v_b_spatial, v_conv_c_w, v_conv_c_b, v_dt_bias, v_a_log, v_d_skip, v_norm_c_g, v_w_out, v_norm2_g, v_w_ff1, v_w_ff2, v_final_g):
    given = dict(locals())
    w = {k: given[k] for k in _WEIGHTS}
    m = {k: given["m_" + k] for k in _WEIGHTS}
    v = {k: given["v_" + k] for k in _WEIGHTS}
    depth = w_in.shape[0]
    nseq, seq, d = x.shape
    xi, yi, ci = _mesh_pos()
    q = 2 * xi + yi

    own = [_own_shards(w, li) for li in range(depth)]
    nb = len(_BIG)
    gathered = _gather_weights(own[0][:nb], own[0][nb:], "l0_gather")
    h = x.reshape(nseq * seq, d)
    layer_params, saved = [], []
    for li in range(depth):
        nxt = None
        if li + 1 < depth:
            srcs, _ = lax.optimization_barrier((own[li + 1], gathered))
            zones = [jax.ShapeDtypeStruct((N_CHIPS,) + a.shape, a.dtype) for a in srcs]
            nxt = _split_start("gather", srcs, zones, f"l{li + 1}_gather_start")
        layer_params.append(_layer_params(w, li, own[li], gathered, q))
        h, s = _layer_fwd(h, layer_params[li], seq, li, after=() if nxt is None else (nxt["token"],))
        saved.append(s)
        if nxt is not None:
            own[li + 1], gathered = _split_wait(nxt, h, f"l{li + 1}_gather_wait")
    loss, dx, d_final = _loss_head(h, final_g, loss_target.reshape(nseq * seq, d))

    grads = [None] * depth
    big_grads = [None] * depth
    pending = None
    for li in reversed(range(depth)):
        swaps = []

        def early_swap(g, li=li, swaps=swaps):
            early = _ffn_out_grads(g)
            swaps.append(_split_start("sibling", early, [_half_shape(a) for a in early], f"l{li}_rs_sib_start"))
            return (swaps[0]["token"],)

        dx, grads[li] = _layer_bwd(dx, layer_params[li], saved[li], seq, li,
                                   after=() if pending is None else (pending[1]["token"],), on_ffn_grads=early_swap)
        if pending is not None:
            lj, scatter, chip_f32 = pending
            big_grads[lj] = _finish_reduce(chip_f32, _split_wait(scatter, dx, f"l{lj}_rs_scatter_wait")[1], lj, ci)
        early, early_from_sib = _split_wait(swaps[0], dx, f"l{li}_rs_sib_wait")
        chip_f32, chip_bf16 = _chip_sums(grads[li], early, early_from_sib, li, ci, q)
        lands = [jax.ShapeDtypeStruct((3,) + a.shape[1:], a.dtype) for a in chip_bf16]
        pending = (li, _split_start("scatter", chip_bf16, lands, f"l{li}_rs_scatter_start"), chip_f32)
    grad_out, delta_out, m_out, v_out = {}, {}, {}, {}

    small_shapes = [grads[0][k].shape for k in _SMALL]
    parts = [grads[li][k] for li in range(depth) for k in _SMALL] + [d_final, loss.reshape(1)]
    packed_parts = _pack(parts)
    gathered = lax.dynamic_update_index_in_dim(_allgather8(packed_parts, "small_allgather", after=(pending[1]["token"],)),
                                               packed_parts, 2 * q + ci, axis=0)

    def sum8(*blocks):
        acc = blocks[0]
        for b in blocks[1:]:
            acc = acc + b
        return (acc,)

    total = _ew(sum8, [gathered] * 8, (F32,), "small_sum", leads=list(range(8)))[0]
    summed = _unpack(total, small_shapes * depth + [d_final.shape, (1,)])
    loss_total = summed[-1][0]
    small_grads = {k: jnp.stack([summed[li * len(_SMALL) + i] for li in range(depth)]) for i, k in enumerate(_SMALL)}
    small_grads["final_g"] = summed[-2]
    for k in _CONV_SHARDED:
        n_shard = w[k].shape[-1]
        small_grads[k] = lax.dynamic_slice_in_dim(small_grads[k], q * n_shard, n_shard, axis=2)
    names = _SMALL + ["final_g"]
    shapes = [w[k].shape for k in names]
    packed = [_pack([src[k] for k in names]) for src in (w, small_grads, m, v)]
    outs = _ew(_adam_fn, packed, (F32, F32, F32), "adam_small")
    for dst, o in zip((delta_out, m_out, v_out), outs):
        for k, a in zip(names, _unpack(o, shapes)):
            dst[k] = a
    for k in names:
        grad_out[k] = small_grads[k]

    lj, scatter, chip_f32 = pending
    big_grads[lj] = _finish_reduce(chip_f32, _split_wait(scatter, outs[0], f"l{lj}_rs_scatter_wait")[1], lj, ci)
    for i, k in enumerate(_BIG):
        grad_out[k] = jnp.stack([big_grads[li][i] for li in range(depth)])
        delta_out[k], m_out[k], v_out[k] = _adam(w[k], grad_out[k], m[k], v[k], "adam_" + k)

    return (loss_total, dx.reshape(nseq, seq, d), *[grad_out[k] for k in _WEIGHTS], *[delta_out[k] for k in _WEIGHTS],
            *[m_out[k] for k in _WEIGHTS], *[v_out[k] for k in _WEIGHTS])
```

```python
import functools
import math

import jax
import jax.numpy as jnp
from jax import lax
from jax.experimental import pallas as pl
from jax.experimental.pallas import tpu as pltpu

F32 = jnp.float32
BF16 = jnp.bfloat16
MESH = pl.DeviceIdType.MESH

D_MODEL = 1024
DEPTH = 4
HEAD_DIM = 64
A_WIDTH = 512
B_WIDTH = 512
C_WIDTH = 1024
C_HEADS = 16
CONV_A_K = 31
CONV_C_K = 4
CHUNK = 128
SSM_STATE = 128
D_CONV_C = 1536
D_MAIN = 4608
D_IN_PROJ = 4624
D_MIX = 2048
D_FF = 4096
EPS = 1e-5
NEG = -1e30
LANES = 128
CONV_PAD = 32
N_CHIPS = 4

ADAM_LR = 0.001
ADAM_B1 = 0.9
ADAM_B2 = 0.999
ADAM_EPS = 1e-08
ADAM_WD = 0.01
ADAM_STEP = 10

VMEM_LIMIT = 56 * 1024 * 1024
MATMUL_VMEM_BUDGET = 44 * 1024 * 1024

COL_AVAL, COL_AGATE, COL_BU, COL_BV, COL_Z, COL_XBC = 0, 4, 8, 12, 16, 24


def _cparams(*sem):
    return pltpu.CompilerParams(dimension_semantics=sem, vmem_limit_bytes=VMEM_LIMIT)


_DN = {"nn": (((1,), (0,)), ((), ())), "nt": (((1,), (1,)), ((), ())), "tn": (((0,), (0,)), ((), ()))}


def _dot_raw(a, b, mode):
    return lax.dot_general(a.astype(BF16), b.astype(BF16), _DN[mode], preferred_element_type=F32)


def _make_dot(mode):
    @jax.custom_vjp
    def f(a, b):
        return _dot_raw(a, b, mode)

    def fwd(a, b):
        return _dot_raw(a, b, mode), (a, b)

    def bwd(res, g):
        a, b = res
        if mode == "nn":
            return _dot_raw(g, b, "nt"), _dot_raw(a, g, "tn")
        if mode == "nt":
            return _dot_raw(g, b, "nn"), _dot_raw(g, a, "tn")
        return _dot_raw(b, g, "nt"), _dot_raw(a, g, "nn")

    f.defvjp(fwd, bwd)
    return f


_nn = _make_dot("nn")
_nt = _make_dot("nt")
_tn = _make_dot("tn")


def _xdot(a, e):
    return jnp.dot(a, e, precision=lax.Precision.HIGHEST, preferred_element_type=F32)


def _iota2(shape, dim):
    return lax.broadcasted_iota(jnp.int32, shape, dim)


def _gmean_impl(x):
    n = x.shape[-1]
    same = (_iota2((n, n), 0) < HEAD_DIM) == (_iota2((n, n), 1) < HEAD_DIM)
    p = jnp.where(same, 1.0 / HEAD_DIM, 0.0).astype(BF16)
    hi = x.astype(BF16)
    lo = (x - hi.astype(F32)).astype(BF16)
    dn = _DN["nn"]
    return (lax.dot_general(hi, p, dn, preferred_element_type=F32)
            + lax.dot_general(lo, p, dn, preferred_element_type=F32))


@jax.custom_vjp
def _gmean(x):
    return _gmean_impl(x)


_gmean.defvjp(lambda x: (_gmean_impl(x), None), lambda _, g: (_gmean_impl(g),))


def _sigmoid(x):
    return 1.0 / (1.0 + jnp.exp(-x))


def _silu(x):
    return x * _sigmoid(x)


def _gelu(x):
    return 0.5 * x * (1.0 + lax.erf(x * 0.7071067811865476))


def _softplus(x):
    return jnp.maximum(x, 0.0) + jnp.log(1.0 + jnp.exp(-jnp.abs(x)))


def _rms(x, g):
    return x * lax.rsqrt(jnp.mean(x * x, axis=-1, keepdims=True) + EPS) * g


def _ln64(x, g, b):
    mu = _gmean(x)
    xc = x - mu
    var = _gmean(xc * xc)
    return xc * lax.rsqrt(var + EPS) * g + b


def _lane_lt64(shape):
    return _iota2(shape, 1) < HEAD_DIM


def _pick(n, pref):
    for t in pref:
        if n % t == 0:
            return t
    return n


_UNREAD = pl.BlockSpec(memory_space=pl.ANY)


def _matmul_tiles(m, n_unit, k, a_item, b_item, out_bytes):
    best = None
    for tm in (1024, 512, 256, 128):
        for tn in (1536, 1024, 768, 512, 256, 128):
            if m % tm or n_unit % tn:
                continue
            need = 2 * k * (tm * a_item + tn * b_item) + 2 * tm * tn * out_bytes
            if need <= MATMUL_VMEM_BUDGET and (best is None or tm * tn > best[0] * best[1]):
                best = (tm, tn)
    assert best is not None, (m, n_unit, k)
    return best


def _matmul(a, b, *, mode, name, add=None, epilogue=None, extra=None, out_dtypes=(F32,), after=(), b_chips=False,
            out_chips=False):
    sh = b.shape[-1] if b_chips else None
    if mode == "nn":
        (m, k), n = a.shape, (N_CHIPS * sh if b_chips else b.shape[1])
    elif mode == "nt":
        (m, k), n = a.shape, b.shape[-2]
    else:
        (k, m), n = a.shape, b.shape[1]
    osh = n // N_CHIPS if out_chips else None
    out_bytes = sum(jnp.dtype(dt).itemsize for dt in out_dtypes) + (0 if add is None else add.dtype.itemsize) \
        + (0 if extra is None else extra.dtype.itemsize)
    tm, tn = _matmul_tiles(m, sh if (b_chips and mode == "nn") else (osh or n), k, a.dtype.itemsize, b.dtype.itemsize,
                           out_bytes)
    a_spec = pl.BlockSpec((k, tm), lambda i, j: (0, i)) if mode == "tn" else pl.BlockSpec((tm, k), lambda i, j: (i, 0))
    if b_chips and mode == "nn":
        per = sh // tn
        b_spec = pl.BlockSpec((None, k, tn), lambda i, j: (j // per, 0, j % per))
    elif b_chips:
        b_spec = pl.BlockSpec((N_CHIPS, tn, sh), lambda i, j: (0, j, 0))
    elif mode == "nt":
        b_spec = pl.BlockSpec((tn, k), lambda i, j: (j, 0))
    else:
        b_spec = pl.BlockSpec((k, tn), lambda i, j: (0, j))
    if out_chips:
        o_per = osh // tn
        o_spec = pl.BlockSpec((None, tm, tn), lambda i, j: (j // o_per, i, j % o_per))
        out_shape = [jax.ShapeDtypeStruct((N_CHIPS, m, osh), dt) for dt in out_dtypes]
    else:
        o_spec = pl.BlockSpec((tm, tn), lambda i, j: (i, j))
        out_shape = [jax.ShapeDtypeStruct((m, n), dt) for dt in out_dtypes]
    ins = [a, b]
    in_specs = [a_spec, b_spec]
    if add is not None:
        ins.append(add)
        in_specs.append(o_spec)
    if extra is not None:
        ins.append(extra)
        in_specs.append(o_spec)
    ins += list(after)
    in_specs += [_UNREAD] * len(after)
    n_out = len(out_dtypes)

    def body(*refs):
        a_ref, b_ref = refs[0], refs[1]
        pos = 2
        add_ref = ex_ref = None
        if add is not None:
            add_ref = refs[pos]
            pos += 1
        if extra is not None:
            ex_ref = refs[pos]
            pos += 1
        pos += len(after)
        if b_chips and mode == "nt":
            acc = _dot_raw(a_ref[:, pl.ds(0, sh)], b_ref[0], mode)
            for chip in range(1, N_CHIPS):
                acc = acc + _dot_raw(a_ref[:, pl.ds(chip * sh, sh)], b_ref[chip], mode)
        else:
            acc = _dot_raw(a_ref[...], b_ref[...], mode)
        if add_ref is not None:
            acc = acc + add_ref[...].astype(F32)
        outs = (acc,) if epilogue is None else epilogue(acc, None if ex_ref is None else ex_ref[...])
        for o_ref, o in zip(refs[pos:pos + n_out], outs):
            o_ref[...] = o.astype(o_ref.dtype)

    res = pl.pallas_call(
        body, name=name, grid=(m // tm, n // tn), in_specs=in_specs, out_specs=[o_spec] * n_out, out_shape=out_shape,
        compiler_params=_cparams("parallel", "parallel"),
    )(*ins)
    return res[0] if n_out == 1 else res


def _matmul_ksplit(a, b, *, mode, name, add=None, epilogue=None, extra=None, out_dtypes=(F32,), after=(), b_chips=False,
                   out_chips=False):
    sh = b.shape[-1] if b_chips else None
    if mode == "nn":
        (m, k), n = a.shape, (N_CHIPS * sh if b_chips else b.shape[1])
    elif mode == "nt":
        (m, k), n = a.shape, b.shape[-2]
    else:
        (k, m), n = a.shape, b.shape[1]
    osh = n // N_CHIPS if out_chips else None
    tm = _pick(m, (1024, 512, 256, 128))
    tn = _pick(sh if (b_chips and mode == "nn") else (osh or n), (1536, 1024, 512, 256, 128))
    out_bytes = sum(jnp.dtype(dt).itemsize for dt in out_dtypes) + (0 if add is None else add.dtype.itemsize) \
        + (0 if extra is None else extra.dtype.itemsize)
    k_dim = sh if (b_chips and mode == "nt") else k
    for tk in (4096, 2048, 1536, 1024, 512, 256, 128):
        if k_dim % tk:
            continue
        acc_bytes = 0 if tk == k else 4 * tm * tn
        need = 2 * tk * (tm * a.dtype.itemsize + tn * b.dtype.itemsize) + 2 * tm * tn * out_bytes + acc_bytes
        if need <= MATMUL_VMEM_BUDGET:
            break
    nk = k // tk
    a_spec = {"nn": pl.BlockSpec((tm, tk), lambda i, j, kk: (i, kk)),
              "nt": pl.BlockSpec((tm, tk), lambda i, j, kk: (i, kk)),
              "tn": pl.BlockSpec((tk, tm), lambda i, j, kk: (kk, i))}[mode]
    if b_chips:
        per = sh // (tn if mode == "nn" else tk)
        b_spec = {"nn": pl.BlockSpec((None, tk, tn), lambda i, j, kk: (j // per, kk, j % per)),
                  "nt": pl.BlockSpec((None, tn, tk), lambda i, j, kk: (kk // per, j, kk % per))}[mode]
    else:
        b_spec = {"nn": pl.BlockSpec((tk, tn), lambda i, j, kk: (kk, j)),
                  "nt": pl.BlockSpec((tn, tk), lambda i, j, kk: (j, kk)),
                  "tn": pl.BlockSpec((tk, tn), lambda i, j, kk: (kk, j))}[mode]
    if out_chips:
        o_per = osh // tn
        o_spec = pl.BlockSpec((None, tm, tn), lambda i, j, kk: (j // o_per, i, j % o_per))
        out_shape = [jax.ShapeDtypeStruct((N_CHIPS, m, osh), dt) for dt in out_dtypes]
    else:
        o_spec = pl.BlockSpec((tm, tn), lambda i, j, kk: (i, j))
        out_shape = [jax.ShapeDtypeStruct((m, n), dt) for dt in out_dtypes]
    ins = [a, b]
    in_specs = [a_spec, b_spec]
    if add is not None:
        ins.append(add)
        in_specs.append(o_spec)
    if extra is not None:
        ins.append(extra)
        in_specs.append(o_spec)
    ins += list(after)
    in_specs += [_UNREAD] * len(after)
    n_out = len(out_dtypes)

    def body(*refs):
        a_ref, b_ref = refs[0], refs[1]
        pos = 2
        add_ref = ex_ref = None
        if add is not None:
            add_ref = refs[pos]
            pos += 1
        if extra is not None:
            ex_ref = refs[pos]
            pos += 1
        pos += len(after)
        o_refs = refs[pos:pos + n_out]

        def finish(acc):
            if add_ref is not None:
                acc = acc + add_ref[...].astype(F32)
            outs = (acc,) if epilogue is None else epilogue(acc, None if ex_ref is None else ex_ref[...])
            for o_ref, o in zip(o_refs, outs):
                o_ref[...] = o.astype(o_ref.dtype)

        part = _dot_raw(a_ref[...], b_ref[...], mode)
        if nk == 1:
            finish(part)
            return
        acc_ref = refs[pos + n_out]
        kk = pl.program_id(2)

        @pl.when(kk == 0)
        def _():
            acc_ref[...] = part

        @pl.when(jnp.logical_and(kk > 0, kk < nk - 1))
        def _():
            acc_ref[...] += part

        @pl.when(kk == nk - 1)
        def _():
            finish(acc_ref[...] + part)

    res = pl.pallas_call(
        body, name=name, grid=(m // tm, n // tn, nk),
        in_specs=in_specs, out_specs=[o_spec] * n_out, out_shape=out_shape,
        scratch_shapes=[pltpu.VMEM((tm, tn), F32)] if nk > 1 else [],
        compiler_params=_cparams("parallel", "parallel", "arbitrary"),
    )(*ins)
    return res[0] if n_out == 1 else res


def _relu2_epilogue(acc, _):
    r = jnp.maximum(acc, 0.0)
    return acc, r * r


def _relu2_bwd_epilogue(acc, u):
    return (acc * (2.0 * jnp.maximum(u, 0.0)),)


def _row_tile(t):
    return _pick(t, (512, 256, 128))


def _rms_fwd(x, g, name, after=()):
    t, d = x.shape
    tm = _row_tile(t)

    def body(x_ref, g_ref, *rest):
        o_ref = rest[-1]
        o_ref[...] = _rms(x_ref[...], g_ref[...]).astype(BF16)

    return pl.pallas_call(
        body, name=name, grid=(t // tm,),
        in_specs=[pl.BlockSpec((tm, d), lambda i: (i, 0)), pl.BlockSpec((1, d), lambda i: (0, 0))] + [_UNREAD] * len(after),
        out_specs=pl.BlockSpec((tm, d), lambda i: (i, 0)),
        out_shape=jax.ShapeDtypeStruct((t, d), BF16),
        compiler_params=_cparams("parallel"),
    )(x, g.reshape(1, d), *after)


def _rms_bwd(x, g, dh, dres, name):
    t, d = x.shape
    tm = _row_tile(t)

    def body(x_ref, g_ref, dh_ref, dres_ref, dx_ref, dg_ref):
        @pl.when(pl.program_id(0) == 0)
        def _():
            dg_ref[...] = jnp.zeros_like(dg_ref)

        _, vjp = jax.vjp(_rms, x_ref[...], g_ref[...])
        dx, dg = vjp(dh_ref[...].astype(F32))
        dx_ref[...] = dx + dres_ref[...]
        dg_ref[...] += dg

    row = pl.BlockSpec((tm, d), lambda i: (i, 0))
    vec = pl.BlockSpec((1, d), lambda i: (0, 0))
    dx, dg = pl.pallas_call(
        body, name=name, grid=(t // tm,),
        in_specs=[row, vec, row, row], out_specs=[row, vec],
        out_shape=[jax.ShapeDtypeStruct((t, d), F32), jax.ShapeDtypeStruct((1, d), F32)],
        compiler_params=_cparams("arbitrary"),
    )(x, g.reshape(1, d), dh, dres)
    return dx, dg.reshape(d)


def _loss_head(x, g, target):
    t, d = x.shape
    tm = _row_tile(t)

    def loss_fn(xv, gv, tv):
        err = _rms(xv, gv) - tv
        return 0.5 * jnp.sum(jnp.mean(err * err, axis=-1, keepdims=True))

    def body(x_ref, g_ref, t_ref, loss_ref, dx_ref, dg_ref):
        @pl.when(pl.program_id(0) == 0)
        def _():
            dg_ref[...] = jnp.zeros_like(dg_ref)
            loss_ref[...] = jnp.zeros_like(loss_ref)

        tv = t_ref[...]
        val, vjp = jax.vjp(lambda xv, gv: loss_fn(xv, gv, tv), x_ref[...], g_ref[...])
        dx, dg = vjp(jnp.ones((), F32))
        dx_ref[...] = dx
        dg_ref[...] += dg
        loss_ref[...] += jnp.full(loss_ref.shape, val, F32)

    row = pl.BlockSpec((tm, d), lambda i: (i, 0))
    vec = pl.BlockSpec((1, d), lambda i: (0, 0))
    loss, dx, dg = pl.pallas_call(
        body, name="loss_head", grid=(t // tm,),
        in_specs=[row, vec, row], out_specs=[pl.BlockSpec((1, LANES), lambda i: (0, 0)), row, vec],
        out_shape=[jax.ShapeDtypeStruct((1, LANES), F32), jax.ShapeDtypeStruct((t, d), F32),
                   jax.ShapeDtypeStruct((1, d), F32)],
        compiler_params=_cparams("arbitrary"),
    )(x, g.reshape(1, d), target)
    return loss[0, 0], dx, dg.reshape(d)


def _pre_glu(val, gate):
    return val * _sigmoid(gate)


def _pre_id(x):
    return x


def _post_lnsilu(c, g, b):
    return _silu(_ln64(c, g, b))


def _post_silu(c):
    return _silu(c)


def _conv_cfg(kind):
    if kind == "a":
        return dict(k=CONV_A_K, pre=_pre_glu, post=_post_lnsilu, n_in=2, n_par=2, nblk=A_WIDTH // LANES,
                    cols=(COL_AVAL, COL_AGATE))
    return dict(k=CONV_C_K, pre=_pre_id, post=_post_silu, n_in=1, n_par=0, nblk=D_CONV_C // LANES,
                cols=(COL_XBC,))


def _conv_fwd(kind, proj, w, bias, params, seq, name, out_dtype=F32, keep_conv=False):
    cfg = _conv_cfg(kind)
    kt, pre, post, n_in = cfg["k"], cfg["pre"], cfg["post"], cfg["n_in"]
    t = proj.shape[0]
    nseq = t // seq
    c = cfg["nblk"] * LANES
    rt = min(256, seq)
    nrt = seq // rt
    off0 = CONV_PAD - (kt - 1)

    def body(*refs):
        in_refs = refs[:n_in]
        w_ref, b_ref = refs[n_in], refs[n_in + 1]
        par_refs = refs[n_in + 2:n_in + 2 + cfg["n_par"]]
        out_refs = refs[n_in + 2 + cfg["n_par"]:-1]
        hpad = refs[-1]
        hpad[pl.ds(0, CONV_PAD), :] = jnp.zeros((CONV_PAD, LANES), F32)
        for r in range(nrt):
            hpad[pl.ds(CONV_PAD + r * rt, rt), :] = pre(*[x[pl.ds(r * rt, rt), :] for x in in_refs])
        pars = [p[...] for p in par_refs]
        for r in range(nrt):
            acc = jnp.broadcast_to(b_ref[...], (rt, LANES))
            for k in range(kt):
                acc = acc + w_ref[pl.ds(k, 1), :] * hpad[pl.ds(off0 + k + r * rt, rt), :]
            out_refs[0][pl.ds(r * rt, rt), :] = post(acc, *pars).astype(out_dtype)
            if keep_conv:
                out_refs[1][pl.ds(r * rt, rt), :] = acc

    in_specs = [pl.BlockSpec((seq, LANES), functools.partial(lambda s, j, col: (s, col + j), col=col))
                for col in cfg["cols"]]
    vec = pl.BlockSpec((1, LANES), lambda s, j: (0, j))
    in_specs += [pl.BlockSpec((CONV_PAD, LANES), lambda s, j: (0, j)), vec] + [vec] * cfg["n_par"]
    blk = pl.BlockSpec((seq, LANES), lambda s, j: (s, j))
    res = pl.pallas_call(
        body, name=name, grid=(nseq, cfg["nblk"]),
        in_specs=in_specs, out_specs=[blk, blk] if keep_conv else [blk],
        out_shape=[jax.ShapeDtypeStruct((t, c), out_dtype)] + ([jax.ShapeDtypeStruct((t, c), F32)] if keep_conv else []),
        scratch_shapes=[pltpu.VMEM((seq + CONV_PAD, LANES), F32)],
        compiler_params=_cparams("parallel", "parallel"),
    )(*([proj] * n_in), w, bias, *params)
    return tuple(res) if keep_conv else res[0]


def _conv_bwd(kind, proj, w, bias, params, dy, seq, name, dy_col=0, conv_out=None):
    kept = conv_out is not None
    cfg = _conv_cfg(kind)
    kt, pre, post, n_in, n_par = cfg["k"], cfg["pre"], cfg["post"], cfg["n_in"], cfg["n_par"]
    t = proj.shape[0]
    nseq = t // seq
    c = cfg["nblk"] * LANES
    rt = min(256, seq)
    nrt = seq // rt
    off0 = CONV_PAD - (kt - 1)

    def body(*refs):
        in_refs = refs[:n_in]
        w_ref, b_ref = refs[n_in], refs[n_in + 1]
        par_refs = refs[n_in + 2:n_in + 2 + n_par]
        pos = n_in + 2 + n_par
        dy_ref = refs[pos]
        if kept:
            pos += 1
            conv_ref = refs[pos]
        din_refs = refs[pos + 1:pos + 1 + n_in]
        dw_ref, db_ref = refs[pos + 1 + n_in], refs[pos + 2 + n_in]
        dpar_refs = refs[pos + 3 + n_in:pos + 3 + n_in + n_par]
        hpad, dcpad = refs[pos + 3 + n_in + n_par:]

        @pl.when(pl.program_id(1) == 0)
        def _():
            dw_ref[...] = jnp.zeros_like(dw_ref)
            db_ref[...] = jnp.zeros_like(db_ref)
            for r in dpar_refs:
                r[...] = jnp.zeros_like(r)

        hpad[pl.ds(0, CONV_PAD), :] = jnp.zeros((CONV_PAD, LANES), F32)
        dcpad[pl.ds(seq, CONV_PAD), :] = jnp.zeros((CONV_PAD, LANES), F32)
        for r in range(nrt):
            hpad[pl.ds(CONV_PAD + r * rt, rt), :] = pre(*[x[pl.ds(r * rt, rt), :] for x in in_refs])
        pars = [p[...] for p in par_refs]
        for r in range(nrt):
            if kept:
                acc = conv_ref[pl.ds(r * rt, rt), :]
            else:
                acc = jnp.broadcast_to(b_ref[...], (rt, LANES))
                for k in range(kt):
                    acc = acc + w_ref[pl.ds(k, 1), :] * hpad[pl.ds(off0 + k + r * rt, rt), :]
            _, vjp = jax.vjp(post, acc, *pars)
            grads = vjp(dy_ref[pl.ds(r * rt, rt), :])
            dcpad[pl.ds(r * rt, rt), :] = grads[0]
            db_ref[...] += jnp.sum(grads[0], axis=0, keepdims=True)
            for ref, gpar in zip(dpar_refs, grads[1:]):
                ref[...] += gpar
        for r in range(nrt):
            dh = jnp.zeros((rt, LANES), F32)
            for k in range(kt):
                dh = dh + w_ref[pl.ds(k, 1), :] * dcpad[pl.ds(r * rt + kt - 1 - k, rt), :]
            _, vjp = jax.vjp(pre, *[x[pl.ds(r * rt, rt), :] for x in in_refs])
            for ref, gin in zip(din_refs, vjp(dh)):
                ref[pl.ds(r * rt, rt), :] = gin.astype(ref.dtype)
        for k in range(kt):
            s = jnp.zeros((1, LANES), F32)
            for r in range(nrt):
                s = s + jnp.sum(dcpad[pl.ds(r * rt, rt), :] * hpad[pl.ds(off0 + k + r * rt, rt), :],
                                axis=0, keepdims=True)
            dw_ref[pl.ds(k, 1), :] += s

    in_specs = [pl.BlockSpec((seq, LANES), functools.partial(lambda j, s, col: (s, col + j), col=col))
                for col in cfg["cols"]]
    vec = pl.BlockSpec((1, LANES), lambda j, s: (0, j))
    wspec = pl.BlockSpec((CONV_PAD, LANES), lambda j, s: (0, j))
    blk = pl.BlockSpec((seq, LANES), lambda j, s: (s, j))
    in_specs += [wspec, vec] + [vec] * n_par + [pl.BlockSpec((seq, LANES), lambda j, s: (s, dy_col + j))]
    in_specs += [blk] if kept else []
    out_specs = [blk] * n_in + [wspec, vec] + [vec] * n_par
    out_shape = ([jax.ShapeDtypeStruct((t, c), BF16)] * n_in
                 + [jax.ShapeDtypeStruct((CONV_PAD, c), F32), jax.ShapeDtypeStruct((1, c), F32)]
                 + [jax.ShapeDtypeStruct((1, c), F32)] * n_par)
    res = pl.pallas_call(
        body, name=name, grid=(cfg["nblk"], nseq),
        in_specs=in_specs, out_specs=out_specs, out_shape=out_shape,
        scratch_shapes=[pltpu.VMEM((seq + CONV_PAD, LANES), F32), pltpu.VMEM((seq + CONV_PAD, LANES), F32)],
        compiler_params=_cparams("parallel", "arbitrary"),
    )(*([proj] * n_in), w, bias, *params, dy, *([conv_out] if kept else []))
    return res[:n_in], res[n_in], res[n_in + 1], res[n_in + 2:]


def _gmlp_chunk(bu, bv, g, b, w0, w1, b0row, b1row):
    u = _gelu(bu)
    vn = _ln64(_gelu(bv), g, b)
    tri = _iota2((CHUNK, CHUNK), 0) >= _iota2((CHUNK, CHUNK), 1)
    m0 = _nn(jnp.where(tri, w0, 0.0), vn) + jnp.broadcast_to(b0row, (CHUNK, CHUNK)).T
    m1 = _nn(jnp.where(tri, w1, 0.0), vn) + jnp.broadcast_to(b1row, (CHUNK, CHUNK)).T
    return u * jnp.where(_lane_lt64((CHUNK, LANES)), m0, m1)


def _gmlp_specs(tm, order):
    def im(f):
        return lambda *ids: f(*order(*ids))
    return dict(
        bu=pl.BlockSpec((tm, LANES), im(lambda j, r: (r, COL_BU + j))),
        bv=pl.BlockSpec((tm, LANES), im(lambda j, r: (r, COL_BV + j))),
        vec=pl.BlockSpec((1, LANES), im(lambda j, r: (0, j))),
        ws=pl.BlockSpec((2, CHUNK, CHUNK), im(lambda j, r: (j, 0, 0))),
        bs=pl.BlockSpec((None, 2, CHUNK), im(lambda j, r: (j, 0, 0))),
        blk=pl.BlockSpec((tm, LANES), im(lambda j, r: (r, j))),
    )


def _gmlp_fwd(proj, ln_g, ln_b, w_s, b_s, name):
    t = proj.shape[0]
    tm = _row_tile(t)
    nch = tm // CHUNK
    sp = _gmlp_specs(tm, lambda r, j: (j, r))

    def body(bu_ref, bv_ref, g_ref, b_ref, ws_ref, bs_ref, o_ref):
        for ci in range(nch):
            rows = pl.ds(ci * CHUNK, CHUNK)
            o_ref[rows, :] = _gmlp_chunk(bu_ref[rows, :], bv_ref[rows, :], g_ref[...], b_ref[...], ws_ref[0], ws_ref[1],
                                         bs_ref[pl.ds(0, 1), :], bs_ref[pl.ds(1, 1), :]).astype(BF16)

    return pl.pallas_call(
        body, name=name, grid=(t // tm, B_WIDTH // LANES),
        in_specs=[sp["bu"], sp["bv"], sp["vec"], sp["vec"], sp["ws"], sp["bs"]],
        out_specs=sp["blk"], out_shape=jax.ShapeDtypeStruct((t, B_WIDTH), BF16),
        compiler_params=_cparams("parallel", "parallel"),
    )(proj, proj, ln_g, ln_b, w_s, b_s.reshape(B_WIDTH // LANES, 2, CHUNK))


def _gmlp_bwd(proj, ln_g, ln_b, w_s, b_s, dy, name, dy_col=0):
    t = proj.shape[0]
    tm = _row_tile(t)
    nch = tm // CHUNK
    sp = _gmlp_specs(tm, lambda j, r: (j, r))
    dy_spec = pl.BlockSpec((tm, LANES), lambda j, r: (r, dy_col + j))

    def body(bu_ref, bv_ref, g_ref, b_ref, ws_ref, bs_ref, dy_ref, dbu_ref, dbv_ref, dg_ref, db_ref, dws_ref, dbs_ref):
        @pl.when(pl.program_id(1) == 0)
        def _():
            for r in (dg_ref, db_ref, dws_ref, dbs_ref):
                r[...] = jnp.zeros_like(r)

        for ci in range(nch):
            rows = pl.ds(ci * CHUNK, CHUNK)
            _, vjp = jax.vjp(_gmlp_chunk, bu_ref[rows, :], bv_ref[rows, :], g_ref[...], b_ref[...],
                             ws_ref[0], ws_ref[1], bs_ref[pl.ds(0, 1), :], bs_ref[pl.ds(1, 1), :])
            dbu, dbv, dg, db, dw0, dw1, db0, db1 = vjp(dy_ref[rows, :])
            dbu_ref[rows, :] = dbu.astype(BF16)
            dbv_ref[rows, :] = dbv.astype(BF16)
            dg_ref[...] += dg
            db_ref[...] += db
            dws_ref[0] += dw0
            dws_ref[1] += dw1
            dbs_ref[pl.ds(0, 1), :] += db0
            dbs_ref[pl.ds(1, 1), :] += db1

    nh = B_WIDTH // LANES
    res = pl.pallas_call(
        body, name=name, grid=(nh, t // tm),
        in_specs=[sp["bu"], sp["bv"], sp["vec"], sp["vec"], sp["ws"], sp["bs"], dy_spec],
        out_specs=[sp["blk"], sp["blk"], sp["vec"], sp["vec"], sp["ws"], sp["bs"]],
        out_shape=[jax.ShapeDtypeStruct((t, B_WIDTH), BF16), jax.ShapeDtypeStruct((t, B_WIDTH), BF16),
                   jax.ShapeDtypeStruct((1, B_WIDTH), F32), jax.ShapeDtypeStruct((1, B_WIDTH), F32),
                   jax.ShapeDtypeStruct(w_s.shape, F32), jax.ShapeDtypeStruct((nh, 2, CHUNK), F32)],
        compiler_params=_cparams("parallel", "arbitrary"),
    )(proj, proj, ln_g, ln_b, w_s, b_s.reshape(nh, 2, CHUNK), dy)
    dbu, dbv, dg, db, dws, dbs = res
    return dbu, dbv, dg, db, dws, dbs.reshape(b_s.shape)


def _tri_apply(a, lower):
    l = a.shape[0]
    r, c = _iota2((l, l), 0), _iota2((l, l), 1)
    t = jnp.where((r >= c) if lower else (r <= c), 1.0, 0.0).astype(BF16)
    hi = a.astype(BF16)
    r1 = a - hi.astype(F32)
    mid = r1.astype(BF16)
    lo = (r1 - mid.astype(F32)).astype(BF16)
    dn = _DN["nn"]
    return (lax.dot_general(t, hi, dn, preferred_element_type=F32) + lax.dot_general(t, mid, dn, preferred_element_type=F32)
            + lax.dot_general(t, lo, dn, preferred_element_type=F32))


@jax.custom_vjp
def _cumsum_rows(a):
    return _tri_apply(a, True)


_cumsum_rows.defvjp(lambda a: (_tri_apply(a, True), None), lambda _, g: (_tri_apply(g, False),))

SSD_GROUP_HEADS = 8
SSD_GROUP_PAIRS = 4


def _ssd_group(x0, x1, x2, x3, dt_raw, bias, alog, bm, cm, p0, p1, p2, p3):
    xs, prevs = (x0, x1, x2, x3), (p0, p1, p2, p3)
    dt = _softplus(dt_raw + bias)
    a = dt * (-jnp.exp(alog))
    acs = _cumsum_rows(a)
    alast = jnp.sum(a, axis=0, keepdims=True)
    dt_t, acs_t = dt.T, acs.T
    cb = _nt(cm, bm)
    tri = _iota2((CHUNK, CHUNK), 0) >= _iota2((CHUNK, CHUNK), 1)
    lane = _iota2((CHUNK, LANES), 1)
    sub = _iota2((LANES, CHUNK), 0)
    lane1 = _iota2((1, LANES), 1)

    def column(v, i):
        return jnp.broadcast_to(jnp.sum(jnp.where(lane == i, v, 0.0), axis=1, keepdims=True), (CHUNK, LANES))

    def row(vt, i):
        return jnp.broadcast_to(jnp.sum(jnp.where(sub == i, vt, 0.0), axis=0, keepdims=True), (CHUNK, CHUNK))

    heads = []
    for i in range(SSD_GROUP_HEADS):
        col_a = column(acs, i)
        al = jnp.sum(jnp.where(lane1 == i, alast, 0.0), axis=1, keepdims=True)
        m = cb * jnp.exp(jnp.where(tri, col_a - row(acs_t, i), NEG)) * row(dt_t, i)
        heads.append((m, jnp.exp(col_a), column(dt, i) * jnp.exp(al - col_a), jnp.exp(al)))
    lo_lanes = _lane_lt64((CHUNK, LANES))
    lo_rows = _iota2((LANES, SSM_STATE), 0) < HEAD_DIM
    ys, news = [], []
    for j in range(SSD_GROUP_PAIRS):
        (m0, ea0, w0, cd0), (m1, ea1, w1, cd1) = heads[2 * j], heads[2 * j + 1]
        x, prev = xs[j], prevs[j]
        ydiag = jnp.where(lo_lanes, _nn(m0, x), _nn(m1, x))
        yoff = jnp.where(lo_lanes, _nt(cm * ea0, prev), _nt(cm * ea1, prev))
        states = jnp.where(lo_rows, _tn(x, bm * w0), _tn(x, bm * w1))
        ys.append(ydiag + yoff)
        news.append(prev * jnp.where(lo_rows, cd0, cd1) + states)
    return tuple(ys) + tuple(news)


def _ssd2_specs(seq, rev):
    ncs = seq // CHUNK
    gw = SSD_GROUP_PAIRS * LANES
    nblk_x = C_WIDTH // LANES

    def row(s, c):
        return s * ncs + (ncs - 1 - c if rev else c)

    return dict(
        x=pl.BlockSpec((CHUNK, gw), lambda g, s, c: (row(s, c), g)),
        dt=pl.BlockSpec((CHUNK, LANES), lambda g, s, c: (row(s, c), g)),
        vec=pl.BlockSpec((1, LANES), lambda g, s, c: (0, g)),
        bm=pl.BlockSpec((CHUNK, SSM_STATE), lambda g, s, c: (row(s, c), nblk_x + g)),
        cm=pl.BlockSpec((CHUNK, SSM_STATE), lambda g, s, c: (row(s, c), nblk_x + 2 + g)),
        st=pl.BlockSpec((None, SSD_GROUP_PAIRS, LANES, SSM_STATE), lambda g, s, c: (row(s, c), g, 0, 0)),
        ncs=ncs,
    )


def _lane_blocks(ref):
    return [ref[:, pl.ds(j * LANES, LANES)] for j in range(SSD_GROUP_PAIRS)]


def _ssd2_fwd(xbc_act, dt_raw, dt_bias, a_log, seq, name):
    t = xbc_act.shape[0]
    sp = _ssd2_specs(seq, False)

    def body(x_ref, dt_ref, bias_ref, alog_ref, bm_ref, cm_ref, y_ref, prev_ref, state):
        @pl.when(pl.program_id(2) == 0)
        def _():
            state[...] = jnp.zeros_like(state)

        prevs = [state[j] for j in range(SSD_GROUP_PAIRS)]
        for j in range(SSD_GROUP_PAIRS):
            prev_ref[j] = prevs[j]
        res = _ssd_group(*_lane_blocks(x_ref), dt_ref[...], bias_ref[...], alog_ref[...], bm_ref[...], cm_ref[...], *prevs)
        for j in range(SSD_GROUP_PAIRS):
            y_ref[:, pl.ds(j * LANES, LANES)] = res[j]
            state[j] = res[SSD_GROUP_PAIRS + j]

    return pl.pallas_call(
        body, name=name, grid=(2, t // seq, sp["ncs"]),
        in_specs=[sp["x"], sp["dt"], sp["vec"], sp["vec"], sp["bm"], sp["cm"]],
        out_specs=[sp["x"], sp["st"]],
        out_shape=[jax.ShapeDtypeStruct((t, C_WIDTH), F32),
                   jax.ShapeDtypeStruct((t // CHUNK, C_WIDTH // LANES, LANES, SSM_STATE), F32)],
        scratch_shapes=[pltpu.VMEM((SSD_GROUP_PAIRS, LANES, SSM_STATE), F32)],
        compiler_params=_cparams("parallel", "parallel", "arbitrary"),
    )(xbc_act, dt_raw, dt_bias, a_log, xbc_act, xbc_act)


def _ssd2_bwd(xbc_act, dt_raw, dt_bias, a_log, prev_saved, dy, seq, name):
    t = xbc_act.shape[0]
    sp = _ssd2_specs(seq, True)
    npair = SSD_GROUP_PAIRS

    def body(x_ref, dt_ref, bias_ref, alog_ref, bm_ref, cm_ref, prev_ref, dy_ref,
             dx_ref, ddt_ref, dbias_ref, dalog_ref, dbm_ref, dcm_ref, dstate):
        @pl.when(pl.program_id(2) == 0)
        def _():
            dstate[...] = jnp.zeros_like(dstate)

        @pl.when(jnp.logical_and(pl.program_id(1) == 0, pl.program_id(2) == 0))
        def _():
            dbias_ref[...] = jnp.zeros_like(dbias_ref)
            dalog_ref[...] = jnp.zeros_like(dalog_ref)

        _, vjp = jax.vjp(_ssd_group, *_lane_blocks(x_ref), dt_ref[...], bias_ref[...], alog_ref[...], bm_ref[...],
                         cm_ref[...], *[prev_ref[j] for j in range(npair)])
        grads = vjp(tuple(_lane_blocks(dy_ref)) + tuple(dstate[j] for j in range(npair)))
        for j in range(npair):
            dx_ref[:, pl.ds(j * LANES, LANES)] = grads[j]
            dstate[j] = grads[npair + 5 + j]
        ddt_ref[...] = grads[npair].astype(BF16)
        dbias_ref[...] += grads[npair + 1]
        dalog_ref[...] += grads[npair + 2]
        dbm_ref[...] = grads[npair + 3]
        dcm_ref[...] = grads[npair + 4]

    return pl.pallas_call(
        body, name=name, grid=(2, t // seq, sp["ncs"]),
        in_specs=[sp["x"], sp["dt"], sp["vec"], sp["vec"], sp["bm"], sp["cm"], sp["st"], sp["x"]],
        out_specs=[sp["x"], sp["dt"], sp["vec"], sp["vec"], sp["dt"], sp["dt"]],
        out_shape=[jax.ShapeDtypeStruct((t, C_WIDTH), F32), jax.ShapeDtypeStruct((t, 2 * LANES), BF16),
                   jax.ShapeDtypeStruct((1, 2 * LANES), F32), jax.ShapeDtypeStruct((1, 2 * LANES), F32),
                   jax.ShapeDtypeStruct((t, 2 * SSM_STATE), F32), jax.ShapeDtypeStruct((t, 2 * SSM_STATE), F32)],
        scratch_shapes=[pltpu.VMEM((npair, LANES, SSM_STATE), F32)],
        compiler_params=_cparams("parallel", "arbitrary", "arbitrary"),
    )(xbc_act, dt_raw, dt_bias, a_log, xbc_act, xbc_act, prev_saved, dy)


def _ssd2_assemble(dxs_ssd, dxs_skip, dbm, dcm, name):
    t = dxs_ssd.shape[0]
    tm = _row_tile(t)

    def body(a_ref, b_ref, dbm_ref, dcm_ref, o_ref):
        o_ref[:, pl.ds(0, C_WIDTH)] = a_ref[...] + b_ref[...]
        o_ref[:, pl.ds(C_WIDTH, 2 * SSM_STATE)] = dbm_ref[...]
        o_ref[:, pl.ds(C_WIDTH + 2 * SSM_STATE, 2 * SSM_STATE)] = dcm_ref[...]

    wide = pl.BlockSpec((tm, C_WIDTH), lambda i: (i, 0))
    narrow = pl.BlockSpec((tm, 2 * SSM_STATE), lambda i: (i, 0))
    return pl.pallas_call(
        body, name=name, grid=(t // tm,), in_specs=[wide, wide, narrow, narrow],
        out_specs=pl.BlockSpec((tm, D_CONV_C), lambda i: (i, 0)),
        out_shape=jax.ShapeDtypeStruct((t, D_CONV_C), F32),
        compiler_params=_cparams("parallel"),
    )(dxs_ssd, dxs_skip, dbm, dcm)


def _expand_mats():
    head = jnp.arange(LANES)[:, None]
    e64 = (head == (jnp.arange(C_WIDTH)[None, :] // HEAD_DIM)).astype(F32)
    e128 = (head == (jnp.arange(C_HEADS * LANES)[None, :] // LANES)).astype(F32)
    return e64, e128


def _ssd_prep_fn(dt_raw, dt_bias, a_log, e64, e128):
    dt = _softplus(dt_raw + dt_bias)
    a = dt * (-jnp.exp(a_log))
    incl = (_iota2((CHUNK, CHUNK), 0) >= _iota2((CHUNK, CHUNK), 1)).astype(F32)
    acs = _xdot(incl, a)
    alast = _xdot(jnp.ones((CHUNK, CHUNK), F32), a)
    return _xdot(dt, e64), _xdot(acs, e64), _xdot(alast, e64), _xdot(acs, e128)


def _ssd_prep_specs():
    blk = lambda w: pl.BlockSpec((CHUNK, w), lambda i: (i, 0))
    const = lambda r, w: pl.BlockSpec((r, w), lambda i: (0, 0))
    ins = [blk(LANES), const(1, LANES), const(1, LANES), const(LANES, C_WIDTH), const(LANES, C_HEADS * LANES)]
    outs = [blk(C_WIDTH), blk(C_WIDTH), blk(C_WIDTH), blk(C_HEADS * LANES)]
    return ins, outs


def _ssd_prep_fwd(dt_raw, dt_bias, a_log, name):
    t = dt_raw.shape[0]
    e64, e128 = _expand_mats()
    ins, outs = _ssd_prep_specs()

    def body(raw_ref, bias_ref, alog_ref, e64_ref, e128_ref, dt_ref, acs_ref, alast_ref, acs128_ref):
        res = _ssd_prep_fn(raw_ref[...], bias_ref[...], alog_ref[...], e64_ref[...], e128_ref[...])
        for ref, v in zip((dt_ref, acs_ref, alast_ref, acs128_ref), res):
            ref[...] = v

    return pl.pallas_call(
        body, name=name, grid=(t // CHUNK,), in_specs=ins, out_specs=outs,
        out_shape=[jax.ShapeDtypeStruct((t, C_WIDTH), F32)] * 3 + [jax.ShapeDtypeStruct((t, C_HEADS * LANES), F32)],
        compiler_params=_cparams("parallel"),
    )(dt_raw, dt_bias, a_log, e64, e128)


def _ssd_prep_bwd(dt_raw, dt_bias, a_log, d_dt, d_acs, d_alast, d_acs128, name):
    t = dt_raw.shape[0]
    e64, e128 = _expand_mats()
    ins, outs = _ssd_prep_specs()
    vec = pl.BlockSpec((1, LANES), lambda i: (0, 0))

    def body(raw_ref, bias_ref, alog_ref, e64_ref, e128_ref, g0, g1, g2, g3, draw_ref, dbias_ref, dalog_ref):
        @pl.when(pl.program_id(0) == 0)
        def _():
            dbias_ref[...] = jnp.zeros_like(dbias_ref)
            dalog_ref[...] = jnp.zeros_like(dalog_ref)

        e64v, e128v = e64_ref[...], e128_ref[...]
        _, vjp = jax.vjp(lambda r, b, al: _ssd_prep_fn(r, b, al, e64v, e128v),
                         raw_ref[...], bias_ref[...], alog_ref[...])
        draw, dbias, dalog = vjp((g0[...], g1[...], g2[...], g3[...]))
        draw_ref[...] = draw.astype(BF16)
        dbias_ref[...] += dbias
        dalog_ref[...] += dalog

    return pl.pallas_call(
        body, name=name, grid=(t // CHUNK,), in_specs=ins + outs,
        out_specs=[pl.BlockSpec((CHUNK, LANES), lambda i: (i, 0)), vec, vec],
        out_shape=[jax.ShapeDtypeStruct((t, LANES), BF16), jax.ShapeDtypeStruct((1, LANES), F32),
                   jax.ShapeDtypeStruct((1, LANES), F32)],
        compiler_params=_cparams("arbitrary"),
    )(dt_raw, dt_bias, a_log, e64, e128, d_dt, d_acs, d_alast, d_acs128)


def _ssd_chunk(x, dt, acs, alast, col0, col1, bm, cm, prev):
    xdt = x * dt
    cb = _nt(cm, bm)
    tri = _iota2((CHUNK, CHUNK), 0) >= _iota2((CHUNK, CHUNK), 1)
    l0 = jnp.exp(jnp.where(tri, col0 - col0.T, NEG))
    l1 = jnp.exp(jnp.where(tri, col1 - col1.T, NEG))
    ydiag = jnp.where(_lane_lt64((CHUNK, LANES)), _nn(cb * l0, xdt), _nn(cb * l1, xdt))
    states = _tn(xdt * jnp.exp(alast - acs), bm)
    yoff = _nt(cm, prev) * jnp.exp(acs)
    new = prev * jnp.exp(alast).T + states
    return ydiag + yoff, new


def _ssd_specs(seq, rev):
    ncs = seq // CHUNK
    npair = C_WIDTH // LANES

    def row(s, c):
        return s * ncs + (ncs - 1 - c if rev else c)

    return dict(
        x=pl.BlockSpec((CHUNK, LANES), lambda s, j, c: (row(s, c), j)),
        bm=pl.BlockSpec((CHUNK, SSM_STATE), lambda s, j, c: (row(s, c), C_WIDTH // LANES + j // 4)),
        cm=pl.BlockSpec((CHUNK, SSM_STATE), lambda s, j, c: (row(s, c), C_WIDTH // LANES + 2 + j // 4)),
        col=pl.BlockSpec((CHUNK, 2 * LANES), lambda s, j, c: (row(s, c), j)),
        st=pl.BlockSpec((None, None, LANES, SSM_STATE), lambda s, j, c: (row(s, c), j, 0, 0)),
        npair=npair, ncs=ncs,
    )


def _ssd_fwd(xbc_act, dt64, acs64, alast64, acs128, seq, name):
    t = xbc_act.shape[0]
    sp = _ssd_specs(seq, False)

    def body(x_ref, dt_ref, acs_ref, alast_ref, col_ref, bm_ref, cm_ref, y_ref, prev_ref, state):
        @pl.when(pl.program_id(2) == 0)
        def _():
            state[...] = jnp.zeros_like(state)

        prev = state[...]
        prev_ref[...] = prev
        y, new = _ssd_chunk(x_ref[...], dt_ref[...], acs_ref[...], alast_ref[...],
                            col_ref[:, pl.ds(0, LANES)], col_ref[:, pl.ds(LANES, LANES)],
                            bm_ref[...], cm_ref[...], prev)
        y_ref[...] = y
        state[...] = new

    return pl.pallas_call(
        body, name=name, grid=(t // seq, sp["npair"], sp["ncs"]),
        in_specs=[sp["x"], sp["x"], sp["x"], sp["x"], sp["col"], sp["bm"], sp["cm"]],
        out_specs=[sp["x"], sp["st"]],
        out_shape=[jax.ShapeDtypeStruct((t, C_WIDTH), F32),
                   jax.ShapeDtypeStruct((t // CHUNK, sp["npair"], LANES, SSM_STATE), F32)],
        scratch_shapes=[pltpu.VMEM((LANES, SSM_STATE), F32)],
        compiler_params=_cparams("parallel", "parallel", "arbitrary"),
    )(xbc_act, dt64, acs64, alast64, acs128, xbc_act, xbc_act)


def _ssd_bwd(xbc_act, dt64, acs64, alast64, acs128, prev_saved, dy, seq, name):
    t = xbc_act.shape[0]
    sp = _ssd_specs(seq, True)

    def body(x_ref, dt_ref, acs_ref, alast_ref, col_ref, bm_ref, cm_ref, prev_ref, dy_ref,
             dx_ref, ddt_ref, dacs_ref, dalast_ref, dcol_ref, dbc_ref, dstate):
        @pl.when(pl.program_id(2) == 0)
        def _():
            dstate[...] = jnp.zeros_like(dstate)

        _, vjp = jax.vjp(_ssd_chunk, x_ref[...], dt_ref[...], acs_ref[...], alast_ref[...],
                         col_ref[:, pl.ds(0, LANES)], col_ref[:, pl.ds(LANES, LANES)],
                         bm_ref[...], cm_ref[...], prev_ref[...])
        dx, ddt, dacs, dalast, dc0, dc1, dbm, dcm, dprev = vjp((dy_ref[...], dstate[...]))
        dx_ref[...] = dx
        ddt_ref[...] = ddt
        dacs_ref[...] = dacs
        dalast_ref[...] = dalast
        dcol_ref[:, pl.ds(0, LANES)] = dc0
        dcol_ref[:, pl.ds(LANES, LANES)] = dc1
        dbc_ref[:, pl.ds(0, SSM_STATE)] = dbm
        dbc_ref[:, pl.ds(SSM_STATE, SSM_STATE)] = dcm
        dstate[...] = dprev

    wide = jax.ShapeDtypeStruct((t, C_WIDTH), F32)
    return pl.pallas_call(
        body, name=name, grid=(t // seq, sp["npair"], sp["ncs"]),
        in_specs=[sp["x"], sp["x"], sp["x"], sp["x"], sp["col"], sp["bm"], sp["cm"], sp["st"], sp["x"]],
        out_specs=[sp["x"], sp["x"], sp["x"], sp["x"], sp["col"], sp["col"]],
        out_shape=[wide, wide, wide, wide, jax.ShapeDtypeStruct((t, 2 * C_WIDTH), F32),
                   jax.ShapeDtypeStruct((t, 2 * C_WIDTH), F32)],
        scratch_shapes=[pltpu.VMEM((LANES, SSM_STATE), F32)],
        compiler_params=_cparams("parallel", "parallel", "arbitrary"),
    )(xbc_act, dt64, acs64, alast64, acs128, xbc_act, xbc_act, prev_saved, dy)


def _ssd_post_fn(y, xs, z, dskip, g):
    v = (y + dskip * xs) * _silu(z)
    return v * lax.rsqrt(jnp.mean(v * v, axis=-1, keepdims=True) + EPS) * g


def _ssd_post_specs(tm, order):
    gw = C_WIDTH // 2

    def im(f):
        return lambda *ids: f(*order(*ids))
    return dict(
        blk=pl.BlockSpec((tm, gw), im(lambda g, r: (r, g))),
        z=pl.BlockSpec((tm, gw), im(lambda g, r: (r, COL_Z * LANES // gw + g))),
        vec=pl.BlockSpec((1, gw), im(lambda g, r: (0, g))),
    )


def _ssd_post_fwd(y_ssd, xbc_act, proj, dskip64, norm_g, name):
    t = y_ssd.shape[0]
    tm = _row_tile(t)
    sp = _ssd_post_specs(tm, lambda r, g: (g, r))

    def body(y_ref, xs_ref, z_ref, ds_ref, g_ref, o_ref):
        o_ref[...] = _ssd_post_fn(y_ref[...], xs_ref[...], z_ref[...], ds_ref[...], g_ref[...]).astype(BF16)

    return pl.pallas_call(
        body, name=name, grid=(t // tm, 2),
        in_specs=[sp["blk"], sp["blk"], sp["z"], sp["vec"], sp["vec"]], out_specs=sp["blk"],
        out_shape=jax.ShapeDtypeStruct((t, C_WIDTH), BF16),
        compiler_params=_cparams("parallel", "parallel"),
    )(y_ssd, xbc_act, proj, dskip64, norm_g)


def _ssd_post_bwd(y_ssd, xbc_act, proj, dskip64, norm_g, dyc, name, dy_col=0):
    t = y_ssd.shape[0]
    tm = _row_tile(t)
    sp = _ssd_post_specs(tm, lambda g, r: (g, r))
    dy_spec = pl.BlockSpec((tm, C_WIDTH // 2), lambda g, r: (r, dy_col + g))

    def body(y_ref, xs_ref, z_ref, ds_ref, g_ref, dyc_ref, dy_ref, dxs_ref, dz_ref, dds_ref, dg_ref):
        @pl.when(pl.program_id(1) == 0)
        def _():
            dds_ref[...] = jnp.zeros_like(dds_ref)
            dg_ref[...] = jnp.zeros_like(dg_ref)

        _, vjp = jax.vjp(_ssd_post_fn, y_ref[...], xs_ref[...], z_ref[...], ds_ref[...], g_ref[...])
        dy, dxs, dz, dds, dg = vjp(dyc_ref[...])
        dy_ref[...] = dy
        dxs_ref[...] = dxs
        dz_ref[...] = dz.astype(BF16)
        dds_ref[...] += dds
        dg_ref[...] += dg

    wide = jax.ShapeDtypeStruct((t, C_WIDTH), F32)
    vec = jax.ShapeDtypeStruct((1, C_WIDTH), F32)
    return pl.pallas_call(
        body, name=name, grid=(2, t // tm),
        in_specs=[sp["blk"], sp["blk"], sp["z"], sp["vec"], sp["vec"], dy_spec],
        out_specs=[sp["blk"], sp["blk"], sp["blk"], sp["vec"], sp["vec"]],
        out_shape=[wide, wide, jax.ShapeDtypeStruct((t, C_WIDTH), BF16), vec, vec],
        compiler_params=_cparams("parallel", "arbitrary"),
    )(y_ssd, xbc_act, proj, dskip64, norm_g, dyc)


def _ssd_assemble(dxs_ssd, dxs_skip, dbc, name):
    t = dxs_ssd.shape[0]
    tm = _row_tile(t)
    npair = C_WIDTH // LANES

    def body(a_ref, b_ref, dbc_ref, o_ref):
        o_ref[:, pl.ds(0, C_WIDTH)] = a_ref[...] + b_ref[...]
        for grp in range(2):
            for which in range(2):
                acc = jnp.zeros((tm, SSM_STATE), F32)
                for j in range(grp * npair // 2, (grp + 1) * npair // 2):
                    acc = acc + dbc_ref[:, pl.ds((2 * j + which) * SSM_STATE, SSM_STATE)]
                o_ref[:, pl.ds(C_WIDTH + (2 * which + grp) * SSM_STATE, SSM_STATE)] = acc

    return pl.pallas_call(
        body, name=name, grid=(t // tm,),
        in_specs=[pl.BlockSpec((tm, C_WIDTH), lambda i: (i, 0))] * 2 + [pl.BlockSpec((tm, 2 * C_WIDTH), lambda i: (i, 0))],
        out_specs=pl.BlockSpec((tm, D_CONV_C), lambda i: (i, 0)),
        out_shape=jax.ShapeDtypeStruct((t, D_CONV_C), F32),
        compiler_params=_cparams("parallel"),
    )(dxs_ssd, dxs_skip, dbc)


def _pad_taps(w):
    return jnp.pad(w, ((0, CONV_PAD - w.shape[0]), (0, 0)))


def _pad_heads(v):
    return jnp.pad(v, (0, LANES - v.shape[0])).reshape(1, LANES)


def _group_heads(a):
    pad = [(0, 0)] * (a.ndim - 1) + [(0, LANES - SSD_GROUP_HEADS)]
    return jnp.concatenate([jnp.pad(a[..., :SSD_GROUP_HEADS], pad), jnp.pad(a[..., SSD_GROUP_HEADS:], pad)], axis=-1)


def _ungroup_heads(a):
    return jnp.concatenate([a[..., :SSD_GROUP_HEADS], a[..., LANES:LANES + SSD_GROUP_HEADS]], axis=-1)


def _layer_fwd(x, p, seq, li, after=()):
    n = f"l{li}_"
    h1 = _rms_fwd(x, p["norm1_g"], n + "rms1", after=after)
    proj = _matmul(h1, p["w_main"], mode="nn", name=n + "inproj")
    dt_raw = _matmul(h1, p["w_dt"], mode="nn", name=n + "inproj_dt")
    row = lambda v: v.reshape(1, -1)
    ya, conv_a = _conv_fwd("a", proj, _pad_taps(p["conv_a_w"]), row(p["conv_a_b"]), (row(p["ln_a_g"]), row(p["ln_a_b"])),
                           seq, n + "conva", out_dtype=BF16, keep_conv=True)
    yb = _gmlp_fwd(proj, row(p["ln_b_g"]), row(p["ln_b_b"]), p["w_spatial"], p["b_spatial"], n + "gmlp")
    xbc_act = _conv_fwd("c", proj, _pad_taps(p["conv_c_w"]), row(p["conv_c_b"]), (), seq, n + "convc")
    y_ssd, prev = _ssd2_fwd(xbc_act, dt_raw, _group_heads(row(p["dt_bias"])), _group_heads(row(p["a_log"])), seq, n + "ssd")
    dskip64 = jnp.repeat(p["d_skip"], HEAD_DIM).reshape(1, C_WIDTH)
    yc = _ssd_post_fwd(y_ssd, xbc_act, proj, dskip64, row(p["norm_c_g"]), n + "ssdpost")
    ycat = jnp.concatenate([ya, yb, yc], axis=1)
    x1 = _matmul(ycat, p["w_out"], mode="nn", name=n + "outproj", add=x)
    h2 = _rms_fwd(x1, p["norm2_g"], n + "rms2")
    u, act = _matmul(h2, p["w_ff1"], mode="nn", name=n + "ff1", epilogue=_relu2_epilogue, out_dtypes=(F32, BF16),
                     b_chips=True)
    x2 = _matmul(act, p["w_ff2"], mode="nn", name=n + "ff2", add=x1)
    saved = dict(x=x, h1=h1, proj=proj, conv_a=conv_a, dt_raw=dt_raw, xbc_act=xbc_act, prev=prev, y_ssd=y_ssd,
                 dskip64=dskip64, ycat=ycat, x1=x1, h2=h2, u=u, act=act)
    return x2, saved


def _layer_bwd(dx2, p, s, seq, li, after=(), on_ffn_grads=None):
    n = f"l{li}_b_"
    row = lambda v: v.reshape(1, -1)
    g = {}
    du = _matmul(dx2, p["w_ff2"], mode="nt", name=n + "ff2_dx", epilogue=_relu2_bwd_epilogue, extra=s["u"],
                 out_dtypes=(BF16,), after=after)
    g["w_ff2"] = _matmul(s["act"], dx2, mode="tn", name=n + "ff2_dw")
    g["w_ff1"] = _matmul(s["h2"], du, mode="tn", name=n + "ff1_dw", out_chips=True)
    dh2 = _matmul(du, p["w_ff1"], mode="nt", name=n + "ff1_dx", b_chips=True)
    dx1, g["norm2_g"] = _rms_bwd(s["x1"], p["norm2_g"], dh2, dx2, n + "rms2")
    g["w_out"] = _matmul(s["ycat"], dx1, mode="tn", name=n + "out_dw")
    dycat = _matmul(dx1, p["w_out"], mode="nt", name=n + "out_dx",
                    after=() if on_ffn_grads is None else on_ffn_grads(g))
    proj = s["proj"]
    (dval, dgate), dwa, dba, (dlag, dlab) = _conv_bwd(
        "a", proj, _pad_taps(p["conv_a_w"]), row(p["conv_a_b"]), (row(p["ln_a_g"]), row(p["ln_a_b"])), dycat, seq,
        n + "conva", dy_col=0, conv_out=s["conv_a"])
    g["conv_a_w"], g["conv_a_b"], g["ln_a_g"], g["ln_a_b"] = dwa[:CONV_A_K], dba[0], dlag[0], dlab[0]
    dbu, dbv, dlbg, dlbb, g["w_spatial"], g["b_spatial"] = _gmlp_bwd(
        proj, row(p["ln_b_g"]), row(p["ln_b_b"]), p["w_spatial"], p["b_spatial"], dycat, n + "gmlp",
        dy_col=A_WIDTH // LANES)
    g["ln_b_g"], g["ln_b_b"] = dlbg[0], dlbb[0]
    dy_ssd, dxs_skip, dz, dds, dncg = _ssd_post_bwd(s["y_ssd"], s["xbc_act"], proj, s["dskip64"], row(p["norm_c_g"]),
                                                    dycat, n + "ssdpost", dy_col=(A_WIDTH + B_WIDTH) * 2 // C_WIDTH)
    g["norm_c_g"] = dncg[0]
    g["d_skip"] = dds.reshape(C_HEADS, HEAD_DIM).sum(axis=1)
    dxs, ddt_raw, ddtb, dalog, dbm, dcm = _ssd2_bwd(
        s["xbc_act"], s["dt_raw"], _group_heads(row(p["dt_bias"])), _group_heads(row(p["a_log"])), s["prev"], dy_ssd, seq,
        n + "ssd")
    g["dt_bias"], g["a_log"] = _ungroup_heads(ddtb)[0], _ungroup_heads(dalog)[0]
    dconv = _ssd2_assemble(dxs, dxs_skip, dbm, dcm, n + "ssdasm")
    (dxbc,), dwc, dbcv, _ = _conv_bwd("c", proj, _pad_taps(p["conv_c_w"]), row(p["conv_c_b"]), (), dconv, seq, n + "convc")
    g["conv_c_w"], g["conv_c_b"] = dwc[:CONV_C_K], dbcv[0]
    dproj = jnp.concatenate([dval, dgate, dbu, dbv, dz, dxbc], axis=1)
    g["w_main"] = _matmul(s["h1"], dproj, mode="tn", name=n + "in_dw")
    g["w_dt"] = _matmul(s["h1"], ddt_raw, mode="tn", name=n + "indt_dw")
    dh1 = _matmul(dproj, p["w_main"], mode="nt", name=n + "in_dx")
    dh1 = _matmul(ddt_raw, p["w_dt"], mode="nt", name=n + "indt_dx", add=dh1)
    dx, g["norm1_g"] = _rms_bwd(s["x"], p["norm1_g"], dh1, dx1, n + "rms1")
    return dx, g


EW_BLOCK_BYTES = 1 << 20


def _ew(fn, ins, out_dtypes, name, leads=None):
    leads = leads or [None] * len(ins)
    rows, c = ins[0].shape[-2:]
    tr = _pick(rows, [t for t in (2048, 1024, 512, 256, 128, 64, 32, 16, 8) if t * c * 4 <= EW_BLOCK_BYTES])
    n_in = len(ins)

    def spec(lead):
        if lead is None:
            return pl.BlockSpec((tr, c), lambda i: (i, 0))
        return pl.BlockSpec((None, tr, c), functools.partial(lambda i, k: (k, i, 0), k=lead))

    def body(*refs):
        outs = fn(*[r[...].astype(F32) for r in refs[:n_in]])
        for o_ref, o in zip(refs[n_in:], outs):
            o_ref[...] = o.astype(o_ref.dtype)

    return pl.pallas_call(
        body, name=name, grid=(rows // tr,),
        in_specs=[spec(l) for l in leads], out_specs=[spec(None)] * len(out_dtypes),
        out_shape=[jax.ShapeDtypeStruct((rows, c), dt) for dt in out_dtypes],
        compiler_params=_cparams("parallel"),
    )(*ins)


def _adam_fn(w, g, m, v):
    m2 = ADAM_B1 * m + (1.0 - ADAM_B1) * g
    v2 = ADAM_B2 * v + (1.0 - ADAM_B2) * (g * g)
    m_hat = m2 / (1.0 - ADAM_B1 ** ADAM_STEP)
    v_hat = v2 / (1.0 - ADAM_B2 ** ADAM_STEP)
    delta = -ADAM_LR * (m_hat / (jnp.sqrt(v_hat) + ADAM_EPS) + ADAM_WD * w)
    return delta, m2, v2


def _adam(w, g, m, v, name):
    shape = w.shape
    two_d = lambda a: a.reshape(-1, shape[-1])
    outs = _ew(_adam_fn, [two_d(w), two_d(g), two_d(m), two_d(v)], (F32, F32, F32), name)
    return [o.reshape(shape) for o in outs]


_ANY = pl.BlockSpec(memory_space=pl.ANY)


def _mesh_pos():
    return lax.axis_index("x"), lax.axis_index("y"), lax.axis_index("c")


def _peer_chips(x, y):
    return [(1 - x, y), (x, 1 - y), (1 - x, 1 - y)]


def _remote(src, dst, send_sems, recv_sems, sem, to):
    return pltpu.make_async_remote_copy(src_ref=src, dst_ref=dst, send_sem=send_sems.at[sem],
                                        recv_sem=recv_sems.at[sem], device_id=to, device_id_type=MESH)


def _half_rows(n_rows, which):
    half = n_rows // 2
    return pl.ds(pl.multiple_of(which * half, 8), half)


def _comm_call(body, ins, out_shapes, n_sems, name):
    scratch = [pltpu.SemaphoreType.DMA((n_sems,)), pltpu.SemaphoreType.DMA((n_sems,))]
    return pl.pallas_call(
        body, name=name, in_specs=[_ANY] * len(ins), out_specs=[_ANY] * len(out_shapes),
        out_shape=out_shapes, scratch_shapes=scratch,
    )(*ins)


def _gather_weights(big, small, name):
    nb, ns = len(big), len(small)
    n = nb + ns

    def body(*refs):
        ins, outs = refs[:n], refs[n:2 * n]
        send_sems, recv_sems = refs[2 * n:]
        x, y, c = _mesh_pos()
        q = 2 * x + y
        me, sib = (x, y, c), (x, y, 1 - c)
        chips = _peer_chips(x, y)
        rem = functools.partial(_remote, send_sems=send_sems, recv_sems=recv_sems)
        first = []
        for i in range(nb):
            mine = _half_rows(big[i].shape[0], c)
            for k, (px, py) in enumerate(chips):
                first.append(rem(ins[i].at[mine], outs[i].at[q, mine], sem=6 * i + k, to=(px, py, c)))
        for j in range(ns):
            for k, (px, py) in enumerate(chips):
                first.append(rem(ins[nb + j], outs[nb + j].at[q], sem=6 * nb + 3 * j + k, to=(px, py, c)))
        for cp in first:
            cp.start()
        passed = []
        for i in range(nb):
            mine = _half_rows(big[i].shape[0], c)
            for k, (px, py) in enumerate(chips):
                landed = outs[i].at[2 * px + py, mine]
                rem(landed, landed, sem=6 * i + k, to=me).wait_recv()
                fwd = rem(landed, landed, sem=6 * i + 3 + k, to=sib)
                fwd.start()
                passed.append(fwd)
        for i in range(nb):
            other = _half_rows(big[i].shape[0], 1 - c)
            for k, (px, py) in enumerate(chips):
                theirs = outs[i].at[2 * px + py, other]
                rem(theirs, theirs, sem=6 * i + 3 + k, to=me).wait_recv()
        for j in range(ns):
            for k, (px, py) in enumerate(chips):
                dst = outs[nb + j].at[2 * px + py]
                rem(dst, dst, sem=6 * nb + 3 * j + k, to=me).wait_recv()
        for cp in first + passed:
            cp.wait_send()

    out_shapes = [jax.ShapeDtypeStruct((N_CHIPS,) + a.shape, a.dtype) for a in list(big) + list(small)]
    return _comm_call(body, list(big) + list(small), out_shapes, 6 * nb + 3 * ns, name)


def _sibling_other_halves(gs, name):
    n = len(gs)

    def other_half(ref, shape, c):
        rows = _half_rows(shape[-2], 1 - c)
        return ref.at[rows] if len(shape) == 2 else ref.at[:, rows]

    def body(*refs):
        ins, outs = refs[:n], refs[n:2 * n]
        send_sems, recv_sems = refs[2 * n:]
        x, y, c = _mesh_pos()
        copies = [_remote(other_half(ins[i], gs[i].shape, c), outs[i], send_sems, recv_sems, i, (x, y, 1 - c))
                  for i in range(n)]
        for cp in copies:
            cp.start()
        for cp in copies:
            cp.wait()

    out_shapes = [jax.ShapeDtypeStruct(g.shape[:-2] + (g.shape[-2] // 2, g.shape[-1]), g.dtype) for g in gs]
    return _comm_call(body, list(gs), out_shapes, n, name)


IN_SHARD = D_IN_PROJ // N_CHIPS


def _chipsum_in(mine, mine_dt, theirs, theirs_dt, name):
    r = mine.shape[0]
    tr = _pick(r, (128, 64, 32, 16, 8))
    last = D_MAIN - (N_CHIPS - 1) * IN_SHARD

    def body(a_ref, adt_ref, b_ref, bdt_ref, o32_ref, o16_ref):
        for p in range(N_CHIPS):
            wid = IN_SHARD if p < N_CHIPS - 1 else last
            s = a_ref[:, pl.ds(IN_SHARD * p, wid)] + b_ref[:, pl.ds(IN_SHARD * p, wid)]
            o32_ref[p, :, pl.ds(0, wid)] = s
            o16_ref[p, :, pl.ds(0, wid)] = s.astype(BF16)
        for grp in range(2):
            src = pl.ds(grp * LANES, SSD_GROUP_HEADS)
            s = adt_ref[:, src] + bdt_ref[:, src]
            dst = pl.ds(last + grp * SSD_GROUP_HEADS, SSD_GROUP_HEADS)
            o32_ref[N_CHIPS - 1, :, dst] = s
            o16_ref[N_CHIPS - 1, :, dst] = s.astype(BF16)

    wide = pl.BlockSpec((tr, D_MAIN), lambda i: (i, 0))
    narrow = pl.BlockSpec((tr, 2 * LANES), lambda i: (i, 0))
    out = pl.BlockSpec((N_CHIPS, tr, IN_SHARD), lambda i: (0, i, 0))
    return pl.pallas_call(
        body, name=name, grid=(r // tr,), in_specs=[wide, narrow, wide, narrow], out_specs=[out, out],
        out_shape=[jax.ShapeDtypeStruct((N_CHIPS, r, IN_SHARD), F32), jax.ShapeDtypeStruct((N_CHIPS, r, IN_SHARD), BF16)],
        compiler_params=_cparams("parallel"),
    )(mine, mine_dt, theirs, theirs_dt)


def _chip_scatter(cs, name):
    n = len(cs)

    def body(*refs):
        ins, outs = refs[:n], refs[n:2 * n]
        send_sems, recv_sems = refs[2 * n:]
        x, y, c = _mesh_pos()
        copies = []
        for i in range(n):
            for k, (px, py) in enumerate(_peer_chips(x, y)):
                copies.append(_remote(ins[i].at[2 * px + py], outs[i].at[k], send_sems, recv_sems, 3 * i + k, (px, py, c)))
        for cp in copies:
            cp.start()
        for cp in copies:
            cp.wait()

    out_shapes = [jax.ShapeDtypeStruct((3,) + a.shape[1:], a.dtype) for a in cs]
    return _comm_call(body, list(cs), out_shapes, 3 * n, name)


_HBM = pl.BlockSpec(memory_space=pltpu.HBM)
_SEM = pl.BlockSpec(memory_space=pltpu.SEMAPHORE)


def _in_hbm(a):
    return pltpu.with_memory_space_constraint(a, pltpu.HBM)


def _split_plan(kind, srcs, lands, x, y, c):
    plan = []
    for src, land in zip(srcs, lands):
        if kind == "sibling":
            rows = _half_rows(src.shape[-2], 1 - c)
            plan.append((src.at[rows] if len(src.shape) == 2 else src.at[:, rows], land, (x, y, 1 - c)))
            continue
        for k, (px, py) in enumerate(_peer_chips(x, y)):
            if kind == "scatter":
                plan.append((src.at[2 * px + py], land.at[k], (px, py, c)))
            else:
                plan.append((src, land.at[2 * x + y], (px, py, c)))
    return plan


def _split_start(kind, srcs, land_shapes, name):
    n = len(srcs)

    def body(*refs):
        ins, lands = refs[:n], refs[n:2 * n]
        send_sems, recv_sems = refs[2 * n], refs[2 * n + 1]
        token = refs[-1]
        x, y, c = _mesh_pos()
        for i, (src, dst, to) in enumerate(_split_plan(kind, ins, lands, x, y, c)):
            pltpu.make_async_remote_copy(src_ref=src, dst_ref=dst, send_sem=send_sems.at[i], recv_sem=recv_sems.at[i],
                                         device_id=to, device_id_type=MESH).start()
        token[...] = jnp.zeros_like(token)

    zones = [lax.empty(s.shape, s.dtype) for s in land_shapes]
    n_sems = n if kind == "sibling" else 3 * n
    res = pl.pallas_call(
        body, name=name,
        out_shape=(pltpu.SemaphoreType.DMA((n_sems,)), pltpu.SemaphoreType.DMA((n_sems,)),
                   *[pltpu.HBM(a.shape, a.dtype) for a in srcs], *[pltpu.HBM(s.shape, s.dtype) for s in land_shapes],
                   jax.ShapeDtypeStruct((8, LANES), F32)),
        in_specs=[_HBM] * (2 * n), out_specs=(_SEM, _SEM, *[_HBM] * (2 * n), pl.BlockSpec(memory_space=pltpu.VMEM)),
        input_output_aliases={i: 2 + i for i in range(2 * n)},
        compiler_params=pltpu.CompilerParams(has_side_effects=pltpu.SideEffectType.DATAFLOW_SIDE_EFFECTING),
    )(*[_in_hbm(a) for a in srcs], *[_in_hbm(z) for z in zones])
    return dict(send=res[0], recv=res[1], srcs=list(res[2:2 + n]), lands=list(res[2 + n:2 + 2 * n]), token=res[-1], kind=kind)


def _split_wait(started, after, name):
    n = len(started["srcs"])
    kind = started["kind"]

    def body(*refs):
        ins, lands = refs[:n], refs[n:2 * n]
        send_sems, recv_sems = refs[2 * n], refs[2 * n + 1]
        x, y, c = _mesh_pos()
        for i, (src, dst, _) in enumerate(_split_plan(kind, ins, lands, x, y, c)):
            cp = pltpu.make_async_remote_copy(src_ref=src, dst_ref=dst, send_sem=send_sems.at[i], recv_sem=recv_sems.at[i],
                                              device_id=(x, y, c), device_id_type=MESH)
            cp.wait_send()
            cp.wait_recv()

    arrs = started["srcs"] + started["lands"]
    res = pl.pallas_call(
        body, name=name, out_shape=tuple(pltpu.HBM(a.shape, a.dtype) for a in arrs),
        in_specs=[_HBM] * (2 * n) + [_SEM, _SEM, pl.BlockSpec(memory_space=pl.ANY)], out_specs=tuple([_HBM] * (2 * n)),
        input_output_aliases={i: i for i in range(2 * n)},
        compiler_params=pltpu.CompilerParams(has_side_effects=pltpu.SideEffectType.DATAFLOW_SIDE_EFFECTING),
    )(*arrs, started["send"], started["recv"], after)
    return list(res[:n]), list(res[n:])


def _sibling_share(fs, name):
    n = len(fs)

    def body(*refs):
        ins, outs = refs[:n], refs[n:2 * n]
        send_sems, recv_sems = refs[2 * n:]
        x, y, c = _mesh_pos()
        copies = [_remote(ins[i], outs[i], send_sems, recv_sems, i, (x, y, 1 - c)) for i in range(n)]
        for cp in copies:
            cp.start()
        for cp in copies:
            cp.wait()

    out_shapes = [jax.ShapeDtypeStruct(a.shape, a.dtype) for a in fs]
    return _comm_call(body, list(fs), out_shapes, n, name)


def _allgather8(v, name, after=()):
    m = v.shape[0]

    def body(v_ref, *rest):
        out_ref, send_sems, recv_sems = rest[len(after):]
        x, y, c = _mesh_pos()
        me, sib = (x, y, c), (x, y, 1 - c)
        chips = _peer_chips(x, y)
        rem = functools.partial(_remote, send_sems=send_sems, recv_sems=recv_sems)

        def blk(px, py, pc):
            return out_ref.at[4 * px + 2 * py + pc]

        first = [rem(v_ref, blk(*me), sem=0, to=sib)]
        first += [rem(v_ref, blk(*me), sem=1 + k, to=(px, py, c)) for k, (px, py) in enumerate(chips)]
        for cp in first:
            cp.start()
        passed = []
        for k, (px, py) in enumerate(chips):
            landed = blk(px, py, c)
            rem(landed, landed, sem=1 + k, to=me).wait_recv()
            fwd = rem(landed, landed, sem=4 + k, to=sib)
            fwd.start()
            passed.append(fwd)
        rem(blk(*sib), blk(*sib), sem=0, to=me).wait_recv()
        for k, (px, py) in enumerate(chips):
            theirs = blk(px, py, 1 - c)
            rem(theirs, theirs, sem=4 + k, to=me).wait_recv()
        for cp in first + passed:
            cp.wait_send()

    return _comm_call(body, [v, *after], [jax.ShapeDtypeStruct((8, m, LANES), v.dtype)], 7, name)[0]


_WEIGHTS = ["norm1_g", "w_in", "conv_a_w", "conv_a_b", "ln_a_g", "ln_a_b", "ln_b_g", "ln_b_b", "w_spatial", "b_spatial",
            "conv_c_w", "conv_c_b", "dt_bias", "a_log", "d_skip", "norm_c_g", "w_out", "norm2_g", "w_ff1", "w_ff2", "final_g"]
_BIG = ["w_in", "w_out", "w_ff1", "w_ff2"]
_CONV_SHARDED = ["conv_a_w", "conv_c_w"]
_SMALL = [w for w in _WEIGHTS if w not in _BIG and w != "final_g"]
_PACK_ROWS = 512


def _pack(arrs):
    flat = jnp.concatenate([a.reshape(-1) for a in arrs])
    blk = _PACK_ROWS * LANES
    n = flat.shape[0]
    return jnp.pad(flat, (0, -(-n // blk) * blk - n)).reshape(-1, LANES)


def _unpack(packed, shapes):
    flat = packed.reshape(-1)
    out, off = [], 0
    for s in shapes:
        n = math.prod(s)
        out.append(flat[off:off + n].reshape(s))
        off += n
    return out


def _cols_to_chips(a):
    k = a.shape[0]
    return a.reshape(k, N_CHIPS, -1).transpose(1, 0, 2)


def _chips_to_cols(a):
    return a.transpose(1, 0, 2).reshape(a.shape[1], -1)


def _own_shards(w, li):
    return [w[k][li].astype(BF16) for k in _BIG] + [w[k][li] for k in _CONV_SHARDED]


def _layer_params(w, li, own, gathered, q):
    g_in, g_out, g_ff1, g_ff2, g_ca, g_cc = [lax.dynamic_update_index_in_dim(g, o, q, axis=0)
                                             for g, o in zip(gathered, own)]
    p = {k: w[k][li] for k in _SMALL if k not in _CONV_SHARDED}
    w_in = _chips_to_cols(g_in)
    p["w_main"] = w_in[:, :D_MAIN]
    p["w_dt"] = _group_heads(w_in[:, D_MAIN:])
    p["w_out"] = g_out.reshape(D_MIX, D_MODEL)
    p["w_ff1"] = g_ff1
    p["w_ff2"] = g_ff2.reshape(D_FF, D_MODEL)
    p["conv_a_w"] = _chips_to_cols(g_ca)
    p["conv_c_w"] = _chips_to_cols(g_cc)
    return p


def _ffn_out_grads(g):
    return [g["w_out"].reshape(N_CHIPS, -1, D_MODEL), g["w_ff1"], g["w_ff2"].reshape(N_CHIPS, -1, D_MODEL)]


def _half_shape(a):
    return jax.ShapeDtypeStruct(a.shape[:-2] + (a.shape[-2] // 2, a.shape[-1]), a.dtype)


def _chip_sums(g, early, early_from_sib, li, c, q):
    n = f"l{li}_rs_"
    late = [g["w_main"], g["w_dt"]]
    full = late + list(early)
    from_sib = list(_sibling_other_halves(late, n + "sib")) + list(early_from_sib)
    mine = [lax.dynamic_slice_in_dim(a, c * b.shape[-2], b.shape[-2], axis=a.ndim - 2) for a, b in zip(full, from_sib)]
    sums = [_chipsum_in(mine[0], mine[1], from_sib[0], from_sib[1], n + "chipsum0")]
    for i in range(2, len(full)):
        shape = from_sib[i].shape
        s32, s16 = _ew(lambda u, v: (u + v, u + v), [mine[i].reshape(-1, shape[-1]), from_sib[i].reshape(-1, shape[-1])],
                       (F32, BF16), n + f"chipsum{i - 1}")
        sums.append((s32.reshape(shape), s16.reshape(shape)))
    chip_f32 = [lax.dynamic_index_in_dim(s32, q, axis=0, keepdims=False) for s32, _ in sums]
    chip_bf16 = [s16 for _, s16 in sums]
    return chip_f32, chip_bf16


def _finish_reduce(chip_f32, from_chips, li, c):
    n = f"l{li}_rs_"
    halves = [_ew(lambda o, r0, r1, r2_: (((o + r0) + r1) + r2_,), [own, rb, rb, rb], (F32,), n + f"final{i}",
                  leads=[None, 0, 1, 2])[0] for i, (own, rb) in enumerate(zip(chip_f32, from_chips))]
    from_sib = _sibling_share(halves, n + "share")
    return [jnp.where(c == 0, jnp.concatenate([h, s], axis=0), jnp.concatenate([s, h], axis=0))
            for h, s in zip(halves, from_sib)]


def kernel(x, norm1_g, w_in, conv_a_w, conv_a_b, ln_a_g, ln_a_b, ln_b_g, ln_b_b, w_spatial, b_spatial, conv_c_w, conv_c_b, dt_bias, a_log, d_skip, norm_c_g, w_out, norm2_g, w_ff1, w_ff2, final_g, loss_target, m_norm1_g, m_w_in, m_conv_a_w, m_conv_a_b, m_ln_a_g, m_ln_a_b, m_ln_b_g, m_ln_b_b, m_w_spatial, m_b_spatial, m_conv_c_w, m_conv_c_b, m_dt_bias, m_a_log, m_d_skip, m_norm_c_g, m_w_out, m_norm2_g, m_w_ff1, m_w_ff2, m_final_g, v_norm1_g, v_w_in, v_conv_a_w, v_conv_a_b, v_ln_a_g, v_ln_a_b, v_ln_b_g, v_ln_b_b, v_w_spatial, v_b_spatial, v_conv_c_w, v_conv_c_b, v_dt_bias, v_a_log, v_d_skip, v_norm_c_g, v_w_out, v_norm2_g, v_w_ff1, v_w_ff2, v_final_g):
    given = dict(locals())
    w = {k: given[k] for k in _WEIGHTS}
    m = {k: given["m_" + k] for k in _WEIGHTS}
    v = {k: given["v_" + k] for k in _WEIGHTS}
    depth = w_in.shape[0]
    nseq, seq, d = x.shape
    xi, yi, ci = _mesh_pos()
    q = 2 * xi + yi

    own = [_own_shards(w, li) for li in range(depth)]
    nb = len(_BIG)
    gathered = _gather_weights(own[0][:nb], own[0][nb:], "l0_gather")
    h = x.reshape(nseq * seq, d)
    layer_params, saved = [], []
    for li in range(depth):
        nxt = None
        if li + 1 < depth:
            srcs, _ = lax.optimization_barrier((own[li + 1], gathered))
            zones = [jax.ShapeDtypeStruct((N_CHIPS,) + a.shape, a.dtype) for a in srcs]
            nxt = _split_start("gather", srcs, zones, f"l{li + 1}_gather_start")
        layer_params.append(_layer_params(w, li, own[li], gathered, q))
        h, s = _layer_fwd(h, layer_params[li], seq, li, after=() if nxt is None else (nxt["token"],))
        saved.append(s)
        if nxt is not None:
            own[li + 1], gathered = _split_wait(nxt, h, f"l{li + 1}_gather_wait")
    loss, dx, d_final = _loss_head(h, final_g, loss_target.reshape(nseq * seq, d))

    grads = [None] * depth
    big_grads = [None] * depth
    pending = None
    for li in reversed(range(depth)):
        swaps = []

        def early_swap(g, li=li, swaps=swaps):
            early = _ffn_out_grads(g)
            swaps.append(_split_start("sibling", early, [_half_shape(a) for a in early], f"l{li}_rs_sib_start"))
            return (swaps[0]["token"],)

        dx, grads[li] = _layer_bwd(dx, layer_params[li], saved[li], seq, li,
                                   after=() if pending is None else (pending[1]["token"],), on_ffn_grads=early_swap)
        if pending is not None:
            lj, scatter, chip_f32 = pending
            big_grads[lj] = _finish_reduce(chip_f32, _split_wait(scatter, dx, f"l{lj}_rs_scatter_wait")[1], lj, ci)
        early, early_from_sib = _split_wait(swaps[0], dx, f"l{li}_rs_sib_wait")
        chip_f32, chip_bf16 = _chip_sums(grads[li], early, early_from_sib, li, ci, q)
        lands = [jax.ShapeDtypeStruct((3,) + a.shape[1:], a.dtype) for a in chip_bf16]
        pending = (li, _split_start("scatter", chip_bf16, lands, f"l{li}_rs_scatter_start"), chip_f32)
    grad_out, delta_out, m_out, v_out = {}, {}, {}, {}

    small_shapes = [grads[0][k].shape for k in _SMALL]
    parts = [grads[li][k] for li in range(depth) for k in _SMALL] + [d_final, loss.reshape(1)]
    packed_parts = _pack(parts)
    gathered = lax.dynamic_update_index_in_dim(_allgather8(packed_parts, "small_allgather", after=(pending[1]["token"],)),
                                               packed_parts, 2 * q + ci, axis=0)

    def sum8(*blocks):
        acc = blocks[0]
        for b in blocks[1:]:
            acc = acc + b
        return (acc,)

    total = _ew(sum8, [gathered] * 8, (F32,), "small_sum", leads=list(range(8)))[0]
    summed = _unpack(total, small_shapes * depth + [d_final.shape, (1,)])
    loss_total = summed[-1][0]
    small_grads = {k: jnp.stack([summed[li * len(_SMALL) + i] for li in range(depth)]) for i, k in enumerate(_SMALL)}
    small_grads["final_g"] = summed[-2]
    for k in _CONV_SHARDED:
        n_shard = w[k].shape[-1]
        small_grads[k] = lax.dynamic_slice_in_dim(small_grads[k], q * n_shard, n_shard, axis=2)
    names = _SMALL + ["final_g"]
    shapes = [w[k].shape for k in names]
    packed = [_pack([src[k] for k in names]) for src in (w, small_grads, m, v)]
    outs = _ew(_adam_fn, packed, (F32, F32, F32), "adam_small")
    for dst, o in zip((delta_out, m_out, v_out), outs):
        for k, a in zip(names, _unpack(o, shapes)):
            dst[k] = a
    for k in names:
        grad_out[k] = small_grads[k]

    lj, scatter, chip_f32 = pending
    big_grads[lj] = _finish_reduce(chip_f32, _split_wait(scatter, outs[0], f"l{lj}_rs_scatter_wait")[1], lj, ci)
    for i, k in enumerate(_BIG):
        grad_out[k] = jnp.stack([big_grads[li][i] for li in range(depth)])
        delta_out[k], m_out[k], v_out[k] = _adam(w[k], grad_out[k], m[k], v[k], "adam_" + k)

    return (loss_total, dx.reshape(nseq, seq, d), *[grad_out[k] for k in _WEIGHTS], *[delta_out[k] for k in _WEIGHTS],
            *[m_out[k] for k in _WEIGHTS], *[v_out[k] for k in _WEIGHTS])
```

```python
import functools
import math

import jax
import jax.numpy as jnp
from jax import lax
from jax.experimental import pallas as pl
from jax.experimental.pallas import tpu as pltpu

F32 = jnp.float32
BF16 = jnp.bfloat16
MESH = pl.DeviceIdType.MESH

D_MODEL = 1024
DEPTH = 4
HEAD_DIM = 64
A_WIDTH = 512
B_WIDTH = 512
C_WIDTH = 1024
C_HEADS = 16
CONV_A_K = 31
CONV_C_K = 4
CHUNK = 128
SSM_STATE = 128
D_CONV_C = 1536
D_MAIN = 4608
D_IN_PROJ = 4624
D_MIX = 2048
D_FF = 4096
EPS = 1e-5
NEG = -1e30
LANES = 128
CONV_PAD = 32
N_CHIPS = 4

ADAM_LR = 0.001
ADAM_B1 = 0.9
ADAM_B2 = 0.999
ADAM_EPS = 1e-08
ADAM_WD = 0.01
ADAM_STEP = 10

VMEM_LIMIT = 56 * 1024 * 1024
MATMUL_VMEM_BUDGET = 44 * 1024 * 1024

COL_AVAL, COL_AGATE, COL_BU, COL_BV, COL_Z, COL_XBC = 0, 4, 8, 12, 16, 24


def _cparams(*sem):
    return pltpu.CompilerParams(dimension_semantics=sem, vmem_limit_bytes=VMEM_LIMIT)


_DN = {"nn": (((1,), (0,)), ((), ())), "nt": (((1,), (1,)), ((), ())), "tn": (((0,), (0,)), ((), ()))}


def _dot_raw(a, b, mode):
    return lax.dot_general(a.astype(BF16), b.astype(BF16), _DN[mode], preferred_element_type=F32)


def _make_dot(mode):
    @jax.custom_vjp
    def f(a, b):
        return _dot_raw(a, b, mode)

    def fwd(a, b):
        return _dot_raw(a, b, mode), (a, b)

    def bwd(res, g):
        a, b = res
        if mode == "nn":
            return _dot_raw(g, b, "nt"), _dot_raw(a, g, "tn")
        if mode == "nt":
            return _dot_raw(g, b, "nn"), _dot_raw(g, a, "tn")
        return _dot_raw(b, g, "nt"), _dot_raw(a, g, "nn")

    f.defvjp(fwd, bwd)
    return f


_nn = _make_dot("nn")
_nt = _make_dot("nt")
_tn = _make_dot("tn")


def _xdot(a, e):
    return jnp.dot(a, e, precision=lax.Precision.HIGHEST, preferred_element_type=F32)


def _iota2(shape, dim):
    return lax.broadcasted_iota(jnp.int32, shape, dim)


def _gmean_impl(x):
    n = x.shape[-1]
    same = (_iota2((n, n), 0) < HEAD_DIM) == (_iota2((n, n), 1) < HEAD_DIM)
    p = jnp.where(same, 1.0 / HEAD_DIM, 0.0).astype(BF16)
    hi = x.astype(BF16)
    lo = (x - hi.astype(F32)).astype(BF16)
    dn = _DN["nn"]
    return (lax.dot_general(hi, p, dn, preferred_element_type=F32)
            + lax.dot_general(lo, p, dn, preferred_element_type=F32))


@jax.custom_vjp
def _gmean(x):
    return _gmean_impl(x)


_gmean.defvjp(lambda x: (_gmean_impl(x), None), lambda _, g: (_gmean_impl(g),))


def _sigmoid(x):
    return 1.0 / (1.0 + jnp.exp(-x))


def _silu(x):
    return x * _sigmoid(x)


def _gelu(x):
    return 0.5 * x * (1.0 + lax.erf(x * 0.7071067811865476))


def _softplus(x):
    return jnp.maximum(x, 0.0) + jnp.log(1.0 + jnp.exp(-jnp.abs(x)))


def _rms(x, g):
    return x * lax.rsqrt(jnp.mean(x * x, axis=-1, keepdims=True) + EPS) * g


def _ln64(x, g, b):
    mu = _gmean(x)
    xc = x - mu
    var = _gmean(xc * xc)
    return xc * lax.rsqrt(var + EPS) * g + b


def _lane_lt64(shape):
    return _iota2(shape, 1) < HEAD_DIM


def _pick(n, pref):
    for t in pref:
        if n % t == 0:
            return t
    return n


_UNREAD = pl.BlockSpec(memory_space=pl.ANY)


def _matmul_tiles(m, n_unit, k, a_item, b_item, out_bytes):
    best = None
    for tm in (1024, 512, 256, 128):
        for tn in (1536, 1024, 768, 512, 256, 128):
            if m % tm or n_unit % tn:
                continue
            need = 2 * k * (tm * a_item + tn * b_item) + 2 * tm * tn * out_bytes
            if need <= MATMUL_VMEM_BUDGET and (best is None or tm * tn > best[0] * best[1]):
                best = (tm, tn)
    assert best is not None, (m, n_unit, k)
    return best


def _matmul(a, b, *, mode, name, add=None, epilogue=None, extra=None, out_dtypes=(F32,), after=(), b_chips=False,
            out_chips=False):
    sh = b.shape[-1] if b_chips else None
    if mode == "nn":
        (m, k), n = a.shape, (N_CHIPS * sh if b_chips else b.shape[1])
    elif mode == "nt":
        (m, k), n = a.shape, b.shape[-2]
    else:
        (k, m), n = a.shape, b.shape[1]
    osh = n // N_CHIPS if out_chips else None
    out_bytes = sum(jnp.dtype(dt).itemsize for dt in out_dtypes) + (0 if add is None else add.dtype.itemsize) \
        + (0 if extra is None else extra.dtype.itemsize)
    tm, tn = _matmul_tiles(m, sh if (b_chips and mode == "nn") else (osh or n), k, a.dtype.itemsize, b.dtype.itemsize,
                           out_bytes)
    a_spec = pl.BlockSpec((k, tm), lambda i, j: (0, i)) if mode == "tn" else pl.BlockSpec((tm, k), lambda i, j: (i, 0))
    if b_chips and mode == "nn":
        per = sh // tn
        b_spec = pl.BlockSpec((None, k, tn), lambda i, j: (j // per, 0, j % per))
    elif b_chips:
        b_spec = pl.BlockSpec((N_CHIPS, tn, sh), lambda i, j: (0, j, 0))
    elif mode == "nt":
        b_spec = pl.BlockSpec((tn, k), lambda i, j: (j, 0))
    else:
        b_spec = pl.BlockSpec((k, tn), lambda i, j: (0, j))
    if out_chips:
        o_per = osh // tn
        o_spec = pl.BlockSpec((None, tm, tn), lambda i, j: (j // o_per, i, j % o_per))
        out_shape = [jax.ShapeDtypeStruct((N_CHIPS, m, osh), dt) for dt in out_dtypes]
    else:
        o_spec = pl.BlockSpec((tm, tn), lambda i, j: (i, j))
        out_shape = [jax.ShapeDtypeStruct((m, n), dt) for dt in out_dtypes]
    ins = [a, b]
    in_specs = [a_spec, b_spec]
    if add is not None:
        ins.append(add)
        in_specs.append(o_spec)
    if extra is not None:
        ins.append(extra)
        in_specs.append(o_spec)
    ins += list(after)
    in_specs += [_UNREAD] * len(after)
    n_out = len(out_dtypes)

    def body(*refs):
        a_ref, b_ref = refs[0], refs[1]
        pos = 2
        add_ref = ex_ref = None
        if add is not None:
            add_ref = refs[pos]
            pos += 1
        if extra is not None:
            ex_ref = refs[pos]
            pos += 1
        pos += len(after)
        if b_chips and mode == "nt":
            acc = _dot_raw(a_ref[:, pl.ds(0, sh)], b_ref[0], mode)
            for chip in range(1, N_CHIPS):
                acc = acc + _dot_raw(a_ref[:, pl.ds(chip * sh, sh)], b_ref[chip], mode)
        else:
            acc = _dot_raw(a_ref[...], b_ref[...], mode)
        if add_ref is not None:
            acc = acc + add_ref[...].astype(F32)
        outs = (acc,) if epilogue is None else epilogue(acc, None if ex_ref is None else ex_ref[...])
        for o_ref, o in zip(refs[pos:pos + n_out], outs):
            o_ref[...] = o.astype(o_ref.dtype)

    res = pl.pallas_call(
        body, name=name, grid=(m // tm, n // tn), in_specs=in_specs, out_specs=[o_spec] * n_out, out_shape=out_shape,
        compiler_params=_cparams("parallel", "parallel"),
    )(*ins)
    return res[0] if n_out == 1 else res


def _matmul_ksplit(a, b, *, mode, name, add=None, epilogue=None, extra=None, out_dtypes=(F32,), after=(), b_chips=False,
                   out_chips=False):
    sh = b.shape[-1] if b_chips else None
    if mode == "nn":
        (m, k), n = a.shape, (N_CHIPS * sh if b_chips else b.shape[1])
    elif mode == "nt":
        (m, k), n = a.shape, b.shape[-2]
    else:
        (k, m), n = a.shape, b.shape[1]
    osh = n // N_CHIPS if out_chips else None
    tm = _pick(m, (1024, 512, 256, 128))
    tn = _pick(sh if (b_chips and mode == "nn") else (osh or n), (1536, 1024, 512, 256, 128))
    out_bytes = sum(jnp.dtype(dt).itemsize for dt in out_dtypes) + (0 if add is None else add.dtype.itemsize) \
        + (0 if extra is None else extra.dtype.itemsize)
    k_dim = sh if (b_chips and mode == "nt") else k
    for tk in (4096, 2048, 1536, 1024, 512, 256, 128):
        if k_dim % tk:
            continue
        acc_bytes = 0 if tk == k else 4 * tm * tn
        need = 2 * tk * (tm * a.dtype.itemsize + tn * b.dtype.itemsize) + 2 * tm * tn * out_bytes + acc_bytes
        if need <= MATMUL_VMEM_BUDGET:
            break
    nk = k // tk
    a_spec = {"nn": pl.BlockSpec((tm, tk), lambda i, j, kk: (i, kk)),
              "nt": pl.BlockSpec((tm, tk), lambda i, j, kk: (i, kk)),
              "tn": pl.BlockSpec((tk, tm), lambda i, j, kk: (kk, i))}[mode]
    if b_chips:
        per = sh // (tn if mode == "nn" else tk)
        b_spec = {"nn": pl.BlockSpec((None, tk, tn), lambda i, j, kk: (j // per, kk, j % per)),
                  "nt": pl.BlockSpec((None, tn, tk), lambda i, j, kk: (kk // per, j, kk % per))}[mode]
    else:
        b_spec = {"nn": pl.BlockSpec((tk, tn), lambda i, j, kk: (kk, j)),
                  "nt": pl.BlockSpec((tn, tk), lambda i, j, kk: (j, kk)),
                  "tn": pl.BlockSpec((tk, tn), lambda i, j, kk: (kk, j))}[mode]
    if out_chips:
        o_per = osh // tn
        o_spec = pl.BlockSpec((None, tm, tn), lambda i, j, kk: (j // o_per, i, j % o_per))
        out_shape = [jax.ShapeDtypeStruct((N_CHIPS, m, osh), dt) for dt in out_dtypes]
    else:
        o_spec = pl.BlockSpec((tm, tn), lambda i, j, kk: (i, j))
        out_shape = [jax.ShapeDtypeStruct((m, n), dt) for dt in out_dtypes]
    ins = [a, b]
    in_specs = [a_spec, b_spec]
    if add is not None:
        ins.append(add)
        in_specs.append(o_spec)
    if extra is not None:
        ins.append(extra)
        in_specs.append(o_spec)
    ins += list(after)
    in_specs += [_UNREAD] * len(after)
    n_out = len(out_dtypes)

    def body(*refs):
        a_ref, b_ref = refs[0], refs[1]
        pos = 2
        add_ref = ex_ref = None
        if add is not None:
            add_ref = refs[pos]
            pos += 1
        if extra is not None:
            ex_ref = refs[pos]
            pos += 1
        pos += len(after)
        o_refs = refs[pos:pos + n_out]

        def finish(acc):
            if add_ref is not None:
                acc = acc + add_ref[...].astype(F32)
            outs = (acc,) if epilogue is None else epilogue(acc, None if ex_ref is None else ex_ref[...])
            for o_ref, o in zip(o_refs, outs):
                o_ref[...] = o.astype(o_ref.dtype)

        part = _dot_raw(a_ref[...], b_ref[...], mode)
        if nk == 1:
            finish(part)
            return
        acc_ref = refs[pos + n_out]
        kk = pl.program_id(2)

        @pl.when(kk == 0)
        def _():
            acc_ref[...] = part

        @pl.when(jnp.logical_and(kk > 0, kk < nk - 1))
        def _():
            acc_ref[...] += part

        @pl.when(kk == nk - 1)
        def _():
            finish(acc_ref[...] + part)

    res = pl.pallas_call(
        body, name=name, grid=(m // tm, n // tn, nk),
        in_specs=in_specs, out_specs=[o_spec] * n_out, out_shape=out_shape,
        scratch_shapes=[pltpu.VMEM((tm, tn), F32)] if nk > 1 else [],
        compiler_params=_cparams("parallel", "parallel", "arbitrary"),
    )(*ins)
    return res[0] if n_out == 1 else res


def _relu2_epilogue(acc, _):
    r = jnp.maximum(acc, 0.0)
    return acc, r * r


def _relu2_bwd_epilogue(acc, u):
    return (acc * (2.0 * jnp.maximum(u, 0.0)),)


def _row_tile(t):
    return _pick(t, (512, 256, 128))


def _rms_fwd(x, g, name, after=()):
    t, d = x.shape
    tm = _row_tile(t)

    def body(x_ref, g_ref, *rest):
        o_ref = rest[-1]
        o_ref[...] = _rms(x_ref[...], g_ref[...]).astype(BF16)

    return pl.pallas_call(
        body, name=name, grid=(t // tm,),
        in_specs=[pl.BlockSpec((tm, d), lambda i: (i, 0)), pl.BlockSpec((1, d), lambda i: (0, 0))] + [_UNREAD] * len(after),
        out_specs=pl.BlockSpec((tm, d), lambda i: (i, 0)),
        out_shape=jax.ShapeDtypeStruct((t, d), BF16),
        compiler_params=_cparams("parallel"),
    )(x, g.reshape(1, d), *after)


def _rms_bwd(x, g, dh, dres, name):
    t, d = x.shape
    tm = _row_tile(t)

    def body(x_ref, g_ref, dh_ref, dres_ref, dx_ref, dg_ref):
        @pl.when(pl.program_id(0) == 0)
        def _():
            dg_ref[...] = jnp.zeros_like(dg_ref)

        _, vjp = jax.vjp(_rms, x_ref[...], g_ref[...])
        dx, dg = vjp(dh_ref[...].astype(F32))
        dx_ref[...] = dx + dres_ref[...]
        dg_ref[...] += dg

    row = pl.BlockSpec((tm, d), lambda i: (i, 0))
    vec = pl.BlockSpec((1, d), lambda i: (0, 0))
    dx, dg = pl.pallas_call(
        body, name=name, grid=(t // tm,),
        in_specs=[row, vec, row, row], out_specs=[row, vec],
        out_shape=[jax.ShapeDtypeStruct((t, d), F32), jax.ShapeDtypeStruct((1, d), F32)],
        compiler_params=_cparams("arbitrary"),
    )(x, g.reshape(1, d), dh, dres)
    return dx, dg.reshape(d)


def _loss_head(x, g, target):
    t, d = x.shape
    tm = _row_tile(t)

    def loss_fn(xv, gv, tv):
        err = _rms(xv, gv) - tv
        return 0.5 * jnp.sum(jnp.mean(err * err, axis=-1, keepdims=True))

    def body(x_ref, g_ref, t_ref, loss_ref, dx_ref, dg_ref):
        @pl.when(pl.program_id(0) == 0)
        def _():
            dg_ref[...] = jnp.zeros_like(dg_ref)
            loss_ref[...] = jnp.zeros_like(loss_ref)

        tv = t_ref[...]
        val, vjp = jax.vjp(lambda xv, gv: loss_fn(xv, gv, tv), x_ref[...], g_ref[...])
        dx, dg = vjp(jnp.ones((), F32))
        dx_ref[...] = dx
        dg_ref[...] += dg
        loss_ref[...] += jnp.full(loss_ref.shape, val, F32)

    row = pl.BlockSpec((tm, d), lambda i: (i, 0))
    vec = pl.BlockSpec((1, d), lambda i: (0, 0))
    loss, dx, dg = pl.pallas_call(
        body, name="loss_head", grid=(t // tm,),
        in_specs=[row, vec, row], out_specs=[pl.BlockSpec((1, LANES), lambda i: (0, 0)), row, vec],
        out_shape=[jax.ShapeDtypeStruct((1, LANES), F32), jax.ShapeDtypeStruct((t, d), F32),
                   jax.ShapeDtypeStruct((1, d), F32)],
        compiler_params=_cparams("arbitrary"),
    )(x, g.reshape(1, d), target)
    return loss[0, 0], dx, dg.reshape(d)


def _pre_glu(val, gate):
    return val * _sigmoid(gate)


def _pre_id(x):
    return x


def _post_lnsilu(c, g, b):
    return _silu(_ln64(c, g, b))


def _post_silu(c):
    return _silu(c)


def _conv_cfg(kind):
    if kind == "a":
        return dict(k=CONV_A_K, pre=_pre_glu, post=_post_lnsilu, n_in=2, n_par=2, nblk=A_WIDTH // LANES,
                    cols=(COL_AVAL, COL_AGATE))
    return dict(k=CONV_C_K, pre=_pre_id, post=_post_silu, n_in=1, n_par=0, nblk=D_CONV_C // LANES,
                cols=(COL_XBC,))


def _conv_fwd(kind, proj, w, bias, params, seq, name, out_dtype=F32, keep_conv=False):
    cfg = _conv_cfg(kind)
    kt, pre, post, n_in = cfg["k"], cfg["pre"], cfg["post"], cfg["n_in"]
    t = proj.shape[0]
    nseq = t // seq
    c = cfg["nblk"] * LANES
    rt = min(256, seq)
    nrt = seq // rt
    off0 = CONV_PAD - (kt - 1)

    def body(*refs):
        in_refs = refs[:n_in]
        w_ref, b_ref = refs[n_in], refs[n_in + 1]
        par_refs = refs[n_in + 2:n_in + 2 + cfg["n_par"]]
        out_refs = refs[n_in + 2 + cfg["n_par"]:-1]
        hpad = refs[-1]
        hpad[pl.ds(0, CONV_PAD), :] = jnp.zeros((CONV_PAD, LANES), F32)
        for r in range(nrt):
            hpad[pl.ds(CONV_PAD + r * rt, rt), :] = pre(*[x[pl.ds(r * rt, rt), :] for x in in_refs])
        pars = [p[...] for p in par_refs]
        for r in range(nrt):
            acc = jnp.broadcast_to(b_ref[...], (rt, LANES))
            for k in range(kt):
                acc = acc + w_ref[pl.ds(k, 1), :] * hpad[pl.ds(off0 + k + r * rt, rt), :]
            out_refs[0][pl.ds(r * rt, rt), :] = post(acc, *pars).astype(out_dtype)
            if keep_conv:
                out_refs[1][pl.ds(r * rt, rt), :] = acc

    in_specs = [pl.BlockSpec((seq, LANES), functools.partial(lambda s, j, col: (s, col + j), col=col))
                for col in cfg["cols"]]
    vec = pl.BlockSpec((1, LANES), lambda s, j: (0, j))
    in_specs += [pl.BlockSpec((CONV_PAD, LANES), lambda s, j: (0, j)), vec] + [vec] * cfg["n_par"]
    blk = pl.BlockSpec((seq, LANES), lambda s, j: (s, j))
    res = pl.pallas_call(
        body, name=name, grid=(nseq, cfg["nblk"]),
        in_specs=in_specs, out_specs=[blk, blk] if keep_conv else [blk],
        out_shape=[jax.ShapeDtypeStruct((t, c), out_dtype)] + ([jax.ShapeDtypeStruct((t, c), F32)] if keep_conv else []),
        scratch_shapes=[pltpu.VMEM((seq + CONV_PAD, LANES), F32)],
        compiler_params=_cparams("parallel", "parallel"),
    )(*([proj] * n_in), w, bias, *params)
    return tuple(res) if keep_conv else res[0]


def _conv_bwd(kind, proj, w, bias, params, dy, seq, name, dy_col=0, conv_out=None):
    kept = conv_out is not None
    cfg = _conv_cfg(kind)
    kt, pre, post, n_in, n_par = cfg["k"], cfg["pre"], cfg["post"], cfg["n_in"], cfg["n_par"]
    t = proj.shape[0]
    nseq = t // seq
    c = cfg["nblk"] * LANES
    rt = min(256, seq)
    nrt = seq // rt
    off0 = CONV_PAD - (kt - 1)

    def body(*refs):
        in_refs = refs[:n_in]
        w_ref, b_ref = refs[n_in], refs[n_in + 1]
        par_refs = refs[n_in + 2:n_in + 2 + n_par]
        pos = n_in + 2 + n_par
        dy_ref = refs[pos]
        if kept:
            pos += 1
            conv_ref = refs[pos]
        din_refs = refs[pos + 1:pos + 1 + n_in]
        dw_ref, db_ref = refs[pos + 1 + n_in], refs[pos + 2 + n_in]
        dpar_refs = refs[pos + 3 + n_in:pos + 3 + n_in + n_par]
        hpad, dcpad = refs[pos + 3 + n_in + n_par:]

        @pl.when(pl.program_id(1) == 0)
        def _():
            dw_ref[...] = jnp.zeros_like(dw_ref)
            db_ref[...] = jnp.zeros_like(db_ref)
            for r in dpar_refs:
                r[...] = jnp.zeros_like(r)

        hpad[pl.ds(0, CONV_PAD), :] = jnp.zeros((CONV_PAD, LANES), F32)
        dcpad[pl.ds(seq, CONV_PAD), :] = jnp.zeros((CONV_PAD, LANES), F32)
        for r in range(nrt):
            hpad[pl.ds(CONV_PAD + r * rt, rt), :] = pre(*[x[pl.ds(r * rt, rt), :] for x in in_refs])
        pars = [p[...] for p in par_refs]
        for r in range(nrt):
            if kept:
                acc = conv_ref[pl.ds(r * rt, rt), :]
            else:
                acc = jnp.broadcast_to(b_ref[...], (rt, LANES))
                for k in range(kt):
                    acc = acc + w_ref[pl.ds(k, 1), :] * hpad[pl.ds(off0 + k + r * rt, rt), :]
            _, vjp = jax.vjp(post, acc, *pars)
            grads = vjp(dy_ref[pl.ds(r * rt, rt), :])
            dcpad[pl.ds(r * rt, rt), :] = grads[0]
            db_ref[...] += jnp.sum(grads[0], axis=0, keepdims=True)
            for ref, gpar in zip(dpar_refs, grads[1:]):
                ref[...] += gpar
        for r in range(nrt):
            dh = jnp.zeros((rt, LANES), F32)
            for k in range(kt):
                dh = dh + w_ref[pl.ds(k, 1), :] * dcpad[pl.ds(r * rt + kt - 1 - k, rt), :]
            _, vjp = jax.vjp(pre, *[x[pl.ds(r * rt, rt), :] for x in in_refs])
            for ref, gin in zip(din_refs, vjp(dh)):
                ref[pl.ds(r * rt, rt), :] = gin.astype(ref.dtype)
        for k in range(kt):
            s = jnp.zeros((1, LANES), F32)
            for r in range(nrt):
                s = s + jnp.sum(dcpad[pl.ds(r * rt, rt), :] * hpad[pl.ds(off0 + k + r * rt, rt), :],
                                axis=0, keepdims=True)
            dw_ref[pl.ds(k, 1), :] += s

    in_specs = [pl.BlockSpec((seq, LANES), functools.partial(lambda j, s, col: (s, col + j), col=col))
                for col in cfg["cols"]]
    vec = pl.BlockSpec((1, LANES), lambda j, s: (0, j))
    wspec = pl.BlockSpec((CONV_PAD, LANES), lambda j, s: (0, j))
    blk = pl.BlockSpec((seq, LANES), lambda j, s: (s, j))
    in_specs += [wspec, vec] + [vec] * n_par + [pl.BlockSpec((seq, LANES), lambda j, s: (s, dy_col + j))]
    in_specs += [blk] if kept else []
    out_specs = [blk] * n_in + [wspec, vec] + [vec] * n_par
    out_shape = ([jax.ShapeDtypeStruct((t, c), BF16)] * n_in
                 + [jax.ShapeDtypeStruct((CONV_PAD, c), F32), jax.ShapeDtypeStruct((1, c), F32)]
                 + [jax.ShapeDtypeStruct((1, c), F32)] * n_par)
    res = pl.pallas_call(
        body, name=name, grid=(cfg["nblk"], nseq),
        in_specs=in_specs, out_specs=out_specs, out_shape=out_shape,
        scratch_shapes=[pltpu.VMEM((seq + CONV_PAD, LANES), F32), pltpu.VMEM((seq + CONV_PAD, LANES), F32)],
        compiler_params=_cparams("parallel", "arbitrary"),
    )(*([proj] * n_in), w, bias, *params, dy, *([conv_out] if kept else []))
    return res[:n_in], res[n_in], res[n_in + 1], res[n_in + 2:]


def _gmlp_chunk(bu, bv, g, b, w0, w1, b0row, b1row):
    u = _gelu(bu)
    vn = _ln64(_gelu(bv), g, b)
    tri = _iota2((CHUNK, CHUNK), 0) >= _iota2((CHUNK, CHUNK), 1)
    m0 = _nn(jnp.where(tri, w0, 0.0), vn) + jnp.broadcast_to(b0row, (CHUNK, CHUNK)).T
    m1 = _nn(jnp.where(tri, w1, 0.0), vn) + jnp.broadcast_to(b1row, (CHUNK, CHUNK)).T
    return u * jnp.where(_lane_lt64((CHUNK, LANES)), m0, m1)


def _gmlp_specs(tm, order):
    def im(f):
        return lambda *ids: f(*order(*ids))
    return dict(
        bu=pl.BlockSpec((tm, LANES), im(lambda j, r: (r, COL_BU + j))),
        bv=pl.BlockSpec((tm, LANES), im(lambda j, r: (r, COL_BV + j))),
        vec=pl.BlockSpec((1, LANES), im(lambda j, r: (0, j))),
        ws=pl.BlockSpec((2, CHUNK, CHUNK), im(lambda j, r: (j, 0, 0))),
        bs=pl.BlockSpec((None, 2, CHUNK), im(lambda j, r: (j, 0, 0))),
        blk=pl.BlockSpec((tm, LANES), im(lambda j, r: (r, j))),
    )


def _gmlp_fwd(proj, ln_g, ln_b, w_s, b_s, name):
    t = proj.shape[0]
    tm = _row_tile(t)
    nch = tm // CHUNK
    sp = _gmlp_specs(tm, lambda r, j: (j, r))

    def body(bu_ref, bv_ref, g_ref, b_ref, ws_ref, bs_ref, o_ref):
        for ci in range(nch):
            rows = pl.ds(ci * CHUNK, CHUNK)
            o_ref[rows, :] = _gmlp_chunk(bu_ref[rows, :], bv_ref[rows, :], g_ref[...], b_ref[...], ws_ref[0], ws_ref[1],
                                         bs_ref[pl.ds(0, 1), :], bs_ref[pl.ds(1, 1), :]).astype(BF16)

    return pl.pallas_call(
        body, name=name, grid=(t // tm, B_WIDTH // LANES),
        in_specs=[sp["bu"], sp["bv"], sp["vec"], sp["vec"], sp["ws"], sp["bs"]],
        out_specs=sp["blk"], out_shape=jax.ShapeDtypeStruct((t, B_WIDTH), BF16),
        compiler_params=_cparams("parallel", "parallel"),
    )(proj, proj, ln_g, ln_b, w_s, b_s.reshape(B_WIDTH // LANES, 2, CHUNK))


def _gmlp_bwd(proj, ln_g, ln_b, w_s, b_s, dy, name, dy_col=0):
    t = proj.shape[0]
    tm = _row_tile(t)
    nch = tm // CHUNK
    sp = _gmlp_specs(tm, lambda j, r: (j, r))
    dy_spec = pl.BlockSpec((tm, LANES), lambda j, r: (r, dy_col + j))

    def body(bu_ref, bv_ref, g_ref, b_ref, ws_ref, bs_ref, dy_ref, dbu_ref, dbv_ref, dg_ref, db_ref, dws_ref, dbs_ref):
        @pl.when(pl.program_id(1) == 0)
        def _():
            for r in (dg_ref, db_ref, dws_ref, dbs_ref):
                r[...] = jnp.zeros_like(r)

        for ci in range(nch):
            rows = pl.ds(ci * CHUNK, CHUNK)
            _, vjp = jax.vjp(_gmlp_chunk, bu_ref[rows, :], bv_ref[rows, :], g_ref[...], b_ref[...],
                             ws_ref[0], ws_ref[1], bs_ref[pl.ds(0, 1), :], bs_ref[pl.ds(1, 1), :])
            dbu, dbv, dg, db, dw0, dw1, db0, db1 = vjp(dy_ref[rows, :])
            dbu_ref[rows, :] = dbu.astype(BF16)
            dbv_ref[rows, :] = dbv.astype(BF16)
            dg_ref[...] += dg
            db_ref[...] += db
            dws_ref[0] += dw0
            dws_ref[1] += dw1
            dbs_ref[pl.ds(0, 1), :] += db0
            dbs_ref[pl.ds(1, 1), :] += db1

    nh = B_WIDTH // LANES
    res = pl.pallas_call(
        body, name=name, grid=(nh, t // tm),
        in_specs=[sp["bu"], sp["bv"], sp["vec"], sp["vec"], sp["ws"], sp["bs"], dy_spec],
        out_specs=[sp["blk"], sp["blk"], sp["vec"], sp["vec"], sp["ws"], sp["bs"]],
        out_shape=[jax.ShapeDtypeStruct((t, B_WIDTH), BF16), jax.ShapeDtypeStruct((t, B_WIDTH), BF16),
                   jax.ShapeDtypeStruct((1, B_WIDTH), F32), jax.ShapeDtypeStruct((1, B_WIDTH), F32),
                   jax.ShapeDtypeStruct(w_s.shape, F32), jax.ShapeDtypeStruct((nh, 2, CHUNK), F32)],
        compiler_params=_cparams("parallel", "arbitrary"),
    )(proj, proj, ln_g, ln_b, w_s, b_s.reshape(nh, 2, CHUNK), dy)
    dbu, dbv, dg, db, dws, dbs = res
    return dbu, dbv, dg, db, dws, dbs.reshape(b_s.shape)


def _tri_apply(a, lower):
    l = a.shape[0]
    r, c = _iota2((l, l), 0), _iota2((l, l), 1)
    t = jnp.where((r >= c) if lower else (r <= c), 1.0, 0.0).astype(BF16)
    hi = a.astype(BF16)
    r1 = a - hi.astype(F32)
    mid = r1.astype(BF16)
    lo = (r1 - mid.astype(F32)).astype(BF16)
    dn = _DN["nn"]
    return (lax.dot_general(t, hi, dn, preferred_element_type=F32) + lax.dot_general(t, mid, dn, preferred_element_type=F32)
            + lax.dot_general(t, lo, dn, preferred_element_type=F32))


@jax.custom_vjp
def _cumsum_rows(a):
    return _tri_apply(a, True)


_cumsum_rows.defvjp(lambda a: (_tri_apply(a, True), None), lambda _, g: (_tri_apply(g, False),))

SSD_GROUP_HEADS = 8
SSD_GROUP_PAIRS = 4


def _ssd_group(x0, x1, x2, x3, dt_raw, bias, alog, bm, cm, p0, p1, p2, p3):
    xs, prevs = (x0, x1, x2, x3), (p0, p1, p2, p3)
    dt = _softplus(dt_raw + bias)
    a = dt * (-jnp.exp(alog))
    acs = _cumsum_rows(a)
    alast = jnp.sum(a, axis=0, keepdims=True)
    dt_t, acs_t = dt.T, acs.T
    cb = _nt(cm, bm)
    tri = _iota2((CHUNK, CHUNK), 0) >= _iota2((CHUNK, CHUNK), 1)
    lane = _iota2((CHUNK, LANES), 1)
    sub = _iota2((LANES, CHUNK), 0)
    lane1 = _iota2((1, LANES), 1)

    def column(v, i):
        return jnp.broadcast_to(jnp.sum(jnp.where(lane == i, v, 0.0), axis=1, keepdims=True), (CHUNK, LANES))

    def row(vt, i):
        return jnp.broadcast_to(jnp.sum(jnp.where(sub == i, vt, 0.0), axis=0, keepdims=True), (CHUNK, CHUNK))

    heads = []
    for i in range(SSD_GROUP_HEADS):
        col_a = column(acs, i)
        al = jnp.sum(jnp.where(lane1 == i, alast, 0.0), axis=1, keepdims=True)
        m = cb * jnp.exp(jnp.where(tri, col_a - row(acs_t, i), NEG)) * row(dt_t, i)
        heads.append((m, jnp.exp(col_a), column(dt, i) * jnp.exp(al - col_a), jnp.exp(al)))
    lo_lanes = _lane_lt64((CHUNK, LANES))
    lo_rows = _iota2((LANES, SSM_STATE), 0) < HEAD_DIM
    ys, news = [], []
    for j in range(SSD_GROUP_PAIRS):
        (m0, ea0, w0, cd0), (m1, ea1, w1, cd1) = heads[2 * j], heads[2 * j + 1]
        x, prev = xs[j], prevs[j]
        ydiag = jnp.where(lo_lanes, _nn(m0, x), _nn(m1, x))
        yoff = jnp.where(lo_lanes, _nt(cm * ea0, prev), _nt(cm * ea1, prev))
        states = jnp.where(lo_rows, _tn(x, bm * w0), _tn(x, bm * w1))
        ys.append(ydiag + yoff)
        news.append(prev * jnp.where(lo_rows, cd0, cd1) + states)
    return tuple(ys) + tuple(news)


SSD_GROUPS = 2


def _ssd2_specs(seq, rev):
    ncs = seq // CHUNK
    wide = SSD_GROUPS * LANES

    def row(s, c):
        return s * ncs + (ncs - 1 - c if rev else c)

    return dict(
        x=pl.BlockSpec((CHUNK, C_WIDTH), lambda s, c: (row(s, c), 0)),
        dt=pl.BlockSpec((CHUNK, wide), lambda s, c: (row(s, c), 0)),
        vec=pl.BlockSpec((1, wide), lambda s, c: (0, 0)),
        bm=pl.BlockSpec((CHUNK, wide), lambda s, c: (row(s, c), C_WIDTH // wide)),
        cm=pl.BlockSpec((CHUNK, wide), lambda s, c: (row(s, c), C_WIDTH // wide + 1)),
        st=pl.BlockSpec((None, C_WIDTH // LANES, LANES, SSM_STATE), lambda s, c: (row(s, c), 0, 0, 0)),
        ncs=ncs,
    )


def _lane_blocks(ref, grp):
    return [ref[:, pl.ds((grp * SSD_GROUP_PAIRS + j) * LANES, LANES)] for j in range(SSD_GROUP_PAIRS)]


def _group_block(ref, grp):
    return ref[:, pl.ds(grp * LANES, LANES)]


def _ssd2_fwd(xbc_act, dt_raw, dt_bias, a_log, seq, name):
    t = xbc_act.shape[0]
    sp = _ssd2_specs(seq, False)

    npair = SSD_GROUP_PAIRS

    def body(x_ref, dt_ref, bias_ref, alog_ref, bm_ref, cm_ref, y_ref, prev_ref, state):
        @pl.when(pl.program_id(1) == 0)
        def _():
            state[...] = jnp.zeros_like(state)

        for grp in range(SSD_GROUPS):
            prevs = [state[grp * npair + j] for j in range(npair)]
            for j in range(npair):
                prev_ref[grp * npair + j] = prevs[j]
            res = _ssd_group(*_lane_blocks(x_ref, grp), _group_block(dt_ref, grp), _group_block(bias_ref, grp),
                             _group_block(alog_ref, grp), _group_block(bm_ref, grp), _group_block(cm_ref, grp), *prevs)
            for j in range(npair):
                y_ref[:, pl.ds((grp * npair + j) * LANES, LANES)] = res[j]
                state[grp * npair + j] = res[npair + j]

    return pl.pallas_call(
        body, name=name, grid=(t // seq, sp["ncs"]),
        in_specs=[sp["x"], sp["dt"], sp["vec"], sp["vec"], sp["bm"], sp["cm"]],
        out_specs=[sp["x"], sp["st"]],
        out_shape=[jax.ShapeDtypeStruct((t, C_WIDTH), F32),
                   jax.ShapeDtypeStruct((t // CHUNK, C_WIDTH // LANES, LANES, SSM_STATE), F32)],
        scratch_shapes=[pltpu.VMEM((C_WIDTH // LANES, LANES, SSM_STATE), F32)],
        compiler_params=_cparams("parallel", "arbitrary"),
    )(xbc_act, dt_raw, dt_bias, a_log, xbc_act, xbc_act)


def _ssd2_bwd(xbc_act, dt_raw, dt_bias, a_log, prev_saved, dy, seq, name):
    t = xbc_act.shape[0]
    sp = _ssd2_specs(seq, True)
    npair = SSD_GROUP_PAIRS

    def body(x_ref, dt_ref, bias_ref, alog_ref, bm_ref, cm_ref, prev_ref, dy_ref,
             dx_ref, ddt_ref, dbias_ref, dalog_ref, dbm_ref, dcm_ref, dstate):
        @pl.when(pl.program_id(1) == 0)
        def _():
            dstate[...] = jnp.zeros_like(dstate)

        @pl.when(jnp.logical_and(pl.program_id(0) == 0, pl.program_id(1) == 0))
        def _():
            dbias_ref[...] = jnp.zeros_like(dbias_ref)
            dalog_ref[...] = jnp.zeros_like(dalog_ref)

        for grp in range(SSD_GROUPS):
            lanes = pl.ds(grp * LANES, LANES)
            _, vjp = jax.vjp(_ssd_group, *_lane_blocks(x_ref, grp), _group_block(dt_ref, grp), _group_block(bias_ref, grp),
                             _group_block(alog_ref, grp), _group_block(bm_ref, grp), _group_block(cm_ref, grp),
                             *[prev_ref[grp * npair + j] for j in range(npair)])
            grads = vjp(tuple(_lane_blocks(dy_ref, grp)) + tuple(dstate[grp * npair + j] for j in range(npair)))
            for j in range(npair):
                dx_ref[:, pl.ds((grp * npair + j) * LANES, LANES)] = grads[j]
                dstate[grp * npair + j] = grads[npair + 5 + j]
            ddt_ref[:, lanes] = grads[npair].astype(BF16)
            dbias_ref[:, lanes] += grads[npair + 1]
            dalog_ref[:, lanes] += grads[npair + 2]
            dbm_ref[:, lanes] = grads[npair + 3]
            dcm_ref[:, lanes] = grads[npair + 4]

    return pl.pallas_call(
        body, name=name, grid=(t // seq, sp["ncs"]),
        in_specs=[sp["x"], sp["dt"], sp["vec"], sp["vec"], sp["bm"], sp["cm"], sp["st"], sp["x"]],
        out_specs=[sp["x"], sp["dt"], sp["vec"], sp["vec"], sp["dt"], sp["dt"]],
        out_shape=[jax.ShapeDtypeStruct((t, C_WIDTH), F32), jax.ShapeDtypeStruct((t, 2 * LANES), BF16),
                   jax.ShapeDtypeStruct((1, 2 * LANES), F32), jax.ShapeDtypeStruct((1, 2 * LANES), F32),
                   jax.ShapeDtypeStruct((t, 2 * SSM_STATE), F32), jax.ShapeDtypeStruct((t, 2 * SSM_STATE), F32)],
        scratch_shapes=[pltpu.VMEM((C_WIDTH // LANES, LANES, SSM_STATE), F32)],
        compiler_params=_cparams("arbitrary", "arbitrary"),
    )(xbc_act, dt_raw, dt_bias, a_log, xbc_act, xbc_act, prev_saved, dy)


def _ssd2_assemble(dxs_ssd, dxs_skip, dbm, dcm, name):
    t = dxs_ssd.shape[0]
    tm = _row_tile(t)

    def body(a_ref, b_ref, dbm_ref, dcm_ref, o_ref):
        o_ref[:, pl.ds(0, C_WIDTH)] = a_ref[...] + b_ref[...]
        o_ref[:, pl.ds(C_WIDTH, 2 * SSM_STATE)] = dbm_ref[...]
        o_ref[:, pl.ds(C_WIDTH + 2 * SSM_STATE, 2 * SSM_STATE)] = dcm_ref[...]

    wide = pl.BlockSpec((tm, C_WIDTH), lambda i: (i, 0))
    narrow = pl.BlockSpec((tm, 2 * SSM_STATE), lambda i: (i, 0))
    return pl.pallas_call(
        body, name=name, grid=(t // tm,), in_specs=[wide, wide, narrow, narrow],
        out_specs=pl.BlockSpec((tm, D_CONV_C), lambda i: (i, 0)),
        out_shape=jax.ShapeDtypeStruct((t, D_CONV_C), F32),
        compiler_params=_cparams("parallel"),
    )(dxs_ssd, dxs_skip, dbm, dcm)


def _expand_mats():
    head = jnp.arange(LANES)[:, None]
    e64 = (head == (jnp.arange(C_WIDTH)[None, :] // HEAD_DIM)).astype(F32)
    e128 = (head == (jnp.arange(C_HEADS * LANES)[None, :] // LANES)).astype(F32)
    return e64, e128


def _ssd_prep_fn(dt_raw, dt_bias, a_log, e64, e128):
    dt = _softplus(dt_raw + dt_bias)
    a = dt * (-jnp.exp(a_log))
    incl = (_iota2((CHUNK, CHUNK), 0) >= _iota2((CHUNK, CHUNK), 1)).astype(F32)
    acs = _xdot(incl, a)
    alast = _xdot(jnp.ones((CHUNK, CHUNK), F32), a)
    return _xdot(dt, e64), _xdot(acs, e64), _xdot(alast, e64), _xdot(acs, e128)


def _ssd_prep_specs():
    blk = lambda w: pl.BlockSpec((CHUNK, w), lambda i: (i, 0))
    const = lambda r, w: pl.BlockSpec((r, w), lambda i: (0, 0))
    ins = [blk(LANES), const(1, LANES), const(1, LANES), const(LANES, C_WIDTH), const(LANES, C_HEADS * LANES)]
    outs = [blk(C_WIDTH), blk(C_WIDTH), blk(C_WIDTH), blk(C_HEADS * LANES)]
    return ins, outs


def _ssd_prep_fwd(dt_raw, dt_bias, a_log, name):
    t = dt_raw.shape[0]
    e64, e128 = _expand_mats()
    ins, outs = _ssd_prep_specs()

    def body(raw_ref, bias_ref, alog_ref, e64_ref, e128_ref, dt_ref, acs_ref, alast_ref, acs128_ref):
        res = _ssd_prep_fn(raw_ref[...], bias_ref[...], alog_ref[...], e64_ref[...], e128_ref[...])
        for ref, v in zip((dt_ref, acs_ref, alast_ref, acs128_ref), res):
            ref[...] = v

    return pl.pallas_call(
        body, name=name, grid=(t // CHUNK,), in_specs=ins, out_specs=outs,
        out_shape=[jax.ShapeDtypeStruct((t, C_WIDTH), F32)] * 3 + [jax.ShapeDtypeStruct((t, C_HEADS * LANES), F32)],
        compiler_params=_cparams("parallel"),
    )(dt_raw, dt_bias, a_log, e64, e128)


def _ssd_prep_bwd(dt_raw, dt_bias, a_log, d_dt, d_acs, d_alast, d_acs128, name):
    t = dt_raw.shape[0]
    e64, e128 = _expand_mats()
    ins, outs = _ssd_prep_specs()
    vec = pl.BlockSpec((1, LANES), lambda i: (0, 0))

    def body(raw_ref, bias_ref, alog_ref, e64_ref, e128_ref, g0, g1, g2, g3, draw_ref, dbias_ref, dalog_ref):
        @pl.when(pl.program_id(0) == 0)
        def _():
            dbias_ref[...] = jnp.zeros_like(dbias_ref)
            dalog_ref[...] = jnp.zeros_like(dalog_ref)

        e64v, e128v = e64_ref[...], e128_ref[...]
        _, vjp = jax.vjp(lambda r, b, al: _ssd_prep_fn(r, b, al, e64v, e128v),
                         raw_ref[...], bias_ref[...], alog_ref[...])
        draw, dbias, dalog = vjp((g0[...], g1[...], g2[...], g3[...]))
        draw_ref[...] = draw.astype(BF16)
        dbias_ref[...] += dbias
        dalog_ref[...] += dalog

    return pl.pallas_call(
        body, name=name, grid=(t // CHUNK,), in_specs=ins + outs,
        out_specs=[pl.BlockSpec((CHUNK, LANES), lambda i: (i, 0)), vec, vec],
        out_shape=[jax.ShapeDtypeStruct((t, LANES), BF16), jax.ShapeDtypeStruct((1, LANES), F32),
                   jax.ShapeDtypeStruct((1, LANES), F32)],
        compiler_params=_cparams("arbitrary"),
    )(dt_raw, dt_bias, a_log, e64, e128, d_dt, d_acs, d_alast, d_acs128)


def _ssd_chunk(x, dt, acs, alast, col0, col1, bm, cm, prev):
    xdt = x * dt
    cb = _nt(cm, bm)
    tri = _iota2((CHUNK, CHUNK), 0) >= _iota2((CHUNK, CHUNK), 1)
    l0 = jnp.exp(jnp.where(tri, col0 - col0.T, NEG))
    l1 = jnp.exp(jnp.where(tri, col1 - col1.T, NEG))
    ydiag = jnp.where(_lane_lt64((CHUNK, LANES)), _nn(cb * l0, xdt), _nn(cb * l1, xdt))
    states = _tn(xdt * jnp.exp(alast - acs), bm)
    yoff = _nt(cm, prev) * jnp.exp(acs)
    new = prev * jnp.exp(alast).T + states
    return ydiag + yoff, new


def _ssd_specs(seq, rev):
    ncs = seq // CHUNK
    npair = C_WIDTH // LANES

    def row(s, c):
        return s * ncs + (ncs - 1 - c if rev else c)

    return dict(
        x=pl.BlockSpec((CHUNK, LANES), lambda s, j, c: (row(s, c), j)),
        bm=pl.BlockSpec((CHUNK, SSM_STATE), lambda s, j, c: (row(s, c), C_WIDTH // LANES + j // 4)),
        cm=pl.BlockSpec((CHUNK, SSM_STATE), lambda s, j, c: (row(s, c), C_WIDTH // LANES + 2 + j // 4)),
        col=pl.BlockSpec((CHUNK, 2 * LANES), lambda s, j, c: (row(s, c), j)),
        st=pl.BlockSpec((None, None, LANES, SSM_STATE), lambda s, j, c: (row(s, c), j, 0, 0)),
        npair=npair, ncs=ncs,
    )


def _ssd_fwd(xbc_act, dt64, acs64, alast64, acs128, seq, name):
    t = xbc_act.shape[0]
    sp = _ssd_specs(seq, False)

    def body(x_ref, dt_ref, acs_ref, alast_ref, col_ref, bm_ref, cm_ref, y_ref, prev_ref, state):
        @pl.when(pl.program_id(2) == 0)
        def _():
            state[...] = jnp.zeros_like(state)

        prev = state[...]
        prev_ref[...] = prev
        y, new = _ssd_chunk(x_ref[...], dt_ref[...], acs_ref[...], alast_ref[...],
                            col_ref[:, pl.ds(0, LANES)], col_ref[:, pl.ds(LANES, LANES)],
                            bm_ref[...], cm_ref[...], prev)
        y_ref[...] = y
        state[...] = new

    return pl.pallas_call(
        body, name=name, grid=(t // seq, sp["npair"], sp["ncs"]),
        in_specs=[sp["x"], sp["x"], sp["x"], sp["x"], sp["col"], sp["bm"], sp["cm"]],
        out_specs=[sp["x"], sp["st"]],
        out_shape=[jax.ShapeDtypeStruct((t, C_WIDTH), F32),
                   jax.ShapeDtypeStruct((t // CHUNK, sp["npair"], LANES, SSM_STATE), F32)],
        scratch_shapes=[pltpu.VMEM((LANES, SSM_STATE), F32)],
        compiler_params=_cparams("parallel", "parallel", "arbitrary"),
    )(xbc_act, dt64, acs64, alast64, acs128, xbc_act, xbc_act)


def _ssd_bwd(xbc_act, dt64, acs64, alast64, acs128, prev_saved, dy, seq, name):
    t = xbc_act.shape[0]
    sp = _ssd_specs(seq, True)

    def body(x_ref, dt_ref, acs_ref, alast_ref, col_ref, bm_ref, cm_ref, prev_ref, dy_ref,
             dx_ref, ddt_ref, dacs_ref, dalast_ref, dcol_ref, dbc_ref, dstate):
        @pl.when(pl.program_id(2) == 0)
        def _():
            dstate[...] = jnp.zeros_like(dstate)

        _, vjp = jax.vjp(_ssd_chunk, x_ref[...], dt_ref[...], acs_ref[...], alast_ref[...],
                         col_ref[:, pl.ds(0, LANES)], col_ref[:, pl.ds(LANES, LANES)],
                         bm_ref[...], cm_ref[...], prev_ref[...])
        dx, ddt, dacs, dalast, dc0, dc1, dbm, dcm, dprev = vjp((dy_ref[...], dstate[...]))
        dx_ref[...] = dx
        ddt_ref[...] = ddt
        dacs_ref[...] = dacs
        dalast_ref[...] = dalast
        dcol_ref[:, pl.ds(0, LANES)] = dc0
        dcol_ref[:, pl.ds(LANES, LANES)] = dc1
        dbc_ref[:, pl.ds(0, SSM_STATE)] = dbm
        dbc_ref[:, pl.ds(SSM_STATE, SSM_STATE)] = dcm
        dstate[...] = dprev

    wide = jax.ShapeDtypeStruct((t, C_WIDTH), F32)
    return pl.pallas_call(
        body, name=name, grid=(t // seq, sp["npair"], sp["ncs"]),
        in_specs=[sp["x"], sp["x"], sp["x"], sp["x"], sp["col"], sp["bm"], sp["cm"], sp["st"], sp["x"]],
        out_specs=[sp["x"], sp["x"], sp["x"], sp["x"], sp["col"], sp["col"]],
        out_shape=[wide, wide, wide, wide, jax.ShapeDtypeStruct((t, 2 * C_WIDTH), F32),
                   jax.ShapeDtypeStruct((t, 2 * C_WIDTH), F32)],
        scratch_shapes=[pltpu.VMEM((LANES, SSM_STATE), F32)],
        compiler_params=_cparams("parallel", "parallel", "arbitrary"),
    )(xbc_act, dt64, acs64, alast64, acs128, xbc_act, xbc_act, prev_saved, dy)


def _ssd_post_fn(y, xs, z, dskip, g):
    v = (y + dskip * xs) * _silu(z)
    return v * lax.rsqrt(jnp.mean(v * v, axis=-1, keepdims=True) + EPS) * g


def _ssd_post_specs(tm, order):
    gw = C_WIDTH // 2

    def im(f):
        return lambda *ids: f(*order(*ids))
    return dict(
        blk=pl.BlockSpec((tm, gw), im(lambda g, r: (r, g))),
        z=pl.BlockSpec((tm, gw), im(lambda g, r: (r, COL_Z * LANES // gw + g))),
        vec=pl.BlockSpec((1, gw), im(lambda g, r: (0, g))),
    )


def _ssd_post_fwd(y_ssd, xbc_act, proj, dskip64, norm_g, name):
    t = y_ssd.shape[0]
    tm = _row_tile(t)
    sp = _ssd_post_specs(tm, lambda r, g: (g, r))

    def body(y_ref, xs_ref, z_ref, ds_ref, g_ref, o_ref):
        o_ref[...] = _ssd_post_fn(y_ref[...], xs_ref[...], z_ref[...], ds_ref[...], g_ref[...]).astype(BF16)

    return pl.pallas_call(
        body, name=name, grid=(t // tm, 2),
        in_specs=[sp["blk"], sp["blk"], sp["z"], sp["vec"], sp["vec"]], out_specs=sp["blk"],
        out_shape=jax.ShapeDtypeStruct((t, C_WIDTH), BF16),
        compiler_params=_cparams("parallel", "parallel"),
    )(y_ssd, xbc_act, proj, dskip64, norm_g)


def _ssd_post_bwd(y_ssd, xbc_act, proj, dskip64, norm_g, dyc, name, dy_col=0):
    t = y_ssd.shape[0]
    tm = _row_tile(t)
    sp = _ssd_post_specs(tm, lambda g, r: (g, r))
    dy_spec = pl.BlockSpec((tm, C_WIDTH // 2), lambda g, r: (r, dy_col + g))

    def body(y_ref, xs_ref, z_ref, ds_ref, g_ref, dyc_ref, dy_ref, dxs_ref, dz_ref, dds_ref, dg_ref):
        @pl.when(pl.program_id(1) == 0)
        def _():
            dds_ref[...] = jnp.zeros_like(dds_ref)
            dg_ref[...] = jnp.zeros_like(dg_ref)

        _, vjp = jax.vjp(_ssd_post_fn, y_ref[...], xs_ref[...], z_ref[...], ds_ref[...], g_ref[...])
        dy, dxs, dz, dds, dg = vjp(dyc_ref[...])
        dy_ref[...] = dy
        dxs_ref[...] = dxs
        dz_ref[...] = dz.astype(BF16)
        dds_ref[...] += dds
        dg_ref[...] += dg

    wide = jax.ShapeDtypeStruct((t, C_WIDTH), F32)
    vec = jax.ShapeDtypeStruct((1, C_WIDTH), F32)
    return pl.pallas_call(
        body, name=name, grid=(2, t // tm),
        in_specs=[sp["blk"], sp["blk"], sp["z"], sp["vec"], sp["vec"], dy_spec],
        out_specs=[sp["blk"], sp["blk"], sp["blk"], sp["vec"], sp["vec"]],
        out_shape=[wide, wide, jax.ShapeDtypeStruct((t, C_WIDTH), BF16), vec, vec],
        compiler_params=_cparams("parallel", "arbitrary"),
    )(y_ssd, xbc_act, proj, dskip64, norm_g, dyc)


def _ssd_assemble(dxs_ssd, dxs_skip, dbc, name):
    t = dxs_ssd.shape[0]
    tm = _row_tile(t)
    npair = C_WIDTH // LANES

    def body(a_ref, b_ref, dbc_ref, o_ref):
        o_ref[:, pl.ds(0, C_WIDTH)] = a_ref[...] + b_ref[...]
        for grp in range(2):
            for which in range(2):
                acc = jnp.zeros((tm, SSM_STATE), F32)
                for j in range(grp * npair // 2, (grp + 1) * npair // 2):
                    acc = acc + dbc_ref[:, pl.ds((2 * j + which) * SSM_STATE, SSM_STATE)]
                o_ref[:, pl.ds(C_WIDTH + (2 * which + grp) * SSM_STATE, SSM_STATE)] = acc

    return pl.pallas_call(
        body, name=name, grid=(t // tm,),
        in_specs=[pl.BlockSpec((tm, C_WIDTH), lambda i: (i, 0))] * 2 + [pl.BlockSpec((tm, 2 * C_WIDTH), lambda i: (i, 0))],
        out_specs=pl.BlockSpec((tm, D_CONV_C), lambda i: (i, 0)),
        out_shape=jax.ShapeDtypeStruct((t, D_CONV_C), F32),
        compiler_params=_cparams("parallel"),
    )(dxs_ssd, dxs_skip, dbc)


def _pad_taps(w):
    return jnp.pad(w, ((0, CONV_PAD - w.shape[0]), (0, 0)))


def _pad_heads(v):
    return jnp.pad(v, (0, LANES - v.shape[0])).reshape(1, LANES)


def _group_heads(a):
    pad = [(0, 0)] * (a.ndim - 1) + [(0, LANES - SSD_GROUP_HEADS)]
    return jnp.concatenate([jnp.pad(a[..., :SSD_GROUP_HEADS], pad), jnp.pad(a[..., SSD_GROUP_HEADS:], pad)], axis=-1)


def _ungroup_heads(a):
    return jnp.concatenate([a[..., :SSD_GROUP_HEADS], a[..., LANES:LANES + SSD_GROUP_HEADS]], axis=-1)


def _layer_fwd(x, p, seq, li, after=()):
    n = f"l{li}_"
    h1 = _rms_fwd(x, p["norm1_g"], n + "rms1", after=after)
    proj = _matmul(h1, p["w_main"], mode="nn", name=n + "inproj")
    dt_raw = _matmul(h1, p["w_dt"], mode="nn", name=n + "inproj_dt")
    row = lambda v: v.reshape(1, -1)
    ya, conv_a = _conv_fwd("a", proj, _pad_taps(p["conv_a_w"]), row(p["conv_a_b"]), (row(p["ln_a_g"]), row(p["ln_a_b"])),
                           seq, n + "conva", out_dtype=BF16, keep_conv=True)
    yb = _gmlp_fwd(proj, row(p["ln_b_g"]), row(p["ln_b_b"]), p["w_spatial"], p["b_spatial"], n + "gmlp")
    xbc_act = _conv_fwd("c", proj, _pad_taps(p["conv_c_w"]), row(p["conv_c_b"]), (), seq, n + "convc")
    y_ssd, prev = _ssd2_fwd(xbc_act, dt_raw, _group_heads(row(p["dt_bias"])), _group_heads(row(p["a_log"])), seq, n + "ssd")
    dskip64 = jnp.repeat(p["d_skip"], HEAD_DIM).reshape(1, C_WIDTH)
    yc = _ssd_post_fwd(y_ssd, xbc_act, proj, dskip64, row(p["norm_c_g"]), n + "ssdpost")
    ycat = jnp.concatenate([ya, yb, yc], axis=1)
    x1 = _matmul(ycat, p["w_out"], mode="nn", name=n + "outproj", add=x)
    h2 = _rms_fwd(x1, p["norm2_g"], n + "rms2")
    u, act = _matmul(h2, p["w_ff1"], mode="nn", name=n + "ff1", epilogue=_relu2_epilogue, out_dtypes=(F32, BF16),
                     b_chips=True)
    x2 = _matmul(act, p["w_ff2"], mode="nn", name=n + "ff2", add=x1)
    saved = dict(x=x, h1=h1, proj=proj, conv_a=conv_a, dt_raw=dt_raw, xbc_act=xbc_act, prev=prev, y_ssd=y_ssd,
                 dskip64=dskip64, ycat=ycat, x1=x1, h2=h2, u=u, act=act)
    return x2, saved


def _layer_bwd(dx2, p, s, seq, li, after=(), on_ffn_grads=None):
    n = f"l{li}_b_"
    row = lambda v: v.reshape(1, -1)
    g = {}
    du = _matmul(dx2, p["w_ff2"], mode="nt", name=n + "ff2_dx", epilogue=_relu2_bwd_epilogue, extra=s["u"],
                 out_dtypes=(BF16,), after=after)
    g["w_ff2"] = _matmul(s["act"], dx2, mode="tn", name=n + "ff2_dw")
    g["w_ff1"] = _matmul(s["h2"], du, mode="tn", name=n + "ff1_dw", out_chips=True)
    dh2 = _matmul(du, p["w_ff1"], mode="nt", name=n + "ff1_dx", b_chips=True)
    dx1, g["norm2_g"] = _rms_bwd(s["x1"], p["norm2_g"], dh2, dx2, n + "rms2")
    g["w_out"] = _matmul(s["ycat"], dx1, mode="tn", name=n + "out_dw")
    dycat = _matmul(dx1, p["w_out"], mode="nt", name=n + "out_dx",
                    after=() if on_ffn_grads is None else on_ffn_grads(g))
    proj = s["proj"]
    (dval, dgate), dwa, dba, (dlag, dlab) = _conv_bwd(
        "a", proj, _pad_taps(p["conv_a_w"]), row(p["conv_a_b"]), (row(p["ln_a_g"]), row(p["ln_a_b"])), dycat, seq,
        n + "conva", dy_col=0, conv_out=s["conv_a"])
    g["conv_a_w"], g["conv_a_b"], g["ln_a_g"], g["ln_a_b"] = dwa[:CONV_A_K], dba[0], dlag[0], dlab[0]
    dbu, dbv, dlbg, dlbb, g["w_spatial"], g["b_spatial"] = _gmlp_bwd(
        proj, row(p["ln_b_g"]), row(p["ln_b_b"]), p["w_spatial"], p["b_spatial"], dycat, n + "gmlp",
        dy_col=A_WIDTH // LANES)
    g["ln_b_g"], g["ln_b_b"] = dlbg[0], dlbb[0]
    dy_ssd, dxs_skip, dz, dds, dncg = _ssd_post_bwd(s["y_ssd"], s["xbc_act"], proj, s["dskip64"], row(p["norm_c_g"]),
                                                    dycat, n + "ssdpost", dy_col=(A_WIDTH + B_WIDTH) * 2 // C_WIDTH)
    g["norm_c_g"] = dncg[0]
    g["d_skip"] = dds.reshape(C_HEADS, HEAD_DIM).sum(axis=1)
    dxs, ddt_raw, ddtb, dalog, dbm, dcm = _ssd2_bwd(
        s["xbc_act"], s["dt_raw"], _group_heads(row(p["dt_bias"])), _group_heads(row(p["a_log"])), s["prev"], dy_ssd, seq,
        n + "ssd")
    g["dt_bias"], g["a_log"] = _ungroup_heads(ddtb)[0], _ungroup_heads(dalog)[0]
    dconv = _ssd2_assemble(dxs, dxs_skip, dbm, dcm, n + "ssdasm")
    (dxbc,), dwc, dbcv, _ = _conv_bwd("c", proj, _pad_taps(p["conv_c_w"]), row(p["conv_c_b"]), (), dconv, seq, n + "convc")
    g["conv_c_w"], g["conv_c_b"] = dwc[:CONV_C_K], dbcv[0]
    dproj = jnp.concatenate([dval, dgate, dbu, dbv, dz, dxbc], axis=1)
    g["w_main"] = _matmul(s["h1"], dproj, mode="tn", name=n + "in_dw")
    g["w_dt"] = _matmul(s["h1"], ddt_raw, mode="tn", name=n + "indt_dw")
    dh1 = _matmul(dproj, p["w_main"], mode="nt", name=n + "in_dx")
    dh1 = _matmul(ddt_raw, p["w_dt"], mode="nt", name=n + "indt_dx", add=dh1)
    dx, g["norm1_g"] = _rms_bwd(s["x"], p["norm1_g"], dh1, dx1, n + "rms1")
    return dx, g


EW_BLOCK_BYTES = 1 << 20


def _ew(fn, ins, out_dtypes, name, leads=None):
    leads = leads or [None] * len(ins)
    rows, c = ins[0].shape[-2:]
    tr = _pick(rows, [t for t in (2048, 1024, 512, 256, 128, 64, 32, 16, 8) if t * c * 4 <= EW_BLOCK_BYTES])
    n_in = len(ins)

    def spec(lead):
        if lead is None:
            return pl.BlockSpec((tr, c), lambda i: (i, 0))
        return pl.BlockSpec((None, tr, c), functools.partial(lambda i, k: (k, i, 0), k=lead))

    def body(*refs):
        outs = fn(*[r[...].astype(F32) for r in refs[:n_in]])
        for o_ref, o in zip(refs[n_in:], outs):
            o_ref[...] = o.astype(o_ref.dtype)

    return pl.pallas_call(
        body, name=name, grid=(rows // tr,),
        in_specs=[spec(l) for l in leads], out_specs=[spec(None)] * len(out_dtypes),
        out_shape=[jax.ShapeDtypeStruct((rows, c), dt) for dt in out_dtypes],
        compiler_params=_cparams("parallel"),
    )(*ins)


def _adam_fn(w, g, m, v):
    m2 = ADAM_B1 * m + (1.0 - ADAM_B1) * g
    v2 = ADAM_B2 * v + (1.0 - ADAM_B2) * (g * g)
    m_hat = m2 / (1.0 - ADAM_B1 ** ADAM_STEP)
    v_hat = v2 / (1.0 - ADAM_B2 ** ADAM_STEP)
    delta = -ADAM_LR * (m_hat / (jnp.sqrt(v_hat) + ADAM_EPS) + ADAM_WD * w)
    return delta, m2, v2


def _adam(w, g, m, v, name):
    shape = w.shape
    two_d = lambda a: a.reshape(-1, shape[-1])
    outs = _ew(_adam_fn, [two_d(w), two_d(g), two_d(m), two_d(v)], (F32, F32, F32), name)
    return [o.reshape(shape) for o in outs]


_ANY = pl.BlockSpec(memory_space=pl.ANY)


def _mesh_pos():
    return lax.axis_index("x"), lax.axis_index("y"), lax.axis_index("c")


def _peer_chips(x, y):
    return [(1 - x, y), (x, 1 - y), (1 - x, 1 - y)]


def _remote(src, dst, send_sems, recv_sems, sem, to):
    return pltpu.make_async_remote_copy(src_ref=src, dst_ref=dst, send_sem=send_sems.at[sem],
                                        recv_sem=recv_sems.at[sem], device_id=to, device_id_type=MESH)


def _half_rows(n_rows, which):
    half = n_rows // 2
    return pl.ds(pl.multiple_of(which * half, 8), half)


def _comm_call(body, ins, out_shapes, n_sems, name):
    scratch = [pltpu.SemaphoreType.DMA((n_sems,)), pltpu.SemaphoreType.DMA((n_sems,))]
    return pl.pallas_call(
        body, name=name, in_specs=[_ANY] * len(ins), out_specs=[_ANY] * len(out_shapes),
        out_shape=out_shapes, scratch_shapes=scratch,
    )(*ins)


def _gather_weights(big, small, name):
    nb, ns = len(big), len(small)
    n = nb + ns

    def body(*refs):
        ins, outs = refs[:n], refs[n:2 * n]
        send_sems, recv_sems = refs[2 * n:]
        x, y, c = _mesh_pos()
        q = 2 * x + y
        me, sib = (x, y, c), (x, y, 1 - c)
        chips = _peer_chips(x, y)
        rem = functools.partial(_remote, send_sems=send_sems, recv_sems=recv_sems)
        first = []
        for i in range(nb):
            mine = _half_rows(big[i].shape[0], c)
            for k, (px, py) in enumerate(chips):
                first.append(rem(ins[i].at[mine], outs[i].at[q, mine], sem=6 * i + k, to=(px, py, c)))
        for j in range(ns):
            for k, (px, py) in enumerate(chips):
                first.append(rem(ins[nb + j], outs[nb + j].at[q], sem=6 * nb + 3 * j + k, to=(px, py, c)))
        for cp in first:
            cp.start()
        passed = []
        for i in range(nb):
            mine = _half_rows(big[i].shape[0], c)
            for k, (px, py) in enumerate(chips):
                landed = outs[i].at[2 * px + py, mine]
                rem(landed, landed, sem=6 * i + k, to=me).wait_recv()
                fwd = rem(landed, landed, sem=6 * i + 3 + k, to=sib)
                fwd.start()
                passed.append(fwd)
        for i in range(nb):
            other = _half_rows(big[i].shape[0], 1 - c)
            for k, (px, py) in enumerate(chips):
                theirs = outs[i].at[2 * px + py, other]
                rem(theirs, theirs, sem=6 * i + 3 + k, to=me).wait_recv()
        for j in range(ns):
            for k, (px, py) in enumerate(chips):
                dst = outs[nb + j].at[2 * px + py]
                rem(dst, dst, sem=6 * nb + 3 * j + k, to=me).wait_recv()
        for cp in first + passed:
            cp.wait_send()

    out_shapes = [jax.ShapeDtypeStruct((N_CHIPS,) + a.shape, a.dtype) for a in list(big) + list(small)]
    return _comm_call(body, list(big) + list(small), out_shapes, 6 * nb + 3 * ns, name)


def _sibling_other_halves(gs, name):
    n = len(gs)

    def other_half(ref, shape, c):
        rows = _half_rows(shape[-2], 1 - c)
        return ref.at[rows] if len(shape) == 2 else ref.at[:, rows]

    def body(*refs):
        ins, outs = refs[:n], refs[n:2 * n]
        send_sems, recv_sems = refs[2 * n:]
        x, y, c = _mesh_pos()
        copies = [_remote(other_half(ins[i], gs[i].shape, c), outs[i], send_sems, recv_sems, i, (x, y, 1 - c))
                  for i in range(n)]
        for cp in copies:
            cp.start()
        for cp in copies:
            cp.wait()

    out_shapes = [jax.ShapeDtypeStruct(g.shape[:-2] + (g.shape[-2] // 2, g.shape[-1]), g.dtype) for g in gs]
    return _comm_call(body, list(gs), out_shapes, n, name)


IN_SHARD = D_IN_PROJ // N_CHIPS


def _chipsum_in(mine, mine_dt, theirs, theirs_dt, name):
    r = mine.shape[0]
    tr = _pick(r, (128, 64, 32, 16, 8))
    last = D_MAIN - (N_CHIPS - 1) * IN_SHARD

    def body(a_ref, adt_ref, b_ref, bdt_ref, o32_ref, o16_ref):
        for p in range(N_CHIPS):
            wid = IN_SHARD if p < N_CHIPS - 1 else last
            s = a_ref[:, pl.ds(IN_SHARD * p, wid)] + b_ref[:, pl.ds(IN_SHARD * p, wid)]
            o32_ref[p, :, pl.ds(0, wid)] = s
            o16_ref[p, :, pl.ds(0, wid)] = s.astype(BF16)
        for grp in range(2):
            src = pl.ds(grp * LANES, SSD_GROUP_HEADS)
            s = adt_ref[:, src] + bdt_ref[:, src]
            dst = pl.ds(last + grp * SSD_GROUP_HEADS, SSD_GROUP_HEADS)
            o32_ref[N_CHIPS - 1, :, dst] = s
            o16_ref[N_CHIPS - 1, :, dst] = s.astype(BF16)

    wide = pl.BlockSpec((tr, D_MAIN), lambda i: (i, 0))
    narrow = pl.BlockSpec((tr, 2 * LANES), lambda i: (i, 0))
    out = pl.BlockSpec((N_CHIPS, tr, IN_SHARD), lambda i: (0, i, 0))
    return pl.pallas_call(
        body, name=name, grid=(r // tr,), in_specs=[wide, narrow, wide, narrow], out_specs=[out, out],
        out_shape=[jax.ShapeDtypeStruct((N_CHIPS, r, IN_SHARD), F32), jax.ShapeDtypeStruct((N_CHIPS, r, IN_SHARD), BF16)],
        compiler_params=_cparams("parallel"),
    )(mine, mine_dt, theirs, theirs_dt)


def _chip_scatter(cs, name):
    n = len(cs)

    def body(*refs):
        ins, outs = refs[:n], refs[n:2 * n]
        send_sems, recv_sems = refs[2 * n:]
        x, y, c = _mesh_pos()
        copies = []
        for i in range(n):
            for k, (px, py) in enumerate(_peer_chips(x, y)):
                copies.append(_remote(ins[i].at[2 * px + py], outs[i].at[k], send_sems, recv_sems, 3 * i + k, (px, py, c)))
        for cp in copies:
            cp.start()
        for cp in copies:
            cp.wait()

    out_shapes = [jax.ShapeDtypeStruct((3,) + a.shape[1:], a.dtype) for a in cs]
    return _comm_call(body, list(cs), out_shapes, 3 * n, name)


_HBM = pl.BlockSpec(memory_space=pltpu.HBM)
_SEM = pl.BlockSpec(memory_space=pltpu.SEMAPHORE)


def _in_hbm(a):
    return pltpu.with_memory_space_constraint(a, pltpu.HBM)


def _split_plan(kind, srcs, lands, x, y, c):
    plan = []
    for src, land in zip(srcs, lands):
        if kind == "sibling":
            rows = _half_rows(src.shape[-2], 1 - c)
            plan.append((src.at[rows] if len(src.shape) == 2 else src.at[:, rows], land, (x, y, 1 - c)))
            continue
        if kind == "allgather":
            peers = [(x, y, 1 - c)] + [(px, py, pc) for px, py in _peer_chips(x, y) for pc in (c, 1 - c)]
            plan += [(src, land.at[4 * x + 2 * y + c], peer) for peer in peers]
            continue
        for k, (px, py) in enumerate(_peer_chips(x, y)):
            if kind == "scatter":
                plan.append((src.at[2 * px + py], land.at[k], (px, py, c)))
            else:
                plan.append((src, land.at[2 * x + y], (px, py, c)))
    return plan


def _split_start(kind, srcs, land_shapes, name):
    n = len(srcs)

    def body(*refs):
        ins, lands = refs[:n], refs[n:2 * n]
        send_sems, recv_sems = refs[2 * n], refs[2 * n + 1]
        token = refs[-1]
        x, y, c = _mesh_pos()
        for i, (src, dst, to) in enumerate(_split_plan(kind, ins, lands, x, y, c)):
            pltpu.make_async_remote_copy(src_ref=src, dst_ref=dst, send_sem=send_sems.at[i], recv_sem=recv_sems.at[i],
                                         device_id=to, device_id_type=MESH).start()
        token[...] = jnp.zeros_like(token)

    zones = [lax.empty(s.shape, s.dtype) for s in land_shapes]
    n_sems = {"sibling": 1, "allgather": 7}.get(kind, 3) * n
    res = pl.pallas_call(
        body, name=name,
        out_shape=(pltpu.SemaphoreType.DMA((n_sems,)), pltpu.SemaphoreType.DMA((n_sems,)),
                   *[pltpu.HBM(a.shape, a.dtype) for a in srcs], *[pltpu.HBM(s.shape, s.dtype) for s in land_shapes],
                   jax.ShapeDtypeStruct((8, LANES), F32)),
        in_specs=[_HBM] * (2 * n), out_specs=(_SEM, _SEM, *[_HBM] * (2 * n), pl.BlockSpec(memory_space=pltpu.VMEM)),
        input_output_aliases={i: 2 + i for i in range(2 * n)},
        compiler_params=pltpu.CompilerParams(has_side_effects=pltpu.SideEffectType.DATAFLOW_SIDE_EFFECTING),
    )(*[_in_hbm(a) for a in srcs], *[_in_hbm(z) for z in zones])
    return dict(send=res[0], recv=res[1], srcs=list(res[2:2 + n]), lands=list(res[2 + n:2 + 2 * n]), token=res[-1], kind=kind)


def _split_wait(started, after, name):
    n = len(started["srcs"])
    kind = started["kind"]

    def body(*refs):
        ins, lands = refs[:n], refs[n:2 * n]
        send_sems, recv_sems = refs[2 * n], refs[2 * n + 1]
        x, y, c = _mesh_pos()
        for i, (src, dst, _) in enumerate(_split_plan(kind, ins, lands, x, y, c)):
            cp = pltpu.make_async_remote_copy(src_ref=src, dst_ref=dst, send_sem=send_sems.at[i], recv_sem=recv_sems.at[i],
                                              device_id=(x, y, c), device_id_type=MESH)
            cp.wait_send()
            cp.wait_recv()

    arrs = started["srcs"] + started["lands"]
    res = pl.pallas_call(
        body, name=name, out_shape=tuple(pltpu.HBM(a.shape, a.dtype) for a in arrs),
        in_specs=[_HBM] * (2 * n) + [_SEM, _SEM, pl.BlockSpec(memory_space=pl.ANY)], out_specs=tuple([_HBM] * (2 * n)),
        input_output_aliases={i: i for i in range(2 * n)},
        compiler_params=pltpu.CompilerParams(has_side_effects=pltpu.SideEffectType.DATAFLOW_SIDE_EFFECTING),
    )(*arrs, started["send"], started["recv"], after)
    return list(res[:n]), list(res[n:])


def _sibling_share(fs, name):
    n = len(fs)

    def body(*refs):
        ins, outs = refs[:n], refs[n:2 * n]
        send_sems, recv_sems = refs[2 * n:]
        x, y, c = _mesh_pos()
        copies = [_remote(ins[i], outs[i], send_sems, recv_sems, i, (x, y, 1 - c)) for i in range(n)]
        for cp in copies:
            cp.start()
        for cp in copies:
            cp.wait()

    out_shapes = [jax.ShapeDtypeStruct(a.shape, a.dtype) for a in fs]
    return _comm_call(body, list(fs), out_shapes, n, name)


def _allgather8(v, name, after=()):
    m = v.shape[0]

    def body(v_ref, *rest):
        out_ref, send_sems, recv_sems = rest[len(after):]
        x, y, c = _mesh_pos()
        me, sib = (x, y, c), (x, y, 1 - c)
        chips = _peer_chips(x, y)
        rem = functools.partial(_remote, send_sems=send_sems, recv_sems=recv_sems)

        def blk(px, py, pc):
            return out_ref.at[4 * px + 2 * py + pc]

        first = [rem(v_ref, blk(*me), sem=0, to=sib)]
        first += [rem(v_ref, blk(*me), sem=1 + k, to=(px, py, c)) for k, (px, py) in enumerate(chips)]
        for cp in first:
            cp.start()
        passed = []
        for k, (px, py) in enumerate(chips):
            landed = blk(px, py, c)
            rem(landed, landed, sem=1 + k, to=me).wait_recv()
            fwd = rem(landed, landed, sem=4 + k, to=sib)
            fwd.start()
            passed.append(fwd)
        rem(blk(*sib), blk(*sib), sem=0, to=me).wait_recv()
        for k, (px, py) in enumerate(chips):
            theirs = blk(px, py, 1 - c)
            rem(theirs, theirs, sem=4 + k, to=me).wait_recv()
        for cp in first + passed:
            cp.wait_send()

    return _comm_call(body, [v, *after], [jax.ShapeDtypeStruct((8, m, LANES), v.dtype)], 7, name)[0]


_WEIGHTS = ["norm1_g", "w_in", "conv_a_w", "conv_a_b", "ln_a_g", "ln_a_b", "ln_b_g", "ln_b_b", "w_spatial", "b_spatial",
            "conv_c_w", "conv_c_b", "dt_bias", "a_log", "d_skip", "norm_c_g", "w_out", "norm2_g", "w_ff1", "w_ff2", "final_g"]
_BIG = ["w_in", "w_out", "w_ff1", "w_ff2"]
_CONV_SHARDED = ["conv_a_w", "conv_c_w"]
_SMALL = [w for w in _WEIGHTS if w not in _BIG and w != "final_g"]
_PACK_ROWS = 512


def _pack(arrs):
    flat = jnp.concatenate([a.reshape(-1) for a in arrs])
    blk = _PACK_ROWS * LANES
    n = flat.shape[0]
    return jnp.pad(flat, (0, -(-n // blk) * blk - n)).reshape(-1, LANES)


def _unpack(packed, shapes):
    flat = packed.reshape(-1)
    out, off = [], 0
    for s in shapes:
        n = math.prod(s)
        out.append(flat[off:off + n].reshape(s))
        off += n
    return out


def _cols_to_chips(a):
    k = a.shape[0]
    return a.reshape(k, N_CHIPS, -1).transpose(1, 0, 2)


def _chips_to_cols(a):
    return a.transpose(1, 0, 2).reshape(a.shape[1], -1)


def _own_shards(w, li):
    return [w[k][li].astype(BF16) for k in _BIG] + [w[k][li] for k in _CONV_SHARDED]


def _layer_params(w, li, own, gathered, q):
    g_in, g_out, g_ff1, g_ff2, g_ca, g_cc = [lax.dynamic_update_index_in_dim(g, o, q, axis=0)
                                             for g, o in zip(gathered, own)]
    p = {k: w[k][li] for k in _SMALL if k not in _CONV_SHARDED}
    w_in = _chips_to_cols(g_in)
    p["w_main"] = w_in[:, :D_MAIN]
    p["w_dt"] = _group_heads(w_in[:, D_MAIN:])
    p["w_out"] = g_out.reshape(D_MIX, D_MODEL)
    p["w_ff1"] = g_ff1
    p["w_ff2"] = g_ff2.reshape(D_FF, D_MODEL)
    p["conv_a_w"] = _chips_to_cols(g_ca)
    p["conv_c_w"] = _chips_to_cols(g_cc)
    return p


def _ffn_out_grads(g):
    return [g["w_out"].reshape(N_CHIPS, -1, D_MODEL), g["w_ff1"], g["w_ff2"].reshape(N_CHIPS, -1, D_MODEL)]


def _half_shape(a):
    return jax.ShapeDtypeStruct(a.shape[:-2] + (a.shape[-2] // 2, a.shape[-1]), a.dtype)


def _chip_sums(g, early, early_from_sib, li, c, q):
    n = f"l{li}_rs_"
    late = [g["w_main"], g["w_dt"]]
    full = late + list(early)
    from_sib = list(_sibling_other_halves(late, n + "sib")) + list(early_from_sib)
    mine = [lax.dynamic_slice_in_dim(a, c * b.shape[-2], b.shape[-2], axis=a.ndim - 2) for a, b in zip(full, from_sib)]
    sums = [_chipsum_in(mine[0], mine[1], from_sib[0], from_sib[1], n + "chipsum0")]
    for i in range(2, len(full)):
        shape = from_sib[i].shape
        s32, s16 = _ew(lambda u, v: (u + v, u + v), [mine[i].reshape(-1, shape[-1]), from_sib[i].reshape(-1, shape[-1])],
                       (F32, BF16), n + f"chipsum{i - 1}")
        sums.append((s32.reshape(shape), s16.reshape(shape)))
    chip_f32 = [lax.dynamic_index_in_dim(s32, q, axis=0, keepdims=False) for s32, _ in sums]
    chip_bf16 = [s16 for _, s16 in sums]
    return chip_f32, chip_bf16


def _finish_reduce(chip_f32, from_chips, li, c):
    n = f"l{li}_rs_"
    halves = [_ew(lambda o, r0, r1, r2_: (((o + r0) + r1) + r2_,), [own, rb, rb, rb], (F32,), n + f"final{i}",
                  leads=[None, 0, 1, 2])[0] for i, (own, rb) in enumerate(zip(chip_f32, from_chips))]
    from_sib = _sibling_share(halves, n + "share")
    return [jnp.where(c == 0, jnp.concatenate([h, s], axis=0), jnp.concatenate([s, h], axis=0))
            for h, s in zip(halves, from_sib)]


def kernel(x, norm1_g, w_in, conv_a_w, conv_a_b, ln_a_g, ln_a_b, ln_b_g, ln_b_b, w_spatial, b_spatial, conv_c_w, conv_c_b, dt_bias, a_log, d_skip, norm_c_g, w_out, norm2_g, w_ff1, w_ff2, final_g, loss_target, m_norm1_g, m_w_in, m_conv_a_w, m_conv_a_b, m_ln_a_g, m_ln_a_b, m_ln_b_g, m_ln_b_b, m_w_spatial, m_b_spatial, m_conv_c_w, m_conv_c_b, m_dt_bias, m_a_log, m_d_skip, m_norm_c_g, m_w_out, m_norm2_g, m_w_ff1, m_w_ff2, m_final_g, v_norm1_g, v_w_in, v_conv_a_w, v_conv_a_b, v_ln_a_g, v_ln_a_b, v_ln_b_g, v_ln_b_b, v_w_spatial, v_b_spatial, v_conv_c_w, v_conv_c_b, v_dt_bias, v_a_log, v_d_skip, v_norm_c_g, v_w_out, v_norm2_g, v_w_ff1, v_w_ff2, v_final_g):
    given = dict(locals())
    w = {k: given[k] for k in _WEIGHTS}
    m = {k: given["m_" + k] for k in _WEIGHTS}
    v = {k: given["v_" + k] for k in _WEIGHTS}
    depth = w_in.shape[0]
    nseq, seq, d = x.shape
    xi, yi, ci = _mesh_pos()
    q = 2 * xi + yi

    own = [_own_shards(w, li) for li in range(depth)]
    nb = len(_BIG)
    gathered = _gather_weights(own[0][:nb], own[0][nb:], "l0_gather")
    h = x.reshape(nseq * seq, d)
    layer_params, saved = [], []
    for li in range(depth):
        nxt = None
        if li + 1 < depth:
            srcs, _ = lax.optimization_barrier((own[li + 1], gathered))
            zones = [jax.ShapeDtypeStruct((N_CHIPS,) + a.shape, a.dtype) for a in srcs]
            nxt = _split_start("gather", srcs, zones, f"l{li + 1}_gather_start")
        layer_params.append(_layer_params(w, li, own[li], gathered, q))
        h, s = _layer_fwd(h, layer_params[li], seq, li, after=() if nxt is None else (nxt["token"],))
        saved.append(s)
        if nxt is not None:
            own[li + 1], gathered = _split_wait(nxt, h, f"l{li + 1}_gather_wait")
    loss, dx, d_final = _loss_head(h, final_g, loss_target.reshape(nseq * seq, d))

    grads = [None] * depth
    big_grads = [None] * depth
    pending = None
    for li in reversed(range(depth)):
        swaps = []

        def early_swap(g, li=li, swaps=swaps):
            early = _ffn_out_grads(g)
            swaps.append(_split_start("sibling", early, [_half_shape(a) for a in early], f"l{li}_rs_sib_start"))
            return (swaps[0]["token"],)

        after = () if pending is None else (pending[1]["token"],)
        if li == 0 and depth > 1:
            early_pack = _pack([grads[lj][k] for lj in range(1, depth) for k in _SMALL])
            early_small = _split_start("allgather", [early_pack], [jax.ShapeDtypeStruct((8,) + early_pack.shape, F32)],
                                       "small_early_start")
            after = after + (early_small["token"],)
        dx, grads[li] = _layer_bwd(dx, layer_params[li], saved[li], seq, li, after=after, on_ffn_grads=early_swap)
        if pending is not None:
            lj, scatter, chip_f32 = pending
            big_grads[lj] = _finish_reduce(chip_f32, _split_wait(scatter, dx, f"l{lj}_rs_scatter_wait")[1], lj, ci)
        early, early_from_sib = _split_wait(swaps[0], dx, f"l{li}_rs_sib_wait")
        chip_f32, chip_bf16 = _chip_sums(grads[li], early, early_from_sib, li, ci, q)
        lands = [jax.ShapeDtypeStruct((3,) + a.shape[1:], a.dtype) for a in chip_bf16]
        pending = (li, _split_start("scatter", chip_bf16, lands, f"l{li}_rs_scatter_start"), chip_f32)
    grad_out, delta_out, m_out, v_out = {}, {}, {}, {}

    small_shapes = [grads[0][k].shape for k in _SMALL]
    me = 2 * q + ci

    def sum8(*blocks):
        acc = blocks[0]
        for b in blocks[1:]:
            acc = acc + b
        return (acc,)

    def total_of(gathered, own, name):
        full = lax.dynamic_update_index_in_dim(gathered, own, me, axis=0)
        return _ew(sum8, [full] * 8, (F32,), name, leads=list(range(8)))[0]

    last_pack = _pack([grads[0][k] for k in _SMALL] + [d_final, loss.reshape(1)])
    last_total = total_of(_allgather8(last_pack, "small_allgather", after=(pending[1]["token"],)), last_pack, "small_sum")
    summed = _unpack(last_total, small_shapes + [d_final.shape, (1,)])
    tail = summed[len(_SMALL):]
    summed = summed[:len(_SMALL)]
    if depth > 1:
        (early_own,), (early_all,) = _split_wait(early_small, dx, "small_early_wait")
        summed += _unpack(total_of(early_all, early_own, "small_early_sum"), small_shapes * (depth - 1))
    summed += tail
    loss_total = summed[-1][0]
    small_grads = {k: jnp.stack([summed[li * len(_SMALL) + i] for li in range(depth)]) for i, k in enumerate(_SMALL)}
    small_grads["final_g"] = summed[-2]
    for k in _CONV_SHARDED:
        n_shard = w[k].shape[-1]
        small_grads[k] = lax.dynamic_slice_in_dim(small_grads[k], q * n_shard, n_shard, axis=2)
    names = _SMALL + ["final_g"]
    shapes = [w[k].shape for k in names]
    packed = [_pack([src[k] for k in names]) for src in (w, small_grads, m, v)]
    outs = _ew(_adam_fn, packed, (F32, F32, F32), "adam_small")
    for dst, o in zip((delta_out, m_out, v_out), outs):
        for k, a in zip(names, _unpack(o, shapes)):
            dst[k] = a
    for k in names:
        grad_out[k] = small_grads[k]

    lj, scatter, chip_f32 = pending
    big_grads[lj] = _finish_reduce(chip_f32, _split_wait(scatter, outs[0], f"l{lj}_rs_scatter_wait")[1], lj, ci)
    for i, k in enumerate(_BIG):
        grad_out[k] = jnp.stack([big_grads[li][i] for li in range(depth)])
        delta_out[k], m_out[k], v_out[k] = _adam(w[k], grad_out[k], m[k], v[k], "adam_" + k)

    return (loss_total, dx.reshape(nseq, seq, d), *[grad_out[k] for k in _WEIGHTS], *[delta_out[k] for k in _WEIGHTS],
            *[m_out[k] for k in _WEIGHTS], *[v_out[k] for k in _WEIGHTS])
```

```python
import functools
import math

import jax
import jax.numpy as jnp
from jax import lax
from jax.experimental import pallas as pl
from jax.experimental.pallas import tpu as pltpu

F32 = jnp.float32
BF16 = jnp.bfloat16
MESH = pl.DeviceIdType.MESH

D_MODEL = 1024
DEPTH = 4
HEAD_DIM = 64
A_WIDTH = 512
B_WIDTH = 512
C_WIDTH = 1024
C_HEADS = 16
CONV_A_K = 31
CONV_C_K = 4
CHUNK = 128
SSM_STATE = 128
D_CONV_C = 1536
D_MAIN = 4608
D_IN_PROJ = 4624
D_MIX = 2048
D_FF = 4096
EPS = 1e-5
NEG = -1e30
LANES = 128
CONV_PAD = 32
N_CHIPS = 4

ADAM_LR = 0.001
ADAM_B1 = 0.9
ADAM_B2 = 0.999
ADAM_EPS = 1e-08
ADAM_WD = 0.01
ADAM_STEP = 10

VMEM_LIMIT = 56 * 1024 * 1024
MATMUL_VMEM_BUDGET = 44 * 1024 * 1024

COL_AVAL, COL_AGATE, COL_BU, COL_BV, COL_Z, COL_XBC = 0, 4, 8, 12, 16, 24


def _cparams(*sem):
    return pltpu.CompilerParams(dimension_semantics=sem, vmem_limit_bytes=VMEM_LIMIT)


_DN = {"nn": (((1,), (0,)), ((), ())), "nt": (((1,), (1,)), ((), ())), "tn": (((0,), (0,)), ((), ()))}


def _dot_raw(a, b, mode):
    return lax.dot_general(a.astype(BF16), b.astype(BF16), _DN[mode], preferred_element_type=F32)


def _make_dot(mode):
    @jax.custom_vjp
    def f(a, b):
        return _dot_raw(a, b, mode)

    def fwd(a, b):
        return _dot_raw(a, b, mode), (a, b)

    def bwd(res, g):
        a, b = res
        if mode == "nn":
            return _dot_raw(g, b, "nt"), _dot_raw(a, g, "tn")
        if mode == "nt":
            return _dot_raw(g, b, "nn"), _dot_raw(g, a, "tn")
        return _dot_raw(b, g, "nt"), _dot_raw(a, g, "nn")

    f.defvjp(fwd, bwd)
    return f


_nn = _make_dot("nn")
_nt = _make_dot("nt")
_tn = _make_dot("tn")


def _xdot(a, e):
    return jnp.dot(a, e, precision=lax.Precision.HIGHEST, preferred_element_type=F32)


def _iota2(shape, dim):
    return lax.broadcasted_iota(jnp.int32, shape, dim)


def _gmean_impl(x):
    n = x.shape[-1]
    same = (_iota2((n, n), 0) < HEAD_DIM) == (_iota2((n, n), 1) < HEAD_DIM)
    p = jnp.where(same, 1.0 / HEAD_DIM, 0.0).astype(BF16)
    hi = x.astype(BF16)
    lo = (x - hi.astype(F32)).astype(BF16)
    dn = _DN["nn"]
    return (lax.dot_general(hi, p, dn, preferred_element_type=F32)
            + lax.dot_general(lo, p, dn, preferred_element_type=F32))


@jax.custom_vjp
def _gmean(x):
    return _gmean_impl(x)


_gmean.defvjp(lambda x: (_gmean_impl(x), None), lambda _, g: (_gmean_impl(g),))


def _sigmoid(x):
    return 1.0 / (1.0 + jnp.exp(-x))


def _silu(x):
    return x * _sigmoid(x)


def _gelu(x):
    return 0.5 * x * (1.0 + lax.erf(x * 0.7071067811865476))


def _softplus(x):
    return jnp.maximum(x, 0.0) + jnp.log(1.0 + jnp.exp(-jnp.abs(x)))


def _rms(x, g):
    return x * lax.rsqrt(jnp.mean(x * x, axis=-1, keepdims=True) + EPS) * g


def _ln64(x, g, b):
    mu = _gmean(x)
    xc = x - mu
    var = _gmean(xc * xc)
    return xc * lax.rsqrt(var + EPS) * g + b


def _lane_lt64(shape):
    return _iota2(shape, 1) < HEAD_DIM


def _pick(n, pref):
    for t in pref:
        if n % t == 0:
            return t
    return n


_UNREAD = pl.BlockSpec(memory_space=pl.ANY)


def _matmul_tiles(m, n_unit, k, a_item, b_item, out_bytes):
    best = None
    for tm in (1024, 512, 256, 128):
        for tn in (1536, 1024, 768, 512, 256, 128):
            if m % tm or n_unit % tn:
                continue
            need = 2 * k * (tm * a_item + tn * b_item) + 2 * tm * tn * out_bytes
            if need <= MATMUL_VMEM_BUDGET and (best is None or tm * tn > best[0] * best[1]):
                best = (tm, tn)
    assert best is not None, (m, n_unit, k)
    return best


def _matmul(a, b, *, mode, name, add=None, epilogue=None, extra=None, out_dtypes=(F32,), after=(), b_chips=False,
            out_chips=False):
    sh = b.shape[-1] if b_chips else None
    if mode == "nn":
        (m, k), n = a.shape, (N_CHIPS * sh if b_chips else b.shape[1])
    elif mode == "nt":
        (m, k), n = a.shape, b.shape[-2]
    else:
        (k, m), n = a.shape, b.shape[1]
    osh = n // N_CHIPS if out_chips else None
    out_bytes = sum(jnp.dtype(dt).itemsize for dt in out_dtypes) + (0 if add is None else add.dtype.itemsize) \
        + (0 if extra is None else extra.dtype.itemsize)
    tm, tn = _matmul_tiles(m, sh if (b_chips and mode == "nn") else (osh or n), k, a.dtype.itemsize, b.dtype.itemsize,
                           out_bytes)
    a_spec = pl.BlockSpec((k, tm), lambda i, j: (0, i)) if mode == "tn" else pl.BlockSpec((tm, k), lambda i, j: (i, 0))
    if b_chips and mode == "nn":
        per = sh // tn
        b_spec = pl.BlockSpec((None, k, tn), lambda i, j: (j // per, 0, j % per))
    elif b_chips:
        b_spec = pl.BlockSpec((N_CHIPS, tn, sh), lambda i, j: (0, j, 0))
    elif mode == "nt":
        b_spec = pl.BlockSpec((tn, k), lambda i, j: (j, 0))
    else:
        b_spec = pl.BlockSpec((k, tn), lambda i, j: (0, j))
    if out_chips:
        o_per = osh // tn
        o_spec = pl.BlockSpec((None, tm, tn), lambda i, j: (j // o_per, i, j % o_per))
        out_shape = [jax.ShapeDtypeStruct((N_CHIPS, m, osh), dt) for dt in out_dtypes]
    else:
        o_spec = pl.BlockSpec((tm, tn), lambda i, j: (i, j))
        out_shape = [jax.ShapeDtypeStruct((m, n), dt) for dt in out_dtypes]
    ins = [a, b]
    in_specs = [a_spec, b_spec]
    if add is not None:
        ins.append(add)
        in_specs.append(o_spec)
    if extra is not None:
        ins.append(extra)
        in_specs.append(o_spec)
    ins += list(after)
    in_specs += [_UNREAD] * len(after)
    n_out = len(out_dtypes)

    def body(*refs):
        a_ref, b_ref = refs[0], refs[1]
        pos = 2
        add_ref = ex_ref = None
        if add is not None:
            add_ref = refs[pos]
            pos += 1
        if extra is not None:
            ex_ref = refs[pos]
            pos += 1
        pos += len(after)
        if b_chips and mode == "nt":
            acc = _dot_raw(a_ref[:, pl.ds(0, sh)], b_ref[0], mode)
            for chip in range(1, N_CHIPS):
                acc = acc + _dot_raw(a_ref[:, pl.ds(chip * sh, sh)], b_ref[chip], mode)
        else:
            acc = _dot_raw(a_ref[...], b_ref[...], mode)
        if add_ref is not None:
            acc = acc + add_ref[...].astype(F32)
        outs = (acc,) if epilogue is None else epilogue(acc, None if ex_ref is None else ex_ref[...])
        for o_ref, o in zip(refs[pos:pos + n_out], outs):
            o_ref[...] = o.astype(o_ref.dtype)

    res = pl.pallas_call(
        body, name=name, grid=(m // tm, n // tn), in_specs=in_specs, out_specs=[o_spec] * n_out, out_shape=out_shape,
        compiler_params=_cparams("parallel", "parallel"),
    )(*ins)
    return res[0] if n_out == 1 else res


def _matmul_ksplit(a, b, *, mode, name, add=None, epilogue=None, extra=None, out_dtypes=(F32,), after=(), b_chips=False,
                   out_chips=False):
    sh = b.shape[-1] if b_chips else None
    if mode == "nn":
        (m, k), n = a.shape, (N_CHIPS * sh if b_chips else b.shape[1])
    elif mode == "nt":
        (m, k), n = a.shape, b.shape[-2]
    else:
        (k, m), n = a.shape, b.shape[1]
    osh = n // N_CHIPS if out_chips else None
    tm = _pick(m, (1024, 512, 256, 128))
    tn = _pick(sh if (b_chips and mode == "nn") else (osh or n), (1536, 1024, 512, 256, 128))
    out_bytes = sum(jnp.dtype(dt).itemsize for dt in out_dtypes) + (0 if add is None else add.dtype.itemsize) \
        + (0 if extra is None else extra.dtype.itemsize)
    k_dim = sh if (b_chips and mode == "nt") else k
    for tk in (4096, 2048, 1536, 1024, 512, 256, 128):
        if k_dim % tk:
            continue
        acc_bytes = 0 if tk == k else 4 * tm * tn
        need = 2 * tk * (tm * a.dtype.itemsize + tn * b.dtype.itemsize) + 2 * tm * tn * out_bytes + acc_bytes
        if need <= MATMUL_VMEM_BUDGET:
            break
    nk = k // tk
    a_spec = {"nn": pl.BlockSpec((tm, tk), lambda i, j, kk: (i, kk)),
              "nt": pl.BlockSpec((tm, tk), lambda i, j, kk: (i, kk)),
              "tn": pl.BlockSpec((tk, tm), lambda i, j, kk: (kk, i))}[mode]
    if b_chips:
        per = sh // (tn if mode == "nn" else tk)
        b_spec = {"nn": pl.BlockSpec((None, tk, tn), lambda i, j, kk: (j // per, kk, j % per)),
                  "nt": pl.BlockSpec((None, tn, tk), lambda i, j, kk: (kk // per, j, kk % per))}[mode]
    else:
        b_spec = {"nn": pl.BlockSpec((tk, tn), lambda i, j, kk: (kk, j)),
                  "nt": pl.BlockSpec((tn, tk), lambda i, j, kk: (j, kk)),
                  "tn": pl.BlockSpec((tk, tn), lambda i, j, kk: (kk, j))}[mode]
    if out_chips:
        o_per = osh // tn
        o_spec = pl.BlockSpec((None, tm, tn), lambda i, j, kk: (j // o_per, i, j % o_per))
        out_shape = [jax.ShapeDtypeStruct((N_CHIPS, m, osh), dt) for dt in out_dtypes]
    else:
        o_spec = pl.BlockSpec((tm, tn), lambda i, j, kk: (i, j))
        out_shape = [jax.ShapeDtypeStruct((m, n), dt) for dt in out_dtypes]
    ins = [a, b]
    in_specs = [a_spec, b_spec]
    if add is not None:
        ins.append(add)
        in_specs.append(o_spec)
    if extra is not None:
        ins.append(extra)
        in_specs.append(o_spec)
    ins += list(after)
    in_specs += [_UNREAD] * len(after)
    n_out = len(out_dtypes)

    def body(*refs):
        a_ref, b_ref = refs[0], refs[1]
        pos = 2
        add_ref = ex_ref = None
        if add is not None:
            add_ref = refs[pos]
            pos += 1
        if extra is not None:
            ex_ref = refs[pos]
            pos += 1
        pos += len(after)
        o_refs = refs[pos:pos + n_out]

        def finish(acc):
            if add_ref is not None:
                acc = acc + add_ref[...].astype(F32)
            outs = (acc,) if epilogue is None else epilogue(acc, None if ex_ref is None else ex_ref[...])
            for o_ref, o in zip(o_refs, outs):
                o_ref[...] = o.astype(o_ref.dtype)

        part = _dot_raw(a_ref[...], b_ref[...], mode)
        if nk == 1:
            finish(part)
            return
        acc_ref = refs[pos + n_out]
        kk = pl.program_id(2)

        @pl.when(kk == 0)
        def _():
            acc_ref[...] = part

        @pl.when(jnp.logical_and(kk > 0, kk < nk - 1))
        def _():
            acc_ref[...] += part

        @pl.when(kk == nk - 1)
        def _():
            finish(acc_ref[...] + part)

    res = pl.pallas_call(
        body, name=name, grid=(m // tm, n // tn, nk),
        in_specs=in_specs, out_specs=[o_spec] * n_out, out_shape=out_shape,
        scratch_shapes=[pltpu.VMEM((tm, tn), F32)] if nk > 1 else [],
        compiler_params=_cparams("parallel", "parallel", "arbitrary"),
    )(*ins)
    return res[0] if n_out == 1 else res


def _relu2_epilogue(acc, _):
    r = jnp.maximum(acc, 0.0)
    return acc, r * r


def _relu2_bwd_epilogue(acc, u):
    return (acc * (2.0 * jnp.maximum(u, 0.0)),)


def _row_tile(t):
    return _pick(t, (512, 256, 128))


def _rms_fwd(x, g, name, after=()):
    t, d = x.shape
    tm = _row_tile(t)

    def body(x_ref, g_ref, *rest):
        o_ref = rest[-1]
        o_ref[...] = _rms(x_ref[...], g_ref[...]).astype(BF16)

    return pl.pallas_call(
        body, name=name, grid=(t // tm,),
        in_specs=[pl.BlockSpec((tm, d), lambda i: (i, 0)), pl.BlockSpec((1, d), lambda i: (0, 0))] + [_UNREAD] * len(after),
        out_specs=pl.BlockSpec((tm, d), lambda i: (i, 0)),
        out_shape=jax.ShapeDtypeStruct((t, d), BF16),
        compiler_params=_cparams("parallel"),
    )(x, g.reshape(1, d), *after)


def _rms_bwd(x, g, dh, dres, name):
    t, d = x.shape
    tm = _row_tile(t)

    def body(x_ref, g_ref, dh_ref, dres_ref, dx_ref, dg_ref):
        @pl.when(pl.program_id(0) == 0)
        def _():
            dg_ref[...] = jnp.zeros_like(dg_ref)

        _, vjp = jax.vjp(_rms, x_ref[...], g_ref[...])
        dx, dg = vjp(dh_ref[...].astype(F32))
        dx_ref[...] = dx + dres_ref[...]
        dg_ref[...] += dg

    row = pl.BlockSpec((tm, d), lambda i: (i, 0))
    vec = pl.BlockSpec((1, d), lambda i: (0, 0))
    dx, dg = pl.pallas_call(
        body, name=name, grid=(t // tm,),
        in_specs=[row, vec, row, row], out_specs=[row, vec],
        out_shape=[jax.ShapeDtypeStruct((t, d), F32), jax.ShapeDtypeStruct((1, d), F32)],
        compiler_params=_cparams("arbitrary"),
    )(x, g.reshape(1, d), dh, dres)
    return dx, dg.reshape(d)


def _loss_head(x, g, target):
    t, d = x.shape
    tm = _row_tile(t)

    def loss_fn(xv, gv, tv):
        err = _rms(xv, gv) - tv
        return 0.5 * jnp.sum(jnp.mean(err * err, axis=-1, keepdims=True))

    def body(x_ref, g_ref, t_ref, loss_ref, dx_ref, dg_ref):
        @pl.when(pl.program_id(0) == 0)
        def _():
            dg_ref[...] = jnp.zeros_like(dg_ref)
            loss_ref[...] = jnp.zeros_like(loss_ref)

        tv = t_ref[...]
        val, vjp = jax.vjp(lambda xv, gv: loss_fn(xv, gv, tv), x_ref[...], g_ref[...])
        dx, dg = vjp(jnp.ones((), F32))
        dx_ref[...] = dx
        dg_ref[...] += dg
        loss_ref[...] += jnp.full(loss_ref.shape, val, F32)

    row = pl.BlockSpec((tm, d), lambda i: (i, 0))
    vec = pl.BlockSpec((1, d), lambda i: (0, 0))
    loss, dx, dg = pl.pallas_call(
        body, name="loss_head", grid=(t // tm,),
        in_specs=[row, vec, row], out_specs=[pl.BlockSpec((1, LANES), lambda i: (0, 0)), row, vec],
        out_shape=[jax.ShapeDtypeStruct((1, LANES), F32), jax.ShapeDtypeStruct((t, d), F32),
                   jax.ShapeDtypeStruct((1, d), F32)],
        compiler_params=_cparams("arbitrary"),
    )(x, g.reshape(1, d), target)
    return loss[0, 0], dx, dg.reshape(d)


def _pre_glu(val, gate):
    return val * _sigmoid(gate)


def _pre_id(x):
    return x


def _post_lnsilu(c, g, b):
    return _silu(_ln64(c, g, b))


def _post_silu(c):
    return _silu(c)


def _conv_cfg(kind):
    if kind == "a":
        return dict(k=CONV_A_K, pre=_pre_glu, post=_post_lnsilu, n_in=2, n_par=2, nblk=A_WIDTH // LANES,
                    cols=(COL_AVAL, COL_AGATE))
    return dict(k=CONV_C_K, pre=_pre_id, post=_post_silu, n_in=1, n_par=0, nblk=D_CONV_C // LANES,
                cols=(COL_XBC,))


def _conv_fwd(kind, proj, w, bias, params, seq, name, out_dtype=F32, keep_conv=False):
    cfg = _conv_cfg(kind)
    kt, pre, post, n_in = cfg["k"], cfg["pre"], cfg["post"], cfg["n_in"]
    t = proj.shape[0]
    nseq = t // seq
    c = cfg["nblk"] * LANES
    rt = min(256, seq)
    nrt = seq // rt
    off0 = CONV_PAD - (kt - 1)

    def body(*refs):
        in_refs = refs[:n_in]
        w_ref, b_ref = refs[n_in], refs[n_in + 1]
        par_refs = refs[n_in + 2:n_in + 2 + cfg["n_par"]]
        out_refs = refs[n_in + 2 + cfg["n_par"]:-1]
        hpad = refs[-1]
        hpad[pl.ds(0, CONV_PAD), :] = jnp.zeros((CONV_PAD, LANES), F32)
        for r in range(nrt):
            hpad[pl.ds(CONV_PAD + r * rt, rt), :] = pre(*[x[pl.ds(r * rt, rt), :] for x in in_refs])
        pars = [p[...] for p in par_refs]
        for r in range(nrt):
            acc = jnp.broadcast_to(b_ref[...], (rt, LANES))
            for k in range(kt):
                acc = acc + w_ref[pl.ds(k, 1), :] * hpad[pl.ds(off0 + k + r * rt, rt), :]
            out_refs[0][pl.ds(r * rt, rt), :] = post(acc, *pars).astype(out_dtype)
            if keep_conv:
                out_refs[1][pl.ds(r * rt, rt), :] = acc

    in_specs = [pl.BlockSpec((seq, LANES), functools.partial(lambda s, j, col: (s, col + j), col=col))
                for col in cfg["cols"]]
    vec = pl.BlockSpec((1, LANES), lambda s, j: (0, j))
    in_specs += [pl.BlockSpec((CONV_PAD, LANES), lambda s, j: (0, j)), vec] + [vec] * cfg["n_par"]
    blk = pl.BlockSpec((seq, LANES), lambda s, j: (s, j))
    res = pl.pallas_call(
        body, name=name, grid=(nseq, cfg["nblk"]),
        in_specs=in_specs, out_specs=[blk, blk] if keep_conv else [blk],
        out_shape=[jax.ShapeDtypeStruct((t, c), out_dtype)] + ([jax.ShapeDtypeStruct((t, c), F32)] if keep_conv else []),
        scratch_shapes=[pltpu.VMEM((seq + CONV_PAD, LANES), F32)],
        compiler_params=_cparams("parallel", "parallel"),
    )(*([proj] * n_in), w, bias, *params)
    return tuple(res) if keep_conv else res[0]


def _conv_bwd(kind, proj, w, bias, params, dy, seq, name, dy_col=0, conv_out=None):
    kept = conv_out is not None
    cfg = _conv_cfg(kind)
    kt, pre, post, n_in, n_par = cfg["k"], cfg["pre"], cfg["post"], cfg["n_in"], cfg["n_par"]
    t = proj.shape[0]
    nseq = t // seq
    c = cfg["nblk"] * LANES
    rt = min(256, seq)
    nrt = seq // rt
    off0 = CONV_PAD - (kt - 1)

    def body(*refs):
        in_refs = refs[:n_in]
        w_ref, b_ref = refs[n_in], refs[n_in + 1]
        par_refs = refs[n_in + 2:n_in + 2 + n_par]
        pos = n_in + 2 + n_par
        dy_ref = refs[pos]
        if kept:
            pos += 1
            conv_ref = refs[pos]
        din_refs = refs[pos + 1:pos + 1 + n_in]
        dw_ref, db_ref = refs[pos + 1 + n_in], refs[pos + 2 + n_in]
        dpar_refs = refs[pos + 3 + n_in:pos + 3 + n_in + n_par]
        hpad, dcpad = refs[pos + 3 + n_in + n_par:]

        @pl.when(pl.program_id(1) == 0)
        def _():
            dw_ref[...] = jnp.zeros_like(dw_ref)
            db_ref[...] = jnp.zeros_like(db_ref)
            for r in dpar_refs:
                r[...] = jnp.zeros_like(r)

        hpad[pl.ds(0, CONV_PAD), :] = jnp.zeros((CONV_PAD, LANES), F32)
        dcpad[pl.ds(seq, CONV_PAD), :] = jnp.zeros((CONV_PAD, LANES), F32)
        for r in range(nrt):
            hpad[pl.ds(CONV_PAD + r * rt, rt), :] = pre(*[x[pl.ds(r * rt, rt), :] for x in in_refs])
        pars = [p[...] for p in par_refs]
        for r in range(nrt):
            if kept:
                acc = conv_ref[pl.ds(r * rt, rt), :]
            else:
                acc = jnp.broadcast_to(b_ref[...], (rt, LANES))
                for k in range(kt):
                    acc = acc + w_ref[pl.ds(k, 1), :] * hpad[pl.ds(off0 + k + r * rt, rt), :]
            _, vjp = jax.vjp(post, acc, *pars)
            grads = vjp(dy_ref[pl.ds(r * rt, rt), :])
            dcpad[pl.ds(r * rt, rt), :] = grads[0]
            db_ref[...] += jnp.sum(grads[0], axis=0, keepdims=True)
            for ref, gpar in zip(dpar_refs, grads[1:]):
                ref[...] += gpar
        for r in range(nrt):
            dh = jnp.zeros((rt, LANES), F32)
            for k in range(kt):
                dh = dh + w_ref[pl.ds(k, 1), :] * dcpad[pl.ds(r * rt + kt - 1 - k, rt), :]
            _, vjp = jax.vjp(pre, *[x[pl.ds(r * rt, rt), :] for x in in_refs])
            for ref, gin in zip(din_refs, vjp(dh)):
                ref[pl.ds(r * rt, rt), :] = gin.astype(ref.dtype)
        for k in range(kt):
            s = jnp.zeros((1, LANES), F32)
            for r in range(nrt):
                s = s + jnp.sum(dcpad[pl.ds(r * rt, rt), :] * hpad[pl.ds(off0 + k + r * rt, rt), :],
                                axis=0, keepdims=True)
            dw_ref[pl.ds(k, 1), :] += s

    in_specs = [pl.BlockSpec((seq, LANES), functools.partial(lambda j, s, col: (s, col + j), col=col))
                for col in cfg["cols"]]
    vec = pl.BlockSpec((1, LANES), lambda j, s: (0, j))
    wspec = pl.BlockSpec((CONV_PAD, LANES), lambda j, s: (0, j))
    blk = pl.BlockSpec((seq, LANES), lambda j, s: (s, j))
    in_specs += [wspec, vec] + [vec] * n_par + [pl.BlockSpec((seq, LANES), lambda j, s: (s, dy_col + j))]
    in_specs += [blk] if kept else []
    out_specs = [blk] * n_in + [wspec, vec] + [vec] * n_par
    out_shape = ([jax.ShapeDtypeStruct((t, c), BF16)] * n_in
                 + [jax.ShapeDtypeStruct((CONV_PAD, c), F32), jax.ShapeDtypeStruct((1, c), F32)]
                 + [jax.ShapeDtypeStruct((1, c), F32)] * n_par)
    res = pl.pallas_call(
        body, name=name, grid=(cfg["nblk"], nseq),
        in_specs=in_specs, out_specs=out_specs, out_shape=out_shape,
        scratch_shapes=[pltpu.VMEM((seq + CONV_PAD, LANES), F32), pltpu.VMEM((seq + CONV_PAD, LANES), F32)],
        compiler_params=_cparams("parallel", "arbitrary"),
    )(*([proj] * n_in), w, bias, *params, dy, *([conv_out] if kept else []))
    return res[:n_in], res[n_in], res[n_in + 1], res[n_in + 2:]


def _gmlp_chunk(bu, bv, g, b, w0, w1, b0row, b1row):
    u = _gelu(bu)
    vn = _ln64(_gelu(bv), g, b)
    tri = _iota2((CHUNK, CHUNK), 0) >= _iota2((CHUNK, CHUNK), 1)
    m0 = _nn(jnp.where(tri, w0, 0.0), vn) + jnp.broadcast_to(b0row, (CHUNK, CHUNK)).T
    m1 = _nn(jnp.where(tri, w1, 0.0), vn) + jnp.broadcast_to(b1row, (CHUNK, CHUNK)).T
    return u * jnp.where(_lane_lt64((CHUNK, LANES)), m0, m1)


def _gmlp_specs(tm, order):
    def im(f):
        return lambda *ids: f(*order(*ids))
    return dict(
        bu=pl.BlockSpec((tm, LANES), im(lambda j, r: (r, COL_BU + j))),
        bv=pl.BlockSpec((tm, LANES), im(lambda j, r: (r, COL_BV + j))),
        vec=pl.BlockSpec((1, LANES), im(lambda j, r: (0, j))),
        ws=pl.BlockSpec((2, CHUNK, CHUNK), im(lambda j, r: (j, 0, 0))),
        bs=pl.BlockSpec((None, 2, CHUNK), im(lambda j, r: (j, 0, 0))),
        blk=pl.BlockSpec((tm, LANES), im(lambda j, r: (r, j))),
    )


def _gmlp_fwd(proj, ln_g, ln_b, w_s, b_s, name):
    t = proj.shape[0]
    tm = _row_tile(t)
    nch = tm // CHUNK
    sp = _gmlp_specs(tm, lambda r, j: (j, r))

    def body(bu_ref, bv_ref, g_ref, b_ref, ws_ref, bs_ref, o_ref):
        for ci in range(nch):
            rows = pl.ds(ci * CHUNK, CHUNK)
            o_ref[rows, :] = _gmlp_chunk(bu_ref[rows, :], bv_ref[rows, :], g_ref[...], b_ref[...], ws_ref[0], ws_ref[1],
                                         bs_ref[pl.ds(0, 1), :], bs_ref[pl.ds(1, 1), :]).astype(BF16)

    return pl.pallas_call(
        body, name=name, grid=(t // tm, B_WIDTH // LANES),
        in_specs=[sp["bu"], sp["bv"], sp["vec"], sp["vec"], sp["ws"], sp["bs"]],
        out_specs=sp["blk"], out_shape=jax.ShapeDtypeStruct((t, B_WIDTH), BF16),
        compiler_params=_cparams("parallel", "parallel"),
    )(proj, proj, ln_g, ln_b, w_s, b_s.reshape(B_WIDTH // LANES, 2, CHUNK))


def _gmlp_bwd(proj, ln_g, ln_b, w_s, b_s, dy, name, dy_col=0):
    t = proj.shape[0]
    tm = _row_tile(t)
    nch = tm // CHUNK
    sp = _gmlp_specs(tm, lambda j, r: (j, r))
    dy_spec = pl.BlockSpec((tm, LANES), lambda j, r: (r, dy_col + j))

    def body(bu_ref, bv_ref, g_ref, b_ref, ws_ref, bs_ref, dy_ref, dbu_ref, dbv_ref, dg_ref, db_ref, dws_ref, dbs_ref):
        @pl.when(pl.program_id(1) == 0)
        def _():
            for r in (dg_ref, db_ref, dws_ref, dbs_ref):
                r[...] = jnp.zeros_like(r)

        for ci in range(nch):
            rows = pl.ds(ci * CHUNK, CHUNK)
            _, vjp = jax.vjp(_gmlp_chunk, bu_ref[rows, :], bv_ref[rows, :], g_ref[...], b_ref[...],
                             ws_ref[0], ws_ref[1], bs_ref[pl.ds(0, 1), :], bs_ref[pl.ds(1, 1), :])
            dbu, dbv, dg, db, dw0, dw1, db0, db1 = vjp(dy_ref[rows, :])
            dbu_ref[rows, :] = dbu.astype(BF16)
            dbv_ref[rows, :] = dbv.astype(BF16)
            dg_ref[...] += dg
            db_ref[...] += db
            dws_ref[0] += dw0
            dws_ref[1] += dw1
            dbs_ref[pl.ds(0, 1), :] += db0
            dbs_ref[pl.ds(1, 1), :] += db1

    nh = B_WIDTH // LANES
    res = pl.pallas_call(
        body, name=name, grid=(nh, t // tm),
        in_specs=[sp["bu"], sp["bv"], sp["vec"], sp["vec"], sp["ws"], sp["bs"], dy_spec],
        out_specs=[sp["blk"], sp["blk"], sp["vec"], sp["vec"], sp["ws"], sp["bs"]],
        out_shape=[jax.ShapeDtypeStruct((t, B_WIDTH), BF16), jax.ShapeDtypeStruct((t, B_WIDTH), BF16),
                   jax.ShapeDtypeStruct((1, B_WIDTH), F32), jax.ShapeDtypeStruct((1, B_WIDTH), F32),
                   jax.ShapeDtypeStruct(w_s.shape, F32), jax.ShapeDtypeStruct((nh, 2, CHUNK), F32)],
        compiler_params=_cparams("parallel", "arbitrary"),
    )(proj, proj, ln_g, ln_b, w_s, b_s.reshape(nh, 2, CHUNK), dy)
    dbu, dbv, dg, db, dws, dbs = res
    return dbu, dbv, dg, db, dws, dbs.reshape(b_s.shape)


def _tri_apply(a, lower):
    l = a.shape[0]
    r, c = _iota2((l, l), 0), _iota2((l, l), 1)
    t = jnp.where((r >= c) if lower else (r <= c), 1.0, 0.0).astype(BF16)
    hi = a.astype(BF16)
    r1 = a - hi.astype(F32)
    mid = r1.astype(BF16)
    lo = (r1 - mid.astype(F32)).astype(BF16)
    dn = _DN["nn"]
    return (lax.dot_general(t, hi, dn, preferred_element_type=F32) + lax.dot_general(t, mid, dn, preferred_element_type=F32)
            + lax.dot_general(t, lo, dn, preferred_element_type=F32))


@jax.custom_vjp
def _cumsum_rows(a):
    return _tri_apply(a, True)


_cumsum_rows.defvjp(lambda a: (_tri_apply(a, True), None), lambda _, g: (_tri_apply(g, False),))

SSD_GROUP_HEADS = 8
SSD_GROUP_PAIRS = 4


def _ssd_group(x0, x1, x2, x3, dt_raw, bias, alog, bm, cm, p0, p1, p2, p3):
    xs, prevs = (x0, x1, x2, x3), (p0, p1, p2, p3)
    dt = _softplus(dt_raw + bias)
    a = dt * (-jnp.exp(alog))
    acs = _cumsum_rows(a)
    alast = jnp.sum(a, axis=0, keepdims=True)
    dt_t, acs_t = dt.T, acs.T
    cb = _nt(cm, bm)
    tri = _iota2((CHUNK, CHUNK), 0) >= _iota2((CHUNK, CHUNK), 1)
    lane = _iota2((CHUNK, LANES), 1)
    sub = _iota2((LANES, CHUNK), 0)
    lane1 = _iota2((1, LANES), 1)

    def column(v, i):
        return jnp.broadcast_to(jnp.sum(jnp.where(lane == i, v, 0.0), axis=1, keepdims=True), (CHUNK, LANES))

    def row(vt, i):
        return jnp.broadcast_to(jnp.sum(jnp.where(sub == i, vt, 0.0), axis=0, keepdims=True), (CHUNK, CHUNK))

    heads = []
    for i in range(SSD_GROUP_HEADS):
        col_a = column(acs, i)
        al = jnp.sum(jnp.where(lane1 == i, alast, 0.0), axis=1, keepdims=True)
        m = cb * jnp.exp(jnp.where(tri, col_a - row(acs_t, i), NEG)) * row(dt_t, i)
        heads.append((m, jnp.exp(col_a), column(dt, i) * jnp.exp(al - col_a), jnp.exp(al)))
    lo_lanes = _lane_lt64((CHUNK, LANES))
    lo_rows = _iota2((LANES, SSM_STATE), 0) < HEAD_DIM
    ys, news = [], []
    for j in range(SSD_GROUP_PAIRS):
        (m0, ea0, w0, cd0), (m1, ea1, w1, cd1) = heads[2 * j], heads[2 * j + 1]
        x, prev = xs[j], prevs[j]
        ydiag = jnp.where(lo_lanes, _nn(m0, x), _nn(m1, x))
        yoff = jnp.where(lo_lanes, _nt(cm * ea0, prev), _nt(cm * ea1, prev))
        states = jnp.where(lo_rows, _tn(x, bm * w0), _tn(x, bm * w1))
        ys.append(ydiag + yoff)
        news.append(prev * jnp.where(lo_rows, cd0, cd1) + states)
    return tuple(ys) + tuple(news)


SSD_GROUPS = 2


def _ssd2_specs(seq, rev):
    ncs = seq // CHUNK
    wide = SSD_GROUPS * LANES

    def row(s, c):
        return s * ncs + (ncs - 1 - c if rev else c)

    return dict(
        x=pl.BlockSpec((CHUNK, C_WIDTH), lambda s, c: (row(s, c), 0)),
        dt=pl.BlockSpec((CHUNK, wide), lambda s, c: (row(s, c), 0)),
        vec=pl.BlockSpec((1, wide), lambda s, c: (0, 0)),
        bm=pl.BlockSpec((CHUNK, wide), lambda s, c: (row(s, c), C_WIDTH // wide)),
        cm=pl.BlockSpec((CHUNK, wide), lambda s, c: (row(s, c), C_WIDTH // wide + 1)),
        st=pl.BlockSpec((None, C_WIDTH // LANES, LANES, SSM_STATE), lambda s, c: (row(s, c), 0, 0, 0)),
        ncs=ncs,
    )


def _lane_blocks(ref, grp):
    return [ref[:, pl.ds((grp * SSD_GROUP_PAIRS + j) * LANES, LANES)] for j in range(SSD_GROUP_PAIRS)]


def _group_block(ref, grp):
    return ref[:, pl.ds(grp * LANES, LANES)]


def _ssd2_fwd(xbc_act, dt_raw, dt_bias, a_log, seq, name):
    t = xbc_act.shape[0]
    sp = _ssd2_specs(seq, False)

    npair = SSD_GROUP_PAIRS

    def body(x_ref, dt_ref, bias_ref, alog_ref, bm_ref, cm_ref, y_ref, prev_ref, state):
        @pl.when(pl.program_id(1) == 0)
        def _():
            state[...] = jnp.zeros_like(state)

        for grp in range(SSD_GROUPS):
            prevs = [state[grp * npair + j] for j in range(npair)]
            for j in range(npair):
                prev_ref[grp * npair + j] = prevs[j]
            res = _ssd_group(*_lane_blocks(x_ref, grp), _group_block(dt_ref, grp), _group_block(bias_ref, grp),
                             _group_block(alog_ref, grp), _group_block(bm_ref, grp), _group_block(cm_ref, grp), *prevs)
            for j in range(npair):
                y_ref[:, pl.ds((grp * npair + j) * LANES, LANES)] = res[j]
                state[grp * npair + j] = res[npair + j]

    return pl.pallas_call(
        body, name=name, grid=(t // seq, sp["ncs"]),
        in_specs=[sp["x"], sp["dt"], sp["vec"], sp["vec"], sp["bm"], sp["cm"]],
        out_specs=[sp["x"], sp["st"]],
        out_shape=[jax.ShapeDtypeStruct((t, C_WIDTH), F32),
                   jax.ShapeDtypeStruct((t // CHUNK, C_WIDTH // LANES, LANES, SSM_STATE), F32)],
        scratch_shapes=[pltpu.VMEM((C_WIDTH // LANES, LANES, SSM_STATE), F32)],
        compiler_params=_cparams("parallel", "arbitrary"),
    )(xbc_act, dt_raw, dt_bias, a_log, xbc_act, xbc_act)


def _ssd2_bwd(xbc_act, dt_raw, dt_bias, a_log, prev_saved, dy, seq, name):
    t = xbc_act.shape[0]
    sp = _ssd2_specs(seq, True)
    npair = SSD_GROUP_PAIRS

    def body(x_ref, dt_ref, bias_ref, alog_ref, bm_ref, cm_ref, prev_ref, dy_ref,
             dx_ref, ddt_ref, dbias_ref, dalog_ref, dbm_ref, dcm_ref, dstate):
        @pl.when(pl.program_id(1) == 0)
        def _():
            dstate[...] = jnp.zeros_like(dstate)

        @pl.when(jnp.logical_and(pl.program_id(0) == 0, pl.program_id(1) == 0))
        def _():
            dbias_ref[...] = jnp.zeros_like(dbias_ref)
            dalog_ref[...] = jnp.zeros_like(dalog_ref)

        for grp in range(SSD_GROUPS):
            lanes = pl.ds(grp * LANES, LANES)
            _, vjp = jax.vjp(_ssd_group, *_lane_blocks(x_ref, grp), _group_block(dt_ref, grp), _group_block(bias_ref, grp),
                             _group_block(alog_ref, grp), _group_block(bm_ref, grp), _group_block(cm_ref, grp),
                             *[prev_ref[grp * npair + j] for j in range(npair)])
            grads = vjp(tuple(_lane_blocks(dy_ref, grp)) + tuple(dstate[grp * npair + j] for j in range(npair)))
            for j in range(npair):
                dx_ref[:, pl.ds((grp * npair + j) * LANES, LANES)] = grads[j]
                dstate[grp * npair + j] = grads[npair + 5 + j]
            ddt_ref[:, lanes] = grads[npair].astype(BF16)
            dbias_ref[:, lanes] += grads[npair + 1]
            dalog_ref[:, lanes] += grads[npair + 2]
            dbm_ref[:, lanes] = grads[npair + 3]
            dcm_ref[:, lanes] = grads[npair + 4]

    return pl.pallas_call(
        body, name=name, grid=(t // seq, sp["ncs"]),
        in_specs=[sp["x"], sp["dt"], sp["vec"], sp["vec"], sp["bm"], sp["cm"], sp["st"], sp["x"]],
        out_specs=[sp["x"], sp["dt"], sp["vec"], sp["vec"], sp["dt"], sp["dt"]],
        out_shape=[jax.ShapeDtypeStruct((t, C_WIDTH), F32), jax.ShapeDtypeStruct((t, 2 * LANES), BF16),
                   jax.ShapeDtypeStruct((1, 2 * LANES), F32), jax.ShapeDtypeStruct((1, 2 * LANES), F32),
                   jax.ShapeDtypeStruct((t, 2 * SSM_STATE), F32), jax.ShapeDtypeStruct((t, 2 * SSM_STATE), F32)],
        scratch_shapes=[pltpu.VMEM((C_WIDTH // LANES, LANES, SSM_STATE), F32)],
        compiler_params=_cparams("arbitrary", "arbitrary"),
    )(xbc_act, dt_raw, dt_bias, a_log, xbc_act, xbc_act, prev_saved, dy)


def _ssd2_assemble(dxs_ssd, dxs_skip, dbm, dcm, name):
    t = dxs_ssd.shape[0]
    tm = _row_tile(t)

    def body(a_ref, b_ref, dbm_ref, dcm_ref, o_ref):
        o_ref[:, pl.ds(0, C_WIDTH)] = a_ref[...] + b_ref[...]
        o_ref[:, pl.ds(C_WIDTH, 2 * SSM_STATE)] = dbm_ref[...]
        o_ref[:, pl.ds(C_WIDTH + 2 * SSM_STATE, 2 * SSM_STATE)] = dcm_ref[...]

    wide = pl.BlockSpec((tm, C_WIDTH), lambda i: (i, 0))
    narrow = pl.BlockSpec((tm, 2 * SSM_STATE), lambda i: (i, 0))
    return pl.pallas_call(
        body, name=name, grid=(t // tm,), in_specs=[wide, wide, narrow, narrow],
        out_specs=pl.BlockSpec((tm, D_CONV_C), lambda i: (i, 0)),
        out_shape=jax.ShapeDtypeStruct((t, D_CONV_C), F32),
        compiler_params=_cparams("parallel"),
    )(dxs_ssd, dxs_skip, dbm, dcm)


def _expand_mats():
    head = jnp.arange(LANES)[:, None]
    e64 = (head == (jnp.arange(C_WIDTH)[None, :] // HEAD_DIM)).astype(F32)
    e128 = (head == (jnp.arange(C_HEADS * LANES)[None, :] // LANES)).astype(F32)
    return e64, e128


def _ssd_prep_fn(dt_raw, dt_bias, a_log, e64, e128):
    dt = _softplus(dt_raw + dt_bias)
    a = dt * (-jnp.exp(a_log))
    incl = (_iota2((CHUNK, CHUNK), 0) >= _iota2((CHUNK, CHUNK), 1)).astype(F32)
    acs = _xdot(incl, a)
    alast = _xdot(jnp.ones((CHUNK, CHUNK), F32), a)
    return _xdot(dt, e64), _xdot(acs, e64), _xdot(alast, e64), _xdot(acs, e128)


def _ssd_prep_specs():
    blk = lambda w: pl.BlockSpec((CHUNK, w), lambda i: (i, 0))
    const = lambda r, w: pl.BlockSpec((r, w), lambda i: (0, 0))
    ins = [blk(LANES), const(1, LANES), const(1, LANES), const(LANES, C_WIDTH), const(LANES, C_HEADS * LANES)]
    outs = [blk(C_WIDTH), blk(C_WIDTH), blk(C_WIDTH), blk(C_HEADS * LANES)]
    return ins, outs


def _ssd_prep_fwd(dt_raw, dt_bias, a_log, name):
    t = dt_raw.shape[0]
    e64, e128 = _expand_mats()
    ins, outs = _ssd_prep_specs()

    def body(raw_ref, bias_ref, alog_ref, e64_ref, e128_ref, dt_ref, acs_ref, alast_ref, acs128_ref):
        res = _ssd_prep_fn(raw_ref[...], bias_ref[...], alog_ref[...], e64_ref[...], e128_ref[...])
        for ref, v in zip((dt_ref, acs_ref, alast_ref, acs128_ref), res):
            ref[...] = v

    return pl.pallas_call(
        body, name=name, grid=(t // CHUNK,), in_specs=ins, out_specs=outs,
        out_shape=[jax.ShapeDtypeStruct((t, C_WIDTH), F32)] * 3 + [jax.ShapeDtypeStruct((t, C_HEADS * LANES), F32)],
        compiler_params=_cparams("parallel"),
    )(dt_raw, dt_bias, a_log, e64, e128)


def _ssd_prep_bwd(dt_raw, dt_bias, a_log, d_dt, d_acs, d_alast, d_acs128, name):
    t = dt_raw.shape[0]
    e64, e128 = _expand_mats()
    ins, outs = _ssd_prep_specs()
    vec = pl.BlockSpec((1, LANES), lambda i: (0, 0))

    def body(raw_ref, bias_ref, alog_ref, e64_ref, e128_ref, g0, g1, g2, g3, draw_ref, dbias_ref, dalog_ref):
        @pl.when(pl.program_id(0) == 0)
        def _():
            dbias_ref[...] = jnp.zeros_like(dbias_ref)
            dalog_ref[...] = jnp.zeros_like(dalog_ref)

        e64v, e128v = e64_ref[...], e128_ref[...]
        _, vjp = jax.vjp(lambda r, b, al: _ssd_prep_fn(r, b, al, e64v, e128v),
                         raw_ref[...], bias_ref[...], alog_ref[...])
        draw, dbias, dalog = vjp((g0[...], g1[...], g2[...], g3[...]))
        draw_ref[...] = draw.astype(BF16)
        dbias_ref[...] += dbias
        dalog_ref[...] += dalog

    return pl.pallas_call(
        body, name=name, grid=(t // CHUNK,), in_specs=ins + outs,
        out_specs=[pl.BlockSpec((CHUNK, LANES), lambda i: (i, 0)), vec, vec],
        out_shape=[jax.ShapeDtypeStruct((t, LANES), BF16), jax.ShapeDtypeStruct((1, LANES), F32),
                   jax.ShapeDtypeStruct((1, LANES), F32)],
        compiler_params=_cparams("arbitrary"),
    )(dt_raw, dt_bias, a_log, e64, e128, d_dt, d_acs, d_alast, d_acs128)


def _ssd_chunk(x, dt, acs, alast, col0, col1, bm, cm, prev):
    xdt = x * dt
    cb = _nt(cm, bm)
    tri = _iota2((CHUNK, CHUNK), 0) >= _iota2((CHUNK, CHUNK), 1)
    l0 = jnp.exp(jnp.where(tri, col0 - col0.T, NEG))
    l1 = jnp.exp(jnp.where(tri, col1 - col1.T, NEG))
    ydiag = jnp.where(_lane_lt64((CHUNK, LANES)), _nn(cb * l0, xdt), _nn(cb * l1, xdt))
    states = _tn(xdt * jnp.exp(alast - acs), bm)
    yoff = _nt(cm, prev) * jnp.exp(acs)
    new = prev * jnp.exp(alast).T + states
    return ydiag + yoff, new


def _ssd_specs(seq, rev):
    ncs = seq // CHUNK
    npair = C_WIDTH // LANES

    def row(s, c):
        return s * ncs + (ncs - 1 - c if rev else c)

    return dict(
        x=pl.BlockSpec((CHUNK, LANES), lambda s, j, c: (row(s, c), j)),
        bm=pl.BlockSpec((CHUNK, SSM_STATE), lambda s, j, c: (row(s, c), C_WIDTH // LANES + j // 4)),
        cm=pl.BlockSpec((CHUNK, SSM_STATE), lambda s, j, c: (row(s, c), C_WIDTH // LANES + 2 + j // 4)),
        col=pl.BlockSpec((CHUNK, 2 * LANES), lambda s, j, c: (row(s, c), j)),
        st=pl.BlockSpec((None, None, LANES, SSM_STATE), lambda s, j, c: (row(s, c), j, 0, 0)),
        npair=npair, ncs=ncs,
    )


def _ssd_fwd(xbc_act, dt64, acs64, alast64, acs128, seq, name):
    t = xbc_act.shape[0]
    sp = _ssd_specs(seq, False)

    def body(x_ref, dt_ref, acs_ref, alast_ref, col_ref, bm_ref, cm_ref, y_ref, prev_ref, state):
        @pl.when(pl.program_id(2) == 0)
        def _():
            state[...] = jnp.zeros_like(state)

        prev = state[...]
        prev_ref[...] = prev
        y, new = _ssd_chunk(x_ref[...], dt_ref[...], acs_ref[...], alast_ref[...],
                            col_ref[:, pl.ds(0, LANES)], col_ref[:, pl.ds(LANES, LANES)],
                            bm_ref[...], cm_ref[...], prev)
        y_ref[...] = y
        state[...] = new

    return pl.pallas_call(
        body, name=name, grid=(t // seq, sp["npair"], sp["ncs"]),
        in_specs=[sp["x"], sp["x"], sp["x"], sp["x"], sp["col"], sp["bm"], sp["cm"]],
        out_specs=[sp["x"], sp["st"]],
        out_shape=[jax.ShapeDtypeStruct((t, C_WIDTH), F32),
                   jax.ShapeDtypeStruct((t // CHUNK, sp["npair"], LANES, SSM_STATE), F32)],
        scratch_shapes=[pltpu.VMEM((LANES, SSM_STATE), F32)],
        compiler_params=_cparams("parallel", "parallel", "arbitrary"),
    )(xbc_act, dt64, acs64, alast64, acs128, xbc_act, xbc_act)


def _ssd_bwd(xbc_act, dt64, acs64, alast64, acs128, prev_saved, dy, seq, name):
    t = xbc_act.shape[0]
    sp = _ssd_specs(seq, True)

    def body(x_ref, dt_ref, acs_ref, alast_ref, col_ref, bm_ref, cm_ref, prev_ref, dy_ref,
             dx_ref, ddt_ref, dacs_ref, dalast_ref, dcol_ref, dbc_ref, dstate):
        @pl.when(pl.program_id(2) == 0)
        def _():
            dstate[...] = jnp.zeros_like(dstate)

        _, vjp = jax.vjp(_ssd_chunk, x_ref[...], dt_ref[...], acs_ref[...], alast_ref[...],
                         col_ref[:, pl.ds(0, LANES)], col_ref[:, pl.ds(LANES, LANES)],
                         bm_ref[...], cm_ref[...], prev_ref[...])
        dx, ddt, dacs, dalast, dc0, dc1, dbm, dcm, dprev = vjp((dy_ref[...], dstate[...]))
        dx_ref[...] = dx
        ddt_ref[...] = ddt
        dacs_ref[...] = dacs
        dalast_ref[...] = dalast
        dcol_ref[:, pl.ds(0, LANES)] = dc0
        dcol_ref[:, pl.ds(LANES, LANES)] = dc1
        dbc_ref[:, pl.ds(0, SSM_STATE)] = dbm
        dbc_ref[:, pl.ds(SSM_STATE, SSM_STATE)] = dcm
        dstate[...] = dprev

    wide = jax.ShapeDtypeStruct((t, C_WIDTH), F32)
    return pl.pallas_call(
        body, name=name, grid=(t // seq, sp["npair"], sp["ncs"]),
        in_specs=[sp["x"], sp["x"], sp["x"], sp["x"], sp["col"], sp["bm"], sp["cm"], sp["st"], sp["x"]],
        out_specs=[sp["x"], sp["x"], sp["x"], sp["x"], sp["col"], sp["col"]],
        out_shape=[wide, wide, wide, wide, jax.ShapeDtypeStruct((t, 2 * C_WIDTH), F32),
                   jax.ShapeDtypeStruct((t, 2 * C_WIDTH), F32)],
        scratch_shapes=[pltpu.VMEM((LANES, SSM_STATE), F32)],
        compiler_params=_cparams("parallel", "parallel", "arbitrary"),
    )(xbc_act, dt64, acs64, alast64, acs128, xbc_act, xbc_act, prev_saved, dy)


def _ssd_post_fn(y, xs, z, dskip, g):
    v = (y + dskip * xs) * _silu(z)
    return v * lax.rsqrt(jnp.mean(v * v, axis=-1, keepdims=True) + EPS) * g


def _ssd_post_specs(tm, order):
    gw = C_WIDTH // 2

    def im(f):
        return lambda *ids: f(*order(*ids))
    return dict(
        blk=pl.BlockSpec((tm, gw), im(lambda g, r: (r, g))),
        z=pl.BlockSpec((tm, gw), im(lambda g, r: (r, COL_Z * LANES // gw + g))),
        vec=pl.BlockSpec((1, gw), im(lambda g, r: (0, g))),
    )


def _ssd_post_fwd(y_ssd, xbc_act, proj, dskip64, norm_g, name):
    t = y_ssd.shape[0]
    tm = _row_tile(t)
    sp = _ssd_post_specs(tm, lambda r, g: (g, r))

    def body(y_ref, xs_ref, z_ref, ds_ref, g_ref, o_ref):
        o_ref[...] = _ssd_post_fn(y_ref[...], xs_ref[...], z_ref[...], ds_ref[...], g_ref[...]).astype(BF16)

    return pl.pallas_call(
        body, name=name, grid=(t // tm, 2),
        in_specs=[sp["blk"], sp["blk"], sp["z"], sp["vec"], sp["vec"]], out_specs=sp["blk"],
        out_shape=jax.ShapeDtypeStruct((t, C_WIDTH), BF16),
        compiler_params=_cparams("parallel", "parallel"),
    )(y_ssd, xbc_act, proj, dskip64, norm_g)


def _ssd_post_bwd(y_ssd, xbc_act, proj, dskip64, norm_g, dyc, name, dy_col=0):
    t = y_ssd.shape[0]
    tm = _row_tile(t)
    sp = _ssd_post_specs(tm, lambda g, r: (g, r))
    dy_spec = pl.BlockSpec((tm, C_WIDTH // 2), lambda g, r: (r, dy_col + g))

    def body(y_ref, xs_ref, z_ref, ds_ref, g_ref, dyc_ref, dy_ref, dxs_ref, dz_ref, dds_ref, dg_ref):
        @pl.when(pl.program_id(1) == 0)
        def _():
            dds_ref[...] = jnp.zeros_like(dds_ref)
            dg_ref[...] = jnp.zeros_like(dg_ref)

        _, vjp = jax.vjp(_ssd_post_fn, y_ref[...], xs_ref[...], z_ref[...], ds_ref[...], g_ref[...])
        dy, dxs, dz, dds, dg = vjp(dyc_ref[...])
        dy_ref[...] = dy
        dxs_ref[...] = dxs
        dz_ref[...] = dz.astype(BF16)
        dds_ref[...] += dds
        dg_ref[...] += dg

    wide = jax.ShapeDtypeStruct((t, C_WIDTH), F32)
    vec = jax.ShapeDtypeStruct((1, C_WIDTH), F32)
    return pl.pallas_call(
        body, name=name, grid=(2, t // tm),
        in_specs=[sp["blk"], sp["blk"], sp["z"], sp["vec"], sp["vec"], dy_spec],
        out_specs=[sp["blk"], sp["blk"], sp["blk"], sp["vec"], sp["vec"]],
        out_shape=[wide, wide, jax.ShapeDtypeStruct((t, C_WIDTH), BF16), vec, vec],
        compiler_params=_cparams("parallel", "arbitrary"),
    )(y_ssd, xbc_act, proj, dskip64, norm_g, dyc)


def _ssd_assemble(dxs_ssd, dxs_skip, dbc, name):
    t = dxs_ssd.shape[0]
    tm = _row_tile(t)
    npair = C_WIDTH // LANES

    def body(a_ref, b_ref, dbc_ref, o_ref):
        o_ref[:, pl.ds(0, C_WIDTH)] = a_ref[...] + b_ref[...]
        for grp in range(2):
            for which in range(2):
                acc = jnp.zeros((tm, SSM_STATE), F32)
                for j in range(grp * npair // 2, (grp + 1) * npair // 2):
                    acc = acc + dbc_ref[:, pl.ds((2 * j + which) * SSM_STATE, SSM_STATE)]
                o_ref[:, pl.ds(C_WIDTH + (2 * which + grp) * SSM_STATE, SSM_STATE)] = acc

    return pl.pallas_call(
        body, name=name, grid=(t // tm,),
        in_specs=[pl.BlockSpec((tm, C_WIDTH), lambda i: (i, 0))] * 2 + [pl.BlockSpec((tm, 2 * C_WIDTH), lambda i: (i, 0))],
        out_specs=pl.BlockSpec((tm, D_CONV_C), lambda i: (i, 0)),
        out_shape=jax.ShapeDtypeStruct((t, D_CONV_C), F32),
        compiler_params=_cparams("parallel"),
    )(dxs_ssd, dxs_skip, dbc)


def _pad_taps(w):
    return jnp.pad(w, ((0, CONV_PAD - w.shape[0]), (0, 0)))


def _pad_heads(v):
    return jnp.pad(v, (0, LANES - v.shape[0])).reshape(1, LANES)


def _group_heads(a):
    pad = [(0, 0)] * (a.ndim - 1) + [(0, LANES - SSD_GROUP_HEADS)]
    return jnp.concatenate([jnp.pad(a[..., :SSD_GROUP_HEADS], pad), jnp.pad(a[..., SSD_GROUP_HEADS:], pad)], axis=-1)


def _ungroup_heads(a):
    return jnp.concatenate([a[..., :SSD_GROUP_HEADS], a[..., LANES:LANES + SSD_GROUP_HEADS]], axis=-1)


def _layer_fwd(x, p, seq, li, after=()):
    n = f"l{li}_"
    h1 = _rms_fwd(x, p["norm1_g"], n + "rms1", after=after)
    proj = _matmul(h1, p["w_main"], mode="nn", name=n + "inproj")
    dt_raw = _matmul(h1, p["w_dt"], mode="nn", name=n + "inproj_dt")
    row = lambda v: v.reshape(1, -1)
    ya, conv_a = _conv_fwd("a", proj, _pad_taps(p["conv_a_w"]), row(p["conv_a_b"]), (row(p["ln_a_g"]), row(p["ln_a_b"])),
                           seq, n + "conva", out_dtype=BF16, keep_conv=True)
    yb = _gmlp_fwd(proj, row(p["ln_b_g"]), row(p["ln_b_b"]), p["w_spatial"], p["b_spatial"], n + "gmlp")
    xbc_act = _conv_fwd("c", proj, _pad_taps(p["conv_c_w"]), row(p["conv_c_b"]), (), seq, n + "convc")
    y_ssd, prev = _ssd2_fwd(xbc_act, dt_raw, _group_heads(row(p["dt_bias"])), _group_heads(row(p["a_log"])), seq, n + "ssd")
    dskip64 = jnp.repeat(p["d_skip"], HEAD_DIM).reshape(1, C_WIDTH)
    yc = _ssd_post_fwd(y_ssd, xbc_act, proj, dskip64, row(p["norm_c_g"]), n + "ssdpost")
    ycat = jnp.concatenate([ya, yb, yc], axis=1)
    x1 = _matmul(ycat, p["w_out"], mode="nn", name=n + "outproj", add=x)
    h2 = _rms_fwd(x1, p["norm2_g"], n + "rms2")
    u, act = _matmul(h2, p["w_ff1"], mode="nn", name=n + "ff1", epilogue=_relu2_epilogue, out_dtypes=(F32, BF16),
                     b_chips=True)
    x2 = _matmul(act, p["w_ff2"], mode="nn", name=n + "ff2", add=x1)
    saved = dict(x=x, h1=h1, proj=proj, conv_a=conv_a, dt_raw=dt_raw, xbc_act=xbc_act, prev=prev, y_ssd=y_ssd,
                 dskip64=dskip64, ycat=ycat, x1=x1, h2=h2, u=u, act=act)
    return x2, saved


def _layer_bwd(dx2, p, s, seq, li, after=(), on_ffn_grads=None):
    n = f"l{li}_b_"
    row = lambda v: v.reshape(1, -1)
    g = {}
    du = _matmul(dx2, p["w_ff2"], mode="nt", name=n + "ff2_dx", epilogue=_relu2_bwd_epilogue, extra=s["u"],
                 out_dtypes=(BF16,), after=after)
    g["w_ff2"] = _matmul(s["act"], dx2, mode="tn", name=n + "ff2_dw")
    g["w_ff1"] = _matmul(s["h2"], du, mode="tn", name=n + "ff1_dw", out_chips=True)
    dh2 = _matmul(du, p["w_ff1"], mode="nt", name=n + "ff1_dx", b_chips=True)
    dx1, g["norm2_g"] = _rms_bwd(s["x1"], p["norm2_g"], dh2, dx2, n + "rms2")
    g["w_out"] = _matmul(s["ycat"], dx1, mode="tn", name=n + "out_dw")
    dycat = _matmul(dx1, p["w_out"], mode="nt", name=n + "out_dx",
                    after=() if on_ffn_grads is None else on_ffn_grads(g))
    proj = s["proj"]
    (dval, dgate), dwa, dba, (dlag, dlab) = _conv_bwd(
        "a", proj, _pad_taps(p["conv_a_w"]), row(p["conv_a_b"]), (row(p["ln_a_g"]), row(p["ln_a_b"])), dycat, seq,
        n + "conva", dy_col=0, conv_out=s["conv_a"])
    g["conv_a_w"], g["conv_a_b"], g["ln_a_g"], g["ln_a_b"] = dwa[:CONV_A_K], dba[0], dlag[0], dlab[0]
    dbu, dbv, dlbg, dlbb, g["w_spatial"], g["b_spatial"] = _gmlp_bwd(
        proj, row(p["ln_b_g"]), row(p["ln_b_b"]), p["w_spatial"], p["b_spatial"], dycat, n + "gmlp",
        dy_col=A_WIDTH // LANES)
    g["ln_b_g"], g["ln_b_b"] = dlbg[0], dlbb[0]
    dy_ssd, dxs_skip, dz, dds, dncg = _ssd_post_bwd(s["y_ssd"], s["xbc_act"], proj, s["dskip64"], row(p["norm_c_g"]),
                                                    dycat, n + "ssdpost", dy_col=(A_WIDTH + B_WIDTH) * 2 // C_WIDTH)
    g["norm_c_g"] = dncg[0]
    g["d_skip"] = dds.reshape(C_HEADS, HEAD_DIM).sum(axis=1)
    dxs, ddt_raw, ddtb, dalog, dbm, dcm = _ssd2_bwd(
        s["xbc_act"], s["dt_raw"], _group_heads(row(p["dt_bias"])), _group_heads(row(p["a_log"])), s["prev"], dy_ssd, seq,
        n + "ssd")
    g["dt_bias"], g["a_log"] = _ungroup_heads(ddtb)[0], _ungroup_heads(dalog)[0]
    dconv = _ssd2_assemble(dxs, dxs_skip, dbm, dcm, n + "ssdasm")
    (dxbc,), dwc, dbcv, _ = _conv_bwd("c", proj, _pad_taps(p["conv_c_w"]), row(p["conv_c_b"]), (), dconv, seq, n + "convc")
    g["conv_c_w"], g["conv_c_b"] = dwc[:CONV_C_K], dbcv[0]
    dproj = jnp.concatenate([dval, dgate, dbu, dbv, dz, dxbc], axis=1)
    g["w_main"] = _matmul(s["h1"], dproj, mode="tn", name=n + "in_dw")
    g["w_dt"] = _matmul(s["h1"], ddt_raw, mode="tn", name=n + "indt_dw")
    dh1 = _matmul(dproj, p["w_main"], mode="nt", name=n + "in_dx")
    dh1 = _matmul(ddt_raw, p["w_dt"], mode="nt", name=n + "indt_dx", add=dh1)
    dx, g["norm1_g"] = _rms_bwd(s["x"], p["norm1_g"], dh1, dx1, n + "rms1")
    return dx, g


EW_BLOCK_BYTES = 1 << 20


def _ew(fn, ins, out_dtypes, name, leads=None):
    leads = leads or [None] * len(ins)
    rows, c = ins[0].shape[-2:]
    tr = _pick(rows, [t for t in (2048, 1024, 512, 256, 128, 64, 32, 16, 8) if t * c * 4 <= EW_BLOCK_BYTES])
    n_in = len(ins)

    def spec(lead):
        if lead is None:
            return pl.BlockSpec((tr, c), lambda i: (i, 0))
        return pl.BlockSpec((None, tr, c), functools.partial(lambda i, k: (k, i, 0), k=lead))

    def body(*refs):
        outs = fn(*[r[...].astype(F32) for r in refs[:n_in]])
        for o_ref, o in zip(refs[n_in:], outs):
            o_ref[...] = o.astype(o_ref.dtype)

    return pl.pallas_call(
        body, name=name, grid=(rows // tr,),
        in_specs=[spec(l) for l in leads], out_specs=[spec(None)] * len(out_dtypes),
        out_shape=[jax.ShapeDtypeStruct((rows, c), dt) for dt in out_dtypes],
        compiler_params=_cparams("parallel"),
    )(*ins)


def _adam_fn(w, g, m, v):
    m2 = ADAM_B1 * m + (1.0 - ADAM_B1) * g
    v2 = ADAM_B2 * v + (1.0 - ADAM_B2) * (g * g)
    m_hat = m2 / (1.0 - ADAM_B1 ** ADAM_STEP)
    v_hat = v2 / (1.0 - ADAM_B2 ** ADAM_STEP)
    delta = -ADAM_LR * (m_hat / (jnp.sqrt(v_hat) + ADAM_EPS) + ADAM_WD * w)
    return delta, m2, v2


def _adam_layer(li, w, g, m, v, prev, name, after=()):
    depth, r, c = w.shape
    tr = _pick(r, [t for t in (2048, 1024, 512, 256, 128, 64, 32, 16, 8) if t * c * 4 <= EW_BLOCK_BYTES])
    n_prev = 0 if prev is None else len(prev)

    def body(w_ref, g_ref, m_ref, v_ref, *rest):
        outs = rest[n_prev + len(after):]
        gv = g_ref[...]
        res = _adam_fn(w_ref[...], gv, m_ref[...], v_ref[...])
        for o_ref, o in zip(outs, (gv,) + tuple(res)):
            o_ref[...] = o

    layer = pl.BlockSpec((None, tr, c), lambda i: (li, i, 0))
    stacked = jax.ShapeDtypeStruct((depth, r, c), F32)
    return pl.pallas_call(
        body, name=name, grid=(r // tr,),
        in_specs=[layer, pl.BlockSpec((tr, c), lambda i: (i, 0)), layer, layer] + [_UNREAD] * (n_prev + len(after)),
        out_specs=[layer] * 4, out_shape=[stacked] * 4,
        input_output_aliases={4 + i: i for i in range(n_prev)},
        compiler_params=_cparams("parallel"),
    )(w, g, m, v, *(prev or ()), *after)


_ANY = pl.BlockSpec(memory_space=pl.ANY)


def _mesh_pos():
    return lax.axis_index("x"), lax.axis_index("y"), lax.axis_index("c")


def _peer_chips(x, y):
    return [(1 - x, y), (x, 1 - y), (1 - x, 1 - y)]


def _remote(src, dst, send_sems, recv_sems, sem, to):
    return pltpu.make_async_remote_copy(src_ref=src, dst_ref=dst, send_sem=send_sems.at[sem],
                                        recv_sem=recv_sems.at[sem], device_id=to, device_id_type=MESH)


def _half_rows(n_rows, which):
    half = n_rows // 2
    return pl.ds(pl.multiple_of(which * half, 8), half)


def _comm_call(body, ins, out_shapes, n_sems, name):
    scratch = [pltpu.SemaphoreType.DMA((n_sems,)), pltpu.SemaphoreType.DMA((n_sems,))]
    return pl.pallas_call(
        body, name=name, in_specs=[_ANY] * len(ins), out_specs=[_ANY] * len(out_shapes),
        out_shape=out_shapes, scratch_shapes=scratch,
    )(*ins)


def _gather_weights(big, small, name):
    nb, ns = len(big), len(small)
    n = nb + ns

    def body(*refs):
        ins, outs = refs[:n], refs[n:2 * n]
        send_sems, recv_sems = refs[2 * n:]
        x, y, c = _mesh_pos()
        q = 2 * x + y
        me, sib = (x, y, c), (x, y, 1 - c)
        chips = _peer_chips(x, y)
        rem = functools.partial(_remote, send_sems=send_sems, recv_sems=recv_sems)
        first = []
        for i in range(nb):
            mine = _half_rows(big[i].shape[0], c)
            for k, (px, py) in enumerate(chips):
                first.append(rem(ins[i].at[mine], outs[i].at[q, mine], sem=6 * i + k, to=(px, py, c)))
        for j in range(ns):
            for k, (px, py) in enumerate(chips):
                first.append(rem(ins[nb + j], outs[nb + j].at[q], sem=6 * nb + 3 * j + k, to=(px, py, c)))
        for cp in first:
            cp.start()
        passed = []
        for i in range(nb):
            mine = _half_rows(big[i].shape[0], c)
            for k, (px, py) in enumerate(chips):
                landed = outs[i].at[2 * px + py, mine]
                rem(landed, landed, sem=6 * i + k, to=me).wait_recv()
                fwd = rem(landed, landed, sem=6 * i + 3 + k, to=sib)
                fwd.start()
                passed.append(fwd)
        for i in range(nb):
            other = _half_rows(big[i].shape[0], 1 - c)
            for k, (px, py) in enumerate(chips):
                theirs = outs[i].at[2 * px + py, other]
                rem(theirs, theirs, sem=6 * i + 3 + k, to=me).wait_recv()
        for j in range(ns):
            for k, (px, py) in enumerate(chips):
                dst = outs[nb + j].at[2 * px + py]
                rem(dst, dst, sem=6 * nb + 3 * j + k, to=me).wait_recv()
        for cp in first + passed:
            cp.wait_send()

    out_shapes = [jax.ShapeDtypeStruct((N_CHIPS,) + a.shape, a.dtype) for a in list(big) + list(small)]
    return _comm_call(body, list(big) + list(small), out_shapes, 6 * nb + 3 * ns, name)


def _sibling_other_halves(gs, name):
    n = len(gs)

    def other_half(ref, shape, c):
        rows = _half_rows(shape[-2], 1 - c)
        return ref.at[rows] if len(shape) == 2 else ref.at[:, rows]

    def body(*refs):
        ins, outs = refs[:n], refs[n:2 * n]
        send_sems, recv_sems = refs[2 * n:]
        x, y, c = _mesh_pos()
        copies = [_remote(other_half(ins[i], gs[i].shape, c), outs[i], send_sems, recv_sems, i, (x, y, 1 - c))
                  for i in range(n)]
        for cp in copies:
            cp.start()
        for cp in copies:
            cp.wait()

    out_shapes = [jax.ShapeDtypeStruct(g.shape[:-2] + (g.shape[-2] // 2, g.shape[-1]), g.dtype) for g in gs]
    return _comm_call(body, list(gs), out_shapes, n, name)


IN_SHARD = D_IN_PROJ // N_CHIPS


def _chipsum_in(mine, mine_dt, theirs, theirs_dt, name):
    r = mine.shape[0]
    tr = _pick(r, (128, 64, 32, 16, 8))
    last = D_MAIN - (N_CHIPS - 1) * IN_SHARD

    def body(a_ref, adt_ref, b_ref, bdt_ref, o32_ref, o16_ref):
        for p in range(N_CHIPS):
            wid = IN_SHARD if p < N_CHIPS - 1 else last
            s = a_ref[:, pl.ds(IN_SHARD * p, wid)] + b_ref[:, pl.ds(IN_SHARD * p, wid)]
            o32_ref[p, :, pl.ds(0, wid)] = s
            o16_ref[p, :, pl.ds(0, wid)] = s.astype(BF16)
        for grp in range(2):
            src = pl.ds(grp * LANES, SSD_GROUP_HEADS)
            s = adt_ref[:, src] + bdt_ref[:, src]
            dst = pl.ds(last + grp * SSD_GROUP_HEADS, SSD_GROUP_HEADS)
            o32_ref[N_CHIPS - 1, :, dst] = s
            o16_ref[N_CHIPS - 1, :, dst] = s.astype(BF16)

    wide = pl.BlockSpec((tr, D_MAIN), lambda i: (i, 0))
    narrow = pl.BlockSpec((tr, 2 * LANES), lambda i: (i, 0))
    out = pl.BlockSpec((N_CHIPS, tr, IN_SHARD), lambda i: (0, i, 0))
    return pl.pallas_call(
        body, name=name, grid=(r // tr,), in_specs=[wide, narrow, wide, narrow], out_specs=[out, out],
        out_shape=[jax.ShapeDtypeStruct((N_CHIPS, r, IN_SHARD), F32), jax.ShapeDtypeStruct((N_CHIPS, r, IN_SHARD), BF16)],
        compiler_params=_cparams("parallel"),
    )(mine, mine_dt, theirs, theirs_dt)


def _chip_scatter(cs, name):
    n = len(cs)

    def body(*refs):
        ins, outs = refs[:n], refs[n:2 * n]
        send_sems, recv_sems = refs[2 * n:]
        x, y, c = _mesh_pos()
        copies = []
        for i in range(n):
            for k, (px, py) in enumerate(_peer_chips(x, y)):
                copies.append(_remote(ins[i].at[2 * px + py], outs[i].at[k], send_sems, recv_sems, 3 * i + k, (px, py, c)))
        for cp in copies:
            cp.start()
        for cp in copies:
            cp.wait()

    out_shapes = [jax.ShapeDtypeStruct((3,) + a.shape[1:], a.dtype) for a in cs]
    return _comm_call(body, list(cs), out_shapes, 3 * n, name)


_HBM = pl.BlockSpec(memory_space=pltpu.HBM)
_SEM = pl.BlockSpec(memory_space=pltpu.SEMAPHORE)


def _in_hbm(a):
    return pltpu.with_memory_space_constraint(a, pltpu.HBM)


def _split_plan(kind, srcs, lands, x, y, c):
    plan = []
    for src, land in zip(srcs, lands):
        if kind == "sibling":
            rows = _half_rows(src.shape[-2], 1 - c)
            plan.append((src.at[rows] if len(src.shape) == 2 else src.at[:, rows], land, (x, y, 1 - c)))
            continue
        if kind == "allgather":
            peers = [(x, y, 1 - c)] + [(px, py, pc) for px, py in _peer_chips(x, y) for pc in (c, 1 - c)]
            plan += [(src, land.at[4 * x + 2 * y + c], peer) for peer in peers]
            continue
        for k, (px, py) in enumerate(_peer_chips(x, y)):
            if kind == "scatter":
                plan.append((src.at[2 * px + py], land.at[k], (px, py, c)))
            else:
                plan.append((src, land.at[2 * x + y], (px, py, c)))
    return plan


def _split_start(kind, srcs, land_shapes, name):
    n = len(srcs)

    def body(*refs):
        ins, lands = refs[:n], refs[n:2 * n]
        send_sems, recv_sems = refs[2 * n], refs[2 * n + 1]
        token = refs[-1]
        x, y, c = _mesh_pos()
        for i, (src, dst, to) in enumerate(_split_plan(kind, ins, lands, x, y, c)):
            pltpu.make_async_remote_copy(src_ref=src, dst_ref=dst, send_sem=send_sems.at[i], recv_sem=recv_sems.at[i],
                                         device_id=to, device_id_type=MESH).start()
        token[...] = jnp.zeros_like(token)

    zones = [lax.empty(s.shape, s.dtype) for s in land_shapes]
    n_sems = {"sibling": 1, "allgather": 7}.get(kind, 3) * n
    res = pl.pallas_call(
        body, name=name,
        out_shape=(pltpu.SemaphoreType.DMA((n_sems,)), pltpu.SemaphoreType.DMA((n_sems,)),
                   *[pltpu.HBM(a.shape, a.dtype) for a in srcs], *[pltpu.HBM(s.shape, s.dtype) for s in land_shapes],
                   jax.ShapeDtypeStruct((8, LANES), F32)),
        in_specs=[_HBM] * (2 * n), out_specs=(_SEM, _SEM, *[_HBM] * (2 * n), pl.BlockSpec(memory_space=pltpu.VMEM)),
        input_output_aliases={i: 2 + i for i in range(2 * n)},
        compiler_params=pltpu.CompilerParams(has_side_effects=pltpu.SideEffectType.DATAFLOW_SIDE_EFFECTING),
    )(*[_in_hbm(a) for a in srcs], *[_in_hbm(z) for z in zones])
    return dict(send=res[0], recv=res[1], srcs=list(res[2:2 + n]), lands=list(res[2 + n:2 + 2 * n]), token=res[-1], kind=kind)


def _split_wait(started, after, name):
    n = len(started["srcs"])
    kind = started["kind"]

    def body(*refs):
        ins, lands = refs[:n], refs[n:2 * n]
        send_sems, recv_sems = refs[2 * n], refs[2 * n + 1]
        x, y, c = _mesh_pos()
        for i, (src, dst, _) in enumerate(_split_plan(kind, ins, lands, x, y, c)):
            cp = pltpu.make_async_remote_copy(src_ref=src, dst_ref=dst, send_sem=send_sems.at[i], recv_sem=recv_sems.at[i],
                                              device_id=(x, y, c), device_id_type=MESH)
            cp.wait_send()
            cp.wait_recv()

    arrs = started["srcs"] + started["lands"]
    res = pl.pallas_call(
        body, name=name, out_shape=tuple(pltpu.HBM(a.shape, a.dtype) for a in arrs),
        in_specs=[_HBM] * (2 * n) + [_SEM, _SEM, pl.BlockSpec(memory_space=pl.ANY)], out_specs=tuple([_HBM] * (2 * n)),
        input_output_aliases={i: i for i in range(2 * n)},
        compiler_params=pltpu.CompilerParams(has_side_effects=pltpu.SideEffectType.DATAFLOW_SIDE_EFFECTING),
    )(*arrs, started["send"], started["recv"], after)
    return list(res[:n]), list(res[n:])


def _sibling_share(fs, name):
    n = len(fs)

    def body(*refs):
        ins, outs = refs[:n], refs[n:2 * n]
        send_sems, recv_sems = refs[2 * n:]
        x, y, c = _mesh_pos()
        copies = [_remote(ins[i], outs[i], send_sems, recv_sems, i, (x, y, 1 - c)) for i in range(n)]
        for cp in copies:
            cp.start()
        for cp in copies:
            cp.wait()

    out_shapes = [jax.ShapeDtypeStruct(a.shape, a.dtype) for a in fs]
    return _comm_call(body, list(fs), out_shapes, n, name)


def _allgather8(v, name, after=()):
    m = v.shape[0]

    def body(v_ref, *rest):
        out_ref, send_sems, recv_sems = rest[len(after):]
        x, y, c = _mesh_pos()
        me, sib = (x, y, c), (x, y, 1 - c)
        chips = _peer_chips(x, y)
        rem = functools.partial(_remote, send_sems=send_sems, recv_sems=recv_sems)

        def blk(px, py, pc):
            return out_ref.at[4 * px + 2 * py + pc]

        first = [rem(v_ref, blk(*me), sem=0, to=sib)]
        first += [rem(v_ref, blk(*me), sem=1 + k, to=(px, py, c)) for k, (px, py) in enumerate(chips)]
        for cp in first:
            cp.start()
        passed = []
        for k, (px, py) in enumerate(chips):
            landed = blk(px, py, c)
            rem(landed, landed, sem=1 + k, to=me).wait_recv()
            fwd = rem(landed, landed, sem=4 + k, to=sib)
            fwd.start()
            passed.append(fwd)
        rem(blk(*sib), blk(*sib), sem=0, to=me).wait_recv()
        for k, (px, py) in enumerate(chips):
            theirs = blk(px, py, 1 - c)
            rem(theirs, theirs, sem=4 + k, to=me).wait_recv()
        for cp in first + passed:
            cp.wait_send()

    return _comm_call(body, [v, *after], [jax.ShapeDtypeStruct((8, m, LANES), v.dtype)], 7, name)[0]


_WEIGHTS = ["norm1_g", "w_in", "conv_a_w", "conv_a_b", "ln_a_g", "ln_a_b", "ln_b_g", "ln_b_b", "w_spatial", "b_spatial",
            "conv_c_w", "conv_c_b", "dt_bias", "a_log", "d_skip", "norm_c_g", "w_out", "norm2_g", "w_ff1", "w_ff2", "final_g"]
_BIG = ["w_in", "w_out", "w_ff1", "w_ff2"]
_CONV_SHARDED = ["conv_a_w", "conv_c_w"]
_SMALL = [w for w in _WEIGHTS if w not in _BIG and w != "final_g"]
_PACK_ROWS = 512


def _pack(arrs):
    flat = jnp.concatenate([a.reshape(-1) for a in arrs])
    blk = _PACK_ROWS * LANES
    n = flat.shape[0]
    return jnp.pad(flat, (0, -(-n // blk) * blk - n)).reshape(-1, LANES)


def _unpack(packed, shapes):
    flat = packed.reshape(-1)
    out, off = [], 0
    for s in shapes:
        n = math.prod(s)
        out.append(flat[off:off + n].reshape(s))
        off += n
    return out


def _cols_to_chips(a):
    k = a.shape[0]
    return a.reshape(k, N_CHIPS, -1).transpose(1, 0, 2)


def _chips_to_cols(a):
    return a.transpose(1, 0, 2).reshape(a.shape[1], -1)


def _own_shards(w, li):
    return [w[k][li].astype(BF16) for k in _BIG] + [w[k][li] for k in _CONV_SHARDED]


def _layer_params(w, li, own, gathered, q):
    g_in, g_out, g_ff1, g_ff2, g_ca, g_cc = [lax.dynamic_update_index_in_dim(g, o, q, axis=0)
                                             for g, o in zip(gathered, own)]
    p = {k: w[k][li] for k in _SMALL if k not in _CONV_SHARDED}
    w_in = _chips_to_cols(g_in)
    p["w_main"] = w_in[:, :D_MAIN]
    p["w_dt"] = _group_heads(w_in[:, D_MAIN:])
    p["w_out"] = g_out.reshape(D_MIX, D_MODEL)
    p["w_ff1"] = g_ff1
    p["w_ff2"] = g_ff2.reshape(D_FF, D_MODEL)
    p["conv_a_w"] = _chips_to_cols(g_ca)
    p["conv_c_w"] = _chips_to_cols(g_cc)
    return p


def _ffn_out_grads(g):
    return [g["w_out"].reshape(N_CHIPS, -1, D_MODEL), g["w_ff1"], g["w_ff2"].reshape(N_CHIPS, -1, D_MODEL)]


def _half_shape(a):
    return jax.ShapeDtypeStruct(a.shape[:-2] + (a.shape[-2] // 2, a.shape[-1]), a.dtype)


def _chip_sums(g, early, early_from_sib, li, c, q):
    n = f"l{li}_rs_"
    late = [g["w_main"], g["w_dt"]]
    full = late + list(early)
    from_sib = list(_sibling_other_halves(late, n + "sib")) + list(early_from_sib)
    mine = [lax.dynamic_slice_in_dim(a, c * b.shape[-2], b.shape[-2], axis=a.ndim - 2) for a, b in zip(full, from_sib)]
    sums = [_chipsum_in(mine[0], mine[1], from_sib[0], from_sib[1], n + "chipsum0")]
    for i in range(2, len(full)):
        shape = from_sib[i].shape
        s32, s16 = _ew(lambda u, v: (u + v, u + v), [mine[i].reshape(-1, shape[-1]), from_sib[i].reshape(-1, shape[-1])],
                       (F32, BF16), n + f"chipsum{i - 1}")
        sums.append((s32.reshape(shape), s16.reshape(shape)))
    chip_f32 = [lax.dynamic_index_in_dim(s32, q, axis=0, keepdims=False) for s32, _ in sums]
    chip_bf16 = [s16 for _, s16 in sums]
    return chip_f32, chip_bf16


def _finish_reduce(chip_f32, from_chips, li, c):
    n = f"l{li}_rs_"
    halves = [_ew(lambda o, r0, r1, r2_: (((o + r0) + r1) + r2_,), [own, rb, rb, rb], (F32,), n + f"final{i}",
                  leads=[None, 0, 1, 2])[0] for i, (own, rb) in enumerate(zip(chip_f32, from_chips))]
    from_sib = _sibling_share(halves, n + "share")
    return [jnp.where(c == 0, jnp.concatenate([h, s], axis=0), jnp.concatenate([s, h], axis=0))
            for h, s in zip(halves, from_sib)]


def kernel(x, norm1_g, w_in, conv_a_w, conv_a_b, ln_a_g, ln_a_b, ln_b_g, ln_b_b, w_spatial, b_spatial, conv_c_w, conv_c_b, dt_bias, a_log, d_skip, norm_c_g, w_out, norm2_g, w_ff1, w_ff2, final_g, loss_target, m_norm1_g, m_w_in, m_conv_a_w, m_conv_a_b, m_ln_a_g, m_ln_a_b, m_ln_b_g, m_ln_b_b, m_w_spatial, m_b_spatial, m_conv_c_w, m_conv_c_b, m_dt_bias, m_a_log, m_d_skip, m_norm_c_g, m_w_out, m_norm2_g, m_w_ff1, m_w_ff2, m_final_g, v_norm1_g, v_w_in, v_conv_a_w, v_conv_a_b, v_ln_a_g, v_ln_a_b, v_ln_b_g, v_ln_b_b, v_w_spatial, v_b_spatial, v_conv_c_w, v_conv_c_b, v_dt_bias, v_a_log, v_d_skip, v_norm_c_g, v_w_out, v_norm2_g, v_w_ff1, v_w_ff2, v_final_g):
    given = dict(locals())
    w = {k: given[k] for k in _WEIGHTS}
    m = {k: given["m_" + k] for k in _WEIGHTS}
    v = {k: given["v_" + k] for k in _WEIGHTS}
    depth = w_in.shape[0]
    nseq, seq, d = x.shape
    xi, yi, ci = _mesh_pos()
    q = 2 * xi + yi

    own = [_own_shards(w, li) for li in range(depth)]
    nb = len(_BIG)
    gathered = _gather_weights(own[0][:nb], own[0][nb:], "l0_gather")
    h = x.reshape(nseq * seq, d)
    layer_params, saved = [], []
    for li in range(depth):
        nxt = None
        if li + 1 < depth:
            srcs, _ = lax.optimization_barrier((own[li + 1], gathered))
            zones = [jax.ShapeDtypeStruct((N_CHIPS,) + a.shape, a.dtype) for a in srcs]
            nxt = _split_start("gather", srcs, zones, f"l{li + 1}_gather_start")
        layer_params.append(_layer_params(w, li, own[li], gathered, q))
        h, s = _layer_fwd(h, layer_params[li], seq, li, after=() if nxt is None else (nxt["token"],))
        saved.append(s)
        if nxt is not None:
            own[li + 1], gathered = _split_wait(nxt, h, f"l{li + 1}_gather_wait")
    loss, dx, d_final = _loss_head(h, final_g, loss_target.reshape(nseq * seq, d))

    grads = [None] * depth
    big_grads = [None] * depth
    pending = None
    for li in reversed(range(depth)):
        swaps = []

        def early_swap(g, li=li, swaps=swaps):
            early = _ffn_out_grads(g)
            swaps.append(_split_start("sibling", early, [_half_shape(a) for a in early], f"l{li}_rs_sib_start"))
            return (swaps[0]["token"],)

        after = () if pending is None else (pending[1]["token"],)
        if li == 0 and depth > 1:
            early_pack = _pack([grads[lj][k] for lj in range(1, depth) for k in _SMALL])
            early_small = _split_start("allgather", [early_pack], [jax.ShapeDtypeStruct((8,) + early_pack.shape, F32)],
                                       "small_early_start")
            after = after + (early_small["token"],)
        dx, grads[li] = _layer_bwd(dx, layer_params[li], saved[li], seq, li, after=after, on_ffn_grads=early_swap)
        if pending is not None:
            lj, scatter, chip_f32 = pending
            big_grads[lj] = _finish_reduce(chip_f32, _split_wait(scatter, dx, f"l{lj}_rs_scatter_wait")[1], lj, ci)
        early, early_from_sib = _split_wait(swaps[0], dx, f"l{li}_rs_sib_wait")
        chip_f32, chip_bf16 = _chip_sums(grads[li], early, early_from_sib, li, ci, q)
        lands = [jax.ShapeDtypeStruct((3,) + a.shape[1:], a.dtype) for a in chip_bf16]
        pending = (li, _split_start("scatter", chip_bf16, lands, f"l{li}_rs_scatter_start"), chip_f32)
    grad_out, delta_out, m_out, v_out = {}, {}, {}, {}

    lj, scatter, chip_f32 = pending
    done = {k: None for k in _BIG}
    behind = (scatter["token"],)
    for li in range(depth - 1, 0, -1):
        for i, k in enumerate(_BIG):
            done[k] = _adam_layer(li, w[k], big_grads[li][i], m[k], v[k], done[k], f"adam_{k}_l{li}", after=behind)
    if depth > 1:
        behind = (done[_BIG[-1]][1],)

    small_shapes = [grads[0][k].shape for k in _SMALL]
    me = 2 * q + ci

    def sum8(*blocks):
        acc = blocks[0]
        for b in blocks[1:]:
            acc = acc + b
        return (acc,)

    def total_of(gathered, own, name):
        full = lax.dynamic_update_index_in_dim(gathered, own, me, axis=0)
        return _ew(sum8, [full] * 8, (F32,), name, leads=list(range(8)))[0]

    last_pack = _pack([grads[0][k] for k in _SMALL] + [d_final, loss.reshape(1)])
    last_total = total_of(_allgather8(last_pack, "small_allgather", after=behind), last_pack, "small_sum")
    summed = _unpack(last_total, small_shapes + [d_final.shape, (1,)])
    tail = summed[len(_SMALL):]
    summed = summed[:len(_SMALL)]
    if depth > 1:
        (early_own,), (early_all,) = _split_wait(early_small, dx, "small_early_wait")
        summed += _unpack(total_of(early_all, early_own, "small_early_sum"), small_shapes * (depth - 1))
    summed += tail
    loss_total = summed[-1][0]
    small_grads = {k: jnp.stack([summed[li * len(_SMALL) + i] for li in range(depth)]) for i, k in enumerate(_SMALL)}
    small_grads["final_g"] = summed[-2]
    for k in _CONV_SHARDED:
        n_shard = w[k].shape[-1]
        small_grads[k] = lax.dynamic_slice_in_dim(small_grads[k], q * n_shard, n_shard, axis=2)
    names = _SMALL + ["final_g"]
    shapes = [w[k].shape for k in names]
    packed = [_pack([src[k] for k in names]) for src in (w, small_grads, m, v)]
    outs = _ew(_adam_fn, packed, (F32, F32, F32), "adam_small")
    for dst, o in zip((delta_out, m_out, v_out), outs):
        for k, a in zip(names, _unpack(o, shapes)):
            dst[k] = a
    for k in names:
        grad_out[k] = small_grads[k]

    big_grads[lj] = _finish_reduce(chip_f32, _split_wait(scatter, outs[0], f"l{lj}_rs_scatter_wait")[1], lj, ci)
    for i, k in enumerate(_BIG):
        grad_out[k], delta_out[k], m_out[k], v_out[k] = _adam_layer(0, w[k], big_grads[0][i], m[k], v[k], done[k],
                                                                    f"adam_{k}_l0")

    return (loss_total, dx.reshape(nseq, seq, d), *[grad_out[k] for k in _WEIGHTS], *[delta_out[k] for k in _WEIGHTS],
            *[m_out[k] for k in _WEIGHTS], *[v_out[k] for k in _WEIGHTS])
```

```python
import functools
import math

import jax
import jax.numpy as jnp
from jax import lax
from jax.experimental import pallas as pl
from jax.experimental.pallas import tpu as pltpu

F32 = jnp.float32
BF16 = jnp.bfloat16
MESH = pl.DeviceIdType.MESH

D_MODEL = 1024
DEPTH = 4
HEAD_DIM = 64
A_WIDTH = 512
B_WIDTH = 512
C_WIDTH = 1024
C_HEADS = 16
CONV_A_K = 31
CONV_C_K = 4
CHUNK = 128
SSM_STATE = 128
D_CONV_C = 1536
D_MAIN = 4608
D_IN_PROJ = 4624
D_MIX = 2048
D_FF = 4096
EPS = 1e-5
NEG = -1e30
LANES = 128
CONV_PAD = 32
N_CHIPS = 4

ADAM_LR = 0.001
ADAM_B1 = 0.9
ADAM_B2 = 0.999
ADAM_EPS = 1e-08
ADAM_WD = 0.01
ADAM_STEP = 10

VMEM_LIMIT = 56 * 1024 * 1024
MATMUL_VMEM_BUDGET = 44 * 1024 * 1024

COL_AVAL, COL_AGATE, COL_BU, COL_BV, COL_Z, COL_XBC = 0, 4, 8, 12, 16, 24


def _cparams(*sem):
    return pltpu.CompilerParams(dimension_semantics=sem, vmem_limit_bytes=VMEM_LIMIT)


_DN = {"nn": (((1,), (0,)), ((), ())), "nt": (((1,), (1,)), ((), ())), "tn": (((0,), (0,)), ((), ()))}


def _dot_raw(a, b, mode):
    return lax.dot_general(a.astype(BF16), b.astype(BF16), _DN[mode], preferred_element_type=F32)


def _make_dot(mode):
    @jax.custom_vjp
    def f(a, b):
        return _dot_raw(a, b, mode)

    def fwd(a, b):
        return _dot_raw(a, b, mode), (a, b)

    def bwd(res, g):
        a, b = res
        if mode == "nn":
            return _dot_raw(g, b, "nt"), _dot_raw(a, g, "tn")
        if mode == "nt":
            return _dot_raw(g, b, "nn"), _dot_raw(g, a, "tn")
        return _dot_raw(b, g, "nt"), _dot_raw(a, g, "nn")

    f.defvjp(fwd, bwd)
    return f


_nn = _make_dot("nn")
_nt = _make_dot("nt")
_tn = _make_dot("tn")


def _xdot(a, e):
    return jnp.dot(a, e, precision=lax.Precision.HIGHEST, preferred_element_type=F32)


def _iota2(shape, dim):
    return lax.broadcasted_iota(jnp.int32, shape, dim)


def _gmean_impl(x):
    n = x.shape[-1]
    same = (_iota2((n, n), 0) < HEAD_DIM) == (_iota2((n, n), 1) < HEAD_DIM)
    p = jnp.where(same, 1.0 / HEAD_DIM, 0.0).astype(BF16)
    hi = x.astype(BF16)
    lo = (x - hi.astype(F32)).astype(BF16)
    dn = _DN["nn"]
    return (lax.dot_general(hi, p, dn, preferred_element_type=F32)
            + lax.dot_general(lo, p, dn, preferred_element_type=F32))


@jax.custom_vjp
def _gmean(x):
    return _gmean_impl(x)


_gmean.defvjp(lambda x: (_gmean_impl(x), None), lambda _, g: (_gmean_impl(g),))


def _sigmoid(x):
    return 1.0 / (1.0 + jnp.exp(-x))


def _silu(x):
    return x * _sigmoid(x)


def _gelu(x):
    return 0.5 * x * (1.0 + lax.erf(x * 0.7071067811865476))


def _softplus(x):
    return jnp.maximum(x, 0.0) + jnp.log(1.0 + jnp.exp(-jnp.abs(x)))


def _rms(x, g):
    return x * lax.rsqrt(jnp.mean(x * x, axis=-1, keepdims=True) + EPS) * g


def _ln64(x, g, b):
    mu = _gmean(x)
    xc = x - mu
    var = _gmean(xc * xc)
    return xc * lax.rsqrt(var + EPS) * g + b


def _lane_lt64(shape):
    return _iota2(shape, 1) < HEAD_DIM


def _pick(n, pref):
    for t in pref:
        if n % t == 0:
            return t
    return n


_UNREAD = pl.BlockSpec(memory_space=pl.ANY)


def _matmul_tiles(m, n_unit, k, a_item, b_item, out_bytes):
    best = None
    for tm in (1024, 512, 256, 128):
        for tn in (1536, 1024, 768, 512, 256, 128):
            if m % tm or n_unit % tn:
                continue
            need = 2 * k * (tm * a_item + tn * b_item) + 2 * tm * tn * out_bytes
            if need <= MATMUL_VMEM_BUDGET and (best is None or tm * tn > best[0] * best[1]):
                best = (tm, tn)
    assert best is not None, (m, n_unit, k)
    return best


def _matmul(a, b, *, mode, name, add=None, epilogue=None, extra=None, out_dtypes=(F32,), after=(), b_chips=False,
            out_chips=False):
    sh = b.shape[-1] if b_chips else None
    if mode == "nn":
        (m, k), n = a.shape, (N_CHIPS * sh if b_chips else b.shape[1])
    elif mode == "nt":
        (m, k), n = a.shape, b.shape[-2]
    else:
        (k, m), n = a.shape, b.shape[1]
    osh = n // N_CHIPS if out_chips else None
    out_bytes = sum(jnp.dtype(dt).itemsize for dt in out_dtypes) + (0 if add is None else add.dtype.itemsize) \
        + (0 if extra is None else extra.dtype.itemsize)
    tm, tn = _matmul_tiles(m, sh if (b_chips and mode == "nn") else (osh or n), k, a.dtype.itemsize, b.dtype.itemsize,
                           out_bytes)
    a_spec = pl.BlockSpec((k, tm), lambda i, j: (0, i)) if mode == "tn" else pl.BlockSpec((tm, k), lambda i, j: (i, 0))
    if b_chips and mode == "nn":
        per = sh // tn
        b_spec = pl.BlockSpec((None, k, tn), lambda i, j: (j // per, 0, j % per))
    elif b_chips:
        b_spec = pl.BlockSpec((N_CHIPS, tn, sh), lambda i, j: (0, j, 0))
    elif mode == "nt":
        b_spec = pl.BlockSpec((tn, k), lambda i, j: (j, 0))
    else:
        b_spec = pl.BlockSpec((k, tn), lambda i, j: (0, j))
    if out_chips:
        o_per = osh // tn
        o_spec = pl.BlockSpec((None, tm, tn), lambda i, j: (j // o_per, i, j % o_per))
        out_shape = [jax.ShapeDtypeStruct((N_CHIPS, m, osh), dt) for dt in out_dtypes]
    else:
        o_spec = pl.BlockSpec((tm, tn), lambda i, j: (i, j))
        out_shape = [jax.ShapeDtypeStruct((m, n), dt) for dt in out_dtypes]
    ins = [a, b]
    in_specs = [a_spec, b_spec]
    if add is not None:
        ins.append(add)
        in_specs.append(o_spec)
    if extra is not None:
        ins.append(extra)
        in_specs.append(o_spec)
    ins += list(after)
    in_specs += [_UNREAD] * len(after)
    n_out = len(out_dtypes)

    def body(*refs):
        a_ref, b_ref = refs[0], refs[1]
        pos = 2
        add_ref = ex_ref = None
        if add is not None:
            add_ref = refs[pos]
            pos += 1
        if extra is not None:
            ex_ref = refs[pos]
            pos += 1
        pos += len(after)
        if b_chips and mode == "nt":
            acc = _dot_raw(a_ref[:, pl.ds(0, sh)], b_ref[0], mode)
            for chip in range(1, N_CHIPS):
                acc = acc + _dot_raw(a_ref[:, pl.ds(chip * sh, sh)], b_ref[chip], mode)
        else:
            acc = _dot_raw(a_ref[...], b_ref[...], mode)
        if add_ref is not None:
            acc = acc + add_ref[...].astype(F32)
        outs = (acc,) if epilogue is None else epilogue(acc, None if ex_ref is None else ex_ref[...])
        for o_ref, o in zip(refs[pos:pos + n_out], outs):
            o_ref[...] = o.astype(o_ref.dtype)

    res = pl.pallas_call(
        body, name=name, grid=(m // tm, n // tn), in_specs=in_specs, out_specs=[o_spec] * n_out, out_shape=out_shape,
        compiler_params=_cparams("parallel", "parallel"),
    )(*ins)
    return res[0] if n_out == 1 else res


def _matmul_ksplit(a, b, *, mode, name, add=None, epilogue=None, extra=None, out_dtypes=(F32,), after=(), b_chips=False,
                   out_chips=False):
    sh = b.shape[-1] if b_chips else None
    if mode == "nn":
        (m, k), n = a.shape, (N_CHIPS * sh if b_chips else b.shape[1])
    elif mode == "nt":
        (m, k), n = a.shape, b.shape[-2]
    else:
        (k, m), n = a.shape, b.shape[1]
    osh = n // N_CHIPS if out_chips else None
    tm = _pick(m, (1024, 512, 256, 128))
    tn = _pick(sh if (b_chips and mode == "nn") else (osh or n), (1536, 1024, 512, 256, 128))
    out_bytes = sum(jnp.dtype(dt).itemsize for dt in out_dtypes) + (0 if add is None else add.dtype.itemsize) \
        + (0 if extra is None else extra.dtype.itemsize)
    k_dim = sh if (b_chips and mode == "nt") else k
    for tk in (4096, 2048, 1536, 1024, 512, 256, 128):
        if k_dim % tk:
            continue
        acc_bytes = 0 if tk == k else 4 * tm * tn
        need = 2 * tk * (tm * a.dtype.itemsize + tn * b.dtype.itemsize) + 2 * tm * tn * out_bytes + acc_bytes
        if need <= MATMUL_VMEM_BUDGET:
            break
    nk = k // tk
    a_spec = {"nn": pl.BlockSpec((tm, tk), lambda i, j, kk: (i, kk)),
              "nt": pl.BlockSpec((tm, tk), lambda i, j, kk: (i, kk)),
              "tn": pl.BlockSpec((tk, tm), lambda i, j, kk: (kk, i))}[mode]
    if b_chips:
        per = sh // (tn if mode == "nn" else tk)
        b_spec = {"nn": pl.BlockSpec((None, tk, tn), lambda i, j, kk: (j // per, kk, j % per)),
                  "nt": pl.BlockSpec((None, tn, tk), lambda i, j, kk: (kk // per, j, kk % per))}[mode]
    else:
        b_spec = {"nn": pl.BlockSpec((tk, tn), lambda i, j, kk: (kk, j)),
                  "nt": pl.BlockSpec((tn, tk), lambda i, j, kk: (j, kk)),
                  "tn": pl.BlockSpec((tk, tn), lambda i, j, kk: (kk, j))}[mode]
    if out_chips:
        o_per = osh // tn
        o_spec = pl.BlockSpec((None, tm, tn), lambda i, j, kk: (j // o_per, i, j % o_per))
        out_shape = [jax.ShapeDtypeStruct((N_CHIPS, m, osh), dt) for dt in out_dtypes]
    else:
        o_spec = pl.BlockSpec((tm, tn), lambda i, j, kk: (i, j))
        out_shape = [jax.ShapeDtypeStruct((m, n), dt) for dt in out_dtypes]
    ins = [a, b]
    in_specs = [a_spec, b_spec]
    if add is not None:
        ins.append(add)
        in_specs.append(o_spec)
    if extra is not None:
        ins.append(extra)
        in_specs.append(o_spec)
    ins += list(after)
    in_specs += [_UNREAD] * len(after)
    n_out = len(out_dtypes)

    def body(*refs):
        a_ref, b_ref = refs[0], refs[1]
        pos = 2
        add_ref = ex_ref = None
        if add is not None:
            add_ref = refs[pos]
            pos += 1
        if extra is not None:
            ex_ref = refs[pos]
            pos += 1
        pos += len(after)
        o_refs = refs[pos:pos + n_out]

        def finish(acc):
            if add_ref is not None:
                acc = acc + add_ref[...].astype(F32)
            outs = (acc,) if epilogue is None else epilogue(acc, None if ex_ref is None else ex_ref[...])
            for o_ref, o in zip(o_refs, outs):
                o_ref[...] = o.astype(o_ref.dtype)

        part = _dot_raw(a_ref[...], b_ref[...], mode)
        if nk == 1:
            finish(part)
            return
        acc_ref = refs[pos + n_out]
        kk = pl.program_id(2)

        @pl.when(kk == 0)
        def _():
            acc_ref[...] = part

        @pl.when(jnp.logical_and(kk > 0, kk < nk - 1))
        def _():
            acc_ref[...] += part

        @pl.when(kk == nk - 1)
        def _():
            finish(acc_ref[...] + part)

    res = pl.pallas_call(
        body, name=name, grid=(m // tm, n // tn, nk),
        in_specs=in_specs, out_specs=[o_spec] * n_out, out_shape=out_shape,
        scratch_shapes=[pltpu.VMEM((tm, tn), F32)] if nk > 1 else [],
        compiler_params=_cparams("parallel", "parallel", "arbitrary"),
    )(*ins)
    return res[0] if n_out == 1 else res


def _relu2_epilogue(acc, _):
    r = jnp.maximum(acc, 0.0)
    return acc, r * r


def _relu2_bwd_epilogue(acc, u):
    return (acc * (2.0 * jnp.maximum(u, 0.0)),)


def _row_tile(t):
    return _pick(t, (512, 256, 128))


def _rms_fwd(x, g, name, after=()):
    t, d = x.shape
    tm = _row_tile(t)

    def body(x_ref, g_ref, *rest):
        o_ref = rest[-1]
        o_ref[...] = _rms(x_ref[...], g_ref[...]).astype(BF16)

    return pl.pallas_call(
        body, name=name, grid=(t // tm,),
        in_specs=[pl.BlockSpec((tm, d), lambda i: (i, 0)), pl.BlockSpec((1, d), lambda i: (0, 0))] + [_UNREAD] * len(after),
        out_specs=pl.BlockSpec((tm, d), lambda i: (i, 0)),
        out_shape=jax.ShapeDtypeStruct((t, d), BF16),
        compiler_params=_cparams("parallel"),
    )(x, g.reshape(1, d), *after)


def _rms_bwd(x, g, dh, dres, name):
    t, d = x.shape
    tm = _row_tile(t)

    def body(x_ref, g_ref, dh_ref, dres_ref, dx_ref, dg_ref):
        @pl.when(pl.program_id(0) == 0)
        def _():
            dg_ref[...] = jnp.zeros_like(dg_ref)

        _, vjp = jax.vjp(_rms, x_ref[...], g_ref[...])
        dx, dg = vjp(dh_ref[...].astype(F32))
        dx_ref[...] = dx + dres_ref[...]
        dg_ref[...] += dg

    row = pl.BlockSpec((tm, d), lambda i: (i, 0))
    vec = pl.BlockSpec((1, d), lambda i: (0, 0))
    dx, dg = pl.pallas_call(
        body, name=name, grid=(t // tm,),
        in_specs=[row, vec, row, row], out_specs=[row, vec],
        out_shape=[jax.ShapeDtypeStruct((t, d), F32), jax.ShapeDtypeStruct((1, d), F32)],
        compiler_params=_cparams("arbitrary"),
    )(x, g.reshape(1, d), dh, dres)
    return dx, dg.reshape(d)


def _loss_head(x, g, target):
    t, d = x.shape
    tm = _row_tile(t)

    def loss_fn(xv, gv, tv):
        err = _rms(xv, gv) - tv
        return 0.5 * jnp.sum(jnp.mean(err * err, axis=-1, keepdims=True))

    def body(x_ref, g_ref, t_ref, loss_ref, dx_ref, dg_ref):
        @pl.when(pl.program_id(0) == 0)
        def _():
            dg_ref[...] = jnp.zeros_like(dg_ref)
            loss_ref[...] = jnp.zeros_like(loss_ref)

        tv = t_ref[...]
        val, vjp = jax.vjp(lambda xv, gv: loss_fn(xv, gv, tv), x_ref[...], g_ref[...])
        dx, dg = vjp(jnp.ones((), F32))
        dx_ref[...] = dx
        dg_ref[...] += dg
        loss_ref[...] += jnp.full(loss_ref.shape, val, F32)

    row = pl.BlockSpec((tm, d), lambda i: (i, 0))
    vec = pl.BlockSpec((1, d), lambda i: (0, 0))
    loss, dx, dg = pl.pallas_call(
        body, name="loss_head", grid=(t // tm,),
        in_specs=[row, vec, row], out_specs=[pl.BlockSpec((1, LANES), lambda i: (0, 0)), row, vec],
        out_shape=[jax.ShapeDtypeStruct((1, LANES), F32), jax.ShapeDtypeStruct((t, d), F32),
                   jax.ShapeDtypeStruct((1, d), F32)],
        compiler_params=_cparams("arbitrary"),
    )(x, g.reshape(1, d), target)
    return loss[0, 0], dx, dg.reshape(d)


def _pre_glu(val, gate):
    return val * _sigmoid(gate)


def _pre_id(x):
    return x


def _post_lnsilu(c, g, b):
    return _silu(_ln64(c, g, b))


def _post_silu(c):
    return _silu(c)


def _conv_cfg(kind):
    if kind == "a":
        return dict(k=CONV_A_K, pre=_pre_glu, post=_post_lnsilu, n_in=2, n_par=2, nblk=A_WIDTH // LANES,
                    cols=(COL_AVAL, COL_AGATE))
    return dict(k=CONV_C_K, pre=_pre_id, post=_post_silu, n_in=1, n_par=0, nblk=D_CONV_C // LANES,
                cols=(COL_XBC,))


def _conv_fwd(kind, proj, w, bias, params, seq, name, out_dtype=F32, keep_conv=False):
    cfg = _conv_cfg(kind)
    kt, pre, post, n_in = cfg["k"], cfg["pre"], cfg["post"], cfg["n_in"]
    t = proj.shape[0]
    nseq = t // seq
    c = cfg["nblk"] * LANES
    rt = min(256, seq)
    nrt = seq // rt
    off0 = CONV_PAD - (kt - 1)

    def body(*refs):
        in_refs = refs[:n_in]
        w_ref, b_ref = refs[n_in], refs[n_in + 1]
        par_refs = refs[n_in + 2:n_in + 2 + cfg["n_par"]]
        out_refs = refs[n_in + 2 + cfg["n_par"]:-1]
        hpad = refs[-1]
        hpad[pl.ds(0, CONV_PAD), :] = jnp.zeros((CONV_PAD, LANES), F32)
        for r in range(nrt):
            hpad[pl.ds(CONV_PAD + r * rt, rt), :] = pre(*[x[pl.ds(r * rt, rt), :] for x in in_refs])
        pars = [p[...] for p in par_refs]
        for r in range(nrt):
            acc = jnp.broadcast_to(b_ref[...], (rt, LANES))
            for k in range(kt):
                acc = acc + w_ref[pl.ds(k, 1), :] * hpad[pl.ds(off0 + k + r * rt, rt), :]
            out_refs[0][pl.ds(r * rt, rt), :] = post(acc, *pars).astype(out_dtype)
            if keep_conv:
                out_refs[1][pl.ds(r * rt, rt), :] = acc

    in_specs = [pl.BlockSpec((seq, LANES), functools.partial(lambda s, j, col: (s, col + j), col=col))
                for col in cfg["cols"]]
    vec = pl.BlockSpec((1, LANES), lambda s, j: (0, j))
    in_specs += [pl.BlockSpec((CONV_PAD, LANES), lambda s, j: (0, j)), vec] + [vec] * cfg["n_par"]
    blk = pl.BlockSpec((seq, LANES), lambda s, j: (s, j))
    res = pl.pallas_call(
        body, name=name, grid=(nseq, cfg["nblk"]),
        in_specs=in_specs, out_specs=[blk, blk] if keep_conv else [blk],
        out_shape=[jax.ShapeDtypeStruct((t, c), out_dtype)] + ([jax.ShapeDtypeStruct((t, c), F32)] if keep_conv else []),
        scratch_shapes=[pltpu.VMEM((seq + CONV_PAD, LANES), F32)],
        compiler_params=_cparams("parallel", "parallel"),
    )(*([proj] * n_in), w, bias, *params)
    return tuple(res) if keep_conv else res[0]


def _conv_bwd(kind, proj, w, bias, params, dy, seq, name, dy_col=0, conv_out=None):
    kept = conv_out is not None
    cfg = _conv_cfg(kind)
    kt, pre, post, n_in, n_par = cfg["k"], cfg["pre"], cfg["post"], cfg["n_in"], cfg["n_par"]
    t = proj.shape[0]
    nseq = t // seq
    c = cfg["nblk"] * LANES
    rt = min(256, seq)
    nrt = seq // rt
    off0 = CONV_PAD - (kt - 1)

    def body(*refs):
        in_refs = refs[:n_in]
        w_ref, b_ref = refs[n_in], refs[n_in + 1]
        par_refs = refs[n_in + 2:n_in + 2 + n_par]
        pos = n_in + 2 + n_par
        dy_ref = refs[pos]
        if kept:
            pos += 1
            conv_ref = refs[pos]
        din_refs = refs[pos + 1:pos + 1 + n_in]
        dw_ref, db_ref = refs[pos + 1 + n_in], refs[pos + 2 + n_in]
        dpar_refs = refs[pos + 3 + n_in:pos + 3 + n_in + n_par]
        hpad, dcpad = refs[pos + 3 + n_in + n_par:]

        @pl.when(pl.program_id(1) == 0)
        def _():
            dw_ref[...] = jnp.zeros_like(dw_ref)
            db_ref[...] = jnp.zeros_like(db_ref)
            for r in dpar_refs:
                r[...] = jnp.zeros_like(r)

        hpad[pl.ds(0, CONV_PAD), :] = jnp.zeros((CONV_PAD, LANES), F32)
        dcpad[pl.ds(seq, CONV_PAD), :] = jnp.zeros((CONV_PAD, LANES), F32)
        for r in range(nrt):
            hpad[pl.ds(CONV_PAD + r * rt, rt), :] = pre(*[x[pl.ds(r * rt, rt), :] for x in in_refs])
        pars = [p[...] for p in par_refs]
        for r in range(nrt):
            if kept:
                acc = conv_ref[pl.ds(r * rt, rt), :]
            else:
                acc = jnp.broadcast_to(b_ref[...], (rt, LANES))
                for k in range(kt):
                    acc = acc + w_ref[pl.ds(k, 1), :] * hpad[pl.ds(off0 + k + r * rt, rt), :]
            _, vjp = jax.vjp(post, acc, *pars)
            grads = vjp(dy_ref[pl.ds(r * rt, rt), :])
            dcpad[pl.ds(r * rt, rt), :] = grads[0]
            db_ref[...] += jnp.sum(grads[0], axis=0, keepdims=True)
            for ref, gpar in zip(dpar_refs, grads[1:]):
                ref[...] += gpar
        for r in range(nrt):
            dh = jnp.zeros((rt, LANES), F32)
            for k in range(kt):
                dh = dh + w_ref[pl.ds(k, 1), :] * dcpad[pl.ds(r * rt + kt - 1 - k, rt), :]
            _, vjp = jax.vjp(pre, *[x[pl.ds(r * rt, rt), :] for x in in_refs])
            for ref, gin in zip(din_refs, vjp(dh)):
                ref[pl.ds(r * rt, rt), :] = gin.astype(ref.dtype)
        for k in range(kt):
            s = jnp.zeros((1, LANES), F32)
            for r in range(nrt):
                s = s + jnp.sum(dcpad[pl.ds(r * rt, rt), :] * hpad[pl.ds(off0 + k + r * rt, rt), :],
                                axis=0, keepdims=True)
            dw_ref[pl.ds(k, 1), :] += s

    in_specs = [pl.BlockSpec((seq, LANES), functools.partial(lambda j, s, col: (s, col + j), col=col))
                for col in cfg["cols"]]
    vec = pl.BlockSpec((1, LANES), lambda j, s: (0, j))
    wspec = pl.BlockSpec((CONV_PAD, LANES), lambda j, s: (0, j))
    blk = pl.BlockSpec((seq, LANES), lambda j, s: (s, j))
    in_specs += [wspec, vec] + [vec] * n_par + [pl.BlockSpec((seq, LANES), lambda j, s: (s, dy_col + j))]
    in_specs += [blk] if kept else []
    out_specs = [blk] * n_in + [wspec, vec] + [vec] * n_par
    out_shape = ([jax.ShapeDtypeStruct((t, c), BF16)] * n_in
                 + [jax.ShapeDtypeStruct((CONV_PAD, c), F32), jax.ShapeDtypeStruct((1, c), F32)]
                 + [jax.ShapeDtypeStruct((1, c), F32)] * n_par)
    res = pl.pallas_call(
        body, name=name, grid=(cfg["nblk"], nseq),
        in_specs=in_specs, out_specs=out_specs, out_shape=out_shape,
        scratch_shapes=[pltpu.VMEM((seq + CONV_PAD, LANES), F32), pltpu.VMEM((seq + CONV_PAD, LANES), F32)],
        compiler_params=_cparams("parallel", "arbitrary"),
    )(*([proj] * n_in), w, bias, *params, dy, *([conv_out] if kept else []))
    return res[:n_in], res[n_in], res[n_in + 1], res[n_in + 2:]


def _gmlp_chunk(bu, bv, g, b, w0, w1, b0row, b1row):
    u = _gelu(bu)
    vn = _ln64(_gelu(bv), g, b)
    tri = _iota2((CHUNK, CHUNK), 0) >= _iota2((CHUNK, CHUNK), 1)
    m0 = _nn(jnp.where(tri, w0, 0.0), vn) + jnp.broadcast_to(b0row, (CHUNK, CHUNK)).T
    m1 = _nn(jnp.where(tri, w1, 0.0), vn) + jnp.broadcast_to(b1row, (CHUNK, CHUNK)).T
    return u * jnp.where(_lane_lt64((CHUNK, LANES)), m0, m1)


def _gmlp_specs(tm, order):
    def im(f):
        return lambda *ids: f(*order(*ids))
    return dict(
        bu=pl.BlockSpec((tm, LANES), im(lambda j, r: (r, COL_BU + j))),
        bv=pl.BlockSpec((tm, LANES), im(lambda j, r: (r, COL_BV + j))),
        vec=pl.BlockSpec((1, LANES), im(lambda j, r: (0, j))),
        ws=pl.BlockSpec((2, CHUNK, CHUNK), im(lambda j, r: (j, 0, 0))),
        bs=pl.BlockSpec((None, 2, CHUNK), im(lambda j, r: (j, 0, 0))),
        blk=pl.BlockSpec((tm, LANES), im(lambda j, r: (r, j))),
    )


def _gmlp_fwd(proj, ln_g, ln_b, w_s, b_s, name):
    t = proj.shape[0]
    tm = _row_tile(t)
    nch = tm // CHUNK
    sp = _gmlp_specs(tm, lambda r, j: (j, r))

    def body(bu_ref, bv_ref, g_ref, b_ref, ws_ref, bs_ref, o_ref):
        for ci in range(nch):
            rows = pl.ds(ci * CHUNK, CHUNK)
            o_ref[rows, :] = _gmlp_chunk(bu_ref[rows, :], bv_ref[rows, :], g_ref[...], b_ref[...], ws_ref[0], ws_ref[1],
                                         bs_ref[pl.ds(0, 1), :], bs_ref[pl.ds(1, 1), :]).astype(BF16)

    return pl.pallas_call(
        body, name=name, grid=(t // tm, B_WIDTH // LANES),
        in_specs=[sp["bu"], sp["bv"], sp["vec"], sp["vec"], sp["ws"], sp["bs"]],
        out_specs=sp["blk"], out_shape=jax.ShapeDtypeStruct((t, B_WIDTH), BF16),
        compiler_params=_cparams("parallel", "parallel"),
    )(proj, proj, ln_g, ln_b, w_s, b_s.reshape(B_WIDTH // LANES, 2, CHUNK))


def _gmlp_bwd(proj, ln_g, ln_b, w_s, b_s, dy, name, dy_col=0):
    t = proj.shape[0]
    tm = _row_tile(t)
    nch = tm // CHUNK
    sp = _gmlp_specs(tm, lambda j, r: (j, r))
    dy_spec = pl.BlockSpec((tm, LANES), lambda j, r: (r, dy_col + j))

    def body(bu_ref, bv_ref, g_ref, b_ref, ws_ref, bs_ref, dy_ref, dbu_ref, dbv_ref, dg_ref, db_ref, dws_ref, dbs_ref):
        @pl.when(pl.program_id(1) == 0)
        def _():
            for r in (dg_ref, db_ref, dws_ref, dbs_ref):
                r[...] = jnp.zeros_like(r)

        for ci in range(nch):
            rows = pl.ds(ci * CHUNK, CHUNK)
            _, vjp = jax.vjp(_gmlp_chunk, bu_ref[rows, :], bv_ref[rows, :], g_ref[...], b_ref[...],
                             ws_ref[0], ws_ref[1], bs_ref[pl.ds(0, 1), :], bs_ref[pl.ds(1, 1), :])
            dbu, dbv, dg, db, dw0, dw1, db0, db1 = vjp(dy_ref[rows, :])
            dbu_ref[rows, :] = dbu.astype(BF16)
            dbv_ref[rows, :] = dbv.astype(BF16)
            dg_ref[...] += dg
            db_ref[...] += db
            dws_ref[0] += dw0
            dws_ref[1] += dw1
            dbs_ref[pl.ds(0, 1), :] += db0
            dbs_ref[pl.ds(1, 1), :] += db1

    nh = B_WIDTH // LANES
    res = pl.pallas_call(
        body, name=name, grid=(nh, t // tm),
        in_specs=[sp["bu"], sp["bv"], sp["vec"], sp["vec"], sp["ws"], sp["bs"], dy_spec],
        out_specs=[sp["blk"], sp["blk"], sp["vec"], sp["vec"], sp["ws"], sp["bs"]],
        out_shape=[jax.ShapeDtypeStruct((t, B_WIDTH), BF16), jax.ShapeDtypeStruct((t, B_WIDTH), BF16),
                   jax.ShapeDtypeStruct((1, B_WIDTH), F32), jax.ShapeDtypeStruct((1, B_WIDTH), F32),
                   jax.ShapeDtypeStruct(w_s.shape, F32), jax.ShapeDtypeStruct((nh, 2, CHUNK), F32)],
        compiler_params=_cparams("parallel", "arbitrary"),
    )(proj, proj, ln_g, ln_b, w_s, b_s.reshape(nh, 2, CHUNK), dy)
    dbu, dbv, dg, db, dws, dbs = res
    return dbu, dbv, dg, db, dws, dbs.reshape(b_s.shape)


def _tri_apply(a, lower):
    l = a.shape[0]
    r, c = _iota2((l, l), 0), _iota2((l, l), 1)
    t = jnp.where((r >= c) if lower else (r <= c), 1.0, 0.0).astype(BF16)
    hi = a.astype(BF16)
    r1 = a - hi.astype(F32)
    mid = r1.astype(BF16)
    lo = (r1 - mid.astype(F32)).astype(BF16)
    dn = _DN["nn"]
    return (lax.dot_general(t, hi, dn, preferred_element_type=F32) + lax.dot_general(t, mid, dn, preferred_element_type=F32)
            + lax.dot_general(t, lo, dn, preferred_element_type=F32))


@jax.custom_vjp
def _cumsum_rows(a):
    return _tri_apply(a, True)


_cumsum_rows.defvjp(lambda a: (_tri_apply(a, True), None), lambda _, g: (_tri_apply(g, False),))

SSD_GROUP_HEADS = 8
SSD_GROUP_PAIRS = 4


def _ssd_group(x0, x1, x2, x3, dt_raw, bias, alog, bm, cm, p0, p1, p2, p3):
    xs, prevs = (x0, x1, x2, x3), (p0, p1, p2, p3)
    dt = _softplus(dt_raw + bias)
    a = dt * (-jnp.exp(alog))
    acs = _cumsum_rows(a)
    alast = jnp.sum(a, axis=0, keepdims=True)
    dt_t, acs_t = dt.T, acs.T
    cb = _nt(cm, bm)
    tri = _iota2((CHUNK, CHUNK), 0) >= _iota2((CHUNK, CHUNK), 1)
    lane = _iota2((CHUNK, LANES), 1)
    sub = _iota2((LANES, CHUNK), 0)
    lane1 = _iota2((1, LANES), 1)

    def column(v, i):
        return jnp.broadcast_to(jnp.sum(jnp.where(lane == i, v, 0.0), axis=1, keepdims=True), (CHUNK, LANES))

    def row(vt, i):
        return jnp.broadcast_to(jnp.sum(jnp.where(sub == i, vt, 0.0), axis=0, keepdims=True), (CHUNK, CHUNK))

    heads = []
    for i in range(SSD_GROUP_HEADS):
        col_a = column(acs, i)
        al = jnp.sum(jnp.where(lane1 == i, alast, 0.0), axis=1, keepdims=True)
        m = cb * jnp.exp(jnp.where(tri, col_a - row(acs_t, i), NEG)) * row(dt_t, i)
        heads.append((m, jnp.exp(col_a), column(dt, i) * jnp.exp(al - col_a), jnp.exp(al)))
    lo_lanes = _lane_lt64((CHUNK, LANES))
    lo_rows = _iota2((LANES, SSM_STATE), 0) < HEAD_DIM
    ys, news = [], []
    for j in range(SSD_GROUP_PAIRS):
        (m0, ea0, w0, cd0), (m1, ea1, w1, cd1) = heads[2 * j], heads[2 * j + 1]
        x, prev = xs[j], prevs[j]
        ydiag = jnp.where(lo_lanes, _nn(m0, x), _nn(m1, x))
        yoff = jnp.where(lo_lanes, _nt(cm * ea0, prev), _nt(cm * ea1, prev))
        states = jnp.where(lo_rows, _tn(x, bm * w0), _tn(x, bm * w1))
        ys.append(ydiag + yoff)
        news.append(prev * jnp.where(lo_rows, cd0, cd1) + states)
    return tuple(ys) + tuple(news)


SSD_GROUPS = 2


def _ssd2_specs(seq, rev):
    ncs = seq // CHUNK
    wide = SSD_GROUPS * LANES

    def row(s, c):
        return s * ncs + (ncs - 1 - c if rev else c)

    return dict(
        x=pl.BlockSpec((CHUNK, C_WIDTH), lambda s, c: (row(s, c), 0)),
        dt=pl.BlockSpec((CHUNK, wide), lambda s, c: (row(s, c), 0)),
        vec=pl.BlockSpec((1, wide), lambda s, c: (0, 0)),
        bm=pl.BlockSpec((CHUNK, wide), lambda s, c: (row(s, c), C_WIDTH // wide)),
        cm=pl.BlockSpec((CHUNK, wide), lambda s, c: (row(s, c), C_WIDTH // wide + 1)),
        st=pl.BlockSpec((None, C_WIDTH // LANES, LANES, SSM_STATE), lambda s, c: (row(s, c), 0, 0, 0)),
        ncs=ncs,
    )


def _lane_blocks(ref, grp):
    return [ref[:, pl.ds((grp * SSD_GROUP_PAIRS + j) * LANES, LANES)] for j in range(SSD_GROUP_PAIRS)]


def _group_block(ref, grp):
    return ref[:, pl.ds(grp * LANES, LANES)]


def _ssd2_fwd(xbc_act, dt_raw, dt_bias, a_log, seq, name):
    t = xbc_act.shape[0]
    sp = _ssd2_specs(seq, False)

    npair = SSD_GROUP_PAIRS

    def body(x_ref, dt_ref, bias_ref, alog_ref, bm_ref, cm_ref, y_ref, prev_ref, state):
        @pl.when(pl.program_id(1) == 0)
        def _():
            state[...] = jnp.zeros_like(state)

        for grp in range(SSD_GROUPS):
            prevs = [state[grp * npair + j] for j in range(npair)]
            for j in range(npair):
                prev_ref[grp * npair + j] = prevs[j]
            res = _ssd_group(*_lane_blocks(x_ref, grp), _group_block(dt_ref, grp), _group_block(bias_ref, grp),
                             _group_block(alog_ref, grp), _group_block(bm_ref, grp), _group_block(cm_ref, grp), *prevs)
            for j in range(npair):
                y_ref[:, pl.ds((grp * npair + j) * LANES, LANES)] = res[j]
                state[grp * npair + j] = res[npair + j]

    return pl.pallas_call(
        body, name=name, grid=(t // seq, sp["ncs"]),
        in_specs=[sp["x"], sp["dt"], sp["vec"], sp["vec"], sp["bm"], sp["cm"]],
        out_specs=[sp["x"], sp["st"]],
        out_shape=[jax.ShapeDtypeStruct((t, C_WIDTH), F32),
                   jax.ShapeDtypeStruct((t // CHUNK, C_WIDTH // LANES, LANES, SSM_STATE), F32)],
        scratch_shapes=[pltpu.VMEM((C_WIDTH // LANES, LANES, SSM_STATE), F32)],
        compiler_params=_cparams("parallel", "arbitrary"),
    )(xbc_act, dt_raw, dt_bias, a_log, xbc_act, xbc_act)


def _ssd2_bwd(xbc_act, dt_raw, dt_bias, a_log, prev_saved, dy, seq, name):
    t = xbc_act.shape[0]
    sp = _ssd2_specs(seq, True)
    npair = SSD_GROUP_PAIRS

    def body(x_ref, dt_ref, bias_ref, alog_ref, bm_ref, cm_ref, prev_ref, dy_ref,
             dx_ref, ddt_ref, dbias_ref, dalog_ref, dbm_ref, dcm_ref, dstate):
        @pl.when(pl.program_id(1) == 0)
        def _():
            dstate[...] = jnp.zeros_like(dstate)

        @pl.when(jnp.logical_and(pl.program_id(0) == 0, pl.program_id(1) == 0))
        def _():
            dbias_ref[...] = jnp.zeros_like(dbias_ref)
            dalog_ref[...] = jnp.zeros_like(dalog_ref)

        for grp in range(SSD_GROUPS):
            lanes = pl.ds(grp * LANES, LANES)
            _, vjp = jax.vjp(_ssd_group, *_lane_blocks(x_ref, grp), _group_block(dt_ref, grp), _group_block(bias_ref, grp),
                             _group_block(alog_ref, grp), _group_block(bm_ref, grp), _group_block(cm_ref, grp),
                             *[prev_ref[grp * npair + j] for j in range(npair)])
            grads = vjp(tuple(_lane_blocks(dy_ref, grp)) + tuple(dstate[grp * npair + j] for j in range(npair)))
            for j in range(npair):
                dx_ref[:, pl.ds((grp * npair + j) * LANES, LANES)] = grads[j]
                dstate[grp * npair + j] = grads[npair + 5 + j]
            ddt_ref[:, lanes] = grads[npair].astype(BF16)
            dbias_ref[:, lanes] += grads[npair + 1]
            dalog_ref[:, lanes] += grads[npair + 2]
            dbm_ref[:, lanes] = grads[npair + 3]
            dcm_ref[:, lanes] = grads[npair + 4]

    return pl.pallas_call(
        body, name=name, grid=(t // seq, sp["ncs"]),
        in_specs=[sp["x"], sp["dt"], sp["vec"], sp["vec"], sp["bm"], sp["cm"], sp["st"], sp["x"]],
        out_specs=[sp["x"], sp["dt"], sp["vec"], sp["vec"], sp["dt"], sp["dt"]],
        out_shape=[jax.ShapeDtypeStruct((t, C_WIDTH), F32), jax.ShapeDtypeStruct((t, 2 * LANES), BF16),
                   jax.ShapeDtypeStruct((1, 2 * LANES), F32), jax.ShapeDtypeStruct((1, 2 * LANES), F32),
                   jax.ShapeDtypeStruct((t, 2 * SSM_STATE), F32), jax.ShapeDtypeStruct((t, 2 * SSM_STATE), F32)],
        scratch_shapes=[pltpu.VMEM((C_WIDTH // LANES, LANES, SSM_STATE), F32)],
        compiler_params=_cparams("arbitrary", "arbitrary"),
    )(xbc_act, dt_raw, dt_bias, a_log, xbc_act, xbc_act, prev_saved, dy)


def _ssd2_assemble(dxs_ssd, dxs_skip, dbm, dcm, name):
    t = dxs_ssd.shape[0]
    tm = _row_tile(t)

    def body(a_ref, b_ref, dbm_ref, dcm_ref, o_ref):
        o_ref[:, pl.ds(0, C_WIDTH)] = a_ref[...] + b_ref[...]
        o_ref[:, pl.ds(C_WIDTH, 2 * SSM_STATE)] = dbm_ref[...]
        o_ref[:, pl.ds(C_WIDTH + 2 * SSM_STATE, 2 * SSM_STATE)] = dcm_ref[...]

    wide = pl.BlockSpec((tm, C_WIDTH), lambda i: (i, 0))
    narrow = pl.BlockSpec((tm, 2 * SSM_STATE), lambda i: (i, 0))
    return pl.pallas_call(
        body, name=name, grid=(t // tm,), in_specs=[wide, wide, narrow, narrow],
        out_specs=pl.BlockSpec((tm, D_CONV_C), lambda i: (i, 0)),
        out_shape=jax.ShapeDtypeStruct((t, D_CONV_C), F32),
        compiler_params=_cparams("parallel"),
    )(dxs_ssd, dxs_skip, dbm, dcm)


def _expand_mats():
    head = jnp.arange(LANES)[:, None]
    e64 = (head == (jnp.arange(C_WIDTH)[None, :] // HEAD_DIM)).astype(F32)
    e128 = (head == (jnp.arange(C_HEADS * LANES)[None, :] // LANES)).astype(F32)
    return e64, e128


def _ssd_prep_fn(dt_raw, dt_bias, a_log, e64, e128):
    dt = _softplus(dt_raw + dt_bias)
    a = dt * (-jnp.exp(a_log))
    incl = (_iota2((CHUNK, CHUNK), 0) >= _iota2((CHUNK, CHUNK), 1)).astype(F32)
    acs = _xdot(incl, a)
    alast = _xdot(jnp.ones((CHUNK, CHUNK), F32), a)
    return _xdot(dt, e64), _xdot(acs, e64), _xdot(alast, e64), _xdot(acs, e128)


def _ssd_prep_specs():
    blk = lambda w: pl.BlockSpec((CHUNK, w), lambda i: (i, 0))
    const = lambda r, w: pl.BlockSpec((r, w), lambda i: (0, 0))
    ins = [blk(LANES), const(1, LANES), const(1, LANES), const(LANES, C_WIDTH), const(LANES, C_HEADS * LANES)]
    outs = [blk(C_WIDTH), blk(C_WIDTH), blk(C_WIDTH), blk(C_HEADS * LANES)]
    return ins, outs


def _ssd_prep_fwd(dt_raw, dt_bias, a_log, name):
    t = dt_raw.shape[0]
    e64, e128 = _expand_mats()
    ins, outs = _ssd_prep_specs()

    def body(raw_ref, bias_ref, alog_ref, e64_ref, e128_ref, dt_ref, acs_ref, alast_ref, acs128_ref):
        res = _ssd_prep_fn(raw_ref[...], bias_ref[...], alog_ref[...], e64_ref[...], e128_ref[...])
        for ref, v in zip((dt_ref, acs_ref, alast_ref, acs128_ref), res):
            ref[...] = v

    return pl.pallas_call(
        body, name=name, grid=(t // CHUNK,), in_specs=ins, out_specs=outs,
        out_shape=[jax.ShapeDtypeStruct((t, C_WIDTH), F32)] * 3 + [jax.ShapeDtypeStruct((t, C_HEADS * LANES), F32)],
        compiler_params=_cparams("parallel"),
    )(dt_raw, dt_bias, a_log, e64, e128)


def _ssd_prep_bwd(dt_raw, dt_bias, a_log, d_dt, d_acs, d_alast, d_acs128, name):
    t = dt_raw.shape[0]
    e64, e128 = _expand_mats()
    ins, outs = _ssd_prep_specs()
    vec = pl.BlockSpec((1, LANES), lambda i: (0, 0))

    def body(raw_ref, bias_ref, alog_ref, e64_ref, e128_ref, g0, g1, g2, g3, draw_ref, dbias_ref, dalog_ref):
        @pl.when(pl.program_id(0) == 0)
        def _():
            dbias_ref[...] = jnp.zeros_like(dbias_ref)
            dalog_ref[...] = jnp.zeros_like(dalog_ref)

        e64v, e128v = e64_ref[...], e128_ref[...]
        _, vjp = jax.vjp(lambda r, b, al: _ssd_prep_fn(r, b, al, e64v, e128v),
                         raw_ref[...], bias_ref[...], alog_ref[...])
        draw, dbias, dalog = vjp((g0[...], g1[...], g2[...], g3[...]))
        draw_ref[...] = draw.astype(BF16)
        dbias_ref[...] += dbias
        dalog_ref[...] += dalog

    return pl.pallas_call(
        body, name=name, grid=(t // CHUNK,), in_specs=ins + outs,
        out_specs=[pl.BlockSpec((CHUNK, LANES), lambda i: (i, 0)), vec, vec],
        out_shape=[jax.ShapeDtypeStruct((t, LANES), BF16), jax.ShapeDtypeStruct((1, LANES), F32),
                   jax.ShapeDtypeStruct((1, LANES), F32)],
        compiler_params=_cparams("arbitrary"),
    )(dt_raw, dt_bias, a_log, e64, e128, d_dt, d_acs, d_alast, d_acs128)


def _ssd_chunk(x, dt, acs, alast, col0, col1, bm, cm, prev):
    xdt = x * dt
    cb = _nt(cm, bm)
    tri = _iota2((CHUNK, CHUNK), 0) >= _iota2((CHUNK, CHUNK), 1)
    l0 = jnp.exp(jnp.where(tri, col0 - col0.T, NEG))
    l1 = jnp.exp(jnp.where(tri, col1 - col1.T, NEG))
    ydiag = jnp.where(_lane_lt64((CHUNK, LANES)), _nn(cb * l0, xdt), _nn(cb * l1, xdt))
    states = _tn(xdt * jnp.exp(alast - acs), bm)
    yoff = _nt(cm, prev) * jnp.exp(acs)
    new = prev * jnp.exp(alast).T + states
    return ydiag + yoff, new


def _ssd_specs(seq, rev):
    ncs = seq // CHUNK
    npair = C_WIDTH // LANES

    def row(s, c):
        return s * ncs + (ncs - 1 - c if rev else c)

    return dict(
        x=pl.BlockSpec((CHUNK, LANES), lambda s, j, c: (row(s, c), j)),
        bm=pl.BlockSpec((CHUNK, SSM_STATE), lambda s, j, c: (row(s, c), C_WIDTH // LANES + j // 4)),
        cm=pl.BlockSpec((CHUNK, SSM_STATE), lambda s, j, c: (row(s, c), C_WIDTH // LANES + 2 + j // 4)),
        col=pl.BlockSpec((CHUNK, 2 * LANES), lambda s, j, c: (row(s, c), j)),
        st=pl.BlockSpec((None, None, LANES, SSM_STATE), lambda s, j, c: (row(s, c), j, 0, 0)),
        npair=npair, ncs=ncs,
    )


def _ssd_fwd(xbc_act, dt64, acs64, alast64, acs128, seq, name):
    t = xbc_act.shape[0]
    sp = _ssd_specs(seq, False)

    def body(x_ref, dt_ref, acs_ref, alast_ref, col_ref, bm_ref, cm_ref, y_ref, prev_ref, state):
        @pl.when(pl.program_id(2) == 0)
        def _():
            state[...] = jnp.zeros_like(state)

        prev = state[...]
        prev_ref[...] = prev
        y, new = _ssd_chunk(x_ref[...], dt_ref[...], acs_ref[...], alast_ref[...],
                            col_ref[:, pl.ds(0, LANES)], col_ref[:, pl.ds(LANES, LANES)],
                            bm_ref[...], cm_ref[...], prev)
        y_ref[...] = y
        state[...] = new

    return pl.pallas_call(
        body, name=name, grid=(t // seq, sp["npair"], sp["ncs"]),
        in_specs=[sp["x"], sp["x"], sp["x"], sp["x"], sp["col"], sp["bm"], sp["cm"]],
        out_specs=[sp["x"], sp["st"]],
        out_shape=[jax.ShapeDtypeStruct((t, C_WIDTH), F32),
                   jax.ShapeDtypeStruct((t // CHUNK, sp["npair"], LANES, SSM_STATE), F32)],
        scratch_shapes=[pltpu.VMEM((LANES, SSM_STATE), F32)],
        compiler_params=_cparams("parallel", "parallel", "arbitrary"),
    )(xbc_act, dt64, acs64, alast64, acs128, xbc_act, xbc_act)


def _ssd_bwd(xbc_act, dt64, acs64, alast64, acs128, prev_saved, dy, seq, name):
    t = xbc_act.shape[0]
    sp = _ssd_specs(seq, True)

    def body(x_ref, dt_ref, acs_ref, alast_ref, col_ref, bm_ref, cm_ref, prev_ref, dy_ref,
             dx_ref, ddt_ref, dacs_ref, dalast_ref, dcol_ref, dbc_ref, dstate):
        @pl.when(pl.program_id(2) == 0)
        def _():
            dstate[...] = jnp.zeros_like(dstate)

        _, vjp = jax.vjp(_ssd_chunk, x_ref[...], dt_ref[...], acs_ref[...], alast_ref[...],
                         col_ref[:, pl.ds(0, LANES)], col_ref[:, pl.ds(LANES, LANES)],
                         bm_ref[...], cm_ref[...], prev_ref[...])
        dx, ddt, dacs, dalast, dc0, dc1, dbm, dcm, dprev = vjp((dy_ref[...], dstate[...]))
        dx_ref[...] = dx
        ddt_ref[...] = ddt
        dacs_ref[...] = dacs
        dalast_ref[...] = dalast
        dcol_ref[:, pl.ds(0, LANES)] = dc0
        dcol_ref[:, pl.ds(LANES, LANES)] = dc1
        dbc_ref[:, pl.ds(0, SSM_STATE)] = dbm
        dbc_ref[:, pl.ds(SSM_STATE, SSM_STATE)] = dcm
        dstate[...] = dprev

    wide = jax.ShapeDtypeStruct((t, C_WIDTH), F32)
    return pl.pallas_call(
        body, name=name, grid=(t // seq, sp["npair"], sp["ncs"]),
        in_specs=[sp["x"], sp["x"], sp["x"], sp["x"], sp["col"], sp["bm"], sp["cm"], sp["st"], sp["x"]],
        out_specs=[sp["x"], sp["x"], sp["x"], sp["x"], sp["col"], sp["col"]],
        out_shape=[wide, wide, wide, wide, jax.ShapeDtypeStruct((t, 2 * C_WIDTH), F32),
                   jax.ShapeDtypeStruct((t, 2 * C_WIDTH), F32)],
        scratch_shapes=[pltpu.VMEM((LANES, SSM_STATE), F32)],
        compiler_params=_cparams("parallel", "parallel", "arbitrary"),
    )(xbc_act, dt64, acs64, alast64, acs128, xbc_act, xbc_act, prev_saved, dy)


def _ssd_post_fn(y, xs, z, dskip, g):
    v = (y + dskip * xs) * _silu(z)
    return v * lax.rsqrt(jnp.mean(v * v, axis=-1, keepdims=True) + EPS) * g


def _ssd_post_specs(tm, order):
    gw = C_WIDTH // 2

    def im(f):
        return lambda *ids: f(*order(*ids))
    return dict(
        blk=pl.BlockSpec((tm, gw), im(lambda g, r: (r, g))),
        z=pl.BlockSpec((tm, gw), im(lambda g, r: (r, COL_Z * LANES // gw + g))),
        vec=pl.BlockSpec((1, gw), im(lambda g, r: (0, g))),
    )


def _ssd_post_fwd(y_ssd, xbc_act, proj, dskip64, norm_g, name):
    t = y_ssd.shape[0]
    tm = _row_tile(t)
    sp = _ssd_post_specs(tm, lambda r, g: (g, r))

    def body(y_ref, xs_ref, z_ref, ds_ref, g_ref, o_ref):
        o_ref[...] = _ssd_post_fn(y_ref[...], xs_ref[...], z_ref[...], ds_ref[...], g_ref[...]).astype(BF16)

    return pl.pallas_call(
        body, name=name, grid=(t // tm, 2),
        in_specs=[sp["blk"], sp["blk"], sp["z"], sp["vec"], sp["vec"]], out_specs=sp["blk"],
        out_shape=jax.ShapeDtypeStruct((t, C_WIDTH), BF16),
        compiler_params=_cparams("parallel", "parallel"),
    )(y_ssd, xbc_act, proj, dskip64, norm_g)


def _ssd_post_bwd(y_ssd, xbc_act, proj, dskip64, norm_g, dyc, name, dy_col=0):
    t = y_ssd.shape[0]
    tm = _row_tile(t)
    sp = _ssd_post_specs(tm, lambda g, r: (g, r))
    dy_spec = pl.BlockSpec((tm, C_WIDTH // 2), lambda g, r: (r, dy_col + g))

    def body(y_ref, xs_ref, z_ref, ds_ref, g_ref, dyc_ref, dy_ref, dxs_ref, dz_ref, dds_ref, dg_ref):
        @pl.when(pl.program_id(1) == 0)
        def _():
            dds_ref[...] = jnp.zeros_like(dds_ref)
            dg_ref[...] = jnp.zeros_like(dg_ref)

        _, vjp = jax.vjp(_ssd_post_fn, y_ref[...], xs_ref[...], z_ref[...], ds_ref[...], g_ref[...])
        dy, dxs, dz, dds, dg = vjp(dyc_ref[...])
        dy_ref[...] = dy
        dxs_ref[...] = dxs
        dz_ref[...] = dz.astype(BF16)
        dds_ref[...] += dds
        dg_ref[...] += dg

    wide = jax.ShapeDtypeStruct((t, C_WIDTH), F32)
    vec = jax.ShapeDtypeStruct((1, C_WIDTH), F32)
    return pl.pallas_call(
        body, name=name, grid=(2, t // tm),
        in_specs=[sp["blk"], sp["blk"], sp["z"], sp["vec"], sp["vec"], dy_spec],
        out_specs=[sp["blk"], sp["blk"], sp["blk"], sp["vec"], sp["vec"]],
        out_shape=[wide, wide, jax.ShapeDtypeStruct((t, C_WIDTH), BF16), vec, vec],
        compiler_params=_cparams("parallel", "arbitrary"),
    )(y_ssd, xbc_act, proj, dskip64, norm_g, dyc)


def _ssd_assemble(dxs_ssd, dxs_skip, dbc, name):
    t = dxs_ssd.shape[0]
    tm = _row_tile(t)
    npair = C_WIDTH // LANES

    def body(a_ref, b_ref, dbc_ref, o_ref):
        o_ref[:, pl.ds(0, C_WIDTH)] = a_ref[...] + b_ref[...]
        for grp in range(2):
            for which in range(2):
                acc = jnp.zeros((tm, SSM_STATE), F32)
                for j in range(grp * npair // 2, (grp + 1) * npair // 2):
                    acc = acc + dbc_ref[:, pl.ds((2 * j + which) * SSM_STATE, SSM_STATE)]
                o_ref[:, pl.ds(C_WIDTH + (2 * which + grp) * SSM_STATE, SSM_STATE)] = acc

    return pl.pallas_call(
        body, name=name, grid=(t // tm,),
        in_specs=[pl.BlockSpec((tm, C_WIDTH), lambda i: (i, 0))] * 2 + [pl.BlockSpec((tm, 2 * C_WIDTH), lambda i: (i, 0))],
        out_specs=pl.BlockSpec((tm, D_CONV_C), lambda i: (i, 0)),
        out_shape=jax.ShapeDtypeStruct((t, D_CONV_C), F32),
        compiler_params=_cparams("parallel"),
    )(dxs_ssd, dxs_skip, dbc)


def _pad_taps(w):
    return jnp.pad(w, ((0, CONV_PAD - w.shape[0]), (0, 0)))


def _pad_heads(v):
    return jnp.pad(v, (0, LANES - v.shape[0])).reshape(1, LANES)


def _group_heads(a):
    pad = [(0, 0)] * (a.ndim - 1) + [(0, LANES - SSD_GROUP_HEADS)]
    return jnp.concatenate([jnp.pad(a[..., :SSD_GROUP_HEADS], pad), jnp.pad(a[..., SSD_GROUP_HEADS:], pad)], axis=-1)


def _ungroup_heads(a):
    return jnp.concatenate([a[..., :SSD_GROUP_HEADS], a[..., LANES:LANES + SSD_GROUP_HEADS]], axis=-1)


def _layer_fwd(x, p, seq, li, after=()):
    n = f"l{li}_"
    h1 = _rms_fwd(x, p["norm1_g"], n + "rms1", after=after)
    proj = _matmul(h1, p["w_main"], mode="nn", name=n + "inproj")
    dt_raw = _matmul(h1, p["w_dt"], mode="nn", name=n + "inproj_dt")
    row = lambda v: v.reshape(1, -1)
    ya, conv_a = _conv_fwd("a", proj, _pad_taps(p["conv_a_w"]), row(p["conv_a_b"]), (row(p["ln_a_g"]), row(p["ln_a_b"])),
                           seq, n + "conva", out_dtype=BF16, keep_conv=True)
    yb = _gmlp_fwd(proj, row(p["ln_b_g"]), row(p["ln_b_b"]), p["w_spatial"], p["b_spatial"], n + "gmlp")
    xbc_act = _conv_fwd("c", proj, _pad_taps(p["conv_c_w"]), row(p["conv_c_b"]), (), seq, n + "convc")
    y_ssd, prev = _ssd2_fwd(xbc_act, dt_raw, _group_heads(row(p["dt_bias"])), _group_heads(row(p["a_log"])), seq, n + "ssd")
    dskip64 = jnp.repeat(p["d_skip"], HEAD_DIM).reshape(1, C_WIDTH)
    yc = _ssd_post_fwd(y_ssd, xbc_act, proj, dskip64, row(p["norm_c_g"]), n + "ssdpost")
    ycat = jnp.concatenate([ya, yb, yc], axis=1)
    x1 = _matmul(ycat, p["w_out"], mode="nn", name=n + "outproj", add=x)
    h2 = _rms_fwd(x1, p["norm2_g"], n + "rms2")
    u, act = _matmul(h2, p["w_ff1"], mode="nn", name=n + "ff1", epilogue=_relu2_epilogue, out_dtypes=(F32, BF16),
                     b_chips=True)
    x2 = _matmul(act, p["w_ff2"], mode="nn", name=n + "ff2", add=x1)
    saved = dict(x=x, h1=h1, proj=proj, conv_a=conv_a, dt_raw=dt_raw, xbc_act=xbc_act, prev=prev, y_ssd=y_ssd,
                 dskip64=dskip64, ycat=ycat, x1=x1, h2=h2, u=u, act=act)
    return x2, saved


def _layer_bwd(dx2, p, s, seq, li, after=(), on_ffn_grads=None):
    n = f"l{li}_b_"
    row = lambda v: v.reshape(1, -1)
    g = {}
    du = _matmul(dx2, p["w_ff2"], mode="nt", name=n + "ff2_dx", epilogue=_relu2_bwd_epilogue, extra=s["u"],
                 out_dtypes=(BF16,), after=after)
    g["w_ff2"] = _matmul(s["act"], dx2, mode="tn", name=n + "ff2_dw")
    g["w_ff1"] = _matmul(s["h2"], du, mode="tn", name=n + "ff1_dw", out_chips=True)
    dh2 = _matmul(du, p["w_ff1"], mode="nt", name=n + "ff1_dx", b_chips=True)
    dx1, g["norm2_g"] = _rms_bwd(s["x1"], p["norm2_g"], dh2, dx2, n + "rms2")
    g["w_out"] = _matmul(s["ycat"], dx1, mode="tn", name=n + "out_dw")
    dycat = _matmul(dx1, p["w_out"], mode="nt", name=n + "out_dx",
                    after=() if on_ffn_grads is None else on_ffn_grads(g))
    proj = s["proj"]
    (dval, dgate), dwa, dba, (dlag, dlab) = _conv_bwd(
        "a", proj, _pad_taps(p["conv_a_w"]), row(p["conv_a_b"]), (row(p["ln_a_g"]), row(p["ln_a_b"])), dycat, seq,
        n + "conva", dy_col=0, conv_out=s["conv_a"])
    g["conv_a_w"], g["conv_a_b"], g["ln_a_g"], g["ln_a_b"] = dwa[:CONV_A_K], dba[0], dlag[0], dlab[0]
    dbu, dbv, dlbg, dlbb, g["w_spatial"], g["b_spatial"] = _gmlp_bwd(
        proj, row(p["ln_b_g"]), row(p["ln_b_b"]), p["w_spatial"], p["b_spatial"], dycat, n + "gmlp",
        dy_col=A_WIDTH // LANES)
    g["ln_b_g"], g["ln_b_b"] = dlbg[0], dlbb[0]
    dy_ssd, dxs_skip, dz, dds, dncg = _ssd_post_bwd(s["y_ssd"], s["xbc_act"], proj, s["dskip64"], row(p["norm_c_g"]),
                                                    dycat, n + "ssdpost", dy_col=(A_WIDTH + B_WIDTH) * 2 // C_WIDTH)
    g["norm_c_g"] = dncg[0]
    g["d_skip"] = dds.reshape(C_HEADS, HEAD_DIM).sum(axis=1)
    dxs, ddt_raw, ddtb, dalog, dbm, dcm = _ssd2_bwd(
        s["xbc_act"], s["dt_raw"], _group_heads(row(p["dt_bias"])), _group_heads(row(p["a_log"])), s["prev"], dy_ssd, seq,
        n + "ssd")
    g["dt_bias"], g["a_log"] = _ungroup_heads(ddtb)[0], _ungroup_heads(dalog)[0]
    dconv = _ssd2_assemble(dxs, dxs_skip, dbm, dcm, n + "ssdasm")
    (dxbc,), dwc, dbcv, _ = _conv_bwd("c", proj, _pad_taps(p["conv_c_w"]), row(p["conv_c_b"]), (), dconv, seq, n + "convc")
    g["conv_c_w"], g["conv_c_b"] = dwc[:CONV_C_K], dbcv[0]
    dproj = jnp.concatenate([dval, dgate, dbu, dbv, dz, dxbc], axis=1)
    g["w_main"] = _matmul(s["h1"], dproj, mode="tn", name=n + "in_dw")
    g["w_dt"] = _matmul(s["h1"], ddt_raw, mode="tn", name=n + "indt_dw")
    dh1 = _matmul(dproj, p["w_main"], mode="nt", name=n + "in_dx")
    dh1 = _matmul(ddt_raw, p["w_dt"], mode="nt", name=n + "indt_dx", add=dh1)
    dx, g["norm1_g"] = _rms_bwd(s["x"], p["norm1_g"], dh1, dx1, n + "rms1")
    return dx, g


EW_BLOCK_BYTES = 1 << 20


def _ew(fn, ins, out_dtypes, name, leads=None):
    leads = leads or [None] * len(ins)
    rows, c = ins[0].shape[-2:]
    tr = _pick(rows, [t for t in (2048, 1024, 512, 256, 128, 64, 32, 16, 8) if t * c * 4 <= EW_BLOCK_BYTES])
    n_in = len(ins)

    def spec(lead):
        if lead is None:
            return pl.BlockSpec((tr, c), lambda i: (i, 0))
        return pl.BlockSpec((None, tr, c), functools.partial(lambda i, k: (k, i, 0), k=lead))

    def body(*refs):
        outs = fn(*[r[...].astype(F32) for r in refs[:n_in]])
        for o_ref, o in zip(refs[n_in:], outs):
            o_ref[...] = o.astype(o_ref.dtype)

    return pl.pallas_call(
        body, name=name, grid=(rows // tr,),
        in_specs=[spec(l) for l in leads], out_specs=[spec(None)] * len(out_dtypes),
        out_shape=[jax.ShapeDtypeStruct((rows, c), dt) for dt in out_dtypes],
        compiler_params=_cparams("parallel"),
    )(*ins)


def _adam_fn(w, g, m, v):
    m2 = ADAM_B1 * m + (1.0 - ADAM_B1) * g
    v2 = ADAM_B2 * v + (1.0 - ADAM_B2) * (g * g)
    m_hat = m2 / (1.0 - ADAM_B1 ** ADAM_STEP)
    v_hat = v2 / (1.0 - ADAM_B2 ** ADAM_STEP)
    delta = -ADAM_LR * (m_hat / (jnp.sqrt(v_hat) + ADAM_EPS) + ADAM_WD * w)
    return delta, m2, v2


def _adam(w, g, m, v, name):
    shape = w.shape
    two_d = lambda a: a.reshape(-1, shape[-1])
    outs = _ew(_adam_fn, [two_d(w), two_d(g), two_d(m), two_d(v)], (F32, F32, F32), name)
    return [o.reshape(shape) for o in outs]


_ANY = pl.BlockSpec(memory_space=pl.ANY)


def _mesh_pos():
    return lax.axis_index("x"), lax.axis_index("y"), lax.axis_index("c")


def _peer_chips(x, y):
    return [(1 - x, y), (x, 1 - y), (1 - x, 1 - y)]


def _remote(src, dst, send_sems, recv_sems, sem, to):
    return pltpu.make_async_remote_copy(src_ref=src, dst_ref=dst, send_sem=send_sems.at[sem],
                                        recv_sem=recv_sems.at[sem], device_id=to, device_id_type=MESH)


def _half_rows(n_rows, which):
    half = n_rows // 2
    return pl.ds(pl.multiple_of(which * half, 8), half)


def _comm_call(body, ins, out_shapes, n_sems, name):
    scratch = [pltpu.SemaphoreType.DMA((n_sems,)), pltpu.SemaphoreType.DMA((n_sems,))]
    return pl.pallas_call(
        body, name=name, in_specs=[_ANY] * len(ins), out_specs=[_ANY] * len(out_shapes),
        out_shape=out_shapes, scratch_shapes=scratch,
    )(*ins)


def _gather_weights(big, small, name):
    nb, ns = len(big), len(small)
    n = nb + ns

    def body(*refs):
        ins, outs = refs[:n], refs[n:2 * n]
        send_sems, recv_sems = refs[2 * n:]
        x, y, c = _mesh_pos()
        q = 2 * x + y
        me, sib = (x, y, c), (x, y, 1 - c)
        chips = _peer_chips(x, y)
        rem = functools.partial(_remote, send_sems=send_sems, recv_sems=recv_sems)
        first = []
        for i in range(nb):
            mine = _half_rows(big[i].shape[0], c)
            for k, (px, py) in enumerate(chips):
                first.append(rem(ins[i].at[mine], outs[i].at[q, mine], sem=6 * i + k, to=(px, py, c)))
        for j in range(ns):
            for k, (px, py) in enumerate(chips):
                first.append(rem(ins[nb + j], outs[nb + j].at[q], sem=6 * nb + 3 * j + k, to=(px, py, c)))
        for cp in first:
            cp.start()
        passed = []
        for i in range(nb):
            mine = _half_rows(big[i].shape[0], c)
            for k, (px, py) in enumerate(chips):
                landed = outs[i].at[2 * px + py, mine]
                rem(landed, landed, sem=6 * i + k, to=me).wait_recv()
                fwd = rem(landed, landed, sem=6 * i + 3 + k, to=sib)
                fwd.start()
                passed.append(fwd)
        for i in range(nb):
            other = _half_rows(big[i].shape[0], 1 - c)
            for k, (px, py) in enumerate(chips):
                theirs = outs[i].at[2 * px + py, other]
                rem(theirs, theirs, sem=6 * i + 3 + k, to=me).wait_recv()
        for j in range(ns):
            for k, (px, py) in enumerate(chips):
                dst = outs[nb + j].at[2 * px + py]
                rem(dst, dst, sem=6 * nb + 3 * j + k, to=me).wait_recv()
        for cp in first + passed:
            cp.wait_send()

    out_shapes = [jax.ShapeDtypeStruct((N_CHIPS,) + a.shape, a.dtype) for a in list(big) + list(small)]
    return _comm_call(body, list(big) + list(small), out_shapes, 6 * nb + 3 * ns, name)


def _sibling_other_halves(gs, name):
    n = len(gs)

    def other_half(ref, shape, c):
        rows = _half_rows(shape[-2], 1 - c)
        return ref.at[rows] if len(shape) == 2 else ref.at[:, rows]

    def body(*refs):
        ins, outs = refs[:n], refs[n:2 * n]
        send_sems, recv_sems = refs[2 * n:]
        x, y, c = _mesh_pos()
        copies = [_remote(other_half(ins[i], gs[i].shape, c), outs[i], send_sems, recv_sems, i, (x, y, 1 - c))
                  for i in range(n)]
        for cp in copies:
            cp.start()
        for cp in copies:
            cp.wait()

    out_shapes = [jax.ShapeDtypeStruct(g.shape[:-2] + (g.shape[-2] // 2, g.shape[-1]), g.dtype) for g in gs]
    return _comm_call(body, list(gs), out_shapes, n, name)


IN_SHARD = D_IN_PROJ // N_CHIPS


def _chipsum_in(mine, mine_dt, theirs, theirs_dt, name):
    r = mine.shape[0]
    tr = _pick(r, (128, 64, 32, 16, 8))
    last = D_MAIN - (N_CHIPS - 1) * IN_SHARD

    def body(a_ref, adt_ref, b_ref, bdt_ref, o32_ref, o16_ref):
        for p in range(N_CHIPS):
            wid = IN_SHARD if p < N_CHIPS - 1 else last
            s = a_ref[:, pl.ds(IN_SHARD * p, wid)] + b_ref[:, pl.ds(IN_SHARD * p, wid)]
            o32_ref[p, :, pl.ds(0, wid)] = s
            o16_ref[p, :, pl.ds(0, wid)] = s.astype(BF16)
        for grp in range(2):
            src = pl.ds(grp * LANES, SSD_GROUP_HEADS)
            s = adt_ref[:, src] + bdt_ref[:, src]
            dst = pl.ds(last + grp * SSD_GROUP_HEADS, SSD_GROUP_HEADS)
            o32_ref[N_CHIPS - 1, :, dst] = s
            o16_ref[N_CHIPS - 1, :, dst] = s.astype(BF16)

    wide = pl.BlockSpec((tr, D_MAIN), lambda i: (i, 0))
    narrow = pl.BlockSpec((tr, 2 * LANES), lambda i: (i, 0))
    out = pl.BlockSpec((N_CHIPS, tr, IN_SHARD), lambda i: (0, i, 0))
    return pl.pallas_call(
        body, name=name, grid=(r // tr,), in_specs=[wide, narrow, wide, narrow], out_specs=[out, out],
        out_shape=[jax.ShapeDtypeStruct((N_CHIPS, r, IN_SHARD), F32), jax.ShapeDtypeStruct((N_CHIPS, r, IN_SHARD), BF16)],
        compiler_params=_cparams("parallel"),
    )(mine, mine_dt, theirs, theirs_dt)


def _chip_scatter(cs, name):
    n = len(cs)

    def body(*refs):
        ins, outs = refs[:n], refs[n:2 * n]
        send_sems, recv_sems = refs[2 * n:]
        x, y, c = _mesh_pos()
        copies = []
        for i in range(n):
            for k, (px, py) in enumerate(_peer_chips(x, y)):
                copies.append(_remote(ins[i].at[2 * px + py], outs[i].at[k], send_sems, recv_sems, 3 * i + k, (px, py, c)))
        for cp in copies:
            cp.start()
        for cp in copies:
            cp.wait()

    out_shapes = [jax.ShapeDtypeStruct((3,) + a.shape[1:], a.dtype) for a in cs]
    return _comm_call(body, list(cs), out_shapes, 3 * n, name)


_HBM = pl.BlockSpec(memory_space=pltpu.HBM)
_SEM = pl.BlockSpec(memory_space=pltpu.SEMAPHORE)


def _in_hbm(a):
    return pltpu.with_memory_space_constraint(a, pltpu.HBM)


def _split_plan(kind, srcs, lands, x, y, c):
    plan = []
    for src, land in zip(srcs, lands):
        if kind == "sibling":
            rows = _half_rows(src.shape[-2], 1 - c)
            plan.append((src.at[rows] if len(src.shape) == 2 else src.at[:, rows], land, (x, y, 1 - c)))
            continue
        if kind == "allgather":
            peers = [(x, y, 1 - c)] + [(px, py, pc) for px, py in _peer_chips(x, y) for pc in (c, 1 - c)]
            plan += [(src, land.at[4 * x + 2 * y + c], peer) for peer in peers]
            continue
        for k, (px, py) in enumerate(_peer_chips(x, y)):
            if kind == "scatter":
                plan.append((src.at[2 * px + py], land.at[k], (px, py, c)))
            else:
                plan.append((src, land.at[2 * x + y], (px, py, c)))
    return plan


def _split_start(kind, srcs, land_shapes, name):
    n = len(srcs)

    def body(*refs):
        ins, lands = refs[:n], refs[n:2 * n]
        send_sems, recv_sems = refs[2 * n], refs[2 * n + 1]
        token = refs[-1]
        x, y, c = _mesh_pos()
        for i, (src, dst, to) in enumerate(_split_plan(kind, ins, lands, x, y, c)):
            pltpu.make_async_remote_copy(src_ref=src, dst_ref=dst, send_sem=send_sems.at[i], recv_sem=recv_sems.at[i],
                                         device_id=to, device_id_type=MESH).start()
        token[...] = jnp.zeros_like(token)

    zones = [lax.empty(s.shape, s.dtype) for s in land_shapes]
    n_sems = {"sibling": 1, "allgather": 7}.get(kind, 3) * n
    res = pl.pallas_call(
        body, name=name,
        out_shape=(pltpu.SemaphoreType.DMA((n_sems,)), pltpu.SemaphoreType.DMA((n_sems,)),
                   *[pltpu.HBM(a.shape, a.dtype) for a in srcs], *[pltpu.HBM(s.shape, s.dtype) for s in land_shapes],
                   jax.ShapeDtypeStruct((8, LANES), F32)),
        in_specs=[_HBM] * (2 * n), out_specs=(_SEM, _SEM, *[_HBM] * (2 * n), pl.BlockSpec(memory_space=pltpu.VMEM)),
        input_output_aliases={i: 2 + i for i in range(2 * n)},
        compiler_params=pltpu.CompilerParams(has_side_effects=pltpu.SideEffectType.DATAFLOW_SIDE_EFFECTING),
    )(*[_in_hbm(a) for a in srcs], *[_in_hbm(z) for z in zones])
    return dict(send=res[0], recv=res[1], srcs=list(res[2:2 + n]), lands=list(res[2 + n:2 + 2 * n]), token=res[-1], kind=kind)


def _split_wait(started, after, name):
    n = len(started["srcs"])
    kind = started["kind"]

    def body(*refs):
        ins, lands = refs[:n], refs[n:2 * n]
        send_sems, recv_sems = refs[2 * n], refs[2 * n + 1]
        x, y, c = _mesh_pos()
        for i, (src, dst, _) in enumerate(_split_plan(kind, ins, lands, x, y, c)):
            cp = pltpu.make_async_remote_copy(src_ref=src, dst_ref=dst, send_sem=send_sems.at[i], recv_sem=recv_sems.at[i],
                                              device_id=(x, y, c), device_id_type=MESH)
            cp.wait_send()
            cp.wait_recv()

    arrs = started["srcs"] + started["lands"]
    res = pl.pallas_call(
        body, name=name, out_shape=tuple(pltpu.HBM(a.shape, a.dtype) for a in arrs),
        in_specs=[_HBM] * (2 * n) + [_SEM, _SEM, pl.BlockSpec(memory_space=pl.ANY)], out_specs=tuple([_HBM] * (2 * n)),
        input_output_aliases={i: i for i in range(2 * n)},
        compiler_params=pltpu.CompilerParams(has_side_effects=pltpu.SideEffectType.DATAFLOW_SIDE_EFFECTING),
    )(*arrs, started["send"], started["recv"], after)
    return list(res[:n]), list(res[n:])


def _sibling_share(fs, name):
    n = len(fs)

    def body(*refs):
        ins, outs = refs[:n], refs[n:2 * n]
        send_sems, recv_sems = refs[2 * n:]
        x, y, c = _mesh_pos()
        copies = [_remote(ins[i], outs[i], send_sems, recv_sems, i, (x, y, 1 - c)) for i in range(n)]
        for cp in copies:
            cp.start()
        for cp in copies:
            cp.wait()

    out_shapes = [jax.ShapeDtypeStruct(a.shape, a.dtype) for a in fs]
    return _comm_call(body, list(fs), out_shapes, n, name)


def _allgather8(v, name, after=()):
    m = v.shape[0]

    def body(v_ref, *rest):
        out_ref, send_sems, recv_sems = rest[len(after):]
        x, y, c = _mesh_pos()
        me, sib = (x, y, c), (x, y, 1 - c)
        chips = _peer_chips(x, y)
        rem = functools.partial(_remote, send_sems=send_sems, recv_sems=recv_sems)

        def blk(px, py, pc):
            return out_ref.at[4 * px + 2 * py + pc]

        first = [rem(v_ref, blk(*me), sem=0, to=sib)]
        first += [rem(v_ref, blk(*me), sem=1 + k, to=(px, py, c)) for k, (px, py) in enumerate(chips)]
        for cp in first:
            cp.start()
        passed = []
        for k, (px, py) in enumerate(chips):
            landed = blk(px, py, c)
            rem(landed, landed, sem=1 + k, to=me).wait_recv()
            fwd = rem(landed, landed, sem=4 + k, to=sib)
            fwd.start()
            passed.append(fwd)
        rem(blk(*sib), blk(*sib), sem=0, to=me).wait_recv()
        for k, (px, py) in enumerate(chips):
            theirs = blk(px, py, 1 - c)
            rem(theirs, theirs, sem=4 + k, to=me).wait_recv()
        for cp in first + passed:
            cp.wait_send()

    return _comm_call(body, [v, *after], [jax.ShapeDtypeStruct((8, m, LANES), v.dtype)], 7, name)[0]


_WEIGHTS = ["norm1_g", "w_in", "conv_a_w", "conv_a_b", "ln_a_g", "ln_a_b", "ln_b_g", "ln_b_b", "w_spatial", "b_spatial",
            "conv_c_w", "conv_c_b", "dt_bias", "a_log", "d_skip", "norm_c_g", "w_out", "norm2_g", "w_ff1", "w_ff2", "final_g"]
_BIG = ["w_in", "w_out", "w_ff1", "w_ff2"]
_CONV_SHARDED = ["conv_a_w", "conv_c_w"]
_SMALL = [w for w in _WEIGHTS if w not in _BIG and w != "final_g"]
_PACK_ROWS = 512


def _pack(arrs):
    flat = jnp.concatenate([a.reshape(-1) for a in arrs])
    blk = _PACK_ROWS * LANES
    n = flat.shape[0]
    return jnp.pad(flat, (0, -(-n // blk) * blk - n)).reshape(-1, LANES)


def _unpack(packed, shapes):
    flat = packed.reshape(-1)
    out, off = [], 0
    for s in shapes:
        n = math.prod(s)
        out.append(flat[off:off + n].reshape(s))
        off += n
    return out


def _cols_to_chips(a):
    k = a.shape[0]
    return a.reshape(k, N_CHIPS, -1).transpose(1, 0, 2)


def _chips_to_cols(a):
    return a.transpose(1, 0, 2).reshape(a.shape[1], -1)


def _own_shards(w, li):
    return [w[k][li].astype(BF16) for k in _BIG] + [w[k][li] for k in _CONV_SHARDED]


def _assemble_w_in(gathered, own, name):
    k = own.shape[0]
    tr = _pick(k, (256, 128, 64, 32, 16))
    last = D_MAIN - (N_CHIPS - 1) * IN_SHARD

    def body(g_ref, own_ref, main_ref, dt_ref):
        x, y, _ = _mesh_pos()
        q = 2 * x + y
        dt_ref[...] = jnp.zeros_like(dt_ref)

        def place(read):
            for p in range(N_CHIPS):
                def _(p=p):
                    wid = IN_SHARD if p < N_CHIPS - 1 else last
                    main_ref[:, pl.ds(IN_SHARD * p, wid)] = read(p, pl.ds(0, wid))
                    if p == N_CHIPS - 1:
                        for grp in range(2):
                            dt_ref[:, pl.ds(grp * LANES, SSD_GROUP_HEADS)] = read(
                                p, pl.ds(last + grp * SSD_GROUP_HEADS, SSD_GROUP_HEADS))
                yield p, _

        for p, put in place(lambda p, cols: own_ref[:, cols]):
            pl.when(q == p)(put)
        for p, put in place(lambda p, cols: g_ref[p, :, cols]):
            pl.when(q != p)(put)

    return pl.pallas_call(
        body, name=name, grid=(k // tr,),
        in_specs=[pl.BlockSpec((N_CHIPS, tr, IN_SHARD), lambda i: (0, i, 0)), pl.BlockSpec((tr, IN_SHARD), lambda i: (i, 0))],
        out_specs=[pl.BlockSpec((tr, D_MAIN), lambda i: (i, 0)), pl.BlockSpec((tr, 2 * LANES), lambda i: (i, 0))],
        out_shape=[jax.ShapeDtypeStruct((k, D_MAIN), own.dtype), jax.ShapeDtypeStruct((k, 2 * LANES), own.dtype)],
        compiler_params=_cparams("parallel"),
    )(gathered, own)


def _layer_params(w, li, own, gathered, q):
    g_out, g_ff1, g_ff2, g_ca, g_cc = [lax.dynamic_update_index_in_dim(g, o, q, axis=0)
                                       for g, o in zip(gathered[1:], own[1:])]
    p = {k: w[k][li] for k in _SMALL if k not in _CONV_SHARDED}
    p["w_main"], p["w_dt"] = _assemble_w_in(gathered[0], own[0], f"l{li}_w_in")
    p["w_out"] = g_out.reshape(D_MIX, D_MODEL)
    p["w_ff1"] = g_ff1
    p["w_ff2"] = g_ff2.reshape(D_FF, D_MODEL)
    p["conv_a_w"] = _chips_to_cols(g_ca)
    p["conv_c_w"] = _chips_to_cols(g_cc)
    return p


def _ffn_out_grads(g):
    return [g["w_out"].reshape(N_CHIPS, -1, D_MODEL), g["w_ff1"], g["w_ff2"].reshape(N_CHIPS, -1, D_MODEL)]


def _half_shape(a):
    return jax.ShapeDtypeStruct(a.shape[:-2] + (a.shape[-2] // 2, a.shape[-1]), a.dtype)


def _chip_sums(g, early, early_from_sib, li, c, q):
    n = f"l{li}_rs_"
    late = [g["w_main"], g["w_dt"]]
    full = late + list(early)
    from_sib = list(_sibling_other_halves(late, n + "sib")) + list(early_from_sib)
    mine = [lax.dynamic_slice_in_dim(a, c * b.shape[-2], b.shape[-2], axis=a.ndim - 2) for a, b in zip(full, from_sib)]
    sums = [_chipsum_in(mine[0], mine[1], from_sib[0], from_sib[1], n + "chipsum0")]
    for i in range(2, len(full)):
        shape = from_sib[i].shape
        s32, s16 = _ew(lambda u, v: (u + v, u + v), [mine[i].reshape(-1, shape[-1]), from_sib[i].reshape(-1, shape[-1])],
                       (F32, BF16), n + f"chipsum{i - 1}")
        sums.append((s32.reshape(shape), s16.reshape(shape)))
    chip_f32 = [lax.dynamic_index_in_dim(s32, q, axis=0, keepdims=False) for s32, _ in sums]
    chip_bf16 = [s16 for _, s16 in sums]
    return chip_f32, chip_bf16


def _finish_reduce(chip_f32, from_chips, li, c):
    n = f"l{li}_rs_"
    halves = [_ew(lambda o, r0, r1, r2_: (((o + r0) + r1) + r2_,), [own, rb, rb, rb], (F32,), n + f"final{i}",
                  leads=[None, 0, 1, 2])[0] for i, (own, rb) in enumerate(zip(chip_f32, from_chips))]
    from_sib = _sibling_share(halves, n + "share")
    return [jnp.where(c == 0, jnp.concatenate([h, s], axis=0), jnp.concatenate([s, h], axis=0))
            for h, s in zip(halves, from_sib)]


def kernel(x, norm1_g, w_in, conv_a_w, conv_a_b, ln_a_g, ln_a_b, ln_b_g, ln_b_b, w_spatial, b_spatial, conv_c_w, conv_c_b, dt_bias, a_log, d_skip, norm_c_g, w_out, norm2_g, w_ff1, w_ff2, final_g, loss_target, m_norm1_g, m_w_in, m_conv_a_w, m_conv_a_b, m_ln_a_g, m_ln_a_b, m_ln_b_g, m_ln_b_b, m_w_spatial, m_b_spatial, m_conv_c_w, m_conv_c_b, m_dt_bias, m_a_log, m_d_skip, m_norm_c_g, m_w_out, m_norm2_g, m_w_ff1, m_w_ff2, m_final_g, v_norm1_g, v_w_in, v_conv_a_w, v_conv_a_b, v_ln_a_g, v_ln_a_b, v_ln_b_g, v_ln_b_b, v_w_spatial, v_b_spatial, v_conv_c_w, v_conv_c_b, v_dt_bias, v_a_log, v_d_skip, v_norm_c_g, v_w_out, v_norm2_g, v_w_ff1, v_w_ff2, v_final_g):
    given = dict(locals())
    w = {k: given[k] for k in _WEIGHTS}
    m = {k: given["m_" + k] for k in _WEIGHTS}
    v = {k: given["v_" + k] for k in _WEIGHTS}
    depth = w_in.shape[0]
    nseq, seq, d = x.shape
    xi, yi, ci = _mesh_pos()
    q = 2 * xi + yi

    own = [_own_shards(w, li) for li in range(depth)]
    nb = len(_BIG)
    gathered = _gather_weights(own[0][:nb], own[0][nb:], "l0_gather")
    h = x.reshape(nseq * seq, d)
    layer_params, saved = [], []
    for li in range(depth):
        nxt = None
        if li + 1 < depth:
            srcs, _ = lax.optimization_barrier((own[li + 1], gathered))
            zones = [jax.ShapeDtypeStruct((N_CHIPS,) + a.shape, a.dtype) for a in srcs]
            nxt = _split_start("gather", srcs, zones, f"l{li + 1}_gather_start")
        layer_params.append(_layer_params(w, li, own[li], gathered, q))
        h, s = _layer_fwd(h, layer_params[li], seq, li, after=() if nxt is None else (nxt["token"],))
        saved.append(s)
        if nxt is not None:
            own[li + 1], gathered = _split_wait(nxt, h, f"l{li + 1}_gather_wait")
    loss, dx, d_final = _loss_head(h, final_g, loss_target.reshape(nseq * seq, d))

    grads = [None] * depth
    big_grads = [None] * depth
    pending = None
    for li in reversed(range(depth)):
        swaps = []

        def early_swap(g, li=li, swaps=swaps):
            early = _ffn_out_grads(g)
            swaps.append(_split_start("sibling", early, [_half_shape(a) for a in early], f"l{li}_rs_sib_start"))
            return (swaps[0]["token"],)

        after = () if pending is None else (pending[1]["token"],)
        if li == 0 and depth > 1:
            early_pack = _pack([grads[lj][k] for lj in range(1, depth) for k in _SMALL])
            early_small = _split_start("allgather", [early_pack], [jax.ShapeDtypeStruct((8,) + early_pack.shape, F32)],
                                       "small_early_start")
            after = after + (early_small["token"],)
        dx, grads[li] = _layer_bwd(dx, layer_params[li], saved[li], seq, li, after=after, on_ffn_grads=early_swap)
        if pending is not None:
            lj, scatter, chip_f32 = pending
            big_grads[lj] = _finish_reduce(chip_f32, _split_wait(scatter, dx, f"l{lj}_rs_scatter_wait")[1], lj, ci)
        early, early_from_sib = _split_wait(swaps[0], dx, f"l{li}_rs_sib_wait")
        chip_f32, chip_bf16 = _chip_sums(grads[li], early, early_from_sib, li, ci, q)
        lands = [jax.ShapeDtypeStruct((3,) + a.shape[1:], a.dtype) for a in chip_bf16]
        pending = (li, _split_start("scatter", chip_bf16, lands, f"l{li}_rs_scatter_start"), chip_f32)
    grad_out, delta_out, m_out, v_out = {}, {}, {}, {}

    small_shapes = [grads[0][k].shape for k in _SMALL]
    me = 2 * q + ci

    def sum8(*blocks):
        acc = blocks[0]
        for b in blocks[1:]:
            acc = acc + b
        return (acc,)

    def total_of(gathered, own, name):
        full = lax.dynamic_update_index_in_dim(gathered, own, me, axis=0)
        return _ew(sum8, [full] * 8, (F32,), name, leads=list(range(8)))[0]

    last_pack = _pack([grads[0][k] for k in _SMALL] + [d_final, loss.reshape(1)])
    last_total = total_of(_allgather8(last_pack, "small_allgather", after=(pending[1]["token"],)), last_pack, "small_sum")
    summed = _unpack(last_total, small_shapes + [d_final.shape, (1,)])
    tail = summed[len(_SMALL):]
    summed = summed[:len(_SMALL)]
    if depth > 1:
        (early_own,), (early_all,) = _split_wait(early_small, dx, "small_early_wait")
        summed += _unpack(total_of(early_all, early_own, "small_early_sum"), small_shapes * (depth - 1))
    summed += tail
    loss_total = summed[-1][0]
    small_grads = {k: jnp.stack([summed[li * len(_SMALL) + i] for li in range(depth)]) for i, k in enumerate(_SMALL)}
    small_grads["final_g"] = summed[-2]
    for k in _CONV_SHARDED:
        n_shard = w[k].shape[-1]
        small_grads[k] = lax.dynamic_slice_in_dim(small_grads[k], q * n_shard, n_shard, axis=2)
    names = _SMALL + ["final_g"]
    shapes = [w[k].shape for k in names]
    packed = [_pack([src[k] for k in names]) for src in (w, small_grads, m, v)]
    outs = _ew(_adam_fn, packed, (F32, F32, F32), "adam_small")
    for dst, o in zip((delta_out, m_out, v_out), outs):
        for k, a in zip(names, _unpack(o, shapes)):
            dst[k] = a
    for k in names:
        grad_out[k] = small_grads[k]

    lj, scatter, chip_f32 = pending
    big_grads[lj] = _finish_reduce(chip_f32, _split_wait(scatter, outs[0], f"l{lj}_rs_scatter_wait")[1], lj, ci)
    for i, k in enumerate(_BIG):
        grad_out[k] = jnp.stack([big_grads[li][i] for li in range(depth)])
        delta_out[k], m_out[k], v_out[k] = _adam(w[k], grad_out[k], m[k], v[k], "adam_" + k)

    return (loss_total, dx.reshape(nseq, seq, d), *[grad_out[k] for k in _WEIGHTS], *[delta_out[k] for k in _WEIGHTS],
            *[m_out[k] for k in _WEIGHTS], *[v_out[k] for k in _WEIGHTS])
```

```python
import functools
import math

import jax
import jax.numpy as jnp
from jax import lax
from jax.experimental import pallas as pl
from jax.experimental.pallas import tpu as pltpu

F32 = jnp.float32
BF16 = jnp.bfloat16
MESH = pl.DeviceIdType.MESH

D_MODEL = 1024
DEPTH = 4
HEAD_DIM = 64
A_WIDTH = 512
B_WIDTH = 512
C_WIDTH = 1024
C_HEADS = 16
CONV_A_K = 31
CONV_C_K = 4
CHUNK = 128
SSM_STATE = 128
D_CONV_C = 1536
D_MAIN = 4608
D_IN_PROJ = 4624
D_MIX = 2048
D_FF = 4096
EPS = 1e-5
NEG = -1e30
LANES = 128
CONV_PAD = 32
N_CHIPS = 4

ADAM_LR = 0.001
ADAM_B1 = 0.9
ADAM_B2 = 0.999
ADAM_EPS = 1e-08
ADAM_WD = 0.01
ADAM_STEP = 10

VMEM_LIMIT = 56 * 1024 * 1024
MATMUL_VMEM_BUDGET = 44 * 1024 * 1024

COL_AVAL, COL_AGATE, COL_BU, COL_BV, COL_Z, COL_XBC = 0, 4, 8, 12, 16, 24


def _cparams(*sem):
    return pltpu.CompilerParams(dimension_semantics=sem, vmem_limit_bytes=VMEM_LIMIT)


_DN = {"nn": (((1,), (0,)), ((), ())), "nt": (((1,), (1,)), ((), ())), "tn": (((0,), (0,)), ((), ()))}


def _dot_raw(a, b, mode):
    return lax.dot_general(a.astype(BF16), b.astype(BF16), _DN[mode], preferred_element_type=F32)


def _make_dot(mode):
    @jax.custom_vjp
    def f(a, b):
        return _dot_raw(a, b, mode)

    def fwd(a, b):
        return _dot_raw(a, b, mode), (a, b)

    def bwd(res, g):
        a, b = res
        if mode == "nn":
            return _dot_raw(g, b, "nt"), _dot_raw(a, g, "tn")
        if mode == "nt":
            return _dot_raw(g, b, "nn"), _dot_raw(g, a, "tn")
        return _dot_raw(b, g, "nt"), _dot_raw(a, g, "nn")

    f.defvjp(fwd, bwd)
    return f


_nn = _make_dot("nn")
_nt = _make_dot("nt")
_tn = _make_dot("tn")


def _xdot(a, e):
    return jnp.dot(a, e, precision=lax.Precision.HIGHEST, preferred_element_type=F32)


def _iota2(shape, dim):
    return lax.broadcasted_iota(jnp.int32, shape, dim)


def _gmean_impl(x):
    n = x.shape[-1]
    same = (_iota2((n, n), 0) < HEAD_DIM) == (_iota2((n, n), 1) < HEAD_DIM)
    p = jnp.where(same, 1.0 / HEAD_DIM, 0.0).astype(BF16)
    hi = x.astype(BF16)
    lo = (x - hi.astype(F32)).astype(BF16)
    dn = _DN["nn"]
    return (lax.dot_general(hi, p, dn, preferred_element_type=F32)
            + lax.dot_general(lo, p, dn, preferred_element_type=F32))


@jax.custom_vjp
def _gmean(x):
    return _gmean_impl(x)


_gmean.defvjp(lambda x: (_gmean_impl(x), None), lambda _, g: (_gmean_impl(g),))


def _sigmoid(x):
    return 1.0 / (1.0 + jnp.exp(-x))


def _silu(x):
    return x * _sigmoid(x)


def _gelu(x):
    return 0.5 * x * (1.0 + lax.erf(x * 0.7071067811865476))


def _softplus(x):
    return jnp.maximum(x, 0.0) + jnp.log(1.0 + jnp.exp(-jnp.abs(x)))


def _rms(x, g):
    return x * lax.rsqrt(jnp.mean(x * x, axis=-1, keepdims=True) + EPS) * g


def _ln64(x, g, b):
    mu = _gmean(x)
    xc = x - mu
    var = _gmean(xc * xc)
    return xc * lax.rsqrt(var + EPS) * g + b


def _lane_lt64(shape):
    return _iota2(shape, 1) < HEAD_DIM


def _pick(n, pref):
    for t in pref:
        if n % t == 0:
            return t
    return n


_UNREAD = pl.BlockSpec(memory_space=pl.ANY)


def _matmul_tiles(m, n_unit, k, a_item, b_item, out_bytes):
    best = None
    for tm in (1024, 512, 256, 128):
        for tn in (1536, 1024, 768, 512, 256, 128):
            if m % tm or n_unit % tn:
                continue
            need = 2 * k * (tm * a_item + tn * b_item) + 2 * tm * tn * out_bytes
            if need <= MATMUL_VMEM_BUDGET and (best is None or tm * tn > best[0] * best[1]):
                best = (tm, tn)
    assert best is not None, (m, n_unit, k)
    return best


def _matmul(a, b, *, mode, name, add=None, epilogue=None, extra=None, out_dtypes=(F32,), after=(), b_chips=False,
            out_chips=False):
    sh = b.shape[-1] if b_chips else None
    if mode == "nn":
        (m, k), n = a.shape, (N_CHIPS * sh if b_chips else b.shape[1])
    elif mode == "nt":
        (m, k), n = a.shape, b.shape[-2]
    else:
        (k, m), n = a.shape, b.shape[1]
    osh = n // N_CHIPS if out_chips else None
    out_bytes = sum(jnp.dtype(dt).itemsize for dt in out_dtypes) + (0 if add is None else add.dtype.itemsize) \
        + (0 if extra is None else extra.dtype.itemsize)
    tm, tn = _matmul_tiles(m, sh if (b_chips and mode == "nn") else (osh or n), k, a.dtype.itemsize, b.dtype.itemsize,
                           out_bytes)
    a_spec = pl.BlockSpec((k, tm), lambda i, j: (0, i)) if mode == "tn" else pl.BlockSpec((tm, k), lambda i, j: (i, 0))
    if b_chips and mode == "nn":
        per = sh // tn
        b_spec = pl.BlockSpec((None, k, tn), lambda i, j: (j // per, 0, j % per))
    elif b_chips:
        b_spec = pl.BlockSpec((N_CHIPS, tn, sh), lambda i, j: (0, j, 0))
    elif mode == "nt":
        b_spec = pl.BlockSpec((tn, k), lambda i, j: (j, 0))
    else:
        b_spec = pl.BlockSpec((k, tn), lambda i, j: (0, j))
    if out_chips:
        o_per = osh // tn
        o_spec = pl.BlockSpec((None, tm, tn), lambda i, j: (j // o_per, i, j % o_per))
        out_shape = [jax.ShapeDtypeStruct((N_CHIPS, m, osh), dt) for dt in out_dtypes]
    else:
        o_spec = pl.BlockSpec((tm, tn), lambda i, j: (i, j))
        out_shape = [jax.ShapeDtypeStruct((m, n), dt) for dt in out_dtypes]
    ins = [a, b]
    in_specs = [a_spec, b_spec]
    if add is not None:
        ins.append(add)
        in_specs.append(o_spec)
    if extra is not None:
        ins.append(extra)
        in_specs.append(o_spec)
    ins += list(after)
    in_specs += [_UNREAD] * len(after)
    n_out = len(out_dtypes)

    def body(*refs):
        a_ref, b_ref = refs[0], refs[1]
        pos = 2
        add_ref = ex_ref = None
        if add is not None:
            add_ref = refs[pos]
            pos += 1
        if extra is not None:
            ex_ref = refs[pos]
            pos += 1
        pos += len(after)
        if b_chips and mode == "nt":
            acc = _dot_raw(a_ref[:, pl.ds(0, sh)], b_ref[0], mode)
            for chip in range(1, N_CHIPS):
                acc = acc + _dot_raw(a_ref[:, pl.ds(chip * sh, sh)], b_ref[chip], mode)
        else:
            acc = _dot_raw(a_ref[...], b_ref[...], mode)
        if add_ref is not None:
            acc = acc + add_ref[...].astype(F32)
        outs = (acc,) if epilogue is None else epilogue(acc, None if ex_ref is None else ex_ref[...])
        for o_ref, o in zip(refs[pos:pos + n_out], outs):
            o_ref[...] = o.astype(o_ref.dtype)

    res = pl.pallas_call(
        body, name=name, grid=(m // tm, n // tn), in_specs=in_specs, out_specs=[o_spec] * n_out, out_shape=out_shape,
        compiler_params=_cparams("parallel", "parallel"),
    )(*ins)
    return res[0] if n_out == 1 else res


def _matmul_ksplit(a, b, *, mode, name, add=None, epilogue=None, extra=None, out_dtypes=(F32,), after=(), b_chips=False,
                   out_chips=False):
    sh = b.shape[-1] if b_chips else None
    if mode == "nn":
        (m, k), n = a.shape, (N_CHIPS * sh if b_chips else b.shape[1])
    elif mode == "nt":
        (m, k), n = a.shape, b.shape[-2]
    else:
        (k, m), n = a.shape, b.shape[1]
    osh = n // N_CHIPS if out_chips else None
    tm = _pick(m, (1024, 512, 256, 128))
    tn = _pick(sh if (b_chips and mode == "nn") else (osh or n), (1536, 1024, 512, 256, 128))
    out_bytes = sum(jnp.dtype(dt).itemsize for dt in out_dtypes) + (0 if add is None else add.dtype.itemsize) \
        + (0 if extra is None else extra.dtype.itemsize)
    k_dim = sh if (b_chips and mode == "nt") else k
    for tk in (4096, 2048, 1536, 1024, 512, 256, 128):
        if k_dim % tk:
            continue
        acc_bytes = 0 if tk == k else 4 * tm * tn
        need = 2 * tk * (tm * a.dtype.itemsize + tn * b.dtype.itemsize) + 2 * tm * tn * out_bytes + acc_bytes
        if need <= MATMUL_VMEM_BUDGET:
            break
    nk = k // tk
    a_spec = {"nn": pl.BlockSpec((tm, tk), lambda i, j, kk: (i, kk)),
              "nt": pl.BlockSpec((tm, tk), lambda i, j, kk: (i, kk)),
              "tn": pl.BlockSpec((tk, tm), lambda i, j, kk: (kk, i))}[mode]
    if b_chips:
        per = sh // (tn if mode == "nn" else tk)
        b_spec = {"nn": pl.BlockSpec((None, tk, tn), lambda i, j, kk: (j // per, kk, j % per)),
                  "nt": pl.BlockSpec((None, tn, tk), lambda i, j, kk: (kk // per, j, kk % per))}[mode]
    else:
        b_spec = {"nn": pl.BlockSpec((tk, tn), lambda i, j, kk: (kk, j)),
                  "nt": pl.BlockSpec((tn, tk), lambda i, j, kk: (j, kk)),
                  "tn": pl.BlockSpec((tk, tn), lambda i, j, kk: (kk, j))}[mode]
    if out_chips:
        o_per = osh // tn
        o_spec = pl.BlockSpec((None, tm, tn), lambda i, j, kk: (j // o_per, i, j % o_per))
        out_shape = [jax.ShapeDtypeStruct((N_CHIPS, m, osh), dt) for dt in out_dtypes]
    else:
        o_spec = pl.BlockSpec((tm, tn), lambda i, j, kk: (i, j))
        out_shape = [jax.ShapeDtypeStruct((m, n), dt) for dt in out_dtypes]
    ins = [a, b]
    in_specs = [a_spec, b_spec]
    if add is not None:
        ins.append(add)
        in_specs.append(o_spec)
    if extra is not None:
        ins.append(extra)
        in_specs.append(o_spec)
    ins += list(after)
    in_specs += [_UNREAD] * len(after)
    n_out = len(out_dtypes)

    def body(*refs):
        a_ref, b_ref = refs[0], refs[1]
        pos = 2
        add_ref = ex_ref = None
        if add is not None:
            add_ref = refs[pos]
            pos += 1
        if extra is not None:
            ex_ref = refs[pos]
            pos += 1
        pos += len(after)
        o_refs = refs[pos:pos + n_out]

        def finish(acc):
            if add_ref is not None:
                acc = acc + add_ref[...].astype(F32)
            outs = (acc,) if epilogue is None else epilogue(acc, None if ex_ref is None else ex_ref[...])
            for o_ref, o in zip(o_refs, outs):
                o_ref[...] = o.astype(o_ref.dtype)

        part = _dot_raw(a_ref[...], b_ref[...], mode)
        if nk == 1:
            finish(part)
            return
        acc_ref = refs[pos + n_out]
        kk = pl.program_id(2)

        @pl.when(kk == 0)
        def _():
            acc_ref[...] = part

        @pl.when(jnp.logical_and(kk > 0, kk < nk - 1))
        def _():
            acc_ref[...] += part

        @pl.when(kk == nk - 1)
        def _():
            finish(acc_ref[...] + part)

    res = pl.pallas_call(
        body, name=name, grid=(m // tm, n // tn, nk),
        in_specs=in_specs, out_specs=[o_spec] * n_out, out_shape=out_shape,
        scratch_shapes=[pltpu.VMEM((tm, tn), F32)] if nk > 1 else [],
        compiler_params=_cparams("parallel", "parallel", "arbitrary"),
    )(*ins)
    return res[0] if n_out == 1 else res


def _relu2_epilogue(acc, _):
    r = jnp.maximum(acc, 0.0)
    return acc, r * r


def _relu2_bwd_epilogue(acc, u):
    return (acc * (2.0 * jnp.maximum(u, 0.0)),)


def _row_tile(t):
    return _pick(t, (512, 256, 128))


def _rms_fwd(x, g, name, after=()):
    t, d = x.shape
    tm = _row_tile(t)

    def body(x_ref, g_ref, *rest):
        o_ref = rest[-1]
        o_ref[...] = _rms(x_ref[...], g_ref[...]).astype(BF16)

    return pl.pallas_call(
        body, name=name, grid=(t // tm,),
        in_specs=[pl.BlockSpec((tm, d), lambda i: (i, 0)), pl.BlockSpec((1, d), lambda i: (0, 0))] + [_UNREAD] * len(after),
        out_specs=pl.BlockSpec((tm, d), lambda i: (i, 0)),
        out_shape=jax.ShapeDtypeStruct((t, d), BF16),
        compiler_params=_cparams("parallel"),
    )(x, g.reshape(1, d), *after)


def _rms_bwd(x, g, dh, dres, name):
    t, d = x.shape
    tm = _row_tile(t)

    def body(x_ref, g_ref, dh_ref, dres_ref, dx_ref, dg_ref):
        @pl.when(pl.program_id(0) == 0)
        def _():
            dg_ref[...] = jnp.zeros_like(dg_ref)

        _, vjp = jax.vjp(_rms, x_ref[...], g_ref[...])
        dx, dg = vjp(dh_ref[...].astype(F32))
        dx_ref[...] = dx + dres_ref[...]
        dg_ref[...] += dg

    row = pl.BlockSpec((tm, d), lambda i: (i, 0))
    vec = pl.BlockSpec((1, d), lambda i: (0, 0))
    dx, dg = pl.pallas_call(
        body, name=name, grid=(t // tm,),
        in_specs=[row, vec, row, row], out_specs=[row, vec],
        out_shape=[jax.ShapeDtypeStruct((t, d), F32), jax.ShapeDtypeStruct((1, d), F32)],
        compiler_params=_cparams("arbitrary"),
    )(x, g.reshape(1, d), dh, dres)
    return dx, dg.reshape(d)


def _loss_head(x, g, target):
    t, d = x.shape
    tm = _row_tile(t)

    def loss_fn(xv, gv, tv):
        err = _rms(xv, gv) - tv
        return 0.5 * jnp.sum(jnp.mean(err * err, axis=-1, keepdims=True))

    def body(x_ref, g_ref, t_ref, loss_ref, dx_ref, dg_ref):
        @pl.when(pl.program_id(0) == 0)
        def _():
            dg_ref[...] = jnp.zeros_like(dg_ref)
            loss_ref[...] = jnp.zeros_like(loss_ref)

        tv = t_ref[...]
        val, vjp = jax.vjp(lambda xv, gv: loss_fn(xv, gv, tv), x_ref[...], g_ref[...])
        dx, dg = vjp(jnp.ones((), F32))
        dx_ref[...] = dx
        dg_ref[...] += dg
        loss_ref[...] += jnp.full(loss_ref.shape, val, F32)

    row = pl.BlockSpec((tm, d), lambda i: (i, 0))
    vec = pl.BlockSpec((1, d), lambda i: (0, 0))
    loss, dx, dg = pl.pallas_call(
        body, name="loss_head", grid=(t // tm,),
        in_specs=[row, vec, row], out_specs=[pl.BlockSpec((1, LANES), lambda i: (0, 0)), row, vec],
        out_shape=[jax.ShapeDtypeStruct((1, LANES), F32), jax.ShapeDtypeStruct((t, d), F32),
                   jax.ShapeDtypeStruct((1, d), F32)],
        compiler_params=_cparams("arbitrary"),
    )(x, g.reshape(1, d), target)
    return loss[0, 0], dx, dg.reshape(d)


def _pre_glu(val, gate):
    return val * _sigmoid(gate)


def _pre_id(x):
    return x


def _post_lnsilu(c, g, b):
    return _silu(_ln64(c, g, b))


def _post_silu(c):
    return _silu(c)


def _conv_cfg(kind):
    if kind == "a":
        return dict(k=CONV_A_K, pre=_pre_glu, post=_post_lnsilu, n_in=2, n_par=2, nblk=A_WIDTH // LANES,
                    cols=(COL_AVAL, COL_AGATE))
    return dict(k=CONV_C_K, pre=_pre_id, post=_post_silu, n_in=1, n_par=0, nblk=D_CONV_C // LANES,
                cols=(COL_XBC,))


def _conv_fwd(kind, proj, w, bias, params, seq, name, out_dtype=F32, keep_conv=False):
    cfg = _conv_cfg(kind)
    kt, pre, post, n_in = cfg["k"], cfg["pre"], cfg["post"], cfg["n_in"]
    t = proj.shape[0]
    nseq = t // seq
    c = cfg["nblk"] * LANES
    rt = min(256, seq)
    nrt = seq // rt
    off0 = CONV_PAD - (kt - 1)

    def body(*refs):
        in_refs = refs[:n_in]
        w_ref, b_ref = refs[n_in], refs[n_in + 1]
        par_refs = refs[n_in + 2:n_in + 2 + cfg["n_par"]]
        out_refs = refs[n_in + 2 + cfg["n_par"]:-1]
        hpad = refs[-1]
        hpad[pl.ds(0, CONV_PAD), :] = jnp.zeros((CONV_PAD, LANES), F32)
        for r in range(nrt):
            hpad[pl.ds(CONV_PAD + r * rt, rt), :] = pre(*[x[pl.ds(r * rt, rt), :] for x in in_refs])
        pars = [p[...] for p in par_refs]
        for r in range(nrt):
            acc = jnp.broadcast_to(b_ref[...], (rt, LANES))
            for k in range(kt):
                acc = acc + w_ref[pl.ds(k, 1), :] * hpad[pl.ds(off0 + k + r * rt, rt), :]
            out_refs[0][pl.ds(r * rt, rt), :] = post(acc, *pars).astype(out_dtype)
            if keep_conv:
                out_refs[1][pl.ds(r * rt, rt), :] = acc

    in_specs = [pl.BlockSpec((seq, LANES), functools.partial(lambda s, j, col: (s, col + j), col=col))
                for col in cfg["cols"]]
    vec = pl.BlockSpec((1, LANES), lambda s, j: (0, j))
    in_specs += [pl.BlockSpec((CONV_PAD, LANES), lambda s, j: (0, j)), vec] + [vec] * cfg["n_par"]
    blk = pl.BlockSpec((seq, LANES), lambda s, j: (s, j))
    res = pl.pallas_call(
        body, name=name, grid=(nseq, cfg["nblk"]),
        in_specs=in_specs, out_specs=[blk, blk] if keep_conv else [blk],
        out_shape=[jax.ShapeDtypeStruct((t, c), out_dtype)] + ([jax.ShapeDtypeStruct((t, c), F32)] if keep_conv else []),
        scratch_shapes=[pltpu.VMEM((seq + CONV_PAD, LANES), F32)],
        compiler_params=_cparams("parallel", "parallel"),
    )(*([proj] * n_in), w, bias, *params)
    return tuple(res) if keep_conv else res[0]


def _conv_bwd(kind, proj, w, bias, params, dy, seq, name, dy_col=0, conv_out=None):
    kept = conv_out is not None
    cfg = _conv_cfg(kind)
    kt, pre, post, n_in, n_par = cfg["k"], cfg["pre"], cfg["post"], cfg["n_in"], cfg["n_par"]
    t = proj.shape[0]
    nseq = t // seq
    c = cfg["nblk"] * LANES
    rt = min(256, seq)
    nrt = seq // rt
    off0 = CONV_PAD - (kt - 1)

    def body(*refs):
        in_refs = refs[:n_in]
        w_ref, b_ref = refs[n_in], refs[n_in + 1]
        par_refs = refs[n_in + 2:n_in + 2 + n_par]
        pos = n_in + 2 + n_par
        dy_ref = refs[pos]
        if kept:
            pos += 1
            conv_ref = refs[pos]
        din_refs = refs[pos + 1:pos + 1 + n_in]
        dw_ref, db_ref = refs[pos + 1 + n_in], refs[pos + 2 + n_in]
        dpar_refs = refs[pos + 3 + n_in:pos + 3 + n_in + n_par]
        hpad, dcpad = refs[pos + 3 + n_in + n_par:]

        @pl.when(pl.program_id(1) == 0)
        def _():
            dw_ref[...] = jnp.zeros_like(dw_ref)
            db_ref[...] = jnp.zeros_like(db_ref)
            for r in dpar_refs:
                r[...] = jnp.zeros_like(r)

        hpad[pl.ds(0, CONV_PAD), :] = jnp.zeros((CONV_PAD, LANES), F32)
        dcpad[pl.ds(seq, CONV_PAD), :] = jnp.zeros((CONV_PAD, LANES), F32)
        for r in range(nrt):
            hpad[pl.ds(CONV_PAD + r * rt, rt), :] = pre(*[x[pl.ds(r * rt, rt), :] for x in in_refs])
        pars = [p[...] for p in par_refs]
        for r in range(nrt):
            if kept:
                acc = conv_ref[pl.ds(r * rt, rt), :]
            else:
                acc = jnp.broadcast_to(b_ref[...], (rt, LANES))
                for k in range(kt):
                    acc = acc + w_ref[pl.ds(k, 1), :] * hpad[pl.ds(off0 + k + r * rt, rt), :]
            _, vjp = jax.vjp(post, acc, *pars)
            grads = vjp(dy_ref[pl.ds(r * rt, rt), :])
            dcpad[pl.ds(r * rt, rt), :] = grads[0]
            db_ref[...] += jnp.sum(grads[0], axis=0, keepdims=True)
            for ref, gpar in zip(dpar_refs, grads[1:]):
                ref[...] += gpar
        for r in range(nrt):
            dh = jnp.zeros((rt, LANES), F32)
            for k in range(kt):
                dh = dh + w_ref[pl.ds(k, 1), :] * dcpad[pl.ds(r * rt + kt - 1 - k, rt), :]
            _, vjp = jax.vjp(pre, *[x[pl.ds(r * rt, rt), :] for x in in_refs])
            for ref, gin in zip(din_refs, vjp(dh)):
                ref[pl.ds(r * rt, rt), :] = gin.astype(ref.dtype)
        for k in range(kt):
            s = jnp.zeros((1, LANES), F32)
            for r in range(nrt):
                s = s + jnp.sum(dcpad[pl.ds(r * rt, rt), :] * hpad[pl.ds(off0 + k + r * rt, rt), :],
                                axis=0, keepdims=True)
            dw_ref[pl.ds(k, 1), :] += s

    in_specs = [pl.BlockSpec((seq, LANES), functools.partial(lambda j, s, col: (s, col + j), col=col))
                for col in cfg["cols"]]
    vec = pl.BlockSpec((1, LANES), lambda j, s: (0, j))
    wspec = pl.BlockSpec((CONV_PAD, LANES), lambda j, s: (0, j))
    blk = pl.BlockSpec((seq, LANES), lambda j, s: (s, j))
    in_specs += [wspec, vec] + [vec] * n_par + [pl.BlockSpec((seq, LANES), lambda j, s: (s, dy_col + j))]
    in_specs += [blk] if kept else []
    out_specs = [blk] * n_in + [wspec, vec] + [vec] * n_par
    out_shape = ([jax.ShapeDtypeStruct((t, c), BF16)] * n_in
                 + [jax.ShapeDtypeStruct((CONV_PAD, c), F32), jax.ShapeDtypeStruct((1, c), F32)]
                 + [jax.ShapeDtypeStruct((1, c), F32)] * n_par)
    res = pl.pallas_call(
        body, name=name, grid=(cfg["nblk"], nseq),
        in_specs=in_specs, out_specs=out_specs, out_shape=out_shape,
        scratch_shapes=[pltpu.VMEM((seq + CONV_PAD, LANES), F32), pltpu.VMEM((seq + CONV_PAD, LANES), F32)],
        compiler_params=_cparams("parallel", "arbitrary"),
    )(*([proj] * n_in), w, bias, *params, dy, *([conv_out] if kept else []))
    return res[:n_in], res[n_in], res[n_in + 1], res[n_in + 2:]


def _gmlp_chunk(bu, bv, g, b, w0, w1, b0row, b1row):
    u = _gelu(bu)
    vn = _ln64(_gelu(bv), g, b)
    tri = _iota2((CHUNK, CHUNK), 0) >= _iota2((CHUNK, CHUNK), 1)
    m0 = _nn(jnp.where(tri, w0, 0.0), vn) + jnp.broadcast_to(b0row, (CHUNK, CHUNK)).T
    m1 = _nn(jnp.where(tri, w1, 0.0), vn) + jnp.broadcast_to(b1row, (CHUNK, CHUNK)).T
    return u * jnp.where(_lane_lt64((CHUNK, LANES)), m0, m1)


def _gmlp_specs(tm, order):
    def im(f):
        return lambda *ids: f(*order(*ids))
    return dict(
        bu=pl.BlockSpec((tm, LANES), im(lambda j, r: (r, COL_BU + j))),
        bv=pl.BlockSpec((tm, LANES), im(lambda j, r: (r, COL_BV + j))),
        vec=pl.BlockSpec((1, LANES), im(lambda j, r: (0, j))),
        ws=pl.BlockSpec((2, CHUNK, CHUNK), im(lambda j, r: (j, 0, 0))),
        bs=pl.BlockSpec((None, 2, CHUNK), im(lambda j, r: (j, 0, 0))),
        blk=pl.BlockSpec((tm, LANES), im(lambda j, r: (r, j))),
    )


def _gmlp_fwd(proj, ln_g, ln_b, w_s, b_s, name):
    t = proj.shape[0]
    tm = _row_tile(t)
    nch = tm // CHUNK
    sp = _gmlp_specs(tm, lambda r, j: (j, r))

    def body(bu_ref, bv_ref, g_ref, b_ref, ws_ref, bs_ref, o_ref):
        for ci in range(nch):
            rows = pl.ds(ci * CHUNK, CHUNK)
            o_ref[rows, :] = _gmlp_chunk(bu_ref[rows, :], bv_ref[rows, :], g_ref[...], b_ref[...], ws_ref[0], ws_ref[1],
                                         bs_ref[pl.ds(0, 1), :], bs_ref[pl.ds(1, 1), :]).astype(BF16)

    return pl.pallas_call(
        body, name=name, grid=(t // tm, B_WIDTH // LANES),
        in_specs=[sp["bu"], sp["bv"], sp["vec"], sp["vec"], sp["ws"], sp["bs"]],
        out_specs=sp["blk"], out_shape=jax.ShapeDtypeStruct((t, B_WIDTH), BF16),
        compiler_params=_cparams("parallel", "parallel"),
    )(proj, proj, ln_g, ln_b, w_s, b_s.reshape(B_WIDTH // LANES, 2, CHUNK))


def _gmlp_bwd(proj, ln_g, ln_b, w_s, b_s, dy, name, dy_col=0):
    t = proj.shape[0]
    tm = _row_tile(t)
    nch = tm // CHUNK
    sp = _gmlp_specs(tm, lambda j, r: (j, r))
    dy_spec = pl.BlockSpec((tm, LANES), lambda j, r: (r, dy_col + j))

    def body(bu_ref, bv_ref, g_ref, b_ref, ws_ref, bs_ref, dy_ref, dbu_ref, dbv_ref, dg_ref, db_ref, dws_ref, dbs_ref):
        @pl.when(pl.program_id(1) == 0)
        def _():
            for r in (dg_ref, db_ref, dws_ref, dbs_ref):
                r[...] = jnp.zeros_like(r)

        for ci in range(nch):
            rows = pl.ds(ci * CHUNK, CHUNK)
            _, vjp = jax.vjp(_gmlp_chunk, bu_ref[rows, :], bv_ref[rows, :], g_ref[...], b_ref[...],
                             ws_ref[0], ws_ref[1], bs_ref[pl.ds(0, 1), :], bs_ref[pl.ds(1, 1), :])
            dbu, dbv, dg, db, dw0, dw1, db0, db1 = vjp(dy_ref[rows, :])
            dbu_ref[rows, :] = dbu.astype(BF16)
            dbv_ref[rows, :] = dbv.astype(BF16)
            dg_ref[...] += dg
            db_ref[...] += db
            dws_ref[0] += dw0
            dws_ref[1] += dw1
            dbs_ref[pl.ds(0, 1), :] += db0
            dbs_ref[pl.ds(1, 1), :] += db1

    nh = B_WIDTH // LANES
    res = pl.pallas_call(
        body, name=name, grid=(nh, t // tm),
        in_specs=[sp["bu"], sp["bv"], sp["vec"], sp["vec"], sp["ws"], sp["bs"], dy_spec],
        out_specs=[sp["blk"], sp["blk"], sp["vec"], sp["vec"], sp["ws"], sp["bs"]],
        out_shape=[jax.ShapeDtypeStruct((t, B_WIDTH), BF16), jax.ShapeDtypeStruct((t, B_WIDTH), BF16),
                   jax.ShapeDtypeStruct((1, B_WIDTH), F32), jax.ShapeDtypeStruct((1, B_WIDTH), F32),
                   jax.ShapeDtypeStruct(w_s.shape, F32), jax.ShapeDtypeStruct((nh, 2, CHUNK), F32)],
        compiler_params=_cparams("parallel", "arbitrary"),
    )(proj, proj, ln_g, ln_b, w_s, b_s.reshape(nh, 2, CHUNK), dy)
    dbu, dbv, dg, db, dws, dbs = res
    return dbu, dbv, dg, db, dws, dbs.reshape(b_s.shape)


def _tri_apply(a, lower):
    l = a.shape[0]
    r, c = _iota2((l, l), 0), _iota2((l, l), 1)
    t = jnp.where((r >= c) if lower else (r <= c), 1.0, 0.0).astype(BF16)
    hi = a.astype(BF16)
    r1 = a - hi.astype(F32)
    mid = r1.astype(BF16)
    lo = (r1 - mid.astype(F32)).astype(BF16)
    dn = _DN["nn"]
    return (lax.dot_general(t, hi, dn, preferred_element_type=F32) + lax.dot_general(t, mid, dn, preferred_element_type=F32)
            + lax.dot_general(t, lo, dn, preferred_element_type=F32))


@jax.custom_vjp
def _cumsum_rows(a):
    return _tri_apply(a, True)


_cumsum_rows.defvjp(lambda a: (_tri_apply(a, True), None), lambda _, g: (_tri_apply(g, False),))

SSD_GROUP_HEADS = 8
SSD_GROUP_PAIRS = 4


def _ssd_group(x0, x1, x2, x3, dt_raw, bias, alog, bm, cm, p0, p1, p2, p3):
    xs, prevs = (x0, x1, x2, x3), (p0, p1, p2, p3)
    dt = _softplus(dt_raw + bias)
    a = dt * (-jnp.exp(alog))
    acs = _cumsum_rows(a)
    alast = jnp.sum(a, axis=0, keepdims=True)
    dt_t, acs_t = dt.T, acs.T
    cb = _nt(cm, bm)
    tri = _iota2((CHUNK, CHUNK), 0) >= _iota2((CHUNK, CHUNK), 1)
    lane = _iota2((CHUNK, LANES), 1)
    sub = _iota2((LANES, CHUNK), 0)
    lane1 = _iota2((1, LANES), 1)

    def column(v, i):
        return jnp.broadcast_to(jnp.sum(jnp.where(lane == i, v, 0.0), axis=1, keepdims=True), (CHUNK, LANES))

    def row(vt, i):
        return jnp.broadcast_to(jnp.sum(jnp.where(sub == i, vt, 0.0), axis=0, keepdims=True), (CHUNK, CHUNK))

    heads = []
    for i in range(SSD_GROUP_HEADS):
        col_a = column(acs, i)
        al = jnp.sum(jnp.where(lane1 == i, alast, 0.0), axis=1, keepdims=True)
        m = cb * jnp.exp(jnp.where(tri, col_a - row(acs_t, i), NEG)) * row(dt_t, i)
        heads.append((m, jnp.exp(col_a), column(dt, i) * jnp.exp(al - col_a), jnp.exp(al)))
    lo_lanes = _lane_lt64((CHUNK, LANES))
    lo_rows = _iota2((LANES, SSM_STATE), 0) < HEAD_DIM
    ys, news = [], []
    for j in range(SSD_GROUP_PAIRS):
        (m0, ea0, w0, cd0), (m1, ea1, w1, cd1) = heads[2 * j], heads[2 * j + 1]
        x, prev = xs[j], prevs[j]
        ydiag = jnp.where(lo_lanes, _nn(m0, x), _nn(m1, x))
        yoff = jnp.where(lo_lanes, _nt(cm * ea0, prev), _nt(cm * ea1, prev))
        states = jnp.where(lo_rows, _tn(x, bm * w0), _tn(x, bm * w1))
        ys.append(ydiag + yoff)
        news.append(prev * jnp.where(lo_rows, cd0, cd1) + states)
    return tuple(ys) + tuple(news)


SSD_GROUPS = 2


def _ssd2_specs(seq, rev):
    ncs = seq // CHUNK
    wide = SSD_GROUPS * LANES

    def row(s, c):
        return s * ncs + (ncs - 1 - c if rev else c)

    return dict(
        x=pl.BlockSpec((CHUNK, C_WIDTH), lambda s, c: (row(s, c), 0)),
        dt=pl.BlockSpec((CHUNK, wide), lambda s, c: (row(s, c), 0)),
        vec=pl.BlockSpec((1, wide), lambda s, c: (0, 0)),
        bm=pl.BlockSpec((CHUNK, wide), lambda s, c: (row(s, c), C_WIDTH // wide)),
        cm=pl.BlockSpec((CHUNK, wide), lambda s, c: (row(s, c), C_WIDTH // wide + 1)),
        st=pl.BlockSpec((None, C_WIDTH // LANES, LANES, SSM_STATE), lambda s, c: (row(s, c), 0, 0, 0)),
        ncs=ncs,
    )


def _lane_blocks(ref, grp):
    return [ref[:, pl.ds((grp * SSD_GROUP_PAIRS + j) * LANES, LANES)] for j in range(SSD_GROUP_PAIRS)]


def _group_block(ref, grp):
    return ref[:, pl.ds(grp * LANES, LANES)]


def _ssd2_fwd(xbc_act, dt_raw, dt_bias, a_log, seq, name):
    t = xbc_act.shape[0]
    sp = _ssd2_specs(seq, False)

    npair = SSD_GROUP_PAIRS

    def body(x_ref, dt_ref, bias_ref, alog_ref, bm_ref, cm_ref, y_ref, prev_ref, state):
        @pl.when(pl.program_id(1) == 0)
        def _():
            state[...] = jnp.zeros_like(state)

        for grp in range(SSD_GROUPS):
            prevs = [state[grp * npair + j] for j in range(npair)]
            for j in range(npair):
                prev_ref[grp * npair + j] = prevs[j]
            res = _ssd_group(*_lane_blocks(x_ref, grp), _group_block(dt_ref, grp), _group_block(bias_ref, grp),
                             _group_block(alog_ref, grp), _group_block(bm_ref, grp), _group_block(cm_ref, grp), *prevs)
            for j in range(npair):
                y_ref[:, pl.ds((grp * npair + j) * LANES, LANES)] = res[j]
                state[grp * npair + j] = res[npair + j]

    return pl.pallas_call(
        body, name=name, grid=(t // seq, sp["ncs"]),
        in_specs=[sp["x"], sp["dt"], sp["vec"], sp["vec"], sp["bm"], sp["cm"]],
        out_specs=[sp["x"], sp["st"]],
        out_shape=[jax.ShapeDtypeStruct((t, C_WIDTH), F32),
                   jax.ShapeDtypeStruct((t // CHUNK, C_WIDTH // LANES, LANES, SSM_STATE), F32)],
        scratch_shapes=[pltpu.VMEM((C_WIDTH // LANES, LANES, SSM_STATE), F32)],
        compiler_params=_cparams("parallel", "arbitrary"),
    )(xbc_act, dt_raw, dt_bias, a_log, xbc_act, xbc_act)


def _ssd2_bwd(xbc_act, dt_raw, dt_bias, a_log, prev_saved, dy, seq, name):
    t = xbc_act.shape[0]
    sp = _ssd2_specs(seq, True)
    npair = SSD_GROUP_PAIRS

    def body(x_ref, dt_ref, bias_ref, alog_ref, bm_ref, cm_ref, prev_ref, dy_ref,
             dx_ref, ddt_ref, dbias_ref, dalog_ref, dbm_ref, dcm_ref, dstate):
        @pl.when(pl.program_id(1) == 0)
        def _():
            dstate[...] = jnp.zeros_like(dstate)

        @pl.when(jnp.logical_and(pl.program_id(0) == 0, pl.program_id(1) == 0))
        def _():
            dbias_ref[...] = jnp.zeros_like(dbias_ref)
            dalog_ref[...] = jnp.zeros_like(dalog_ref)

        for grp in range(SSD_GROUPS):
            lanes = pl.ds(grp * LANES, LANES)
            _, vjp = jax.vjp(_ssd_group, *_lane_blocks(x_ref, grp), _group_block(dt_ref, grp), _group_block(bias_ref, grp),
                             _group_block(alog_ref, grp), _group_block(bm_ref, grp), _group_block(cm_ref, grp),
                             *[prev_ref[grp * npair + j] for j in range(npair)])
            grads = vjp(tuple(_lane_blocks(dy_ref, grp)) + tuple(dstate[grp * npair + j] for j in range(npair)))
            for j in range(npair):
                dx_ref[:, pl.ds((grp * npair + j) * LANES, LANES)] = grads[j]
                dstate[grp * npair + j] = grads[npair + 5 + j]
            ddt_ref[:, lanes] = grads[npair].astype(BF16)
            dbias_ref[:, lanes] += grads[npair + 1]
            dalog_ref[:, lanes] += grads[npair + 2]
            dbm_ref[:, lanes] = grads[npair + 3]
            dcm_ref[:, lanes] = grads[npair + 4]

    return pl.pallas_call(
        body, name=name, grid=(t // seq, sp["ncs"]),
        in_specs=[sp["x"], sp["dt"], sp["vec"], sp["vec"], sp["bm"], sp["cm"], sp["st"], sp["x"]],
        out_specs=[sp["x"], sp["dt"], sp["vec"], sp["vec"], sp["dt"], sp["dt"]],
        out_shape=[jax.ShapeDtypeStruct((t, C_WIDTH), F32), jax.ShapeDtypeStruct((t, 2 * LANES), BF16),
                   jax.ShapeDtypeStruct((1, 2 * LANES), F32), jax.ShapeDtypeStruct((1, 2 * LANES), F32),
                   jax.ShapeDtypeStruct((t, 2 * SSM_STATE), F32), jax.ShapeDtypeStruct((t, 2 * SSM_STATE), F32)],
        scratch_shapes=[pltpu.VMEM((C_WIDTH // LANES, LANES, SSM_STATE), F32)],
        compiler_params=_cparams("arbitrary", "arbitrary"),
    )(xbc_act, dt_raw, dt_bias, a_log, xbc_act, xbc_act, prev_saved, dy)


def _ssd2_assemble(dxs_ssd, dxs_skip, dbm, dcm, name):
    t = dxs_ssd.shape[0]
    tm = _row_tile(t)

    def body(a_ref, b_ref, dbm_ref, dcm_ref, o_ref):
        o_ref[:, pl.ds(0, C_WIDTH)] = a_ref[...] + b_ref[...]
        o_ref[:, pl.ds(C_WIDTH, 2 * SSM_STATE)] = dbm_ref[...]
        o_ref[:, pl.ds(C_WIDTH + 2 * SSM_STATE, 2 * SSM_STATE)] = dcm_ref[...]

    wide = pl.BlockSpec((tm, C_WIDTH), lambda i: (i, 0))
    narrow = pl.BlockSpec((tm, 2 * SSM_STATE), lambda i: (i, 0))
    return pl.pallas_call(
        body, name=name, grid=(t // tm,), in_specs=[wide, wide, narrow, narrow],
        out_specs=pl.BlockSpec((tm, D_CONV_C), lambda i: (i, 0)),
        out_shape=jax.ShapeDtypeStruct((t, D_CONV_C), F32),
        compiler_params=_cparams("parallel"),
    )(dxs_ssd, dxs_skip, dbm, dcm)


def _expand_mats():
    head = jnp.arange(LANES)[:, None]
    e64 = (head == (jnp.arange(C_WIDTH)[None, :] // HEAD_DIM)).astype(F32)
    e128 = (head == (jnp.arange(C_HEADS * LANES)[None, :] // LANES)).astype(F32)
    return e64, e128


def _ssd_prep_fn(dt_raw, dt_bias, a_log, e64, e128):
    dt = _softplus(dt_raw + dt_bias)
    a = dt * (-jnp.exp(a_log))
    incl = (_iota2((CHUNK, CHUNK), 0) >= _iota2((CHUNK, CHUNK), 1)).astype(F32)
    acs = _xdot(incl, a)
    alast = _xdot(jnp.ones((CHUNK, CHUNK), F32), a)
    return _xdot(dt, e64), _xdot(acs, e64), _xdot(alast, e64), _xdot(acs, e128)


def _ssd_prep_specs():
    blk = lambda w: pl.BlockSpec((CHUNK, w), lambda i: (i, 0))
    const = lambda r, w: pl.BlockSpec((r, w), lambda i: (0, 0))
    ins = [blk(LANES), const(1, LANES), const(1, LANES), const(LANES, C_WIDTH), const(LANES, C_HEADS * LANES)]
    outs = [blk(C_WIDTH), blk(C_WIDTH), blk(C_WIDTH), blk(C_HEADS * LANES)]
    return ins, outs


def _ssd_prep_fwd(dt_raw, dt_bias, a_log, name):
    t = dt_raw.shape[0]
    e64, e128 = _expand_mats()
    ins, outs = _ssd_prep_specs()

    def body(raw_ref, bias_ref, alog_ref, e64_ref, e128_ref, dt_ref, acs_ref, alast_ref, acs128_ref):
        res = _ssd_prep_fn(raw_ref[...], bias_ref[...], alog_ref[...], e64_ref[...], e128_ref[...])
        for ref, v in zip((dt_ref, acs_ref, alast_ref, acs128_ref), res):
            ref[...] = v

    return pl.pallas_call(
        body, name=name, grid=(t // CHUNK,), in_specs=ins, out_specs=outs,
        out_shape=[jax.ShapeDtypeStruct((t, C_WIDTH), F32)] * 3 + [jax.ShapeDtypeStruct((t, C_HEADS * LANES), F32)],
        compiler_params=_cparams("parallel"),
    )(dt_raw, dt_bias, a_log, e64, e128)


def _ssd_prep_bwd(dt_raw, dt_bias, a_log, d_dt, d_acs, d_alast, d_acs128, name):
    t = dt_raw.shape[0]
    e64, e128 = _expand_mats()
    ins, outs = _ssd_prep_specs()
    vec = pl.BlockSpec((1, LANES), lambda i: (0, 0))

    def body(raw_ref, bias_ref, alog_ref, e64_ref, e128_ref, g0, g1, g2, g3, draw_ref, dbias_ref, dalog_ref):
        @pl.when(pl.program_id(0) == 0)
        def _():
            dbias_ref[...] = jnp.zeros_like(dbias_ref)
            dalog_ref[...] = jnp.zeros_like(dalog_ref)

        e64v, e128v = e64_ref[...], e128_ref[...]
        _, vjp = jax.vjp(lambda r, b, al: _ssd_prep_fn(r, b, al, e64v, e128v),
                         raw_ref[...], bias_ref[...], alog_ref[...])
        draw, dbias, dalog = vjp((g0[...], g1[...], g2[...], g3[...]))
        draw_ref[...] = draw.astype(BF16)
        dbias_ref[...] += dbias
        dalog_ref[...] += dalog

    return pl.pallas_call(
        body, name=name, grid=(t // CHUNK,), in_specs=ins + outs,
        out_specs=[pl.BlockSpec((CHUNK, LANES), lambda i: (i, 0)), vec, vec],
        out_shape=[jax.ShapeDtypeStruct((t, LANES), BF16), jax.ShapeDtypeStruct((1, LANES), F32),
                   jax.ShapeDtypeStruct((1, LANES), F32)],
        compiler_params=_cparams("arbitrary"),
    )(dt_raw, dt_bias, a_log, e64, e128, d_dt, d_acs, d_alast, d_acs128)


def _ssd_chunk(x, dt, acs, alast, col0, col1, bm, cm, prev):
    xdt = x * dt
    cb = _nt(cm, bm)
    tri = _iota2((CHUNK, CHUNK), 0) >= _iota2((CHUNK, CHUNK), 1)
    l0 = jnp.exp(jnp.where(tri, col0 - col0.T, NEG))
    l1 = jnp.exp(jnp.where(tri, col1 - col1.T, NEG))
    ydiag = jnp.where(_lane_lt64((CHUNK, LANES)), _nn(cb * l0, xdt), _nn(cb * l1, xdt))
    states = _tn(xdt * jnp.exp(alast - acs), bm)
    yoff = _nt(cm, prev) * jnp.exp(acs)
    new = prev * jnp.exp(alast).T + states
    return ydiag + yoff, new


def _ssd_specs(seq, rev):
    ncs = seq // CHUNK
    npair = C_WIDTH // LANES

    def row(s, c):
        return s * ncs + (ncs - 1 - c if rev else c)

    return dict(
        x=pl.BlockSpec((CHUNK, LANES), lambda s, j, c: (row(s, c), j)),
        bm=pl.BlockSpec((CHUNK, SSM_STATE), lambda s, j, c: (row(s, c), C_WIDTH // LANES + j // 4)),
        cm=pl.BlockSpec((CHUNK, SSM_STATE), lambda s, j, c: (row(s, c), C_WIDTH // LANES + 2 + j // 4)),
        col=pl.BlockSpec((CHUNK, 2 * LANES), lambda s, j, c: (row(s, c), j)),
        st=pl.BlockSpec((None, None, LANES, SSM_STATE), lambda s, j, c: (row(s, c), j, 0, 0)),
        npair=npair, ncs=ncs,
    )


def _ssd_fwd(xbc_act, dt64, acs64, alast64, acs128, seq, name):
    t = xbc_act.shape[0]
    sp = _ssd_specs(seq, False)

    def body(x_ref, dt_ref, acs_ref, alast_ref, col_ref, bm_ref, cm_ref, y_ref, prev_ref, state):
        @pl.when(pl.program_id(2) == 0)
        def _():
            state[...] = jnp.zeros_like(state)

        prev = state[...]
        prev_ref[...] = prev
        y, new = _ssd_chunk(x_ref[...], dt_ref[...], acs_ref[...], alast_ref[...],
                            col_ref[:, pl.ds(0, LANES)], col_ref[:, pl.ds(LANES, LANES)],
                            bm_ref[...], cm_ref[...], prev)
        y_ref[...] = y
        state[...] = new

    return pl.pallas_call(
        body, name=name, grid=(t // seq, sp["npair"], sp["ncs"]),
        in_specs=[sp["x"], sp["x"], sp["x"], sp["x"], sp["col"], sp["bm"], sp["cm"]],
        out_specs=[sp["x"], sp["st"]],
        out_shape=[jax.ShapeDtypeStruct((t, C_WIDTH), F32),
                   jax.ShapeDtypeStruct((t // CHUNK, sp["npair"], LANES, SSM_STATE), F32)],
        scratch_shapes=[pltpu.VMEM((LANES, SSM_STATE), F32)],
        compiler_params=_cparams("parallel", "parallel", "arbitrary"),
    )(xbc_act, dt64, acs64, alast64, acs128, xbc_act, xbc_act)


def _ssd_bwd(xbc_act, dt64, acs64, alast64, acs128, prev_saved, dy, seq, name):
    t = xbc_act.shape[0]
    sp = _ssd_specs(seq, True)

    def body(x_ref, dt_ref, acs_ref, alast_ref, col_ref, bm_ref, cm_ref, prev_ref, dy_ref,
             dx_ref, ddt_ref, dacs_ref, dalast_ref, dcol_ref, dbc_ref, dstate):
        @pl.when(pl.program_id(2) == 0)
        def _():
            dstate[...] = jnp.zeros_like(dstate)

        _, vjp = jax.vjp(_ssd_chunk, x_ref[...], dt_ref[...], acs_ref[...], alast_ref[...],
                         col_ref[:, pl.ds(0, LANES)], col_ref[:, pl.ds(LANES, LANES)],
                         bm_ref[...], cm_ref[...], prev_ref[...])
        dx, ddt, dacs, dalast, dc0, dc1, dbm, dcm, dprev = vjp((dy_ref[...], dstate[...]))
        dx_ref[...] = dx
        ddt_ref[...] = ddt
        dacs_ref[...] = dacs
        dalast_ref[...] = dalast
        dcol_ref[:, pl.ds(0, LANES)] = dc0
        dcol_ref[:, pl.ds(LANES, LANES)] = dc1
        dbc_ref[:, pl.ds(0, SSM_STATE)] = dbm
        dbc_ref[:, pl.ds(SSM_STATE, SSM_STATE)] = dcm
        dstate[...] = dprev

    wide = jax.ShapeDtypeStruct((t, C_WIDTH), F32)
    return pl.pallas_call(
        body, name=name, grid=(t // seq, sp["npair"], sp["ncs"]),
        in_specs=[sp["x"], sp["x"], sp["x"], sp["x"], sp["col"], sp["bm"], sp["cm"], sp["st"], sp["x"]],
        out_specs=[sp["x"], sp["x"], sp["x"], sp["x"], sp["col"], sp["col"]],
        out_shape=[wide, wide, wide, wide, jax.ShapeDtypeStruct((t, 2 * C_WIDTH), F32),
                   jax.ShapeDtypeStruct((t, 2 * C_WIDTH), F32)],
        scratch_shapes=[pltpu.VMEM((LANES, SSM_STATE), F32)],
        compiler_params=_cparams("parallel", "parallel", "arbitrary"),
    )(xbc_act, dt64, acs64, alast64, acs128, xbc_act, xbc_act, prev_saved, dy)


def _ssd_post_fn(y, xs, z, dskip, g):
    v = (y + dskip * xs) * _silu(z)
    return v * lax.rsqrt(jnp.mean(v * v, axis=-1, keepdims=True) + EPS) * g


def _ssd_post_specs(tm, order):
    gw = C_WIDTH // 2

    def im(f):
        return lambda *ids: f(*order(*ids))
    return dict(
        blk=pl.BlockSpec((tm, gw), im(lambda g, r: (r, g))),
        z=pl.BlockSpec((tm, gw), im(lambda g, r: (r, COL_Z * LANES // gw + g))),
        vec=pl.BlockSpec((1, gw), im(lambda g, r: (0, g))),
    )


def _ssd_post_fwd(y_ssd, xbc_act, proj, dskip64, norm_g, name):
    t = y_ssd.shape[0]
    tm = _row_tile(t)
    sp = _ssd_post_specs(tm, lambda r, g: (g, r))

    def body(y_ref, xs_ref, z_ref, ds_ref, g_ref, o_ref):
        o_ref[...] = _ssd_post_fn(y_ref[...], xs_ref[...], z_ref[...], ds_ref[...], g_ref[...]).astype(BF16)

    return pl.pallas_call(
        body, name=name, grid=(t // tm, 2),
        in_specs=[sp["blk"], sp["blk"], sp["z"], sp["vec"], sp["vec"]], out_specs=sp["blk"],
        out_shape=jax.ShapeDtypeStruct((t, C_WIDTH), BF16),
        compiler_params=_cparams("parallel", "parallel"),
    )(y_ssd, xbc_act, proj, dskip64, norm_g)


def _ssd_post_bwd(y_ssd, xbc_act, proj, dskip64, norm_g, dyc, name, dy_col=0):
    t = y_ssd.shape[0]
    tm = _row_tile(t)
    sp = _ssd_post_specs(tm, lambda g, r: (g, r))
    dy_spec = pl.BlockSpec((tm, C_WIDTH // 2), lambda g, r: (r, dy_col + g))

    def body(y_ref, xs_ref, z_ref, ds_ref, g_ref, dyc_ref, dy_ref, dxs_ref, dz_ref, dds_ref, dg_ref):
        @pl.when(pl.program_id(1) == 0)
        def _():
            dds_ref[...] = jnp.zeros_like(dds_ref)
            dg_ref[...] = jnp.zeros_like(dg_ref)

        _, vjp = jax.vjp(_ssd_post_fn, y_ref[...], xs_ref[...], z_ref[...], ds_ref[...], g_ref[...])
        dy, dxs, dz, dds, dg = vjp(dyc_ref[...])
        dy_ref[...] = dy
        dxs_ref[...] = dxs
        dz_ref[...] = dz.astype(BF16)
        dds_ref[...] += dds
        dg_ref[...] += dg

    wide = jax.ShapeDtypeStruct((t, C_WIDTH), F32)
    vec = jax.ShapeDtypeStruct((1, C_WIDTH), F32)
    return pl.pallas_call(
        body, name=name, grid=(2, t // tm),
        in_specs=[sp["blk"], sp["blk"], sp["z"], sp["vec"], sp["vec"], dy_spec],
        out_specs=[sp["blk"], sp["blk"], sp["blk"], sp["vec"], sp["vec"]],
        out_shape=[wide, wide, jax.ShapeDtypeStruct((t, C_WIDTH), BF16), vec, vec],
        compiler_params=_cparams("parallel", "arbitrary"),
    )(y_ssd, xbc_act, proj, dskip64, norm_g, dyc)


def _ssd_assemble(dxs_ssd, dxs_skip, dbc, name):
    t = dxs_ssd.shape[0]
    tm = _row_tile(t)
    npair = C_WIDTH // LANES

    def body(a_ref, b_ref, dbc_ref, o_ref):
        o_ref[:, pl.ds(0, C_WIDTH)] = a_ref[...] + b_ref[...]
        for grp in range(2):
            for which in range(2):
                acc = jnp.zeros((tm, SSM_STATE), F32)
                for j in range(grp * npair // 2, (grp + 1) * npair // 2):
                    acc = acc + dbc_ref[:, pl.ds((2 * j + which) * SSM_STATE, SSM_STATE)]
                o_ref[:, pl.ds(C_WIDTH + (2 * which + grp) * SSM_STATE, SSM_STATE)] = acc

    return pl.pallas_call(
        body, name=name, grid=(t // tm,),
        in_specs=[pl.BlockSpec((tm, C_WIDTH), lambda i: (i, 0))] * 2 + [pl.BlockSpec((tm, 2 * C_WIDTH), lambda i: (i, 0))],
        out_specs=pl.BlockSpec((tm, D_CONV_C), lambda i: (i, 0)),
        out_shape=jax.ShapeDtypeStruct((t, D_CONV_C), F32),
        compiler_params=_cparams("parallel"),
    )(dxs_ssd, dxs_skip, dbc)


def _pad_taps(w):
    return jnp.pad(w, ((0, CONV_PAD - w.shape[0]), (0, 0)))


def _pad_heads(v):
    return jnp.pad(v, (0, LANES - v.shape[0])).reshape(1, LANES)


def _group_heads(a):
    pad = [(0, 0)] * (a.ndim - 1) + [(0, LANES - SSD_GROUP_HEADS)]
    return jnp.concatenate([jnp.pad(a[..., :SSD_GROUP_HEADS], pad), jnp.pad(a[..., SSD_GROUP_HEADS:], pad)], axis=-1)


def _ungroup_heads(a):
    return jnp.concatenate([a[..., :SSD_GROUP_HEADS], a[..., LANES:LANES + SSD_GROUP_HEADS]], axis=-1)


def _layer_fwd(x, p, seq, li, after=()):
    n = f"l{li}_"
    h1 = _rms_fwd(x, p["norm1_g"], n + "rms1", after=after)
    proj = _matmul(h1, p["w_main"], mode="nn", name=n + "inproj")
    dt_raw = _matmul(h1, p["w_dt"], mode="nn", name=n + "inproj_dt")
    row = lambda v: v.reshape(1, -1)
    ya, conv_a = _conv_fwd("a", proj, _pad_taps(p["conv_a_w"]), row(p["conv_a_b"]), (row(p["ln_a_g"]), row(p["ln_a_b"])),
                           seq, n + "conva", out_dtype=BF16, keep_conv=True)
    yb = _gmlp_fwd(proj, row(p["ln_b_g"]), row(p["ln_b_b"]), p["w_spatial"], p["b_spatial"], n + "gmlp")
    xbc_act = _conv_fwd("c", proj, _pad_taps(p["conv_c_w"]), row(p["conv_c_b"]), (), seq, n + "convc")
    y_ssd, prev = _ssd2_fwd(xbc_act, dt_raw, _group_heads(row(p["dt_bias"])), _group_heads(row(p["a_log"])), seq, n + "ssd")
    dskip64 = jnp.repeat(p["d_skip"], HEAD_DIM).reshape(1, C_WIDTH)
    yc = _ssd_post_fwd(y_ssd, xbc_act, proj, dskip64, row(p["norm_c_g"]), n + "ssdpost")
    ycat = jnp.concatenate([ya, yb, yc], axis=1)
    x1 = _matmul(ycat, p["w_out"], mode="nn", name=n + "outproj", add=x)
    h2 = _rms_fwd(x1, p["norm2_g"], n + "rms2")
    u, act = _matmul(h2, p["w_ff1"], mode="nn", name=n + "ff1", epilogue=_relu2_epilogue, out_dtypes=(F32, BF16),
                     b_chips=True)
    x2 = _matmul(act, p["w_ff2"], mode="nn", name=n + "ff2", add=x1)
    saved = dict(x=x, h1=h1, proj=proj, conv_a=conv_a, dt_raw=dt_raw, xbc_act=xbc_act, prev=prev, y_ssd=y_ssd,
                 dskip64=dskip64, ycat=ycat, x1=x1, h2=h2, u=u, act=act)
    return x2, saved


def _layer_bwd(dx2, p, s, seq, li, after=(), on_ffn_grads=None):
    n = f"l{li}_b_"
    row = lambda v: v.reshape(1, -1)
    g = {}
    du = _matmul(dx2, p["w_ff2"], mode="nt", name=n + "ff2_dx", epilogue=_relu2_bwd_epilogue, extra=s["u"],
                 out_dtypes=(BF16,), after=after)
    g["w_ff2"] = _matmul(s["act"], dx2, mode="tn", name=n + "ff2_dw")
    g["w_ff1"] = _matmul(s["h2"], du, mode="tn", name=n + "ff1_dw", out_chips=True)
    dh2 = _matmul(du, p["w_ff1"], mode="nt", name=n + "ff1_dx", b_chips=True)
    dx1, g["norm2_g"] = _rms_bwd(s["x1"], p["norm2_g"], dh2, dx2, n + "rms2")
    g["w_out"] = _matmul(s["ycat"], dx1, mode="tn", name=n + "out_dw")
    dycat = _matmul(dx1, p["w_out"], mode="nt", name=n + "out_dx",
                    after=() if on_ffn_grads is None else on_ffn_grads(g))
    proj = s["proj"]
    (dval, dgate), dwa, dba, (dlag, dlab) = _conv_bwd(
        "a", proj, _pad_taps(p["conv_a_w"]), row(p["conv_a_b"]), (row(p["ln_a_g"]), row(p["ln_a_b"])), dycat, seq,
        n + "conva", dy_col=0, conv_out=s["conv_a"])
    g["conv_a_w"], g["conv_a_b"], g["ln_a_g"], g["ln_a_b"] = dwa[:CONV_A_K], dba[0], dlag[0], dlab[0]
    dbu, dbv, dlbg, dlbb, g["w_spatial"], g["b_spatial"] = _gmlp_bwd(
        proj, row(p["ln_b_g"]), row(p["ln_b_b"]), p["w_spatial"], p["b_spatial"], dycat, n + "gmlp",
        dy_col=A_WIDTH // LANES)
    g["ln_b_g"], g["ln_b_b"] = dlbg[0], dlbb[0]
    dy_ssd, dxs_skip, dz, dds, dncg = _ssd_post_bwd(s["y_ssd"], s["xbc_act"], proj, s["dskip64"], row(p["norm_c_g"]),
                                                    dycat, n + "ssdpost", dy_col=(A_WIDTH + B_WIDTH) * 2 // C_WIDTH)
    g["norm_c_g"] = dncg[0]
    g["d_skip"] = dds.reshape(C_HEADS, HEAD_DIM).sum(axis=1)
    dxs, ddt_raw, ddtb, dalog, dbm, dcm = _ssd2_bwd(
        s["xbc_act"], s["dt_raw"], _group_heads(row(p["dt_bias"])), _group_heads(row(p["a_log"])), s["prev"], dy_ssd, seq,
        n + "ssd")
    g["dt_bias"], g["a_log"] = _ungroup_heads(ddtb)[0], _ungroup_heads(dalog)[0]
    dconv = _ssd2_assemble(dxs, dxs_skip, dbm, dcm, n + "ssdasm")
    (dxbc,), dwc, dbcv, _ = _conv_bwd("c", proj, _pad_taps(p["conv_c_w"]), row(p["conv_c_b"]), (), dconv, seq, n + "convc")
    g["conv_c_w"], g["conv_c_b"] = dwc[:CONV_C_K], dbcv[0]
    dproj = jnp.concatenate([dval, dgate, dbu, dbv, dz, dxbc], axis=1)
    g["w_main"] = _matmul(s["h1"], dproj, mode="tn", name=n + "in_dw")
    g["w_dt"] = _matmul(s["h1"], ddt_raw, mode="tn", name=n + "indt_dw")
    dh1 = _matmul(dproj, p["w_main"], mode="nt", name=n + "in_dx")
    dh1 = _matmul(ddt_raw, p["w_dt"], mode="nt", name=n + "indt_dx", add=dh1)
    dx, g["norm1_g"] = _rms_bwd(s["x"], p["norm1_g"], dh1, dx1, n + "rms1")
    return dx, g


EW_BLOCK_BYTES = 1 << 20


def _ew(fn, ins, out_dtypes, name, leads=None):
    leads = leads or [None] * len(ins)
    rows, c = ins[0].shape[-2:]
    tr = _pick(rows, [t for t in (2048, 1024, 512, 256, 128, 64, 32, 16, 8) if t * c * 4 <= EW_BLOCK_BYTES])
    n_in = len(ins)

    def spec(lead):
        if lead is None:
            return pl.BlockSpec((tr, c), lambda i: (i, 0))
        return pl.BlockSpec((None, tr, c), functools.partial(lambda i, k: (k, i, 0), k=lead))

    def body(*refs):
        outs = fn(*[r[...].astype(F32) for r in refs[:n_in]])
        for o_ref, o in zip(refs[n_in:], outs):
            o_ref[...] = o.astype(o_ref.dtype)

    return pl.pallas_call(
        body, name=name, grid=(rows // tr,),
        in_specs=[spec(l) for l in leads], out_specs=[spec(None)] * len(out_dtypes),
        out_shape=[jax.ShapeDtypeStruct((rows, c), dt) for dt in out_dtypes],
        compiler_params=_cparams("parallel"),
    )(*ins)


def _adam_fn(w, g, m, v):
    m2 = ADAM_B1 * m + (1.0 - ADAM_B1) * g
    v2 = ADAM_B2 * v + (1.0 - ADAM_B2) * (g * g)
    m_hat = m2 / (1.0 - ADAM_B1 ** ADAM_STEP)
    v_hat = v2 / (1.0 - ADAM_B2 ** ADAM_STEP)
    delta = -ADAM_LR * (m_hat / (jnp.sqrt(v_hat) + ADAM_EPS) + ADAM_WD * w)
    return delta, m2, v2


def _adam(w, g, m, v, name):
    shape = w.shape
    two_d = lambda a: a.reshape(-1, shape[-1])
    outs = _ew(_adam_fn, [two_d(w), two_d(g), two_d(m), two_d(v)], (F32, F32, F32), name)
    return [o.reshape(shape) for o in outs]


_ANY = pl.BlockSpec(memory_space=pl.ANY)


def _mesh_pos():
    return lax.axis_index("x"), lax.axis_index("y"), lax.axis_index("c")


def _peer_chips(x, y):
    return [(1 - x, y), (x, 1 - y), (1 - x, 1 - y)]


def _remote(src, dst, send_sems, recv_sems, sem, to):
    return pltpu.make_async_remote_copy(src_ref=src, dst_ref=dst, send_sem=send_sems.at[sem],
                                        recv_sem=recv_sems.at[sem], device_id=to, device_id_type=MESH)


def _half_rows(n_rows, which):
    half = n_rows // 2
    return pl.ds(pl.multiple_of(which * half, 8), half)


def _comm_call(body, ins, out_shapes, n_sems, name):
    scratch = [pltpu.SemaphoreType.DMA((n_sems,)), pltpu.SemaphoreType.DMA((n_sems,))]
    return pl.pallas_call(
        body, name=name, in_specs=[_ANY] * len(ins), out_specs=[_ANY] * len(out_shapes),
        out_shape=out_shapes, scratch_shapes=scratch,
    )(*ins)


def _gather_weights(big, small, name):
    nb, ns = len(big), len(small)
    n = nb + ns

    def body(*refs):
        ins, outs = refs[:n], refs[n:2 * n]
        send_sems, recv_sems = refs[2 * n:]
        x, y, c = _mesh_pos()
        q = 2 * x + y
        me, sib = (x, y, c), (x, y, 1 - c)
        chips = _peer_chips(x, y)
        rem = functools.partial(_remote, send_sems=send_sems, recv_sems=recv_sems)
        first = []
        for i in range(nb):
            mine = _half_rows(big[i].shape[0], c)
            for k, (px, py) in enumerate(chips):
                first.append(rem(ins[i].at[mine], outs[i].at[q, mine], sem=6 * i + k, to=(px, py, c)))
        for j in range(ns):
            for k, (px, py) in enumerate(chips):
                first.append(rem(ins[nb + j], outs[nb + j].at[q], sem=6 * nb + 3 * j + k, to=(px, py, c)))
        for cp in first:
            cp.start()
        passed = []
        for i in range(nb):
            mine = _half_rows(big[i].shape[0], c)
            for k, (px, py) in enumerate(chips):
                landed = outs[i].at[2 * px + py, mine]
                rem(landed, landed, sem=6 * i + k, to=me).wait_recv()
                fwd = rem(landed, landed, sem=6 * i + 3 + k, to=sib)
                fwd.start()
                passed.append(fwd)
        for i in range(nb):
            other = _half_rows(big[i].shape[0], 1 - c)
            for k, (px, py) in enumerate(chips):
                theirs = outs[i].at[2 * px + py, other]
                rem(theirs, theirs, sem=6 * i + 3 + k, to=me).wait_recv()
        for j in range(ns):
            for k, (px, py) in enumerate(chips):
                dst = outs[nb + j].at[2 * px + py]
                rem(dst, dst, sem=6 * nb + 3 * j + k, to=me).wait_recv()
        for cp in first + passed:
            cp.wait_send()

    out_shapes = [jax.ShapeDtypeStruct((N_CHIPS,) + a.shape, a.dtype) for a in list(big) + list(small)]
    return _comm_call(body, list(big) + list(small), out_shapes, 6 * nb + 3 * ns, name)


def _sibling_other_halves(gs, name):
    n = len(gs)

    def other_half(ref, shape, c):
        rows = _half_rows(shape[-2], 1 - c)
        return ref.at[rows] if len(shape) == 2 else ref.at[:, rows]

    def body(*refs):
        ins, outs = refs[:n], refs[n:2 * n]
        send_sems, recv_sems = refs[2 * n:]
        x, y, c = _mesh_pos()
        copies = [_remote(other_half(ins[i], gs[i].shape, c), outs[i], send_sems, recv_sems, i, (x, y, 1 - c))
                  for i in range(n)]
        for cp in copies:
            cp.start()
        for cp in copies:
            cp.wait()

    out_shapes = [jax.ShapeDtypeStruct(g.shape[:-2] + (g.shape[-2] // 2, g.shape[-1]), g.dtype) for g in gs]
    return _comm_call(body, list(gs), out_shapes, n, name)


IN_SHARD = D_IN_PROJ // N_CHIPS


def _chipsum_in(mine, mine_dt, theirs, theirs_dt, name):
    r = mine.shape[0]
    tr = _pick(r, (128, 64, 32, 16, 8))
    last = D_MAIN - (N_CHIPS - 1) * IN_SHARD

    def body(a_ref, adt_ref, b_ref, bdt_ref, o32_ref, o16_ref):
        for p in range(N_CHIPS):
            wid = IN_SHARD if p < N_CHIPS - 1 else last
            s = a_ref[:, pl.ds(IN_SHARD * p, wid)] + b_ref[:, pl.ds(IN_SHARD * p, wid)]
            o32_ref[p, :, pl.ds(0, wid)] = s
            o16_ref[p, :, pl.ds(0, wid)] = s.astype(BF16)
        for grp in range(2):
            src = pl.ds(grp * LANES, SSD_GROUP_HEADS)
            s = adt_ref[:, src] + bdt_ref[:, src]
            dst = pl.ds(last + grp * SSD_GROUP_HEADS, SSD_GROUP_HEADS)
            o32_ref[N_CHIPS - 1, :, dst] = s
            o16_ref[N_CHIPS - 1, :, dst] = s.astype(BF16)

    wide = pl.BlockSpec((tr, D_MAIN), lambda i: (i, 0))
    narrow = pl.BlockSpec((tr, 2 * LANES), lambda i: (i, 0))
    out = pl.BlockSpec((N_CHIPS, tr, IN_SHARD), lambda i: (0, i, 0))
    return pl.pallas_call(
        body, name=name, grid=(r // tr,), in_specs=[wide, narrow, wide, narrow], out_specs=[out, out],
        out_shape=[jax.ShapeDtypeStruct((N_CHIPS, r, IN_SHARD), F32), jax.ShapeDtypeStruct((N_CHIPS, r, IN_SHARD), BF16)],
        compiler_params=_cparams("parallel"),
    )(mine, mine_dt, theirs, theirs_dt)


def _chip_scatter(cs, name):
    n = len(cs)

    def body(*refs):
        ins, outs = refs[:n], refs[n:2 * n]
        send_sems, recv_sems = refs[2 * n:]
        x, y, c = _mesh_pos()
        copies = []
        for i in range(n):
            for k, (px, py) in enumerate(_peer_chips(x, y)):
                copies.append(_remote(ins[i].at[2 * px + py], outs[i].at[k], send_sems, recv_sems, 3 * i + k, (px, py, c)))
        for cp in copies:
            cp.start()
        for cp in copies:
            cp.wait()

    out_shapes = [jax.ShapeDtypeStruct((3,) + a.shape[1:], a.dtype) for a in cs]
    return _comm_call(body, list(cs), out_shapes, 3 * n, name)


_HBM = pl.BlockSpec(memory_space=pltpu.HBM)
_SEM = pl.BlockSpec(memory_space=pltpu.SEMAPHORE)


def _in_hbm(a):
    return pltpu.with_memory_space_constraint(a, pltpu.HBM)


def _split_plan(kind, srcs, lands, x, y, c):
    plan = []
    for src, land in zip(srcs, lands):
        if kind == "sibling":
            rows = _half_rows(src.shape[-2], 1 - c)
            plan.append((src.at[rows] if len(src.shape) == 2 else src.at[:, rows], land, (x, y, 1 - c)))
            continue
        if kind == "allgather":
            peers = [(x, y, 1 - c)] + [(px, py, pc) for px, py in _peer_chips(x, y) for pc in (c, 1 - c)]
            plan += [(src, land.at[4 * x + 2 * y + c], peer) for peer in peers]
            continue
        for k, (px, py) in enumerate(_peer_chips(x, y)):
            if kind == "scatter":
                plan.append((src.at[2 * px + py], land.at[k], (px, py, c)))
            else:
                plan.append((src, land.at[2 * x + y], (px, py, c)))
    return plan


def _split_start(kind, srcs, land_shapes, name):
    n = len(srcs)

    def body(*refs):
        ins, lands = refs[:n], refs[n:2 * n]
        send_sems, recv_sems = refs[2 * n], refs[2 * n + 1]
        token = refs[-1]
        x, y, c = _mesh_pos()
        for i, (src, dst, to) in enumerate(_split_plan(kind, ins, lands, x, y, c)):
            pltpu.make_async_remote_copy(src_ref=src, dst_ref=dst, send_sem=send_sems.at[i], recv_sem=recv_sems.at[i],
                                         device_id=to, device_id_type=MESH).start()
        token[...] = jnp.zeros_like(token)

    zones = [lax.empty(s.shape, s.dtype) for s in land_shapes]
    n_sems = {"sibling": 1, "allgather": 7}.get(kind, 3) * n
    res = pl.pallas_call(
        body, name=name,
        out_shape=(pltpu.SemaphoreType.DMA((n_sems,)), pltpu.SemaphoreType.DMA((n_sems,)),
                   *[pltpu.HBM(a.shape, a.dtype) for a in srcs], *[pltpu.HBM(s.shape, s.dtype) for s in land_shapes],
                   jax.ShapeDtypeStruct((8, LANES), F32)),
        in_specs=[_HBM] * (2 * n), out_specs=(_SEM, _SEM, *[_HBM] * (2 * n), pl.BlockSpec(memory_space=pltpu.VMEM)),
        input_output_aliases={i: 2 + i for i in range(2 * n)},
        compiler_params=pltpu.CompilerParams(has_side_effects=pltpu.SideEffectType.DATAFLOW_SIDE_EFFECTING),
    )(*[_in_hbm(a) for a in srcs], *[_in_hbm(z) for z in zones])
    return dict(send=res[0], recv=res[1], srcs=list(res[2:2 + n]), lands=list(res[2 + n:2 + 2 * n]), token=res[-1], kind=kind)


def _split_wait(started, after, name):
    n = len(started["srcs"])
    kind = started["kind"]

    def body(*refs):
        ins, lands = refs[:n], refs[n:2 * n]
        send_sems, recv_sems = refs[2 * n], refs[2 * n + 1]
        x, y, c = _mesh_pos()
        for i, (src, dst, _) in enumerate(_split_plan(kind, ins, lands, x, y, c)):
            cp = pltpu.make_async_remote_copy(src_ref=src, dst_ref=dst, send_sem=send_sems.at[i], recv_sem=recv_sems.at[i],
                                              device_id=(x, y, c), device_id_type=MESH)
            cp.wait_send()
            cp.wait_recv()

    arrs = started["srcs"] + started["lands"]
    res = pl.pallas_call(
        body, name=name, out_shape=tuple(pltpu.HBM(a.shape, a.dtype) for a in arrs),
        in_specs=[_HBM] * (2 * n) + [_SEM, _SEM, pl.BlockSpec(memory_space=pl.ANY)], out_specs=tuple([_HBM] * (2 * n)),
        input_output_aliases={i: i for i in range(2 * n)},
        compiler_params=pltpu.CompilerParams(has_side_effects=pltpu.SideEffectType.DATAFLOW_SIDE_EFFECTING),
    )(*arrs, started["send"], started["recv"], after)
    return list(res[:n]), list(res[n:])


def _sibling_share(fs, name):
    n = len(fs)

    def body(*refs):
        ins, outs = refs[:n], refs[n:2 * n]
        send_sems, recv_sems = refs[2 * n:]
        x, y, c = _mesh_pos()
        copies = [_remote(ins[i], outs[i], send_sems, recv_sems, i, (x, y, 1 - c)) for i in range(n)]
        for cp in copies:
            cp.start()
        for cp in copies:
            cp.wait()

    out_shapes = [jax.ShapeDtypeStruct(a.shape, a.dtype) for a in fs]
    return _comm_call(body, list(fs), out_shapes, n, name)


def _allgather8(v, name, after=()):
    m = v.shape[0]

    def body(v_ref, *rest):
        out_ref, send_sems, recv_sems = rest[len(after):]
        x, y, c = _mesh_pos()
        me, sib = (x, y, c), (x, y, 1 - c)
        chips = _peer_chips(x, y)
        rem = functools.partial(_remote, send_sems=send_sems, recv_sems=recv_sems)

        def blk(px, py, pc):
            return out_ref.at[4 * px + 2 * py + pc]

        first = [rem(v_ref, blk(*me), sem=0, to=sib)]
        first += [rem(v_ref, blk(*me), sem=1 + k, to=(px, py, c)) for k, (px, py) in enumerate(chips)]
        for cp in first:
            cp.start()
        passed = []
        for k, (px, py) in enumerate(chips):
            landed = blk(px, py, c)
            rem(landed, landed, sem=1 + k, to=me).wait_recv()
            fwd = rem(landed, landed, sem=4 + k, to=sib)
            fwd.start()
            passed.append(fwd)
        rem(blk(*sib), blk(*sib), sem=0, to=me).wait_recv()
        for k, (px, py) in enumerate(chips):
            theirs = blk(px, py, 1 - c)
            rem(theirs, theirs, sem=4 + k, to=me).wait_recv()
        for cp in first + passed:
            cp.wait_send()

    return _comm_call(body, [v, *after], [jax.ShapeDtypeStruct((8, m, LANES), v.dtype)], 7, name)[0]


_WEIGHTS = ["norm1_g", "w_in", "conv_a_w", "conv_a_b", "ln_a_g", "ln_a_b", "ln_b_g", "ln_b_b", "w_spatial", "b_spatial",
            "conv_c_w", "conv_c_b", "dt_bias", "a_log", "d_skip", "norm_c_g", "w_out", "norm2_g", "w_ff1", "w_ff2", "final_g"]
_BIG = ["w_in", "w_out", "w_ff1", "w_ff2"]
_CONV_SHARDED = ["conv_a_w", "conv_c_w"]
_SMALL = [w for w in _WEIGHTS if w not in _BIG and w != "final_g"]
_PACK_ROWS = 512


def _pack(arrs):
    flat = jnp.concatenate([a.reshape(-1) for a in arrs])
    blk = _PACK_ROWS * LANES
    n = flat.shape[0]
    return jnp.pad(flat, (0, -(-n // blk) * blk - n)).reshape(-1, LANES)


def _unpack(packed, shapes):
    flat = packed.reshape(-1)
    out, off = [], 0
    for s in shapes:
        n = math.prod(s)
        out.append(flat[off:off + n].reshape(s))
        off += n
    return out


def _cols_to_chips(a):
    k = a.shape[0]
    return a.reshape(k, N_CHIPS, -1).transpose(1, 0, 2)


def _chips_to_cols(a):
    return a.transpose(1, 0, 2).reshape(a.shape[1], -1)


def _own_shards(w, li):
    return [w[k][li].astype(BF16) for k in _BIG] + [w[k][li] for k in _CONV_SHARDED]


def _assemble_w_in(gathered, own, name):
    k = own.shape[0]
    tr = _pick(k, (256, 128, 64, 32, 16))
    last = D_MAIN - (N_CHIPS - 1) * IN_SHARD

    def body(g_ref, own_ref, main_ref, dt_ref):
        x, y, _ = _mesh_pos()
        q = 2 * x + y
        dt_ref[...] = jnp.zeros_like(dt_ref)

        def place(read):
            for p in range(N_CHIPS):
                def _(p=p):
                    wid = IN_SHARD if p < N_CHIPS - 1 else last
                    main_ref[:, pl.ds(IN_SHARD * p, wid)] = read(p, pl.ds(0, wid))
                    if p == N_CHIPS - 1:
                        for grp in range(2):
                            dt_ref[:, pl.ds(grp * LANES, SSD_GROUP_HEADS)] = read(
                                p, pl.ds(last + grp * SSD_GROUP_HEADS, SSD_GROUP_HEADS))
                yield p, _

        for p, put in place(lambda p, cols: own_ref[:, cols]):
            pl.when(q == p)(put)
        for p, put in place(lambda p, cols: g_ref[p, :, cols]):
            pl.when(q != p)(put)

    return pl.pallas_call(
        body, name=name, grid=(k // tr,),
        in_specs=[pl.BlockSpec((N_CHIPS, tr, IN_SHARD), lambda i: (0, i, 0)), pl.BlockSpec((tr, IN_SHARD), lambda i: (i, 0))],
        out_specs=[pl.BlockSpec((tr, D_MAIN), lambda i: (i, 0)), pl.BlockSpec((tr, 2 * LANES), lambda i: (i, 0))],
        out_shape=[jax.ShapeDtypeStruct((k, D_MAIN), own.dtype), jax.ShapeDtypeStruct((k, 2 * LANES), own.dtype)],
        compiler_params=_cparams("parallel"),
    )(gathered, own)


def _with_own_shard(gathered, own, name):
    _, r, c = gathered.shape
    tr = _pick(r, (512, 256, 128, 64, 32, 16))

    def body(g_ref, own_ref, o_ref):
        x, y, _ = _mesh_pos()
        mine = pl.program_id(0) == 2 * x + y

        @pl.when(mine)
        def _():
            o_ref[...] = own_ref[...]

        @pl.when(jnp.logical_not(mine))
        def _():
            o_ref[...] = g_ref[...]

    slot = pl.BlockSpec((None, tr, c), lambda p, i: (p, i, 0))
    return pl.pallas_call(
        body, name=name, grid=(N_CHIPS, r // tr), in_specs=[slot, pl.BlockSpec((tr, c), lambda p, i: (i, 0))],
        out_specs=slot, out_shape=jax.ShapeDtypeStruct(gathered.shape, gathered.dtype),
        compiler_params=_cparams("parallel", "parallel"),
    )(gathered, own)


def _layer_params(w, li, own, gathered, q):
    g_out, g_ff1, g_ff2 = [_with_own_shard(g, o, f"l{li}_own{i}") for i, (g, o) in enumerate(zip(gathered[1:4], own[1:4]))]
    g_ca, g_cc = [lax.dynamic_update_index_in_dim(g, o, q, axis=0) for g, o in zip(gathered[4:], own[4:])]
    p = {k: w[k][li] for k in _SMALL if k not in _CONV_SHARDED}
    p["w_main"], p["w_dt"] = _assemble_w_in(gathered[0], own[0], f"l{li}_w_in")
    p["w_out"] = g_out.reshape(D_MIX, D_MODEL)
    p["w_ff1"] = g_ff1
    p["w_ff2"] = g_ff2.reshape(D_FF, D_MODEL)
    p["conv_a_w"] = _chips_to_cols(g_ca)
    p["conv_c_w"] = _chips_to_cols(g_cc)
    return p


def _ffn_out_grads(g):
    return [g["w_out"].reshape(N_CHIPS, -1, D_MODEL), g["w_ff1"], g["w_ff2"].reshape(N_CHIPS, -1, D_MODEL)]


def _half_shape(a):
    return jax.ShapeDtypeStruct(a.shape[:-2] + (a.shape[-2] // 2, a.shape[-1]), a.dtype)


def _chip_sums(g, early, early_from_sib, li, c, q):
    n = f"l{li}_rs_"
    late = [g["w_main"], g["w_dt"]]
    full = late + list(early)
    from_sib = list(_sibling_other_halves(late, n + "sib")) + list(early_from_sib)
    mine = [lax.dynamic_slice_in_dim(a, c * b.shape[-2], b.shape[-2], axis=a.ndim - 2) for a, b in zip(full, from_sib)]
    sums = [_chipsum_in(mine[0], mine[1], from_sib[0], from_sib[1], n + "chipsum0")]
    for i in range(2, len(full)):
        shape = from_sib[i].shape
        s32, s16 = _ew(lambda u, v: (u + v, u + v), [mine[i].reshape(-1, shape[-1]), from_sib[i].reshape(-1, shape[-1])],
                       (F32, BF16), n + f"chipsum{i - 1}")
        sums.append((s32.reshape(shape), s16.reshape(shape)))
    chip_f32 = [lax.dynamic_index_in_dim(s32, q, axis=0, keepdims=False) for s32, _ in sums]
    chip_bf16 = [s16 for _, s16 in sums]
    return chip_f32, chip_bf16


def _finish_reduce(chip_f32, from_chips, li, c):
    n = f"l{li}_rs_"
    halves = [_ew(lambda o, r0, r1, r2_: (((o + r0) + r1) + r2_,), [own, rb, rb, rb], (F32,), n + f"final{i}",
                  leads=[None, 0, 1, 2])[0] for i, (own, rb) in enumerate(zip(chip_f32, from_chips))]
    from_sib = _sibling_share(halves, n + "share")
    return [jnp.where(c == 0, jnp.concatenate([h, s], axis=0), jnp.concatenate([s, h], axis=0))
            for h, s in zip(halves, from_sib)]


def kernel(x, norm1_g, w_in, conv_a_w, conv_a_b, ln_a_g, ln_a_b, ln_b_g, ln_b_b, w_spatial, b_spatial, conv_c_w, conv_c_b, dt_bias, a_log, d_skip, norm_c_g, w_out, norm2_g, w_ff1, w_ff2, final_g, loss_target, m_norm1_g, m_w_in, m_conv_a_w, m_conv_a_b, m_ln_a_g, m_ln_a_b, m_ln_b_g, m_ln_b_b, m_w_spatial, m_b_spatial, m_conv_c_w, m_conv_c_b, m_dt_bias, m_a_log, m_d_skip, m_norm_c_g, m_w_out, m_norm2_g, m_w_ff1, m_w_ff2, m_final_g, v_norm1_g, v_w_in, v_conv_a_w, v_conv_a_b, v_ln_a_g, v_ln_a_b, v_ln_b_g, v_ln_b_b, v_w_spatial, v_b_spatial, v_conv_c_w, v_conv_c_b, v_dt_bias, v_a_log, v_d_skip, v_norm_c_g, v_w_out, v_norm2_g, v_w_ff1, v_w_ff2, v_final_g):
    given = dict(locals())
    w = {k: given[k] for k in _WEIGHTS}
    m = {k: given["m_" + k] for k in _WEIGHTS}
    v = {k: given["v_" + k] for k in _WEIGHTS}
    depth = w_in.shape[0]
    nseq, seq, d = x.shape
    xi, yi, ci = _mesh_pos()
    q = 2 * xi + yi

    own = [_own_shards(w, li) for li in range(depth)]
    nb = len(_BIG)
    gathered = _gather_weights(own[0][:nb], own[0][nb:], "l0_gather")
    h = x.reshape(nseq * seq, d)
    layer_params, saved = [], []
    for li in range(depth):
        nxt = None
        if li + 1 < depth:
            srcs, _ = lax.optimization_barrier((own[li + 1], gathered))
            zones = [jax.ShapeDtypeStruct((N_CHIPS,) + a.shape, a.dtype) for a in srcs]
            nxt = _split_start("gather", srcs, zones, f"l{li + 1}_gather_start")
        layer_params.append(_layer_params(w, li, own[li], gathered, q))
        h, s = _layer_fwd(h, layer_params[li], seq, li, after=() if nxt is None else (nxt["token"],))
        saved.append(s)
        if nxt is not None:
            own[li + 1], gathered = _split_wait(nxt, h, f"l{li + 1}_gather_wait")
    loss, dx, d_final = _loss_head(h, final_g, loss_target.reshape(nseq * seq, d))

    grads = [None] * depth
    big_grads = [None] * depth
    pending = None
    for li in reversed(range(depth)):
        swaps = []

        def early_swap(g, li=li, swaps=swaps):
            early = _ffn_out_grads(g)
            swaps.append(_split_start("sibling", early, [_half_shape(a) for a in early], f"l{li}_rs_sib_start"))
            return (swaps[0]["token"],)

        after = () if pending is None else (pending[1]["token"],)
        if li == 0 and depth > 1:
            early_pack = _pack([grads[lj][k] for lj in range(1, depth) for k in _SMALL])
            early_small = _split_start("allgather", [early_pack], [jax.ShapeDtypeStruct((8,) + early_pack.shape, F32)],
                                       "small_early_start")
            after = after + (early_small["token"],)
        dx, grads[li] = _layer_bwd(dx, layer_params[li], saved[li], seq, li, after=after, on_ffn_grads=early_swap)
        if pending is not None:
            lj, scatter, chip_f32 = pending
            big_grads[lj] = _finish_reduce(chip_f32, _split_wait(scatter, dx, f"l{lj}_rs_scatter_wait")[1], lj, ci)
        early, early_from_sib = _split_wait(swaps[0], dx, f"l{li}_rs_sib_wait")
        chip_f32, chip_bf16 = _chip_sums(grads[li], early, early_from_sib, li, ci, q)
        lands = [jax.ShapeDtypeStruct((3,) + a.shape[1:], a.dtype) for a in chip_bf16]
        if li > 0:
            pending = (li, _split_start("scatter", chip_bf16, lands, f"l{li}_rs_scatter_start"), chip_f32)
    grad_out, delta_out, m_out, v_out = {}, {}, {}, {}

    small_shapes = [grads[0][k].shape for k in _SMALL]
    me = 2 * q + ci

    def sum8(*blocks):
        acc = blocks[0]
        for b in blocks[1:]:
            acc = acc + b
        return (acc,)

    def total_of(gathered, own, name):
        full = lax.dynamic_update_index_in_dim(gathered, own, me, axis=0)
        return _ew(sum8, [full] * 8, (F32,), name, leads=list(range(8)))[0]

    last_pack = _pack([grads[0][k] for k in _SMALL] + [d_final, loss.reshape(1)])
    last_all = _allgather8(last_pack, "small_allgather")
    chip_bf16, last_all = lax.optimization_barrier((chip_bf16, last_all))
    pending = (0, _split_start("scatter", chip_bf16, lands, "l0_rs_scatter_start"), chip_f32)
    last_total = total_of(last_all, last_pack, "small_sum")
    summed = _unpack(last_total, small_shapes + [d_final.shape, (1,)])
    tail = summed[len(_SMALL):]
    summed = summed[:len(_SMALL)]
    if depth > 1:
        (early_own,), (early_all,) = _split_wait(early_small, dx, "small_early_wait")
        summed += _unpack(total_of(early_all, early_own, "small_early_sum"), small_shapes * (depth - 1))
    summed += tail
    loss_total = summed[-1][0]
    small_grads = {k: jnp.stack([summed[li * len(_SMALL) + i] for li in range(depth)]) for i, k in enumerate(_SMALL)}
    small_grads["final_g"] = summed[-2]
    for k in _CONV_SHARDED:
        n_shard = w[k].shape[-1]
        small_grads[k] = lax.dynamic_slice_in_dim(small_grads[k], q * n_shard, n_shard, axis=2)
    names = _SMALL + ["final_g"]
    shapes = [w[k].shape for k in names]
    packed = [_pack([src[k] for k in names]) for src in (w, small_grads, m, v)]
    outs = _ew(_adam_fn, packed, (F32, F32, F32), "adam_small")
    for dst, o in zip((delta_out, m_out, v_out), outs):
        for k, a in zip(names, _unpack(o, shapes)):
            dst[k] = a
    for k in names:
        grad_out[k] = small_grads[k]

    lj, scatter, chip_f32 = pending
    big_grads[lj] = _finish_reduce(chip_f32, _split_wait(scatter, outs[0], f"l{lj}_rs_scatter_wait")[1], lj, ci)
    for i, k in enumerate(_BIG):
        grad_out[k] = jnp.stack([big_grads[li][i] for li in range(depth)])
        delta_out[k], m_out[k], v_out[k] = _adam(w[k], grad_out[k], m[k], v[k], "adam_" + k)

    return (loss_total, dx.reshape(nseq, seq, d), *[grad_out[k] for k in _WEIGHTS], *[delta_out[k] for k in _WEIGHTS],
            *[m_out[k] for k in _WEIGHTS], *[v_out[k] for k in _WEIGHTS])
```

```python
import functools
import math

import jax
import jax.numpy as jnp
from jax import lax
from jax.experimental import pallas as pl
from jax.experimental.pallas import tpu as pltpu

F32 = jnp.float32
BF16 = jnp.bfloat16
MESH = pl.DeviceIdType.MESH

D_MODEL = 1024
HEAD_DIM = 64
A_WIDTH = 512
B_WIDTH = 512
C_WIDTH = 1024
C_HEADS = 16
CONV_A_K = 31
CONV_C_K = 4
CHUNK = 128
SSM_STATE = 128
D_CONV_C = 1536
D_MAIN = 4608
D_IN_PROJ = 4624
D_MIX = 2048
D_FF = 4096
EPS = 1e-5
NEG = -1e30
LANES = 128
CONV_PAD = 32
N_CHIPS = 4

ADAM_LR = 0.001
ADAM_B1 = 0.9
ADAM_B2 = 0.999
ADAM_EPS = 1e-08
ADAM_WD = 0.01
ADAM_STEP = 10

VMEM_LIMIT = 56 * 1024 * 1024
MATMUL_VMEM_BUDGET = 44 * 1024 * 1024

COL_AVAL, COL_AGATE, COL_BU, COL_BV, COL_Z, COL_XBC = 0, 4, 8, 12, 16, 24


def _cparams(*sem):
    return pltpu.CompilerParams(dimension_semantics=sem, vmem_limit_bytes=VMEM_LIMIT)


_DN = {"nn": (((1,), (0,)), ((), ())), "nt": (((1,), (1,)), ((), ())), "tn": (((0,), (0,)), ((), ()))}


def _dot_raw(a, b, mode):
    return lax.dot_general(a.astype(BF16), b.astype(BF16), _DN[mode], preferred_element_type=F32)


def _make_dot(mode):
    @jax.custom_vjp
    def f(a, b):
        return _dot_raw(a, b, mode)

    def fwd(a, b):
        return _dot_raw(a, b, mode), (a, b)

    def bwd(res, g):
        a, b = res
        if mode == "nn":
            return _dot_raw(g, b, "nt"), _dot_raw(a, g, "tn")
        if mode == "nt":
            return _dot_raw(g, b, "nn"), _dot_raw(g, a, "tn")
        return _dot_raw(b, g, "nt"), _dot_raw(a, g, "nn")

    f.defvjp(fwd, bwd)
    return f


_nn = _make_dot("nn")
_nt = _make_dot("nt")
_tn = _make_dot("tn")


def _iota2(shape, dim):
    return lax.broadcasted_iota(jnp.int32, shape, dim)


def _gmean_impl(x):
    n = x.shape[-1]
    same = (_iota2((n, n), 0) < HEAD_DIM) == (_iota2((n, n), 1) < HEAD_DIM)
    p = jnp.where(same, 1.0 / HEAD_DIM, 0.0).astype(BF16)
    hi = x.astype(BF16)
    lo = (x - hi.astype(F32)).astype(BF16)
    dn = _DN["nn"]
    return (lax.dot_general(hi, p, dn, preferred_element_type=F32)
            + lax.dot_general(lo, p, dn, preferred_element_type=F32))


@jax.custom_vjp
def _gmean(x):
    return _gmean_impl(x)


_gmean.defvjp(lambda x: (_gmean_impl(x), None), lambda _, g: (_gmean_impl(g),))


def _sigmoid(x):
    return 1.0 / (1.0 + jnp.exp(-x))


def _silu(x):
    return x * _sigmoid(x)


def _gelu(x):
    return 0.5 * x * (1.0 + lax.erf(x * 0.7071067811865476))


def _softplus(x):
    return jnp.maximum(x, 0.0) + jnp.log(1.0 + jnp.exp(-jnp.abs(x)))


def _rms(x, g):
    return x * lax.rsqrt(jnp.mean(x * x, axis=-1, keepdims=True) + EPS) * g


def _ln64(x, g, b):
    mu = _gmean(x)
    xc = x - mu
    var = _gmean(xc * xc)
    return xc * lax.rsqrt(var + EPS) * g + b


def _lane_lt64(shape):
    return _iota2(shape, 1) < HEAD_DIM


def _pick(n, pref):
    for t in pref:
        if n % t == 0:
            return t
    return n


_UNREAD = pl.BlockSpec(memory_space=pl.ANY)


def _matmul_tiles(m, n_unit, k, a_item, b_item, out_bytes):
    best = None
    for tm in (1024, 512, 256, 128):
        for tn in (1536, 1024, 768, 512, 256, 128):
            if m % tm or n_unit % tn:
                continue
            need = 2 * k * (tm * a_item + tn * b_item) + 2 * tm * tn * out_bytes
            if need <= MATMUL_VMEM_BUDGET and (best is None or tm * tn > best[0] * best[1]):
                best = (tm, tn)
    assert best is not None, (m, n_unit, k)
    return best


def _matmul(a, b, *, mode, name, add=None, epilogue=None, extra=None, out_dtypes=(F32,), after=(), b_chips=False,
            out_chips=False):
    sh = b.shape[-1] if b_chips else None
    if mode == "nn":
        (m, k), n = a.shape, (N_CHIPS * sh if b_chips else b.shape[1])
    elif mode == "nt":
        (m, k), n = a.shape, b.shape[-2]
    else:
        (k, m), n = a.shape, b.shape[1]
    osh = n // N_CHIPS if out_chips else None
    out_bytes = sum(jnp.dtype(dt).itemsize for dt in out_dtypes) + (0 if add is None else add.dtype.itemsize) \
        + (0 if extra is None else extra.dtype.itemsize)
    tm, tn = _matmul_tiles(m, sh if (b_chips and mode == "nn") else (osh or n), k, a.dtype.itemsize, b.dtype.itemsize,
                           out_bytes)
    a_spec = pl.BlockSpec((k, tm), lambda i, j: (0, i)) if mode == "tn" else pl.BlockSpec((tm, k), lambda i, j: (i, 0))
    if b_chips and mode == "nn":
        per = sh // tn
        b_spec = pl.BlockSpec((None, k, tn), lambda i, j: (j // per, 0, j % per))
    elif b_chips:
        b_spec = pl.BlockSpec((N_CHIPS, tn, sh), lambda i, j: (0, j, 0))
    elif mode == "nt":
        b_spec = pl.BlockSpec((tn, k), lambda i, j: (j, 0))
    else:
        b_spec = pl.BlockSpec((k, tn), lambda i, j: (0, j))
    if out_chips:
        o_per = osh // tn
        o_spec = pl.BlockSpec((None, tm, tn), lambda i, j: (j // o_per, i, j % o_per))
        out_shape = [jax.ShapeDtypeStruct((N_CHIPS, m, osh), dt) for dt in out_dtypes]
    else:
        o_spec = pl.BlockSpec((tm, tn), lambda i, j: (i, j))
        out_shape = [jax.ShapeDtypeStruct((m, n), dt) for dt in out_dtypes]
    ins = [a, b]
    in_specs = [a_spec, b_spec]
    if add is not None:
        ins.append(add)
        in_specs.append(o_spec)
    if extra is not None:
        ins.append(extra)
        in_specs.append(o_spec)
    ins += list(after)
    in_specs += [_UNREAD] * len(after)
    n_out = len(out_dtypes)

    def body(*refs):
        a_ref, b_ref = refs[0], refs[1]
        pos = 2
        add_ref = ex_ref = None
        if add is not None:
            add_ref = refs[pos]
            pos += 1
        if extra is not None:
            ex_ref = refs[pos]
            pos += 1
        pos += len(after)
        if b_chips and mode == "nt":
            acc = _dot_raw(a_ref[:, pl.ds(0, sh)], b_ref[0], mode)
            for chip in range(1, N_CHIPS):
                acc = acc + _dot_raw(a_ref[:, pl.ds(chip * sh, sh)], b_ref[chip], mode)
        else:
            acc = _dot_raw(a_ref[...], b_ref[...], mode)
        if add_ref is not None:
            acc = acc + add_ref[...].astype(F32)
        outs = (acc,) if epilogue is None else epilogue(acc, None if ex_ref is None else ex_ref[...])
        for o_ref, o in zip(refs[pos:pos + n_out], outs):
            o_ref[...] = o.astype(o_ref.dtype)

    res = pl.pallas_call(
        body, name=name, grid=(m // tm, n // tn), in_specs=in_specs, out_specs=[o_spec] * n_out, out_shape=out_shape,
        compiler_params=_cparams("parallel", "parallel"),
    )(*ins)
    return res[0] if n_out == 1 else res


def _relu2_epilogue(acc, _):
    r = jnp.maximum(acc, 0.0)
    return acc, r * r


def _relu2_bwd_epilogue(acc, u):
    return (acc * (2.0 * jnp.maximum(u.astype(F32), 0.0)),)


def _row_tile(t):
    return _pick(t, (512, 256, 128))


def _rms_fwd(x, g, name, after=()):
    t, d = x.shape
    tm = _row_tile(t)

    def body(x_ref, g_ref, *rest):
        o_ref = rest[-1]
        o_ref[...] = _rms(x_ref[...], g_ref[...]).astype(BF16)

    return pl.pallas_call(
        body, name=name, grid=(t // tm,),
        in_specs=[pl.BlockSpec((tm, d), lambda i: (i, 0)), pl.BlockSpec((1, d), lambda i: (0, 0))] + [_UNREAD] * len(after),
        out_specs=pl.BlockSpec((tm, d), lambda i: (i, 0)),
        out_shape=jax.ShapeDtypeStruct((t, d), BF16),
        compiler_params=_cparams("parallel"),
    )(x, g.reshape(1, d), *after)


def _rms_bwd(x, g, dh, dres, name):
    t, d = x.shape
    tm = _row_tile(t)

    def body(x_ref, g_ref, dh_ref, dres_ref, dx_ref, dg_ref):
        @pl.when(pl.program_id(0) == 0)
        def _():
            dg_ref[...] = jnp.zeros_like(dg_ref)

        _, vjp = jax.vjp(_rms, x_ref[...], g_ref[...])
        dx, dg = vjp(dh_ref[...].astype(F32))
        dx_ref[...] = dx + dres_ref[...]
        dg_ref[...] += dg

    row = pl.BlockSpec((tm, d), lambda i: (i, 0))
    vec = pl.BlockSpec((1, d), lambda i: (0, 0))
    dx, dg = pl.pallas_call(
        body, name=name, grid=(t // tm,),
        in_specs=[row, vec, row, row], out_specs=[row, vec],
        out_shape=[jax.ShapeDtypeStruct((t, d), F32), jax.ShapeDtypeStruct((1, d), F32)],
        compiler_params=_cparams("arbitrary"),
    )(x, g.reshape(1, d), dh, dres)
    return dx, dg.reshape(d)


def _loss_head(x, g, target):
    t, d = x.shape
    tm = _row_tile(t)

    def loss_fn(xv, gv, tv):
        err = _rms(xv, gv) - tv
        return 0.5 * jnp.sum(jnp.mean(err * err, axis=-1, keepdims=True))

    def body(x_ref, g_ref, t_ref, loss_ref, dx_ref, dg_ref):
        @pl.when(pl.program_id(0) == 0)
        def _():
            dg_ref[...] = jnp.zeros_like(dg_ref)
            loss_ref[...] = jnp.zeros_like(loss_ref)

        tv = t_ref[...]
        val, vjp = jax.vjp(lambda xv, gv: loss_fn(xv, gv, tv), x_ref[...], g_ref[...])
        dx, dg = vjp(jnp.ones((), F32))
        dx_ref[...] = dx
        dg_ref[...] += dg
        loss_ref[...] += jnp.full(loss_ref.shape, val, F32)

    row = pl.BlockSpec((tm, d), lambda i: (i, 0))
    vec = pl.BlockSpec((1, d), lambda i: (0, 0))
    loss, dx, dg = pl.pallas_call(
        body, name="loss_head", grid=(t // tm,),
        in_specs=[row, vec, row], out_specs=[pl.BlockSpec((1, LANES), lambda i: (0, 0)), row, vec],
        out_shape=[jax.ShapeDtypeStruct((1, LANES), F32), jax.ShapeDtypeStruct((t, d), F32),
                   jax.ShapeDtypeStruct((1, d), F32)],
        compiler_params=_cparams("arbitrary"),
    )(x, g.reshape(1, d), target)
    return loss[0, 0], dx, dg.reshape(d)


def _pre_glu(val, gate):
    return val * _sigmoid(gate)


def _pre_id(x):
    return x


def _post_lnsilu(c, g, b):
    return _silu(_ln64(c, g, b))


def _post_silu(c):
    return _silu(c)


def _conv_cfg(kind):
    if kind == "a":
        return dict(k=CONV_A_K, pre=_pre_glu, post=_post_lnsilu, n_in=2, n_par=2, nblk=A_WIDTH // LANES,
                    cols=(COL_AVAL, COL_AGATE))
    return dict(k=CONV_C_K, pre=_pre_id, post=_post_silu, n_in=1, n_par=0, nblk=D_CONV_C // LANES,
                cols=(COL_XBC,))


def _conv_fwd(kind, proj, w, bias, params, seq, name, out_dtype=F32, keep_conv=False):
    cfg = _conv_cfg(kind)
    kt, pre, post, n_in = cfg["k"], cfg["pre"], cfg["post"], cfg["n_in"]
    t = proj.shape[0]
    nseq = t // seq
    c = cfg["nblk"] * LANES
    rt = min(256, seq)
    nrt = seq // rt
    off0 = CONV_PAD - (kt - 1)

    def body(*refs):
        in_refs = refs[:n_in]
        w_ref, b_ref = refs[n_in], refs[n_in + 1]
        par_refs = refs[n_in + 2:n_in + 2 + cfg["n_par"]]
        out_refs = refs[n_in + 2 + cfg["n_par"]:-1]
        hpad = refs[-1]
        hpad[pl.ds(0, CONV_PAD), :] = jnp.zeros((CONV_PAD, LANES), F32)
        for r in range(nrt):
            hpad[pl.ds(CONV_PAD + r * rt, rt), :] = pre(*[x[pl.ds(r * rt, rt), :] for x in in_refs])
        pars = [p[...] for p in par_refs]
        for r in range(nrt):
            acc = jnp.broadcast_to(b_ref[...], (rt, LANES))
            for k in range(kt):
                acc = acc + w_ref[pl.ds(k, 1), :] * hpad[pl.ds(off0 + k + r * rt, rt), :]
            out_refs[0][pl.ds(r * rt, rt), :] = post(acc, *pars).astype(out_dtype)
            if keep_conv:
                out_refs[1][pl.ds(r * rt, rt), :] = acc

    in_specs = [pl.BlockSpec((seq, LANES), functools.partial(lambda s, j, col: (s, col + j), col=col))
                for col in cfg["cols"]]
    vec = pl.BlockSpec((1, LANES), lambda s, j: (0, j))
    in_specs += [pl.BlockSpec((CONV_PAD, LANES), lambda s, j: (0, j)), vec] + [vec] * cfg["n_par"]
    blk = pl.BlockSpec((seq, LANES), lambda s, j: (s, j))
    res = pl.pallas_call(
        body, name=name, grid=(nseq, cfg["nblk"]),
        in_specs=in_specs, out_specs=[blk, blk] if keep_conv else [blk],
        out_shape=[jax.ShapeDtypeStruct((t, c), out_dtype)] + ([jax.ShapeDtypeStruct((t, c), F32)] if keep_conv else []),
        scratch_shapes=[pltpu.VMEM((seq + CONV_PAD, LANES), F32)],
        compiler_params=_cparams("parallel", "parallel"),
    )(*([proj] * n_in), w, bias, *params)
    return tuple(res) if keep_conv else res[0]


def _conv_bwd(kind, proj, w, bias, params, dy, seq, name, dy_col=0, conv_out=None):
    kept = conv_out is not None
    cfg = _conv_cfg(kind)
    kt, pre, post, n_in, n_par = cfg["k"], cfg["pre"], cfg["post"], cfg["n_in"], cfg["n_par"]
    t = proj.shape[0]
    nseq = t // seq
    c = cfg["nblk"] * LANES
    rt = min(256, seq)
    nrt = seq // rt
    off0 = CONV_PAD - (kt - 1)

    def body(*refs):
        in_refs = refs[:n_in]
        w_ref, b_ref = refs[n_in], refs[n_in + 1]
        par_refs = refs[n_in + 2:n_in + 2 + n_par]
        pos = n_in + 2 + n_par
        dy_ref = refs[pos]
        if kept:
            pos += 1
            conv_ref = refs[pos]
        din_refs = refs[pos + 1:pos + 1 + n_in]
        dw_ref, db_ref = refs[pos + 1 + n_in], refs[pos + 2 + n_in]
        dpar_refs = refs[pos + 3 + n_in:pos + 3 + n_in + n_par]
        hpad, dcpad = refs[pos + 3 + n_in + n_par:]

        @pl.when(pl.program_id(1) == 0)
        def _():
            dw_ref[...] = jnp.zeros_like(dw_ref)
            db_ref[...] = jnp.zeros_like(db_ref)
            for r in dpar_refs:
                r[...] = jnp.zeros_like(r)

        hpad[pl.ds(0, CONV_PAD), :] = jnp.zeros((CONV_PAD, LANES), F32)
        dcpad[pl.ds(seq, CONV_PAD), :] = jnp.zeros((CONV_PAD, LANES), F32)
        for r in range(nrt):
            hpad[pl.ds(CONV_PAD + r * rt, rt), :] = pre(*[x[pl.ds(r * rt, rt), :] for x in in_refs])
        pars = [p[...] for p in par_refs]
        for r in range(nrt):
            if kept:
                acc = conv_ref[pl.ds(r * rt, rt), :]
            else:
                acc = jnp.broadcast_to(b_ref[...], (rt, LANES))
                for k in range(kt):
                    acc = acc + w_ref[pl.ds(k, 1), :] * hpad[pl.ds(off0 + k + r * rt, rt), :]
            _, vjp = jax.vjp(post, acc, *pars)
            grads = vjp(dy_ref[pl.ds(r * rt, rt), :])
            dcpad[pl.ds(r * rt, rt), :] = grads[0]
            db_ref[...] += jnp.sum(grads[0], axis=0, keepdims=True)
            for ref, gpar in zip(dpar_refs, grads[1:]):
                ref[...] += gpar
        for r in range(nrt):
            dh = jnp.zeros((rt, LANES), F32)
            for k in range(kt):
                dh = dh + w_ref[pl.ds(k, 1), :] * dcpad[pl.ds(r * rt + kt - 1 - k, rt), :]
            _, vjp = jax.vjp(pre, *[x[pl.ds(r * rt, rt), :] for x in in_refs])
            for ref, gin in zip(din_refs, vjp(dh)):
                ref[pl.ds(r * rt, rt), :] = gin.astype(ref.dtype)
        for k in range(kt):
            s = jnp.zeros((1, LANES), F32)
            for r in range(nrt):
                s = s + jnp.sum(dcpad[pl.ds(r * rt, rt), :] * hpad[pl.ds(off0 + k + r * rt, rt), :],
                                axis=0, keepdims=True)
            dw_ref[pl.ds(k, 1), :] += s

    in_specs = [pl.BlockSpec((seq, LANES), functools.partial(lambda j, s, col: (s, col + j), col=col))
                for col in cfg["cols"]]
    vec = pl.BlockSpec((1, LANES), lambda j, s: (0, j))
    wspec = pl.BlockSpec((CONV_PAD, LANES), lambda j, s: (0, j))
    blk = pl.BlockSpec((seq, LANES), lambda j, s: (s, j))
    in_specs += [wspec, vec] + [vec] * n_par + [pl.BlockSpec((seq, LANES), lambda j, s: (s, dy_col + j))]
    in_specs += [blk] if kept else []
    out_specs = [blk] * n_in + [wspec, vec] + [vec] * n_par
    out_shape = ([jax.ShapeDtypeStruct((t, c), BF16)] * n_in
                 + [jax.ShapeDtypeStruct((CONV_PAD, c), F32), jax.ShapeDtypeStruct((1, c), F32)]
                 + [jax.ShapeDtypeStruct((1, c), F32)] * n_par)
    res = pl.pallas_call(
        body, name=name, grid=(cfg["nblk"], nseq),
        in_specs=in_specs, out_specs=out_specs, out_shape=out_shape,
        scratch_shapes=[pltpu.VMEM((seq + CONV_PAD, LANES), F32), pltpu.VMEM((seq + CONV_PAD, LANES), F32)],
        compiler_params=_cparams("parallel", "arbitrary"),
    )(*([proj] * n_in), w, bias, *params, dy, *([conv_out] if kept else []))
    return res[:n_in], res[n_in], res[n_in + 1], res[n_in + 2:]


def _gmlp_chunk(bu, bv, g, b, w0, w1, b0row, b1row):
    u = _gelu(bu)
    vn = _ln64(_gelu(bv), g, b)
    tri = _iota2((CHUNK, CHUNK), 0) >= _iota2((CHUNK, CHUNK), 1)
    m0 = _nn(jnp.where(tri, w0, 0.0), vn) + jnp.broadcast_to(b0row, (CHUNK, CHUNK)).T
    m1 = _nn(jnp.where(tri, w1, 0.0), vn) + jnp.broadcast_to(b1row, (CHUNK, CHUNK)).T
    return u * jnp.where(_lane_lt64((CHUNK, LANES)), m0, m1)


def _gmlp_specs(tm, order):
    def im(f):
        return lambda *ids: f(*order(*ids))
    return dict(
        bu=pl.BlockSpec((tm, LANES), im(lambda j, r: (r, COL_BU + j))),
        bv=pl.BlockSpec((tm, LANES), im(lambda j, r: (r, COL_BV + j))),
        vec=pl.BlockSpec((1, LANES), im(lambda j, r: (0, j))),
        ws=pl.BlockSpec((2, CHUNK, CHUNK), im(lambda j, r: (j, 0, 0))),
        bs=pl.BlockSpec((None, 2, CHUNK), im(lambda j, r: (j, 0, 0))),
        blk=pl.BlockSpec((tm, LANES), im(lambda j, r: (r, j))),
    )


def _gmlp_fwd(proj, ln_g, ln_b, w_s, b_s, name):
    t = proj.shape[0]
    tm = _row_tile(t)
    nch = tm // CHUNK
    sp = _gmlp_specs(tm, lambda r, j: (j, r))

    def body(bu_ref, bv_ref, g_ref, b_ref, ws_ref, bs_ref, o_ref):
        for ci in range(nch):
            rows = pl.ds(ci * CHUNK, CHUNK)
            o_ref[rows, :] = _gmlp_chunk(bu_ref[rows, :], bv_ref[rows, :], g_ref[...], b_ref[...], ws_ref[0], ws_ref[1],
                                         bs_ref[pl.ds(0, 1), :], bs_ref[pl.ds(1, 1), :]).astype(BF16)

    return pl.pallas_call(
        body, name=name, grid=(t // tm, B_WIDTH // LANES),
        in_specs=[sp["bu"], sp["bv"], sp["vec"], sp["vec"], sp["ws"], sp["bs"]],
        out_specs=sp["blk"], out_shape=jax.ShapeDtypeStruct((t, B_WIDTH), BF16),
        compiler_params=_cparams("parallel", "parallel"),
    )(proj, proj, ln_g, ln_b, w_s, b_s.reshape(B_WIDTH // LANES, 2, CHUNK))


def _gmlp_bwd(proj, ln_g, ln_b, w_s, b_s, dy, name, dy_col=0):
    t = proj.shape[0]
    tm = _row_tile(t)
    nch = tm // CHUNK
    sp = _gmlp_specs(tm, lambda j, r: (j, r))
    dy_spec = pl.BlockSpec((tm, LANES), lambda j, r: (r, dy_col + j))

    def body(bu_ref, bv_ref, g_ref, b_ref, ws_ref, bs_ref, dy_ref, dbu_ref, dbv_ref, dg_ref, db_ref, dws_ref, dbs_ref):
        @pl.when(pl.program_id(1) == 0)
        def _():
            for r in (dg_ref, db_ref, dws_ref, dbs_ref):
                r[...] = jnp.zeros_like(r)

        for ci in range(nch):
            rows = pl.ds(ci * CHUNK, CHUNK)
            _, vjp = jax.vjp(_gmlp_chunk, bu_ref[rows, :], bv_ref[rows, :], g_ref[...], b_ref[...],
                             ws_ref[0], ws_ref[1], bs_ref[pl.ds(0, 1), :], bs_ref[pl.ds(1, 1), :])
            dbu, dbv, dg, db, dw0, dw1, db0, db1 = vjp(dy_ref[rows, :])
            dbu_ref[rows, :] = dbu.astype(BF16)
            dbv_ref[rows, :] = dbv.astype(BF16)
            dg_ref[...] += dg
            db_ref[...] += db
            dws_ref[0] += dw0
            dws_ref[1] += dw1
            dbs_ref[pl.ds(0, 1), :] += db0
            dbs_ref[pl.ds(1, 1), :] += db1

    nh = B_WIDTH // LANES
    res = pl.pallas_call(
        body, name=name, grid=(nh, t // tm),
        in_specs=[sp["bu"], sp["bv"], sp["vec"], sp["vec"], sp["ws"], sp["bs"], dy_spec],
        out_specs=[sp["blk"], sp["blk"], sp["vec"], sp["vec"], sp["ws"], sp["bs"]],
        out_shape=[jax.ShapeDtypeStruct((t, B_WIDTH), BF16), jax.ShapeDtypeStruct((t, B_WIDTH), BF16),
                   jax.ShapeDtypeStruct((1, B_WIDTH), F32), jax.ShapeDtypeStruct((1, B_WIDTH), F32),
                   jax.ShapeDtypeStruct(w_s.shape, F32), jax.ShapeDtypeStruct((nh, 2, CHUNK), F32)],
        compiler_params=_cparams("parallel", "arbitrary"),
    )(proj, proj, ln_g, ln_b, w_s, b_s.reshape(nh, 2, CHUNK), dy)
    dbu, dbv, dg, db, dws, dbs = res
    return dbu, dbv, dg, db, dws, dbs.reshape(b_s.shape)


def _tri_apply(a, lower):
    l = a.shape[0]
    r, c = _iota2((l, l), 0), _iota2((l, l), 1)
    t = jnp.where((r >= c) if lower else (r <= c), 1.0, 0.0).astype(BF16)
    hi = a.astype(BF16)
    r1 = a - hi.astype(F32)
    mid = r1.astype(BF16)
    lo = (r1 - mid.astype(F32)).astype(BF16)
    dn = _DN["nn"]
    return (lax.dot_general(t, hi, dn, preferred_element_type=F32) + lax.dot_general(t, mid, dn, preferred_element_type=F32)
            + lax.dot_general(t, lo, dn, preferred_element_type=F32))


@jax.custom_vjp
def _cumsum_rows(a):
    return _tri_apply(a, True)


_cumsum_rows.defvjp(lambda a: (_tri_apply(a, True), None), lambda _, g: (_tri_apply(g, False),))

SSD_GROUP_HEADS = 8
SSD_GROUP_PAIRS = 4


def _ssd_group(x0, x1, x2, x3, dt_raw, bias, alog, bm, cm, p0, p1, p2, p3):
    xs, prevs = (x0, x1, x2, x3), (p0, p1, p2, p3)
    dt = _softplus(dt_raw + bias)
    a = dt * (-jnp.exp(alog))
    acs = _cumsum_rows(a)
    alast = jnp.sum(a, axis=0, keepdims=True)
    dt_t, acs_t = dt.T, acs.T
    cb = _nt(cm, bm)
    tri = _iota2((CHUNK, CHUNK), 0) >= _iota2((CHUNK, CHUNK), 1)
    lane = _iota2((CHUNK, LANES), 1)
    sub = _iota2((LANES, CHUNK), 0)
    lane1 = _iota2((1, LANES), 1)

    def column(v, i):
        return jnp.broadcast_to(jnp.sum(jnp.where(lane == i, v, 0.0), axis=1, keepdims=True), (CHUNK, LANES))

    def row(vt, i):
        return jnp.broadcast_to(jnp.sum(jnp.where(sub == i, vt, 0.0), axis=0, keepdims=True), (CHUNK, CHUNK))

    heads = []
    for i in range(SSD_GROUP_HEADS):
        col_a = column(acs, i)
        al = jnp.sum(jnp.where(lane1 == i, alast, 0.0), axis=1, keepdims=True)
        m = cb * jnp.exp(jnp.where(tri, col_a - row(acs_t, i), NEG)) * row(dt_t, i)
        heads.append((m, jnp.exp(col_a), column(dt, i) * jnp.exp(al - col_a), jnp.exp(al)))
    lo_lanes = _lane_lt64((CHUNK, LANES))
    lo_rows = _iota2((LANES, SSM_STATE), 0) < HEAD_DIM
    ys, news = [], []
    for j in range(SSD_GROUP_PAIRS):
        (m0, ea0, w0, cd0), (m1, ea1, w1, cd1) = heads[2 * j], heads[2 * j + 1]
        x, prev = xs[j], prevs[j]
        ydiag = jnp.where(lo_lanes, _nn(m0, x), _nn(m1, x))
        yoff = jnp.where(lo_lanes, _nt(cm * ea0, prev), _nt(cm * ea1, prev))
        states = jnp.where(lo_rows, _tn(x, bm * w0), _tn(x, bm * w1))
        ys.append(ydiag + yoff)
        news.append(prev * jnp.where(lo_rows, cd0, cd1) + states)
    return tuple(ys) + tuple(news)


SSD_GROUPS = 2


def _ssd2_specs(seq, rev):
    ncs = seq // CHUNK
    wide = SSD_GROUPS * LANES

    def row(s, c):
        return s * ncs + (ncs - 1 - c if rev else c)

    return dict(
        x=pl.BlockSpec((CHUNK, C_WIDTH), lambda s, c: (row(s, c), 0)),
        dt=pl.BlockSpec((CHUNK, wide), lambda s, c: (row(s, c), 0)),
        vec=pl.BlockSpec((1, wide), lambda s, c: (0, 0)),
        bm=pl.BlockSpec((CHUNK, wide), lambda s, c: (row(s, c), C_WIDTH // wide)),
        cm=pl.BlockSpec((CHUNK, wide), lambda s, c: (row(s, c), C_WIDTH // wide + 1)),
        st=pl.BlockSpec((None, C_WIDTH // LANES, LANES, SSM_STATE), lambda s, c: (row(s, c), 0, 0, 0)),
        ncs=ncs,
    )


def _lane_blocks(ref, grp):
    return [ref[:, pl.ds((grp * SSD_GROUP_PAIRS + j) * LANES, LANES)] for j in range(SSD_GROUP_PAIRS)]


def _group_block(ref, grp):
    return ref[:, pl.ds(grp * LANES, LANES)]


def _ssd2_fwd(xbc_act, dt_raw, dt_bias, a_log, seq, name):
    t = xbc_act.shape[0]
    sp = _ssd2_specs(seq, False)

    npair = SSD_GROUP_PAIRS

    def body(x_ref, dt_ref, bias_ref, alog_ref, bm_ref, cm_ref, y_ref, prev_ref, state):
        @pl.when(pl.program_id(1) == 0)
        def _():
            state[...] = jnp.zeros_like(state)

        for grp in range(SSD_GROUPS):
            prevs = [state[grp * npair + j] for j in range(npair)]
            for j in range(npair):
                prev_ref[grp * npair + j] = prevs[j]
            res = _ssd_group(*_lane_blocks(x_ref, grp), _group_block(dt_ref, grp), _group_block(bias_ref, grp),
                             _group_block(alog_ref, grp), _group_block(bm_ref, grp), _group_block(cm_ref, grp), *prevs)
            for j in range(npair):
                y_ref[:, pl.ds((grp * npair + j) * LANES, LANES)] = res[j]
                state[grp * npair + j] = res[npair + j]

    return pl.pallas_call(
        body, name=name, grid=(t // seq, sp["ncs"]),
        in_specs=[sp["x"], sp["dt"], sp["vec"], sp["vec"], sp["bm"], sp["cm"]],
        out_specs=[sp["x"], sp["st"]],
        out_shape=[jax.ShapeDtypeStruct((t, C_WIDTH), F32),
                   jax.ShapeDtypeStruct((t // CHUNK, C_WIDTH // LANES, LANES, SSM_STATE), F32)],
        scratch_shapes=[pltpu.VMEM((C_WIDTH // LANES, LANES, SSM_STATE), F32)],
        compiler_params=_cparams("parallel", "arbitrary"),
    )(xbc_act, dt_raw, dt_bias, a_log, xbc_act, xbc_act)


def _ssd2_bwd(xbc_act, dt_raw, dt_bias, a_log, prev_saved, dy, seq, name):
    t = xbc_act.shape[0]
    sp = _ssd2_specs(seq, True)
    npair = SSD_GROUP_PAIRS

    def body(x_ref, dt_ref, bias_ref, alog_ref, bm_ref, cm_ref, prev_ref, dy_ref,
             dx_ref, ddt_ref, dbias_ref, dalog_ref, dbm_ref, dcm_ref, dstate):
        @pl.when(pl.program_id(1) == 0)
        def _():
            dstate[...] = jnp.zeros_like(dstate)

        @pl.when(jnp.logical_and(pl.program_id(0) == 0, pl.program_id(1) == 0))
        def _():
            dbias_ref[...] = jnp.zeros_like(dbias_ref)
            dalog_ref[...] = jnp.zeros_like(dalog_ref)

        for grp in range(SSD_GROUPS):
            lanes = pl.ds(grp * LANES, LANES)
            _, vjp = jax.vjp(_ssd_group, *_lane_blocks(x_ref, grp), _group_block(dt_ref, grp), _group_block(bias_ref, grp),
                             _group_block(alog_ref, grp), _group_block(bm_ref, grp), _group_block(cm_ref, grp),
                             *[prev_ref[grp * npair + j] for j in range(npair)])
            grads = vjp(tuple(_lane_blocks(dy_ref, grp)) + tuple(dstate[grp * npair + j] for j in range(npair)))
            for j in range(npair):
                dx_ref[:, pl.ds((grp * npair + j) * LANES, LANES)] = grads[j]
                dstate[grp * npair + j] = grads[npair + 5 + j]
            ddt_ref[:, lanes] = grads[npair].astype(BF16)
            dbias_ref[:, lanes] += grads[npair + 1]
            dalog_ref[:, lanes] += grads[npair + 2]
            dbm_ref[:, lanes] = grads[npair + 3]
            dcm_ref[:, lanes] = grads[npair + 4]

    return pl.pallas_call(
        body, name=name, grid=(t // seq, sp["ncs"]),
        in_specs=[sp["x"], sp["dt"], sp["vec"], sp["vec"], sp["bm"], sp["cm"], sp["st"], sp["x"]],
        out_specs=[sp["x"], sp["dt"], sp["vec"], sp["vec"], sp["dt"], sp["dt"]],
        out_shape=[jax.ShapeDtypeStruct((t, C_WIDTH), F32), jax.ShapeDtypeStruct((t, 2 * LANES), BF16),
                   jax.ShapeDtypeStruct((1, 2 * LANES), F32), jax.ShapeDtypeStruct((1, 2 * LANES), F32),
                   jax.ShapeDtypeStruct((t, 2 * SSM_STATE), F32), jax.ShapeDtypeStruct((t, 2 * SSM_STATE), F32)],
        scratch_shapes=[pltpu.VMEM((C_WIDTH // LANES, LANES, SSM_STATE), F32)],
        compiler_params=_cparams("arbitrary", "arbitrary"),
    )(xbc_act, dt_raw, dt_bias, a_log, xbc_act, xbc_act, prev_saved, dy)


def _ssd2_assemble(dxs_ssd, dxs_skip, dbm, dcm, name):
    t = dxs_ssd.shape[0]
    tm = _row_tile(t)

    def body(a_ref, b_ref, dbm_ref, dcm_ref, o_ref):
        o_ref[:, pl.ds(0, C_WIDTH)] = a_ref[...] + b_ref[...]
        o_ref[:, pl.ds(C_WIDTH, 2 * SSM_STATE)] = dbm_ref[...]
        o_ref[:, pl.ds(C_WIDTH + 2 * SSM_STATE, 2 * SSM_STATE)] = dcm_ref[...]

    wide = pl.BlockSpec((tm, C_WIDTH), lambda i: (i, 0))
    narrow = pl.BlockSpec((tm, 2 * SSM_STATE), lambda i: (i, 0))
    return pl.pallas_call(
        body, name=name, grid=(t // tm,), in_specs=[wide, wide, narrow, narrow],
        out_specs=pl.BlockSpec((tm, D_CONV_C), lambda i: (i, 0)),
        out_shape=jax.ShapeDtypeStruct((t, D_CONV_C), F32),
        compiler_params=_cparams("parallel"),
    )(dxs_ssd, dxs_skip, dbm, dcm)


def _ssd_post_fn(y, xs, z, dskip, g):
    v = (y + dskip * xs) * _silu(z)
    return v * lax.rsqrt(jnp.mean(v * v, axis=-1, keepdims=True) + EPS) * g


def _ssd_post_specs(tm, order):
    gw = C_WIDTH // 2

    def im(f):
        return lambda *ids: f(*order(*ids))
    return dict(
        blk=pl.BlockSpec((tm, gw), im(lambda g, r: (r, g))),
        z=pl.BlockSpec((tm, gw), im(lambda g, r: (r, COL_Z * LANES // gw + g))),
        vec=pl.BlockSpec((1, gw), im(lambda g, r: (0, g))),
    )


def _ssd_post_fwd(y_ssd, xbc_act, proj, dskip64, norm_g, name):
    t = y_ssd.shape[0]
    tm = _row_tile(t)
    sp = _ssd_post_specs(tm, lambda r, g: (g, r))

    def body(y_ref, xs_ref, z_ref, ds_ref, g_ref, o_ref):
        o_ref[...] = _ssd_post_fn(y_ref[...], xs_ref[...], z_ref[...], ds_ref[...], g_ref[...]).astype(BF16)

    return pl.pallas_call(
        body, name=name, grid=(t // tm, 2),
        in_specs=[sp["blk"], sp["blk"], sp["z"], sp["vec"], sp["vec"]], out_specs=sp["blk"],
        out_shape=jax.ShapeDtypeStruct((t, C_WIDTH), BF16),
        compiler_params=_cparams("parallel", "parallel"),
    )(y_ssd, xbc_act, proj, dskip64, norm_g)


def _ssd_post_bwd(y_ssd, xbc_act, proj, dskip64, norm_g, dyc, name, dy_col=0):
    t = y_ssd.shape[0]
    tm = _row_tile(t)
    sp = _ssd_post_specs(tm, lambda g, r: (g, r))
    dy_spec = pl.BlockSpec((tm, C_WIDTH // 2), lambda g, r: (r, dy_col + g))

    def body(y_ref, xs_ref, z_ref, ds_ref, g_ref, dyc_ref, dy_ref, dxs_ref, dz_ref, dds_ref, dg_ref):
        @pl.when(pl.program_id(1) == 0)
        def _():
            dds_ref[...] = jnp.zeros_like(dds_ref)
            dg_ref[...] = jnp.zeros_like(dg_ref)

        _, vjp = jax.vjp(_ssd_post_fn, y_ref[...], xs_ref[...], z_ref[...], ds_ref[...], g_ref[...])
        dy, dxs, dz, dds, dg = vjp(dyc_ref[...])
        dy_ref[...] = dy
        dxs_ref[...] = dxs
        dz_ref[...] = dz.astype(BF16)
        dds_ref[...] += dds
        dg_ref[...] += dg

    wide = jax.ShapeDtypeStruct((t, C_WIDTH), F32)
    vec = jax.ShapeDtypeStruct((1, C_WIDTH), F32)
    return pl.pallas_call(
        body, name=name, grid=(2, t // tm),
        in_specs=[sp["blk"], sp["blk"], sp["z"], sp["vec"], sp["vec"], dy_spec],
        out_specs=[sp["blk"], sp["blk"], sp["blk"], sp["vec"], sp["vec"]],
        out_shape=[wide, wide, jax.ShapeDtypeStruct((t, C_WIDTH), BF16), vec, vec],
        compiler_params=_cparams("parallel", "arbitrary"),
    )(y_ssd, xbc_act, proj, dskip64, norm_g, dyc)


def _pad_taps(w):
    return jnp.pad(w, ((0, CONV_PAD - w.shape[0]), (0, 0)))


def _group_heads(a):
    pad = [(0, 0)] * (a.ndim - 1) + [(0, LANES - SSD_GROUP_HEADS)]
    return jnp.concatenate([jnp.pad(a[..., :SSD_GROUP_HEADS], pad), jnp.pad(a[..., SSD_GROUP_HEADS:], pad)], axis=-1)


def _ungroup_heads(a):
    return jnp.concatenate([a[..., :SSD_GROUP_HEADS], a[..., LANES:LANES + SSD_GROUP_HEADS]], axis=-1)


def _layer_fwd(x, p, seq, li, after=()):
    n = f"l{li}_"
    h1 = _rms_fwd(x, p["norm1_g"], n + "rms1", after=after)
    proj = _matmul(h1, p["w_main"], mode="nn", name=n + "inproj")
    dt_raw = _matmul(h1, p["w_dt"], mode="nn", name=n + "inproj_dt")
    row = lambda v: v.reshape(1, -1)
    ya, conv_a = _conv_fwd("a", proj, _pad_taps(p["conv_a_w"]), row(p["conv_a_b"]), (row(p["ln_a_g"]), row(p["ln_a_b"])),
                           seq, n + "conva", out_dtype=BF16, keep_conv=True)
    yb = _gmlp_fwd(proj, row(p["ln_b_g"]), row(p["ln_b_b"]), p["w_spatial"], p["b_spatial"], n + "gmlp")
    xbc_act = _conv_fwd("c", proj, _pad_taps(p["conv_c_w"]), row(p["conv_c_b"]), (), seq, n + "convc")
    y_ssd, prev = _ssd2_fwd(xbc_act, dt_raw, _group_heads(row(p["dt_bias"])), _group_heads(row(p["a_log"])), seq, n + "ssd")
    dskip64 = jnp.repeat(p["d_skip"], HEAD_DIM).reshape(1, C_WIDTH)
    yc = _ssd_post_fwd(y_ssd, xbc_act, proj, dskip64, row(p["norm_c_g"]), n + "ssdpost")
    ycat = jnp.concatenate([ya, yb, yc], axis=1)
    x1 = _matmul(ycat, p["w_out"], mode="nn", name=n + "outproj", add=x)
    h2 = _rms_fwd(x1, p["norm2_g"], n + "rms2")
    u, act = _matmul(h2, p["w_ff1"], mode="nn", name=n + "ff1", epilogue=_relu2_epilogue, out_dtypes=(BF16, BF16),
                     b_chips=True)
    x2 = _matmul(act, p["w_ff2"], mode="nn", name=n + "ff2", add=x1)
    saved = dict(x=x, h1=h1, proj=proj, conv_a=conv_a, dt_raw=dt_raw, xbc_act=xbc_act, prev=prev, y_ssd=y_ssd,
                 dskip64=dskip64, ycat=ycat, x1=x1, h2=h2, u=u, act=act)
    return x2, saved


def _layer_bwd(dx2, p, s, seq, li, after=(), on_ffn_grads=None):
    n = f"l{li}_b_"
    row = lambda v: v.reshape(1, -1)
    g = {}
    du = _matmul(dx2, p["w_ff2"], mode="nt", name=n + "ff2_dx", epilogue=_relu2_bwd_epilogue, extra=s["u"],
                 out_dtypes=(BF16,), after=after)
    g["w_ff2"] = _matmul(s["act"], dx2, mode="tn", name=n + "ff2_dw")
    g["w_ff1"] = _matmul(s["h2"], du, mode="tn", name=n + "ff1_dw", out_chips=True)
    dh2 = _matmul(du, p["w_ff1"], mode="nt", name=n + "ff1_dx", b_chips=True)
    dx1, g["norm2_g"] = _rms_bwd(s["x1"], p["norm2_g"], dh2, dx2, n + "rms2")
    g["w_out"] = _matmul(s["ycat"], dx1, mode="tn", name=n + "out_dw")
    dycat = _matmul(dx1, p["w_out"], mode="nt", name=n + "out_dx",
                    after=() if on_ffn_grads is None else on_ffn_grads(g))
    proj = s["proj"]
    (dval, dgate), dwa, dba, (dlag, dlab) = _conv_bwd(
        "a", proj, _pad_taps(p["conv_a_w"]), row(p["conv_a_b"]), (row(p["ln_a_g"]), row(p["ln_a_b"])), dycat, seq,
        n + "conva", dy_col=0, conv_out=s["conv_a"])
    g["conv_a_w"], g["conv_a_b"], g["ln_a_g"], g["ln_a_b"] = dwa[:CONV_A_K], dba[0], dlag[0], dlab[0]
    dbu, dbv, dlbg, dlbb, g["w_spatial"], g["b_spatial"] = _gmlp_bwd(
        proj, row(p["ln_b_g"]), row(p["ln_b_b"]), p["w_spatial"], p["b_spatial"], dycat, n + "gmlp",
        dy_col=A_WIDTH // LANES)
    g["ln_b_g"], g["ln_b_b"] = dlbg[0], dlbb[0]
    dy_ssd, dxs_skip, dz, dds, dncg = _ssd_post_bwd(s["y_ssd"], s["xbc_act"], proj, s["dskip64"], row(p["norm_c_g"]),
                                                    dycat, n + "ssdpost", dy_col=(A_WIDTH + B_WIDTH) * 2 // C_WIDTH)
    g["norm_c_g"] = dncg[0]
    g["d_skip"] = dds.reshape(C_HEADS, HEAD_DIM).sum(axis=1)
    dxs, ddt_raw, ddtb, dalog, dbm, dcm = _ssd2_bwd(
        s["xbc_act"], s["dt_raw"], _group_heads(row(p["dt_bias"])), _group_heads(row(p["a_log"])), s["prev"], dy_ssd, seq,
        n + "ssd")
    g["dt_bias"], g["a_log"] = _ungroup_heads(ddtb)[0], _ungroup_heads(dalog)[0]
    dconv = _ssd2_assemble(dxs, dxs_skip, dbm, dcm, n + "ssdasm")
    (dxbc,), dwc, dbcv, _ = _conv_bwd("c", proj, _pad_taps(p["conv_c_w"]), row(p["conv_c_b"]), (), dconv, seq, n + "convc")
    g["conv_c_w"], g["conv_c_b"] = dwc[:CONV_C_K], dbcv[0]
    dproj = jnp.concatenate([dval, dgate, dbu, dbv, dz, dxbc], axis=1)
    g["w_main"] = _matmul(s["h1"], dproj, mode="tn", name=n + "in_dw")
    g["w_dt"] = _matmul(s["h1"], ddt_raw, mode="tn", name=n + "indt_dw")
    dh1 = _matmul(dproj, p["w_main"], mode="nt", name=n + "in_dx")
    dh1 = _matmul(ddt_raw, p["w_dt"], mode="nt", name=n + "indt_dx", add=dh1)
    dx, g["norm1_g"] = _rms_bwd(s["x"], p["norm1_g"], dh1, dx1, n + "rms1")
    return dx, g


EW_BLOCK_BYTES = 1 << 20


def _ew(fn, ins, out_dtypes, name, leads=None):
    leads = leads or [None] * len(ins)
    rows, c = ins[0].shape[-2:]
    tr = _pick(rows, [t for t in (2048, 1024, 512, 256, 128, 64, 32, 16, 8) if t * c * 4 <= EW_BLOCK_BYTES])
    n_in = len(ins)

    def spec(lead):
        if lead is None:
            return pl.BlockSpec((tr, c), lambda i: (i, 0))
        return pl.BlockSpec((None, tr, c), functools.partial(lambda i, k: (k, i, 0), k=lead))

    def body(*refs):
        outs = fn(*[r[...].astype(F32) for r in refs[:n_in]])
        for o_ref, o in zip(refs[n_in:], outs):
            o_ref[...] = o.astype(o_ref.dtype)

    return pl.pallas_call(
        body, name=name, grid=(rows // tr,),
        in_specs=[spec(l) for l in leads], out_specs=[spec(None)] * len(out_dtypes),
        out_shape=[jax.ShapeDtypeStruct((rows, c), dt) for dt in out_dtypes],
        compiler_params=_cparams("parallel"),
    )(*ins)


def _adam_fn(w, g, m, v):
    m2 = ADAM_B1 * m + (1.0 - ADAM_B1) * g
    v2 = ADAM_B2 * v + (1.0 - ADAM_B2) * (g * g)
    m_hat = m2 / (1.0 - ADAM_B1 ** ADAM_STEP)
    v_hat = v2 / (1.0 - ADAM_B2 ** ADAM_STEP)
    delta = -ADAM_LR * (m_hat / (jnp.sqrt(v_hat) + ADAM_EPS) + ADAM_WD * w)
    return delta, m2, v2


def _adam(w, g, m, v, name):
    shape = w.shape
    two_d = lambda a: a.reshape(-1, shape[-1])
    outs = _ew(_adam_fn, [two_d(w), two_d(g), two_d(m), two_d(v)], (F32, F32, F32), name)
    return [o.reshape(shape) for o in outs]


_ANY = pl.BlockSpec(memory_space=pl.ANY)


def _mesh_pos():
    return lax.axis_index("x"), lax.axis_index("y"), lax.axis_index("c")


def _peer_chips(x, y):
    return [(1 - x, y), (x, 1 - y), (1 - x, 1 - y)]


def _remote(src, dst, send_sems, recv_sems, sem, to):
    return pltpu.make_async_remote_copy(src_ref=src, dst_ref=dst, send_sem=send_sems.at[sem],
                                        recv_sem=recv_sems.at[sem], device_id=to, device_id_type=MESH)


def _half_rows(n_rows, which):
    half = n_rows // 2
    return pl.ds(pl.multiple_of(which * half, 8), half)


def _comm_call(body, ins, out_shapes, n_sems, name):
    scratch = [pltpu.SemaphoreType.DMA((n_sems,)), pltpu.SemaphoreType.DMA((n_sems,))]
    return pl.pallas_call(
        body, name=name, in_specs=[_ANY] * len(ins), out_specs=[_ANY] * len(out_shapes),
        out_shape=out_shapes, scratch_shapes=scratch,
    )(*ins)


def _gather_weights(big, small, name):
    nb, ns = len(big), len(small)
    n = nb + ns

    def body(*refs):
        ins, outs = refs[:n], refs[n:2 * n]
        send_sems, recv_sems = refs[2 * n:]
        x, y, c = _mesh_pos()
        q = 2 * x + y
        me, sib = (x, y, c), (x, y, 1 - c)
        chips = _peer_chips(x, y)
        rem = functools.partial(_remote, send_sems=send_sems, recv_sems=recv_sems)
        first = []
        for i in range(nb):
            mine = _half_rows(big[i].shape[0], c)
            for k, (px, py) in enumerate(chips):
                first.append(rem(ins[i].at[mine], outs[i].at[q, mine], sem=6 * i + k, to=(px, py, c)))
        for j in range(ns):
            for k, (px, py) in enumerate(chips):
                first.append(rem(ins[nb + j], outs[nb + j].at[q], sem=6 * nb + 3 * j + k, to=(px, py, c)))
        for cp in first:
            cp.start()
        passed = []
        for i in range(nb):
            mine = _half_rows(big[i].shape[0], c)
            for k, (px, py) in enumerate(chips):
                landed = outs[i].at[2 * px + py, mine]
                rem(landed, landed, sem=6 * i + k, to=me).wait_recv()
                fwd = rem(landed, landed, sem=6 * i + 3 + k, to=sib)
                fwd.start()
                passed.append(fwd)
        for i in range(nb):
            other = _half_rows(big[i].shape[0], 1 - c)
            for k, (px, py) in enumerate(chips):
                theirs = outs[i].at[2 * px + py, other]
                rem(theirs, theirs, sem=6 * i + 3 + k, to=me).wait_recv()
        for j in range(ns):
            for k, (px, py) in enumerate(chips):
                dst = outs[nb + j].at[2 * px + py]
                rem(dst, dst, sem=6 * nb + 3 * j + k, to=me).wait_recv()
        for cp in first + passed:
            cp.wait_send()

    out_shapes = [jax.ShapeDtypeStruct((N_CHIPS,) + a.shape, a.dtype) for a in list(big) + list(small)]
    return _comm_call(body, list(big) + list(small), out_shapes, 6 * nb + 3 * ns, name)


def _sibling_other_halves(gs, name):
    n = len(gs)

    def other_half(ref, shape, c):
        rows = _half_rows(shape[-2], 1 - c)
        return ref.at[rows] if len(shape) == 2 else ref.at[:, rows]

    def body(*refs):
        ins, outs = refs[:n], refs[n:2 * n]
        send_sems, recv_sems = refs[2 * n:]
        x, y, c = _mesh_pos()
        copies = [_remote(other_half(ins[i], gs[i].shape, c), outs[i], send_sems, recv_sems, i, (x, y, 1 - c))
                  for i in range(n)]
        for cp in copies:
            cp.start()
        for cp in copies:
            cp.wait()

    out_shapes = [jax.ShapeDtypeStruct(g.shape[:-2] + (g.shape[-2] // 2, g.shape[-1]), g.dtype) for g in gs]
    return _comm_call(body, list(gs), out_shapes, n, name)


IN_SHARD = D_IN_PROJ // N_CHIPS


def _chipsum_in(mine, mine_dt, theirs, theirs_dt, name):
    r = mine.shape[0]
    tr = _pick(r, (128, 64, 32, 16, 8))
    last = D_MAIN - (N_CHIPS - 1) * IN_SHARD

    def body(a_ref, adt_ref, b_ref, bdt_ref, o32_ref, o16_ref):
        for p in range(N_CHIPS):
            wid = IN_SHARD if p < N_CHIPS - 1 else last
            s = a_ref[:, pl.ds(IN_SHARD * p, wid)] + b_ref[:, pl.ds(IN_SHARD * p, wid)]
            o32_ref[p, :, pl.ds(0, wid)] = s
            o16_ref[p, :, pl.ds(0, wid)] = s.astype(BF16)
        for grp in range(2):
            src = pl.ds(grp * LANES, SSD_GROUP_HEADS)
            s = adt_ref[:, src] + bdt_ref[:, src]
            dst = pl.ds(last + grp * SSD_GROUP_HEADS, SSD_GROUP_HEADS)
            o32_ref[N_CHIPS - 1, :, dst] = s
            o16_ref[N_CHIPS - 1, :, dst] = s.astype(BF16)

    wide = pl.BlockSpec((tr, D_MAIN), lambda i: (i, 0))
    narrow = pl.BlockSpec((tr, 2 * LANES), lambda i: (i, 0))
    out = pl.BlockSpec((N_CHIPS, tr, IN_SHARD), lambda i: (0, i, 0))
    return pl.pallas_call(
        body, name=name, grid=(r // tr,), in_specs=[wide, narrow, wide, narrow], out_specs=[out, out],
        out_shape=[jax.ShapeDtypeStruct((N_CHIPS, r, IN_SHARD), F32), jax.ShapeDtypeStruct((N_CHIPS, r, IN_SHARD), BF16)],
        compiler_params=_cparams("parallel"),
    )(mine, mine_dt, theirs, theirs_dt)


_HBM = pl.BlockSpec(memory_space=pltpu.HBM)
_SEM = pl.BlockSpec(memory_space=pltpu.SEMAPHORE)


def _in_hbm(a):
    return pltpu.with_memory_space_constraint(a, pltpu.HBM)


def _split_plan(kind, srcs, lands, x, y, c):
    plan = []
    for src, land in zip(srcs, lands):
        if kind == "sibling":
            rows = _half_rows(src.shape[-2], 1 - c)
            plan.append((src.at[rows] if len(src.shape) == 2 else src.at[:, rows], land, (x, y, 1 - c)))
            continue
        if kind == "allgather":
            peers = [(x, y, 1 - c)] + [(px, py, pc) for px, py in _peer_chips(x, y) for pc in (c, 1 - c)]
            plan += [(src, land.at[4 * x + 2 * y + c], peer) for peer in peers]
            continue
        for k, (px, py) in enumerate(_peer_chips(x, y)):
            if kind == "scatter":
                plan.append((src.at[2 * px + py], land.at[k], (px, py, c)))
            else:
                plan.append((src, land.at[2 * x + y], (px, py, c)))
    return plan


def _split_start(kind, srcs, land_shapes, name):
    n = len(srcs)

    def body(*refs):
        ins, lands = refs[:n], refs[n:2 * n]
        send_sems, recv_sems = refs[2 * n], refs[2 * n + 1]
        token = refs[-1]
        x, y, c = _mesh_pos()
        for i, (src, dst, to) in enumerate(_split_plan(kind, ins, lands, x, y, c)):
            pltpu.make_async_remote_copy(src_ref=src, dst_ref=dst, send_sem=send_sems.at[i], recv_sem=recv_sems.at[i],
                                         device_id=to, device_id_type=MESH).start()
        token[...] = jnp.zeros_like(token)

    zones = [lax.empty(s.shape, s.dtype) for s in land_shapes]
    n_sems = {"sibling": 1, "allgather": 7}.get(kind, 3) * n
    res = pl.pallas_call(
        body, name=name,
        out_shape=(pltpu.SemaphoreType.DMA((n_sems,)), pltpu.SemaphoreType.DMA((n_sems,)),
                   *[pltpu.HBM(a.shape, a.dtype) for a in srcs], *[pltpu.HBM(s.shape, s.dtype) for s in land_shapes],
                   jax.ShapeDtypeStruct((8, LANES), F32)),
        in_specs=[_HBM] * (2 * n), out_specs=(_SEM, _SEM, *[_HBM] * (2 * n), pl.BlockSpec(memory_space=pltpu.VMEM)),
        input_output_aliases={i: 2 + i for i in range(2 * n)},
        compiler_params=pltpu.CompilerParams(has_side_effects=pltpu.SideEffectType.DATAFLOW_SIDE_EFFECTING),
    )(*[_in_hbm(a) for a in srcs], *[_in_hbm(z) for z in zones])
    return dict(send=res[0], recv=res[1], srcs=list(res[2:2 + n]), lands=list(res[2 + n:2 + 2 * n]), token=res[-1], kind=kind)


def _split_wait(started, after, name):
    n = len(started["srcs"])
    kind = started["kind"]

    def body(*refs):
        ins, lands = refs[:n], refs[n:2 * n]
        send_sems, recv_sems = refs[2 * n], refs[2 * n + 1]
        x, y, c = _mesh_pos()
        for i, (src, dst, _) in enumerate(_split_plan(kind, ins, lands, x, y, c)):
            cp = pltpu.make_async_remote_copy(src_ref=src, dst_ref=dst, send_sem=send_sems.at[i], recv_sem=recv_sems.at[i],
                                              device_id=(x, y, c), device_id_type=MESH)
            cp.wait_send()
            cp.wait_recv()

    arrs = started["srcs"] + started["lands"]
    res = pl.pallas_call(
        body, name=name, out_shape=tuple(pltpu.HBM(a.shape, a.dtype) for a in arrs),
        in_specs=[_HBM] * (2 * n) + [_SEM, _SEM, pl.BlockSpec(memory_space=pl.ANY)], out_specs=tuple([_HBM] * (2 * n)),
        input_output_aliases={i: i for i in range(2 * n)},
        compiler_params=pltpu.CompilerParams(has_side_effects=pltpu.SideEffectType.DATAFLOW_SIDE_EFFECTING),
    )(*arrs, started["send"], started["recv"], after)
    return list(res[:n]), list(res[n:])


def _sibling_share(fs, name):
    n = len(fs)

    def body(*refs):
        ins, outs = refs[:n], refs[n:2 * n]
        send_sems, recv_sems = refs[2 * n:]
        x, y, c = _mesh_pos()
        copies = [_remote(ins[i], outs[i], send_sems, recv_sems, i, (x, y, 1 - c)) for i in range(n)]
        for cp in copies:
            cp.start()
        for cp in copies:
            cp.wait()

    out_shapes = [jax.ShapeDtypeStruct(a.shape, a.dtype) for a in fs]
    return _comm_call(body, list(fs), out_shapes, n, name)


def _allgather8(v, name, after=()):
    m = v.shape[0]

    def body(v_ref, *rest):
        out_ref, send_sems, recv_sems = rest[len(after):]
        x, y, c = _mesh_pos()
        me, sib = (x, y, c), (x, y, 1 - c)
        chips = _peer_chips(x, y)
        rem = functools.partial(_remote, send_sems=send_sems, recv_sems=recv_sems)

        def blk(px, py, pc):
            return out_ref.at[4 * px + 2 * py + pc]

        first = [rem(v_ref, blk(*me), sem=0, to=sib)]
        first += [rem(v_ref, blk(*me), sem=1 + k, to=(px, py, c)) for k, (px, py) in enumerate(chips)]
        for cp in first:
            cp.start()
        passed = []
        for k, (px, py) in enumerate(chips):
            landed = blk(px, py, c)
            rem(landed, landed, sem=1 + k, to=me).wait_recv()
            fwd = rem(landed, landed, sem=4 + k, to=sib)
            fwd.start()
            passed.append(fwd)
        rem(blk(*sib), blk(*sib), sem=0, to=me).wait_recv()
        for k, (px, py) in enumerate(chips):
            theirs = blk(px, py, 1 - c)
            rem(theirs, theirs, sem=4 + k, to=me).wait_recv()
        for cp in first + passed:
            cp.wait_send()

    return _comm_call(body, [v, *after], [jax.ShapeDtypeStruct((8, m, LANES), v.dtype)], 7, name)[0]


_WEIGHTS = ["norm1_g", "w_in", "conv_a_w", "conv_a_b", "ln_a_g", "ln_a_b", "ln_b_g", "ln_b_b", "w_spatial", "b_spatial",
            "conv_c_w", "conv_c_b", "dt_bias", "a_log", "d_skip", "norm_c_g", "w_out", "norm2_g", "w_ff1", "w_ff2", "final_g"]
_BIG = ["w_in", "w_out", "w_ff1", "w_ff2"]
_CONV_SHARDED = ["conv_a_w", "conv_c_w"]
_SMALL = [w for w in _WEIGHTS if w not in _BIG and w != "final_g"]
_PACK_ROWS = 512


def _pack(arrs):
    flat = jnp.concatenate([a.reshape(-1) for a in arrs])
    blk = _PACK_ROWS * LANES
    n = flat.shape[0]
    return jnp.pad(flat, (0, -(-n // blk) * blk - n)).reshape(-1, LANES)


def _unpack(packed, shapes):
    flat = packed.reshape(-1)
    out, off = [], 0
    for s in shapes:
        n = math.prod(s)
        out.append(flat[off:off + n].reshape(s))
        off += n
    return out


def _chips_to_cols(a):
    return a.transpose(1, 0, 2).reshape(a.shape[1], -1)


def _own_shards(w, li):
    return [w[k][li].astype(BF16) for k in _BIG] + [w[k][li] for k in _CONV_SHARDED]


def _assemble_w_in(gathered, own, name):
    k = own.shape[0]
    tr = _pick(k, (256, 128, 64, 32, 16))
    last = D_MAIN - (N_CHIPS - 1) * IN_SHARD

    def body(g_ref, own_ref, main_ref, dt_ref):
        x, y, _ = _mesh_pos()
        q = 2 * x + y
        dt_ref[...] = jnp.zeros_like(dt_ref)

        def place(read):
            for p in range(N_CHIPS):
                def _(p=p):
                    wid = IN_SHARD if p < N_CHIPS - 1 else last
                    main_ref[:, pl.ds(IN_SHARD * p, wid)] = read(p, pl.ds(0, wid))
                    if p == N_CHIPS - 1:
                        for grp in range(2):
                            dt_ref[:, pl.ds(grp * LANES, SSD_GROUP_HEADS)] = read(
                                p, pl.ds(last + grp * SSD_GROUP_HEADS, SSD_GROUP_HEADS))
                yield p, _

        for p, put in place(lambda p, cols: own_ref[:, cols]):
            pl.when(q == p)(put)
        for p, put in place(lambda p, cols: g_ref[p, :, cols]):
            pl.when(q != p)(put)

    return pl.pallas_call(
        body, name=name, grid=(k // tr,),
        in_specs=[pl.BlockSpec((N_CHIPS, tr, IN_SHARD), lambda i: (0, i, 0)), pl.BlockSpec((tr, IN_SHARD), lambda i: (i, 0))],
        out_specs=[pl.BlockSpec((tr, D_MAIN), lambda i: (i, 0)), pl.BlockSpec((tr, 2 * LANES), lambda i: (i, 0))],
        out_shape=[jax.ShapeDtypeStruct((k, D_MAIN), own.dtype), jax.ShapeDtypeStruct((k, 2 * LANES), own.dtype)],
        compiler_params=_cparams("parallel"),
    )(gathered, own)


def _layer_params(w, li, own, gathered, q):
    g_out, g_ff1, g_ff2, g_ca, g_cc = [lax.dynamic_update_index_in_dim(g, o, q, axis=0)
                                       for g, o in zip(gathered[1:], own[1:])]
    p = {k: w[k][li] for k in _SMALL if k not in _CONV_SHARDED}
    p["w_main"], p["w_dt"] = _assemble_w_in(gathered[0], own[0], f"l{li}_w_in")
    p["w_out"] = g_out.reshape(D_MIX, D_MODEL)
    p["w_ff1"] = g_ff1
    p["w_ff2"] = g_ff2.reshape(D_FF, D_MODEL)
    p["conv_a_w"] = _chips_to_cols(g_ca)
    p["conv_c_w"] = _chips_to_cols(g_cc)
    return p


def _ffn_out_grads(g):
    return [g["w_out"].reshape(N_CHIPS, -1, D_MODEL), g["w_ff1"], g["w_ff2"].reshape(N_CHIPS, -1, D_MODEL)]


def _half_shape(a):
    return jax.ShapeDtypeStruct(a.shape[:-2] + (a.shape[-2] // 2, a.shape[-1]), a.dtype)


def _chip_sums(g, early, early_from_sib, li, c, q):
    n = f"l{li}_rs_"
    late = [g["w_main"], g["w_dt"]]
    full = late + list(early)
    from_sib = list(_sibling_other_halves(late, n + "sib")) + list(early_from_sib)
    mine = [lax.dynamic_slice_in_dim(a, c * b.shape[-2], b.shape[-2], axis=a.ndim - 2) for a, b in zip(full, from_sib)]
    sums = [_chipsum_in(mine[0], mine[1], from_sib[0], from_sib[1], n + "chipsum0")]
    for i in range(2, len(full)):
        shape = from_sib[i].shape
        s32, s16 = _ew(lambda u, v: (u + v, u + v), [mine[i].reshape(-1, shape[-1]), from_sib[i].reshape(-1, shape[-1])],
                       (F32, BF16), n + f"chipsum{i - 1}")
        sums.append((s32.reshape(shape), s16.reshape(shape)))
    chip_f32 = [lax.dynamic_index_in_dim(s32, q, axis=0, keepdims=False) for s32, _ in sums]
    chip_bf16 = [s16 for _, s16 in sums]
    return chip_f32, chip_bf16


def _finish_reduce(chip_f32, from_chips, li, c):
    n = f"l{li}_rs_"
    halves = [_ew(lambda o, r0, r1, r2_: (((o + r0) + r1) + r2_,), [own, rb, rb, rb], (F32,), n + f"final{i}",
                  leads=[None, 0, 1, 2])[0] for i, (own, rb) in enumerate(zip(chip_f32, from_chips))]
    from_sib = _sibling_share(halves, n + "share")
    return [jnp.where(c == 0, jnp.concatenate([h, s], axis=0), jnp.concatenate([s, h], axis=0))
            for h, s in zip(halves, from_sib)]


def kernel(x, norm1_g, w_in, conv_a_w, conv_a_b, ln_a_g, ln_a_b, ln_b_g, ln_b_b, w_spatial, b_spatial, conv_c_w, conv_c_b, dt_bias, a_log, d_skip, norm_c_g, w_out, norm2_g, w_ff1, w_ff2, final_g, loss_target, m_norm1_g, m_w_in, m_conv_a_w, m_conv_a_b, m_ln_a_g, m_ln_a_b, m_ln_b_g, m_ln_b_b, m_w_spatial, m_b_spatial, m_conv_c_w, m_conv_c_b, m_dt_bias, m_a_log, m_d_skip, m_norm_c_g, m_w_out, m_norm2_g, m_w_ff1, m_w_ff2, m_final_g, v_norm1_g, v_w_in, v_conv_a_w, v_conv_a_b, v_ln_a_g, v_ln_a_b, v_ln_b_g, v_ln_b_b, v_w_spatial, v_b_spatial, v_conv_c_w, v_conv_c_b, v_dt_bias, v_a_log, v_d_skip, v_norm_c_g, v_w_out, v_norm2_g, v_w_ff1, v_w_ff2, v_final_g):
    given = dict(locals())
    w = {k: given[k] for k in _WEIGHTS}
    m = {k: given["m_" + k] for k in _WEIGHTS}
    v = {k: given["v_" + k] for k in _WEIGHTS}
    depth = w_in.shape[0]
    nseq, seq, d = x.shape
    xi, yi, ci = _mesh_pos()
    q = 2 * xi + yi

    own = [_own_shards(w, li) for li in range(depth)]
    nb = len(_BIG)
    gathered = _gather_weights(own[0][:nb], own[0][nb:], "l0_gather")
    h = x.reshape(nseq * seq, d)
    layer_params, saved = [], []
    for li in range(depth):
        nxt = None
        if li + 1 < depth:
            srcs, _ = lax.optimization_barrier((own[li + 1], gathered))
            zones = [jax.ShapeDtypeStruct((N_CHIPS,) + a.shape, a.dtype) for a in srcs]
            nxt = _split_start("gather", srcs, zones, f"l{li + 1}_gather_start")
        layer_params.append(_layer_params(w, li, own[li], gathered, q))
        h, s = _layer_fwd(h, layer_params[li], seq, li, after=() if nxt is None else (nxt["token"],))
        saved.append(s)
        if nxt is not None:
            own[li + 1], gathered = _split_wait(nxt, h, f"l{li + 1}_gather_wait")
    loss, dx, d_final = _loss_head(h, final_g, loss_target.reshape(nseq * seq, d))

    grads = [None] * depth
    big_grads = [None] * depth
    pending = None
    for li in reversed(range(depth)):
        swaps = []

        def early_swap(g, li=li, swaps=swaps):
            early = _ffn_out_grads(g)
            swaps.append(_split_start("sibling", early, [_half_shape(a) for a in early], f"l{li}_rs_sib_start"))
            return (swaps[0]["token"],)

        after = () if pending is None else (pending[1]["token"],)
        if li == 0 and depth > 1:
            early_pack = _pack([grads[lj][k] for lj in range(1, depth) for k in _SMALL])
            early_small = _split_start("allgather", [early_pack], [jax.ShapeDtypeStruct((8,) + early_pack.shape, F32)],
                                       "small_early_start")
            after = after + (early_small["token"],)
        dx, grads[li] = _layer_bwd(dx, layer_params[li], saved[li], seq, li, after=after, on_ffn_grads=early_swap)
        if pending is not None:
            lj, scatter, chip_f32 = pending
            big_grads[lj] = _finish_reduce(chip_f32, _split_wait(scatter, dx, f"l{lj}_rs_scatter_wait")[1], lj, ci)
        early, early_from_sib = _split_wait(swaps[0], dx, f"l{li}_rs_sib_wait")
        chip_f32, chip_bf16 = _chip_sums(grads[li], early, early_from_sib, li, ci, q)
        lands = [jax.ShapeDtypeStruct((3,) + a.shape[1:], a.dtype) for a in chip_bf16]
        pending = (li, _split_start("scatter", chip_bf16, lands, f"l{li}_rs_scatter_start"), chip_f32)
    grad_out, delta_out, m_out, v_out = {}, {}, {}, {}

    small_shapes = [grads[0][k].shape for k in _SMALL]
    me = 2 * q + ci

    def sum8(*blocks):
        acc = blocks[0]
        for b in blocks[1:]:
            acc = acc + b
        return (acc,)

    def total_of(gathered, own, name):
        full = lax.dynamic_update_index_in_dim(gathered, own, me, axis=0)
        return _ew(sum8, [full] * 8, (F32,), name, leads=list(range(8)))[0]

    last_pack = _pack([grads[0][k] for k in _SMALL] + [d_final, loss.reshape(1)])
    last_total = total_of(_allgather8(last_pack, "small_allgather", after=(pending[1]["token"],)), last_pack, "small_sum")
    summed = _unpack(last_total, small_shapes + [d_final.shape, (1,)])
    tail = summed[len(_SMALL):]
    summed = summed[:len(_SMALL)]
    if depth > 1:
        (early_own,), (early_all,) = _split_wait(early_small, dx, "small_early_wait")
        summed += _unpack(total_of(early_all, early_own, "small_early_sum"), small_shapes * (depth - 1))
    summed += tail
    loss_total = summed[-1][0]
    small_grads = {k: jnp.stack([summed[li * len(_SMALL) + i] for li in range(depth)]) for i, k in enumerate(_SMALL)}
    small_grads["final_g"] = summed[-2]
    for k in _CONV_SHARDED:
        n_shard = w[k].shape[-1]
        small_grads[k] = lax.dynamic_slice_in_dim(small_grads[k], q * n_shard, n_shard, axis=2)
    names = _SMALL + ["final_g"]
    shapes = [w[k].shape for k in names]
    packed = [_pack([src[k] for k in names]) for src in (w, small_grads, m, v)]
    outs = _ew(_adam_fn, packed, (F32, F32, F32), "adam_small")
    for dst, o in zip((delta_out, m_out, v_out), outs):
        for k, a in zip(names, _unpack(o, shapes)):
            dst[k] = a
    for k in names:
        grad_out[k] = small_grads[k]

    lj, scatter, chip_f32 = pending
    big_grads[lj] = _finish_reduce(chip_f32, _split_wait(scatter, outs[0], f"l{lj}_rs_scatter_wait")[1], lj, ci)
    for i, k in enumerate(_BIG):
        grad_out[k] = jnp.stack([big_grads[li][i] for li in range(depth)])
        delta_out[k], m_out[k], v_out[k] = _adam(w[k], grad_out[k], m[k], v[k], "adam_" + k)

    return (loss_total, dx.reshape(nseq, seq, d), *[grad_out[k] for k in _WEIGHTS], *[delta_out[k] for k in _WEIGHTS],
            *[m_out[k] for k in _WEIGHTS], *[v_out[k] for k in _WEIGHTS])
```

```python
import functools
import math

import jax
import jax.numpy as jnp
from jax import lax
from jax.experimental import pallas as pl
from jax.experimental.pallas import tpu as pltpu

F32 = jnp.float32
BF16 = jnp.bfloat16
MESH = pl.DeviceIdType.MESH

D_MODEL = 1024
HEAD_DIM = 64
A_WIDTH = 512
B_WIDTH = 512
C_WIDTH = 1024
C_HEADS = 16
CONV_A_K = 31
CONV_C_K = 4
CHUNK = 128
SSM_STATE = 128
D_CONV_C = 1536
D_MAIN = 4608
D_IN_PROJ = 4624
D_MIX = 2048
D_FF = 4096
EPS = 1e-5
NEG = -1e30
LANES = 128
CONV_PAD = 32
N_CHIPS = 4

ADAM_LR = 0.001
ADAM_B1 = 0.9
ADAM_B2 = 0.999
ADAM_EPS = 1e-08
ADAM_WD = 0.01
ADAM_STEP = 10

VMEM_LIMIT = 56 * 1024 * 1024
MATMUL_VMEM_BUDGET = 44 * 1024 * 1024

COL_AVAL, COL_AGATE, COL_BU, COL_BV, COL_Z, COL_XBC = 0, 4, 8, 12, 16, 24


def _cparams(*sem):
    return pltpu.CompilerParams(dimension_semantics=sem, vmem_limit_bytes=VMEM_LIMIT)


_DN = {"nn": (((1,), (0,)), ((), ())), "nt": (((1,), (1,)), ((), ())), "tn": (((0,), (0,)), ((), ()))}


def _dot_raw(a, b, mode):
    return lax.dot_general(a.astype(BF16), b.astype(BF16), _DN[mode], preferred_element_type=F32)


def _make_dot(mode):
    @jax.custom_vjp
    def f(a, b):
        return _dot_raw(a, b, mode)

    def fwd(a, b):
        return _dot_raw(a, b, mode), (a, b)

    def bwd(res, g):
        a, b = res
        if mode == "nn":
            return _dot_raw(g, b, "nt"), _dot_raw(a, g, "tn")
        if mode == "nt":
            return _dot_raw(g, b, "nn"), _dot_raw(g, a, "tn")
        return _dot_raw(b, g, "nt"), _dot_raw(a, g, "nn")

    f.defvjp(fwd, bwd)
    return f


_nn = _make_dot("nn")
_nt = _make_dot("nt")
_tn = _make_dot("tn")


def _iota2(shape, dim):
    return lax.broadcasted_iota(jnp.int32, shape, dim)


def _gmean_impl(x):
    n = x.shape[-1]
    same = (_iota2((n, n), 0) < HEAD_DIM) == (_iota2((n, n), 1) < HEAD_DIM)
    p = jnp.where(same, 1.0 / HEAD_DIM, 0.0).astype(BF16)
    hi = x.astype(BF16)
    lo = (x - hi.astype(F32)).astype(BF16)
    dn = _DN["nn"]
    return (lax.dot_general(hi, p, dn, preferred_element_type=F32)
            + lax.dot_general(lo, p, dn, preferred_element_type=F32))


@jax.custom_vjp
def _gmean(x):
    return _gmean_impl(x)


_gmean.defvjp(lambda x: (_gmean_impl(x), None), lambda _, g: (_gmean_impl(g),))


def _sigmoid(x):
    return 1.0 / (1.0 + jnp.exp(-x))


def _silu(x):
    return x * _sigmoid(x)


def _gelu(x):
    return 0.5 * x * (1.0 + lax.erf(x * 0.7071067811865476))


def _softplus(x):
    return jnp.maximum(x, 0.0) + jnp.log(1.0 + jnp.exp(-jnp.abs(x)))


def _rms(x, g):
    return x * lax.rsqrt(jnp.mean(x * x, axis=-1, keepdims=True) + EPS) * g


def _ln64(x, g, b):
    mu = _gmean(x)
    xc = x - mu
    var = _gmean(xc * xc)
    return xc * lax.rsqrt(var + EPS) * g + b


def _lane_lt64(shape):
    return _iota2(shape, 1) < HEAD_DIM


def _pick(n, pref):
    for t in pref:
        if n % t == 0:
            return t
    return n


_UNREAD = pl.BlockSpec(memory_space=pl.ANY)


def _matmul_tiles(m, n_unit, k, a_item, b_item, out_bytes):
    best = None
    for tm in (1024, 512, 256, 128):
        for tn in (1536, 1024, 768, 512, 256, 128):
            if m % tm or n_unit % tn:
                continue
            need = 2 * k * (tm * a_item + tn * b_item) + 2 * tm * tn * out_bytes
            if need <= MATMUL_VMEM_BUDGET and (best is None or tm * tn > best[0] * best[1]):
                best = (tm, tn)
    assert best is not None, (m, n_unit, k)
    return best


def _matmul(a, b, *, mode, name, add=None, epilogue=None, extra=None, out_dtypes=(F32,), after=(), b_chips=False,
            out_chips=False):
    sh = b.shape[-1] if b_chips else None
    if mode == "nn":
        (m, k), n = a.shape, (N_CHIPS * sh if b_chips else b.shape[1])
    elif mode == "nt":
        (m, k), n = a.shape, b.shape[-2]
    else:
        (k, m), n = a.shape, b.shape[1]
    osh = n // N_CHIPS if out_chips else None
    out_bytes = sum(jnp.dtype(dt).itemsize for dt in out_dtypes) + (0 if add is None else add.dtype.itemsize) \
        + (0 if extra is None else extra.dtype.itemsize)
    tm, tn = _matmul_tiles(m, sh if (b_chips and mode == "nn") else (osh or n), k, a.dtype.itemsize, b.dtype.itemsize,
                           out_bytes)
    a_spec = pl.BlockSpec((k, tm), lambda i, j: (0, i)) if mode == "tn" else pl.BlockSpec((tm, k), lambda i, j: (i, 0))
    if b_chips and mode == "nn":
        per = sh // tn
        b_spec = pl.BlockSpec((None, k, tn), lambda i, j: (j // per, 0, j % per))
    elif b_chips:
        b_spec = pl.BlockSpec((N_CHIPS, tn, sh), lambda i, j: (0, j, 0))
    elif mode == "nt":
        b_spec = pl.BlockSpec((tn, k), lambda i, j: (j, 0))
    else:
        b_spec = pl.BlockSpec((k, tn), lambda i, j: (0, j))
    if out_chips:
        o_per = osh // tn
        o_spec = pl.BlockSpec((None, tm, tn), lambda i, j: (j // o_per, i, j % o_per))
        out_shape = [jax.ShapeDtypeStruct((N_CHIPS, m, osh), dt) for dt in out_dtypes]
    else:
        o_spec = pl.BlockSpec((tm, tn), lambda i, j: (i, j))
        out_shape = [jax.ShapeDtypeStruct((m, n), dt) for dt in out_dtypes]
    ins = [a, b]
    in_specs = [a_spec, b_spec]
    if add is not None:
        ins.append(add)
        in_specs.append(o_spec)
    if extra is not None:
        ins.append(extra)
        in_specs.append(o_spec)
    ins += list(after)
    in_specs += [_UNREAD] * len(after)
    n_out = len(out_dtypes)

    def body(*refs):
        a_ref, b_ref = refs[0], refs[1]
        pos = 2
        add_ref = ex_ref = None
        if add is not None:
            add_ref = refs[pos]
            pos += 1
        if extra is not None:
            ex_ref = refs[pos]
            pos += 1
        pos += len(after)
        if b_chips and mode == "nt":
            acc = _dot_raw(a_ref[:, pl.ds(0, sh)], b_ref[0], mode)
            for chip in range(1, N_CHIPS):
                acc = acc + _dot_raw(a_ref[:, pl.ds(chip * sh, sh)], b_ref[chip], mode)
        else:
            acc = _dot_raw(a_ref[...], b_ref[...], mode)
        if add_ref is not None:
            acc = acc + add_ref[...].astype(F32)
        outs = (acc,) if epilogue is None else epilogue(acc, None if ex_ref is None else ex_ref[...])
        for o_ref, o in zip(refs[pos:pos + n_out], outs):
            o_ref[...] = o.astype(o_ref.dtype)

    res = pl.pallas_call(
        body, name=name, grid=(m // tm, n // tn), in_specs=in_specs, out_specs=[o_spec] * n_out, out_shape=out_shape,
        compiler_params=_cparams("parallel", "parallel"),
    )(*ins)
    return res[0] if n_out == 1 else res


def _relu2_epilogue(acc, _):
    r = jnp.maximum(acc, 0.0)
    return acc, r * r


def _relu2_bwd_epilogue(acc, u):
    return (acc * (2.0 * jnp.maximum(u, 0.0)),)


def _row_tile(t):
    return _pick(t, (512, 256, 128))


def _rms_fwd(x, g, name, after=()):
    t, d = x.shape
    tm = _row_tile(t)

    def body(x_ref, g_ref, *rest):
        o_ref = rest[-1]
        o_ref[...] = _rms(x_ref[...], g_ref[...]).astype(BF16)

    return pl.pallas_call(
        body, name=name, grid=(t // tm,),
        in_specs=[pl.BlockSpec((tm, d), lambda i: (i, 0)), pl.BlockSpec((1, d), lambda i: (0, 0))] + [_UNREAD] * len(after),
        out_specs=pl.BlockSpec((tm, d), lambda i: (i, 0)),
        out_shape=jax.ShapeDtypeStruct((t, d), BF16),
        compiler_params=_cparams("parallel"),
    )(x, g.reshape(1, d), *after)


def _rms_bwd(x, g, dh, dres, name):
    t, d = x.shape
    tm = _row_tile(t)

    def body(x_ref, g_ref, dh_ref, dres_ref, dx_ref, dg_ref):
        @pl.when(pl.program_id(0) == 0)
        def _():
            dg_ref[...] = jnp.zeros_like(dg_ref)

        _, vjp = jax.vjp(_rms, x_ref[...], g_ref[...])
        dx, dg = vjp(dh_ref[...].astype(F32))
        dx_ref[...] = dx + dres_ref[...]
        dg_ref[...] += dg

    row = pl.BlockSpec((tm, d), lambda i: (i, 0))
    vec = pl.BlockSpec((1, d), lambda i: (0, 0))
    dx, dg = pl.pallas_call(
        body, name=name, grid=(t // tm,),
        in_specs=[row, vec, row, row], out_specs=[row, vec],
        out_shape=[jax.ShapeDtypeStruct((t, d), F32), jax.ShapeDtypeStruct((1, d), F32)],
        compiler_params=_cparams("arbitrary"),
    )(x, g.reshape(1, d), dh, dres)
    return dx, dg.reshape(d)


def _loss_head(x, g, target):
    t, d = x.shape
    tm = _row_tile(t)

    def loss_fn(xv, gv, tv):
        err = _rms(xv, gv) - tv
        return 0.5 * jnp.sum(jnp.mean(err * err, axis=-1, keepdims=True))

    def body(x_ref, g_ref, t_ref, loss_ref, dx_ref, dg_ref):
        @pl.when(pl.program_id(0) == 0)
        def _():
            dg_ref[...] = jnp.zeros_like(dg_ref)
            loss_ref[...] = jnp.zeros_like(loss_ref)

        tv = t_ref[...]
        val, vjp = jax.vjp(lambda xv, gv: loss_fn(xv, gv, tv), x_ref[...], g_ref[...])
        dx, dg = vjp(jnp.ones((), F32))
        dx_ref[...] = dx
        dg_ref[...] += dg
        loss_ref[...] += jnp.full(loss_ref.shape, val, F32)

    row = pl.BlockSpec((tm, d), lambda i: (i, 0))
    vec = pl.BlockSpec((1, d), lambda i: (0, 0))
    loss, dx, dg = pl.pallas_call(
        body, name="loss_head", grid=(t // tm,),
        in_specs=[row, vec, row], out_specs=[pl.BlockSpec((1, LANES), lambda i: (0, 0)), row, vec],
        out_shape=[jax.ShapeDtypeStruct((1, LANES), F32), jax.ShapeDtypeStruct((t, d), F32),
                   jax.ShapeDtypeStruct((1, d), F32)],
        compiler_params=_cparams("arbitrary"),
    )(x, g.reshape(1, d), target)
    return loss[0, 0], dx, dg.reshape(d)


def _pre_glu(val, gate):
    return val * _sigmoid(gate)


def _pre_id(x):
    return x


def _post_lnsilu(c, g, b):
    return _silu(_ln64(c, g, b))


def _post_silu(c):
    return _silu(c)


def _conv_cfg(kind):
    if kind == "a":
        return dict(k=CONV_A_K, pre=_pre_glu, post=_post_lnsilu, n_in=2, n_par=2, nblk=A_WIDTH // LANES,
                    cols=(COL_AVAL, COL_AGATE))
    return dict(k=CONV_C_K, pre=_pre_id, post=_post_silu, n_in=1, n_par=0, nblk=D_CONV_C // LANES,
                cols=(COL_XBC,))


def _conv_fwd(kind, proj, w, bias, params, seq, name, out_dtype=F32, keep_conv=False):
    cfg = _conv_cfg(kind)
    kt, pre, post, n_in = cfg["k"], cfg["pre"], cfg["post"], cfg["n_in"]
    t = proj.shape[0]
    nseq = t // seq
    c = cfg["nblk"] * LANES
    rt = min(256, seq)
    nrt = seq // rt
    off0 = CONV_PAD - (kt - 1)

    def body(*refs):
        in_refs = refs[:n_in]
        w_ref, b_ref = refs[n_in], refs[n_in + 1]
        par_refs = refs[n_in + 2:n_in + 2 + cfg["n_par"]]
        out_refs = refs[n_in + 2 + cfg["n_par"]:-1]
        hpad = refs[-1]
        hpad[pl.ds(0, CONV_PAD), :] = jnp.zeros((CONV_PAD, LANES), F32)
        for r in range(nrt):
            hpad[pl.ds(CONV_PAD + r * rt, rt), :] = pre(*[x[pl.ds(r * rt, rt), :] for x in in_refs])
        pars = [p[...] for p in par_refs]
        for r in range(nrt):
            acc = jnp.broadcast_to(b_ref[...], (rt, LANES))
            for k in range(kt):
                acc = acc + w_ref[pl.ds(k, 1), :] * hpad[pl.ds(off0 + k + r * rt, rt), :]
            out_refs[0][pl.ds(r * rt, rt), :] = post(acc, *pars).astype(out_dtype)
            if keep_conv:
                out_refs[1][pl.ds(r * rt, rt), :] = acc

    in_specs = [pl.BlockSpec((seq, LANES), functools.partial(lambda s, j, col: (s, col + j), col=col))
                for col in cfg["cols"]]
    vec = pl.BlockSpec((1, LANES), lambda s, j: (0, j))
    in_specs += [pl.BlockSpec((CONV_PAD, LANES), lambda s, j: (0, j)), vec] + [vec] * cfg["n_par"]
    blk = pl.BlockSpec((seq, LANES), lambda s, j: (s, j))
    res = pl.pallas_call(
        body, name=name, grid=(nseq, cfg["nblk"]),
        in_specs=in_specs, out_specs=[blk, blk] if keep_conv else [blk],
        out_shape=[jax.ShapeDtypeStruct((t, c), out_dtype)] + ([jax.ShapeDtypeStruct((t, c), F32)] if keep_conv else []),
        scratch_shapes=[pltpu.VMEM((seq + CONV_PAD, LANES), F32)],
        compiler_params=_cparams("parallel", "parallel"),
    )(*([proj] * n_in), w, bias, *params)
    return tuple(res) if keep_conv else res[0]


def _conv_bwd(kind, proj, w, bias, params, dy, seq, name, dy_col=0, conv_out=None):
    kept = conv_out is not None
    cfg = _conv_cfg(kind)
    kt, pre, post, n_in, n_par = cfg["k"], cfg["pre"], cfg["post"], cfg["n_in"], cfg["n_par"]
    t = proj.shape[0]
    nseq = t // seq
    c = cfg["nblk"] * LANES
    rt = min(256, seq)
    nrt = seq // rt
    off0 = CONV_PAD - (kt - 1)

    def body(*refs):
        in_refs = refs[:n_in]
        w_ref, b_ref = refs[n_in], refs[n_in + 1]
        par_refs = refs[n_in + 2:n_in + 2 + n_par]
        pos = n_in + 2 + n_par
        dy_ref = refs[pos]
        if kept:
            pos += 1
            conv_ref = refs[pos]
        din_refs = refs[pos + 1:pos + 1 + n_in]
        dw_ref, db_ref = refs[pos + 1 + n_in], refs[pos + 2 + n_in]
        dpar_refs = refs[pos + 3 + n_in:pos + 3 + n_in + n_par]
        hpad, dcpad = refs[pos + 3 + n_in + n_par:]

        @pl.when(pl.program_id(1) == 0)
        def _():
            dw_ref[...] = jnp.zeros_like(dw_ref)
            db_ref[...] = jnp.zeros_like(db_ref)
            for r in dpar_refs:
                r[...] = jnp.zeros_like(r)

        hpad[pl.ds(0, CONV_PAD), :] = jnp.zeros((CONV_PAD, LANES), F32)
        dcpad[pl.ds(seq, CONV_PAD), :] = jnp.zeros((CONV_PAD, LANES), F32)
        for r in range(nrt):
            hpad[pl.ds(CONV_PAD + r * rt, rt), :] = pre(*[x[pl.ds(r * rt, rt), :] for x in in_refs])
        pars = [p[...] for p in par_refs]
        for r in range(nrt):
            if kept:
                acc = conv_ref[pl.ds(r * rt, rt), :]
            else:
                acc = jnp.broadcast_to(b_ref[...], (rt, LANES))
                for k in range(kt):
                    acc = acc + w_ref[pl.ds(k, 1), :] * hpad[pl.ds(off0 + k + r * rt, rt), :]
            _, vjp = jax.vjp(post, acc, *pars)
            grads = vjp(dy_ref[pl.ds(r * rt, rt), :])
            dcpad[pl.ds(r * rt, rt), :] = grads[0]
            db_ref[...] += jnp.sum(grads[0], axis=0, keepdims=True)
            for ref, gpar in zip(dpar_refs, grads[1:]):
                ref[...] += gpar
        for r in range(nrt):
            dh = jnp.zeros((rt, LANES), F32)
            for k in range(kt):
                dh = dh + w_ref[pl.ds(k, 1), :] * dcpad[pl.ds(r * rt + kt - 1 - k, rt), :]
            _, vjp = jax.vjp(pre, *[x[pl.ds(r * rt, rt), :] for x in in_refs])
            for ref, gin in zip(din_refs, vjp(dh)):
                ref[pl.ds(r * rt, rt), :] = gin.astype(ref.dtype)
        for k in range(kt):
            s = jnp.zeros((1, LANES), F32)
            for r in range(nrt):
                s = s + jnp.sum(dcpad[pl.ds(r * rt, rt), :] * hpad[pl.ds(off0 + k + r * rt, rt), :],
                                axis=0, keepdims=True)
            dw_ref[pl.ds(k, 1), :] += s

    in_specs = [pl.BlockSpec((seq, LANES), functools.partial(lambda j, s, col: (s, col + j), col=col))
                for col in cfg["cols"]]
    vec = pl.BlockSpec((1, LANES), lambda j, s: (0, j))
    wspec = pl.BlockSpec((CONV_PAD, LANES), lambda j, s: (0, j))
    blk = pl.BlockSpec((seq, LANES), lambda j, s: (s, j))
    in_specs += [wspec, vec] + [vec] * n_par + [pl.BlockSpec((seq, LANES), lambda j, s: (s, dy_col + j))]
    in_specs += [blk] if kept else []
    out_specs = [blk] * n_in + [wspec, vec] + [vec] * n_par
    out_shape = ([jax.ShapeDtypeStruct((t, c), BF16)] * n_in
                 + [jax.ShapeDtypeStruct((CONV_PAD, c), F32), jax.ShapeDtypeStruct((1, c), F32)]
                 + [jax.ShapeDtypeStruct((1, c), F32)] * n_par)
    res = pl.pallas_call(
        body, name=name, grid=(cfg["nblk"], nseq),
        in_specs=in_specs, out_specs=out_specs, out_shape=out_shape,
        scratch_shapes=[pltpu.VMEM((seq + CONV_PAD, LANES), F32), pltpu.VMEM((seq + CONV_PAD, LANES), F32)],
        compiler_params=_cparams("parallel", "arbitrary"),
    )(*([proj] * n_in), w, bias, *params, dy, *([conv_out] if kept else []))
    return res[:n_in], res[n_in], res[n_in + 1], res[n_in + 2:]


def _gmlp_chunk(bu, bv, g, b, w0, w1, b0row, b1row):
    u = _gelu(bu)
    vn = _ln64(_gelu(bv), g, b)
    tri = _iota2((CHUNK, CHUNK), 0) >= _iota2((CHUNK, CHUNK), 1)
    m0 = _nn(jnp.where(tri, w0, 0.0), vn) + jnp.broadcast_to(b0row, (CHUNK, CHUNK)).T
    m1 = _nn(jnp.where(tri, w1, 0.0), vn) + jnp.broadcast_to(b1row, (CHUNK, CHUNK)).T
    return u * jnp.where(_lane_lt64((CHUNK, LANES)), m0, m1)


def _gmlp_specs(tm, order):
    def im(f):
        return lambda *ids: f(*order(*ids))
    return dict(
        bu=pl.BlockSpec((tm, LANES), im(lambda j, r: (r, COL_BU + j))),
        bv=pl.BlockSpec((tm, LANES), im(lambda j, r: (r, COL_BV + j))),
        vec=pl.BlockSpec((1, LANES), im(lambda j, r: (0, j))),
        ws=pl.BlockSpec((2, CHUNK, CHUNK), im(lambda j, r: (j, 0, 0))),
        bs=pl.BlockSpec((None, 2, CHUNK), im(lambda j, r: (j, 0, 0))),
        blk=pl.BlockSpec((tm, LANES), im(lambda j, r: (r, j))),
    )


def _gmlp_fwd(proj, ln_g, ln_b, w_s, b_s, name):
    t = proj.shape[0]
    tm = _row_tile(t)
    nch = tm // CHUNK
    sp = _gmlp_specs(tm, lambda r, j: (j, r))

    def body(bu_ref, bv_ref, g_ref, b_ref, ws_ref, bs_ref, o_ref):
        for ci in range(nch):
            rows = pl.ds(ci * CHUNK, CHUNK)
            o_ref[rows, :] = _gmlp_chunk(bu_ref[rows, :], bv_ref[rows, :], g_ref[...], b_ref[...], ws_ref[0], ws_ref[1],
                                         bs_ref[pl.ds(0, 1), :], bs_ref[pl.ds(1, 1), :]).astype(BF16)

    return pl.pallas_call(
        body, name=name, grid=(t // tm, B_WIDTH // LANES),
        in_specs=[sp["bu"], sp["bv"], sp["vec"], sp["vec"], sp["ws"], sp["bs"]],
        out_specs=sp["blk"], out_shape=jax.ShapeDtypeStruct((t, B_WIDTH), BF16),
        compiler_params=_cparams("parallel", "parallel"),
    )(proj, proj, ln_g, ln_b, w_s, b_s.reshape(B_WIDTH // LANES, 2, CHUNK))


def _gmlp_bwd(proj, ln_g, ln_b, w_s, b_s, dy, name, dy_col=0):
    t = proj.shape[0]
    tm = _row_tile(t)
    nch = tm // CHUNK
    sp = _gmlp_specs(tm, lambda j, r: (j, r))
    dy_spec = pl.BlockSpec((tm, LANES), lambda j, r: (r, dy_col + j))

    def body(bu_ref, bv_ref, g_ref, b_ref, ws_ref, bs_ref, dy_ref, dbu_ref, dbv_ref, dg_ref, db_ref, dws_ref, dbs_ref):
        @pl.when(pl.program_id(1) == 0)
        def _():
            for r in (dg_ref, db_ref, dws_ref, dbs_ref):
                r[...] = jnp.zeros_like(r)

        for ci in range(nch):
            rows = pl.ds(ci * CHUNK, CHUNK)
            _, vjp = jax.vjp(_gmlp_chunk, bu_ref[rows, :], bv_ref[rows, :], g_ref[...], b_ref[...],
                             ws_ref[0], ws_ref[1], bs_ref[pl.ds(0, 1), :], bs_ref[pl.ds(1, 1), :])
            dbu, dbv, dg, db, dw0, dw1, db0, db1 = vjp(dy_ref[rows, :])
            dbu_ref[rows, :] = dbu.astype(BF16)
            dbv_ref[rows, :] = dbv.astype(BF16)
            dg_ref[...] += dg
            db_ref[...] += db
            dws_ref[0] += dw0
            dws_ref[1] += dw1
            dbs_ref[pl.ds(0, 1), :] += db0
            dbs_ref[pl.ds(1, 1), :] += db1

    nh = B_WIDTH // LANES
    res = pl.pallas_call(
        body, name=name, grid=(nh, t // tm),
        in_specs=[sp["bu"], sp["bv"], sp["vec"], sp["vec"], sp["ws"], sp["bs"], dy_spec],
        out_specs=[sp["blk"], sp["blk"], sp["vec"], sp["vec"], sp["ws"], sp["bs"]],
        out_shape=[jax.ShapeDtypeStruct((t, B_WIDTH), BF16), jax.ShapeDtypeStruct((t, B_WIDTH), BF16),
                   jax.ShapeDtypeStruct((1, B_WIDTH), F32), jax.ShapeDtypeStruct((1, B_WIDTH), F32),
                   jax.ShapeDtypeStruct(w_s.shape, F32), jax.ShapeDtypeStruct((nh, 2, CHUNK), F32)],
        compiler_params=_cparams("parallel", "arbitrary"),
    )(proj, proj, ln_g, ln_b, w_s, b_s.reshape(nh, 2, CHUNK), dy)
    dbu, dbv, dg, db, dws, dbs = res
    return dbu, dbv, dg, db, dws, dbs.reshape(b_s.shape)


def _tri_apply(a, lower):
    l = a.shape[0]
    r, c = _iota2((l, l), 0), _iota2((l, l), 1)
    t = jnp.where((r >= c) if lower else (r <= c), 1.0, 0.0).astype(BF16)
    hi = a.astype(BF16)
    r1 = a - hi.astype(F32)
    mid = r1.astype(BF16)
    lo = (r1 - mid.astype(F32)).astype(BF16)
    dn = _DN["nn"]
    return (lax.dot_general(t, hi, dn, preferred_element_type=F32) + lax.dot_general(t, mid, dn, preferred_element_type=F32)
            + lax.dot_general(t, lo, dn, preferred_element_type=F32))


@jax.custom_vjp
def _cumsum_rows(a):
    return _tri_apply(a, True)


_cumsum_rows.defvjp(lambda a: (_tri_apply(a, True), None), lambda _, g: (_tri_apply(g, False),))

SSD_GROUP_HEADS = 8
SSD_GROUP_PAIRS = 4


def _ssd_group(x0, x1, x2, x3, dt_raw, bias, alog, bm, cm, p0, p1, p2, p3):
    xs, prevs = (x0, x1, x2, x3), (p0, p1, p2, p3)
    dt = _softplus(dt_raw + bias)
    a = dt * (-jnp.exp(alog))
    acs = _cumsum_rows(a)
    alast = jnp.sum(a, axis=0, keepdims=True)
    dt_t, acs_t = dt.T, acs.T
    cb = _nt(cm, bm)
    tri = _iota2((CHUNK, CHUNK), 0) >= _iota2((CHUNK, CHUNK), 1)
    lane = _iota2((CHUNK, LANES), 1)
    sub = _iota2((LANES, CHUNK), 0)
    lane1 = _iota2((1, LANES), 1)

    def column(v, i):
        return jnp.broadcast_to(jnp.sum(jnp.where(lane == i, v, 0.0), axis=1, keepdims=True), (CHUNK, LANES))

    def row(vt, i):
        return jnp.broadcast_to(jnp.sum(jnp.where(sub == i, vt, 0.0), axis=0, keepdims=True), (CHUNK, CHUNK))

    heads = []
    for i in range(SSD_GROUP_HEADS):
        col_a = column(acs, i)
        al = jnp.sum(jnp.where(lane1 == i, alast, 0.0), axis=1, keepdims=True)
        m = cb * jnp.exp(jnp.where(tri, col_a - row(acs_t, i), NEG)) * row(dt_t, i)
        heads.append((m, jnp.exp(col_a), column(dt, i) * jnp.exp(al - col_a), jnp.exp(al)))
    lo_lanes = _lane_lt64((CHUNK, LANES))
    lo_rows = _iota2((LANES, SSM_STATE), 0) < HEAD_DIM
    ys, news = [], []
    for j in range(SSD_GROUP_PAIRS):
        (m0, ea0, w0, cd0), (m1, ea1, w1, cd1) = heads[2 * j], heads[2 * j + 1]
        x, prev = xs[j], prevs[j]
        ydiag = jnp.where(lo_lanes, _nn(m0, x), _nn(m1, x))
        yoff = jnp.where(lo_lanes, _nt(cm * ea0, prev), _nt(cm * ea1, prev))
        states = jnp.where(lo_rows, _tn(x, bm * w0), _tn(x, bm * w1))
        ys.append(ydiag + yoff)
        news.append(prev * jnp.where(lo_rows, cd0, cd1) + states)
    return tuple(ys) + tuple(news)


SSD_GROUPS = 2


def _ssd2_specs(seq, rev):
    ncs = seq // CHUNK
    wide = SSD_GROUPS * LANES

    def row(s, c):
        return s * ncs + (ncs - 1 - c if rev else c)

    return dict(
        x=pl.BlockSpec((CHUNK, C_WIDTH), lambda s, c: (row(s, c), 0)),
        dt=pl.BlockSpec((CHUNK, wide), lambda s, c: (row(s, c), 0)),
        vec=pl.BlockSpec((1, wide), lambda s, c: (0, 0)),
        bm=pl.BlockSpec((CHUNK, wide), lambda s, c: (row(s, c), C_WIDTH // wide)),
        cm=pl.BlockSpec((CHUNK, wide), lambda s, c: (row(s, c), C_WIDTH // wide + 1)),
        st=pl.BlockSpec((None, C_WIDTH // LANES, LANES, SSM_STATE), lambda s, c: (row(s, c), 0, 0, 0)),
        ncs=ncs,
    )


def _lane_blocks(ref, grp):
    return [ref[:, pl.ds((grp * SSD_GROUP_PAIRS + j) * LANES, LANES)] for j in range(SSD_GROUP_PAIRS)]


def _group_block(ref, grp):
    return ref[:, pl.ds(grp * LANES, LANES)]


def _ssd2_fwd(xbc_act, dt_raw, dt_bias, a_log, seq, name):
    t = xbc_act.shape[0]
    sp = _ssd2_specs(seq, False)

    npair = SSD_GROUP_PAIRS

    def body(x_ref, dt_ref, bias_ref, alog_ref, bm_ref, cm_ref, y_ref, prev_ref, state):
        @pl.when(pl.program_id(1) == 0)
        def _():
            state[...] = jnp.zeros_like(state)

        for grp in range(SSD_GROUPS):
            prevs = [state[grp * npair + j] for j in range(npair)]
            for j in range(npair):
                prev_ref[grp * npair + j] = prevs[j]
            res = _ssd_group(*_lane_blocks(x_ref, grp), _group_block(dt_ref, grp), _group_block(bias_ref, grp),
                             _group_block(alog_ref, grp), _group_block(bm_ref, grp), _group_block(cm_ref, grp), *prevs)
            for j in range(npair):
                y_ref[:, pl.ds((grp * npair + j) * LANES, LANES)] = res[j]
                state[grp * npair + j] = res[npair + j]

    return pl.pallas_call(
        body, name=name, grid=(t // seq, sp["ncs"]),
        in_specs=[sp["x"], sp["dt"], sp["vec"], sp["vec"], sp["bm"], sp["cm"]],
        out_specs=[sp["x"], sp["st"]],
        out_shape=[jax.ShapeDtypeStruct((t, C_WIDTH), F32),
                   jax.ShapeDtypeStruct((t // CHUNK, C_WIDTH // LANES, LANES, SSM_STATE), F32)],
        scratch_shapes=[pltpu.VMEM((C_WIDTH // LANES, LANES, SSM_STATE), F32)],
        compiler_params=_cparams("parallel", "arbitrary"),
    )(xbc_act, dt_raw, dt_bias, a_log, xbc_act, xbc_act)


def _ssd2_bwd(xbc_act, dt_raw, dt_bias, a_log, prev_saved, dy, seq, name):
    t = xbc_act.shape[0]
    sp = _ssd2_specs(seq, True)
    npair = SSD_GROUP_PAIRS

    def body(x_ref, dt_ref, bias_ref, alog_ref, bm_ref, cm_ref, prev_ref, dy_ref,
             dx_ref, ddt_ref, dbias_ref, dalog_ref, dbm_ref, dcm_ref, dstate):
        @pl.when(pl.program_id(1) == 0)
        def _():
            dstate[...] = jnp.zeros_like(dstate)

        @pl.when(jnp.logical_and(pl.program_id(0) == 0, pl.program_id(1) == 0))
        def _():
            dbias_ref[...] = jnp.zeros_like(dbias_ref)
            dalog_ref[...] = jnp.zeros_like(dalog_ref)

        for grp in range(SSD_GROUPS):
            lanes = pl.ds(grp * LANES, LANES)
            _, vjp = jax.vjp(_ssd_group, *_lane_blocks(x_ref, grp), _group_block(dt_ref, grp), _group_block(bias_ref, grp),
                             _group_block(alog_ref, grp), _group_block(bm_ref, grp), _group_block(cm_ref, grp),
                             *[prev_ref[grp * npair + j] for j in range(npair)])
            grads = vjp(tuple(_lane_blocks(dy_ref, grp)) + tuple(dstate[grp * npair + j] for j in range(npair)))
            for j in range(npair):
                dx_ref[:, pl.ds((grp * npair + j) * LANES, LANES)] = grads[j]
                dstate[grp * npair + j] = grads[npair + 5 + j]
            ddt_ref[:, lanes] = grads[npair].astype(BF16)
            dbias_ref[:, lanes] += grads[npair + 1]
            dalog_ref[:, lanes] += grads[npair + 2]
            dbm_ref[:, lanes] = grads[npair + 3]
            dcm_ref[:, lanes] = grads[npair + 4]

    return pl.pallas_call(
        body, name=name, grid=(t // seq, sp["ncs"]),
        in_specs=[sp["x"], sp["dt"], sp["vec"], sp["vec"], sp["bm"], sp["cm"], sp["st"], sp["x"]],
        out_specs=[sp["x"], sp["dt"], sp["vec"], sp["vec"], sp["dt"], sp["dt"]],
        out_shape=[jax.ShapeDtypeStruct((t, C_WIDTH), F32), jax.ShapeDtypeStruct((t, 2 * LANES), BF16),
                   jax.ShapeDtypeStruct((1, 2 * LANES), F32), jax.ShapeDtypeStruct((1, 2 * LANES), F32),
                   jax.ShapeDtypeStruct((t, 2 * SSM_STATE), F32), jax.ShapeDtypeStruct((t, 2 * SSM_STATE), F32)],
        scratch_shapes=[pltpu.VMEM((C_WIDTH // LANES, LANES, SSM_STATE), F32)],
        compiler_params=_cparams("arbitrary", "arbitrary"),
    )(xbc_act, dt_raw, dt_bias, a_log, xbc_act, xbc_act, prev_saved, dy)


def _ssd2_assemble(dxs_ssd, dxs_skip, dbm, dcm, name):
    t = dxs_ssd.shape[0]
    tm = _row_tile(t)

    def body(a_ref, b_ref, dbm_ref, dcm_ref, o_ref):
        o_ref[:, pl.ds(0, C_WIDTH)] = a_ref[...] + b_ref[...]
        o_ref[:, pl.ds(C_WIDTH, 2 * SSM_STATE)] = dbm_ref[...]
        o_ref[:, pl.ds(C_WIDTH + 2 * SSM_STATE, 2 * SSM_STATE)] = dcm_ref[...]

    wide = pl.BlockSpec((tm, C_WIDTH), lambda i: (i, 0))
    narrow = pl.BlockSpec((tm, 2 * SSM_STATE), lambda i: (i, 0))
    return pl.pallas_call(
        body, name=name, grid=(t // tm,), in_specs=[wide, wide, narrow, narrow],
        out_specs=pl.BlockSpec((tm, D_CONV_C), lambda i: (i, 0)),
        out_shape=jax.ShapeDtypeStruct((t, D_CONV_C), F32),
        compiler_params=_cparams("parallel"),
    )(dxs_ssd, dxs_skip, dbm, dcm)


def _ssd_post_fn(y, xs, z, dskip, g):
    v = (y + dskip * xs) * _silu(z)
    return v * lax.rsqrt(jnp.mean(v * v, axis=-1, keepdims=True) + EPS) * g


def _ssd_post_specs(tm, order):
    gw = C_WIDTH // 2

    def im(f):
        return lambda *ids: f(*order(*ids))
    return dict(
        blk=pl.BlockSpec((tm, gw), im(lambda g, r: (r, g))),
        z=pl.BlockSpec((tm, gw), im(lambda g, r: (r, COL_Z * LANES // gw + g))),
        vec=pl.BlockSpec((1, gw), im(lambda g, r: (0, g))),
    )


def _ssd_post_fwd(y_ssd, xbc_act, proj, dskip64, norm_g, name):
    t = y_ssd.shape[0]
    tm = _row_tile(t)
    sp = _ssd_post_specs(tm, lambda r, g: (g, r))

    def body(y_ref, xs_ref, z_ref, ds_ref, g_ref, o_ref):
        o_ref[...] = _ssd_post_fn(y_ref[...], xs_ref[...], z_ref[...], ds_ref[...], g_ref[...]).astype(BF16)

    return pl.pallas_call(
        body, name=name, grid=(t // tm, 2),
        in_specs=[sp["blk"], sp["blk"], sp["z"], sp["vec"], sp["vec"]], out_specs=sp["blk"],
        out_shape=jax.ShapeDtypeStruct((t, C_WIDTH), BF16),
        compiler_params=_cparams("parallel", "parallel"),
    )(y_ssd, xbc_act, proj, dskip64, norm_g)


def _ssd_post_bwd(y_ssd, xbc_act, proj, dskip64, norm_g, dyc, name, dy_col=0):
    t = y_ssd.shape[0]
    tm = _row_tile(t)
    sp = _ssd_post_specs(tm, lambda g, r: (g, r))
    dy_spec = pl.BlockSpec((tm, C_WIDTH // 2), lambda g, r: (r, dy_col + g))

    def body(y_ref, xs_ref, z_ref, ds_ref, g_ref, dyc_ref, dy_ref, dxs_ref, dz_ref, dds_ref, dg_ref):
        @pl.when(pl.program_id(1) == 0)
        def _():
            dds_ref[...] = jnp.zeros_like(dds_ref)
            dg_ref[...] = jnp.zeros_like(dg_ref)

        _, vjp = jax.vjp(_ssd_post_fn, y_ref[...], xs_ref[...], z_ref[...], ds_ref[...], g_ref[...])
        dy, dxs, dz, dds, dg = vjp(dyc_ref[...])
        dy_ref[...] = dy
        dxs_ref[...] = dxs
        dz_ref[...] = dz.astype(BF16)
        dds_ref[...] += dds
        dg_ref[...] += dg

    wide = jax.ShapeDtypeStruct((t, C_WIDTH), F32)
    vec = jax.ShapeDtypeStruct((1, C_WIDTH), F32)
    return pl.pallas_call(
        body, name=name, grid=(2, t // tm),
        in_specs=[sp["blk"], sp["blk"], sp["z"], sp["vec"], sp["vec"], dy_spec],
        out_specs=[sp["blk"], sp["blk"], sp["blk"], sp["vec"], sp["vec"]],
        out_shape=[wide, wide, jax.ShapeDtypeStruct((t, C_WIDTH), BF16), vec, vec],
        compiler_params=_cparams("parallel", "arbitrary"),
    )(y_ssd, xbc_act, proj, dskip64, norm_g, dyc)


def _pad_taps(w):
    return jnp.pad(w, ((0, CONV_PAD - w.shape[0]), (0, 0)))


def _group_heads(a):
    pad = [(0, 0)] * (a.ndim - 1) + [(0, LANES - SSD_GROUP_HEADS)]
    return jnp.concatenate([jnp.pad(a[..., :SSD_GROUP_HEADS], pad), jnp.pad(a[..., SSD_GROUP_HEADS:], pad)], axis=-1)


def _ungroup_heads(a):
    return jnp.concatenate([a[..., :SSD_GROUP_HEADS], a[..., LANES:LANES + SSD_GROUP_HEADS]], axis=-1)


def _layer_fwd(x, p, seq, li, after=(), late=None):
    n = f"l{li}_"
    h1 = _rms_fwd(x, p["norm1_g"], n + "rms1", after=after)
    proj = _matmul(h1, p["w_main"], mode="nn", name=n + "inproj")
    dt_raw = _matmul(h1, p["w_dt"], mode="nn", name=n + "inproj_dt")
    row = lambda v: v.reshape(1, -1)
    ya, conv_a = _conv_fwd("a", proj, _pad_taps(p["conv_a_w"]), row(p["conv_a_b"]), (row(p["ln_a_g"]), row(p["ln_a_b"])),
                           seq, n + "conva", out_dtype=BF16, keep_conv=True)
    yb = _gmlp_fwd(proj, row(p["ln_b_g"]), row(p["ln_b_b"]), p["w_spatial"], p["b_spatial"], n + "gmlp")
    xbc_act = _conv_fwd("c", proj, _pad_taps(p["conv_c_w"]), row(p["conv_c_b"]), (), seq, n + "convc")
    y_ssd, prev = _ssd2_fwd(xbc_act, dt_raw, _group_heads(row(p["dt_bias"])), _group_heads(row(p["a_log"])), seq, n + "ssd")
    dskip64 = jnp.repeat(p["d_skip"], HEAD_DIM).reshape(1, C_WIDTH)
    yc = _ssd_post_fwd(y_ssd, xbc_act, proj, dskip64, row(p["norm_c_g"]), n + "ssdpost")
    ycat = jnp.concatenate([ya, yb, yc], axis=1)
    behind = ()
    if late is not None:
        more, behind = late(ycat)
        p.update(more)
    x1 = _matmul(ycat, p["w_out"], mode="nn", name=n + "outproj", add=x, after=behind)
    h2 = _rms_fwd(x1, p["norm2_g"], n + "rms2")
    u, act = _matmul(h2, p["w_ff1"], mode="nn", name=n + "ff1", epilogue=_relu2_epilogue, out_dtypes=(F32, BF16),
                     b_chips=True)
    x2 = _matmul(act, p["w_ff2"], mode="nn", name=n + "ff2", add=x1)
    saved = dict(x=x, h1=h1, proj=proj, conv_a=conv_a, dt_raw=dt_raw, xbc_act=xbc_act, prev=prev, y_ssd=y_ssd,
                 dskip64=dskip64, ycat=ycat, x1=x1, h2=h2, u=u, act=act)
    return x2, saved


def _layer_bwd(dx2, p, s, seq, li, after=(), on_ffn_grads=None):
    n = f"l{li}_b_"
    row = lambda v: v.reshape(1, -1)
    g = {}
    du = _matmul(dx2, p["w_ff2"], mode="nt", name=n + "ff2_dx", epilogue=_relu2_bwd_epilogue, extra=s["u"],
                 out_dtypes=(BF16,), after=after)
    g["w_ff2"] = _matmul(s["act"], dx2, mode="tn", name=n + "ff2_dw")
    g["w_ff1"] = _matmul(s["h2"], du, mode="tn", name=n + "ff1_dw", out_chips=True)
    dh2 = _matmul(du, p["w_ff1"], mode="nt", name=n + "ff1_dx", b_chips=True)
    dx1, g["norm2_g"] = _rms_bwd(s["x1"], p["norm2_g"], dh2, dx2, n + "rms2")
    g["w_out"] = _matmul(s["ycat"], dx1, mode="tn", name=n + "out_dw")
    dycat = _matmul(dx1, p["w_out"], mode="nt", name=n + "out_dx",
                    after=() if on_ffn_grads is None else on_ffn_grads(g))
    proj = s["proj"]
    (dval, dgate), dwa, dba, (dlag, dlab) = _conv_bwd(
        "a", proj, _pad_taps(p["conv_a_w"]), row(p["conv_a_b"]), (row(p["ln_a_g"]), row(p["ln_a_b"])), dycat, seq,
        n + "conva", dy_col=0, conv_out=s["conv_a"])
    g["conv_a_w"], g["conv_a_b"], g["ln_a_g"], g["ln_a_b"] = dwa[:CONV_A_K], dba[0], dlag[0], dlab[0]
    dbu, dbv, dlbg, dlbb, g["w_spatial"], g["b_spatial"] = _gmlp_bwd(
        proj, row(p["ln_b_g"]), row(p["ln_b_b"]), p["w_spatial"], p["b_spatial"], dycat, n + "gmlp",
        dy_col=A_WIDTH // LANES)
    g["ln_b_g"], g["ln_b_b"] = dlbg[0], dlbb[0]
    dy_ssd, dxs_skip, dz, dds, dncg = _ssd_post_bwd(s["y_ssd"], s["xbc_act"], proj, s["dskip64"], row(p["norm_c_g"]),
                                                    dycat, n + "ssdpost", dy_col=(A_WIDTH + B_WIDTH) * 2 // C_WIDTH)
    g["norm_c_g"] = dncg[0]
    g["d_skip"] = dds.reshape(C_HEADS, HEAD_DIM).sum(axis=1)
    dxs, ddt_raw, ddtb, dalog, dbm, dcm = _ssd2_bwd(
        s["xbc_act"], s["dt_raw"], _group_heads(row(p["dt_bias"])), _group_heads(row(p["a_log"])), s["prev"], dy_ssd, seq,
        n + "ssd")
    g["dt_bias"], g["a_log"] = _ungroup_heads(ddtb)[0], _ungroup_heads(dalog)[0]
    dconv = _ssd2_assemble(dxs, dxs_skip, dbm, dcm, n + "ssdasm")
    (dxbc,), dwc, dbcv, _ = _conv_bwd("c", proj, _pad_taps(p["conv_c_w"]), row(p["conv_c_b"]), (), dconv, seq, n + "convc")
    g["conv_c_w"], g["conv_c_b"] = dwc[:CONV_C_K], dbcv[0]
    dproj = jnp.concatenate([dval, dgate, dbu, dbv, dz, dxbc], axis=1)
    g["w_main"] = _matmul(s["h1"], dproj, mode="tn", name=n + "in_dw")
    g["w_dt"] = _matmul(s["h1"], ddt_raw, mode="tn", name=n + "indt_dw")
    dh1 = _matmul(dproj, p["w_main"], mode="nt", name=n + "in_dx")
    dh1 = _matmul(ddt_raw, p["w_dt"], mode="nt", name=n + "indt_dx", add=dh1)
    dx, g["norm1_g"] = _rms_bwd(s["x"], p["norm1_g"], dh1, dx1, n + "rms1")
    return dx, g


EW_BLOCK_BYTES = 1 << 20


def _ew(fn, ins, out_dtypes, name, leads=None):
    leads = leads or [None] * len(ins)
    rows, c = ins[0].shape[-2:]
    tr = _pick(rows, [t for t in (2048, 1024, 512, 256, 128, 64, 32, 16, 8) if t * c * 4 <= EW_BLOCK_BYTES])
    n_in = len(ins)

    def spec(lead):
        if lead is None:
            return pl.BlockSpec((tr, c), lambda i: (i, 0))
        return pl.BlockSpec((None, tr, c), functools.partial(lambda i, k: (k, i, 0), k=lead))

    def body(*refs):
        outs = fn(*[r[...].astype(F32) for r in refs[:n_in]])
        for o_ref, o in zip(refs[n_in:], outs):
            o_ref[...] = o.astype(o_ref.dtype)

    return pl.pallas_call(
        body, name=name, grid=(rows // tr,),
        in_specs=[spec(l) for l in leads], out_specs=[spec(None)] * len(out_dtypes),
        out_shape=[jax.ShapeDtypeStruct((rows, c), dt) for dt in out_dtypes],
        compiler_params=_cparams("parallel"),
    )(*ins)


def _adam_fn(w, g, m, v):
    m2 = ADAM_B1 * m + (1.0 - ADAM_B1) * g
    v2 = ADAM_B2 * v + (1.0 - ADAM_B2) * (g * g)
    m_hat = m2 / (1.0 - ADAM_B1 ** ADAM_STEP)
    v_hat = v2 / (1.0 - ADAM_B2 ** ADAM_STEP)
    delta = -ADAM_LR * (m_hat / (jnp.sqrt(v_hat) + ADAM_EPS) + ADAM_WD * w)
    return delta, m2, v2


def _adam(w, g, m, v, name):
    shape = w.shape
    two_d = lambda a: a.reshape(-1, shape[-1])
    outs = _ew(_adam_fn, [two_d(w), two_d(g), two_d(m), two_d(v)], (F32, F32, F32), name)
    return [o.reshape(shape) for o in outs]


_ANY = pl.BlockSpec(memory_space=pl.ANY)


def _mesh_pos():
    return lax.axis_index("x"), lax.axis_index("y"), lax.axis_index("c")


def _peer_chips(x, y):
    return [(1 - x, y), (x, 1 - y), (1 - x, 1 - y)]


def _remote(src, dst, send_sems, recv_sems, sem, to):
    return pltpu.make_async_remote_copy(src_ref=src, dst_ref=dst, send_sem=send_sems.at[sem],
                                        recv_sem=recv_sems.at[sem], device_id=to, device_id_type=MESH)


def _half_rows(n_rows, which):
    half = n_rows // 2
    return pl.ds(pl.multiple_of(which * half, 8), half)


def _comm_call(body, ins, out_shapes, n_sems, name):
    scratch = [pltpu.SemaphoreType.DMA((n_sems,)), pltpu.SemaphoreType.DMA((n_sems,))]
    return pl.pallas_call(
        body, name=name, in_specs=[_ANY] * len(ins), out_specs=[_ANY] * len(out_shapes),
        out_shape=out_shapes, scratch_shapes=scratch,
    )(*ins)


def _gather_weights(big, small, name):
    nb, ns = len(big), len(small)
    n = nb + ns

    def body(*refs):
        ins, outs = refs[:n], refs[n:2 * n]
        send_sems, recv_sems = refs[2 * n:]
        x, y, c = _mesh_pos()
        q = 2 * x + y
        me, sib = (x, y, c), (x, y, 1 - c)
        chips = _peer_chips(x, y)
        rem = functools.partial(_remote, send_sems=send_sems, recv_sems=recv_sems)
        first = []
        for i in range(nb):
            mine = _half_rows(big[i].shape[0], c)
            for k, (px, py) in enumerate(chips):
                first.append(rem(ins[i].at[mine], outs[i].at[q, mine], sem=6 * i + k, to=(px, py, c)))
        for j in range(ns):
            for k, (px, py) in enumerate(chips):
                first.append(rem(ins[nb + j], outs[nb + j].at[q], sem=6 * nb + 3 * j + k, to=(px, py, c)))
        for cp in first:
            cp.start()
        passed = []
        for i in range(nb):
            mine = _half_rows(big[i].shape[0], c)
            for k, (px, py) in enumerate(chips):
                landed = outs[i].at[2 * px + py, mine]
                rem(landed, landed, sem=6 * i + k, to=me).wait_recv()
                fwd = rem(landed, landed, sem=6 * i + 3 + k, to=sib)
                fwd.start()
                passed.append(fwd)
        for i in range(nb):
            other = _half_rows(big[i].shape[0], 1 - c)
            for k, (px, py) in enumerate(chips):
                theirs = outs[i].at[2 * px + py, other]
                rem(theirs, theirs, sem=6 * i + 3 + k, to=me).wait_recv()
        for j in range(ns):
            for k, (px, py) in enumerate(chips):
                dst = outs[nb + j].at[2 * px + py]
                rem(dst, dst, sem=6 * nb + 3 * j + k, to=me).wait_recv()
        for cp in first + passed:
            cp.wait_send()

    out_shapes = [jax.ShapeDtypeStruct((N_CHIPS,) + a.shape, a.dtype) for a in list(big) + list(small)]
    return _comm_call(body, list(big) + list(small), out_shapes, 6 * nb + 3 * ns, name)


def _sibling_other_halves(gs, name):
    n = len(gs)

    def other_half(ref, shape, c):
        rows = _half_rows(shape[-2], 1 - c)
        return ref.at[rows] if len(shape) == 2 else ref.at[:, rows]

    def body(*refs):
        ins, outs = refs[:n], refs[n:2 * n]
        send_sems, recv_sems = refs[2 * n:]
        x, y, c = _mesh_pos()
        copies = [_remote(other_half(ins[i], gs[i].shape, c), outs[i], send_sems, recv_sems, i, (x, y, 1 - c))
                  for i in range(n)]
        for cp in copies:
            cp.start()
        for cp in copies:
            cp.wait()

    out_shapes = [jax.ShapeDtypeStruct(g.shape[:-2] + (g.shape[-2] // 2, g.shape[-1]), g.dtype) for g in gs]
    return _comm_call(body, list(gs), out_shapes, n, name)


IN_SHARD = D_IN_PROJ // N_CHIPS


def _chipsum_in(mine, mine_dt, theirs, theirs_dt, name):
    r = mine.shape[0]
    tr = _pick(r, (128, 64, 32, 16, 8))
    last = D_MAIN - (N_CHIPS - 1) * IN_SHARD

    def body(a_ref, adt_ref, b_ref, bdt_ref, o32_ref, o16_ref):
        for p in range(N_CHIPS):
            wid = IN_SHARD if p < N_CHIPS - 1 else last
            s = a_ref[:, pl.ds(IN_SHARD * p, wid)] + b_ref[:, pl.ds(IN_SHARD * p, wid)]
            o32_ref[p, :, pl.ds(0, wid)] = s
            o16_ref[p, :, pl.ds(0, wid)] = s.astype(BF16)
        for grp in range(2):
            src = pl.ds(grp * LANES, SSD_GROUP_HEADS)
            s = adt_ref[:, src] + bdt_ref[:, src]
            dst = pl.ds(last + grp * SSD_GROUP_HEADS, SSD_GROUP_HEADS)
            o32_ref[N_CHIPS - 1, :, dst] = s
            o16_ref[N_CHIPS - 1, :, dst] = s.astype(BF16)

    wide = pl.BlockSpec((tr, D_MAIN), lambda i: (i, 0))
    narrow = pl.BlockSpec((tr, 2 * LANES), lambda i: (i, 0))
    out = pl.BlockSpec((N_CHIPS, tr, IN_SHARD), lambda i: (0, i, 0))
    return pl.pallas_call(
        body, name=name, grid=(r // tr,), in_specs=[wide, narrow, wide, narrow], out_specs=[out, out],
        out_shape=[jax.ShapeDtypeStruct((N_CHIPS, r, IN_SHARD), F32), jax.ShapeDtypeStruct((N_CHIPS, r, IN_SHARD), BF16)],
        compiler_params=_cparams("parallel"),
    )(mine, mine_dt, theirs, theirs_dt)


_HBM = pl.BlockSpec(memory_space=pltpu.HBM)
_SEM = pl.BlockSpec(memory_space=pltpu.SEMAPHORE)


def _in_hbm(a):
    return pltpu.with_memory_space_constraint(a, pltpu.HBM)


def _split_plan(kind, srcs, lands, x, y, c):
    plan = []
    for src, land in zip(srcs, lands):
        if kind == "sibling":
            rows = _half_rows(src.shape[-2], 1 - c)
            plan.append((src.at[rows] if len(src.shape) == 2 else src.at[:, rows], land, (x, y, 1 - c)))
            continue
        if kind == "allgather":
            peers = [(x, y, 1 - c)] + [(px, py, pc) for px, py in _peer_chips(x, y) for pc in (c, 1 - c)]
            plan += [(src, land.at[4 * x + 2 * y + c], peer) for peer in peers]
            continue
        for k, (px, py) in enumerate(_peer_chips(x, y)):
            if kind == "scatter":
                plan.append((src.at[2 * px + py], land.at[k], (px, py, c)))
            else:
                plan.append((src, land.at[2 * x + y], (px, py, c)))
    return plan


def _split_start(kind, srcs, land_shapes, name):
    n = len(srcs)

    def body(*refs):
        ins, lands = refs[:n], refs[n:2 * n]
        send_sems, recv_sems = refs[2 * n], refs[2 * n + 1]
        token = refs[-1]
        x, y, c = _mesh_pos()
        for i, (src, dst, to) in enumerate(_split_plan(kind, ins, lands, x, y, c)):
            pltpu.make_async_remote_copy(src_ref=src, dst_ref=dst, send_sem=send_sems.at[i], recv_sem=recv_sems.at[i],
                                         device_id=to, device_id_type=MESH).start()
        token[...] = jnp.zeros_like(token)

    zones = [lax.empty(s.shape, s.dtype) for s in land_shapes]
    n_sems = {"sibling": 1, "allgather": 7}.get(kind, 3) * n
    res = pl.pallas_call(
        body, name=name,
        out_shape=(pltpu.SemaphoreType.DMA((n_sems,)), pltpu.SemaphoreType.DMA((n_sems,)),
                   *[pltpu.HBM(a.shape, a.dtype) for a in srcs], *[pltpu.HBM(s.shape, s.dtype) for s in land_shapes],
                   jax.ShapeDtypeStruct((8, LANES), F32)),
        in_specs=[_HBM] * (2 * n), out_specs=(_SEM, _SEM, *[_HBM] * (2 * n), pl.BlockSpec(memory_space=pltpu.VMEM)),
        input_output_aliases={i: 2 + i for i in range(2 * n)},
        compiler_params=pltpu.CompilerParams(has_side_effects=pltpu.SideEffectType.DATAFLOW_SIDE_EFFECTING),
    )(*[_in_hbm(a) for a in srcs], *[_in_hbm(z) for z in zones])
    return dict(send=res[0], recv=res[1], srcs=list(res[2:2 + n]), lands=list(res[2 + n:2 + 2 * n]), token=res[-1], kind=kind)


def _split_wait(started, after, name):
    n = len(started["srcs"])
    kind = started["kind"]

    def body(*refs):
        ins, lands = refs[:n], refs[n:2 * n]
        send_sems, recv_sems = refs[2 * n], refs[2 * n + 1]
        x, y, c = _mesh_pos()
        for i, (src, dst, _) in enumerate(_split_plan(kind, ins, lands, x, y, c)):
            cp = pltpu.make_async_remote_copy(src_ref=src, dst_ref=dst, send_sem=send_sems.at[i], recv_sem=recv_sems.at[i],
                                              device_id=(x, y, c), device_id_type=MESH)
            cp.wait_send()
            cp.wait_recv()

    arrs = started["srcs"] + started["lands"]
    res = pl.pallas_call(
        body, name=name, out_shape=tuple(pltpu.HBM(a.shape, a.dtype) for a in arrs),
        in_specs=[_HBM] * (2 * n) + [_SEM, _SEM, pl.BlockSpec(memory_space=pl.ANY)], out_specs=tuple([_HBM] * (2 * n)),
        input_output_aliases={i: i for i in range(2 * n)},
        compiler_params=pltpu.CompilerParams(has_side_effects=pltpu.SideEffectType.DATAFLOW_SIDE_EFFECTING),
    )(*arrs, started["send"], started["recv"], after)
    return list(res[:n]), list(res[n:])


def _sibling_share(fs, name):
    n = len(fs)

    def body(*refs):
        ins, outs = refs[:n], refs[n:2 * n]
        send_sems, recv_sems = refs[2 * n:]
        x, y, c = _mesh_pos()
        copies = [_remote(ins[i], outs[i], send_sems, recv_sems, i, (x, y, 1 - c)) for i in range(n)]
        for cp in copies:
            cp.start()
        for cp in copies:
            cp.wait()

    out_shapes = [jax.ShapeDtypeStruct(a.shape, a.dtype) for a in fs]
    return _comm_call(body, list(fs), out_shapes, n, name)


def _allgather8(v, name, after=()):
    m = v.shape[0]

    def body(v_ref, *rest):
        out_ref, send_sems, recv_sems = rest[len(after):]
        x, y, c = _mesh_pos()
        me, sib = (x, y, c), (x, y, 1 - c)
        chips = _peer_chips(x, y)
        rem = functools.partial(_remote, send_sems=send_sems, recv_sems=recv_sems)

        def blk(px, py, pc):
            return out_ref.at[4 * px + 2 * py + pc]

        first = [rem(v_ref, blk(*me), sem=0, to=sib)]
        first += [rem(v_ref, blk(*me), sem=1 + k, to=(px, py, c)) for k, (px, py) in enumerate(chips)]
        for cp in first:
            cp.start()
        passed = []
        for k, (px, py) in enumerate(chips):
            landed = blk(px, py, c)
            rem(landed, landed, sem=1 + k, to=me).wait_recv()
            fwd = rem(landed, landed, sem=4 + k, to=sib)
            fwd.start()
            passed.append(fwd)
        rem(blk(*sib), blk(*sib), sem=0, to=me).wait_recv()
        for k, (px, py) in enumerate(chips):
            theirs = blk(px, py, 1 - c)
            rem(theirs, theirs, sem=4 + k, to=me).wait_recv()
        for cp in first + passed:
            cp.wait_send()

    return _comm_call(body, [v, *after], [jax.ShapeDtypeStruct((8, m, LANES), v.dtype)], 7, name)[0]


_WEIGHTS = ["norm1_g", "w_in", "conv_a_w", "conv_a_b", "ln_a_g", "ln_a_b", "ln_b_g", "ln_b_b", "w_spatial", "b_spatial",
            "conv_c_w", "conv_c_b", "dt_bias", "a_log", "d_skip", "norm_c_g", "w_out", "norm2_g", "w_ff1", "w_ff2", "final_g"]
_BIG = ["w_in", "w_out", "w_ff1", "w_ff2"]
_CONV_SHARDED = ["conv_a_w", "conv_c_w"]
_SMALL = [w for w in _WEIGHTS if w not in _BIG and w != "final_g"]
_PACK_ROWS = 512


def _pack(arrs):
    flat = jnp.concatenate([a.reshape(-1) for a in arrs])
    blk = _PACK_ROWS * LANES
    n = flat.shape[0]
    return jnp.pad(flat, (0, -(-n // blk) * blk - n)).reshape(-1, LANES)


def _unpack(packed, shapes):
    flat = packed.reshape(-1)
    out, off = [], 0
    for s in shapes:
        n = math.prod(s)
        out.append(flat[off:off + n].reshape(s))
        off += n
    return out


def _chips_to_cols(a):
    return a.transpose(1, 0, 2).reshape(a.shape[1], -1)


def _own_shards(w, li):
    return [w[k][li].astype(BF16) for k in _BIG] + [w[k][li] for k in _CONV_SHARDED]


def _assemble_w_in(gathered, own, name):
    k = own.shape[0]
    tr = _pick(k, (256, 128, 64, 32, 16))
    last = D_MAIN - (N_CHIPS - 1) * IN_SHARD

    def body(g_ref, own_ref, main_ref, dt_ref):
        x, y, _ = _mesh_pos()
        q = 2 * x + y
        dt_ref[...] = jnp.zeros_like(dt_ref)

        def place(read):
            for p in range(N_CHIPS):
                def _(p=p):
                    wid = IN_SHARD if p < N_CHIPS - 1 else last
                    main_ref[:, pl.ds(IN_SHARD * p, wid)] = read(p, pl.ds(0, wid))
                    if p == N_CHIPS - 1:
                        for grp in range(2):
                            dt_ref[:, pl.ds(grp * LANES, SSD_GROUP_HEADS)] = read(
                                p, pl.ds(last + grp * SSD_GROUP_HEADS, SSD_GROUP_HEADS))
                yield p, _

        for p, put in place(lambda p, cols: own_ref[:, cols]):
            pl.when(q == p)(put)
        for p, put in place(lambda p, cols: g_ref[p, :, cols]):
            pl.when(q != p)(put)

    return pl.pallas_call(
        body, name=name, grid=(k // tr,),
        in_specs=[pl.BlockSpec((N_CHIPS, tr, IN_SHARD), lambda i: (0, i, 0)), pl.BlockSpec((tr, IN_SHARD), lambda i: (i, 0))],
        out_specs=[pl.BlockSpec((tr, D_MAIN), lambda i: (i, 0)), pl.BlockSpec((tr, 2 * LANES), lambda i: (i, 0))],
        out_shape=[jax.ShapeDtypeStruct((k, D_MAIN), own.dtype), jax.ShapeDtypeStruct((k, 2 * LANES), own.dtype)],
        compiler_params=_cparams("parallel"),
    )(gathered, own)


def _mixer_params(w, li, own, gathered, q):
    g_ca, g_cc = [lax.dynamic_update_index_in_dim(g, o, q, axis=0) for g, o in zip(gathered[1:], own[1:])]
    p = {k: w[k][li] for k in _SMALL if k not in _CONV_SHARDED}
    p["w_main"], p["w_dt"] = _assemble_w_in(gathered[0], own[0], f"l{li}_w_in")
    p["conv_a_w"] = _chips_to_cols(g_ca)
    p["conv_c_w"] = _chips_to_cols(g_cc)
    return p


def _ffn_out_params(own, gathered, q):
    g_out, g_ff1, g_ff2 = [lax.dynamic_update_index_in_dim(g, o, q, axis=0) for g, o in zip(gathered, own)]
    return dict(w_out=g_out.reshape(D_MIX, D_MODEL), w_ff1=g_ff1, w_ff2=g_ff2.reshape(D_FF, D_MODEL))


def _layer_params(w, li, own, gathered, q):
    p = _mixer_params(w, li, [own[0]] + own[4:], [gathered[0]] + gathered[4:], q)
    p.update(_ffn_out_params(own[1:4], gathered[1:4], q))
    return p


def _ffn_out_grads(g):
    return [g["w_out"].reshape(N_CHIPS, -1, D_MODEL), g["w_ff1"], g["w_ff2"].reshape(N_CHIPS, -1, D_MODEL)]


def _half_shape(a):
    return jax.ShapeDtypeStruct(a.shape[:-2] + (a.shape[-2] // 2, a.shape[-1]), a.dtype)


def _chip_sums(g, early, early_from_sib, li, c, q):
    n = f"l{li}_rs_"
    late = [g["w_main"], g["w_dt"]]
    full = late + list(early)
    from_sib = list(_sibling_other_halves(late, n + "sib")) + list(early_from_sib)
    mine = [lax.dynamic_slice_in_dim(a, c * b.shape[-2], b.shape[-2], axis=a.ndim - 2) for a, b in zip(full, from_sib)]
    sums = [_chipsum_in(mine[0], mine[1], from_sib[0], from_sib[1], n + "chipsum0")]
    for i in range(2, len(full)):
        shape = from_sib[i].shape
        s32, s16 = _ew(lambda u, v: (u + v, u + v), [mine[i].reshape(-1, shape[-1]), from_sib[i].reshape(-1, shape[-1])],
                       (F32, BF16), n + f"chipsum{i - 1}")
        sums.append((s32.reshape(shape), s16.reshape(shape)))
    chip_f32 = [lax.dynamic_index_in_dim(s32, q, axis=0, keepdims=False) for s32, _ in sums]
    chip_bf16 = [s16 for _, s16 in sums]
    return chip_f32, chip_bf16


def _finish_reduce(chip_f32, from_chips, li, c):
    n = f"l{li}_rs_"
    halves = [_ew(lambda o, r0, r1, r2_: (((o + r0) + r1) + r2_,), [own, rb, rb, rb], (F32,), n + f"final{i}",
                  leads=[None, 0, 1, 2])[0] for i, (own, rb) in enumerate(zip(chip_f32, from_chips))]
    from_sib = _sibling_share(halves, n + "share")
    return [jnp.where(c == 0, jnp.concatenate([h, s], axis=0), jnp.concatenate([s, h], axis=0))
            for h, s in zip(halves, from_sib)]


def kernel(x, norm1_g, w_in, conv_a_w, conv_a_b, ln_a_g, ln_a_b, ln_b_g, ln_b_b, w_spatial, b_spatial, conv_c_w, conv_c_b, dt_bias, a_log, d_skip, norm_c_g, w_out, norm2_g, w_ff1, w_ff2, final_g, loss_target, m_norm1_g, m_w_in, m_conv_a_w, m_conv_a_b, m_ln_a_g, m_ln_a_b, m_ln_b_g, m_ln_b_b, m_w_spatial, m_b_spatial, m_conv_c_w, m_conv_c_b, m_dt_bias, m_a_log, m_d_skip, m_norm_c_g, m_w_out, m_norm2_g, m_w_ff1, m_w_ff2, m_final_g, v_norm1_g, v_w_in, v_conv_a_w, v_conv_a_b, v_ln_a_g, v_ln_a_b, v_ln_b_g, v_ln_b_b, v_w_spatial, v_b_spatial, v_conv_c_w, v_conv_c_b, v_dt_bias, v_a_log, v_d_skip, v_norm_c_g, v_w_out, v_norm2_g, v_w_ff1, v_w_ff2, v_final_g):
    given = dict(locals())
    w = {k: given[k] for k in _WEIGHTS}
    m = {k: given["m_" + k] for k in _WEIGHTS}
    v = {k: given["v_" + k] for k in _WEIGHTS}
    depth = w_in.shape[0]
    nseq, seq, d = x.shape
    xi, yi, ci = _mesh_pos()
    q = 2 * xi + yi

    own = [_own_shards(w, li) for li in range(depth)]
    nb = len(_BIG)
    h = x.reshape(nseq * seq, d)
    layer_params, saved = [], []

    def zones(srcs):
        return [jax.ShapeDtypeStruct((N_CHIPS,) + a.shape, a.dtype) for a in srcs]

    def start_gather(li, behind):
        srcs, _ = lax.optimization_barrier((own[li], behind))
        return _split_start("gather", srcs, zones(srcs), f"l{li}_gather_start")

    first_own = [own[0][0]] + own[0][nb:]
    first = list(_gather_weights(first_own[:1], first_own[1:], "l0_gather"))
    rest_srcs, _ = lax.optimization_barrier((own[0][1:nb], first))
    rest = _split_start("gather", rest_srcs, zones(rest_srcs), "l0_gather_rest_start")
    nxt = None

    def rest_of_layer0(ycat):
        nonlocal nxt
        rest_own, rest_all = _split_wait(rest, ycat, "l0_gather_rest_wait")
        if depth > 1:
            nxt = start_gather(1, rest_all)
        return _ffn_out_params(rest_own, rest_all, q), (() if nxt is None else (nxt["token"],))

    for li in range(depth):
        if li == 0:
            layer_params.append(_mixer_params(w, 0, first_own, first, q))
            h, s = _layer_fwd(h, layer_params[0], seq, 0, after=(rest["token"],), late=rest_of_layer0)
        else:
            own[li], gathered = _split_wait(nxt, h, f"l{li}_gather_wait")
            nxt = start_gather(li + 1, gathered) if li + 1 < depth else None
            layer_params.append(_layer_params(w, li, own[li], list(gathered), q))
            h, s = _layer_fwd(h, layer_params[li], seq, li, after=() if nxt is None else (nxt["token"],))
        saved.append(s)
    loss, dx, d_final = _loss_head(h, final_g, loss_target.reshape(nseq * seq, d))

    grads = [None] * depth
    big_grads = [None] * depth
    pending = None
    for li in reversed(range(depth)):
        swaps = []

        def early_swap(g, li=li, swaps=swaps):
            early = _ffn_out_grads(g)
            swaps.append(_split_start("sibling", early, [_half_shape(a) for a in early], f"l{li}_rs_sib_start"))
            return (swaps[0]["token"],)

        after = () if pending is None else (pending[1]["token"],)
        if li == 0 and depth > 1:
            early_pack = _pack([grads[lj][k] for lj in range(1, depth) for k in _SMALL])
            early_small = _split_start("allgather", [early_pack], [jax.ShapeDtypeStruct((8,) + early_pack.shape, F32)],
                                       "small_early_start")
            after = after + (early_small["token"],)
        dx, grads[li] = _layer_bwd(dx, layer_params[li], saved[li], seq, li, after=after, on_ffn_grads=early_swap)
        if pending is not None:
            lj, scatter, chip_f32 = pending
            big_grads[lj] = _finish_reduce(chip_f32, _split_wait(scatter, dx, f"l{lj}_rs_scatter_wait")[1], lj, ci)
        early, early_from_sib = _split_wait(swaps[0], dx, f"l{li}_rs_sib_wait")
        chip_f32, chip_bf16 = _chip_sums(grads[li], early, early_from_sib, li, ci, q)
        lands = [jax.ShapeDtypeStruct((3,) + a.shape[1:], a.dtype) for a in chip_bf16]
        pending = (li, _split_start("scatter", chip_bf16, lands, f"l{li}_rs_scatter_start"), chip_f32)
    grad_out, delta_out, m_out, v_out = {}, {}, {}, {}

    small_shapes = [grads[0][k].shape for k in _SMALL]
    me = 2 * q + ci

    def sum8(*blocks):
        acc = blocks[0]
        for b in blocks[1:]:
            acc = acc + b
        return (acc,)

    def total_of(gathered, own, name):
        full = lax.dynamic_update_index_in_dim(gathered, own, me, axis=0)
        return _ew(sum8, [full] * 8, (F32,), name, leads=list(range(8)))[0]

    last_pack = _pack([grads[0][k] for k in _SMALL] + [d_final, loss.reshape(1)])
    last_total = total_of(_allgather8(last_pack, "small_allgather", after=(pending[1]["token"],)), last_pack, "small_sum")
    summed = _unpack(last_total, small_shapes + [d_final.shape, (1,)])
    tail = summed[len(_SMALL):]
    summed = summed[:len(_SMALL)]
    if depth > 1:
        (early_own,), (early_all,) = _split_wait(early_small, dx, "small_early_wait")
        summed += _unpack(total_of(early_all, early_own, "small_early_sum"), small_shapes * (depth - 1))
    summed += tail
    loss_total = summed[-1][0]
    small_grads = {k: jnp.stack([summed[li * len(_SMALL) + i] for li in range(depth)]) for i, k in enumerate(_SMALL)}
    small_grads["final_g"] = summed[-2]
    for k in _CONV_SHARDED:
        n_shard = w[k].shape[-1]
        small_grads[k] = lax.dynamic_slice_in_dim(small_grads[k], q * n_shard, n_shard, axis=2)
    names = _SMALL + ["final_g"]
    shapes = [w[k].shape for k in names]
    packed = [_pack([src[k] for k in names]) for src in (w, small_grads, m, v)]
    outs = _ew(_adam_fn, packed, (F32, F32, F32), "adam_small")
    for dst, o in zip((delta_out, m_out, v_out), outs):
        for k, a in zip(names, _unpack(o, shapes)):
            dst[k] = a
    for k in names:
        grad_out[k] = small_grads[k]

    lj, scatter, chip_f32 = pending
    big_grads[lj] = _finish_reduce(chip_f32, _split_wait(scatter, outs[0], f"l{lj}_rs_scatter_wait")[1], lj, ci)
    for i, k in enumerate(_BIG):
        grad_out[k] = jnp.stack([big_grads[li][i] for li in range(depth)])
        delta_out[k], m_out[k], v_out[k] = _adam(w[k], grad_out[k], m[k], v[k], "adam_" + k)

    return (loss_total, dx.reshape(nseq, seq, d), *[grad_out[k] for k in _WEIGHTS], *[delta_out[k] for k in _WEIGHTS],
            *[m_out[k] for k in _WEIGHTS], *[v_out[k] for k in _WEIGHTS])
```

```python
import functools
import math

import jax
import jax.numpy as jnp
from jax import lax
from jax.experimental import pallas as pl
from jax.experimental.pallas import tpu as pltpu

F32 = jnp.float32
BF16 = jnp.bfloat16
MESH = pl.DeviceIdType.MESH

D_MODEL = 1024
HEAD_DIM = 64
A_WIDTH = 512
B_WIDTH = 512
C_WIDTH = 1024
C_HEADS = 16
CONV_A_K = 31
CONV_C_K = 4
CHUNK = 128
SSM_STATE = 128
D_CONV_C = 1536
D_MAIN = 4608
D_IN_PROJ = 4624
D_MIX = 2048
D_FF = 4096
EPS = 1e-5
NEG = -1e30
LANES = 128
CONV_PAD = 32
N_CHIPS = 4

ADAM_LR = 0.001
ADAM_B1 = 0.9
ADAM_B2 = 0.999
ADAM_EPS = 1e-08
ADAM_WD = 0.01
ADAM_STEP = 10

VMEM_LIMIT = 56 * 1024 * 1024
MATMUL_VMEM_BUDGET = 44 * 1024 * 1024

COL_AVAL, COL_AGATE, COL_BU, COL_BV, COL_Z, COL_XBC = 0, 4, 8, 12, 16, 24


def _cparams(*sem):
    return pltpu.CompilerParams(dimension_semantics=sem, vmem_limit_bytes=VMEM_LIMIT)


_DN = {"nn": (((1,), (0,)), ((), ())), "nt": (((1,), (1,)), ((), ())), "tn": (((0,), (0,)), ((), ()))}


def _dot_raw(a, b, mode):
    return lax.dot_general(a.astype(BF16), b.astype(BF16), _DN[mode], preferred_element_type=F32)


def _make_dot(mode):
    @jax.custom_vjp
    def f(a, b):
        return _dot_raw(a, b, mode)

    def fwd(a, b):
        return _dot_raw(a, b, mode), (a, b)

    def bwd(res, g):
        a, b = res
        if mode == "nn":
            return _dot_raw(g, b, "nt"), _dot_raw(a, g, "tn")
        if mode == "nt":
            return _dot_raw(g, b, "nn"), _dot_raw(g, a, "tn")
        return _dot_raw(b, g, "nt"), _dot_raw(a, g, "nn")

    f.defvjp(fwd, bwd)
    return f


_nn = _make_dot("nn")
_nt = _make_dot("nt")
_tn = _make_dot("tn")


def _iota2(shape, dim):
    return lax.broadcasted_iota(jnp.int32, shape, dim)


def _gmean_impl(x):
    n = x.shape[-1]
    same = (_iota2((n, n), 0) < HEAD_DIM) == (_iota2((n, n), 1) < HEAD_DIM)
    p = jnp.where(same, 1.0 / HEAD_DIM, 0.0).astype(BF16)
    hi = x.astype(BF16)
    lo = (x - hi.astype(F32)).astype(BF16)
    dn = _DN["nn"]
    return (lax.dot_general(hi, p, dn, preferred_element_type=F32)
            + lax.dot_general(lo, p, dn, preferred_element_type=F32))


@jax.custom_vjp
def _gmean(x):
    return _gmean_impl(x)


_gmean.defvjp(lambda x: (_gmean_impl(x), None), lambda _, g: (_gmean_impl(g),))


def _sigmoid(x):
    return 1.0 / (1.0 + jnp.exp(-x))


def _silu(x):
    return x * _sigmoid(x)


def _gelu(x):
    return 0.5 * x * (1.0 + lax.erf(x * 0.7071067811865476))


def _softplus(x):
    return jnp.maximum(x, 0.0) + jnp.log(1.0 + jnp.exp(-jnp.abs(x)))


def _rms(x, g):
    return x * lax.rsqrt(jnp.mean(x * x, axis=-1, keepdims=True) + EPS) * g


def _ln64(x, g, b):
    mu = _gmean(x)
    xc = x - mu
    var = _gmean(xc * xc)
    return xc * lax.rsqrt(var + EPS) * g + b


def _lane_lt64(shape):
    return _iota2(shape, 1) < HEAD_DIM


def _pick(n, pref):
    for t in pref:
        if n % t == 0:
            return t
    return n


_UNREAD = pl.BlockSpec(memory_space=pl.ANY)


def _matmul_tiles(m, n_unit, k, a_item, b_item, out_bytes):
    best = None
    for tm in (1024, 512, 256, 128):
        for tn in (1536, 1024, 768, 512, 256, 128):
            if m % tm or n_unit % tn:
                continue
            need = 2 * k * (tm * a_item + tn * b_item) + 2 * tm * tn * out_bytes
            if need <= MATMUL_VMEM_BUDGET and (best is None or tm * tn > best[0] * best[1]):
                best = (tm, tn)
    assert best is not None, (m, n_unit, k)
    return best


def _matmul(a, b, *, mode, name, add=None, epilogue=None, extra=None, out_dtypes=(F32,), after=(), b_chips=False,
            out_chips=False):
    sh = b.shape[-1] if b_chips else None
    if mode == "nn":
        (m, k), n = a.shape, (N_CHIPS * sh if b_chips else b.shape[1])
    elif mode == "nt":
        (m, k), n = a.shape, b.shape[-2]
    else:
        (k, m), n = a.shape, b.shape[1]
    osh = n // N_CHIPS if out_chips else None
    out_bytes = sum(jnp.dtype(dt).itemsize for dt in out_dtypes) + (0 if add is None else add.dtype.itemsize) \
        + (0 if extra is None else extra.dtype.itemsize)
    tm, tn = _matmul_tiles(m, sh if (b_chips and mode == "nn") else (osh or n), k, a.dtype.itemsize, b.dtype.itemsize,
                           out_bytes)
    a_spec = pl.BlockSpec((k, tm), lambda i, j: (0, i)) if mode == "tn" else pl.BlockSpec((tm, k), lambda i, j: (i, 0))
    if b_chips and mode == "nn":
        per = sh // tn
        b_spec = pl.BlockSpec((None, k, tn), lambda i, j: (j // per, 0, j % per))
    elif b_chips:
        b_spec = pl.BlockSpec((N_CHIPS, tn, sh), lambda i, j: (0, j, 0))
    elif mode == "nt":
        b_spec = pl.BlockSpec((tn, k), lambda i, j: (j, 0))
    else:
        b_spec = pl.BlockSpec((k, tn), lambda i, j: (0, j))
    if out_chips:
        o_per = osh // tn
        o_spec = pl.BlockSpec((None, tm, tn), lambda i, j: (j // o_per, i, j % o_per))
        out_shape = [jax.ShapeDtypeStruct((N_CHIPS, m, osh), dt) for dt in out_dtypes]
    else:
        o_spec = pl.BlockSpec((tm, tn), lambda i, j: (i, j))
        out_shape = [jax.ShapeDtypeStruct((m, n), dt) for dt in out_dtypes]
    ins = [a, b]
    in_specs = [a_spec, b_spec]
    if add is not None:
        ins.append(add)
        in_specs.append(o_spec)
    if extra is not None:
        ins.append(extra)
        in_specs.append(o_spec)
    ins += list(after)
    in_specs += [_UNREAD] * len(after)
    n_out = len(out_dtypes)

    def body(*refs):
        a_ref, b_ref = refs[0], refs[1]
        pos = 2
        add_ref = ex_ref = None
        if add is not None:
            add_ref = refs[pos]
            pos += 1
        if extra is not None:
            ex_ref = refs[pos]
            pos += 1
        pos += len(after)
        if b_chips and mode == "nt":
            acc = _dot_raw(a_ref[:, pl.ds(0, sh)], b_ref[0], mode)
            for chip in range(1, N_CHIPS):
                acc = acc + _dot_raw(a_ref[:, pl.ds(chip * sh, sh)], b_ref[chip], mode)
        else:
            acc = _dot_raw(a_ref[...], b_ref[...], mode)
        if add_ref is not None:
            acc = acc + add_ref[...].astype(F32)
        outs = (acc,) if epilogue is None else epilogue(acc, None if ex_ref is None else ex_ref[...])
        for o_ref, o in zip(refs[pos:pos + n_out], outs):
            o_ref[...] = o.astype(o_ref.dtype)

    res = pl.pallas_call(
        body, name=name, grid=(m // tm, n // tn), in_specs=in_specs, out_specs=[o_spec] * n_out, out_shape=out_shape,
        compiler_params=_cparams("parallel", "parallel"),
    )(*ins)
    return res[0] if n_out == 1 else res


def _relu2_epilogue(acc, _):
    r = jnp.maximum(acc, 0.0)
    return acc, r * r


def _relu2_bwd_epilogue(acc, u):
    return (acc * (2.0 * jnp.maximum(u, 0.0)),)


def _row_tile(t):
    return _pick(t, (512, 256, 128))


def _rms_fwd(x, g, name, after=()):
    t, d = x.shape
    tm = _row_tile(t)

    def body(x_ref, g_ref, *rest):
        o_ref = rest[-1]
        o_ref[...] = _rms(x_ref[...], g_ref[...]).astype(BF16)

    return pl.pallas_call(
        body, name=name, grid=(t // tm,),
        in_specs=[pl.BlockSpec((tm, d), lambda i: (i, 0)), pl.BlockSpec((1, d), lambda i: (0, 0))] + [_UNREAD] * len(after),
        out_specs=pl.BlockSpec((tm, d), lambda i: (i, 0)),
        out_shape=jax.ShapeDtypeStruct((t, d), BF16),
        compiler_params=_cparams("parallel"),
    )(x, g.reshape(1, d), *after)


def _rms_bwd(x, g, dh, dres, name):
    t, d = x.shape
    tm = _row_tile(t)

    def body(x_ref, g_ref, dh_ref, dres_ref, dx_ref, dx16_ref, dg_ref):
        @pl.when(pl.program_id(0) == 0)
        def _():
            dg_ref[...] = jnp.zeros_like(dg_ref)

        _, vjp = jax.vjp(_rms, x_ref[...], g_ref[...])
        dx, dg = vjp(dh_ref[...].astype(F32))
        dx = dx + dres_ref[...]
        dx_ref[...] = dx
        dx16_ref[...] = dx.astype(BF16)
        dg_ref[...] += dg

    row = pl.BlockSpec((tm, d), lambda i: (i, 0))
    vec = pl.BlockSpec((1, d), lambda i: (0, 0))
    dx, dx16, dg = pl.pallas_call(
        body, name=name, grid=(t // tm,),
        in_specs=[row, vec, row, row], out_specs=[row, row, vec],
        out_shape=[jax.ShapeDtypeStruct((t, d), F32), jax.ShapeDtypeStruct((t, d), BF16), jax.ShapeDtypeStruct((1, d), F32)],
        compiler_params=_cparams("arbitrary"),
    )(x, g.reshape(1, d), dh, dres)
    return dx, dx16, dg.reshape(d)


def _loss_head(x, g, target):
    t, d = x.shape
    tm = _row_tile(t)

    def loss_fn(xv, gv, tv):
        err = _rms(xv, gv) - tv
        return 0.5 * jnp.sum(jnp.mean(err * err, axis=-1, keepdims=True))

    def body(x_ref, g_ref, t_ref, loss_ref, dx_ref, dx16_ref, dg_ref):
        @pl.when(pl.program_id(0) == 0)
        def _():
            dg_ref[...] = jnp.zeros_like(dg_ref)
            loss_ref[...] = jnp.zeros_like(loss_ref)

        tv = t_ref[...]
        val, vjp = jax.vjp(lambda xv, gv: loss_fn(xv, gv, tv), x_ref[...], g_ref[...])
        dx, dg = vjp(jnp.ones((), F32))
        dx_ref[...] = dx
        dx16_ref[...] = dx.astype(BF16)
        dg_ref[...] += dg
        loss_ref[...] += jnp.full(loss_ref.shape, val, F32)

    row = pl.BlockSpec((tm, d), lambda i: (i, 0))
    vec = pl.BlockSpec((1, d), lambda i: (0, 0))
    loss, dx, dx16, dg = pl.pallas_call(
        body, name="loss_head", grid=(t // tm,),
        in_specs=[row, vec, row], out_specs=[pl.BlockSpec((1, LANES), lambda i: (0, 0)), row, row, vec],
        out_shape=[jax.ShapeDtypeStruct((1, LANES), F32), jax.ShapeDtypeStruct((t, d), F32),
                   jax.ShapeDtypeStruct((t, d), BF16), jax.ShapeDtypeStruct((1, d), F32)],
        compiler_params=_cparams("arbitrary"),
    )(x, g.reshape(1, d), target)
    return loss[0, 0], dx, dx16, dg.reshape(d)


def _pre_glu(val, gate):
    return val * _sigmoid(gate)


def _pre_id(x):
    return x


def _post_lnsilu(c, g, b):
    return _silu(_ln64(c, g, b))


def _post_silu(c):
    return _silu(c)


def _conv_cfg(kind):
    if kind == "a":
        return dict(k=CONV_A_K, pre=_pre_glu, post=_post_lnsilu, n_in=2, n_par=2, nblk=A_WIDTH // LANES,
                    cols=(COL_AVAL, COL_AGATE))
    return dict(k=CONV_C_K, pre=_pre_id, post=_post_silu, n_in=1, n_par=0, nblk=D_CONV_C // LANES,
                cols=(COL_XBC,))


def _conv_fwd(kind, proj, w, bias, params, seq, name, out_dtype=F32, keep_conv=False):
    cfg = _conv_cfg(kind)
    kt, pre, post, n_in = cfg["k"], cfg["pre"], cfg["post"], cfg["n_in"]
    t = proj.shape[0]
    nseq = t // seq
    c = cfg["nblk"] * LANES
    rt = min(256, seq)
    nrt = seq // rt
    off0 = CONV_PAD - (kt - 1)

    def body(*refs):
        in_refs = refs[:n_in]
        w_ref, b_ref = refs[n_in], refs[n_in + 1]
        par_refs = refs[n_in + 2:n_in + 2 + cfg["n_par"]]
        out_refs = refs[n_in + 2 + cfg["n_par"]:-1]
        hpad = refs[-1]
        hpad[pl.ds(0, CONV_PAD), :] = jnp.zeros((CONV_PAD, LANES), F32)
        for r in range(nrt):
            hpad[pl.ds(CONV_PAD + r * rt, rt), :] = pre(*[x[pl.ds(r * rt, rt), :] for x in in_refs])
        pars = [p[...] for p in par_refs]
        for r in range(nrt):
            acc = jnp.broadcast_to(b_ref[...], (rt, LANES))
            for k in range(kt):
                acc = acc + w_ref[pl.ds(k, 1), :] * hpad[pl.ds(off0 + k + r * rt, rt), :]
            out_refs[0][pl.ds(r * rt, rt), :] = post(acc, *pars).astype(out_dtype)
            if keep_conv:
                out_refs[1][pl.ds(r * rt, rt), :] = acc

    in_specs = [pl.BlockSpec((seq, LANES), functools.partial(lambda s, j, col: (s, col + j), col=col))
                for col in cfg["cols"]]
    vec = pl.BlockSpec((1, LANES), lambda s, j: (0, j))
    in_specs += [pl.BlockSpec((CONV_PAD, LANES), lambda s, j: (0, j)), vec] + [vec] * cfg["n_par"]
    blk = pl.BlockSpec((seq, LANES), lambda s, j: (s, j))
    res = pl.pallas_call(
        body, name=name, grid=(nseq, cfg["nblk"]),
        in_specs=in_specs, out_specs=[blk, blk] if keep_conv else [blk],
        out_shape=[jax.ShapeDtypeStruct((t, c), out_dtype)] + ([jax.ShapeDtypeStruct((t, c), F32)] if keep_conv else []),
        scratch_shapes=[pltpu.VMEM((seq + CONV_PAD, LANES), F32)],
        compiler_params=_cparams("parallel", "parallel"),
    )(*([proj] * n_in), w, bias, *params)
    return tuple(res) if keep_conv else res[0]


def _conv_bwd(kind, proj, w, bias, params, dy, seq, name, dy_col=0, conv_out=None):
    kept = conv_out is not None
    cfg = _conv_cfg(kind)
    kt, pre, post, n_in, n_par = cfg["k"], cfg["pre"], cfg["post"], cfg["n_in"], cfg["n_par"]
    t = proj.shape[0]
    nseq = t // seq
    c = cfg["nblk"] * LANES
    rt = min(256, seq)
    nrt = seq // rt
    off0 = CONV_PAD - (kt - 1)

    def body(*refs):
        in_refs = refs[:n_in]
        w_ref, b_ref = refs[n_in], refs[n_in + 1]
        par_refs = refs[n_in + 2:n_in + 2 + n_par]
        pos = n_in + 2 + n_par
        dy_ref = refs[pos]
        if kept:
            pos += 1
            conv_ref = refs[pos]
        din_refs = refs[pos + 1:pos + 1 + n_in]
        dw_ref, db_ref = refs[pos + 1 + n_in], refs[pos + 2 + n_in]
        dpar_refs = refs[pos + 3 + n_in:pos + 3 + n_in + n_par]
        hpad, dcpad = refs[pos + 3 + n_in + n_par:]

        @pl.when(pl.program_id(1) == 0)
        def _():
            dw_ref[...] = jnp.zeros_like(dw_ref)
            db_ref[...] = jnp.zeros_like(db_ref)
            for r in dpar_refs:
                r[...] = jnp.zeros_like(r)

        hpad[pl.ds(0, CONV_PAD), :] = jnp.zeros((CONV_PAD, LANES), F32)
        dcpad[pl.ds(seq, CONV_PAD), :] = jnp.zeros((CONV_PAD, LANES), F32)
        for r in range(nrt):
            hpad[pl.ds(CONV_PAD + r * rt, rt), :] = pre(*[x[pl.ds(r * rt, rt), :] for x in in_refs])
        pars = [p[...] for p in par_refs]
        for r in range(nrt):
            if kept:
                acc = conv_ref[pl.ds(r * rt, rt), :]
            else:
                acc = jnp.broadcast_to(b_ref[...], (rt, LANES))
                for k in range(kt):
                    acc = acc + w_ref[pl.ds(k, 1), :] * hpad[pl.ds(off0 + k + r * rt, rt), :]
            _, vjp = jax.vjp(post, acc, *pars)
            grads = vjp(dy_ref[pl.ds(r * rt, rt), :])
            dcpad[pl.ds(r * rt, rt), :] = grads[0]
            db_ref[...] += jnp.sum(grads[0], axis=0, keepdims=True)
            for ref, gpar in zip(dpar_refs, grads[1:]):
                ref[...] += gpar
        for r in range(nrt):
            dh = jnp.zeros((rt, LANES), F32)
            for k in range(kt):
                dh = dh + w_ref[pl.ds(k, 1), :] * dcpad[pl.ds(r * rt + kt - 1 - k, rt), :]
            _, vjp = jax.vjp(pre, *[x[pl.ds(r * rt, rt), :] for x in in_refs])
            for ref, gin in zip(din_refs, vjp(dh)):
                ref[pl.ds(r * rt, rt), :] = gin.astype(ref.dtype)
        for k in range(kt):
            s = jnp.zeros((1, LANES), F32)
            for r in range(nrt):
                s = s + jnp.sum(dcpad[pl.ds(r * rt, rt), :] * hpad[pl.ds(off0 + k + r * rt, rt), :],
                                axis=0, keepdims=True)
            dw_ref[pl.ds(k, 1), :] += s

    in_specs = [pl.BlockSpec((seq, LANES), functools.partial(lambda j, s, col: (s, col + j), col=col))
                for col in cfg["cols"]]
    vec = pl.BlockSpec((1, LANES), lambda j, s: (0, j))
    wspec = pl.BlockSpec((CONV_PAD, LANES), lambda j, s: (0, j))
    blk = pl.BlockSpec((seq, LANES), lambda j, s: (s, j))
    in_specs += [wspec, vec] + [vec] * n_par + [pl.BlockSpec((seq, LANES), lambda j, s: (s, dy_col + j))]
    in_specs += [blk] if kept else []
    out_specs = [blk] * n_in + [wspec, vec] + [vec] * n_par
    out_shape = ([jax.ShapeDtypeStruct((t, c), BF16)] * n_in
                 + [jax.ShapeDtypeStruct((CONV_PAD, c), F32), jax.ShapeDtypeStruct((1, c), F32)]
                 + [jax.ShapeDtypeStruct((1, c), F32)] * n_par)
    res = pl.pallas_call(
        body, name=name, grid=(cfg["nblk"], nseq),
        in_specs=in_specs, out_specs=out_specs, out_shape=out_shape,
        scratch_shapes=[pltpu.VMEM((seq + CONV_PAD, LANES), F32), pltpu.VMEM((seq + CONV_PAD, LANES), F32)],
        compiler_params=_cparams("parallel", "arbitrary"),
    )(*([proj] * n_in), w, bias, *params, dy, *([conv_out] if kept else []))
    return res[:n_in], res[n_in], res[n_in + 1], res[n_in + 2:]


def _gmlp_chunk(bu, bv, g, b, w0, w1, b0row, b1row):
    u = _gelu(bu)
    vn = _ln64(_gelu(bv), g, b)
    tri = _iota2((CHUNK, CHUNK), 0) >= _iota2((CHUNK, CHUNK), 1)
    m0 = _nn(jnp.where(tri, w0, 0.0), vn) + jnp.broadcast_to(b0row, (CHUNK, CHUNK)).T
    m1 = _nn(jnp.where(tri, w1, 0.0), vn) + jnp.broadcast_to(b1row, (CHUNK, CHUNK)).T
    return u * jnp.where(_lane_lt64((CHUNK, LANES)), m0, m1)


def _gmlp_specs(tm, order):
    def im(f):
        return lambda *ids: f(*order(*ids))
    return dict(
        bu=pl.BlockSpec((tm, LANES), im(lambda j, r: (r, COL_BU + j))),
        bv=pl.BlockSpec((tm, LANES), im(lambda j, r: (r, COL_BV + j))),
        vec=pl.BlockSpec((1, LANES), im(lambda j, r: (0, j))),
        ws=pl.BlockSpec((2, CHUNK, CHUNK), im(lambda j, r: (j, 0, 0))),
        bs=pl.BlockSpec((None, 2, CHUNK), im(lambda j, r: (j, 0, 0))),
        blk=pl.BlockSpec((tm, LANES), im(lambda j, r: (r, j))),
    )


def _gmlp_fwd(proj, ln_g, ln_b, w_s, b_s, name):
    t = proj.shape[0]
    tm = _row_tile(t)
    nch = tm // CHUNK
    sp = _gmlp_specs(tm, lambda r, j: (j, r))

    def body(bu_ref, bv_ref, g_ref, b_ref, ws_ref, bs_ref, o_ref):
        for ci in range(nch):
            rows = pl.ds(ci * CHUNK, CHUNK)
            o_ref[rows, :] = _gmlp_chunk(bu_ref[rows, :], bv_ref[rows, :], g_ref[...], b_ref[...], ws_ref[0], ws_ref[1],
                                         bs_ref[pl.ds(0, 1), :], bs_ref[pl.ds(1, 1), :]).astype(BF16)

    return pl.pallas_call(
        body, name=name, grid=(t // tm, B_WIDTH // LANES),
        in_specs=[sp["bu"], sp["bv"], sp["vec"], sp["vec"], sp["ws"], sp["bs"]],
        out_specs=sp["blk"], out_shape=jax.ShapeDtypeStruct((t, B_WIDTH), BF16),
        compiler_params=_cparams("parallel", "parallel"),
    )(proj, proj, ln_g, ln_b, w_s, b_s.reshape(B_WIDTH // LANES, 2, CHUNK))


def _gmlp_bwd(proj, ln_g, ln_b, w_s, b_s, dy, name, dy_col=0):
    t = proj.shape[0]
    tm = _row_tile(t)
    nch = tm // CHUNK
    sp = _gmlp_specs(tm, lambda j, r: (j, r))
    dy_spec = pl.BlockSpec((tm, LANES), lambda j, r: (r, dy_col + j))

    def body(bu_ref, bv_ref, g_ref, b_ref, ws_ref, bs_ref, dy_ref, dbu_ref, dbv_ref, dg_ref, db_ref, dws_ref, dbs_ref):
        @pl.when(pl.program_id(1) == 0)
        def _():
            for r in (dg_ref, db_ref, dws_ref, dbs_ref):
                r[...] = jnp.zeros_like(r)

        for ci in range(nch):
            rows = pl.ds(ci * CHUNK, CHUNK)
            _, vjp = jax.vjp(_gmlp_chunk, bu_ref[rows, :], bv_ref[rows, :], g_ref[...], b_ref[...],
                             ws_ref[0], ws_ref[1], bs_ref[pl.ds(0, 1), :], bs_ref[pl.ds(1, 1), :])
            dbu, dbv, dg, db, dw0, dw1, db0, db1 = vjp(dy_ref[rows, :])
            dbu_ref[rows, :] = dbu.astype(BF16)
            dbv_ref[rows, :] = dbv.astype(BF16)
            dg_ref[...] += dg
            db_ref[...] += db
            dws_ref[0] += dw0
            dws_ref[1] += dw1
            dbs_ref[pl.ds(0, 1), :] += db0
            dbs_ref[pl.ds(1, 1), :] += db1

    nh = B_WIDTH // LANES
    res = pl.pallas_call(
        body, name=name, grid=(nh, t // tm),
        in_specs=[sp["bu"], sp["bv"], sp["vec"], sp["vec"], sp["ws"], sp["bs"], dy_spec],
        out_specs=[sp["blk"], sp["blk"], sp["vec"], sp["vec"], sp["ws"], sp["bs"]],
        out_shape=[jax.ShapeDtypeStruct((t, B_WIDTH), BF16), jax.ShapeDtypeStruct((t, B_WIDTH), BF16),
                   jax.ShapeDtypeStruct((1, B_WIDTH), F32), jax.ShapeDtypeStruct((1, B_WIDTH), F32),
                   jax.ShapeDtypeStruct(w_s.shape, F32), jax.ShapeDtypeStruct((nh, 2, CHUNK), F32)],
        compiler_params=_cparams("parallel", "arbitrary"),
    )(proj, proj, ln_g, ln_b, w_s, b_s.reshape(nh, 2, CHUNK), dy)
    dbu, dbv, dg, db, dws, dbs = res
    return dbu, dbv, dg, db, dws, dbs.reshape(b_s.shape)


def _tri_apply(a, lower):
    l = a.shape[0]
    r, c = _iota2((l, l), 0), _iota2((l, l), 1)
    t = jnp.where((r >= c) if lower else (r <= c), 1.0, 0.0).astype(BF16)
    hi = a.astype(BF16)
    r1 = a - hi.astype(F32)
    mid = r1.astype(BF16)
    lo = (r1 - mid.astype(F32)).astype(BF16)
    dn = _DN["nn"]
    return (lax.dot_general(t, hi, dn, preferred_element_type=F32) + lax.dot_general(t, mid, dn, preferred_element_type=F32)
            + lax.dot_general(t, lo, dn, preferred_element_type=F32))


@jax.custom_vjp
def _cumsum_rows(a):
    return _tri_apply(a, True)


_cumsum_rows.defvjp(lambda a: (_tri_apply(a, True), None), lambda _, g: (_tri_apply(g, False),))

SSD_GROUP_HEADS = 8
SSD_GROUP_PAIRS = 4


def _ssd_group(x0, x1, x2, x3, dt_raw, bias, alog, bm, cm, p0, p1, p2, p3):
    xs, prevs = (x0, x1, x2, x3), (p0, p1, p2, p3)
    dt = _softplus(dt_raw + bias)
    a = dt * (-jnp.exp(alog))
    acs = _cumsum_rows(a)
    alast = jnp.sum(a, axis=0, keepdims=True)
    dt_t, acs_t = dt.T, acs.T
    cb = _nt(cm, bm)
    tri = _iota2((CHUNK, CHUNK), 0) >= _iota2((CHUNK, CHUNK), 1)
    lane = _iota2((CHUNK, LANES), 1)
    sub = _iota2((LANES, CHUNK), 0)
    lane1 = _iota2((1, LANES), 1)

    def column(v, i):
        return jnp.broadcast_to(jnp.sum(jnp.where(lane == i, v, 0.0), axis=1, keepdims=True), (CHUNK, LANES))

    def row(vt, i):
        return jnp.broadcast_to(jnp.sum(jnp.where(sub == i, vt, 0.0), axis=0, keepdims=True), (CHUNK, CHUNK))

    heads = []
    for i in range(SSD_GROUP_HEADS):
        col_a = column(acs, i)
        al = jnp.sum(jnp.where(lane1 == i, alast, 0.0), axis=1, keepdims=True)
        m = cb * jnp.exp(jnp.where(tri, col_a - row(acs_t, i), NEG)) * row(dt_t, i)
        heads.append((m, jnp.exp(col_a), column(dt, i) * jnp.exp(al - col_a), jnp.exp(al)))
    lo_lanes = _lane_lt64((CHUNK, LANES))
    lo_rows = _iota2((LANES, SSM_STATE), 0) < HEAD_DIM
    ys, news = [], []
    for j in range(SSD_GROUP_PAIRS):
        (m0, ea0, w0, cd0), (m1, ea1, w1, cd1) = heads[2 * j], heads[2 * j + 1]
        x, prev = xs[j], prevs[j]
        ydiag = jnp.where(lo_lanes, _nn(m0, x), _nn(m1, x))
        yoff = jnp.where(lo_lanes, _nt(cm * ea0, prev), _nt(cm * ea1, prev))
        states = jnp.where(lo_rows, _tn(x, bm * w0), _tn(x, bm * w1))
        ys.append(ydiag + yoff)
        news.append(prev * jnp.where(lo_rows, cd0, cd1) + states)
    return tuple(ys) + tuple(news)


SSD_GROUPS = 2


def _ssd2_specs(seq, rev):
    ncs = seq // CHUNK
    wide = SSD_GROUPS * LANES

    def row(s, c):
        return s * ncs + (ncs - 1 - c if rev else c)

    return dict(
        x=pl.BlockSpec((CHUNK, C_WIDTH), lambda s, c: (row(s, c), 0)),
        dt=pl.BlockSpec((CHUNK, wide), lambda s, c: (row(s, c), 0)),
        vec=pl.BlockSpec((1, wide), lambda s, c: (0, 0)),
        bm=pl.BlockSpec((CHUNK, wide), lambda s, c: (row(s, c), C_WIDTH // wide)),
        cm=pl.BlockSpec((CHUNK, wide), lambda s, c: (row(s, c), C_WIDTH // wide + 1)),
        st=pl.BlockSpec((None, C_WIDTH // LANES, LANES, SSM_STATE), lambda s, c: (row(s, c), 0, 0, 0)),
        ncs=ncs,
    )


def _lane_blocks(ref, grp):
    return [ref[:, pl.ds((grp * SSD_GROUP_PAIRS + j) * LANES, LANES)] for j in range(SSD_GROUP_PAIRS)]


def _group_block(ref, grp):
    return ref[:, pl.ds(grp * LANES, LANES)]


def _ssd2_fwd(xbc_act, dt_raw, dt_bias, a_log, seq, name):
    t = xbc_act.shape[0]
    sp = _ssd2_specs(seq, False)

    npair = SSD_GROUP_PAIRS

    def body(x_ref, dt_ref, bias_ref, alog_ref, bm_ref, cm_ref, y_ref, prev_ref, state):
        @pl.when(pl.program_id(1) == 0)
        def _():
            state[...] = jnp.zeros_like(state)

        for grp in range(SSD_GROUPS):
            prevs = [state[grp * npair + j] for j in range(npair)]
            for j in range(npair):
                prev_ref[grp * npair + j] = prevs[j]
            res = _ssd_group(*_lane_blocks(x_ref, grp), _group_block(dt_ref, grp), _group_block(bias_ref, grp),
                             _group_block(alog_ref, grp), _group_block(bm_ref, grp), _group_block(cm_ref, grp), *prevs)
            for j in range(npair):
                y_ref[:, pl.ds((grp * npair + j) * LANES, LANES)] = res[j]
                state[grp * npair + j] = res[npair + j]

    return pl.pallas_call(
        body, name=name, grid=(t // seq, sp["ncs"]),
        in_specs=[sp["x"], sp["dt"], sp["vec"], sp["vec"], sp["bm"], sp["cm"]],
        out_specs=[sp["x"], sp["st"]],
        out_shape=[jax.ShapeDtypeStruct((t, C_WIDTH), F32),
                   jax.ShapeDtypeStruct((t // CHUNK, C_WIDTH // LANES, LANES, SSM_STATE), F32)],
        scratch_shapes=[pltpu.VMEM((C_WIDTH // LANES, LANES, SSM_STATE), F32)],
        compiler_params=_cparams("parallel", "arbitrary"),
    )(xbc_act, dt_raw, dt_bias, a_log, xbc_act, xbc_act)


def _ssd2_bwd(xbc_act, dt_raw, dt_bias, a_log, prev_saved, dy, seq, name):
    t = xbc_act.shape[0]
    sp = _ssd2_specs(seq, True)
    npair = SSD_GROUP_PAIRS

    def body(x_ref, dt_ref, bias_ref, alog_ref, bm_ref, cm_ref, prev_ref, dy_ref,
             dx_ref, ddt_ref, dbias_ref, dalog_ref, dbm_ref, dcm_ref, dstate):
        @pl.when(pl.program_id(1) == 0)
        def _():
            dstate[...] = jnp.zeros_like(dstate)

        @pl.when(jnp.logical_and(pl.program_id(0) == 0, pl.program_id(1) == 0))
        def _():
            dbias_ref[...] = jnp.zeros_like(dbias_ref)
            dalog_ref[...] = jnp.zeros_like(dalog_ref)

        for grp in range(SSD_GROUPS):
            lanes = pl.ds(grp * LANES, LANES)
            _, vjp = jax.vjp(_ssd_group, *_lane_blocks(x_ref, grp), _group_block(dt_ref, grp), _group_block(bias_ref, grp),
                             _group_block(alog_ref, grp), _group_block(bm_ref, grp), _group_block(cm_ref, grp),
                             *[prev_ref[grp * npair + j] for j in range(npair)])
            grads = vjp(tuple(_lane_blocks(dy_ref, grp)) + tuple(dstate[grp * npair + j] for j in range(npair)))
            for j in range(npair):
                dx_ref[:, pl.ds((grp * npair + j) * LANES, LANES)] = grads[j]
                dstate[grp * npair + j] = grads[npair + 5 + j]
            ddt_ref[:, lanes] = grads[npair].astype(BF16)
            dbias_ref[:, lanes] += grads[npair + 1]
            dalog_ref[:, lanes] += grads[npair + 2]
            dbm_ref[:, lanes] = grads[npair + 3]
            dcm_ref[:, lanes] = grads[npair + 4]

    return pl.pallas_call(
        body, name=name, grid=(t // seq, sp["ncs"]),
        in_specs=[sp["x"], sp["dt"], sp["vec"], sp["vec"], sp["bm"], sp["cm"], sp["st"], sp["x"]],
        out_specs=[sp["x"], sp["dt"], sp["vec"], sp["vec"], sp["dt"], sp["dt"]],
        out_shape=[jax.ShapeDtypeStruct((t, C_WIDTH), F32), jax.ShapeDtypeStruct((t, 2 * LANES), BF16),
                   jax.ShapeDtypeStruct((1, 2 * LANES), F32), jax.ShapeDtypeStruct((1, 2 * LANES), F32),
                   jax.ShapeDtypeStruct((t, 2 * SSM_STATE), F32), jax.ShapeDtypeStruct((t, 2 * SSM_STATE), F32)],
        scratch_shapes=[pltpu.VMEM((C_WIDTH // LANES, LANES, SSM_STATE), F32)],
        compiler_params=_cparams("arbitrary", "arbitrary"),
    )(xbc_act, dt_raw, dt_bias, a_log, xbc_act, xbc_act, prev_saved, dy)


def _ssd2_assemble(dxs_ssd, dxs_skip, dbm, dcm, name):
    t = dxs_ssd.shape[0]
    tm = _row_tile(t)

    def body(a_ref, b_ref, dbm_ref, dcm_ref, o_ref):
        o_ref[:, pl.ds(0, C_WIDTH)] = a_ref[...] + b_ref[...]
        o_ref[:, pl.ds(C_WIDTH, 2 * SSM_STATE)] = dbm_ref[...]
        o_ref[:, pl.ds(C_WIDTH + 2 * SSM_STATE, 2 * SSM_STATE)] = dcm_ref[...]

    wide = pl.BlockSpec((tm, C_WIDTH), lambda i: (i, 0))
    narrow = pl.BlockSpec((tm, 2 * SSM_STATE), lambda i: (i, 0))
    return pl.pallas_call(
        body, name=name, grid=(t // tm,), in_specs=[wide, wide, narrow, narrow],
        out_specs=pl.BlockSpec((tm, D_CONV_C), lambda i: (i, 0)),
        out_shape=jax.ShapeDtypeStruct((t, D_CONV_C), F32),
        compiler_params=_cparams("parallel"),
    )(dxs_ssd, dxs_skip, dbm, dcm)


def _ssd_post_fn(y, xs, z, dskip, g):
    v = (y + dskip * xs) * _silu(z)
    return v * lax.rsqrt(jnp.mean(v * v, axis=-1, keepdims=True) + EPS) * g


def _ssd_post_specs(tm, order):
    gw = C_WIDTH // 2

    def im(f):
        return lambda *ids: f(*order(*ids))
    return dict(
        blk=pl.BlockSpec((tm, gw), im(lambda g, r: (r, g))),
        z=pl.BlockSpec((tm, gw), im(lambda g, r: (r, COL_Z * LANES // gw + g))),
        vec=pl.BlockSpec((1, gw), im(lambda g, r: (0, g))),
    )


def _ssd_post_fwd(y_ssd, xbc_act, proj, dskip64, norm_g, name):
    t = y_ssd.shape[0]
    tm = _row_tile(t)
    sp = _ssd_post_specs(tm, lambda r, g: (g, r))

    def body(y_ref, xs_ref, z_ref, ds_ref, g_ref, o_ref):
        o_ref[...] = _ssd_post_fn(y_ref[...], xs_ref[...], z_ref[...], ds_ref[...], g_ref[...]).astype(BF16)

    return pl.pallas_call(
        body, name=name, grid=(t // tm, 2),
        in_specs=[sp["blk"], sp["blk"], sp["z"], sp["vec"], sp["vec"]], out_specs=sp["blk"],
        out_shape=jax.ShapeDtypeStruct((t, C_WIDTH), BF16),
        compiler_params=_cparams("parallel", "parallel"),
    )(y_ssd, xbc_act, proj, dskip64, norm_g)


def _ssd_post_bwd(y_ssd, xbc_act, proj, dskip64, norm_g, dyc, name, dy_col=0):
    t = y_ssd.shape[0]
    tm = _row_tile(t)
    sp = _ssd_post_specs(tm, lambda g, r: (g, r))
    dy_spec = pl.BlockSpec((tm, C_WIDTH // 2), lambda g, r: (r, dy_col + g))

    def body(y_ref, xs_ref, z_ref, ds_ref, g_ref, dyc_ref, dy_ref, dxs_ref, dz_ref, dds_ref, dg_ref):
        @pl.when(pl.program_id(1) == 0)
        def _():
            dds_ref[...] = jnp.zeros_like(dds_ref)
            dg_ref[...] = jnp.zeros_like(dg_ref)

        _, vjp = jax.vjp(_ssd_post_fn, y_ref[...], xs_ref[...], z_ref[...], ds_ref[...], g_ref[...])
        dy, dxs, dz, dds, dg = vjp(dyc_ref[...])
        dy_ref[...] = dy
        dxs_ref[...] = dxs
        dz_ref[...] = dz.astype(BF16)
        dds_ref[...] += dds
        dg_ref[...] += dg

    wide = jax.ShapeDtypeStruct((t, C_WIDTH), F32)
    vec = jax.ShapeDtypeStruct((1, C_WIDTH), F32)
    return pl.pallas_call(
        body, name=name, grid=(2, t // tm),
        in_specs=[sp["blk"], sp["blk"], sp["z"], sp["vec"], sp["vec"], dy_spec],
        out_specs=[sp["blk"], sp["blk"], sp["blk"], sp["vec"], sp["vec"]],
        out_shape=[wide, wide, jax.ShapeDtypeStruct((t, C_WIDTH), BF16), vec, vec],
        compiler_params=_cparams("parallel", "arbitrary"),
    )(y_ssd, xbc_act, proj, dskip64, norm_g, dyc)


def _pad_taps(w):
    return jnp.pad(w, ((0, CONV_PAD - w.shape[0]), (0, 0)))


def _group_heads(a):
    pad = [(0, 0)] * (a.ndim - 1) + [(0, LANES - SSD_GROUP_HEADS)]
    return jnp.concatenate([jnp.pad(a[..., :SSD_GROUP_HEADS], pad), jnp.pad(a[..., SSD_GROUP_HEADS:], pad)], axis=-1)


def _ungroup_heads(a):
    return jnp.concatenate([a[..., :SSD_GROUP_HEADS], a[..., LANES:LANES + SSD_GROUP_HEADS]], axis=-1)


def _layer_fwd(x, p, seq, li, after=()):
    n = f"l{li}_"
    h1 = _rms_fwd(x, p["norm1_g"], n + "rms1", after=after)
    proj = _matmul(h1, p["w_main"], mode="nn", name=n + "inproj")
    dt_raw = _matmul(h1, p["w_dt"], mode="nn", name=n + "inproj_dt")
    row = lambda v: v.reshape(1, -1)
    ya, conv_a = _conv_fwd("a", proj, _pad_taps(p["conv_a_w"]), row(p["conv_a_b"]), (row(p["ln_a_g"]), row(p["ln_a_b"])),
                           seq, n + "conva", out_dtype=BF16, keep_conv=True)
    yb = _gmlp_fwd(proj, row(p["ln_b_g"]), row(p["ln_b_b"]), p["w_spatial"], p["b_spatial"], n + "gmlp")
    xbc_act = _conv_fwd("c", proj, _pad_taps(p["conv_c_w"]), row(p["conv_c_b"]), (), seq, n + "convc")
    y_ssd, prev = _ssd2_fwd(xbc_act, dt_raw, _group_heads(row(p["dt_bias"])), _group_heads(row(p["a_log"])), seq, n + "ssd")
    dskip64 = jnp.repeat(p["d_skip"], HEAD_DIM).reshape(1, C_WIDTH)
    yc = _ssd_post_fwd(y_ssd, xbc_act, proj, dskip64, row(p["norm_c_g"]), n + "ssdpost")
    ycat = jnp.concatenate([ya, yb, yc], axis=1)
    x1 = _matmul(ycat, p["w_out"], mode="nn", name=n + "outproj", add=x)
    h2 = _rms_fwd(x1, p["norm2_g"], n + "rms2")
    u, act = _matmul(h2, p["w_ff1"], mode="nn", name=n + "ff1", epilogue=_relu2_epilogue, out_dtypes=(F32, BF16),
                     b_chips=True)
    x2 = _matmul(act, p["w_ff2"], mode="nn", name=n + "ff2", add=x1)
    saved = dict(x=x, h1=h1, proj=proj, conv_a=conv_a, dt_raw=dt_raw, xbc_act=xbc_act, prev=prev, y_ssd=y_ssd,
                 dskip64=dskip64, ycat=ycat, x1=x1, h2=h2, u=u, act=act)
    return x2, saved


def _layer_bwd(dx2, dx2_16, p, s, seq, li, after=(), on_ffn_grads=None):
    n = f"l{li}_b_"
    row = lambda v: v.reshape(1, -1)
    g = {}
    du = _matmul(dx2_16, p["w_ff2"], mode="nt", name=n + "ff2_dx", epilogue=_relu2_bwd_epilogue, extra=s["u"],
                 out_dtypes=(BF16,), after=after)
    g["w_ff2"] = _matmul(s["act"], dx2_16, mode="tn", name=n + "ff2_dw")
    g["w_ff1"] = _matmul(s["h2"], du, mode="tn", name=n + "ff1_dw", out_chips=True)
    dh2 = _matmul(du, p["w_ff1"], mode="nt", name=n + "ff1_dx", b_chips=True)
    dx1, dx1_16, g["norm2_g"] = _rms_bwd(s["x1"], p["norm2_g"], dh2, dx2, n + "rms2")
    g["w_out"] = _matmul(s["ycat"], dx1_16, mode="tn", name=n + "out_dw")
    dycat = _matmul(dx1_16, p["w_out"], mode="nt", name=n + "out_dx",
                    after=() if on_ffn_grads is None else on_ffn_grads(g))
    proj = s["proj"]
    (dval, dgate), dwa, dba, (dlag, dlab) = _conv_bwd(
        "a", proj, _pad_taps(p["conv_a_w"]), row(p["conv_a_b"]), (row(p["ln_a_g"]), row(p["ln_a_b"])), dycat, seq,
        n + "conva", dy_col=0, conv_out=s["conv_a"])
    g["conv_a_w"], g["conv_a_b"], g["ln_a_g"], g["ln_a_b"] = dwa[:CONV_A_K], dba[0], dlag[0], dlab[0]
    dbu, dbv, dlbg, dlbb, g["w_spatial"], g["b_spatial"] = _gmlp_bwd(
        proj, row(p["ln_b_g"]), row(p["ln_b_b"]), p["w_spatial"], p["b_spatial"], dycat, n + "gmlp",
        dy_col=A_WIDTH // LANES)
    g["ln_b_g"], g["ln_b_b"] = dlbg[0], dlbb[0]
    dy_ssd, dxs_skip, dz, dds, dncg = _ssd_post_bwd(s["y_ssd"], s["xbc_act"], proj, s["dskip64"], row(p["norm_c_g"]),
                                                    dycat, n + "ssdpost", dy_col=(A_WIDTH + B_WIDTH) * 2 // C_WIDTH)
    g["norm_c_g"] = dncg[0]
    g["d_skip"] = dds.reshape(C_HEADS, HEAD_DIM).sum(axis=1)
    dxs, ddt_raw, ddtb, dalog, dbm, dcm = _ssd2_bwd(
        s["xbc_act"], s["dt_raw"], _group_heads(row(p["dt_bias"])), _group_heads(row(p["a_log"])), s["prev"], dy_ssd, seq,
        n + "ssd")
    g["dt_bias"], g["a_log"] = _ungroup_heads(ddtb)[0], _ungroup_heads(dalog)[0]
    dconv = _ssd2_assemble(dxs, dxs_skip, dbm, dcm, n + "ssdasm")
    (dxbc,), dwc, dbcv, _ = _conv_bwd("c", proj, _pad_taps(p["conv_c_w"]), row(p["conv_c_b"]), (), dconv, seq, n + "convc")
    g["conv_c_w"], g["conv_c_b"] = dwc[:CONV_C_K], dbcv[0]
    dproj = jnp.concatenate([dval, dgate, dbu, dbv, dz, dxbc], axis=1)
    g["w_main"] = _matmul(s["h1"], dproj, mode="tn", name=n + "in_dw")
    g["w_dt"] = _matmul(s["h1"], ddt_raw, mode="tn", name=n + "indt_dw")
    dh1 = _matmul(dproj, p["w_main"], mode="nt", name=n + "in_dx")
    dh1 = _matmul(ddt_raw, p["w_dt"], mode="nt", name=n + "indt_dx", add=dh1)
    dx, dx_16, g["norm1_g"] = _rms_bwd(s["x"], p["norm1_g"], dh1, dx1, n + "rms1")
    return dx, dx_16, g


EW_BLOCK_BYTES = 1 << 20


def _ew(fn, ins, out_dtypes, name, leads=None):
    leads = leads or [None] * len(ins)
    rows, c = ins[0].shape[-2:]
    tr = _pick(rows, [t for t in (2048, 1024, 512, 256, 128, 64, 32, 16, 8) if t * c * 4 <= EW_BLOCK_BYTES])
    n_in = len(ins)

    def spec(lead):
        if lead is None:
            return pl.BlockSpec((tr, c), lambda i: (i, 0))
        return pl.BlockSpec((None, tr, c), functools.partial(lambda i, k: (k, i, 0), k=lead))

    def body(*refs):
        outs = fn(*[r[...].astype(F32) for r in refs[:n_in]])
        for o_ref, o in zip(refs[n_in:], outs):
            o_ref[...] = o.astype(o_ref.dtype)

    return pl.pallas_call(
        body, name=name, grid=(rows // tr,),
        in_specs=[spec(l) for l in leads], out_specs=[spec(None)] * len(out_dtypes),
        out_shape=[jax.ShapeDtypeStruct((rows, c), dt) for dt in out_dtypes],
        compiler_params=_cparams("parallel"),
    )(*ins)


def _adam_fn(w, g, m, v):
    m2 = ADAM_B1 * m + (1.0 - ADAM_B1) * g
    v2 = ADAM_B2 * v + (1.0 - ADAM_B2) * (g * g)
    m_hat = m2 / (1.0 - ADAM_B1 ** ADAM_STEP)
    v_hat = v2 / (1.0 - ADAM_B2 ** ADAM_STEP)
    delta = -ADAM_LR * (m_hat / (jnp.sqrt(v_hat) + ADAM_EPS) + ADAM_WD * w)
    return delta, m2, v2


def _adam(w, g, m, v, name):
    shape = w.shape
    two_d = lambda a: a.reshape(-1, shape[-1])
    outs = _ew(_adam_fn, [two_d(w), two_d(g), two_d(m), two_d(v)], (F32, F32, F32), name)
    return [o.reshape(shape) for o in outs]


_ANY = pl.BlockSpec(memory_space=pl.ANY)


def _mesh_pos():
    return lax.axis_index("x"), lax.axis_index("y"), lax.axis_index("c")


def _peer_chips(x, y):
    return [(1 - x, y), (x, 1 - y), (1 - x, 1 - y)]


def _remote(src, dst, send_sems, recv_sems, sem, to):
    return pltpu.make_async_remote_copy(src_ref=src, dst_ref=dst, send_sem=send_sems.at[sem],
                                        recv_sem=recv_sems.at[sem], device_id=to, device_id_type=MESH)


def _half_rows(n_rows, which):
    half = n_rows // 2
    return pl.ds(pl.multiple_of(which * half, 8), half)


def _comm_call(body, ins, out_shapes, n_sems, name):
    scratch = [pltpu.SemaphoreType.DMA((n_sems,)), pltpu.SemaphoreType.DMA((n_sems,))]
    return pl.pallas_call(
        body, name=name, in_specs=[_ANY] * len(ins), out_specs=[_ANY] * len(out_shapes),
        out_shape=out_shapes, scratch_shapes=scratch,
    )(*ins)


def _gather_weights(big, small, name):
    nb, ns = len(big), len(small)
    n = nb + ns

    def body(*refs):
        ins, outs = refs[:n], refs[n:2 * n]
        send_sems, recv_sems = refs[2 * n:]
        x, y, c = _mesh_pos()
        q = 2 * x + y
        me, sib = (x, y, c), (x, y, 1 - c)
        chips = _peer_chips(x, y)
        rem = functools.partial(_remote, send_sems=send_sems, recv_sems=recv_sems)
        first = []
        for i in range(nb):
            mine = _half_rows(big[i].shape[0], c)
            for k, (px, py) in enumerate(chips):
                first.append(rem(ins[i].at[mine], outs[i].at[q, mine], sem=6 * i + k, to=(px, py, c)))
        for j in range(ns):
            for k, (px, py) in enumerate(chips):
                first.append(rem(ins[nb + j], outs[nb + j].at[q], sem=6 * nb + 3 * j + k, to=(px, py, c)))
        for cp in first:
            cp.start()
        passed = []
        for i in range(nb):
            mine = _half_rows(big[i].shape[0], c)
            for k, (px, py) in enumerate(chips):
                landed = outs[i].at[2 * px + py, mine]
                rem(landed, landed, sem=6 * i + k, to=me).wait_recv()
                fwd = rem(landed, landed, sem=6 * i + 3 + k, to=sib)
                fwd.start()
                passed.append(fwd)
        for i in range(nb):
            other = _half_rows(big[i].shape[0], 1 - c)
            for k, (px, py) in enumerate(chips):
                theirs = outs[i].at[2 * px + py, other]
                rem(theirs, theirs, sem=6 * i + 3 + k, to=me).wait_recv()
        for j in range(ns):
            for k, (px, py) in enumerate(chips):
                dst = outs[nb + j].at[2 * px + py]
                rem(dst, dst, sem=6 * nb + 3 * j + k, to=me).wait_recv()
        for cp in first + passed:
            cp.wait_send()

    out_shapes = [jax.ShapeDtypeStruct((N_CHIPS,) + a.shape, a.dtype) for a in list(big) + list(small)]
    return _comm_call(body, list(big) + list(small), out_shapes, 6 * nb + 3 * ns, name)


def _sibling_other_halves(gs, name):
    n = len(gs)

    def other_half(ref, shape, c):
        rows = _half_rows(shape[-2], 1 - c)
        return ref.at[rows] if len(shape) == 2 else ref.at[:, rows]

    def body(*refs):
        ins, outs = refs[:n], refs[n:2 * n]
        send_sems, recv_sems = refs[2 * n:]
        x, y, c = _mesh_pos()
        copies = [_remote(other_half(ins[i], gs[i].shape, c), outs[i], send_sems, recv_sems, i, (x, y, 1 - c))
                  for i in range(n)]
        for cp in copies:
            cp.start()
        for cp in copies:
            cp.wait()

    out_shapes = [jax.ShapeDtypeStruct(g.shape[:-2] + (g.shape[-2] // 2, g.shape[-1]), g.dtype) for g in gs]
    return _comm_call(body, list(gs), out_shapes, n, name)


IN_SHARD = D_IN_PROJ // N_CHIPS


def _chipsum_in(mine, mine_dt, theirs, theirs_dt, name):
    r = mine.shape[0]
    tr = _pick(r, (128, 64, 32, 16, 8))
    last = D_MAIN - (N_CHIPS - 1) * IN_SHARD

    def body(a_ref, adt_ref, b_ref, bdt_ref, o32_ref, o16_ref):
        for p in range(N_CHIPS):
            wid = IN_SHARD if p < N_CHIPS - 1 else last
            s = a_ref[:, pl.ds(IN_SHARD * p, wid)] + b_ref[:, pl.ds(IN_SHARD * p, wid)]
            o32_ref[p, :, pl.ds(0, wid)] = s
            o16_ref[p, :, pl.ds(0, wid)] = s.astype(BF16)
        for grp in range(2):
            src = pl.ds(grp * LANES, SSD_GROUP_HEADS)
            s = adt_ref[:, src] + bdt_ref[:, src]
            dst = pl.ds(last + grp * SSD_GROUP_HEADS, SSD_GROUP_HEADS)
            o32_ref[N_CHIPS - 1, :, dst] = s
            o16_ref[N_CHIPS - 1, :, dst] = s.astype(BF16)

    wide = pl.BlockSpec((tr, D_MAIN), lambda i: (i, 0))
    narrow = pl.BlockSpec((tr, 2 * LANES), lambda i: (i, 0))
    out = pl.BlockSpec((N_CHIPS, tr, IN_SHARD), lambda i: (0, i, 0))
    return pl.pallas_call(
        body, name=name, grid=(r // tr,), in_specs=[wide, narrow, wide, narrow], out_specs=[out, out],
        out_shape=[jax.ShapeDtypeStruct((N_CHIPS, r, IN_SHARD), F32), jax.ShapeDtypeStruct((N_CHIPS, r, IN_SHARD), BF16)],
        compiler_params=_cparams("parallel"),
    )(mine, mine_dt, theirs, theirs_dt)


_HBM = pl.BlockSpec(memory_space=pltpu.HBM)
_SEM = pl.BlockSpec(memory_space=pltpu.SEMAPHORE)


def _in_hbm(a):
    return pltpu.with_memory_space_constraint(a, pltpu.HBM)


def _split_plan(kind, srcs, lands, x, y, c):
    plan = []
    for src, land in zip(srcs, lands):
        if kind == "sibling":
            rows = _half_rows(src.shape[-2], 1 - c)
            plan.append((src.at[rows] if len(src.shape) == 2 else src.at[:, rows], land, (x, y, 1 - c)))
            continue
        if kind == "allgather":
            peers = [(x, y, 1 - c)] + [(px, py, pc) for px, py in _peer_chips(x, y) for pc in (c, 1 - c)]
            plan += [(src, land.at[4 * x + 2 * y + c], peer) for peer in peers]
            continue
        for k, (px, py) in enumerate(_peer_chips(x, y)):
            if kind == "scatter":
                plan.append((src.at[2 * px + py], land.at[k], (px, py, c)))
            else:
                plan.append((src, land.at[2 * x + y], (px, py, c)))
    return plan


def _split_start(kind, srcs, land_shapes, name):
    n = len(srcs)

    def body(*refs):
        ins, lands = refs[:n], refs[n:2 * n]
        send_sems, recv_sems = refs[2 * n], refs[2 * n + 1]
        token = refs[-1]
        x, y, c = _mesh_pos()
        for i, (src, dst, to) in enumerate(_split_plan(kind, ins, lands, x, y, c)):
            pltpu.make_async_remote_copy(src_ref=src, dst_ref=dst, send_sem=send_sems.at[i], recv_sem=recv_sems.at[i],
                                         device_id=to, device_id_type=MESH).start()
        token[...] = jnp.zeros_like(token)

    zones = [lax.empty(s.shape, s.dtype) for s in land_shapes]
    n_sems = {"sibling": 1, "allgather": 7}.get(kind, 3) * n
    res = pl.pallas_call(
        body, name=name,
        out_shape=(pltpu.SemaphoreType.DMA((n_sems,)), pltpu.SemaphoreType.DMA((n_sems,)),
                   *[pltpu.HBM(a.shape, a.dtype) for a in srcs], *[pltpu.HBM(s.shape, s.dtype) for s in land_shapes],
                   jax.ShapeDtypeStruct((8, LANES), F32)),
        in_specs=[_HBM] * (2 * n), out_specs=(_SEM, _SEM, *[_HBM] * (2 * n), pl.BlockSpec(memory_space=pltpu.VMEM)),
        input_output_aliases={i: 2 + i for i in range(2 * n)},
        compiler_params=pltpu.CompilerParams(has_side_effects=pltpu.SideEffectType.DATAFLOW_SIDE_EFFECTING),
    )(*[_in_hbm(a) for a in srcs], *[_in_hbm(z) for z in zones])
    return dict(send=res[0], recv=res[1], srcs=list(res[2:2 + n]), lands=list(res[2 + n:2 + 2 * n]), token=res[-1], kind=kind)


def _split_wait(started, after, name):
    n = len(started["srcs"])
    kind = started["kind"]

    def body(*refs):
        ins, lands = refs[:n], refs[n:2 * n]
        send_sems, recv_sems = refs[2 * n], refs[2 * n + 1]
        x, y, c = _mesh_pos()
        for i, (src, dst, _) in enumerate(_split_plan(kind, ins, lands, x, y, c)):
            cp = pltpu.make_async_remote_copy(src_ref=src, dst_ref=dst, send_sem=send_sems.at[i], recv_sem=recv_sems.at[i],
                                              device_id=(x, y, c), device_id_type=MESH)
            cp.wait_send()
            cp.wait_recv()

    arrs = started["srcs"] + started["lands"]
    res = pl.pallas_call(
        body, name=name, out_shape=tuple(pltpu.HBM(a.shape, a.dtype) for a in arrs),
        in_specs=[_HBM] * (2 * n) + [_SEM, _SEM, pl.BlockSpec(memory_space=pl.ANY)], out_specs=tuple([_HBM] * (2 * n)),
        input_output_aliases={i: i for i in range(2 * n)},
        compiler_params=pltpu.CompilerParams(has_side_effects=pltpu.SideEffectType.DATAFLOW_SIDE_EFFECTING),
    )(*arrs, started["send"], started["recv"], after)
    return list(res[:n]), list(res[n:])


def _sibling_share(fs, name):
    n = len(fs)

    def body(*refs):
        ins, outs = refs[:n], refs[n:2 * n]
        send_sems, recv_sems = refs[2 * n:]
        x, y, c = _mesh_pos()
        copies = [_remote(ins[i], outs[i], send_sems, recv_sems, i, (x, y, 1 - c)) for i in range(n)]
        for cp in copies:
            cp.start()
        for cp in copies:
            cp.wait()

    out_shapes = [jax.ShapeDtypeStruct(a.shape, a.dtype) for a in fs]
    return _comm_call(body, list(fs), out_shapes, n, name)


def _allgather8(v, name, after=()):
    m = v.shape[0]

    def body(v_ref, *rest):
        out_ref, send_sems, recv_sems = rest[len(after):]
        x, y, c = _mesh_pos()
        me, sib = (x, y, c), (x, y, 1 - c)
        chips = _peer_chips(x, y)
        rem = functools.partial(_remote, send_sems=send_sems, recv_sems=recv_sems)

        def blk(px, py, pc):
            return out_ref.at[4 * px + 2 * py + pc]

        first = [rem(v_ref, blk(*me), sem=0, to=sib)]
        first += [rem(v_ref, blk(*me), sem=1 + k, to=(px, py, c)) for k, (px, py) in enumerate(chips)]
        for cp in first:
            cp.start()
        passed = []
        for k, (px, py) in enumerate(chips):
            landed = blk(px, py, c)
            rem(landed, landed, sem=1 + k, to=me).wait_recv()
            fwd = rem(landed, landed, sem=4 + k, to=sib)
            fwd.start()
            passed.append(fwd)
        rem(blk(*sib), blk(*sib), sem=0, to=me).wait_recv()
        for k, (px, py) in enumerate(chips):
            theirs = blk(px, py, 1 - c)
            rem(theirs, theirs, sem=4 + k, to=me).wait_recv()
        for cp in first + passed:
            cp.wait_send()

    return _comm_call(body, [v, *after], [jax.ShapeDtypeStruct((8, m, LANES), v.dtype)], 7, name)[0]


_WEIGHTS = ["norm1_g", "w_in", "conv_a_w", "conv_a_b", "ln_a_g", "ln_a_b", "ln_b_g", "ln_b_b", "w_spatial", "b_spatial",
            "conv_c_w", "conv_c_b", "dt_bias", "a_log", "d_skip", "norm_c_g", "w_out", "norm2_g", "w_ff1", "w_ff2", "final_g"]
_BIG = ["w_in", "w_out", "w_ff1", "w_ff2"]
_CONV_SHARDED = ["conv_a_w", "conv_c_w"]
_SMALL = [w for w in _WEIGHTS if w not in _BIG and w != "final_g"]
_PACK_ROWS = 512


def _pack(arrs):
    flat = jnp.concatenate([a.reshape(-1) for a in arrs])
    blk = _PACK_ROWS * LANES
    n = flat.shape[0]
    return jnp.pad(flat, (0, -(-n // blk) * blk - n)).reshape(-1, LANES)


def _unpack(packed, shapes):
    flat = packed.reshape(-1)
    out, off = [], 0
    for s in shapes:
        n = math.prod(s)
        out.append(flat[off:off + n].reshape(s))
        off += n
    return out


def _chips_to_cols(a):
    return a.transpose(1, 0, 2).reshape(a.shape[1], -1)


def _own_shards(w, li):
    return [w[k][li].astype(BF16) for k in _BIG] + [w[k][li] for k in _CONV_SHARDED]


def _assemble_w_in(gathered, own, name):
    k = own.shape[0]
    tr = _pick(k, (256, 128, 64, 32, 16))
    last = D_MAIN - (N_CHIPS - 1) * IN_SHARD

    def body(g_ref, own_ref, main_ref, dt_ref):
        x, y, _ = _mesh_pos()
        q = 2 * x + y
        dt_ref[...] = jnp.zeros_like(dt_ref)

        def place(read):
            for p in range(N_CHIPS):
                def _(p=p):
                    wid = IN_SHARD if p < N_CHIPS - 1 else last
                    main_ref[:, pl.ds(IN_SHARD * p, wid)] = read(p, pl.ds(0, wid))
                    if p == N_CHIPS - 1:
                        for grp in range(2):
                            dt_ref[:, pl.ds(grp * LANES, SSD_GROUP_HEADS)] = read(
                                p, pl.ds(last + grp * SSD_GROUP_HEADS, SSD_GROUP_HEADS))
                yield p, _

        for p, put in place(lambda p, cols: own_ref[:, cols]):
            pl.when(q == p)(put)
        for p, put in place(lambda p, cols: g_ref[p, :, cols]):
            pl.when(q != p)(put)

    return pl.pallas_call(
        body, name=name, grid=(k // tr,),
        in_specs=[pl.BlockSpec((N_CHIPS, tr, IN_SHARD), lambda i: (0, i, 0)), pl.BlockSpec((tr, IN_SHARD), lambda i: (i, 0))],
        out_specs=[pl.BlockSpec((tr, D_MAIN), lambda i: (i, 0)), pl.BlockSpec((tr, 2 * LANES), lambda i: (i, 0))],
        out_shape=[jax.ShapeDtypeStruct((k, D_MAIN), own.dtype), jax.ShapeDtypeStruct((k, 2 * LANES), own.dtype)],
        compiler_params=_cparams("parallel"),
    )(gathered, own)


def _layer_params(w, li, own, gathered, q):
    g_out, g_ff1, g_ff2, g_ca, g_cc = [lax.dynamic_update_index_in_dim(g, o, q, axis=0)
                                       for g, o in zip(gathered[1:], own[1:])]
    p = {k: w[k][li] for k in _SMALL if k not in _CONV_SHARDED}
    p["w_main"], p["w_dt"] = _assemble_w_in(gathered[0], own[0], f"l{li}_w_in")
    p["w_out"] = g_out.reshape(D_MIX, D_MODEL)
    p["w_ff1"] = g_ff1
    p["w_ff2"] = g_ff2.reshape(D_FF, D_MODEL)
    p["conv_a_w"] = _chips_to_cols(g_ca)
    p["conv_c_w"] = _chips_to_cols(g_cc)
    return p


def _ffn_out_grads(g):
    return [g["w_out"].reshape(N_CHIPS, -1, D_MODEL), g["w_ff1"], g["w_ff2"].reshape(N_CHIPS, -1, D_MODEL)]


def _half_shape(a):
    return jax.ShapeDtypeStruct(a.shape[:-2] + (a.shape[-2] // 2, a.shape[-1]), a.dtype)


def _chip_sums(g, early, early_from_sib, li, c, q):
    n = f"l{li}_rs_"
    late = [g["w_main"], g["w_dt"]]
    full = late + list(early)
    from_sib = list(_sibling_other_halves(late, n + "sib")) + list(early_from_sib)
    mine = [lax.dynamic_slice_in_dim(a, c * b.shape[-2], b.shape[-2], axis=a.ndim - 2) for a, b in zip(full, from_sib)]
    sums = [_chipsum_in(mine[0], mine[1], from_sib[0], from_sib[1], n + "chipsum0")]
    for i in range(2, len(full)):
        shape = from_sib[i].shape
        s32, s16 = _ew(lambda u, v: (u + v, u + v), [mine[i].reshape(-1, shape[-1]), from_sib[i].reshape(-1, shape[-1])],
                       (F32, BF16), n + f"chipsum{i - 1}")
        sums.append((s32.reshape(shape), s16.reshape(shape)))
    chip_f32 = [lax.dynamic_index_in_dim(s32, q, axis=0, keepdims=False) for s32, _ in sums]
    chip_bf16 = [s16 for _, s16 in sums]
    return chip_f32, chip_bf16


def _finish_reduce(chip_f32, from_chips, li, c):
    n = f"l{li}_rs_"
    halves = [_ew(lambda o, r0, r1, r2_: (((o + r0) + r1) + r2_,), [own, rb, rb, rb], (F32,), n + f"final{i}",
                  leads=[None, 0, 1, 2])[0] for i, (own, rb) in enumerate(zip(chip_f32, from_chips))]
    from_sib = _sibling_share(halves, n + "share")
    return [jnp.where(c == 0, jnp.concatenate([h, s], axis=0), jnp.concatenate([s, h], axis=0))
            for h, s in zip(halves, from_sib)]


def kernel(x, norm1_g, w_in, conv_a_w, conv_a_b, ln_a_g, ln_a_b, ln_b_g, ln_b_b, w_spatial, b_spatial, conv_c_w, conv_c_b, dt_bias, a_log, d_skip, norm_c_g, w_out, norm2_g, w_ff1, w_ff2, final_g, loss_target, m_norm1_g, m_w_in, m_conv_a_w, m_conv_a_b, m_ln_a_g, m_ln_a_b, m_ln_b_g, m_ln_b_b, m_w_spatial, m_b_spatial, m_conv_c_w, m_conv_c_b, m_dt_bias, m_a_log, m_d_skip, m_norm_c_g, m_w_out, m_norm2_g, m_w_ff1, m_w_ff2, m_final_g, v_norm1_g, v_w_in, v_conv_a_w, v_conv_a_b, v_ln_a_g, v_ln_a_b, v_ln_b_g, v_ln_b_b, v_w_spatial, v_b_spatial, v_conv_c_w, v_conv_c_b, v_dt_bias, v_a_log, v_d_skip, v_norm_c_g, v_w_out, v_norm2_g, v_w_ff1, v_w_ff2, v_final_g):
    given = dict(locals())
    w = {k: given[k] for k in _WEIGHTS}
    m = {k: given["m_" + k] for k in _WEIGHTS}
    v = {k: given["v_" + k] for k in _WEIGHTS}
    depth = w_in.shape[0]
    nseq, seq, d = x.shape
    xi, yi, ci = _mesh_pos()
    q = 2 * xi + yi

    own = [_own_shards(w, li) for li in range(depth)]
    nb = len(_BIG)
    gathered = _gather_weights(own[0][:nb], own[0][nb:], "l0_gather")
    h = x.reshape(nseq * seq, d)
    layer_params, saved = [], []
    for li in range(depth):
        nxt = None
        if li + 1 < depth:
            srcs, _ = lax.optimization_barrier((own[li + 1], gathered))
            zones = [jax.ShapeDtypeStruct((N_CHIPS,) + a.shape, a.dtype) for a in srcs]
            nxt = _split_start("gather", srcs, zones, f"l{li + 1}_gather_start")
        layer_params.append(_layer_params(w, li, own[li], gathered, q))
        h, s = _layer_fwd(h, layer_params[li], seq, li, after=() if nxt is None else (nxt["token"],))
        saved.append(s)
        if nxt is not None:
            own[li + 1], gathered = _split_wait(nxt, h, f"l{li + 1}_gather_wait")
    loss, dx, dx_16, d_final = _loss_head(h, final_g, loss_target.reshape(nseq * seq, d))

    grads = [None] * depth
    big_grads = [None] * depth
    pending = None
    for li in reversed(range(depth)):
        swaps = []

        def early_swap(g, li=li, swaps=swaps):
            early = _ffn_out_grads(g)
            swaps.append(_split_start("sibling", early, [_half_shape(a) for a in early], f"l{li}_rs_sib_start"))
            return (swaps[0]["token"],)

        after = () if pending is None else (pending[1]["token"],)
        if li == 0 and depth > 1:
            early_pack = _pack([grads[lj][k] for lj in range(1, depth) for k in _SMALL])
            early_small = _split_start("allgather", [early_pack], [jax.ShapeDtypeStruct((8,) + early_pack.shape, F32)],
                                       "small_early_start")
            after = after + (early_small["token"],)
        dx, dx_16, grads[li] = _layer_bwd(dx, dx_16, layer_params[li], saved[li], seq, li, after=after,
                                          on_ffn_grads=early_swap)
        if pending is not None:
            lj, scatter, chip_f32 = pending
            big_grads[lj] = _finish_reduce(chip_f32, _split_wait(scatter, dx, f"l{lj}_rs_scatter_wait")[1], lj, ci)
        early, early_from_sib = _split_wait(swaps[0], dx, f"l{li}_rs_sib_wait")
        chip_f32, chip_bf16 = _chip_sums(grads[li], early, early_from_sib, li, ci, q)
        lands = [jax.ShapeDtypeStruct((3,) + a.shape[1:], a.dtype) for a in chip_bf16]
        pending = (li, _split_start("scatter", chip_bf16, lands, f"l{li}_rs_scatter_start"), chip_f32)
    grad_out, delta_out, m_out, v_out = {}, {}, {}, {}

    small_shapes = [grads[0][k].shape for k in _SMALL]
    me = 2 * q + ci

    def sum8(*blocks):
        acc = blocks[0]
        for b in blocks[1:]:
            acc = acc + b
        return (acc,)

    def total_of(gathered, own, name):
        full = lax.dynamic_update_index_in_dim(gathered, own, me, axis=0)
        return _ew(sum8, [full] * 8, (F32,), name, leads=list(range(8)))[0]

    last_pack = _pack([grads[0][k] for k in _SMALL] + [d_final, loss.reshape(1)])
    last_total = total_of(_allgather8(last_pack, "small_allgather", after=(pending[1]["token"],)), last_pack, "small_sum")
    summed = _unpack(last_total, small_shapes + [d_final.shape, (1,)])
    tail = summed[len(_SMALL):]
    summed = summed[:len(_SMALL)]
    if depth > 1:
        (early_own,), (early_all,) = _split_wait(early_small, dx, "small_early_wait")
        summed += _unpack(total_of(early_all, early_own, "small_early_sum"), small_shapes * (depth - 1))
    summed += tail
    loss_total = summed[-1][0]
    small_grads = {k: jnp.stack([summed[li * len(_SMALL) + i] for li in range(depth)]) for i, k in enumerate(_SMALL)}
    small_grads["final_g"] = summed[-2]
    for k in _CONV_SHARDED:
        n_shard = w[k].shape[-1]
        small_grads[k] = lax.dynamic_slice_in_dim(small_grads[k], q * n_shard, n_shard, axis=2)
    names = _SMALL + ["final_g"]
    shapes = [w[k].shape for k in names]
    packed = [_pack([src[k] for k in names]) for src in (w, small_grads, m, v)]
    outs = _ew(_adam_fn, packed, (F32, F32, F32), "adam_small")
    for dst, o in zip((delta_out, m_out, v_out), outs):
        for k, a in zip(names, _unpack(o, shapes)):
            dst[k] = a
    for k in names:
        grad_out[k] = small_grads[k]

    lj, scatter, chip_f32 = pending
    big_grads[lj] = _finish_reduce(chip_f32, _split_wait(scatter, outs[0], f"l{lj}_rs_scatter_wait")[1], lj, ci)
    for i, k in enumerate(_BIG):
        grad_out[k] = jnp.stack([big_grads[li][i] for li in range(depth)])
        delta_out[k], m_out[k], v_out[k] = _adam(w[k], grad_out[k], m[k], v[k], "adam_" + k)

    return (loss_total, dx.reshape(nseq, seq, d), *[grad_out[k] for k in _WEIGHTS], *[delta_out[k] for k in _WEIGHTS],
            *[m_out[k] for k in _WEIGHTS], *[v_out[k] for k in _WEIGHTS])
```

```python
import functools
import math

import jax
import jax.numpy as jnp
from jax import lax
from jax.experimental import pallas as pl
from jax.experimental.pallas import tpu as pltpu

F32 = jnp.float32
BF16 = jnp.bfloat16
MESH = pl.DeviceIdType.MESH

D_MODEL = 1024
HEAD_DIM = 64
A_WIDTH = 512
B_WIDTH = 512
C_WIDTH = 1024
C_HEADS = 16
CONV_A_K = 31
CONV_C_K = 4
CHUNK = 128
SSM_STATE = 128
D_CONV_C = 1536
D_MAIN = 4608
D_IN_PROJ = 4624
D_MIX = 2048
D_FF = 4096
EPS = 1e-5
NEG = -1e30
LANES = 128
CONV_PAD = 32
N_CHIPS = 4

ADAM_LR = 0.001
ADAM_B1 = 0.9
ADAM_B2 = 0.999
ADAM_EPS = 1e-08
ADAM_WD = 0.01
ADAM_STEP = 10

VMEM_LIMIT = 56 * 1024 * 1024
MATMUL_VMEM_BUDGET = 44 * 1024 * 1024

COL_AVAL, COL_AGATE, COL_BU, COL_BV, COL_Z, COL_XBC = 0, 4, 8, 12, 16, 24


def _cparams(*sem):
    return pltpu.CompilerParams(dimension_semantics=sem, vmem_limit_bytes=VMEM_LIMIT)


_DN = {"nn": (((1,), (0,)), ((), ())), "nt": (((1,), (1,)), ((), ())), "tn": (((0,), (0,)), ((), ()))}


def _dot_raw(a, b, mode):
    return lax.dot_general(a.astype(BF16), b.astype(BF16), _DN[mode], preferred_element_type=F32)


def _make_dot(mode):
    @jax.custom_vjp
    def f(a, b):
        return _dot_raw(a, b, mode)

    def fwd(a, b):
        return _dot_raw(a, b, mode), (a, b)

    def bwd(res, g):
        a, b = res
        if mode == "nn":
            return _dot_raw(g, b, "nt"), _dot_raw(a, g, "tn")
        if mode == "nt":
            return _dot_raw(g, b, "nn"), _dot_raw(g, a, "tn")
        return _dot_raw(b, g, "nt"), _dot_raw(a, g, "nn")

    f.defvjp(fwd, bwd)
    return f


_nn = _make_dot("nn")
_nt = _make_dot("nt")
_tn = _make_dot("tn")


def _iota2(shape, dim):
    return lax.broadcasted_iota(jnp.int32, shape, dim)


def _gmean_impl(x):
    n = x.shape[-1]
    same = (_iota2((n, n), 0) < HEAD_DIM) == (_iota2((n, n), 1) < HEAD_DIM)
    p = jnp.where(same, 1.0 / HEAD_DIM, 0.0).astype(BF16)
    hi = x.astype(BF16)
    lo = (x - hi.astype(F32)).astype(BF16)
    dn = _DN["nn"]
    return (lax.dot_general(hi, p, dn, preferred_element_type=F32)
            + lax.dot_general(lo, p, dn, preferred_element_type=F32))


@jax.custom_vjp
def _gmean(x):
    return _gmean_impl(x)


_gmean.defvjp(lambda x: (_gmean_impl(x), None), lambda _, g: (_gmean_impl(g),))


def _sigmoid(x):
    return 1.0 / (1.0 + jnp.exp(-x))


def _silu(x):
    return x * _sigmoid(x)


def _gelu(x):
    return 0.5 * x * (1.0 + lax.erf(x * 0.7071067811865476))


def _softplus(x):
    return jnp.maximum(x, 0.0) + jnp.log(1.0 + jnp.exp(-jnp.abs(x)))


def _rms(x, g):
    return x * lax.rsqrt(jnp.mean(x * x, axis=-1, keepdims=True) + EPS) * g


def _ln64(x, g, b):
    mu = _gmean(x)
    xc = x - mu
    var = _gmean(xc * xc)
    return xc * lax.rsqrt(var + EPS) * g + b


def _lane_lt64(shape):
    return _iota2(shape, 1) < HEAD_DIM


def _pick(n, pref):
    for t in pref:
        if n % t == 0:
            return t
    return n


_UNREAD = pl.BlockSpec(memory_space=pl.ANY)


def _matmul_tiles(m, n_unit, k, a_item, b_item, out_bytes):
    best = None
    for tm in (1024, 512, 256, 128):
        for tn in (1536, 1024, 768, 512, 256, 128):
            if m % tm or n_unit % tn:
                continue
            need = 2 * k * (tm * a_item + tn * b_item) + 2 * tm * tn * out_bytes
            if need <= MATMUL_VMEM_BUDGET and (best is None or tm * tn > best[0] * best[1]):
                best = (tm, tn)
    assert best is not None, (m, n_unit, k)
    return best


def _matmul(a, b, *, mode, name, add=None, epilogue=None, extra=None, out_dtypes=(F32,), after=(), b_chips=False,
            out_chips=False):
    sh = b.shape[-1] if b_chips else None
    if mode == "nn":
        (m, k), n = a.shape, (N_CHIPS * sh if b_chips else b.shape[1])
    elif mode == "nt":
        (m, k), n = a.shape, b.shape[-2]
    else:
        (k, m), n = a.shape, b.shape[1]
    osh = n // N_CHIPS if out_chips else None
    out_bytes = sum(jnp.dtype(dt).itemsize for dt in out_dtypes) + (0 if add is None else add.dtype.itemsize) \
        + (0 if extra is None else extra.dtype.itemsize)
    tm, tn = _matmul_tiles(m, sh if (b_chips and mode == "nn") else (osh or n), k, a.dtype.itemsize, b.dtype.itemsize,
                           out_bytes)
    a_spec = pl.BlockSpec((k, tm), lambda i, j: (0, i)) if mode == "tn" else pl.BlockSpec((tm, k), lambda i, j: (i, 0))
    if b_chips and mode == "nn":
        per = sh // tn
        b_spec = pl.BlockSpec((None, k, tn), lambda i, j: (j // per, 0, j % per))
    elif b_chips:
        b_spec = pl.BlockSpec((N_CHIPS, tn, sh), lambda i, j: (0, j, 0))
    elif mode == "nt":
        b_spec = pl.BlockSpec((tn, k), lambda i, j: (j, 0))
    else:
        b_spec = pl.BlockSpec((k, tn), lambda i, j: (0, j))
    if out_chips:
        o_per = osh // tn
        o_spec = pl.BlockSpec((None, tm, tn), lambda i, j: (j // o_per, i, j % o_per))
        out_shape = [jax.ShapeDtypeStruct((N_CHIPS, m, osh), dt) for dt in out_dtypes]
    else:
        o_spec = pl.BlockSpec((tm, tn), lambda i, j: (i, j))
        out_shape = [jax.ShapeDtypeStruct((m, n), dt) for dt in out_dtypes]
    ins = [a, b]
    in_specs = [a_spec, b_spec]
    if add is not None:
        ins.append(add)
        in_specs.append(o_spec)
    if extra is not None:
        ins.append(extra)
        in_specs.append(o_spec)
    ins += list(after)
    in_specs += [_UNREAD] * len(after)
    n_out = len(out_dtypes)

    def body(*refs):
        a_ref, b_ref = refs[0], refs[1]
        pos = 2
        add_ref = ex_ref = None
        if add is not None:
            add_ref = refs[pos]
            pos += 1
        if extra is not None:
            ex_ref = refs[pos]
            pos += 1
        pos += len(after)
        if b_chips and mode == "nt":
            acc = _dot_raw(a_ref[:, pl.ds(0, sh)], b_ref[0], mode)
            for chip in range(1, N_CHIPS):
                acc = acc + _dot_raw(a_ref[:, pl.ds(chip * sh, sh)], b_ref[chip], mode)
        else:
            acc = _dot_raw(a_ref[...], b_ref[...], mode)
        if add_ref is not None:
            acc = acc + add_ref[...].astype(F32)
        outs = (acc,) if epilogue is None else epilogue(acc, None if ex_ref is None else ex_ref[...])
        for o_ref, o in zip(refs[pos:pos + n_out], outs):
            o_ref[...] = o.astype(o_ref.dtype)

    res = pl.pallas_call(
        body, name=name, grid=(m // tm, n // tn), in_specs=in_specs, out_specs=[o_spec] * n_out, out_shape=out_shape,
        compiler_params=_cparams("parallel", "parallel"),
    )(*ins)
    return res[0] if n_out == 1 else res


def _relu2_epilogue(acc, _):
    r = jnp.maximum(acc, 0.0)
    return acc, r * r


def _relu2_bwd_epilogue(acc, u):
    return (acc * (2.0 * jnp.maximum(u, 0.0)),)


def _row_tile(t):
    return _pick(t, (512, 256, 128))


def _rms_fwd(x, g, name, after=()):
    t, d = x.shape
    tm = _row_tile(t)

    def body(x_ref, g_ref, *rest):
        o_ref = rest[-1]
        o_ref[...] = _rms(x_ref[...], g_ref[...]).astype(BF16)

    return pl.pallas_call(
        body, name=name, grid=(t // tm,),
        in_specs=[pl.BlockSpec((tm, d), lambda i: (i, 0)), pl.BlockSpec((1, d), lambda i: (0, 0))] + [_UNREAD] * len(after),
        out_specs=pl.BlockSpec((tm, d), lambda i: (i, 0)),
        out_shape=jax.ShapeDtypeStruct((t, d), BF16),
        compiler_params=_cparams("parallel"),
    )(x, g.reshape(1, d), *after)


def _rms_bwd(x, g, dh, dres, name):
    t, d = x.shape
    tm = _row_tile(t)

    def body(x_ref, g_ref, dh_ref, dres_ref, dx_ref, dx16_ref, dg_ref):
        @pl.when(pl.program_id(0) == 0)
        def _():
            dg_ref[...] = jnp.zeros_like(dg_ref)

        _, vjp = jax.vjp(_rms, x_ref[...], g_ref[...])
        dx, dg = vjp(dh_ref[...].astype(F32))
        dx = dx + dres_ref[...]
        dx_ref[...] = dx
        dx16_ref[...] = dx.astype(BF16)
        dg_ref[...] += dg

    row = pl.BlockSpec((tm, d), lambda i: (i, 0))
    vec = pl.BlockSpec((1, d), lambda i: (0, 0))
    dx, dx16, dg = pl.pallas_call(
        body, name=name, grid=(t // tm,),
        in_specs=[row, vec, row, row], out_specs=[row, row, vec],
        out_shape=[jax.ShapeDtypeStruct((t, d), F32), jax.ShapeDtypeStruct((t, d), BF16), jax.ShapeDtypeStruct((1, d), F32)],
        compiler_params=_cparams("arbitrary"),
    )(x, g.reshape(1, d), dh, dres)
    return dx, dx16, dg.reshape(d)


def _loss_head(x, g, target):
    t, d = x.shape
    tm = _row_tile(t)

    def loss_fn(xv, gv, tv):
        err = _rms(xv, gv) - tv
        return 0.5 * jnp.sum(jnp.mean(err * err, axis=-1, keepdims=True))

    def body(x_ref, g_ref, t_ref, loss_ref, dx_ref, dx16_ref, dg_ref):
        @pl.when(pl.program_id(0) == 0)
        def _():
            dg_ref[...] = jnp.zeros_like(dg_ref)
            loss_ref[...] = jnp.zeros_like(loss_ref)

        tv = t_ref[...]
        val, vjp = jax.vjp(lambda xv, gv: loss_fn(xv, gv, tv), x_ref[...], g_ref[...])
        dx, dg = vjp(jnp.ones((), F32))
        dx_ref[...] = dx
        dx16_ref[...] = dx.astype(BF16)
        dg_ref[...] += dg
        loss_ref[...] += jnp.full(loss_ref.shape, val, F32)

    row = pl.BlockSpec((tm, d), lambda i: (i, 0))
    vec = pl.BlockSpec((1, d), lambda i: (0, 0))
    loss, dx, dx16, dg = pl.pallas_call(
        body, name="loss_head", grid=(t // tm,),
        in_specs=[row, vec, row], out_specs=[pl.BlockSpec((1, LANES), lambda i: (0, 0)), row, row, vec],
        out_shape=[jax.ShapeDtypeStruct((1, LANES), F32), jax.ShapeDtypeStruct((t, d), F32),
                   jax.ShapeDtypeStruct((t, d), BF16), jax.ShapeDtypeStruct((1, d), F32)],
        compiler_params=_cparams("arbitrary"),
    )(x, g.reshape(1, d), target)
    return loss[0, 0], dx, dx16, dg.reshape(d)


def _pre_glu(val, gate):
    return val * _sigmoid(gate)


def _pre_id(x):
    return x


def _post_lnsilu(c, g, b):
    return _silu(_ln64(c, g, b))


def _post_silu(c):
    return _silu(c)


def _conv_cfg(kind):
    if kind == "a":
        return dict(k=CONV_A_K, pre=_pre_glu, post=_post_lnsilu, n_in=2, n_par=2, nblk=A_WIDTH // LANES,
                    cols=(COL_AVAL, COL_AGATE))
    return dict(k=CONV_C_K, pre=_pre_id, post=_post_silu, n_in=1, n_par=0, nblk=D_CONV_C // LANES,
                cols=(COL_XBC,))


def _conv_fwd(kind, proj, w, bias, params, seq, name, out_dtype=F32, keep_conv=False):
    cfg = _conv_cfg(kind)
    kt, pre, post, n_in = cfg["k"], cfg["pre"], cfg["post"], cfg["n_in"]
    t = proj.shape[0]
    nseq = t // seq
    c = cfg["nblk"] * LANES
    rt = min(256, seq)
    nrt = seq // rt
    off0 = CONV_PAD - (kt - 1)

    def body(*refs):
        in_refs = refs[:n_in]
        w_ref, b_ref = refs[n_in], refs[n_in + 1]
        par_refs = refs[n_in + 2:n_in + 2 + cfg["n_par"]]
        out_refs = refs[n_in + 2 + cfg["n_par"]:-1]
        hpad = refs[-1]
        hpad[pl.ds(0, CONV_PAD), :] = jnp.zeros((CONV_PAD, LANES), F32)
        for r in range(nrt):
            hpad[pl.ds(CONV_PAD + r * rt, rt), :] = pre(*[x[pl.ds(r * rt, rt), :] for x in in_refs])
        pars = [p[...] for p in par_refs]
        for r in range(nrt):
            acc = jnp.broadcast_to(b_ref[...], (rt, LANES))
            for k in range(kt):
                acc = acc + w_ref[pl.ds(k, 1), :] * hpad[pl.ds(off0 + k + r * rt, rt), :]
            out_refs[0][pl.ds(r * rt, rt), :] = post(acc, *pars).astype(out_dtype)
            if keep_conv:
                out_refs[1][pl.ds(r * rt, rt), :] = acc

    in_specs = [pl.BlockSpec((seq, LANES), functools.partial(lambda s, j, col: (s, col + j), col=col))
                for col in cfg["cols"]]
    vec = pl.BlockSpec((1, LANES), lambda s, j: (0, j))
    in_specs += [pl.BlockSpec((CONV_PAD, LANES), lambda s, j: (0, j)), vec] + [vec] * cfg["n_par"]
    blk = pl.BlockSpec((seq, LANES), lambda s, j: (s, j))
    res = pl.pallas_call(
        body, name=name, grid=(nseq, cfg["nblk"]),
        in_specs=in_specs, out_specs=[blk, blk] if keep_conv else [blk],
        out_shape=[jax.ShapeDtypeStruct((t, c), out_dtype)] + ([jax.ShapeDtypeStruct((t, c), F32)] if keep_conv else []),
        scratch_shapes=[pltpu.VMEM((seq + CONV_PAD, LANES), F32)],
        compiler_params=_cparams("parallel", "parallel"),
    )(*([proj] * n_in), w, bias, *params)
    return tuple(res) if keep_conv else res[0]


def _conv_bwd(kind, proj, w, bias, params, dy, seq, name, dy_col=0, conv_out=None):
    kept = conv_out is not None
    cfg = _conv_cfg(kind)
    kt, pre, post, n_in, n_par = cfg["k"], cfg["pre"], cfg["post"], cfg["n_in"], cfg["n_par"]
    t = proj.shape[0]
    nseq = t // seq
    c = cfg["nblk"] * LANES
    rt = min(256, seq)
    nrt = seq // rt
    off0 = CONV_PAD - (kt - 1)

    def body(*refs):
        in_refs = refs[:n_in]
        w_ref, b_ref = refs[n_in], refs[n_in + 1]
        par_refs = refs[n_in + 2:n_in + 2 + n_par]
        pos = n_in + 2 + n_par
        dy_ref = refs[pos]
        if kept:
            pos += 1
            conv_ref = refs[pos]
        din_refs = refs[pos + 1:pos + 1 + n_in]
        dw_ref, db_ref = refs[pos + 1 + n_in], refs[pos + 2 + n_in]
        dpar_refs = refs[pos + 3 + n_in:pos + 3 + n_in + n_par]
        hpad, dcpad = refs[pos + 3 + n_in + n_par:]

        @pl.when(pl.program_id(1) == 0)
        def _():
            dw_ref[...] = jnp.zeros_like(dw_ref)
            db_ref[...] = jnp.zeros_like(db_ref)
            for r in dpar_refs:
                r[...] = jnp.zeros_like(r)

        hpad[pl.ds(0, CONV_PAD), :] = jnp.zeros((CONV_PAD, LANES), F32)
        dcpad[pl.ds(seq, CONV_PAD), :] = jnp.zeros((CONV_PAD, LANES), F32)
        for r in range(nrt):
            hpad[pl.ds(CONV_PAD + r * rt, rt), :] = pre(*[x[pl.ds(r * rt, rt), :] for x in in_refs])
        pars = [p[...] for p in par_refs]
        for r in range(nrt):
            if kept:
                acc = conv_ref[pl.ds(r * rt, rt), :]
            else:
                acc = jnp.broadcast_to(b_ref[...], (rt, LANES))
                for k in range(kt):
                    acc = acc + w_ref[pl.ds(k, 1), :] * hpad[pl.ds(off0 + k + r * rt, rt), :]
            _, vjp = jax.vjp(post, acc, *pars)
            grads = vjp(dy_ref[pl.ds(r * rt, rt), :])
            dcpad[pl.ds(r * rt, rt), :] = grads[0]
            db_ref[...] += jnp.sum(grads[0], axis=0, keepdims=True)
            for ref, gpar in zip(dpar_refs, grads[1:]):
                ref[...] += gpar
        for r in range(nrt):
            dh = jnp.zeros((rt, LANES), F32)
            for k in range(kt):
                dh = dh + w_ref[pl.ds(k, 1), :] * dcpad[pl.ds(r * rt + kt - 1 - k, rt), :]
            _, vjp = jax.vjp(pre, *[x[pl.ds(r * rt, rt), :] for x in in_refs])
            for ref, gin in zip(din_refs, vjp(dh)):
                ref[pl.ds(r * rt, rt), :] = gin.astype(ref.dtype)
        for k in range(kt):
            s = jnp.zeros((1, LANES), F32)
            for r in range(nrt):
                s = s + jnp.sum(dcpad[pl.ds(r * rt, rt), :] * hpad[pl.ds(off0 + k + r * rt, rt), :],
                                axis=0, keepdims=True)
            dw_ref[pl.ds(k, 1), :] += s

    in_specs = [pl.BlockSpec((seq, LANES), functools.partial(lambda j, s, col: (s, col + j), col=col))
                for col in cfg["cols"]]
    vec = pl.BlockSpec((1, LANES), lambda j, s: (0, j))
    wspec = pl.BlockSpec((CONV_PAD, LANES), lambda j, s: (0, j))
    blk = pl.BlockSpec((seq, LANES), lambda j, s: (s, j))
    in_specs += [wspec, vec] + [vec] * n_par + [pl.BlockSpec((seq, LANES), lambda j, s: (s, dy_col + j))]
    in_specs += [blk] if kept else []
    out_specs = [blk] * n_in + [wspec, vec] + [vec] * n_par
    out_shape = ([jax.ShapeDtypeStruct((t, c), BF16)] * n_in
                 + [jax.ShapeDtypeStruct((CONV_PAD, c), F32), jax.ShapeDtypeStruct((1, c), F32)]
                 + [jax.ShapeDtypeStruct((1, c), F32)] * n_par)
    res = pl.pallas_call(
        body, name=name, grid=(cfg["nblk"], nseq),
        in_specs=in_specs, out_specs=out_specs, out_shape=out_shape,
        scratch_shapes=[pltpu.VMEM((seq + CONV_PAD, LANES), F32), pltpu.VMEM((seq + CONV_PAD, LANES), F32)],
        compiler_params=_cparams("parallel", "arbitrary"),
    )(*([proj] * n_in), w, bias, *params, dy, *([conv_out] if kept else []))
    return res[:n_in], res[n_in], res[n_in + 1], res[n_in + 2:]


def _gmlp_chunk(bu, bv, g, b, w0, w1, b0row, b1row):
    u = _gelu(bu)
    vn = _ln64(_gelu(bv), g, b)
    tri = _iota2((CHUNK, CHUNK), 0) >= _iota2((CHUNK, CHUNK), 1)
    m0 = _nn(jnp.where(tri, w0, 0.0), vn) + jnp.broadcast_to(b0row, (CHUNK, CHUNK)).T
    m1 = _nn(jnp.where(tri, w1, 0.0), vn) + jnp.broadcast_to(b1row, (CHUNK, CHUNK)).T
    return u * jnp.where(_lane_lt64((CHUNK, LANES)), m0, m1)


def _gmlp_specs(tm, order):
    def im(f):
        return lambda *ids: f(*order(*ids))
    return dict(
        bu=pl.BlockSpec((tm, LANES), im(lambda j, r: (r, COL_BU + j))),
        bv=pl.BlockSpec((tm, LANES), im(lambda j, r: (r, COL_BV + j))),
        vec=pl.BlockSpec((1, LANES), im(lambda j, r: (0, j))),
        ws=pl.BlockSpec((2, CHUNK, CHUNK), im(lambda j, r: (j, 0, 0))),
        bs=pl.BlockSpec((None, 2, CHUNK), im(lambda j, r: (j, 0, 0))),
        blk=pl.BlockSpec((tm, LANES), im(lambda j, r: (r, j))),
    )


def _gmlp_fwd(proj, ln_g, ln_b, w_s, b_s, name):
    t = proj.shape[0]
    tm = _row_tile(t)
    nch = tm // CHUNK
    sp = _gmlp_specs(tm, lambda r, j: (j, r))

    def body(bu_ref, bv_ref, g_ref, b_ref, ws_ref, bs_ref, o_ref):
        for ci in range(nch):
            rows = pl.ds(ci * CHUNK, CHUNK)
            o_ref[rows, :] = _gmlp_chunk(bu_ref[rows, :], bv_ref[rows, :], g_ref[...], b_ref[...], ws_ref[0], ws_ref[1],
                                         bs_ref[pl.ds(0, 1), :], bs_ref[pl.ds(1, 1), :]).astype(BF16)

    return pl.pallas_call(
        body, name=name, grid=(t // tm, B_WIDTH // LANES),
        in_specs=[sp["bu"], sp["bv"], sp["vec"], sp["vec"], sp["ws"], sp["bs"]],
        out_specs=sp["blk"], out_shape=jax.ShapeDtypeStruct((t, B_WIDTH), BF16),
        compiler_params=_cparams("parallel", "parallel"),
    )(proj, proj, ln_g, ln_b, w_s, b_s.reshape(B_WIDTH // LANES, 2, CHUNK))


def _gmlp_bwd(proj, ln_g, ln_b, w_s, b_s, dy, name, dy_col=0):
    t = proj.shape[0]
    tm = _row_tile(t)
    nch = tm // CHUNK
    sp = _gmlp_specs(tm, lambda j, r: (j, r))
    dy_spec = pl.BlockSpec((tm, LANES), lambda j, r: (r, dy_col + j))

    def body(bu_ref, bv_ref, g_ref, b_ref, ws_ref, bs_ref, dy_ref, dbu_ref, dbv_ref, dg_ref, db_ref, dws_ref, dbs_ref):
        @pl.when(pl.program_id(1) == 0)
        def _():
            for r in (dg_ref, db_ref, dws_ref, dbs_ref):
                r[...] = jnp.zeros_like(r)

        for ci in range(nch):
            rows = pl.ds(ci * CHUNK, CHUNK)
            _, vjp = jax.vjp(_gmlp_chunk, bu_ref[rows, :], bv_ref[rows, :], g_ref[...], b_ref[...],
                             ws_ref[0], ws_ref[1], bs_ref[pl.ds(0, 1), :], bs_ref[pl.ds(1, 1), :])
            dbu, dbv, dg, db, dw0, dw1, db0, db1 = vjp(dy_ref[rows, :])
            dbu_ref[rows, :] = dbu.astype(BF16)
            dbv_ref[rows, :] = dbv.astype(BF16)
            dg_ref[...] += dg
            db_ref[...] += db
            dws_ref[0] += dw0
            dws_ref[1] += dw1
            dbs_ref[pl.ds(0, 1), :] += db0
            dbs_ref[pl.ds(1, 1), :] += db1

    nh = B_WIDTH // LANES
    res = pl.pallas_call(
        body, name=name, grid=(nh, t // tm),
        in_specs=[sp["bu"], sp["bv"], sp["vec"], sp["vec"], sp["ws"], sp["bs"], dy_spec],
        out_specs=[sp["blk"], sp["blk"], sp["vec"], sp["vec"], sp["ws"], sp["bs"]],
        out_shape=[jax.ShapeDtypeStruct((t, B_WIDTH), BF16), jax.ShapeDtypeStruct((t, B_WIDTH), BF16),
                   jax.ShapeDtypeStruct((1, B_WIDTH), F32), jax.ShapeDtypeStruct((1, B_WIDTH), F32),
                   jax.ShapeDtypeStruct(w_s.shape, F32), jax.ShapeDtypeStruct((nh, 2, CHUNK), F32)],
        compiler_params=_cparams("parallel", "arbitrary"),
    )(proj, proj, ln_g, ln_b, w_s, b_s.reshape(nh, 2, CHUNK), dy)
    dbu, dbv, dg, db, dws, dbs = res
    return dbu, dbv, dg, db, dws, dbs.reshape(b_s.shape)


def _tri_apply(a, lower):
    l = a.shape[0]
    r, c = _iota2((l, l), 0), _iota2((l, l), 1)
    t = jnp.where((r >= c) if lower else (r <= c), 1.0, 0.0).astype(BF16)
    hi = a.astype(BF16)
    r1 = a - hi.astype(F32)
    mid = r1.astype(BF16)
    lo = (r1 - mid.astype(F32)).astype(BF16)
    dn = _DN["nn"]
    return (lax.dot_general(t, hi, dn, preferred_element_type=F32) + lax.dot_general(t, mid, dn, preferred_element_type=F32)
            + lax.dot_general(t, lo, dn, preferred_element_type=F32))


@jax.custom_vjp
def _cumsum_rows(a):
    return _tri_apply(a, True)


_cumsum_rows.defvjp(lambda a: (_tri_apply(a, True), None), lambda _, g: (_tri_apply(g, False),))

SSD_GROUP_HEADS = 8
SSD_GROUP_PAIRS = 4


def _ssd_group(x0, x1, x2, x3, dt_raw, bias, alog, bm, cm, p0, p1, p2, p3):
    xs, prevs = (x0, x1, x2, x3), (p0, p1, p2, p3)
    dt = _softplus(dt_raw + bias)
    a = dt * (-jnp.exp(alog))
    acs = _cumsum_rows(a)
    alast = jnp.sum(a, axis=0, keepdims=True)
    dt_t, acs_t = dt.T, acs.T
    cb = _nt(cm, bm)
    tri = _iota2((CHUNK, CHUNK), 0) >= _iota2((CHUNK, CHUNK), 1)
    lane = _iota2((CHUNK, LANES), 1)
    sub = _iota2((LANES, CHUNK), 0)
    lane1 = _iota2((1, LANES), 1)

    def column(v, i):
        return jnp.broadcast_to(jnp.sum(jnp.where(lane == i, v, 0.0), axis=1, keepdims=True), (CHUNK, LANES))

    def row(vt, i):
        return jnp.broadcast_to(jnp.sum(jnp.where(sub == i, vt, 0.0), axis=0, keepdims=True), (CHUNK, CHUNK))

    heads = []
    for i in range(SSD_GROUP_HEADS):
        col_a = column(acs, i)
        al = jnp.sum(jnp.where(lane1 == i, alast, 0.0), axis=1, keepdims=True)
        m = cb * jnp.exp(jnp.where(tri, col_a - row(acs_t, i), NEG)) * row(dt_t, i)
        heads.append((m, jnp.exp(col_a), column(dt, i) * jnp.exp(al - col_a), jnp.exp(al)))
    lo_lanes = _lane_lt64((CHUNK, LANES))
    lo_rows = _iota2((LANES, SSM_STATE), 0) < HEAD_DIM
    ys, news = [], []
    for j in range(SSD_GROUP_PAIRS):
        (m0, ea0, w0, cd0), (m1, ea1, w1, cd1) = heads[2 * j], heads[2 * j + 1]
        x, prev = xs[j], prevs[j]
        ydiag = jnp.where(lo_lanes, _nn(m0, x), _nn(m1, x))
        yoff = jnp.where(lo_lanes, _nt(cm * ea0, prev), _nt(cm * ea1, prev))
        states = jnp.where(lo_rows, _tn(x, bm * w0), _tn(x, bm * w1))
        ys.append(ydiag + yoff)
        news.append(prev * jnp.where(lo_rows, cd0, cd1) + states)
    return tuple(ys) + tuple(news)


SSD_GROUPS = 2


def _ssd2_specs(seq, rev):
    ncs = seq // CHUNK
    wide = SSD_GROUPS * LANES

    def row(s, c):
        return s * ncs + (ncs - 1 - c if rev else c)

    return dict(
        x=pl.BlockSpec((CHUNK, C_WIDTH), lambda s, c: (row(s, c), 0)),
        dt=pl.BlockSpec((CHUNK, wide), lambda s, c: (row(s, c), 0)),
        vec=pl.BlockSpec((1, wide), lambda s, c: (0, 0)),
        bm=pl.BlockSpec((CHUNK, wide), lambda s, c: (row(s, c), C_WIDTH // wide)),
        cm=pl.BlockSpec((CHUNK, wide), lambda s, c: (row(s, c), C_WIDTH // wide + 1)),
        st=pl.BlockSpec((None, C_WIDTH // LANES, LANES, SSM_STATE), lambda s, c: (row(s, c), 0, 0, 0)),
        ncs=ncs,
    )


def _lane_blocks(ref, grp):
    return [ref[:, pl.ds((grp * SSD_GROUP_PAIRS + j) * LANES, LANES)] for j in range(SSD_GROUP_PAIRS)]


def _group_block(ref, grp):
    return ref[:, pl.ds(grp * LANES, LANES)]


def _ssd2_fwd(xbc_act, dt_raw, dt_bias, a_log, seq, name):
    t = xbc_act.shape[0]
    sp = _ssd2_specs(seq, False)

    npair = SSD_GROUP_PAIRS

    def body(x_ref, dt_ref, bias_ref, alog_ref, bm_ref, cm_ref, y_ref, prev_ref, state):
        @pl.when(pl.program_id(1) == 0)
        def _():
            state[...] = jnp.zeros_like(state)

        for grp in range(SSD_GROUPS):
            prevs = [state[grp * npair + j] for j in range(npair)]
            for j in range(npair):
                prev_ref[grp * npair + j] = prevs[j]
            res = _ssd_group(*_lane_blocks(x_ref, grp), _group_block(dt_ref, grp), _group_block(bias_ref, grp),
                             _group_block(alog_ref, grp), _group_block(bm_ref, grp), _group_block(cm_ref, grp), *prevs)
            for j in range(npair):
                y_ref[:, pl.ds((grp * npair + j) * LANES, LANES)] = res[j]
                state[grp * npair + j] = res[npair + j]

    return pl.pallas_call(
        body, name=name, grid=(t // seq, sp["ncs"]),
        in_specs=[sp["x"], sp["dt"], sp["vec"], sp["vec"], sp["bm"], sp["cm"]],
        out_specs=[sp["x"], sp["st"]],
        out_shape=[jax.ShapeDtypeStruct((t, C_WIDTH), F32),
                   jax.ShapeDtypeStruct((t // CHUNK, C_WIDTH // LANES, LANES, SSM_STATE), F32)],
        scratch_shapes=[pltpu.VMEM((C_WIDTH // LANES, LANES, SSM_STATE), F32)],
        compiler_params=_cparams("parallel", "arbitrary"),
    )(xbc_act, dt_raw, dt_bias, a_log, xbc_act, xbc_act)


def _ssd2_bwd(xbc_act, dt_raw, dt_bias, a_log, prev_saved, dy, seq, name):
    t = xbc_act.shape[0]
    sp = _ssd2_specs(seq, True)
    npair = SSD_GROUP_PAIRS

    def body(x_ref, dt_ref, bias_ref, alog_ref, bm_ref, cm_ref, prev_ref, dy_ref,
             dx_ref, ddt_ref, dbias_ref, dalog_ref, dbm_ref, dcm_ref, dstate):
        @pl.when(pl.program_id(1) == 0)
        def _():
            dstate[...] = jnp.zeros_like(dstate)

        @pl.when(jnp.logical_and(pl.program_id(0) == 0, pl.program_id(1) == 0))
        def _():
            dbias_ref[...] = jnp.zeros_like(dbias_ref)
            dalog_ref[...] = jnp.zeros_like(dalog_ref)

        for grp in range(SSD_GROUPS):
            lanes = pl.ds(grp * LANES, LANES)
            _, vjp = jax.vjp(_ssd_group, *_lane_blocks(x_ref, grp), _group_block(dt_ref, grp), _group_block(bias_ref, grp),
                             _group_block(alog_ref, grp), _group_block(bm_ref, grp), _group_block(cm_ref, grp),
                             *[prev_ref[grp * npair + j] for j in range(npair)])
            grads = vjp(tuple(_lane_blocks(dy_ref, grp)) + tuple(dstate[grp * npair + j] for j in range(npair)))
            for j in range(npair):
                dx_ref[:, pl.ds((grp * npair + j) * LANES, LANES)] = grads[j]
                dstate[grp * npair + j] = grads[npair + 5 + j]
            ddt_ref[:, lanes] = grads[npair].astype(BF16)
            dbias_ref[:, lanes] += grads[npair + 1]
            dalog_ref[:, lanes] += grads[npair + 2]
            dbm_ref[:, lanes] = grads[npair + 3]
            dcm_ref[:, lanes] = grads[npair + 4]

    return pl.pallas_call(
        body, name=name, grid=(t // seq, sp["ncs"]),
        in_specs=[sp["x"], sp["dt"], sp["vec"], sp["vec"], sp["bm"], sp["cm"], sp["st"], sp["x"]],
        out_specs=[sp["x"], sp["dt"], sp["vec"], sp["vec"], sp["dt"], sp["dt"]],
        out_shape=[jax.ShapeDtypeStruct((t, C_WIDTH), F32), jax.ShapeDtypeStruct((t, 2 * LANES), BF16),
                   jax.ShapeDtypeStruct((1, 2 * LANES), F32), jax.ShapeDtypeStruct((1, 2 * LANES), F32),
                   jax.ShapeDtypeStruct((t, 2 * SSM_STATE), F32), jax.ShapeDtypeStruct((t, 2 * SSM_STATE), F32)],
        scratch_shapes=[pltpu.VMEM((C_WIDTH // LANES, LANES, SSM_STATE), F32)],
        compiler_params=_cparams("arbitrary", "arbitrary"),
    )(xbc_act, dt_raw, dt_bias, a_log, xbc_act, xbc_act, prev_saved, dy)


def _ssd2_assemble(dxs_ssd, dxs_skip, dbm, dcm, name):
    t = dxs_ssd.shape[0]
    tm = _row_tile(t)

    def body(a_ref, b_ref, dbm_ref, dcm_ref, o_ref):
        o_ref[:, pl.ds(0, C_WIDTH)] = a_ref[...] + b_ref[...]
        o_ref[:, pl.ds(C_WIDTH, 2 * SSM_STATE)] = dbm_ref[...]
        o_ref[:, pl.ds(C_WIDTH + 2 * SSM_STATE, 2 * SSM_STATE)] = dcm_ref[...]

    wide = pl.BlockSpec((tm, C_WIDTH), lambda i: (i, 0))
    narrow = pl.BlockSpec((tm, 2 * SSM_STATE), lambda i: (i, 0))
    return pl.pallas_call(
        body, name=name, grid=(t // tm,), in_specs=[wide, wide, narrow, narrow],
        out_specs=pl.BlockSpec((tm, D_CONV_C), lambda i: (i, 0)),
        out_shape=jax.ShapeDtypeStruct((t, D_CONV_C), F32),
        compiler_params=_cparams("parallel"),
    )(dxs_ssd, dxs_skip, dbm, dcm)


def _ssd_post_fn(y, xs, z, dskip, g):
    v = (y + dskip * xs) * _silu(z)
    return v * lax.rsqrt(jnp.mean(v * v, axis=-1, keepdims=True) + EPS) * g


def _ssd_post_specs(tm, order):
    gw = C_WIDTH // 2

    def im(f):
        return lambda *ids: f(*order(*ids))
    return dict(
        blk=pl.BlockSpec((tm, gw), im(lambda g, r: (r, g))),
        z=pl.BlockSpec((tm, gw), im(lambda g, r: (r, COL_Z * LANES // gw + g))),
        vec=pl.BlockSpec((1, gw), im(lambda g, r: (0, g))),
    )


def _ssd_post_fwd(y_ssd, xbc_act, proj, dskip64, norm_g, name):
    t = y_ssd.shape[0]
    tm = _row_tile(t)
    sp = _ssd_post_specs(tm, lambda r, g: (g, r))

    def body(y_ref, xs_ref, z_ref, ds_ref, g_ref, o_ref):
        o_ref[...] = _ssd_post_fn(y_ref[...], xs_ref[...], z_ref[...], ds_ref[...], g_ref[...]).astype(BF16)

    return pl.pallas_call(
        body, name=name, grid=(t // tm, 2),
        in_specs=[sp["blk"], sp["blk"], sp["z"], sp["vec"], sp["vec"]], out_specs=sp["blk"],
        out_shape=jax.ShapeDtypeStruct((t, C_WIDTH), BF16),
        compiler_params=_cparams("parallel", "parallel"),
    )(y_ssd, xbc_act, proj, dskip64, norm_g)


def _ssd_post_bwd(y_ssd, xbc_act, proj, dskip64, norm_g, dyc, name, dy_col=0):
    t = y_ssd.shape[0]
    tm = _row_tile(t)
    sp = _ssd_post_specs(tm, lambda g, r: (g, r))
    dy_spec = pl.BlockSpec((tm, C_WIDTH // 2), lambda g, r: (r, dy_col + g))

    def body(y_ref, xs_ref, z_ref, ds_ref, g_ref, dyc_ref, dy_ref, dxs_ref, dz_ref, dds_ref, dg_ref):
        @pl.when(pl.program_id(1) == 0)
        def _():
            dds_ref[...] = jnp.zeros_like(dds_ref)
            dg_ref[...] = jnp.zeros_like(dg_ref)

        _, vjp = jax.vjp(_ssd_post_fn, y_ref[...], xs_ref[...], z_ref[...], ds_ref[...], g_ref[...])
        dy, dxs, dz, dds, dg = vjp(dyc_ref[...])
        dy_ref[...] = dy
        dxs_ref[...] = dxs
        dz_ref[...] = dz.astype(BF16)
        dds_ref[...] += dds
        dg_ref[...] += dg

    wide = jax.ShapeDtypeStruct((t, C_WIDTH), F32)
    vec = jax.ShapeDtypeStruct((1, C_WIDTH), F32)
    return pl.pallas_call(
        body, name=name, grid=(2, t // tm),
        in_specs=[sp["blk"], sp["blk"], sp["z"], sp["vec"], sp["vec"], dy_spec],
        out_specs=[sp["blk"], sp["blk"], sp["blk"], sp["vec"], sp["vec"]],
        out_shape=[wide, wide, jax.ShapeDtypeStruct((t, C_WIDTH), BF16), vec, vec],
        compiler_params=_cparams("parallel", "arbitrary"),
    )(y_ssd, xbc_act, proj, dskip64, norm_g, dyc)


def _pad_taps(w):
    return jnp.pad(w, ((0, CONV_PAD - w.shape[0]), (0, 0)))


def _group_heads(a):
    pad = [(0, 0)] * (a.ndim - 1) + [(0, LANES - SSD_GROUP_HEADS)]
    return jnp.concatenate([jnp.pad(a[..., :SSD_GROUP_HEADS], pad), jnp.pad(a[..., SSD_GROUP_HEADS:], pad)], axis=-1)


def _ungroup_heads(a):
    return jnp.concatenate([a[..., :SSD_GROUP_HEADS], a[..., LANES:LANES + SSD_GROUP_HEADS]], axis=-1)


def _layer_fwd(x, p, seq, li, after=()):
    n = f"l{li}_"
    h1 = _rms_fwd(x, p["norm1_g"], n + "rms1", after=after)
    proj = _matmul(h1, p["w_main"], mode="nn", name=n + "inproj")
    dt_raw = _matmul(h1, p["w_dt"], mode="nn", name=n + "inproj_dt")
    row = lambda v: v.reshape(1, -1)
    ya, conv_a = _conv_fwd("a", proj, _pad_taps(p["conv_a_w"]), row(p["conv_a_b"]), (row(p["ln_a_g"]), row(p["ln_a_b"])),
                           seq, n + "conva", out_dtype=BF16, keep_conv=True)
    yb = _gmlp_fwd(proj, row(p["ln_b_g"]), row(p["ln_b_b"]), p["w_spatial"], p["b_spatial"], n + "gmlp")
    xbc_act = _conv_fwd("c", proj, _pad_taps(p["conv_c_w"]), row(p["conv_c_b"]), (), seq, n + "convc")
    y_ssd, prev = _ssd2_fwd(xbc_act, dt_raw, _group_heads(row(p["dt_bias"])), _group_heads(row(p["a_log"])), seq, n + "ssd")
    dskip64 = jnp.repeat(p["d_skip"], HEAD_DIM).reshape(1, C_WIDTH)
    yc = _ssd_post_fwd(y_ssd, xbc_act, proj, dskip64, row(p["norm_c_g"]), n + "ssdpost")
    ycat = jnp.concatenate([ya, yb, yc], axis=1)
    x1 = _matmul(ycat, p["w_out"], mode="nn", name=n + "outproj", add=x)
    h2 = _rms_fwd(x1, p["norm2_g"], n + "rms2")
    u, act = _matmul(h2, p["w_ff1"], mode="nn", name=n + "ff1", epilogue=_relu2_epilogue, out_dtypes=(F32, BF16),
                     b_chips=True)
    x2 = _matmul(act, p["w_ff2"], mode="nn", name=n + "ff2", add=x1)
    saved = dict(x=x, h1=h1, proj=proj, conv_a=conv_a, dt_raw=dt_raw, xbc_act=xbc_act, prev=prev, y_ssd=y_ssd,
                 dskip64=dskip64, ycat=ycat, x1=x1, h2=h2, u=u, act=act)
    return x2, saved


def _layer_bwd(dx2, dx2_16, p, s, seq, li, after=(), on_ffn_grads=None):
    n = f"l{li}_b_"
    row = lambda v: v.reshape(1, -1)
    g = {}
    du = _matmul(dx2_16, p["w_ff2"], mode="nt", name=n + "ff2_dx", epilogue=_relu2_bwd_epilogue, extra=s["u"],
                 out_dtypes=(BF16,), after=after)
    g["w_ff2"] = _matmul(s["act"], dx2_16, mode="tn", name=n + "ff2_dw")
    g["w_ff1"] = _matmul(s["h2"], du, mode="tn", name=n + "ff1_dw", out_chips=True)
    dh2 = _matmul(du, p["w_ff1"], mode="nt", name=n + "ff1_dx", b_chips=True)
    dx1, dx1_16, g["norm2_g"] = _rms_bwd(s["x1"], p["norm2_g"], dh2, dx2, n + "rms2")
    g["w_out"] = _matmul(s["ycat"], dx1_16, mode="tn", name=n + "out_dw")
    dycat = _matmul(dx1_16, p["w_out"], mode="nt", name=n + "out_dx",
                    after=() if on_ffn_grads is None else on_ffn_grads(g))
    proj = s["proj"]
    (dval, dgate), dwa, dba, (dlag, dlab) = _conv_bwd(
        "a", proj, _pad_taps(p["conv_a_w"]), row(p["conv_a_b"]), (row(p["ln_a_g"]), row(p["ln_a_b"])), dycat, seq,
        n + "conva", dy_col=0, conv_out=s["conv_a"])
    g["conv_a_w"], g["conv_a_b"], g["ln_a_g"], g["ln_a_b"] = dwa[:CONV_A_K], dba[0], dlag[0], dlab[0]
    dbu, dbv, dlbg, dlbb, g["w_spatial"], g["b_spatial"] = _gmlp_bwd(
        proj, row(p["ln_b_g"]), row(p["ln_b_b"]), p["w_spatial"], p["b_spatial"], dycat, n + "gmlp",
        dy_col=A_WIDTH // LANES)
    g["ln_b_g"], g["ln_b_b"] = dlbg[0], dlbb[0]
    dy_ssd, dxs_skip, dz, dds, dncg = _ssd_post_bwd(s["y_ssd"], s["xbc_act"], proj, s["dskip64"], row(p["norm_c_g"]),
                                                    dycat, n + "ssdpost", dy_col=(A_WIDTH + B_WIDTH) * 2 // C_WIDTH)
    g["norm_c_g"] = dncg[0]
    g["d_skip"] = dds.reshape(C_HEADS, HEAD_DIM).sum(axis=1)
    dxs, ddt_raw, ddtb, dalog, dbm, dcm = _ssd2_bwd(
        s["xbc_act"], s["dt_raw"], _group_heads(row(p["dt_bias"])), _group_heads(row(p["a_log"])), s["prev"], dy_ssd, seq,
        n + "ssd")
    g["dt_bias"], g["a_log"] = _ungroup_heads(ddtb)[0], _ungroup_heads(dalog)[0]
    dconv = _ssd2_assemble(dxs, dxs_skip, dbm, dcm, n + "ssdasm")
    (dxbc,), dwc, dbcv, _ = _conv_bwd("c", proj, _pad_taps(p["conv_c_w"]), row(p["conv_c_b"]), (), dconv, seq, n + "convc")
    g["conv_c_w"], g["conv_c_b"] = dwc[:CONV_C_K], dbcv[0]
    dproj = jnp.concatenate([dval, dgate, dbu, dbv, dz, dxbc], axis=1)
    g["w_main"] = _matmul(s["h1"], dproj, mode="tn", name=n + "in_dw")
    g["w_dt"] = _matmul(s["h1"], ddt_raw, mode="tn", name=n + "indt_dw")
    dh1 = _matmul(dproj, p["w_main"], mode="nt", name=n + "in_dx")
    dh1 = _matmul(ddt_raw, p["w_dt"], mode="nt", name=n + "indt_dx", add=dh1)
    dx, dx_16, g["norm1_g"] = _rms_bwd(s["x"], p["norm1_g"], dh1, dx1, n + "rms1")
    return dx, dx_16, g


EW_BLOCK_BYTES = 2 << 20


def _ew(fn, ins, out_dtypes, name, leads=None):
    leads = leads or [None] * len(ins)
    rows, c = ins[0].shape[-2:]
    tr = _pick(rows, [t for t in (2048, 1024, 512, 256, 128, 64, 32, 16, 8) if t * c * 4 <= EW_BLOCK_BYTES])
    n_in = len(ins)

    def spec(lead):
        if lead is None:
            return pl.BlockSpec((tr, c), lambda i: (i, 0))
        return pl.BlockSpec((None, tr, c), functools.partial(lambda i, k: (k, i, 0), k=lead))

    def body(*refs):
        outs = fn(*[r[...].astype(F32) for r in refs[:n_in]])
        for o_ref, o in zip(refs[n_in:], outs):
            o_ref[...] = o.astype(o_ref.dtype)

    return pl.pallas_call(
        body, name=name, grid=(rows // tr,),
        in_specs=[spec(l) for l in leads], out_specs=[spec(None)] * len(out_dtypes),
        out_shape=[jax.ShapeDtypeStruct((rows, c), dt) for dt in out_dtypes],
        compiler_params=_cparams("parallel"),
    )(*ins)


def _adam_fn(w, g, m, v):
    m2 = ADAM_B1 * m + (1.0 - ADAM_B1) * g
    v2 = ADAM_B2 * v + (1.0 - ADAM_B2) * (g * g)
    m_hat = m2 / (1.0 - ADAM_B1 ** ADAM_STEP)
    v_hat = v2 / (1.0 - ADAM_B2 ** ADAM_STEP)
    delta = -ADAM_LR * (m_hat / (jnp.sqrt(v_hat) + ADAM_EPS) + ADAM_WD * w)
    return delta, m2, v2


def _adam(w, g, m, v, name):
    shape = w.shape
    two_d = lambda a: a.reshape(-1, shape[-1])
    outs = _ew(_adam_fn, [two_d(w), two_d(g), two_d(m), two_d(v)], (F32, F32, F32), name)
    return [o.reshape(shape) for o in outs]


_ANY = pl.BlockSpec(memory_space=pl.ANY)


def _mesh_pos():
    return lax.axis_index("x"), lax.axis_index("y"), lax.axis_index("c")


def _peer_chips(x, y):
    return [(1 - x, y), (x, 1 - y), (1 - x, 1 - y)]


def _remote(src, dst, send_sems, recv_sems, sem, to):
    return pltpu.make_async_remote_copy(src_ref=src, dst_ref=dst, send_sem=send_sems.at[sem],
                                        recv_sem=recv_sems.at[sem], device_id=to, device_id_type=MESH)


def _half_rows(n_rows, which):
    half = n_rows // 2
    return pl.ds(pl.multiple_of(which * half, 8), half)


def _comm_call(body, ins, out_shapes, n_sems, name):
    scratch = [pltpu.SemaphoreType.DMA((n_sems,)), pltpu.SemaphoreType.DMA((n_sems,))]
    return pl.pallas_call(
        body, name=name, in_specs=[_ANY] * len(ins), out_specs=[_ANY] * len(out_shapes),
        out_shape=out_shapes, scratch_shapes=scratch,
    )(*ins)


def _gather_weights(big, small, name):
    nb, ns = len(big), len(small)
    n = nb + ns

    def body(*refs):
        ins, outs = refs[:n], refs[n:2 * n]
        send_sems, recv_sems = refs[2 * n:]
        x, y, c = _mesh_pos()
        q = 2 * x + y
        me, sib = (x, y, c), (x, y, 1 - c)
        chips = _peer_chips(x, y)
        rem = functools.partial(_remote, send_sems=send_sems, recv_sems=recv_sems)
        first = []
        for i in range(nb):
            mine = _half_rows(big[i].shape[0], c)
            for k, (px, py) in enumerate(chips):
                first.append(rem(ins[i].at[mine], outs[i].at[q, mine], sem=6 * i + k, to=(px, py, c)))
        for j in range(ns):
            for k, (px, py) in enumerate(chips):
                first.append(rem(ins[nb + j], outs[nb + j].at[q], sem=6 * nb + 3 * j + k, to=(px, py, c)))
        for cp in first:
            cp.start()
        passed = []
        for i in range(nb):
            mine = _half_rows(big[i].shape[0], c)
            for k, (px, py) in enumerate(chips):
                landed = outs[i].at[2 * px + py, mine]
                rem(landed, landed, sem=6 * i + k, to=me).wait_recv()
                fwd = rem(landed, landed, sem=6 * i + 3 + k, to=sib)
                fwd.start()
                passed.append(fwd)
        for i in range(nb):
            other = _half_rows(big[i].shape[0], 1 - c)
            for k, (px, py) in enumerate(chips):
                theirs = outs[i].at[2 * px + py, other]
                rem(theirs, theirs, sem=6 * i + 3 + k, to=me).wait_recv()
        for j in range(ns):
            for k, (px, py) in enumerate(chips):
                dst = outs[nb + j].at[2 * px + py]
                rem(dst, dst, sem=6 * nb + 3 * j + k, to=me).wait_recv()
        for cp in first + passed:
            cp.wait_send()

    out_shapes = [jax.ShapeDtypeStruct((N_CHIPS,) + a.shape, a.dtype) for a in list(big) + list(small)]
    return _comm_call(body, list(big) + list(small), out_shapes, 6 * nb + 3 * ns, name)


def _sibling_other_halves(gs, name):
    n = len(gs)

    def other_half(ref, shape, c):
        rows = _half_rows(shape[-2], 1 - c)
        return ref.at[rows] if len(shape) == 2 else ref.at[:, rows]

    def body(*refs):
        ins, outs = refs[:n], refs[n:2 * n]
        send_sems, recv_sems = refs[2 * n:]
        x, y, c = _mesh_pos()
        copies = [_remote(other_half(ins[i], gs[i].shape, c), outs[i], send_sems, recv_sems, i, (x, y, 1 - c))
                  for i in range(n)]
        for cp in copies:
            cp.start()
        for cp in copies:
            cp.wait()

    out_shapes = [jax.ShapeDtypeStruct(g.shape[:-2] + (g.shape[-2] // 2, g.shape[-1]), g.dtype) for g in gs]
    return _comm_call(body, list(gs), out_shapes, n, name)


IN_SHARD = D_IN_PROJ // N_CHIPS


def _chipsum_in(mine, mine_dt, theirs, theirs_dt, name):
    r = mine.shape[0]
    tr = _pick(r, (128, 64, 32, 16, 8))
    last = D_MAIN - (N_CHIPS - 1) * IN_SHARD

    def body(a_ref, adt_ref, b_ref, bdt_ref, o32_ref, o16_ref):
        for p in range(N_CHIPS):
            wid = IN_SHARD if p < N_CHIPS - 1 else last
            s = a_ref[:, pl.ds(IN_SHARD * p, wid)] + b_ref[:, pl.ds(IN_SHARD * p, wid)]
            o32_ref[p, :, pl.ds(0, wid)] = s
            o16_ref[p, :, pl.ds(0, wid)] = s.astype(BF16)
        for grp in range(2):
            src = pl.ds(grp * LANES, SSD_GROUP_HEADS)
            s = adt_ref[:, src] + bdt_ref[:, src]
            dst = pl.ds(last + grp * SSD_GROUP_HEADS, SSD_GROUP_HEADS)
            o32_ref[N_CHIPS - 1, :, dst] = s
            o16_ref[N_CHIPS - 1, :, dst] = s.astype(BF16)

    wide = pl.BlockSpec((tr, D_MAIN), lambda i: (i, 0))
    narrow = pl.BlockSpec((tr, 2 * LANES), lambda i: (i, 0))
    out = pl.BlockSpec((N_CHIPS, tr, IN_SHARD), lambda i: (0, i, 0))
    return pl.pallas_call(
        body, name=name, grid=(r // tr,), in_specs=[wide, narrow, wide, narrow], out_specs=[out, out],
        out_shape=[jax.ShapeDtypeStruct((N_CHIPS, r, IN_SHARD), F32), jax.ShapeDtypeStruct((N_CHIPS, r, IN_SHARD), BF16)],
        compiler_params=_cparams("parallel"),
    )(mine, mine_dt, theirs, theirs_dt)


_HBM = pl.BlockSpec(memory_space=pltpu.HBM)
_SEM = pl.BlockSpec(memory_space=pltpu.SEMAPHORE)


def _in_hbm(a):
    return pltpu.with_memory_space_constraint(a, pltpu.HBM)


def _split_plan(kind, srcs, lands, x, y, c):
    plan = []
    for src, land in zip(srcs, lands):
        if kind == "sibling":
            rows = _half_rows(src.shape[-2], 1 - c)
            plan.append((src.at[rows] if len(src.shape) == 2 else src.at[:, rows], land, (x, y, 1 - c)))
            continue
        if kind == "allgather":
            peers = [(x, y, 1 - c)] + [(px, py, pc) for px, py in _peer_chips(x, y) for pc in (c, 1 - c)]
            plan += [(src, land.at[4 * x + 2 * y + c], peer) for peer in peers]
            continue
        for k, (px, py) in enumerate(_peer_chips(x, y)):
            if kind == "scatter":
                plan.append((src.at[2 * px + py], land.at[k], (px, py, c)))
            else:
                plan.append((src, land.at[2 * x + y], (px, py, c)))
    return plan


def _split_start(kind, srcs, land_shapes, name):
    n = len(srcs)

    def body(*refs):
        ins, lands = refs[:n], refs[n:2 * n]
        send_sems, recv_sems = refs[2 * n], refs[2 * n + 1]
        token = refs[-1]
        x, y, c = _mesh_pos()
        for i, (src, dst, to) in enumerate(_split_plan(kind, ins, lands, x, y, c)):
            pltpu.make_async_remote_copy(src_ref=src, dst_ref=dst, send_sem=send_sems.at[i], recv_sem=recv_sems.at[i],
                                         device_id=to, device_id_type=MESH).start()
        token[...] = jnp.zeros_like(token)

    zones = [lax.empty(s.shape, s.dtype) for s in land_shapes]
    n_sems = {"sibling": 1, "allgather": 7}.get(kind, 3) * n
    res = pl.pallas_call(
        body, name=name,
        out_shape=(pltpu.SemaphoreType.DMA((n_sems,)), pltpu.SemaphoreType.DMA((n_sems,)),
                   *[pltpu.HBM(a.shape, a.dtype) for a in srcs], *[pltpu.HBM(s.shape, s.dtype) for s in land_shapes],
                   jax.ShapeDtypeStruct((8, LANES), F32)),
        in_specs=[_HBM] * (2 * n), out_specs=(_SEM, _SEM, *[_HBM] * (2 * n), pl.BlockSpec(memory_space=pltpu.VMEM)),
        input_output_aliases={i: 2 + i for i in range(2 * n)},
        compiler_params=pltpu.CompilerParams(has_side_effects=pltpu.SideEffectType.DATAFLOW_SIDE_EFFECTING),
    )(*[_in_hbm(a) for a in srcs], *[_in_hbm(z) for z in zones])
    return dict(send=res[0], recv=res[1], srcs=list(res[2:2 + n]), lands=list(res[2 + n:2 + 2 * n]), token=res[-1], kind=kind)


def _split_wait(started, after, name):
    n = len(started["srcs"])
    kind = started["kind"]

    def body(*refs):
        ins, lands = refs[:n], refs[n:2 * n]
        send_sems, recv_sems = refs[2 * n], refs[2 * n + 1]
        x, y, c = _mesh_pos()
        for i, (src, dst, _) in enumerate(_split_plan(kind, ins, lands, x, y, c)):
            cp = pltpu.make_async_remote_copy(src_ref=src, dst_ref=dst, send_sem=send_sems.at[i], recv_sem=recv_sems.at[i],
                                              device_id=(x, y, c), device_id_type=MESH)
            cp.wait_send()
            cp.wait_recv()

    arrs = started["srcs"] + started["lands"]
    res = pl.pallas_call(
        body, name=name, out_shape=tuple(pltpu.HBM(a.shape, a.dtype) for a in arrs),
        in_specs=[_HBM] * (2 * n) + [_SEM, _SEM, pl.BlockSpec(memory_space=pl.ANY)], out_specs=tuple([_HBM] * (2 * n)),
        input_output_aliases={i: i for i in range(2 * n)},
        compiler_params=pltpu.CompilerParams(has_side_effects=pltpu.SideEffectType.DATAFLOW_SIDE_EFFECTING),
    )(*arrs, started["send"], started["recv"], after)
    return list(res[:n]), list(res[n:])


def _sibling_share(fs, name):
    n = len(fs)

    def body(*refs):
        ins, outs = refs[:n], refs[n:2 * n]
        send_sems, recv_sems = refs[2 * n:]
        x, y, c = _mesh_pos()
        copies = [_remote(ins[i], outs[i], send_sems, recv_sems, i, (x, y, 1 - c)) for i in range(n)]
        for cp in copies:
            cp.start()
        for cp in copies:
            cp.wait()

    out_shapes = [jax.ShapeDtypeStruct(a.shape, a.dtype) for a in fs]
    return _comm_call(body, list(fs), out_shapes, n, name)


def _allgather8(v, name, after=()):
    m = v.shape[0]

    def body(v_ref, *rest):
        out_ref, send_sems, recv_sems = rest[len(after):]
        x, y, c = _mesh_pos()
        me, sib = (x, y, c), (x, y, 1 - c)
        chips = _peer_chips(x, y)
        rem = functools.partial(_remote, send_sems=send_sems, recv_sems=recv_sems)

        def blk(px, py, pc):
            return out_ref.at[4 * px + 2 * py + pc]

        first = [rem(v_ref, blk(*me), sem=0, to=sib)]
        first += [rem(v_ref, blk(*me), sem=1 + k, to=(px, py, c)) for k, (px, py) in enumerate(chips)]
        for cp in first:
            cp.start()
        passed = []
        for k, (px, py) in enumerate(chips):
            landed = blk(px, py, c)
            rem(landed, landed, sem=1 + k, to=me).wait_recv()
            fwd = rem(landed, landed, sem=4 + k, to=sib)
            fwd.start()
            passed.append(fwd)
        rem(blk(*sib), blk(*sib), sem=0, to=me).wait_recv()
        for k, (px, py) in enumerate(chips):
            theirs = blk(px, py, 1 - c)
            rem(theirs, theirs, sem=4 + k, to=me).wait_recv()
        for cp in first + passed:
            cp.wait_send()

    return _comm_call(body, [v, *after], [jax.ShapeDtypeStruct((8, m, LANES), v.dtype)], 7, name)[0]


_WEIGHTS = ["norm1_g", "w_in", "conv_a_w", "conv_a_b", "ln_a_g", "ln_a_b", "ln_b_g", "ln_b_b", "w_spatial", "b_spatial",
            "conv_c_w", "conv_c_b", "dt_bias", "a_log", "d_skip", "norm_c_g", "w_out", "norm2_g", "w_ff1", "w_ff2", "final_g"]
_BIG = ["w_in", "w_out", "w_ff1", "w_ff2"]
_CONV_SHARDED = ["conv_a_w", "conv_c_w"]
_SMALL = [w for w in _WEIGHTS if w not in _BIG and w != "final_g"]
_PACK_ROWS = 512


def _pack(arrs):
    flat = jnp.concatenate([a.reshape(-1) for a in arrs])
    blk = _PACK_ROWS * LANES
    n = flat.shape[0]
    return jnp.pad(flat, (0, -(-n // blk) * blk - n)).reshape(-1, LANES)


def _unpack(packed, shapes):
    flat = packed.reshape(-1)
    out, off = [], 0
    for s in shapes:
        n = math.prod(s)
        out.append(flat[off:off + n].reshape(s))
        off += n
    return out


def _chips_to_cols(a):
    return a.transpose(1, 0, 2).reshape(a.shape[1], -1)


def _own_shards(w, li):
    return [w[k][li].astype(BF16) for k in _BIG] + [w[k][li] for k in _CONV_SHARDED]


def _assemble_w_in(gathered, own, name):
    k = own.shape[0]
    tr = _pick(k, (256, 128, 64, 32, 16))
    last = D_MAIN - (N_CHIPS - 1) * IN_SHARD

    def body(g_ref, own_ref, main_ref, dt_ref):
        x, y, _ = _mesh_pos()
        q = 2 * x + y
        dt_ref[...] = jnp.zeros_like(dt_ref)

        def place(read):
            for p in range(N_CHIPS):
                def _(p=p):
                    wid = IN_SHARD if p < N_CHIPS - 1 else last
                    main_ref[:, pl.ds(IN_SHARD * p, wid)] = read(p, pl.ds(0, wid))
                    if p == N_CHIPS - 1:
                        for grp in range(2):
                            dt_ref[:, pl.ds(grp * LANES, SSD_GROUP_HEADS)] = read(
                                p, pl.ds(last + grp * SSD_GROUP_HEADS, SSD_GROUP_HEADS))
                yield p, _

        for p, put in place(lambda p, cols: own_ref[:, cols]):
            pl.when(q == p)(put)
        for p, put in place(lambda p, cols: g_ref[p, :, cols]):
            pl.when(q != p)(put)

    return pl.pallas_call(
        body, name=name, grid=(k // tr,),
        in_specs=[pl.BlockSpec((N_CHIPS, tr, IN_SHARD), lambda i: (0, i, 0)), pl.BlockSpec((tr, IN_SHARD), lambda i: (i, 0))],
        out_specs=[pl.BlockSpec((tr, D_MAIN), lambda i: (i, 0)), pl.BlockSpec((tr, 2 * LANES), lambda i: (i, 0))],
        out_shape=[jax.ShapeDtypeStruct((k, D_MAIN), own.dtype), jax.ShapeDtypeStruct((k, 2 * LANES), own.dtype)],
        compiler_params=_cparams("parallel"),
    )(gathered, own)


def _layer_params(w, li, own, gathered, q):
    g_out, g_ff1, g_ff2, g_ca, g_cc = [lax.dynamic_update_index_in_dim(g, o, q, axis=0)
                                       for g, o in zip(gathered[1:], own[1:])]
    p = {k: w[k][li] for k in _SMALL if k not in _CONV_SHARDED}
    p["w_main"], p["w_dt"] = _assemble_w_in(gathered[0], own[0], f"l{li}_w_in")
    p["w_out"] = g_out.reshape(D_MIX, D_MODEL)
    p["w_ff1"] = g_ff1
    p["w_ff2"] = g_ff2.reshape(D_FF, D_MODEL)
    p["conv_a_w"] = _chips_to_cols(g_ca)
    p["conv_c_w"] = _chips_to_cols(g_cc)
    return p


def _ffn_out_grads(g):
    return [g["w_out"].reshape(N_CHIPS, -1, D_MODEL), g["w_ff1"], g["w_ff2"].reshape(N_CHIPS, -1, D_MODEL)]


def _half_shape(a):
    return jax.ShapeDtypeStruct(a.shape[:-2] + (a.shape[-2] // 2, a.shape[-1]), a.dtype)


def _chip_sums(g, early, early_from_sib, li, c, q):
    n = f"l{li}_rs_"
    late = [g["w_main"], g["w_dt"]]
    full = late + list(early)
    from_sib = list(_sibling_other_halves(late, n + "sib")) + list(early_from_sib)
    mine = [lax.dynamic_slice_in_dim(a, c * b.shape[-2], b.shape[-2], axis=a.ndim - 2) for a, b in zip(full, from_sib)]
    sums = [_chipsum_in(mine[0], mine[1], from_sib[0], from_sib[1], n + "chipsum0")]
    for i in range(2, len(full)):
        shape = from_sib[i].shape
        s32, s16 = _ew(lambda u, v: (u + v, u + v), [mine[i].reshape(-1, shape[-1]), from_sib[i].reshape(-1, shape[-1])],
                       (F32, BF16), n + f"chipsum{i - 1}")
        sums.append((s32.reshape(shape), s16.reshape(shape)))
    chip_f32 = [lax.dynamic_index_in_dim(s32, q, axis=0, keepdims=False) for s32, _ in sums]
    chip_bf16 = [s16 for _, s16 in sums]
    return chip_f32, chip_bf16


def _finish_reduce(chip_f32, from_chips, li, c):
    n = f"l{li}_rs_"
    halves = [_ew(lambda o, r0, r1, r2_: (((o + r0) + r1) + r2_,), [own, rb, rb, rb], (F32,), n + f"final{i}",
                  leads=[None, 0, 1, 2])[0] for i, (own, rb) in enumerate(zip(chip_f32, from_chips))]
    from_sib = _sibling_share(halves, n + "share")
    return [jnp.where(c == 0, jnp.concatenate([h, s], axis=0), jnp.concatenate([s, h], axis=0))
            for h, s in zip(halves, from_sib)]


def kernel(x, norm1_g, w_in, conv_a_w, conv_a_b, ln_a_g, ln_a_b, ln_b_g, ln_b_b, w_spatial, b_spatial, conv_c_w, conv_c_b, dt_bias, a_log, d_skip, norm_c_g, w_out, norm2_g, w_ff1, w_ff2, final_g, loss_target, m_norm1_g, m_w_in, m_conv_a_w, m_conv_a_b, m_ln_a_g, m_ln_a_b, m_ln_b_g, m_ln_b_b, m_w_spatial, m_b_spatial, m_conv_c_w, m_conv_c_b, m_dt_bias, m_a_log, m_d_skip, m_norm_c_g, m_w_out, m_norm2_g, m_w_ff1, m_w_ff2, m_final_g, v_norm1_g, v_w_in, v_conv_a_w, v_conv_a_b, v_ln_a_g, v_ln_a_b, v_ln_b_g, v_ln_b_b, v_w_spatial, v_b_spatial, v_conv_c_w, v_conv_c_b, v_dt_bias, v_a_log, v_d_skip, v_norm_c_g, v_w_out, v_norm2_g, v_w_ff1, v_w_ff2, v_final_g):
    given = dict(locals())
    w = {k: given[k] for k in _WEIGHTS}
    m = {k: given["m_" + k] for k in _WEIGHTS}
    v = {k: given["v_" + k] for k in _WEIGHTS}
    depth = w_in.shape[0]
    nseq, seq, d = x.shape
    xi, yi, ci = _mesh_pos()
    q = 2 * xi + yi

    own = [_own_shards(w, li) for li in range(depth)]
    nb = len(_BIG)
    gathered = _gather_weights(own[0][:nb], own[0][nb:], "l0_gather")
    h = x.reshape(nseq * seq, d)
    layer_params, saved = [], []
    for li in range(depth):
        nxt = None
        if li + 1 < depth:
            srcs, _ = lax.optimization_barrier((own[li + 1], gathered))
            zones = [jax.ShapeDtypeStruct((N_CHIPS,) + a.shape, a.dtype) for a in srcs]
            nxt = _split_start("gather", srcs, zones, f"l{li + 1}_gather_start")
        layer_params.append(_layer_params(w, li, own[li], gathered, q))
        h, s = _layer_fwd(h, layer_params[li], seq, li, after=() if nxt is None else (nxt["token"],))
        saved.append(s)
        if nxt is not None:
            own[li + 1], gathered = _split_wait(nxt, h, f"l{li + 1}_gather_wait")
    loss, dx, dx_16, d_final = _loss_head(h, final_g, loss_target.reshape(nseq * seq, d))

    grads = [None] * depth
    big_grads = [None] * depth
    pending = None
    for li in reversed(range(depth)):
        swaps = []

        def early_swap(g, li=li, swaps=swaps):
            early = _ffn_out_grads(g)
            swaps.append(_split_start("sibling", early, [_half_shape(a) for a in early], f"l{li}_rs_sib_start"))
            return (swaps[0]["token"],)

        after = () if pending is None else (pending[1]["token"],)
        if li == 0 and depth > 1:
            early_pack = _pack([grads[lj][k] for lj in range(1, depth) for k in _SMALL])
            early_small = _split_start("allgather", [early_pack], [jax.ShapeDtypeStruct((8,) + early_pack.shape, F32)],
                                       "small_early_start")
            after = after + (early_small["token"],)
        dx, dx_16, grads[li] = _layer_bwd(dx, dx_16, layer_params[li], saved[li], seq, li, after=after,
                                          on_ffn_grads=early_swap)
        if pending is not None:
            lj, scatter, chip_f32 = pending
            big_grads[lj] = _finish_reduce(chip_f32, _split_wait(scatter, dx, f"l{lj}_rs_scatter_wait")[1], lj, ci)
        early, early_from_sib = _split_wait(swaps[0], dx, f"l{li}_rs_sib_wait")
        chip_f32, chip_bf16 = _chip_sums(grads[li], early, early_from_sib, li, ci, q)
        lands = [jax.ShapeDtypeStruct((3,) + a.shape[1:], a.dtype) for a in chip_bf16]
        pending = (li, _split_start("scatter", chip_bf16, lands, f"l{li}_rs_scatter_start"), chip_f32)
    grad_out, delta_out, m_out, v_out = {}, {}, {}, {}

    small_shapes = [grads[0][k].shape for k in _SMALL]
    me = 2 * q + ci

    def sum8(*blocks):
        acc = blocks[0]
        for b in blocks[1:]:
            acc = acc + b
        return (acc,)

    def total_of(gathered, own, name):
        full = lax.dynamic_update_index_in_dim(gathered, own, me, axis=0)
        return _ew(sum8, [full] * 8, (F32,), name, leads=list(range(8)))[0]

    last_pack = _pack([grads[0][k] for k in _SMALL] + [d_final, loss.reshape(1)])
    last_total = total_of(_allgather8(last_pack, "small_allgather", after=(pending[1]["token"],)), last_pack, "small_sum")
    summed = _unpack(last_total, small_shapes + [d_final.shape, (1,)])
    tail = summed[len(_SMALL):]
    summed = summed[:len(_SMALL)]
    if depth > 1:
        (early_own,), (early_all,) = _split_wait(early_small, dx, "small_early_wait")
        summed += _unpack(total_of(early_all, early_own, "small_early_sum"), small_shapes * (depth - 1))
    summed += tail
    loss_total = summed[-1][0]
    small_grads = {k: jnp.stack([summed[li * len(_SMALL) + i] for li in range(depth)]) for i, k in enumerate(_SMALL)}
    small_grads["final_g"] = summed[-2]
    for k in _CONV_SHARDED:
        n_shard = w[k].shape[-1]
        small_grads[k] = lax.dynamic_slice_in_dim(small_grads[k], q * n_shard, n_shard, axis=2)
    names = _SMALL + ["final_g"]
    shapes = [w[k].shape for k in names]
    packed = [_pack([src[k] for k in names]) for src in (w, small_grads, m, v)]
    outs = _ew(_adam_fn, packed, (F32, F32, F32), "adam_small")
    for dst, o in zip((delta_out, m_out, v_out), outs):
        for k, a in zip(names, _unpack(o, shapes)):
            dst[k] = a
    for k in names:
        grad_out[k] = small_grads[k]

    lj, scatter, chip_f32 = pending
    big_grads[lj] = _finish_reduce(chip_f32, _split_wait(scatter, outs[0], f"l{lj}_rs_scatter_wait")[1], lj, ci)
    for i, k in enumerate(_BIG):
        grad_out[k] = jnp.stack([big_grads[li][i] for li in range(depth)])
        delta_out[k], m_out[k], v_out[k] = _adam(w[k], grad_out[k], m[k], v[k], "adam_" + k)

    return (loss_total, dx.reshape(nseq, seq, d), *[grad_out[k] for k in _WEIGHTS], *[delta_out[k] for k in _WEIGHTS],
            *[m_out[k] for k in _WEIGHTS], *[v_out[k] for k in _WEIGHTS])
```
